```python
import math
import jax, jax.numpy as jnp
from jax import lax
import numpy as np

D_MODEL = 1024
BATCH = 8
SEQ = 4096
DEPTH = 1

GDN_HEADS = 4
GDN_HEAD_DIM = 128
GDN_WIDTH = GDN_HEADS * GDN_HEAD_DIM
CONV_WIDTH = 4
CHUNK = 64
SWA_HEADS = 8
SWA_HEAD_DIM = 64
SWA_WIDTH = SWA_HEADS * SWA_HEAD_DIM
DILATED_PATTERNS = ((128, 1), (512, 4), (2048, 16))
NUM_BUCKETS = 32
MAX_DISTANCE = 2048
MIX_WIDTH = GDN_WIDTH + SWA_WIDTH
IN_SIZES = (GDN_WIDTH, GDN_WIDTH, GDN_WIDTH, GDN_WIDTH, GDN_HEADS, GDN_HEADS,
            SWA_WIDTH, SWA_WIDTH, SWA_WIDTH)
IN_COLS = sum(IN_SIZES)
D_FF = ((-(-8 * D_MODEL // 3)) + 255) // 256 * 256
RMS_EPS = 1e-6

kernel_name = 'hybrid_gdn_dilated_swa_block'


def rmsnorm(x, g):
    xf = x.astype(jnp.float32)
    y = xf * lax.rsqrt(jnp.mean(xf * xf, axis=-1, keepdims=True) + RMS_EPS)
    return (y * g.astype(jnp.float32)).astype(x.dtype)


def l2norm(x):
    return x * lax.rsqrt(jnp.sum(x * x, axis=-1, keepdims=True) + 1e-6)


def causal_depthwise_conv(x, w):
    K, C = w.shape
    return lax.conv_general_dilated(x, w[:, None, :].astype(x.dtype), (1,), ((K - 1, 0),),
                                    dimension_numbers=('NWC', 'WIO', 'NWC'),
                                    feature_group_count=C)


def chunk_gated_delta_rule(q, k, v, g, beta):
    Bn, S, H, Dk = q.shape
    Dv = v.shape[-1]
    N = S // CHUNK

    def chunks(t):
        t = t.reshape((Bn, N, CHUNK, H) + t.shape[3:])
        return jnp.moveaxis(jnp.moveaxis(t, 1, 0), 3, 2)

    qc, kc, vc = chunks(q), chunks(k), chunks(v)
    bc = chunks(beta)
    gc = jnp.cumsum(chunks(g), axis=-1)
    idx = jnp.arange(CHUNK)
    causal = idx[:, None] >= idx[None, :]
    strict = idx[:, None] > idx[None, :]
    decay = jnp.exp(jnp.where(causal, gc[..., :, None] - gc[..., None, :], -jnp.inf))
    kb = kc * bc[..., None]
    a = jnp.where(strict, jnp.einsum('nbhid,nbhjd->nbhij', kb, kc) * decay, 0.0)
    eye = jnp.eye(CHUNK, dtype=q.dtype)
    t_inv = lax.linalg.triangular_solve(eye + a, jnp.broadcast_to(eye, a.shape),
                                        left_side=True, lower=True)
    u = t_inv @ (vc * bc[..., None])
    w = t_inv @ (kb * jnp.exp(gc)[..., None])
    attn = jnp.einsum('nbhid,nbhjd->nbhij', qc, kc) * decay
    q_dec = qc * jnp.exp(gc)[..., None]
    k_dec = kc * jnp.exp(gc[..., -1:] - gc)[..., None]
    g_end = jnp.exp(gc[..., -1])

    def step(state, inp):
        u_i, w_i, attn_i, qd_i, kd_i, ge_i = inp
        v_new = u_i - jnp.einsum('bhck,bhkv->bhcv', w_i, state)
        o_i = (jnp.einsum('bhck,bhkv->bhcv', qd_i, state)
               + jnp.einsum('bhij,bhjv->bhiv', attn_i, v_new))
        state = state * ge_i[..., None, None] + jnp.einsum('bhck,bhcv->bhkv', kd_i, v_new)
        return state, o_i

    s0 = jnp.zeros((Bn, H, Dk, Dv), q.dtype)
    _, o = lax.scan(step, s0, (u, w, attn, q_dec, k_dec, g_end))
    return jnp.moveaxis(o, 0, 1).transpose(0, 1, 3, 2, 4).reshape(Bn, S, H, Dv)


def gated_deltanet(q, k, v, gate, beta_logit, a_logit, conv_w, a_log, dt_bias, onorm_g):
    f32 = jnp.float32
    Bn, S, _ = q.shape
    qkv = jax.nn.silu(causal_depthwise_conv(jnp.concatenate([q, k, v], axis=-1), conv_w)).astype(f32)
    q, k, v = jnp.split(qkv, 3, axis=-1)
    heads = lambda t: t.reshape(Bn, S, GDN_HEADS, GDN_HEAD_DIM)
    q = l2norm(heads(q)) * GDN_HEAD_DIM ** -0.5
    k = l2norm(heads(k))
    v = heads(v)
    beta = jax.nn.sigmoid(beta_logit.astype(f32))
    g = -jnp.exp(a_log.astype(f32)) * jax.nn.softplus(a_logit.astype(f32) + dt_bias.astype(f32))
    o = chunk_gated_delta_rule(q, k, v, g, beta)
    o = o * lax.rsqrt(jnp.mean(o * o, axis=-1, keepdims=True) + RMS_EPS) * onorm_g.astype(f32)
    o = o * jax.nn.silu(heads(gate).astype(f32))
    return o.reshape(Bn, S, GDN_WIDTH).astype(gate.dtype)


def t5_causal_bucket(dist):
    max_exact = NUM_BUCKETS // 2
    d = jnp.maximum(dist, 1).astype(jnp.float32)
    log_b = max_exact + (jnp.log(d / max_exact) / math.log(MAX_DISTANCE / max_exact)
                         * (NUM_BUCKETS - max_exact)).astype(jnp.int32)
    return jnp.where(dist < max_exact, dist, jnp.minimum(log_b, NUM_BUCKETS - 1))


def dilated_band(q, k, v, rel_bias, window, dilation):
    Bn, S, H, Dh = q.shape
    w = window // dilation
    L = S // dilation
    nb = -(-L // w)
    Lp = nb * w

    def to_sub(t):
        t = t.reshape(Bn, L, dilation, H, Dh).transpose(0, 3, 2, 1, 4)
        return jnp.pad(t, ((0, 0), (0, 0), (0, 0), (0, Lp - L), (0, 0)))

    blocks = lambda t: t.reshape(Bn, H, dilation, nb, w, Dh)

    def band(t):
        prev = jnp.pad(t, ((0, 0), (0, 0), (0, 0), (w, 0), (0, 0)))[:, :, :, :Lp]
        return jnp.concatenate([blocks(prev), blocks(t)], axis=-2)

    qb, kb, vb = blocks(to_sub(q)), band(to_sub(k)), band(to_sub(v))
    qi = jnp.arange(w)[:, None]
    kj = jnp.arange(2 * w)[None, :]
    rel = qi + w - kj
    blk = jnp.arange(nb)[:, None, None]
    valid = (rel >= 0) & (rel <= w) & (blk * w + kj - w >= 0)
    bias_steps = rel_bias[t5_causal_bucket(jnp.arange(w + 1) * dilation)]
    bias = jnp.moveaxis(bias_steps[jnp.clip(rel, 0, w)], -1, 0).astype(jnp.float32)
    logits = jnp.einsum('bhrnqd,bhrnkd->bhrnqk', qb, kb) * Dh ** -0.5 + bias[None, :, None, None]
    logits = jnp.where(valid, logits, -jnp.inf)
    m = jnp.max(logits, axis=-1)
    p = jnp.exp(logits - m[..., None])
    s = jnp.sum(p, axis=-1)
    num = jnp.einsum('bhrnqk,bhrnkd->bhrnqd', p, vb)

    def from_sub(t):
        rest = t.shape[5:]
        t = t.reshape((Bn, H, dilation, Lp) + rest)[:, :, :, :L]
        t = t.transpose((0, 3, 2, 1) + tuple(range(4, t.ndim)))
        return t.reshape((Bn, S, H) + rest)

    return from_sub(num), from_sub(m), from_sub(s)


def dilated_attention(q, k, v, rel_bias):
    Bn, S, _ = q.shape
    heads = lambda t: t.reshape(Bn, S, SWA_HEADS, SWA_HEAD_DIM).astype(jnp.float32)
    qh, kh, vh = heads(q), heads(k), heads(v)
    parts = [dilated_band(qh, kh, vh, rel_bias, win, dil) for win, dil in DILATED_PATTERNS]
    nums = jnp.stack([pt[0] for pt in parts])
    ms = jnp.stack([pt[1] for pt in parts])
    ss = jnp.stack([pt[2] for pt in parts])
    wts = jnp.exp(ms - jnp.max(ms, axis=0, keepdims=True))
    out = jnp.sum(wts[..., None] * nums, axis=0) / jnp.sum(wts * ss, axis=0)[..., None]
    return out.reshape(Bn, S, SWA_WIDTH).astype(q.dtype)


def _fwd_setup_inputs(seed: int = 0) -> dict:
    key = jax.random.key(seed)
    ks = jax.random.split(key, 16)
    f32 = jnp.float32
    nrm = lambda k_, shape, scale: jax.random.normal(k_, shape, f32) * scale
    dt = jnp.exp(jax.random.uniform(ks[4], (DEPTH, GDN_HEADS), f32, math.log(1e-3), math.log(1e-1)))
    return {
        'x': nrm(ks[0], (BATCH, SEQ, D_MODEL), 1.0),
        'w_in': nrm(ks[1], (DEPTH, D_MODEL, IN_COLS), D_MODEL ** -0.5),
        'conv_w': nrm(ks[2], (DEPTH, CONV_WIDTH, 3 * GDN_WIDTH), CONV_WIDTH ** -0.5),
        'a_log': jnp.log(jax.random.uniform(ks[3], (DEPTH, GDN_HEADS), f32, 1.0, 16.0)),
        'dt_bias': dt + jnp.log(-jnp.expm1(-dt)),
        'onorm_g': 1.0 + nrm(ks[5], (DEPTH, GDN_HEAD_DIM), 0.1),
        'rel_bias': nrm(ks[6], (NUM_BUCKETS, SWA_HEADS), 0.2),
        'w_out': nrm(ks[7], (DEPTH, MIX_WIDTH, D_MODEL), MIX_WIDTH ** -0.5),
        'g_mix_pre': 1.0 + nrm(ks[8], (DEPTH, D_MODEL), 0.1),
        'g_mix_post': 1.0 + nrm(ks[9], (DEPTH, D_MODEL), 0.1),
        'w_gate': nrm(ks[10], (DEPTH, D_MODEL, D_FF), D_MODEL ** -0.5),
        'w_up': nrm(ks[11], (DEPTH, D_MODEL, D_FF), D_MODEL ** -0.5),
        'w_down': nrm(ks[12], (DEPTH, D_FF, D_MODEL), D_FF ** -0.5),
        'g_ffn_pre': 1.0 + nrm(ks[13], (DEPTH, D_MODEL), 0.1),
        'g_ffn_post': 1.0 + nrm(ks[14], (DEPTH, D_MODEL), 0.1),
    }


def _fwd_reference(x, w_in, conv_w, a_log, dt_bias, onorm_g, rel_bias, w_out, g_mix_pre, g_mix_post,
              w_gate, w_up, w_down, g_ffn_pre, g_ffn_post):
    split_at = [int(c) for c in np.cumsum(IN_SIZES)[:-1]]
    for l in range(DEPTH):
        h = rmsnorm(x, g_mix_pre[l])
        proj = h @ w_in[l]
        qa, ka, va, gate_a, beta_a, alpha_a, qb, kb, vb = jnp.split(proj, split_at, axis=-1)
        out_a = gated_deltanet(qa, ka, va, gate_a, beta_a, alpha_a,
                               conv_w[l], a_log[l], dt_bias[l], onorm_g[l])
        out_b = dilated_attention(qb, kb, vb, rel_bias)
        mix = jnp.concatenate([out_a, out_b], axis=-1) @ w_out[l]
        x = x + rmsnorm(mix, g_mix_post[l])
        h = rmsnorm(x, g_ffn_pre[l])
        f = (jax.nn.silu(h @ w_gate[l]) * (h @ w_up[l])) @ w_down[l]
        x = x + rmsnorm(f, g_ffn_post[l])
    return x


import jax as _jax
import jax.numpy as _jnp

TWIN_FORMAT = 'train_step'
FWD_PARAMS = ['x', 'w_in', 'conv_w', 'a_log', 'dt_bias', 'onorm_g', 'rel_bias', 'w_out', 'g_mix_pre', 'g_mix_post', 'w_gate', 'w_up', 'w_down', 'g_ffn_pre', 'g_ffn_post']
TWIN_WEIGHTS = ['w_in', 'conv_w', 'a_log', 'dt_bias', 'onorm_g', 'rel_bias', 'w_out', 'g_mix_pre', 'g_mix_post', 'w_gate', 'w_up', 'w_down', 'g_ffn_pre', 'g_ffn_post']
TWIN_DIFF_INPUT = 'x'
TWIN_INPUTS = ['x', 'w_in', 'conv_w', 'a_log', 'dt_bias', 'onorm_g', 'rel_bias', 'w_out', 'g_mix_pre', 'g_mix_post', 'w_gate', 'w_up', 'w_down', 'g_ffn_pre', 'g_ffn_post', 'loss_target', 'm_w_in', 'm_conv_w', 'm_a_log', 'm_dt_bias', 'm_onorm_g', 'm_rel_bias', 'm_w_out', 'm_g_mix_pre', 'm_g_mix_post', 'm_w_gate', 'm_w_up', 'm_w_down', 'm_g_ffn_pre', 'm_g_ffn_post', 'v_w_in', 'v_conv_w', 'v_a_log', 'v_dt_bias', 'v_onorm_g', 'v_rel_bias', 'v_w_out', 'v_g_mix_pre', 'v_g_mix_post', 'v_w_gate', 'v_w_up', 'v_w_down', 'v_g_ffn_pre', 'v_g_ffn_post']
TWIN_OUTPUTS = ['loss', 'grad_x', 'grad_w_in', 'grad_conv_w', 'grad_a_log', 'grad_dt_bias', 'grad_onorm_g', 'grad_rel_bias', 'grad_w_out', 'grad_g_mix_pre', 'grad_g_mix_post', 'grad_w_gate', 'grad_w_up', 'grad_w_down', 'grad_g_ffn_pre', 'grad_g_ffn_post', 'delta_w_in', 'delta_conv_w', 'delta_a_log', 'delta_dt_bias', 'delta_onorm_g', 'delta_rel_bias', 'delta_w_out', 'delta_g_mix_pre', 'delta_g_mix_post', 'delta_w_gate', 'delta_w_up', 'delta_w_down', 'delta_g_ffn_pre', 'delta_g_ffn_post', 'new_m_w_in', 'new_m_conv_w', 'new_m_a_log', 'new_m_dt_bias', 'new_m_onorm_g', 'new_m_rel_bias', 'new_m_w_out', 'new_m_g_mix_pre', 'new_m_g_mix_post', 'new_m_w_gate', 'new_m_w_up', 'new_m_w_down', 'new_m_g_ffn_pre', 'new_m_g_ffn_post', 'new_v_w_in', 'new_v_conv_w', 'new_v_a_log', 'new_v_dt_bias', 'new_v_onorm_g', 'new_v_rel_bias', 'new_v_w_out', 'new_v_g_mix_pre', 'new_v_g_mix_post', 'new_v_w_gate', 'new_v_w_up', 'new_v_w_down', 'new_v_g_ffn_pre', 'new_v_g_ffn_post']
TWIN_LEAF_KINDS = {'loss': 'loss', 'grad_x': 'grad_x', 'grad_w_in': 'grad_w', 'grad_conv_w': 'grad_w', 'grad_a_log': 'grad_w', 'grad_dt_bias': 'grad_w', 'grad_onorm_g': 'grad_w', 'grad_rel_bias': 'grad_w', 'grad_w_out': 'grad_w', 'grad_g_mix_pre': 'grad_w', 'grad_g_mix_post': 'grad_w', 'grad_w_gate': 'grad_w', 'grad_w_up': 'grad_w', 'grad_w_down': 'grad_w', 'grad_g_ffn_pre': 'grad_w', 'grad_g_ffn_post': 'grad_w', 'delta_w_in': 'delta_w', 'delta_conv_w': 'delta_w', 'delta_a_log': 'delta_w', 'delta_dt_bias': 'delta_w', 'delta_onorm_g': 'delta_w', 'delta_rel_bias': 'delta_w', 'delta_w_out': 'delta_w', 'delta_g_mix_pre': 'delta_w', 'delta_g_mix_post': 'delta_w', 'delta_w_gate': 'delta_w', 'delta_w_up': 'delta_w', 'delta_w_down': 'delta_w', 'delta_g_ffn_pre': 'delta_w', 'delta_g_ffn_post': 'delta_w', 'new_m_w_in': 'new_m', 'new_m_conv_w': 'new_m', 'new_m_a_log': 'new_m', 'new_m_dt_bias': 'new_m', 'new_m_onorm_g': 'new_m', 'new_m_rel_bias': 'new_m', 'new_m_w_out': 'new_m', 'new_m_g_mix_pre': 'new_m', 'new_m_g_mix_post': 'new_m', 'new_m_w_gate': 'new_m', 'new_m_w_up': 'new_m', 'new_m_w_down': 'new_m', 'new_m_g_ffn_pre': 'new_m', 'new_m_g_ffn_post': 'new_m', 'new_v_w_in': 'new_v', 'new_v_conv_w': 'new_v', 'new_v_a_log': 'new_v', 'new_v_dt_bias': 'new_v', 'new_v_onorm_g': 'new_v', 'new_v_rel_bias': 'new_v', 'new_v_w_out': 'new_v', 'new_v_g_mix_pre': 'new_v', 'new_v_g_mix_post': 'new_v', 'new_v_w_gate': 'new_v', 'new_v_w_up': 'new_v', 'new_v_w_down': 'new_v', 'new_v_g_ffn_pre': 'new_v', 'new_v_g_ffn_post': 'new_v'}


def _forward(args):
    return _fwd_reference(*[args[k] for k in FWD_PARAMS])


def _output_shape():
    out = _jax.eval_shape(lambda: _forward(_fwd_setup_inputs(0)))
    return out.shape, out.dtype

N_MICROBATCH = 1
ADAM_LR = 0.001
ADAM_B1 = 0.9
ADAM_B2 = 0.999
ADAM_EPS = 1e-08
ADAM_WD = 0.01
ADAM_STEP = 10
PER_EXAMPLE_BATCH_AXIS = {'x': 0, 'loss_target': 0}
SHARED_INPUTS = []
_WEIGHT_DTYPES = {'w_in': _jnp.float32, 'conv_w': _jnp.float32, 'a_log': _jnp.float32, 'dt_bias': _jnp.float32, 'onorm_g': _jnp.float32, 'rel_bias': _jnp.float32, 'w_out': _jnp.float32, 'g_mix_pre': _jnp.float32, 'g_mix_post': _jnp.float32, 'w_gate': _jnp.float32, 'w_up': _jnp.float32, 'w_down': _jnp.float32, 'g_ffn_pre': _jnp.float32, 'g_ffn_post': _jnp.float32}
MOMENT_SCALE = {'w_in': 4.399636e-01, 'conv_w': 6.366393e-01, 'a_log': 4.239621e+00, 'dt_bias': 4.156297e+00, 'onorm_g': 3.334735e+00, 'rel_bias': 2.689348e-01, 'w_out': 9.694688e-01, 'g_mix_pre': 8.712510e-01, 'g_mix_post': 3.232021e+01, 'w_gate': 2.685902e-01, 'w_up': 4.184042e-01, 'w_down': 6.982149e-01, 'g_ffn_pre': 7.825758e-01, 'g_ffn_post': 3.210586e+01}


def _to_microbatches(a, axis):
    t = _jnp.moveaxis(a, axis, 0)
    t = t.reshape((N_MICROBATCH, t.shape[0] // N_MICROBATCH) + t.shape[1:])
    return _jnp.moveaxis(t, 1, axis + 1)


def setup_inputs(seed: int = 0) -> dict:
    inp = _fwd_setup_inputs(seed)
    key = _jax.random.fold_in(_jax.random.key(seed), 7919)
    shape, _ = _output_shape()
    out = dict(inp)
    out["loss_target"] = _jax.random.normal(_jax.random.fold_in(key, 0), shape, _jnp.float32)
    for i, name in enumerate(TWIN_WEIGHTS):
        w = inp[name].astype(_jnp.float32)
        if MOMENT_SCALE is None:
            s = _jnp.sqrt(_jnp.mean(_jnp.square(w)) + 1e-30)
        else:
            s = MOMENT_SCALE[name]
        km, kv = _jax.random.split(_jax.random.fold_in(key, i + 1))
        out[name] = w
        out["m_" + name] = s * _jax.random.normal(km, w.shape, _jnp.float32)
        out["v_" + name] = (s * s) * _jax.random.uniform(kv, w.shape, _jnp.float32, 0.5, 1.5)
    if N_MICROBATCH > 1:
        for name, axis in PER_EXAMPLE_BATCH_AXIS.items():
            out[name] = _to_microbatches(out[name], axis)
    return {'x': out['x'], 'w_in': out['w_in'], 'conv_w': out['conv_w'], 'a_log': out['a_log'], 'dt_bias': out['dt_bias'], 'onorm_g': out['onorm_g'], 'rel_bias': out['rel_bias'], 'w_out': out['w_out'], 'g_mix_pre': out['g_mix_pre'], 'g_mix_post': out['g_mix_post'], 'w_gate': out['w_gate'], 'w_up': out['w_up'], 'w_down': out['w_down'], 'g_ffn_pre': out['g_ffn_pre'], 'g_ffn_post': out['g_ffn_post'], 'loss_target': out['loss_target'], 'm_w_in': out['m_w_in'], 'm_conv_w': out['m_conv_w'], 'm_a_log': out['m_a_log'], 'm_dt_bias': out['m_dt_bias'], 'm_onorm_g': out['m_onorm_g'], 'm_rel_bias': out['m_rel_bias'], 'm_w_out': out['m_w_out'], 'm_g_mix_pre': out['m_g_mix_pre'], 'm_g_mix_post': out['m_g_mix_post'], 'm_w_gate': out['m_w_gate'], 'm_w_up': out['m_w_up'], 'm_w_down': out['m_w_down'], 'm_g_ffn_pre': out['m_g_ffn_pre'], 'm_g_ffn_post': out['m_g_ffn_post'], 'v_w_in': out['v_w_in'], 'v_conv_w': out['v_conv_w'], 'v_a_log': out['v_a_log'], 'v_dt_bias': out['v_dt_bias'], 'v_onorm_g': out['v_onorm_g'], 'v_rel_bias': out['v_rel_bias'], 'v_w_out': out['v_w_out'], 'v_g_mix_pre': out['v_g_mix_pre'], 'v_g_mix_post': out['v_g_mix_post'], 'v_w_gate': out['v_w_gate'], 'v_w_up': out['v_w_up'], 'v_w_down': out['v_w_down'], 'v_g_ffn_pre': out['v_g_ffn_pre'], 'v_g_ffn_post': out['v_g_ffn_post']}


def _loss(weights, diff, rest, loss_target):
    with _jax.named_scope("forward"):
        args = {**rest, TWIN_DIFF_INPUT: diff, **{k: w.astype(_WEIGHT_DTYPES[k]) for k, w in weights.items()}}
        y = _forward(args)
    with _jax.named_scope("loss_head"):
        err = _jnp.square(y.astype(_jnp.float32) - loss_target)
        return 0.5 * _jnp.sum(_jnp.mean(err, axis=-1)) if err.ndim else 0.5 * err


def _adamw(w, g, m, v):
    m = ADAM_B1 * m + (1.0 - ADAM_B1) * g
    v = ADAM_B2 * v + (1.0 - ADAM_B2) * _jnp.square(g)
    m_hat = m / (1.0 - ADAM_B1 ** ADAM_STEP)
    v_hat = v / (1.0 - ADAM_B2 ** ADAM_STEP)
    delta = -ADAM_LR * (m_hat / (_jnp.sqrt(v_hat) + ADAM_EPS) + ADAM_WD * w)
    return delta, m, v


def reference(x, w_in, conv_w, a_log, dt_bias, onorm_g, rel_bias, w_out, g_mix_pre, g_mix_post, w_gate, w_up, w_down, g_ffn_pre, g_ffn_post, loss_target, m_w_in, m_conv_w, m_a_log, m_dt_bias, m_onorm_g, m_rel_bias, m_w_out, m_g_mix_pre, m_g_mix_post, m_w_gate, m_w_up, m_w_down, m_g_ffn_pre, m_g_ffn_post, v_w_in, v_conv_w, v_a_log, v_dt_bias, v_onorm_g, v_rel_bias, v_w_out, v_g_mix_pre, v_g_mix_post, v_w_gate, v_w_up, v_w_down, v_g_ffn_pre, v_g_ffn_post):
    given = dict(x=x, w_in=w_in, conv_w=conv_w, a_log=a_log, dt_bias=dt_bias, onorm_g=onorm_g, rel_bias=rel_bias, w_out=w_out, g_mix_pre=g_mix_pre, g_mix_post=g_mix_post, w_gate=w_gate, w_up=w_up, w_down=w_down, g_ffn_pre=g_ffn_pre, g_ffn_post=g_ffn_post, loss_target=loss_target, m_w_in=m_w_in, m_conv_w=m_conv_w, m_a_log=m_a_log, m_dt_bias=m_dt_bias, m_onorm_g=m_onorm_g, m_rel_bias=m_rel_bias, m_w_out=m_w_out, m_g_mix_pre=m_g_mix_pre, m_g_mix_post=m_g_mix_post, m_w_gate=m_w_gate, m_w_up=m_w_up, m_w_down=m_w_down, m_g_ffn_pre=m_g_ffn_pre, m_g_ffn_post=m_g_ffn_post, v_w_in=v_w_in, v_conv_w=v_conv_w, v_a_log=v_a_log, v_dt_bias=v_dt_bias, v_onorm_g=v_onorm_g, v_rel_bias=v_rel_bias, v_w_out=v_w_out, v_g_mix_pre=v_g_mix_pre, v_g_mix_post=v_g_mix_post, v_w_gate=v_w_gate, v_w_up=v_w_up, v_w_down=v_w_down, v_g_ffn_pre=v_g_ffn_pre, v_g_ffn_post=v_g_ffn_post)
    weights = {n: given[n] for n in TWIN_WEIGHTS}
    shared = {n: given[n] for n in SHARED_INPUTS}
    per_example = {n: given[n] for n in ['x']}
    grad_fn = _jax.value_and_grad(_loss, argnums=(0, 1))

    def one_microbatch(ex, loss_target):
        ex = dict(ex)
        diff = ex.pop(TWIN_DIFF_INPUT)
        return grad_fn(weights, diff, {**shared, **ex}, loss_target)

    if N_MICROBATCH == 1:
        loss, (grad_w, grad_x) = one_microbatch(per_example, given["loss_target"])
    else:
        def body(carry, xs):
            loss_sum, grad_sum = carry
            l_k, (gw_k, gx_k) = one_microbatch(xs[0], xs[1])
            with _jax.named_scope("update"):
                return (loss_sum + l_k, _jax.tree.map(_jnp.add, grad_sum, gw_k)), gx_k

        init = (_jnp.zeros((), _jnp.float32), _jax.tree.map(_jnp.zeros_like, weights))
        (loss, grad_w), grad_x = _jax.lax.scan(body, init, (per_example, given["loss_target"]))
    with _jax.named_scope("update"):
        delta_w, new_m, new_v = {}, {}, {}
        for n in TWIN_WEIGHTS:
            delta_w[n], new_m[n], new_v[n] = _adamw(weights[n], grad_w[n], given["m_" + n], given["v_" + n])
    return (loss, grad_x, *[grad_w[n] for n in TWIN_WEIGHTS], *[delta_w[n] for n in TWIN_WEIGHTS],
            *[new_m[n] for n in TWIN_WEIGHTS], *[new_v[n] for n in TWIN_WEIGHTS])
```

```python
import functools
import math

import numpy as np
import jax
import jax.numpy as jnp
from jax import lax
from jax.experimental import pallas as pl
from jax.experimental.pallas import tpu as pltpu

f32 = jnp.float32
bf16 = jnp.bfloat16
SDS = jax.ShapeDtypeStruct

D_MODEL = 1024
GDN_HEADS = 4
GDN_HD = 128
GDN_W = 512
CHUNK = 64
SWA_HEADS = 8
SWA_HD = 64
SWA_W = 512
D_FF = 2816
IN_COLS = 3592
PATTERNS = ((128, 1), (512, 4), (2048, 16))
SWA_BLK = 128
NUM_BUCKETS = 32
MAX_DISTANCE = 2048
RMS_EPS = 1e-6
NEG = -1e30
N_DEV = 8

COL_A = 0
COL_B = 2048
COL_G = 3584
NCOL = 3712
LANE = 128

ADAM_LR, ADAM_B1, ADAM_B2, ADAM_EPS, ADAM_WD, ADAM_STEP = 0.001, 0.9, 0.999, 1e-08, 0.01, 10

VMEM_LIMIT = 56 * 1024 * 1024

HI = lax.Precision.HIGHEST


def _cp(n_grid=0, vmem=None):
    kw = {}
    if n_grid:
        kw["dimension_semantics"] = ("arbitrary",) * n_grid
    if vmem:
        kw["vmem_limit_bytes"] = vmem
    return pltpu.CompilerParams(**kw)


def _dot(a, b):
    return jnp.dot(a, b, preferred_element_type=f32)


def _dot_nt(a, b):
    return lax.dot_general(a, b, (((1,), (1,)), ((), ())), preferred_element_type=f32)


def _dot_tn(a, b):
    return lax.dot_general(a, b, (((0,), (0,)), ((), ())), preferred_element_type=f32)


def _dot_hi(a, b):
    return jnp.dot(a, b, precision=HI, preferred_element_type=f32)


def _sigmoid(x):
    return 1.0 / (1.0 + jnp.exp(-x))


def _softplus(x):
    return jnp.maximum(x, 0.0) + jnp.log(1.0 + jnp.exp(-jnp.abs(x)))


def _const_spec(shape):
    nd = len(shape)
    return pl.BlockSpec(shape, lambda *_: (0,) * nd)


def _resident_spec(shape):
    nd = len(shape)
    return pl.BlockSpec(shape, lambda *_: (0,) * nd, pipeline_mode=pl.Buffered(1))


def _t5_bucket_np(dist):
    max_exact = NUM_BUCKETS // 2
    d = np.maximum(dist, 1).astype(np.float32)
    log_b = max_exact + (np.log(d / np.float32(max_exact)) / np.float32(math.log(MAX_DISTANCE / max_exact))
                         * np.float32(NUM_BUCKETS - max_exact)).astype(np.int32)
    return np.where(dist < max_exact, dist, np.minimum(log_b, NUM_BUCKETS - 1)).astype(np.int32)


def _bucket_tables():
    w = SWA_BLK
    qi = np.arange(w)[:, None]
    kj = np.arange(w)[None, :]
    out = np.zeros((len(PATTERNS), 2, w, w), np.int32)
    for p, (_, dil) in enumerate(PATTERNS):
        steps = _t5_bucket_np(np.arange(w + 1) * dil)
        rel_prev = qi + w - kj
        rel_cur = qi - kj
        out[p, 0] = np.where(rel_prev <= w, steps[np.clip(rel_prev, 0, w)], -1)
        out[p, 1] = np.where(rel_cur >= 0, steps[np.clip(rel_cur, 0, w)], -1)
    return out


def _bias_tables(rel_bias, bk):
    def body(rb_ref, bk_ref, o_ref):
        b_idx = bk_ref[0, 0]
        for h in range(SWA_HEADS):
            def lp(b, acc):
                return jnp.where(b_idx == b, rb_ref[b, h], acc)
            o_ref[0, h, 0] = lax.fori_loop(0, NUM_BUCKETS, lp, jnp.full((SWA_BLK, SWA_BLK), NEG, f32))

    return pl.pallas_call(
        body, name="bias_tables", grid=(3, 2),
        in_specs=[pl.BlockSpec(memory_space=pltpu.SMEM),
                  pl.BlockSpec((1, 1, SWA_BLK, SWA_BLK), lambda p, t: (p, t, 0, 0))],
        out_specs=pl.BlockSpec((1, SWA_HEADS, 1, SWA_BLK, SWA_BLK), lambda p, t: (p, 0, t, 0, 0)),
        out_shape=SDS((3, SWA_HEADS, 2, SWA_BLK, SWA_BLK), f32),
        compiler_params=_cp(2),
    )(rel_bias, bk)


def _rel_bias_grad(dsb, bk, bk_np):
    present = [[sorted(set(int(v) for v in np.unique(bk_np[p, t]) if v >= 0)) for t in range(2)] for p in range(3)]

    def body(ds_ref, bk_ref, o_ref):
        row = lax.broadcasted_iota(jnp.int32, (NUM_BUCKETS, LANE), 0)
        col = lax.broadcasted_iota(jnp.int32, (NUM_BUCKETS, SWA_HEADS), 1)
        out = jnp.zeros((NUM_BUCKETS, SWA_HEADS), f32)
        for hp in range(4):
            for hh in range(2):
                acc = jnp.zeros((NUM_BUCKETS, LANE), f32)
                for p in range(3):
                    for t in range(2):
                        tile = ds_ref[hp, p, hh, t]
                        b_idx = bk_ref[p, t]
                        for b in present[p][t]:
                            part = jnp.sum(jnp.where(b_idx == b, tile, 0.0), axis=0, keepdims=True)
                            acc = acc + jnp.where(row == b, part, 0.0)
                tot = jnp.sum(acc, axis=1, keepdims=True)
                out = out + jnp.where(col == 2 * hp + hh, tot, 0.0)
        o_ref[...] = out

    return pl.pallas_call(body, name="rel_bias_grad", out_shape=SDS((NUM_BUCKETS, SWA_HEADS), f32),
                          compiler_params=_cp(0, 32 * 1024 * 1024))(dsb, bk)


def _proj_fwd(x, g_pre, wcat, S):
    TS = 256

    def body(x_ref, g_ref, w_ref, o_ref):
        xv = x_ref[...]
        r = lax.rsqrt(jnp.mean(xv * xv, axis=-1, keepdims=True) + RMS_EPS)
        h = (xv * r * g_ref[...]).astype(bf16)
        o_ref[...] = _dot(h, w_ref[...])

    return pl.pallas_call(
        body, name="proj_fwd", grid=(S // TS,),
        in_specs=[pl.BlockSpec((TS, D_MODEL), lambda i: (i, 0)), _const_spec((1, D_MODEL)),
                  _resident_spec((D_MODEL, NCOL))],
        out_specs=pl.BlockSpec((TS, NCOL), lambda i: (i, 0)),
        out_shape=SDS((S, NCOL), f32),
        compiler_params=_cp(1, VMEM_LIMIT),
    )(x, g_pre, wcat)


CONV_RT = 256
HALO = 8


def _gdn_prep(proj, conv_w, S):
    def body(p_ref, cw_ref, o_ref, xs_ref):
        j = pl.program_id(0)
        xs_ref[pl.ds(0, HALO), :] = jnp.zeros((HALO, LANE), f32)
        xs_ref[pl.ds(HALO, S), :] = p_ref[...]
        w = cw_ref[...]
        is_qk = j < 2 * GDN_HEADS
        scale = jnp.where(j < GDN_HEADS, GDN_HD ** -0.5, 1.0).astype(f32)

        def lp(c, carry):
            st = pl.multiple_of(c * CONV_RT, CONV_RT)
            pre = xs_ref[pl.ds(st + HALO - 3, CONV_RT), :] * w[0:1, :]
            for i in range(1, 4):
                pre = pre + xs_ref[pl.ds(st + HALO - 3 + i, CONV_RT), :] * w[i:i + 1, :]
            s = pre * _sigmoid(pre)
            nrm = s * lax.rsqrt(jnp.sum(s * s, axis=-1, keepdims=True) + 1e-6) * scale
            o_ref[pl.ds(st, CONV_RT), :] = jnp.where(is_qk, nrm, s)
            return carry

        lax.fori_loop(0, S // CONV_RT, lp, 0)

    return pl.pallas_call(
        body, name="gdn_prep", grid=(12,),
        in_specs=[pl.BlockSpec((S, LANE), lambda j: (0, j)), pl.BlockSpec((4, LANE), lambda j: (0, j))],
        out_specs=pl.BlockSpec((S, LANE), lambda j: (0, j)),
        out_shape=SDS((S, 3 * GDN_W), f32),
        scratch_shapes=[pltpu.VMEM((S + HALO, LANE), f32)],
        compiler_params=_cp(1, VMEM_LIMIT),
    )(proj, conv_w)


def _gdn_prep_bwd(proj, conv_w, dqkv, S):
    def body(p_ref, cw_ref, d_ref, dx_ref, dw_ref, xs_ref, dp_ref):
        j = pl.program_id(0)
        xs_ref[pl.ds(0, HALO), :] = jnp.zeros((HALO, LANE), f32)
        xs_ref[pl.ds(HALO, S), :] = p_ref[...]
        dp_ref[pl.ds(S, HALO), :] = jnp.zeros((HALO, LANE), f32)
        w = cw_ref[...]
        is_qk = j < 2 * GDN_HEADS
        scale = jnp.where(j < GDN_HEADS, GDN_HD ** -0.5, 1.0).astype(f32)

        def lp1(c, dw):
            st = pl.multiple_of(c * CONV_RT, CONV_RT)
            taps = [xs_ref[pl.ds(st + HALO - 3 + i, CONV_RT), :] for i in range(4)]
            pre = taps[0] * w[0:1, :]
            for i in range(1, 4):
                pre = pre + taps[i] * w[i:i + 1, :]
            sg = _sigmoid(pre)
            s = pre * sg
            d_out = d_ref[pl.ds(st, CONV_RT), :]
            rn = lax.rsqrt(jnp.sum(s * s, axis=-1, keepdims=True) + 1e-6)
            n = s * rn
            dn = d_out * scale
            ds_qk = rn * (dn - n * jnp.sum(dn * n, axis=-1, keepdims=True))
            ds = jnp.where(is_qk, ds_qk, d_out)
            dpre = ds * (sg * (1.0 + pre * (1.0 - sg)))
            dp_ref[pl.ds(st, CONV_RT), :] = dpre
            return tuple(dw[i] + jnp.sum(dpre * taps[i], axis=0, keepdims=True) for i in range(4))

        z = jnp.zeros((1, LANE), f32)
        dw = lax.fori_loop(0, S // CONV_RT, lp1, (z, z, z, z))
        for i in range(4):
            dw_ref[pl.ds(i, 1), :] = dw[i]

        def lp2(c, carry):
            st = pl.multiple_of(c * CONV_RT, CONV_RT)
            dx = dp_ref[pl.ds(st, CONV_RT), :] * w[3:4, :]
            for i in range(3):
                dx = dx + dp_ref[pl.ds(st + 3 - i, CONV_RT), :] * w[i:i + 1, :]
            dx_ref[pl.ds(st, CONV_RT), :] = dx.astype(bf16)
            return carry

        lax.fori_loop(0, S // CONV_RT, lp2, 0)

    return pl.pallas_call(
        body, name="gdn_prep_bwd", grid=(12,),
        in_specs=[pl.BlockSpec((S, LANE), lambda j: (0, j)), pl.BlockSpec((4, LANE), lambda j: (0, j)),
                  pl.BlockSpec((S, LANE), lambda j: (0, j))],
        out_specs=[pl.BlockSpec((S, LANE), lambda j: (0, j)), pl.BlockSpec((4, LANE), lambda j: (0, j))],
        out_shape=[SDS((S, 3 * GDN_W), bf16), SDS((4, 3 * GDN_W), f32)],
        scratch_shapes=[pltpu.VMEM((S + HALO, LANE), f32), pltpu.VMEM((S + HALO, LANE), f32)],
        compiler_params=_cp(1, VMEM_LIMIT),
    )(proj, conv_w, dqkv)


def _gdn_chunk(q, k, v, bl, al, gate, s0, a_log, dt_bias, og):
    c = CHUNK
    ii = lax.broadcasted_iota(jnp.int32, (c, c), 0)
    jj = lax.broadcasted_iota(jnp.int32, (c, c), 1)
    eye = ii == jj
    tril = ii >= jj
    strict = ii > jj
    ones = jnp.ones((c, c), f32)
    eye_f = eye.astype(f32)

    beta = _sigmoid(bl)
    g = -jnp.exp(a_log) * _softplus(al + dt_bias)
    g_row = _dot_hi(ones, jnp.where(eye, g, 0.0))
    gc = jnp.sum(jnp.where(tril, g_row, 0.0), axis=1, keepdims=True)
    gc_row = _dot_hi(ones, jnp.where(eye, gc, 0.0))
    decay = jnp.where(tril, jnp.exp(jnp.where(tril, gc - gc_row, 0.0)), 0.0)
    last = lax.broadcasted_iota(jnp.int32, (c, 1), 0) == c - 1
    gc_last = jnp.sum(jnp.where(last, gc, 0.0), axis=0, keepdims=True)
    e_gc = jnp.exp(gc)

    kb = k * beta
    k16 = k.astype(bf16)
    a = jnp.where(strict, _dot_nt(kb.astype(bf16), k16) * decay, 0.0)
    xp = -a
    t_inv = eye_f + xp
    for _ in range(5):
        xp = _dot_hi(xp, xp)
        t_inv = _dot_hi(t_inv, eye_f + xp)
    t16 = t_inv.astype(bf16)
    u = _dot(t16, (v * beta).astype(bf16))
    w = _dot(t16, (kb * e_gc).astype(bf16))
    attn = jnp.where(tril, _dot_nt(q.astype(bf16), k16) * decay, 0.0)
    s016 = s0.astype(bf16)
    v_new = u - _dot(w.astype(bf16), s016)
    o = _dot((q * e_gc).astype(bf16), s016) + _dot(attn.astype(bf16), v_new.astype(bf16))
    k_dec = k * jnp.exp(gc_last - gc)
    s1 = s0 * jnp.exp(gc_last) + _dot_tn(k_dec.astype(bf16), v_new.astype(bf16))
    on = o * lax.rsqrt(jnp.mean(o * o, axis=-1, keepdims=True) + RMS_EPS) * og
    return on * (gate * _sigmoid(gate)), s1


GDN_TB = 256


def _gdn_fwd(qkv, proj, a_log, dt_bias, onorm_g, S):
    ncb = GDN_TB // CHUNK
    n_steps = S // GDN_TB

    def body(qkv_ref, gate_ref, pg_ref, al_ref, db_ref, og_ref, out_ref, st_ref, s_scr):
        @pl.when(pl.program_id(0) == 0)
        def _():
            s_scr[...] = jnp.zeros_like(s_scr)

        og = og_ref[...]

        def lp(cl, carry):
            r0 = pl.multiple_of(cl * CHUNK, CHUNK)
            pg = pg_ref[pl.ds(r0, CHUNK), :]
            for h in range(GDN_HEADS):
                q = qkv_ref[pl.ds(r0, CHUNK), h * GDN_HD:(h + 1) * GDN_HD]
                k = qkv_ref[pl.ds(r0, CHUNK), GDN_W + h * GDN_HD:GDN_W + (h + 1) * GDN_HD]
                v = qkv_ref[pl.ds(r0, CHUNK), 2 * GDN_W + h * GDN_HD:2 * GDN_W + (h + 1) * GDN_HD]
                gt = gate_ref[pl.ds(r0, CHUNK), h * GDN_HD:(h + 1) * GDN_HD]
                s0 = s_scr[h]
                st_ref[pl.ds(pl.multiple_of((cl * GDN_HEADS + h) * GDN_HD, GDN_HD), GDN_HD), :] = s0
                out, s1 = _gdn_chunk(q, k, v, pg[:, h:h + 1], pg[:, GDN_HEADS + h:GDN_HEADS + h + 1], gt, s0,
                                     jnp.full((1, 1), al_ref[0, h], f32), jnp.full((1, 1), db_ref[0, h], f32), og)
                s_scr[h] = s1
                out_ref[pl.ds(r0, CHUNK), h * GDN_HD:(h + 1) * GDN_HD] = out
            return carry

        lax.fori_loop(0, ncb, lp, 0)

    st_rows = ncb * GDN_HEADS * GDN_HD
    return pl.pallas_call(
        body, name="gdn_fwd", grid=(n_steps,),
        in_specs=[pl.BlockSpec((GDN_TB, 3 * GDN_W), lambda i: (i, 0)),
                  pl.BlockSpec((GDN_TB, GDN_W), lambda i: (i, 3)),
                  pl.BlockSpec((GDN_TB, LANE), lambda i: (i, COL_G // LANE)),
                  pl.BlockSpec(memory_space=pltpu.SMEM), pl.BlockSpec(memory_space=pltpu.SMEM),
                  _const_spec((1, GDN_HD))],
        out_specs=[pl.BlockSpec((GDN_TB, GDN_W), lambda i: (i, 0)), pl.BlockSpec((st_rows, GDN_HD), lambda i: (i, 0))],
        out_shape=[SDS((S, GDN_W), f32), SDS((S // CHUNK * GDN_HEADS * GDN_HD, GDN_HD), f32)],
        scratch_shapes=[pltpu.VMEM((GDN_HEADS, GDN_HD, GDN_HD), f32)],
        compiler_params=_cp(1, VMEM_LIMIT),
    )(qkv, proj, proj, a_log, dt_bias, onorm_g)


def _gdn_bwd(qkv, proj, states, d_oab, a_log, dt_bias, onorm_g, S):
    ncb = GDN_TB // CHUNK
    n_steps = S // GDN_TB

    def body(qkv_ref, gate_ref, pg_ref, st_ref, do_ref, al_ref, db_ref, og_ref,
             dqkv_ref, dgate_ref, dpg_ref, dal_ref, ddb_ref, dog_ref, ds_scr):
        @pl.when(pl.program_id(0) == 0)
        def _():
            ds_scr[...] = jnp.zeros_like(ds_scr)
            dal_ref[...] = jnp.zeros_like(dal_ref)
            ddb_ref[...] = jnp.zeros_like(ddb_ref)
            dog_ref[...] = jnp.zeros_like(dog_ref)

        og = og_ref[...]
        lane = lax.broadcasted_iota(jnp.int32, (CHUNK, LANE), 1)
        lane1 = lax.broadcasted_iota(jnp.int32, (1, LANE), 1)

        def lp(t, carry):
            cl = ncb - 1 - t
            r0 = pl.multiple_of(cl * CHUNK, CHUNK)
            pg = pg_ref[pl.ds(r0, CHUNK), :]
            dpg = jnp.zeros((CHUNK, LANE), f32)
            for h in range(GDN_HEADS):
                q = qkv_ref[pl.ds(r0, CHUNK), h * GDN_HD:(h + 1) * GDN_HD]
                k = qkv_ref[pl.ds(r0, CHUNK), GDN_W + h * GDN_HD:GDN_W + (h + 1) * GDN_HD]
                v = qkv_ref[pl.ds(r0, CHUNK), 2 * GDN_W + h * GDN_HD:2 * GDN_W + (h + 1) * GDN_HD]
                gt = gate_ref[pl.ds(r0, CHUNK), h * GDN_HD:(h + 1) * GDN_HD]
                s0 = st_ref[pl.ds(pl.multiple_of((cl * GDN_HEADS + h) * GDN_HD, GDN_HD), GDN_HD), :]
                d_out = do_ref[pl.ds(r0, CHUNK), h * GDN_HD:(h + 1) * GDN_HD]
                _, vjp = jax.vjp(_gdn_chunk, q, k, v, pg[:, h:h + 1], pg[:, GDN_HEADS + h:GDN_HEADS + h + 1], gt, s0,
                                 jnp.full((1, 1), al_ref[0, h], f32), jnp.full((1, 1), db_ref[0, h], f32), og)
                dq, dk, dv, dbl, dal, dgt, ds0, da, ddb, dog = vjp((d_out, ds_scr[h]))
                ds_scr[h] = ds0
                dqkv_ref[pl.ds(r0, CHUNK), h * GDN_HD:(h + 1) * GDN_HD] = dq
                dqkv_ref[pl.ds(r0, CHUNK), GDN_W + h * GDN_HD:GDN_W + (h + 1) * GDN_HD] = dk
                dqkv_ref[pl.ds(r0, CHUNK), 2 * GDN_W + h * GDN_HD:2 * GDN_W + (h + 1) * GDN_HD] = dv
                dgate_ref[pl.ds(r0, CHUNK), h * GDN_HD:(h + 1) * GDN_HD] = dgt.astype(bf16)
                dpg = dpg + jnp.where(lane == h, dbl, 0.0) + jnp.where(lane == GDN_HEADS + h, dal, 0.0)
                dal_ref[...] += jnp.where(lane1 == h, da, 0.0)
                ddb_ref[...] += jnp.where(lane1 == h, ddb, 0.0)
                dog_ref[...] += dog
            dpg_ref[pl.ds(r0, CHUNK), :] = dpg.astype(bf16)
            return carry

        lax.fori_loop(0, ncb, lp, 0)

    st_rows = ncb * GDN_HEADS * GDN_HD
    rev = lambda i: n_steps - 1 - i
    acc_spec = _const_spec((1, LANE))
    return pl.pallas_call(
        body, name="gdn_bwd", grid=(n_steps,),
        in_specs=[pl.BlockSpec((GDN_TB, 3 * GDN_W), lambda i: (rev(i), 0)),
                  pl.BlockSpec((GDN_TB, GDN_W), lambda i: (rev(i), 3)),
                  pl.BlockSpec((GDN_TB, LANE), lambda i: (rev(i), COL_G // LANE)),
                  pl.BlockSpec((st_rows, GDN_HD), lambda i: (rev(i), 0)),
                  pl.BlockSpec((GDN_TB, GDN_W), lambda i: (rev(i), 0)),
                  pl.BlockSpec(memory_space=pltpu.SMEM), pl.BlockSpec(memory_space=pltpu.SMEM),
                  _const_spec((1, GDN_HD))],
        out_specs=[pl.BlockSpec((GDN_TB, 3 * GDN_W), lambda i: (rev(i), 0)),
                   pl.BlockSpec((GDN_TB, GDN_W), lambda i: (rev(i), 0)),
                   pl.BlockSpec((GDN_TB, LANE), lambda i: (rev(i), 0)),
                   acc_spec, acc_spec, acc_spec],
        out_shape=[SDS((S, 3 * GDN_W), f32), SDS((S, GDN_W), bf16), SDS((S, LANE), bf16),
                   SDS((1, LANE), f32), SDS((1, LANE), f32), SDS((1, LANE), f32)],
        scratch_shapes=[pltpu.VMEM((GDN_HEADS, GDN_HD, GDN_HD), f32)],
        compiler_params=_cp(1, VMEM_LIMIT),
    )(qkv, proj, proj, states, d_oab, a_log, dt_bias, onorm_g)


def _swa_tiles(q_ref, k_ref, v_ref, it, d, nb_log2, S):
    nb = 1 << nb_log2
    r = lax.shift_right_logical(it, nb_log2)
    blk = lax.bitwise_and(it, nb - 1)
    qs = blk * (SWA_BLK * d) + r
    ps = jnp.maximum(blk - 1, 0) * (SWA_BLK * d) + r
    if d > 1:
        rows_c, rows_p = pl.ds(qs, SWA_BLK, stride=d), pl.ds(ps, SWA_BLK, stride=d)
    else:
        rows_c, rows_p = pl.ds(pl.multiple_of(qs, SWA_BLK), SWA_BLK), pl.ds(pl.multiple_of(ps, SWA_BLK), SWA_BLK)
    return rows_c, rows_p, blk > 0


def _swa_fwd(proj, bt, S):
    scale = SWA_HD ** -0.5

    def body(q_ref, k_ref, v_ref, bt_ref, o_ref, lse_ref, m_scr, l_scr, acc_scr):
        lane = lax.broadcasted_iota(jnp.int32, (SWA_BLK, LANE), 1)
        h0 = lane < SWA_HD
        m_scr[...] = jnp.full((S, LANE), NEG, f32)
        l_scr[...] = jnp.zeros((S, LANE), f32)
        acc_scr[...] = jnp.zeros((S, LANE), f32)
        for p, (_, d) in enumerate(PATTERNS):
            nb_log2 = int(math.log2(S // d // SWA_BLK))

            def lp(it, carry, p=p, d=d, nb_log2=nb_log2):
                rows_c, rows_p, has_prev = _swa_tiles(q_ref, k_ref, v_ref, it, d, nb_log2, S)
                q = q_ref[rows_c, :]
                kc = k_ref[rows_c, :].astype(bf16)
                vc = v_ref[rows_c, :].astype(bf16)
                kp = k_ref[rows_p, :].astype(bf16)
                vp = v_ref[rows_p, :].astype(bf16)
                m_old = m_scr[rows_c, :]
                l_old = l_scr[rows_c, :]
                acc_old = acc_scr[rows_c, :]
                res = []
                for h in range(2):
                    mh = h0 if h == 0 else jnp.logical_not(h0)
                    qh = jnp.where(mh, q, 0.0).astype(bf16)
                    s_c = _dot_nt(qh, kc) * scale + bt_ref[p, h, 1]
                    s_p = jnp.where(has_prev, _dot_nt(qh, kp) * scale + bt_ref[p, h, 0], NEG)
                    mo = m_old[:, h * SWA_HD:h * SWA_HD + 1]
                    lo = l_old[:, h * SWA_HD:h * SWA_HD + 1]
                    mn = jnp.maximum(mo, jnp.maximum(jnp.max(s_c, axis=1, keepdims=True), jnp.max(s_p, axis=1, keepdims=True)))
                    alpha = jnp.exp(mo - mn)
                    p_c = jnp.exp(s_c - mn)
                    p_p = jnp.exp(s_p - mn)
                    ln = alpha * lo + jnp.sum(p_c, axis=1, keepdims=True) + jnp.sum(p_p, axis=1, keepdims=True)
                    pv = _dot(p_c.astype(bf16), vc) + _dot(p_p.astype(bf16), vp)
                    res.append((mn, ln, alpha, pv))
                m_scr[rows_c, :] = jnp.where(h0, res[0][0], res[1][0])
                l_scr[rows_c, :] = jnp.where(h0, res[0][1], res[1][1])
                acc_scr[rows_c, :] = acc_old * jnp.where(h0, res[0][2], res[1][2]) + jnp.where(h0, res[0][3], res[1][3])
                return carry

            lax.fori_loop(0, S // SWA_BLK, lp, 0)
        l_all = l_scr[...]
        o_ref[...] = acc_scr[...] / l_all
        lse_ref[...] = m_scr[...] + jnp.log(l_all)

    qb = COL_B // LANE
    return pl.pallas_call(
        body, name="swa_fwd", grid=(4,),
        in_specs=[pl.BlockSpec((S, LANE), lambda hp: (0, qb + hp)), pl.BlockSpec((S, LANE), lambda hp: (0, qb + 4 + hp)),
                  pl.BlockSpec((S, LANE), lambda hp: (0, qb + 8 + hp)),
                  pl.BlockSpec((3, 2, 2, SWA_BLK, SWA_BLK), lambda hp: (0, hp, 0, 0, 0))],
        out_specs=[pl.BlockSpec((S, LANE), lambda hp: (0, hp)), pl.BlockSpec((S, LANE), lambda hp: (0, hp))],
        out_shape=[SDS((S, SWA_W), f32), SDS((S, SWA_W), f32)],
        scratch_shapes=[pltpu.VMEM((S, LANE), f32)] * 3,
        compiler_params=_cp(1, VMEM_LIMIT),
    )(proj, proj, proj, bt)


def _swa_bwd(proj, bt, ob, lse, d_oab, S):
    scale = SWA_HD ** -0.5

    def body(q_ref, k_ref, v_ref, bt_ref, o_ref, lse_ref, do_ref, dq_ref, dk_ref, dv_ref, dsb_ref, dq_scr, dk_scr, dv_scr):
        lane = lax.broadcasted_iota(jnp.int32, (SWA_BLK, LANE), 1)
        h0 = lane < SWA_HD
        dq_scr[...] = jnp.zeros((S, LANE), f32)
        dk_scr[...] = jnp.zeros((S, LANE), f32)
        dv_scr[...] = jnp.zeros((S, LANE), f32)
        dsb_ref[...] = jnp.zeros_like(dsb_ref)
        for p, (_, d) in enumerate(PATTERNS):
            nb_log2 = int(math.log2(S // d // SWA_BLK))

            def lp(it, carry, p=p, d=d, nb_log2=nb_log2):
                rows_c, rows_p, has_prev = _swa_tiles(q_ref, k_ref, v_ref, it, d, nb_log2, S)
                q = q_ref[rows_c, :]
                kc = k_ref[rows_c, :].astype(bf16)
                vc = v_ref[rows_c, :].astype(bf16)
                kp = k_ref[rows_p, :].astype(bf16)
                vp = v_ref[rows_p, :].astype(bf16)
                do = do_ref[rows_c, :]
                lse_t = lse_ref[rows_c, :]
                dlt = do * o_ref[rows_c, :]
                dq_t = jnp.zeros((SWA_BLK, LANE), f32)
                dkc_t = jnp.zeros((SWA_BLK, LANE), f32)
                dkp_t = jnp.zeros((SWA_BLK, LANE), f32)
                dvc_t = jnp.zeros((SWA_BLK, LANE), f32)
                dvp_t = jnp.zeros((SWA_BLK, LANE), f32)
                for h in range(2):
                    mh = h0 if h == 0 else jnp.logical_not(h0)
                    qh = jnp.where(mh, q, 0.0).astype(bf16)
                    doh = jnp.where(mh, do, 0.0).astype(bf16)
                    delta = jnp.sum(jnp.where(mh, dlt, 0.0), axis=1, keepdims=True)
                    lse_h = lse_t[:, h * SWA_HD:h * SWA_HD + 1]
                    s_c = _dot_nt(qh, kc) * scale + bt_ref[p, h, 1]
                    s_p = jnp.where(has_prev, _dot_nt(qh, kp) * scale + bt_ref[p, h, 0], NEG)
                    p_c = jnp.exp(s_c - lse_h)
                    p_p = jnp.exp(s_p - lse_h)
                    ds_c = p_c * (_dot_nt(doh, vc) - delta)
                    ds_p = p_p * (_dot_nt(doh, vp) - delta)
                    dsb_ref[0, p, h, 1] += ds_c
                    dsb_ref[0, p, h, 0] += ds_p
                    dsc16 = ds_c.astype(bf16)
                    dsp16 = ds_p.astype(bf16)
                    dq_t = dq_t + jnp.where(mh, (_dot(dsc16, kc) + _dot(dsp16, kp)) * scale, 0.0)
                    dkc_t = dkc_t + _dot_tn(dsc16, qh) * scale
                    dkp_t = dkp_t + _dot_tn(dsp16, qh) * scale
                    dvc_t = dvc_t + _dot_tn(p_c.astype(bf16), doh)
                    dvp_t = dvp_t + _dot_tn(p_p.astype(bf16), doh)
                dq_scr[rows_c, :] = dq_scr[rows_c, :] + dq_t
                dk_scr[rows_c, :] = dk_scr[rows_c, :] + dkc_t
                dv_scr[rows_c, :] = dv_scr[rows_c, :] + dvc_t
                dk_scr[rows_p, :] = dk_scr[rows_p, :] + dkp_t
                dv_scr[rows_p, :] = dv_scr[rows_p, :] + dvp_t
                return carry

            lax.fori_loop(0, S // SWA_BLK, lp, 0)
        dq_ref[...] = dq_scr[...].astype(bf16)
        dk_ref[...] = dk_scr[...].astype(bf16)
        dv_ref[...] = dv_scr[...].astype(bf16)

    qb = COL_B // LANE
    col = lambda c: pl.BlockSpec((S, LANE), lambda hp, c=c: (0, c + hp))
    return pl.pallas_call(
        body, name="swa_bwd", grid=(4,),
        in_specs=[col(qb), col(qb + 4), col(qb + 8),
                  pl.BlockSpec((3, 2, 2, SWA_BLK, SWA_BLK), lambda hp: (0, hp, 0, 0, 0)),
                  col(0), col(0), col(4)],
        out_specs=[col(0), col(0), col(0),
                   pl.BlockSpec((1, 3, 2, 2, SWA_BLK, SWA_BLK), lambda hp: (hp, 0, 0, 0, 0, 0))],
        out_shape=[SDS((S, SWA_W), bf16)] * 3 + [SDS((4, 3, 2, 2, SWA_BLK, SWA_BLK), f32)],
        scratch_shapes=[pltpu.VMEM((S, LANE), f32)] * 3,
        compiler_params=_cp(1, VMEM_LIMIT),
    )(proj, proj, proj, bt, ob, lse, d_oab)


def _mix_fwd(oa, ob, w_out, x, g_post, S):
    TS = 512

    def body(oa_ref, ob_ref, w_ref, x_ref, g_ref, mix_ref, x1_ref):
        mix = _dot(oa_ref[...].astype(bf16), w_ref[0:GDN_W, :]) + _dot(ob_ref[...].astype(bf16), w_ref[GDN_W:D_MODEL, :])
        r = lax.rsqrt(jnp.mean(mix * mix, axis=-1, keepdims=True) + RMS_EPS)
        mix_ref[...] = mix
        x1_ref[...] = x_ref[...] + mix * r * g_ref[...]

    row = lambda w: pl.BlockSpec((TS, w), lambda i: (i, 0))
    return pl.pallas_call(
        body, name="mix_fwd", grid=(S // TS,),
        in_specs=[row(GDN_W), row(SWA_W), _resident_spec((D_MODEL, D_MODEL)), row(D_MODEL), _const_spec((1, D_MODEL))],
        out_specs=[row(D_MODEL), row(D_MODEL)],
        out_shape=[SDS((S, D_MODEL), f32), SDS((S, D_MODEL), f32)],
        compiler_params=_cp(1, VMEM_LIMIT),
    )(oa, ob, w_out, x, g_post)


def _mix_bwd(dx1, mix, g_post, w_out, S):
    TS = 512

    def body(dx1_ref, mix_ref, g_ref, w_ref, dmix_ref, doab_ref, dg_ref):
        @pl.when(pl.program_id(0) == 0)
        def _():
            dg_ref[...] = jnp.zeros_like(dg_ref)

        mix = mix_ref[...]
        dz = dx1_ref[...]
        r = lax.rsqrt(jnp.mean(mix * mix, axis=-1, keepdims=True) + RMS_EPS)
        n = mix * r
        dg_ref[...] += jnp.sum(dz * n, axis=0, keepdims=True)
        dn = dz * g_ref[...]
        dmix = (r * (dn - n * jnp.mean(dn * n, axis=-1, keepdims=True))).astype(bf16)
        dmix_ref[...] = dmix
        doab_ref[...] = _dot_nt(dmix, w_ref[...])

    row = lambda: pl.BlockSpec((TS, D_MODEL), lambda i: (i, 0))
    return pl.pallas_call(
        body, name="mix_bwd", grid=(S // TS,),
        in_specs=[row(), row(), _const_spec((1, D_MODEL)), _resident_spec((D_MODEL, D_MODEL))],
        out_specs=[row(), row(), _const_spec((1, D_MODEL))],
        out_shape=[SDS((S, D_MODEL), bf16), SDS((S, D_MODEL), f32), SDS((1, D_MODEL), f32)],
        compiler_params=_cp(1, VMEM_LIMIT),
    )(dx1, mix, g_post, w_out)


FFN_TS = 256
FFN_CH = 1408


def _ffn(x1, tgt, g_pre, g_post, wg, wu, wd, S):
    def body(x1_ref, t_ref, gp_ref, gq_ref, wg_ref, wu_ref, wd_ref,
             dx1_ref, h2_ref, act_ref, dgate_ref, dup_ref, df_ref, loss_ref, dgp_ref, dgq_ref, gate_scr, up_scr):
        @pl.when(pl.program_id(0) == 0)
        def _():
            loss_ref[...] = jnp.zeros_like(loss_ref)
            dgp_ref[...] = jnp.zeros_like(dgp_ref)
            dgq_ref[...] = jnp.zeros_like(dgq_ref)

        x1v = x1_ref[...]
        gp = gp_ref[...]
        gq = gq_ref[...]
        r2 = lax.rsqrt(jnp.mean(x1v * x1v, axis=-1, keepdims=True) + RMS_EPS)
        n2 = x1v * r2
        h2 = (n2 * gp).astype(bf16)
        h2_ref[...] = h2
        f = jnp.zeros((FFN_TS, D_MODEL), f32)
        for c in range(D_FF // FFN_CH):
            cs = slice(c * FFN_CH, (c + 1) * FFN_CH)
            gate = _dot(h2, wg_ref[:, cs])
            up = _dot(h2, wu_ref[:, cs])
            gate_scr[:, cs] = gate
            up_scr[:, cs] = up
            act = (gate * _sigmoid(gate) * up).astype(bf16)
            act_ref[:, cs] = act
            f = f + _dot(act, wd_ref[cs, :])
        r3 = lax.rsqrt(jnp.mean(f * f, axis=-1, keepdims=True) + RMS_EPS)
        n3 = f * r3
        err = x1v + n3 * gq - t_ref[...]
        loss_ref[...] += 0.5 * jnp.sum(jnp.mean(err * err, axis=-1, keepdims=True), axis=0, keepdims=True)
        dy = err * (1.0 / D_MODEL)
        dgq_ref[...] += jnp.sum(dy * n3, axis=0, keepdims=True)
        dn3 = dy * gq
        df = (r3 * (dn3 - n3 * jnp.mean(dn3 * n3, axis=-1, keepdims=True))).astype(bf16)
        df_ref[...] = df
        dh2 = jnp.zeros((FFN_TS, D_MODEL), f32)
        for c in range(D_FF // FFN_CH):
            cs = slice(c * FFN_CH, (c + 1) * FFN_CH)
            gate = gate_scr[:, cs]
            up = up_scr[:, cs]
            dact = _dot_nt(df, wd_ref[cs, :])
            sg = _sigmoid(gate)
            dup = (dact * gate * sg).astype(bf16)
            dgate = (dact * up * (sg * (1.0 + gate * (1.0 - sg)))).astype(bf16)
            dup_ref[:, cs] = dup
            dgate_ref[:, cs] = dgate
            dh2 = dh2 + _dot_nt(dgate, wg_ref[:, cs]) + _dot_nt(dup, wu_ref[:, cs])
        dgp_ref[...] += jnp.sum(dh2 * n2, axis=0, keepdims=True)
        dn2 = dh2 * gp
        dx1_ref[...] = dy + r2 * (dn2 - n2 * jnp.mean(dn2 * n2, axis=-1, keepdims=True))

    row = lambda w: pl.BlockSpec((FFN_TS, w), lambda i: (i, 0))
    vec = _const_spec((1, D_MODEL))
    return pl.pallas_call(
        body, name="ffn_fwd_bwd", grid=(S // FFN_TS,),
        in_specs=[row(D_MODEL), row(D_MODEL), vec, vec, _resident_spec((D_MODEL, D_FF)), _resident_spec((D_MODEL, D_FF)),
                  _resident_spec((D_FF, D_MODEL))],
        out_specs=[row(D_MODEL), row(D_MODEL), row(D_FF), row(D_FF), row(D_FF), row(D_MODEL), _const_spec((1, LANE)), vec, vec],
        out_shape=[SDS((S, D_MODEL), f32), SDS((S, D_MODEL), bf16), SDS((S, D_FF), bf16), SDS((S, D_FF), bf16),
                   SDS((S, D_FF), bf16), SDS((S, D_MODEL), bf16), SDS((1, LANE), f32), SDS((1, D_MODEL), f32),
                   SDS((1, D_MODEL), f32)],
        scratch_shapes=[pltpu.VMEM((FFN_TS, D_FF), f32), pltpu.VMEM((FFN_TS, D_FF), f32)],
        compiler_params=_cp(1, VMEM_LIMIT),
    )(x1, tgt, g_pre, g_post, wg, wu, wd)


def _proj_bwd(x, dx1, g_pre, wcat, segs, S):
    TS = 256
    n = len(segs)
    cols = [(c0, a.shape[1]) for a, c0 in segs]

    def body(*refs):
        x_ref, dx1_ref, g_ref, w_ref = refs[:4]
        seg_refs = refs[4:4 + n]
        gx_ref, h1_ref, dg_ref = refs[4 + n:]

        @pl.when(pl.program_id(0) == 0)
        def _():
            dg_ref[...] = jnp.zeros_like(dg_ref)

        dh = jnp.zeros((TS, D_MODEL), f32)
        for s_ref, (c0, w) in zip(seg_refs, cols):
            dh = dh + _dot_nt(s_ref[...], w_ref[:, c0:c0 + w])
        xv = x_ref[...]
        g = g_ref[...]
        r = lax.rsqrt(jnp.mean(xv * xv, axis=-1, keepdims=True) + RMS_EPS)
        nx = xv * r
        h1_ref[...] = (nx * g).astype(bf16)
        dg_ref[...] += jnp.sum(dh * nx, axis=0, keepdims=True)
        dn = dh * g
        gx_ref[...] = dx1_ref[...] + r * (dn - nx * jnp.mean(dn * nx, axis=-1, keepdims=True))

    row = lambda w: pl.BlockSpec((TS, w), lambda i: (i, 0))
    return pl.pallas_call(
        body, name="proj_bwd", grid=(S // TS,),
        in_specs=[row(D_MODEL), row(D_MODEL), _const_spec((1, D_MODEL)), _resident_spec((D_MODEL, NCOL))]
                 + [row(w) for _, w in cols],
        out_specs=[row(D_MODEL), row(D_MODEL), _const_spec((1, D_MODEL))],
        out_shape=[SDS((S, D_MODEL), f32), SDS((S, D_MODEL), bf16), SDS((1, D_MODEL), f32)],
        compiler_params=_cp(1, VMEM_LIMIT),
    )(x, dx1, g_pre, wcat, *[a for a, _ in segs])


def _wgrad(a, b, S, name):
    TS = 512
    K = a.shape[1]
    N = b.shape[1]
    TN = next(t for t in (512, 1408, N) if N % t == 0)

    def body(a_ref, b_ref, o_ref):
        @pl.when(pl.program_id(1) == 0)
        def _():
            o_ref[...] = jnp.zeros_like(o_ref)

        o_ref[...] += _dot_tn(a_ref[...].astype(bf16), b_ref[...])

    return pl.pallas_call(
        body, name=name, grid=(N // TN, S // TS),
        in_specs=[pl.BlockSpec((TS, K), lambda j, s: (s, 0)), pl.BlockSpec((TS, TN), lambda j, s: (s, j))],
        out_specs=pl.BlockSpec((K, TN), lambda j, s: (0, j)),
        out_shape=SDS((K, N), f32),
        compiler_params=_cp(2, VMEM_LIMIT),
    )(a, b)


def _adamw(recv, w, m, v, name):
    R, C = w.shape
    TR = 256 if R % 256 == 0 else R
    c1 = 1.0 / (1.0 - ADAM_B1 ** ADAM_STEP)
    c2 = 1.0 / (1.0 - ADAM_B2 ** ADAM_STEP)

    def body(r_ref, w_ref, m_ref, v_ref, g_out, d_out, m_out, v_out):
        g = r_ref[0].astype(f32)
        for s in range(1, N_DEV):
            g = g + r_ref[s].astype(f32)
        mn = ADAM_B1 * m_ref[...] + (1.0 - ADAM_B1) * g
        vn = ADAM_B2 * v_ref[...] + (1.0 - ADAM_B2) * (g * g)
        g_out[...] = g
        m_out[...] = mn
        v_out[...] = vn
        d_out[...] = -ADAM_LR * ((mn * c1) / (jnp.sqrt(vn * c2) + ADAM_EPS) + ADAM_WD * w_ref[...])

    blk = pl.BlockSpec((TR, C), lambda i: (i, 0))
    return pl.pallas_call(
        body, name=name, grid=(R // TR,),
        in_specs=[pl.BlockSpec((N_DEV, TR, C), lambda i: (0, i, 0)), blk, blk, blk],
        out_specs=[blk, blk, blk, blk],
        out_shape=[SDS((R, C), f32)] * 4,
        compiler_params=_cp(1, VMEM_LIMIT),
    )(recv, w, m, v)


MESH = pl.DeviceIdType.MESH
_ANY = pl.BlockSpec(memory_space=pl.ANY)


def _flip(v, d):
    return 1 - v if d else v


def _all_gather(shards):
    n = len(shards)

    def body(*refs):
        ins = refs[:n]
        outs = refs[n:2 * n]
        send_sems, recv_sems, local_sems = refs[2 * n:]
        x, y, c = lax.axis_index("x"), lax.axis_index("y"), lax.axis_index("c")
        me, sibling = (x, y, c), (x, y, 1 - c)
        chips = [(1 - x, y), (x, 1 - y), (1 - x, 1 - y)]

        def slot(px, py, pc):
            return 4 * px + 2 * py + pc

        def copy(a, k, block, to, src=None):
            dst = outs[a].at[slot(*block)]
            return pltpu.make_async_remote_copy(src_ref=dst if src is None else src, dst_ref=dst,
                                                send_sem=send_sems.at[a, k], recv_sem=recv_sems.at[a, k],
                                                device_id=to, device_id_type=MESH)

        mine, first, passed = [], [], []
        for a in range(n):
            cp = pltpu.make_async_copy(ins[a], outs[a].at[slot(*me)], local_sems.at[a])
            cp.start()
            mine.append(cp)
            fs = [copy(a, 0, me, sibling, src=ins[a])]
            fs += [copy(a, 1 + j, me, (*chip, c), src=ins[a]) for j, chip in enumerate(chips)]
            for cp in fs:
                cp.start()
            first += fs
        for j, chip in enumerate(chips):
            for a in range(n):
                copy(a, 1 + j, (*chip, c), me).wait_recv()
                cp = copy(a, 4 + j, (*chip, c), sibling)
                cp.start()
                passed.append(cp)
        for a in range(n):
            copy(a, 0, sibling, me).wait_recv()
            for j, chip in enumerate(chips):
                copy(a, 4 + j, (*chip, 1 - c), me).wait_recv()
        for cp in first + passed:
            cp.wait_send()
        for cp in mine:
            cp.wait()

    return pl.pallas_call(
        body, name="weight_all_gather",
        in_specs=[_ANY] * n, out_specs=[_ANY] * n,
        out_shape=[SDS((N_DEV,) + s.shape, s.dtype) for s in shards],
        scratch_shapes=[pltpu.SemaphoreType.DMA((n, 7)), pltpu.SemaphoreType.DMA((n, 7)), pltpu.SemaphoreType.DMA((n,))],
        compiler_params=pltpu.CompilerParams(has_side_effects=True),
    )(*shards)


def _grad_exchange(blocked, whole):
    arrs = list(blocked) + list(whole)
    n, nb = len(arrs), len(blocked)
    rel = [(dx, dy, dc) for dx in (0, 1) for dy in (0, 1) for dc in (0, 1) if dx or dy or dc]

    def body(*refs):
        ins = refs[:n]
        outs = refs[n:2 * n]
        send_sems, recv_sems, local_sems = refs[2 * n:]
        x, y, c = lax.axis_index("x"), lax.axis_index("y"), lax.axis_index("c")
        me = 4 * x + 2 * y + c
        sends, locs = [], []
        for a in range(n):
            cp = pltpu.make_async_copy(ins[a].at[me] if a < nb else ins[a], outs[a].at[me], local_sems.at[a])
            cp.start()
            locs.append(cp)
            for k, (dx, dy, dc) in enumerate(rel):
                peer = (_flip(x, dx), _flip(y, dy), _flip(c, dc))
                pidx = 4 * peer[0] + 2 * peer[1] + peer[2]
                cp = pltpu.make_async_remote_copy(src_ref=ins[a].at[pidx] if a < nb else ins[a], dst_ref=outs[a].at[me],
                                                  send_sem=send_sems.at[a, k], recv_sem=recv_sems.at[a, k],
                                                  device_id=peer, device_id_type=MESH)
                cp.start()
                sends.append(cp)
        for a in range(n):
            for k, (dx, dy, dc) in enumerate(rel):
                peer = (_flip(x, dx), _flip(y, dy), _flip(c, dc))
                pidx = 4 * peer[0] + 2 * peer[1] + peer[2]
                pltpu.make_async_remote_copy(src_ref=outs[a].at[pidx], dst_ref=outs[a].at[pidx],
                                             send_sem=send_sems.at[a, k], recv_sem=recv_sems.at[a, k],
                                             device_id=peer, device_id_type=MESH).wait_recv()
        for cp in sends:
            cp.wait_send()
        for cp in locs:
            cp.wait()

    shapes = [SDS(a.shape, a.dtype) for a in blocked] + [SDS((N_DEV,) + a.shape, a.dtype) for a in whole]
    return pl.pallas_call(
        body, name="grad_exchange",
        in_specs=[_ANY] * n, out_specs=[_ANY] * n, out_shape=shapes,
        scratch_shapes=[pltpu.SemaphoreType.DMA((n, 7)), pltpu.SemaphoreType.DMA((n, 7)), pltpu.SemaphoreType.DMA((n,))],
        compiler_params=pltpu.CompilerParams(has_side_effects=True),
    )(*arrs)


def _local_step(x, tgt, wcat, convw, wout, wgate, wup, wdown, a_log, dt_bias, onorm_g, rel_bias,
                g_mix_pre, g_mix_post, g_ffn_pre, g_ffn_post):
    S = x.shape[0]
    bk_np = _bucket_tables()
    bk = jnp.asarray(bk_np)
    bt = _bias_tables(rel_bias, bk)
    proj = _proj_fwd(x, g_mix_pre, wcat, S)
    qkv = _gdn_prep(proj, convw, S)
    oa, states = _gdn_fwd(qkv, proj, a_log, dt_bias, onorm_g, S)
    ob, lse = _swa_fwd(proj, bt, S)
    mix, x1 = _mix_fwd(oa, ob, wout, x, g_mix_post, S)
    dx1, h2, act, dgate_f, dup_f, df, loss, d_gfpre, d_gfpost = _ffn(x1, tgt, g_ffn_pre, g_ffn_post, wgate, wup, wdown, S)
    dmix, d_oab, d_gmpost = _mix_bwd(dx1, mix, g_mix_post, wout, S)
    dqb, dkb, dvb, dsb = _swa_bwd(proj, bt, ob, lse, d_oab, S)
    dqkv, dgate_a, dpg, d_alog, d_dtb, d_og = _gdn_bwd(qkv, proj, states, d_oab, a_log, dt_bias, onorm_g, S)
    dqkv_a, d_conv = _gdn_prep_bwd(proj, convw, dqkv, S)
    segs = [(dqkv_a, COL_A), (dgate_a, COL_A + 3 * GDN_W), (dqb, COL_B), (dkb, COL_B + SWA_W), (dvb, COL_B + 2 * SWA_W),
            (dpg, COL_G)]
    grad_x, h1, d_gmpre = _proj_bwd(x, dx1, g_mix_pre, wcat, segs, S)
    g_segs = [_wgrad(h1, a, S, "wgrad_in_%d" % i) for i, (a, _) in enumerate(segs)]
    g_in = jnp.concatenate([g_segs[0], g_segs[1], g_segs[5][:, :2 * GDN_HEADS], g_segs[2], g_segs[3], g_segs[4]], axis=1)
    g_out = jnp.concatenate([_wgrad(oa, dmix, S, "wgrad_out_a"), _wgrad(ob, dmix, S, "wgrad_out_b")], axis=0)
    g_gate = _wgrad(h2, dgate_f, S, "wgrad_gate")
    g_up = _wgrad(h2, dup_f, S, "wgrad_up")
    g_down = _wgrad(act, df, S, "wgrad_down")
    d_rel = _rel_bias_grad(dsb, bk, bk_np)
    small = dict(a_log=d_alog[:, :GDN_HEADS], dt_bias=d_dtb[:, :GDN_HEADS], onorm_g=d_og, rel_bias=d_rel,
                 g_mix_pre=d_gmpre, g_mix_post=d_gmpost, g_ffn_pre=d_gfpre, g_ffn_post=d_gfpost)
    return loss, grad_x, (g_in, d_conv, g_out, g_gate, g_up, g_down), small


SMALL = ("a_log", "dt_bias", "onorm_g", "rel_bias", "g_mix_pre", "g_mix_post", "g_ffn_pre", "g_ffn_post")
PACK_ROWS = 8


def _pack_small(d):
    rest = jnp.concatenate([d["onorm_g"].reshape(-1), d["a_log"].reshape(-1), d["dt_bias"].reshape(-1),
                            d["rel_bias"].reshape(-1)])
    rest = jnp.concatenate([rest, jnp.zeros((D_MODEL - rest.shape[0],), f32)])
    rows = [d["g_mix_pre"].reshape(-1), d["g_mix_post"].reshape(-1), d["g_ffn_pre"].reshape(-1),
            d["g_ffn_post"].reshape(-1), rest]
    return jnp.concatenate([jnp.stack(rows), jnp.zeros((PACK_ROWS - len(rows), D_MODEL), f32)], axis=0)


def _unpack_small(p):
    o = GDN_HD
    return dict(g_mix_pre=p[0:1], g_mix_post=p[1:2], g_ffn_pre=p[2:3], g_ffn_post=p[3:4],
                onorm_g=p[4:5, :o], a_log=p[4:5, o:o + 4], dt_bias=p[4:5, o + 4:o + 8],
                rel_bias=p[4, o + 8:o + 8 + NUM_BUCKETS * SWA_HEADS].reshape(NUM_BUCKETS, SWA_HEADS))


def kernel(x, w_in, conv_w, a_log, dt_bias, onorm_g, rel_bias, w_out, g_mix_pre, g_mix_post, w_gate, w_up, w_down, g_ffn_pre, g_ffn_post, loss_target, m_w_in, m_conv_w, m_a_log, m_dt_bias, m_onorm_g, m_rel_bias, m_w_out, m_g_mix_pre, m_g_mix_post, m_w_gate, m_w_up, m_w_down, m_g_ffn_pre, m_g_ffn_post, v_w_in, v_conv_w, v_a_log, v_dt_bias, v_onorm_g, v_rel_bias, v_w_out, v_g_mix_pre, v_g_mix_post, v_w_gate, v_w_up, v_w_down, v_g_ffn_pre, v_g_ffn_post):
    big = ("w_in", "conv_w", "w_out", "w_gate", "w_up", "w_down")
    w_sh = dict(w_in=w_in[0], conv_w=conv_w[0], w_out=w_out[0], w_gate=w_gate[0], w_up=w_up[0], w_down=w_down[0])
    m_sh = dict(w_in=m_w_in[0], conv_w=m_conv_w[0], w_out=m_w_out[0], w_gate=m_w_gate[0], w_up=m_w_up[0], w_down=m_w_down[0])
    v_sh = dict(w_in=v_w_in[0], conv_w=v_conv_w[0], w_out=v_w_out[0], w_gate=v_w_gate[0], w_up=v_w_up[0], w_down=v_w_down[0])
    w_small = dict(a_log=a_log, dt_bias=dt_bias, onorm_g=onorm_g, rel_bias=rel_bias, g_mix_pre=g_mix_pre,
                   g_mix_post=g_mix_post, g_ffn_pre=g_ffn_pre, g_ffn_post=g_ffn_post)
    m_small = dict(a_log=m_a_log, dt_bias=m_dt_bias, onorm_g=m_onorm_g, rel_bias=m_rel_bias, g_mix_pre=m_g_mix_pre,
                   g_mix_post=m_g_mix_post, g_ffn_pre=m_g_ffn_pre, g_ffn_post=m_g_ffn_post)
    v_small = dict(a_log=v_a_log, dt_bias=v_dt_bias, onorm_g=v_onorm_g, rel_bias=v_rel_bias, g_mix_pre=v_g_mix_pre,
                   g_mix_post=v_g_mix_post, g_ffn_pre=v_g_ffn_pre, g_ffn_post=v_g_ffn_post)

    shards = [w_sh[k] if k == "conv_w" else w_sh[k].astype(bf16) for k in big]
    g_in, g_conv, g_out, g_gate, g_up, g_down = _all_gather(shards)
    wfull = g_in.transpose(1, 0, 2).reshape(D_MODEL, IN_COLS)
    n_a = 4 * GDN_W
    wcat = jnp.concatenate([wfull[:, :n_a], wfull[:, n_a + 2 * GDN_HEADS:], wfull[:, n_a:n_a + 2 * GDN_HEADS],
                            jnp.zeros((D_MODEL, LANE - 2 * GDN_HEADS), bf16)], axis=1)
    convw = g_conv.transpose(1, 0, 2).reshape(4, 3 * GDN_W)
    wout = g_out.reshape(D_MODEL, D_MODEL)
    wgate = g_gate.transpose(1, 0, 2).reshape(D_MODEL, D_FF)
    wup = g_up.transpose(1, 0, 2).reshape(D_MODEL, D_FF)
    wdown = g_down.reshape(D_FF, D_MODEL)

    loss_p, grad_x, gbig, gsmall = _local_step(x[0], loss_target[0], wcat, convw, wout, wgate, wup, wdown,
                                               a_log, dt_bias, onorm_g, rel_bias, g_mix_pre, g_mix_post, g_ffn_pre, g_ffn_post)
    gw_in, gw_conv, gw_out, gw_gate, gw_up, gw_down = gbig

    cols = lambda g: g.reshape(g.shape[0], N_DEV, g.shape[1] // N_DEV).transpose(1, 0, 2)
    rows = lambda g: g.reshape(N_DEV, g.shape[0] // N_DEV, g.shape[1])
    blocked = [cols(gw_in).astype(bf16), cols(gw_conv), rows(gw_out).astype(bf16), cols(gw_gate).astype(bf16),
               cols(gw_up).astype(bf16), rows(gw_down).astype(bf16)]
    recv = _grad_exchange(blocked, [_pack_small(gsmall)])

    outs = {}
    for k, r in zip(big, recv[:len(big)]):
        outs[k] = _adamw(r, w_sh[k], m_sh[k], v_sh[k], "adamw_" + k)
    sm = _adamw(recv[-1], _pack_small(w_small), _pack_small(m_small), _pack_small(v_small), "adamw_small")
    sm = [_unpack_small(t) for t in sm]
    for k in SMALL:
        outs[k] = tuple(t[k].reshape(w_small[k].shape) for t in sm)

    loss = lax.psum(loss_p[0, 0], ("x", "y", "c"))
    order = ("w_in", "conv_w", "a_log", "dt_bias", "onorm_g", "rel_bias", "w_out", "g_mix_pre", "g_mix_post", "w_gate",
             "w_up", "w_down", "g_ffn_pre", "g_ffn_post")
    lead = lambda k, t: t[None] if k in big else t
    res = [loss, grad_x[None]]
    for i in range(4):
        res += [lead(k, outs[k][i]) for k in order]
    return tuple(res)
```

```python
import functools
import math

import numpy as np
import jax
import jax.numpy as jnp
from jax import lax
from jax.experimental import pallas as pl
from jax.experimental.pallas import tpu as pltpu

f32 = jnp.float32
bf16 = jnp.bfloat16
SDS = jax.ShapeDtypeStruct

D_MODEL = 1024
GDN_HEADS = 4
GDN_HD = 128
GDN_W = 512
CHUNK = 64
SWA_HEADS = 8
SWA_HD = 64
SWA_W = 512
D_FF = 2816
IN_COLS = 3592
PATTERNS = ((128, 1), (512, 4), (2048, 16))
SWA_BLK = 128
NUM_BUCKETS = 32
MAX_DISTANCE = 2048
RMS_EPS = 1e-6
NEG = -1e30
N_DEV = 8

COL_A = 0
COL_B = 2048
COL_G = 3584
NCOL = 3712
LANE = 128

ADAM_LR, ADAM_B1, ADAM_B2, ADAM_EPS, ADAM_WD, ADAM_STEP = 0.001, 0.9, 0.999, 1e-08, 0.01, 10

VMEM_LIMIT = 56 * 1024 * 1024

HI = lax.Precision.HIGHEST
HIGH = lax.Precision.HIGH


def _cp(n_grid=0, vmem=None):
    kw = {}
    if n_grid:
        kw["dimension_semantics"] = ("arbitrary",) * n_grid
    if vmem:
        kw["vmem_limit_bytes"] = vmem
    return pltpu.CompilerParams(**kw)


def _dot(a, b):
    return jnp.dot(a, b, preferred_element_type=f32)


def _dot_nt(a, b):
    return lax.dot_general(a, b, (((1,), (1,)), ((), ())), preferred_element_type=f32)


def _dot_tn(a, b):
    return lax.dot_general(a, b, (((0,), (0,)), ((), ())), preferred_element_type=f32)


def _dot_hi(a, b):
    return jnp.dot(a, b, precision=HI, preferred_element_type=f32)


def _sigmoid(x):
    return 1.0 / (1.0 + jnp.exp(-x))


def _softplus(x):
    return jnp.maximum(x, 0.0) + jnp.log(1.0 + jnp.exp(-jnp.abs(x)))


def _const_spec(shape):
    nd = len(shape)
    return pl.BlockSpec(shape, lambda *_: (0,) * nd)


def _resident_spec(shape):
    nd = len(shape)
    return pl.BlockSpec(shape, lambda *_: (0,) * nd, pipeline_mode=pl.Buffered(1))


def _t5_bucket_np(dist):
    max_exact = NUM_BUCKETS // 2
    d = np.maximum(dist, 1).astype(np.float32)
    log_b = max_exact + (np.log(d / np.float32(max_exact)) / np.float32(math.log(MAX_DISTANCE / max_exact))
                         * np.float32(NUM_BUCKETS - max_exact)).astype(np.int32)
    return np.where(dist < max_exact, dist, np.minimum(log_b, NUM_BUCKETS - 1)).astype(np.int32)


def _bucket_tables():
    w = SWA_BLK
    qi = np.arange(w)[:, None]
    kj = np.arange(w)[None, :]
    out = np.zeros((len(PATTERNS), 2, w, w), np.int32)
    for p, (_, dil) in enumerate(PATTERNS):
        steps = _t5_bucket_np(np.arange(w + 1) * dil)
        rel_prev = qi + w - kj
        rel_cur = qi - kj
        out[p, 0] = np.where(rel_prev <= w, steps[np.clip(rel_prev, 0, w)], -1)
        out[p, 1] = np.where(rel_cur >= 0, steps[np.clip(rel_cur, 0, w)], -1)
    return out


def _bias_tables(rel_bias, bk):
    def body(rb_ref, bk_ref, o_ref):
        b_idx = bk_ref[0, 0]
        for h in range(SWA_HEADS):
            def lp(b, acc):
                return jnp.where(b_idx == b, rb_ref[b, h], acc)
            o_ref[0, h, 0] = lax.fori_loop(0, NUM_BUCKETS, lp, jnp.full((SWA_BLK, SWA_BLK), NEG, f32))

    return pl.pallas_call(
        body, name="bias_tables", grid=(3, 2),
        in_specs=[pl.BlockSpec(memory_space=pltpu.SMEM),
                  pl.BlockSpec((1, 1, SWA_BLK, SWA_BLK), lambda p, t: (p, t, 0, 0))],
        out_specs=pl.BlockSpec((1, SWA_HEADS, 1, SWA_BLK, SWA_BLK), lambda p, t: (p, 0, t, 0, 0)),
        out_shape=SDS((3, SWA_HEADS, 2, SWA_BLK, SWA_BLK), f32),
        compiler_params=_cp(2),
    )(rel_bias, bk)


def _rel_bias_grad(dsb, bk, bk_np):
    present = [[sorted(set(int(v) for v in np.unique(bk_np[p, t]) if v >= 0)) for t in range(2)] for p in range(3)]

    def body(ds_ref, bk_ref, o_ref):
        row = lax.broadcasted_iota(jnp.int32, (NUM_BUCKETS, LANE), 0)
        col = lax.broadcasted_iota(jnp.int32, (NUM_BUCKETS, SWA_HEADS), 1)
        out = jnp.zeros((NUM_BUCKETS, SWA_HEADS), f32)
        for hp in range(4):
            for hh in range(2):
                acc = jnp.zeros((NUM_BUCKETS, LANE), f32)
                for p in range(3):
                    for t in range(2):
                        tile = ds_ref[hp, p, hh, t]
                        b_idx = bk_ref[p, t]
                        for b in present[p][t]:
                            part = jnp.sum(jnp.where(b_idx == b, tile, 0.0), axis=0, keepdims=True)
                            acc = acc + jnp.where(row == b, part, 0.0)
                tot = jnp.sum(acc, axis=1, keepdims=True)
                out = out + jnp.where(col == 2 * hp + hh, tot, 0.0)
        o_ref[...] = out

    return pl.pallas_call(body, name="rel_bias_grad", out_shape=SDS((NUM_BUCKETS, SWA_HEADS), f32),
                          compiler_params=_cp(0, 32 * 1024 * 1024))(dsb, bk)


def _proj_fwd(x, g_pre, wcat, S):
    TS = 256

    def body(x_ref, g_ref, w_ref, o_ref):
        xv = x_ref[...]
        r = lax.rsqrt(jnp.mean(xv * xv, axis=-1, keepdims=True) + RMS_EPS)
        h = (xv * r * g_ref[...]).astype(bf16)
        o_ref[...] = _dot(h, w_ref[...])

    return pl.pallas_call(
        body, name="proj_fwd", grid=(S // TS,),
        in_specs=[pl.BlockSpec((TS, D_MODEL), lambda i: (i, 0)), _const_spec((1, D_MODEL)),
                  _resident_spec((D_MODEL, NCOL))],
        out_specs=pl.BlockSpec((TS, NCOL), lambda i: (i, 0)),
        out_shape=SDS((S, NCOL), f32),
        compiler_params=_cp(1, VMEM_LIMIT),
    )(x, g_pre, wcat)


CONV_RT = 256
HALO = 8


CONV_NC = CONV_RT // CHUNK


def _gdn_prep(proj, conv_w, S):
    def body(p_ref, cw_ref, o_ref, xs_ref):
        t = pl.program_id(0)
        xs_ref[pl.ds(0, HALO), :] = jnp.zeros((HALO, LANE), f32)
        xs_ref[pl.ds(HALO, S), :] = p_ref[...]
        w = cw_ref[...]
        is_qk = t < 2
        scale = jnp.where(t == 0, GDN_HD ** -0.5, 1.0).astype(f32)

        def lp(c, carry):
            st = pl.multiple_of(c * CONV_RT, CONV_RT)
            pre = xs_ref[pl.ds(st + HALO - 3, CONV_RT), :] * w[0:1, :]
            for i in range(1, 4):
                pre = pre + xs_ref[pl.ds(st + HALO - 3 + i, CONV_RT), :] * w[i:i + 1, :]
            s = pre * _sigmoid(pre)
            nrm = s * lax.rsqrt(jnp.sum(s * s, axis=-1, keepdims=True) + 1e-6) * scale
            out = jnp.where(is_qk, nrm, s)
            for i in range(CONV_NC):
                o_ref[0, c * CONV_NC + i, 0] = out[i * CHUNK:(i + 1) * CHUNK]
            return carry

        lax.fori_loop(0, S // CONV_RT, lp, 0)

    return pl.pallas_call(
        body, name="gdn_prep", grid=(3, GDN_HEADS),
        in_specs=[pl.BlockSpec((S, LANE), lambda t, h: (0, t * GDN_HEADS + h)),
                  pl.BlockSpec((4, LANE), lambda t, h: (0, t * GDN_HEADS + h))],
        out_specs=pl.BlockSpec((1, S // CHUNK, 1, CHUNK, GDN_HD), lambda t, h: (t, 0, h, 0, 0)),
        out_shape=SDS((3, S // CHUNK, GDN_HEADS, CHUNK, GDN_HD), f32),
        scratch_shapes=[pltpu.VMEM((S + HALO, LANE), f32)],
        compiler_params=_cp(2, VMEM_LIMIT),
    )(proj, conv_w)


def _gdn_prep_bwd(proj, conv_w, dqkv, S):
    def body(p_ref, cw_ref, d_ref, dx_ref, dw_ref, xs_ref, dp_ref):
        t = pl.program_id(0)
        xs_ref[pl.ds(0, HALO), :] = jnp.zeros((HALO, LANE), f32)
        xs_ref[pl.ds(HALO, S), :] = p_ref[...]
        dp_ref[pl.ds(S, HALO), :] = jnp.zeros((HALO, LANE), f32)
        w = cw_ref[...]
        is_qk = t < 2
        scale = jnp.where(t == 0, GDN_HD ** -0.5, 1.0).astype(f32)

        def lp1(c, dw):
            st = pl.multiple_of(c * CONV_RT, CONV_RT)
            taps = [xs_ref[pl.ds(st + HALO - 3 + i, CONV_RT), :] for i in range(4)]
            pre = taps[0] * w[0:1, :]
            for i in range(1, 4):
                pre = pre + taps[i] * w[i:i + 1, :]
            sg = _sigmoid(pre)
            s = pre * sg
            d_out = jnp.concatenate([d_ref[0, c * CONV_NC + i, 0] for i in range(CONV_NC)], axis=0)
            rn = lax.rsqrt(jnp.sum(s * s, axis=-1, keepdims=True) + 1e-6)
            n = s * rn
            dn = d_out * scale
            ds_qk = rn * (dn - n * jnp.sum(dn * n, axis=-1, keepdims=True))
            ds = jnp.where(is_qk, ds_qk, d_out)
            dpre = ds * (sg * (1.0 + pre * (1.0 - sg)))
            dp_ref[pl.ds(st, CONV_RT), :] = dpre
            return tuple(dw[i] + jnp.sum(dpre * taps[i], axis=0, keepdims=True) for i in range(4))

        z = jnp.zeros((1, LANE), f32)
        dw = lax.fori_loop(0, S // CONV_RT, lp1, (z, z, z, z))
        for i in range(4):
            dw_ref[pl.ds(i, 1), :] = dw[i]

        def lp2(c, carry):
            st = pl.multiple_of(c * CONV_RT, CONV_RT)
            dx = dp_ref[pl.ds(st, CONV_RT), :] * w[3:4, :]
            for i in range(3):
                dx = dx + dp_ref[pl.ds(st + 3 - i, CONV_RT), :] * w[i:i + 1, :]
            dx_ref[pl.ds(st, CONV_RT), :] = dx.astype(bf16)
            return carry

        lax.fori_loop(0, S // CONV_RT, lp2, 0)

    col = lambda rows: pl.BlockSpec((rows, LANE), lambda t, h: (0, t * GDN_HEADS + h))
    return pl.pallas_call(
        body, name="gdn_prep_bwd", grid=(3, GDN_HEADS),
        in_specs=[col(S), col(4), pl.BlockSpec((1, S // CHUNK, 1, CHUNK, GDN_HD), lambda t, h: (t, 0, h, 0, 0))],
        out_specs=[col(S), col(4)],
        out_shape=[SDS((S, 3 * GDN_W), bf16), SDS((4, 3 * GDN_W), f32)],
        scratch_shapes=[pltpu.VMEM((S + HALO, LANE), f32), pltpu.VMEM((S + HALO, LANE), f32)],
        compiler_params=_cp(2, VMEM_LIMIT),
    )(proj, conv_w, dqkv)


def _bdot(a, b, prec=None):
    return lax.dot_general(a, b, (((2,), (1,)), ((0,), (0,))), precision=prec, preferred_element_type=f32)


def _bdot_nt(a, b, prec=None):
    return lax.dot_general(a, b, (((2,), (2,)), ((0,), (0,))), precision=prec, preferred_element_type=f32)


def _bdot_tn(a, b, prec=None):
    return lax.dot_general(a, b, (((1,), (1,)), ((0,), (0,))), precision=prec, preferred_element_type=f32)


@jax.custom_vjp
def _tri_inv_saved(a, t):
    return t


def _tri_inv_saved_fwd(a, t):
    return t, t


def _tri_inv_saved_bwd(t, dt):
    return -_bdot_tn(t, _bdot_nt(dt, t, HIGH), HIGH), jnp.zeros_like(t)


_tri_inv_saved.defvjp(_tri_inv_saved_fwd, _tri_inv_saved_bwd)


def _gdn_intra(q, k, v, bl, al, a_log, dt_bias, t_saved=None):
    nb = q.shape[0]
    c = CHUNK
    ii = lax.broadcasted_iota(jnp.int32, (c, c), 0)
    jj = lax.broadcasted_iota(jnp.int32, (c, c), 1)
    eye = ii == jj
    tril = ii >= jj
    strict = ii > jj
    ones = jnp.ones((nb, c, c), f32)
    eye_f = eye.astype(f32)

    beta = _sigmoid(bl)
    g = -jnp.exp(a_log) * _softplus(al + dt_bias)
    g_row = _bdot(ones, jnp.where(eye, g, 0.0), HI)
    gc = jnp.sum(jnp.where(tril, g_row, 0.0), axis=2, keepdims=True)
    gc_row = _bdot(ones, jnp.where(eye, gc, 0.0), HI)
    decay = jnp.where(tril, jnp.exp(jnp.where(tril, gc - gc_row, 0.0)), 0.0)
    last = lax.broadcasted_iota(jnp.int32, (c, 1), 0) == c - 1
    gc_last = jnp.sum(jnp.where(last, gc, 0.0), axis=1, keepdims=True)
    e_gc = jnp.exp(gc)

    kb = k * beta
    k16 = k.astype(bf16)
    a = jnp.where(strict, _bdot_nt(kb.astype(bf16), k16) * decay, 0.0)
    if t_saved is None:
        xp = -a
        t_inv = eye_f + xp
        for _ in range(5):
            xp = _bdot(xp, xp, HIGH)
            t_inv = _bdot(t_inv, eye_f + xp, HIGH)
    else:
        t_inv = _tri_inv_saved(a, t_saved)
    t16 = t_inv.astype(bf16)
    u = _bdot(t16, (v * beta).astype(bf16))
    w = _bdot(t16, (kb * e_gc).astype(bf16))
    attn = jnp.where(tril, _bdot_nt(q.astype(bf16), k16) * decay, 0.0)
    gam = jnp.broadcast_to(jnp.exp(gc_last), (nb, 1, GDN_HD))
    return u, w, attn, q * e_gc, k * jnp.exp(gc_last - gc), gam, t_inv


GDN_TB = 256
GDN_NC = GDN_TB // CHUNK
GDN_NU = GDN_NC * GDN_HEADS


def _gdn_unit_inputs(qkv_ref, pg_ref, al_ref, db_ref):
    pg = pg_ref[...]
    bl = jnp.stack([pg[cl * CHUNK:(cl + 1) * CHUNK, h:h + 1] for cl in range(GDN_NC) for h in range(GDN_HEADS)])
    al = jnp.stack([pg[cl * CHUNK:(cl + 1) * CHUNK, GDN_HEADS + h:GDN_HEADS + h + 1]
                    for cl in range(GDN_NC) for h in range(GDN_HEADS)])
    a_log = jnp.stack([jnp.full((1, 1), al_ref[0, h], f32) for _ in range(GDN_NC) for h in range(GDN_HEADS)])
    dt_b = jnp.stack([jnp.full((1, 1), db_ref[0, h], f32) for _ in range(GDN_NC) for h in range(GDN_HEADS)])
    return qkv_ref[0], qkv_ref[1], qkv_ref[2], bl, al, a_log, dt_b


def _unit_spec(*tail):
    nd = len(tail)
    return pl.BlockSpec((GDN_NU,) + tail, lambda i: (i,) + (0,) * nd)


def _gdn_intra_shapes(S):
    nu = S // CHUNK * GDN_HEADS
    row = SDS((nu, CHUNK, GDN_HD), f32)
    return [row, row, SDS((nu, CHUNK, CHUNK), f32), row, row, SDS((nu, 1, GDN_HD), f32)]


_GDN_INTRA_SPECS = lambda: [_unit_spec(CHUNK, GDN_HD), _unit_spec(CHUNK, GDN_HD), _unit_spec(CHUNK, CHUNK),
                            _unit_spec(CHUNK, GDN_HD), _unit_spec(CHUNK, GDN_HD), _unit_spec(1, GDN_HD)]


def _gdn_intra_fwd(qkv_u, proj, a_log, dt_bias, S):
    def body(qkv_ref, pg_ref, al_ref, db_ref, *outs):
        res = _gdn_intra(*_gdn_unit_inputs(qkv_ref, pg_ref, al_ref, db_ref))
        for o_ref, r in zip(outs, res):
            o_ref[...] = r

    nu = S // CHUNK * GDN_HEADS
    *intra, t_inv = pl.pallas_call(
        body, name="gdn_intra_fwd", grid=(S // GDN_TB,),
        in_specs=[pl.BlockSpec((3, GDN_NU, CHUNK, GDN_HD), lambda i: (0, i, 0, 0)),
                  pl.BlockSpec((GDN_TB, LANE), lambda i: (i, COL_G // LANE)),
                  pl.BlockSpec(memory_space=pltpu.SMEM), pl.BlockSpec(memory_space=pltpu.SMEM)],
        out_specs=_GDN_INTRA_SPECS() + [_unit_spec(CHUNK, CHUNK)],
        out_shape=_gdn_intra_shapes(S) + [SDS((nu, CHUNK, CHUNK), f32)],
        compiler_params=_cp(1, VMEM_LIMIT),
    )(qkv_u, proj, a_log, dt_bias)
    return intra, t_inv


def _gdn_intra_bwd(qkv_u, proj, a_log, dt_bias, t_inv, cots, S):
    def body(qkv_ref, pg_ref, al_ref, db_ref, t_ref, du_ref, dw_ref, da_ref, dqd_ref, dkd_ref, dgm_ref,
             dqkv_ref, dpg_ref, dal_ref, ddb_ref):
        @pl.when(pl.program_id(0) == 0)
        def _():
            dal_ref[...] = jnp.zeros_like(dal_ref)
            ddb_ref[...] = jnp.zeros_like(ddb_ref)

        t_saved = t_ref[...]
        _, vjp = jax.vjp(lambda *a: _gdn_intra(*a, t_saved=t_saved)[:6], *_gdn_unit_inputs(qkv_ref, pg_ref, al_ref, db_ref))
        dq, dk, dv, dbl, dal, da, ddb = vjp((du_ref[...], dw_ref[...], da_ref[...], dqd_ref[...], dkd_ref[...], dgm_ref[...]))
        dqkv_ref[0] = dq
        dqkv_ref[1] = dk
        dqkv_ref[2] = dv
        lane = lax.broadcasted_iota(jnp.int32, (CHUNK, LANE), 1)
        lane1 = lax.broadcasted_iota(jnp.int32, (1, LANE), 1)
        da_tot = jnp.zeros((1, LANE), f32)
        ddb_tot = jnp.zeros((1, LANE), f32)
        for cl in range(GDN_NC):
            dpg = jnp.zeros((CHUNK, LANE), f32)
            for h in range(GDN_HEADS):
                b = cl * GDN_HEADS + h
                dpg = dpg + jnp.where(lane == h, dbl[b], 0.0) + jnp.where(lane == GDN_HEADS + h, dal[b], 0.0)
                da_tot = da_tot + jnp.where(lane1 == h, da[b], 0.0)
                ddb_tot = ddb_tot + jnp.where(lane1 == h, ddb[b], 0.0)
            dpg_ref[cl * CHUNK:(cl + 1) * CHUNK, :] = dpg.astype(bf16)
        dal_ref[...] += da_tot
        ddb_ref[...] += ddb_tot

    acc = _const_spec((1, LANE))
    nu = S // CHUNK * GDN_HEADS
    return pl.pallas_call(
        body, name="gdn_intra_bwd", grid=(S // GDN_TB,),
        in_specs=[pl.BlockSpec((3, GDN_NU, CHUNK, GDN_HD), lambda i: (0, i, 0, 0)),
                  pl.BlockSpec((GDN_TB, LANE), lambda i: (i, COL_G // LANE)),
                  pl.BlockSpec(memory_space=pltpu.SMEM), pl.BlockSpec(memory_space=pltpu.SMEM),
                  _unit_spec(CHUNK, CHUNK)] + _GDN_INTRA_SPECS(),
        out_specs=[pl.BlockSpec((3, GDN_NU, CHUNK, GDN_HD), lambda i: (0, i, 0, 0)),
                   pl.BlockSpec((GDN_TB, LANE), lambda i: (i, 0)), acc, acc],
        out_shape=[SDS((3, nu, CHUNK, GDN_HD), f32), SDS((S, LANE), bf16), SDS((1, LANE), f32), SDS((1, LANE), f32)],
        compiler_params=_cp(1, VMEM_LIMIT),
    )(qkv_u, proj, a_log, dt_bias, t_inv, *cots)


def _gdn_scan_fwd(intra, proj, onorm_g, S):
    def body(u_ref, w_ref, at_ref, qd_ref, kd_ref, gm_ref, gate_ref, og_ref, out_ref, st_ref, s_scr):
        @pl.when(pl.program_id(0) == 0)
        def _():
            s_scr[...] = jnp.zeros_like(s_scr)

        og = og_ref[...]
        s = s_scr[...]
        for cl in range(GDN_NC):
            us = slice(cl * GDN_HEADS, (cl + 1) * GDN_HEADS)
            rows = slice(cl * CHUNK, (cl + 1) * CHUNK)
            st_ref[us] = s
            s16 = s.astype(bf16)
            vn = u_ref[us] - _bdot(w_ref[us].astype(bf16), s16)
            vn16 = vn.astype(bf16)
            o = _bdot(qd_ref[us].astype(bf16), s16) + _bdot(at_ref[us].astype(bf16), vn16)
            s = s * gm_ref[us] + _bdot_tn(kd_ref[us].astype(bf16), vn16)
            for h in range(GDN_HEADS):
                oh = o[h]
                gt = gate_ref[rows, h * GDN_HD:(h + 1) * GDN_HD]
                on = oh * lax.rsqrt(jnp.mean(oh * oh, axis=-1, keepdims=True) + RMS_EPS) * og
                out_ref[rows, h * GDN_HD:(h + 1) * GDN_HD] = on * (gt * _sigmoid(gt))
        s_scr[...] = s

    nu = S // CHUNK * GDN_HEADS
    return pl.pallas_call(
        body, name="gdn_scan_fwd", grid=(S // GDN_TB,),
        in_specs=_GDN_INTRA_SPECS() + [pl.BlockSpec((GDN_TB, GDN_W), lambda i: (i, 3)), _const_spec((1, GDN_HD))],
        out_specs=[pl.BlockSpec((GDN_TB, GDN_W), lambda i: (i, 0)), _unit_spec(GDN_HD, GDN_HD)],
        out_shape=[SDS((S, GDN_W), f32), SDS((nu, GDN_HD, GDN_HD), f32)],
        scratch_shapes=[pltpu.VMEM((GDN_HEADS, GDN_HD, GDN_HD), f32)],
        compiler_params=_cp(1, VMEM_LIMIT),
    )(*intra, proj, onorm_g)


def _gdn_scan_bwd(intra, states, proj, d_oab, onorm_g, S):
    n_steps = S // GDN_TB

    def body(u_ref, w_ref, at_ref, qd_ref, kd_ref, gm_ref, st_ref, gate_ref, do_ref, og_ref,
             du_ref, dw_ref, dat_ref, dqd_ref, dkd_ref, dgm_ref, dgate_ref, dog_ref, ds_scr):
        @pl.when(pl.program_id(0) == 0)
        def _():
            ds_scr[...] = jnp.zeros_like(ds_scr)
            dog_ref[...] = jnp.zeros_like(dog_ref)

        og = og_ref[...]
        ii = lax.broadcasted_iota(jnp.int32, (CHUNK, CHUNK), 0)
        jj = lax.broadcasted_iota(jnp.int32, (CHUNK, CHUNK), 1)
        tril = ii >= jj
        ds = ds_scr[...]
        dog = jnp.zeros((1, GDN_HD), f32)
        for cl in reversed(range(GDN_NC)):
            us = slice(cl * GDN_HEADS, (cl + 1) * GDN_HEADS)
            rows = slice(cl * CHUNK, (cl + 1) * CHUNK)
            s0 = st_ref[us]
            s016 = s0.astype(bf16)
            w16 = w_ref[us].astype(bf16)
            qd16 = qd_ref[us].astype(bf16)
            kd16 = kd_ref[us].astype(bf16)
            at16 = at_ref[us].astype(bf16)
            vn = u_ref[us] - _bdot(w16, s016)
            vn16 = vn.astype(bf16)
            o = _bdot(qd16, s016) + _bdot(at16, vn16)
            do_h = []
            for h in range(GDN_HEADS):
                oh = o[h]
                lanes = slice(h * GDN_HD, (h + 1) * GDN_HD)
                gt = gate_ref[rows, lanes]
                d_out = do_ref[rows, lanes]
                r = lax.rsqrt(jnp.mean(oh * oh, axis=-1, keepdims=True) + RMS_EPS)
                n = oh * r
                sg = _sigmoid(gt)
                silu = gt * sg
                dog = dog + jnp.sum(d_out * n * silu, axis=0, keepdims=True)
                dgate_ref[rows, lanes] = (d_out * n * og * (sg * (1.0 + gt * (1.0 - sg)))).astype(bf16)
                dn = d_out * og * silu
                do_h.append(r * (dn - n * jnp.mean(dn * n, axis=-1, keepdims=True)))
            do16 = jnp.stack(do_h).astype(bf16)
            ds16 = ds.astype(bf16)
            dvn = _bdot_tn(at16, do16) + _bdot(kd16, ds16)
            dvn16 = dvn.astype(bf16)
            du_ref[us] = dvn
            dw_ref[us] = -_bdot_nt(dvn16, s016)
            dat_ref[us] = jnp.where(tril, _bdot_nt(do16, vn16), 0.0)
            dqd_ref[us] = _bdot_nt(do16, s016)
            dkd_ref[us] = _bdot_nt(vn16, ds16)
            dgm_ref[us] = jnp.sum(s0 * ds, axis=1, keepdims=True)
            ds = _bdot_tn(qd16, do16) + ds * gm_ref[us] - _bdot_tn(w16, dvn16)
        ds_scr[...] = ds
        dog_ref[...] += dog

    def unit(*tail):
        nd = len(tail)
        return pl.BlockSpec((GDN_NU,) + tail, lambda i: (n_steps - 1 - i,) + (0,) * nd)

    intra_specs = [unit(CHUNK, GDN_HD), unit(CHUNK, GDN_HD), unit(CHUNK, CHUNK), unit(CHUNK, GDN_HD),
                   unit(CHUNK, GDN_HD), unit(1, GDN_HD)]
    tok = lambda c: pl.BlockSpec((GDN_TB, GDN_W), lambda i: (n_steps - 1 - i, c))
    return pl.pallas_call(
        body, name="gdn_scan_bwd", grid=(n_steps,),
        in_specs=intra_specs + [unit(GDN_HD, GDN_HD), tok(3), tok(0), _const_spec((1, GDN_HD))],
        out_specs=intra_specs + [tok(0), _const_spec((1, GDN_HD))],
        out_shape=_gdn_intra_shapes(S) + [SDS((S, GDN_W), bf16), SDS((1, GDN_HD), f32)],
        scratch_shapes=[pltpu.VMEM((GDN_HEADS, GDN_HD, GDN_HD), f32)],
        compiler_params=_cp(1, VMEM_LIMIT),
    )(*intra, states, proj, d_oab, onorm_g)


SWA_UNROLL = 4


def _swa_tiles(q_ref, k_ref, v_ref, it, d, nb_log2, S):
    nb = 1 << nb_log2
    r = lax.shift_right_logical(it, nb_log2)
    blk = lax.bitwise_and(it, nb - 1)
    qs = blk * (SWA_BLK * d) + r
    ps = jnp.maximum(blk - 1, 0) * (SWA_BLK * d) + r
    if d > 1:
        rows_c, rows_p = pl.ds(qs, SWA_BLK, stride=d), pl.ds(ps, SWA_BLK, stride=d)
    else:
        rows_c, rows_p = pl.ds(pl.multiple_of(qs, SWA_BLK), SWA_BLK), pl.ds(pl.multiple_of(ps, SWA_BLK), SWA_BLK)
    return rows_c, rows_p, blk > 0


def _swa_fwd(proj, bt, S):
    scale = SWA_HD ** -0.5

    def body(q_ref, k_ref, v_ref, bt_ref, o_ref, lse_ref, m_scr, l_scr, acc_scr):
        lane = lax.broadcasted_iota(jnp.int32, (SWA_BLK, LANE), 1)
        h0 = lane < SWA_HD
        m_scr[...] = jnp.full((S, LANE), NEG, f32)
        l_scr[...] = jnp.zeros((S, LANE), f32)
        acc_scr[...] = jnp.zeros((S, LANE), f32)
        for p, (_, d) in enumerate(PATTERNS):
            nb_log2 = int(math.log2(S // d // SWA_BLK))

            def tile(it, p=p, d=d, nb_log2=nb_log2):
                rows_c, rows_p, has_prev = _swa_tiles(q_ref, k_ref, v_ref, it, d, nb_log2, S)
                q = q_ref[rows_c, :]
                kc = k_ref[rows_c, :].astype(bf16)
                vc = v_ref[rows_c, :].astype(bf16)
                kp = k_ref[rows_p, :].astype(bf16)
                vp = v_ref[rows_p, :].astype(bf16)
                m_old = m_scr[rows_c, :]
                l_old = l_scr[rows_c, :]
                acc_old = acc_scr[rows_c, :]
                res = []
                for h in range(2):
                    mh = h0 if h == 0 else jnp.logical_not(h0)
                    qh = jnp.where(mh, q, 0.0).astype(bf16)
                    s_c = _dot_nt(qh, kc) * scale + bt_ref[p, h, 1]
                    s_p = jnp.where(has_prev, _dot_nt(qh, kp) * scale + bt_ref[p, h, 0], NEG)
                    mo = m_old[:, h * SWA_HD:h * SWA_HD + 1]
                    lo = l_old[:, h * SWA_HD:h * SWA_HD + 1]
                    mn = jnp.maximum(mo, jnp.maximum(jnp.max(s_c, axis=1, keepdims=True), jnp.max(s_p, axis=1, keepdims=True)))
                    alpha = jnp.exp(mo - mn)
                    p_c = jnp.exp(s_c - mn)
                    p_p = jnp.exp(s_p - mn)
                    ln = alpha * lo + jnp.sum(p_c, axis=1, keepdims=True) + jnp.sum(p_p, axis=1, keepdims=True)
                    pv = _dot(p_c.astype(bf16), vc) + _dot(p_p.astype(bf16), vp)
                    res.append((mn, ln, alpha, pv))
                return (rows_c, jnp.where(h0, res[0][0], res[1][0]), jnp.where(h0, res[0][1], res[1][1]),
                        acc_old * jnp.where(h0, res[0][2], res[1][2]) + jnp.where(h0, res[0][3], res[1][3]))

            def lp(i, carry, tile=tile):
                done = [tile(i * SWA_UNROLL + u) for u in range(SWA_UNROLL)]
                for rows_c, m_new, l_new, acc_new in done:
                    m_scr[rows_c, :] = m_new
                    l_scr[rows_c, :] = l_new
                    acc_scr[rows_c, :] = acc_new
                return carry

            lax.fori_loop(0, S // SWA_BLK // SWA_UNROLL, lp, 0)
        l_all = l_scr[...]
        o_ref[...] = acc_scr[...] / l_all
        lse_ref[...] = m_scr[...] + jnp.log(l_all)

    qb = COL_B // LANE
    return pl.pallas_call(
        body, name="swa_fwd", grid=(4,),
        in_specs=[pl.BlockSpec((S, LANE), lambda hp: (0, qb + hp)), pl.BlockSpec((S, LANE), lambda hp: (0, qb + 4 + hp)),
                  pl.BlockSpec((S, LANE), lambda hp: (0, qb + 8 + hp)),
                  pl.BlockSpec((3, 2, 2, SWA_BLK, SWA_BLK), lambda hp: (0, hp, 0, 0, 0))],
        out_specs=[pl.BlockSpec((S, LANE), lambda hp: (0, hp)), pl.BlockSpec((S, LANE), lambda hp: (0, hp))],
        out_shape=[SDS((S, SWA_W), f32), SDS((S, SWA_W), f32)],
        scratch_shapes=[pltpu.VMEM((S, LANE), f32)] * 3,
        compiler_params=_cp(1, VMEM_LIMIT),
    )(proj, proj, proj, bt)


def _swa_bwd(proj, bt, ob, lse, d_oab, S):
    scale = SWA_HD ** -0.5

    def body(q_ref, k_ref, v_ref, bt_ref, o_ref, lse_ref, do_ref, dq_ref, dk_ref, dv_ref, dsb_ref, dq_scr, dk_scr, dv_scr):
        lane = lax.broadcasted_iota(jnp.int32, (SWA_BLK, LANE), 1)
        h0 = lane < SWA_HD
        dq_scr[...] = jnp.zeros((S, LANE), f32)
        dk_scr[...] = jnp.zeros((S, LANE), f32)
        dv_scr[...] = jnp.zeros((S, LANE), f32)
        dsb_ref[...] = jnp.zeros_like(dsb_ref)
        for p, (_, d) in enumerate(PATTERNS):
            nb_log2 = int(math.log2(S // d // SWA_BLK))

            def tile(it, p=p, d=d, nb_log2=nb_log2):
                rows_c, rows_p, has_prev = _swa_tiles(q_ref, k_ref, v_ref, it, d, nb_log2, S)
                q = q_ref[rows_c, :]
                kc = k_ref[rows_c, :].astype(bf16)
                vc = v_ref[rows_c, :].astype(bf16)
                kp = k_ref[rows_p, :].astype(bf16)
                vp = v_ref[rows_p, :].astype(bf16)
                do = do_ref[rows_c, :]
                lse_t = lse_ref[rows_c, :]
                dlt = do * o_ref[rows_c, :]
                dq_t = jnp.zeros((SWA_BLK, LANE), f32)
                dkc_t = jnp.zeros((SWA_BLK, LANE), f32)
                dkp_t = jnp.zeros((SWA_BLK, LANE), f32)
                dvc_t = jnp.zeros((SWA_BLK, LANE), f32)
                dvp_t = jnp.zeros((SWA_BLK, LANE), f32)
                ds_all = []
                for h in range(2):
                    mh = h0 if h == 0 else jnp.logical_not(h0)
                    qh = jnp.where(mh, q, 0.0).astype(bf16)
                    doh = jnp.where(mh, do, 0.0).astype(bf16)
                    delta = jnp.sum(jnp.where(mh, dlt, 0.0), axis=1, keepdims=True)
                    lse_h = lse_t[:, h * SWA_HD:h * SWA_HD + 1]
                    s_c = _dot_nt(qh, kc) * scale + bt_ref[p, h, 1]
                    s_p = jnp.where(has_prev, _dot_nt(qh, kp) * scale + bt_ref[p, h, 0], NEG)
                    p_c = jnp.exp(s_c - lse_h)
                    p_p = jnp.exp(s_p - lse_h)
                    ds_c = p_c * (_dot_nt(doh, vc) - delta)
                    ds_p = p_p * (_dot_nt(doh, vp) - delta)
                    ds_all.append((ds_p, ds_c))
                    dsc16 = ds_c.astype(bf16)
                    dsp16 = ds_p.astype(bf16)
                    dq_t = dq_t + jnp.where(mh, (_dot(dsc16, kc) + _dot(dsp16, kp)) * scale, 0.0)
                    dkc_t = dkc_t + _dot_tn(dsc16, qh) * scale
                    dkp_t = dkp_t + _dot_tn(dsp16, qh) * scale
                    dvc_t = dvc_t + _dot_tn(p_c.astype(bf16), doh)
                    dvp_t = dvp_t + _dot_tn(p_p.astype(bf16), doh)
                return rows_c, rows_p, dq_t, dkc_t, dkp_t, dvc_t, dvp_t, ds_all

            def lp(i, carry, tile=tile, p=p):
                done = [tile(i * SWA_UNROLL + u) for u in range(SWA_UNROLL)]
                for h in range(2):
                    for t in range(2):
                        tot = done[0][7][h][t]
                        for dn in done[1:]:
                            tot = tot + dn[7][h][t]
                        dsb_ref[0, p, h, t] += tot
                for rows_c, rows_p, dq_t, dkc_t, dkp_t, dvc_t, dvp_t, _ in done:
                    dq_scr[rows_c, :] = dq_scr[rows_c, :] + dq_t
                    dk_scr[rows_c, :] = dk_scr[rows_c, :] + dkc_t
                    dv_scr[rows_c, :] = dv_scr[rows_c, :] + dvc_t
                    dk_scr[rows_p, :] = dk_scr[rows_p, :] + dkp_t
                    dv_scr[rows_p, :] = dv_scr[rows_p, :] + dvp_t
                return carry

            lax.fori_loop(0, S // SWA_BLK // SWA_UNROLL, lp, 0)
        dq_ref[...] = dq_scr[...].astype(bf16)
        dk_ref[...] = dk_scr[...].astype(bf16)
        dv_ref[...] = dv_scr[...].astype(bf16)

    qb = COL_B // LANE
    col = lambda c: pl.BlockSpec((S, LANE), lambda hp, c=c: (0, c + hp))
    return pl.pallas_call(
        body, name="swa_bwd", grid=(4,),
        in_specs=[col(qb), col(qb + 4), col(qb + 8),
                  pl.BlockSpec((3, 2, 2, SWA_BLK, SWA_BLK), lambda hp: (0, hp, 0, 0, 0)),
                  col(0), col(0), col(4)],
        out_specs=[col(0), col(0), col(0),
                   pl.BlockSpec((1, 3, 2, 2, SWA_BLK, SWA_BLK), lambda hp: (hp, 0, 0, 0, 0, 0))],
        out_shape=[SDS((S, SWA_W), bf16)] * 3 + [SDS((4, 3, 2, 2, SWA_BLK, SWA_BLK), f32)],
        scratch_shapes=[pltpu.VMEM((S, LANE), f32)] * 3,
        compiler_params=_cp(1, VMEM_LIMIT),
    )(proj, proj, proj, bt, ob, lse, d_oab)


def _mix_fwd(oa, ob, w_out, x, g_post, S):
    TS = 512

    def body(oa_ref, ob_ref, w_ref, x_ref, g_ref, mix_ref, x1_ref):
        mix = _dot(oa_ref[...].astype(bf16), w_ref[0:GDN_W, :]) + _dot(ob_ref[...].astype(bf16), w_ref[GDN_W:D_MODEL, :])
        r = lax.rsqrt(jnp.mean(mix * mix, axis=-1, keepdims=True) + RMS_EPS)
        mix_ref[...] = mix
        x1_ref[...] = x_ref[...] + mix * r * g_ref[...]

    row = lambda w: pl.BlockSpec((TS, w), lambda i: (i, 0))
    return pl.pallas_call(
        body, name="mix_fwd", grid=(S // TS,),
        in_specs=[row(GDN_W), row(SWA_W), _resident_spec((D_MODEL, D_MODEL)), row(D_MODEL), _const_spec((1, D_MODEL))],
        out_specs=[row(D_MODEL), row(D_MODEL)],
        out_shape=[SDS((S, D_MODEL), f32), SDS((S, D_MODEL), f32)],
        compiler_params=_cp(1, VMEM_LIMIT),
    )(oa, ob, w_out, x, g_post)


def _mix_bwd(dx1, mix, g_post, w_out, S):
    TS = 512

    def body(dx1_ref, mix_ref, g_ref, w_ref, dmix_ref, doab_ref, dg_ref):
        @pl.when(pl.program_id(0) == 0)
        def _():
            dg_ref[...] = jnp.zeros_like(dg_ref)

        mix = mix_ref[...]
        dz = dx1_ref[...]
        r = lax.rsqrt(jnp.mean(mix * mix, axis=-1, keepdims=True) + RMS_EPS)
        n = mix * r
        dg_ref[...] += jnp.sum(dz * n, axis=0, keepdims=True)
        dn = dz * g_ref[...]
        dmix = (r * (dn - n * jnp.mean(dn * n, axis=-1, keepdims=True))).astype(bf16)
        dmix_ref[...] = dmix
        doab_ref[...] = _dot_nt(dmix, w_ref[...])

    row = lambda: pl.BlockSpec((TS, D_MODEL), lambda i: (i, 0))
    return pl.pallas_call(
        body, name="mix_bwd", grid=(S // TS,),
        in_specs=[row(), row(), _const_spec((1, D_MODEL)), _resident_spec((D_MODEL, D_MODEL))],
        out_specs=[row(), row(), _const_spec((1, D_MODEL))],
        out_shape=[SDS((S, D_MODEL), bf16), SDS((S, D_MODEL), f32), SDS((1, D_MODEL), f32)],
        compiler_params=_cp(1, VMEM_LIMIT),
    )(dx1, mix, g_post, w_out)


FFN_TS = 256
FFN_CH = 1408


def _ffn(x1, tgt, g_pre, g_post, wg, wu, wd, S):
    def body(x1_ref, t_ref, gp_ref, gq_ref, wg_ref, wu_ref, wd_ref,
             dx1_ref, h2_ref, act_ref, dgate_ref, dup_ref, df_ref, loss_ref, dgp_ref, dgq_ref, gate_scr, up_scr):
        @pl.when(pl.program_id(0) == 0)
        def _():
            loss_ref[...] = jnp.zeros_like(loss_ref)
            dgp_ref[...] = jnp.zeros_like(dgp_ref)
            dgq_ref[...] = jnp.zeros_like(dgq_ref)

        x1v = x1_ref[...]
        gp = gp_ref[...]
        gq = gq_ref[...]
        r2 = lax.rsqrt(jnp.mean(x1v * x1v, axis=-1, keepdims=True) + RMS_EPS)
        n2 = x1v * r2
        h2 = (n2 * gp).astype(bf16)
        h2_ref[...] = h2
        f = jnp.zeros((FFN_TS, D_MODEL), f32)
        for c in range(D_FF // FFN_CH):
            cs = slice(c * FFN_CH, (c + 1) * FFN_CH)
            gate = _dot(h2, wg_ref[:, cs])
            up = _dot(h2, wu_ref[:, cs])
            gate_scr[:, cs] = gate
            up_scr[:, cs] = up
            act = (gate * _sigmoid(gate) * up).astype(bf16)
            act_ref[:, cs] = act
            f = f + _dot(act, wd_ref[cs, :])
        r3 = lax.rsqrt(jnp.mean(f * f, axis=-1, keepdims=True) + RMS_EPS)
        n3 = f * r3
        err = x1v + n3 * gq - t_ref[...]
        loss_ref[...] += 0.5 * jnp.sum(jnp.mean(err * err, axis=-1, keepdims=True), axis=0, keepdims=True)
        dy = err * (1.0 / D_MODEL)
        dgq_ref[...] += jnp.sum(dy * n3, axis=0, keepdims=True)
        dn3 = dy * gq
        df = (r3 * (dn3 - n3 * jnp.mean(dn3 * n3, axis=-1, keepdims=True))).astype(bf16)
        df_ref[...] = df
        dh2 = jnp.zeros((FFN_TS, D_MODEL), f32)
        for c in range(D_FF // FFN_CH):
            cs = slice(c * FFN_CH, (c + 1) * FFN_CH)
            gate = gate_scr[:, cs]
            up = up_scr[:, cs]
            dact = _dot_nt(df, wd_ref[cs, :])
            sg = _sigmoid(gate)
            dup = (dact * gate * sg).astype(bf16)
            dgate = (dact * up * (sg * (1.0 + gate * (1.0 - sg)))).astype(bf16)
            dup_ref[:, cs] = dup
            dgate_ref[:, cs] = dgate
            dh2 = dh2 + _dot_nt(dgate, wg_ref[:, cs]) + _dot_nt(dup, wu_ref[:, cs])
        dgp_ref[...] += jnp.sum(dh2 * n2, axis=0, keepdims=True)
        dn2 = dh2 * gp
        dx1_ref[...] = dy + r2 * (dn2 - n2 * jnp.mean(dn2 * n2, axis=-1, keepdims=True))

    row = lambda w: pl.BlockSpec((FFN_TS, w), lambda i: (i, 0))
    vec = _const_spec((1, D_MODEL))
    return pl.pallas_call(
        body, name="ffn_fwd_bwd", grid=(S // FFN_TS,),
        in_specs=[row(D_MODEL), row(D_MODEL), vec, vec, _resident_spec((D_MODEL, D_FF)), _resident_spec((D_MODEL, D_FF)),
                  _resident_spec((D_FF, D_MODEL))],
        out_specs=[row(D_MODEL), row(D_MODEL), row(D_FF), row(D_FF), row(D_FF), row(D_MODEL), _const_spec((1, LANE)), vec, vec],
        out_shape=[SDS((S, D_MODEL), f32), SDS((S, D_MODEL), bf16), SDS((S, D_FF), bf16), SDS((S, D_FF), bf16),
                   SDS((S, D_FF), bf16), SDS((S, D_MODEL), bf16), SDS((1, LANE), f32), SDS((1, D_MODEL), f32),
                   SDS((1, D_MODEL), f32)],
        scratch_shapes=[pltpu.VMEM((FFN_TS, D_FF), f32), pltpu.VMEM((FFN_TS, D_FF), f32)],
        compiler_params=_cp(1, VMEM_LIMIT),
    )(x1, tgt, g_pre, g_post, wg, wu, wd)


def _proj_bwd(x, dx1, g_pre, wcat, segs, S):
    TS = 256
    n = len(segs)
    cols = [(c0, a.shape[1]) for a, c0 in segs]

    def body(*refs):
        x_ref, dx1_ref, g_ref, w_ref = refs[:4]
        seg_refs = refs[4:4 + n]
        gx_ref, h1_ref, dg_ref = refs[4 + n:]

        @pl.when(pl.program_id(0) == 0)
        def _():
            dg_ref[...] = jnp.zeros_like(dg_ref)

        dh = jnp.zeros((TS, D_MODEL), f32)
        for s_ref, (c0, w) in zip(seg_refs, cols):
            dh = dh + _dot_nt(s_ref[...], w_ref[:, c0:c0 + w])
        xv = x_ref[...]
        g = g_ref[...]
        r = lax.rsqrt(jnp.mean(xv * xv, axis=-1, keepdims=True) + RMS_EPS)
        nx = xv * r
        h1_ref[...] = (nx * g).astype(bf16)
        dg_ref[...] += jnp.sum(dh * nx, axis=0, keepdims=True)
        dn = dh * g
        gx_ref[...] = dx1_ref[...] + r * (dn - nx * jnp.mean(dn * nx, axis=-1, keepdims=True))

    row = lambda w: pl.BlockSpec((TS, w), lambda i: (i, 0))
    return pl.pallas_call(
        body, name="proj_bwd", grid=(S // TS,),
        in_specs=[row(D_MODEL), row(D_MODEL), _const_spec((1, D_MODEL)), _resident_spec((D_MODEL, NCOL))]
                 + [row(w) for _, w in cols],
        out_specs=[row(D_MODEL), row(D_MODEL), _const_spec((1, D_MODEL))],
        out_shape=[SDS((S, D_MODEL), f32), SDS((S, D_MODEL), bf16), SDS((1, D_MODEL), f32)],
        compiler_params=_cp(1, VMEM_LIMIT),
    )(x, dx1, g_pre, wcat, *[a for a, _ in segs])


def _wgrad(a, b, S, name):
    TS = 512
    K = a.shape[1]
    N = b.shape[1]
    TN = next(t for t in (512, 1408, N) if N % t == 0)

    def body(a_ref, b_ref, o_ref):
        @pl.when(pl.program_id(1) == 0)
        def _():
            o_ref[...] = jnp.zeros_like(o_ref)

        o_ref[...] += _dot_tn(a_ref[...].astype(bf16), b_ref[...])

    return pl.pallas_call(
        body, name=name, grid=(N // TN, S // TS),
        in_specs=[pl.BlockSpec((TS, K), lambda j, s: (s, 0)), pl.BlockSpec((TS, TN), lambda j, s: (s, j))],
        out_specs=pl.BlockSpec((K, TN), lambda j, s: (0, j)),
        out_shape=SDS((K, N), f32),
        compiler_params=_cp(2, VMEM_LIMIT),
    )(a, b)


def _adamw(recv, w, m, v, name):
    R, C = w.shape
    TR = 256 if R % 256 == 0 else R
    c1 = 1.0 / (1.0 - ADAM_B1 ** ADAM_STEP)
    c2 = 1.0 / (1.0 - ADAM_B2 ** ADAM_STEP)

    def body(r_ref, w_ref, m_ref, v_ref, g_out, d_out, m_out, v_out):
        g = r_ref[0].astype(f32)
        for s in range(1, N_DEV):
            g = g + r_ref[s].astype(f32)
        mn = ADAM_B1 * m_ref[...] + (1.0 - ADAM_B1) * g
        vn = ADAM_B2 * v_ref[...] + (1.0 - ADAM_B2) * (g * g)
        g_out[...] = g
        m_out[...] = mn
        v_out[...] = vn
        d_out[...] = -ADAM_LR * ((mn * c1) / (jnp.sqrt(vn * c2) + ADAM_EPS) + ADAM_WD * w_ref[...])

    blk = pl.BlockSpec((TR, C), lambda i: (i, 0))
    return pl.pallas_call(
        body, name=name, grid=(R // TR,),
        in_specs=[pl.BlockSpec((N_DEV, TR, C), lambda i: (0, i, 0)), blk, blk, blk],
        out_specs=[blk, blk, blk, blk],
        out_shape=[SDS((R, C), f32)] * 4,
        compiler_params=_cp(1, VMEM_LIMIT),
    )(recv, w, m, v)


MESH = pl.DeviceIdType.MESH
_ANY = pl.BlockSpec(memory_space=pl.ANY)


def _flip(v, d):
    return 1 - v if d else v


def _all_gather(shards):
    n = len(shards)

    def body(*refs):
        ins = refs[:n]
        outs = refs[n:2 * n]
        send_sems, recv_sems, local_sems = refs[2 * n:]
        x, y, c = lax.axis_index("x"), lax.axis_index("y"), lax.axis_index("c")
        me, sibling = (x, y, c), (x, y, 1 - c)
        chips = [(1 - x, y), (x, 1 - y), (1 - x, 1 - y)]

        def slot(px, py, pc):
            return 4 * px + 2 * py + pc

        def copy(a, k, block, to, src=None):
            dst = outs[a].at[slot(*block)]
            return pltpu.make_async_remote_copy(src_ref=dst if src is None else src, dst_ref=dst,
                                                send_sem=send_sems.at[a, k], recv_sem=recv_sems.at[a, k],
                                                device_id=to, device_id_type=MESH)

        mine, first, passed = [], [], []
        for a in range(n):
            cp = pltpu.make_async_copy(ins[a], outs[a].at[slot(*me)], local_sems.at[a])
            cp.start()
            mine.append(cp)
            fs = [copy(a, 0, me, sibling, src=ins[a])]
            fs += [copy(a, 1 + j, me, (*chip, c), src=ins[a]) for j, chip in enumerate(chips)]
            for cp in fs:
                cp.start()
            first += fs
        for j, chip in enumerate(chips):
            for a in range(n):
                copy(a, 1 + j, (*chip, c), me).wait_recv()
                cp = copy(a, 4 + j, (*chip, c), sibling)
                cp.start()
                passed.append(cp)
        for a in range(n):
            copy(a, 0, sibling, me).wait_recv()
            for j, chip in enumerate(chips):
                copy(a, 4 + j, (*chip, 1 - c), me).wait_recv()
        for cp in first + passed:
            cp.wait_send()
        for cp in mine:
            cp.wait()

    return pl.pallas_call(
        body, name="weight_all_gather",
        in_specs=[_ANY] * n, out_specs=[_ANY] * n,
        out_shape=[SDS((N_DEV,) + s.shape, s.dtype) for s in shards],
        scratch_shapes=[pltpu.SemaphoreType.DMA((n, 7)), pltpu.SemaphoreType.DMA((n, 7)), pltpu.SemaphoreType.DMA((n,))],
        compiler_params=pltpu.CompilerParams(has_side_effects=True),
    )(*shards)


def _grad_exchange(blocked, whole):
    arrs = list(blocked) + list(whole)
    n, nb = len(arrs), len(blocked)
    rel = [(dx, dy, dc) for dx in (0, 1) for dy in (0, 1) for dc in (0, 1) if dx or dy or dc]

    def body(*refs):
        ins = refs[:n]
        outs = refs[n:2 * n]
        send_sems, recv_sems, local_sems = refs[2 * n:]
        x, y, c = lax.axis_index("x"), lax.axis_index("y"), lax.axis_index("c")
        me = 4 * x + 2 * y + c
        sends, locs = [], []
        for a in range(n):
            cp = pltpu.make_async_copy(ins[a].at[me] if a < nb else ins[a], outs[a].at[me], local_sems.at[a])
            cp.start()
            locs.append(cp)
            for k, (dx, dy, dc) in enumerate(rel):
                peer = (_flip(x, dx), _flip(y, dy), _flip(c, dc))
                pidx = 4 * peer[0] + 2 * peer[1] + peer[2]
                cp = pltpu.make_async_remote_copy(src_ref=ins[a].at[pidx] if a < nb else ins[a], dst_ref=outs[a].at[me],
                                                  send_sem=send_sems.at[a, k], recv_sem=recv_sems.at[a, k],
                                                  device_id=peer, device_id_type=MESH)
                cp.start()
                sends.append(cp)
        for a in range(n):
            for k, (dx, dy, dc) in enumerate(rel):
                peer = (_flip(x, dx), _flip(y, dy), _flip(c, dc))
                pidx = 4 * peer[0] + 2 * peer[1] + peer[2]
                pltpu.make_async_remote_copy(src_ref=outs[a].at[pidx], dst_ref=outs[a].at[pidx],
                                             send_sem=send_sems.at[a, k], recv_sem=recv_sems.at[a, k],
                                             device_id=peer, device_id_type=MESH).wait_recv()
        for cp in sends:
            cp.wait_send()
        for cp in locs:
            cp.wait()

    shapes = [SDS(a.shape, a.dtype) for a in blocked] + [SDS((N_DEV,) + a.shape, a.dtype) for a in whole]
    return pl.pallas_call(
        body, name="grad_exchange",
        in_specs=[_ANY] * n, out_specs=[_ANY] * n, out_shape=shapes,
        scratch_shapes=[pltpu.SemaphoreType.DMA((n, 7)), pltpu.SemaphoreType.DMA((n, 7)), pltpu.SemaphoreType.DMA((n,))],
        compiler_params=pltpu.CompilerParams(has_side_effects=True),
    )(*arrs)


def _local_step(x, tgt, wcat, convw, wout, wgate, wup, wdown, a_log, dt_bias, onorm_g, rel_bias,
                g_mix_pre, g_mix_post, g_ffn_pre, g_ffn_post):
    S = x.shape[0]
    bk_np = _bucket_tables()
    bk = jnp.asarray(bk_np)
    bt = _bias_tables(rel_bias, bk)
    proj = _proj_fwd(x, g_mix_pre, wcat, S)
    nu = S // CHUNK * GDN_HEADS
    qkv_u = _gdn_prep(proj, convw, S).reshape(3, nu, CHUNK, GDN_HD)
    intra, t_inv = _gdn_intra_fwd(qkv_u, proj, a_log, dt_bias, S)
    oa, states = _gdn_scan_fwd(intra, proj, onorm_g, S)
    ob, lse = _swa_fwd(proj, bt, S)
    mix, x1 = _mix_fwd(oa, ob, wout, x, g_mix_post, S)
    dx1, h2, act, dgate_f, dup_f, df, loss, d_gfpre, d_gfpost = _ffn(x1, tgt, g_ffn_pre, g_ffn_post, wgate, wup, wdown, S)
    dmix, d_oab, d_gmpost = _mix_bwd(dx1, mix, g_mix_post, wout, S)
    dqb, dkb, dvb, dsb = _swa_bwd(proj, bt, ob, lse, d_oab, S)
    *cots, dgate_a, d_og = _gdn_scan_bwd(intra, states, proj, d_oab, onorm_g, S)
    dqkv_u, dpg, d_alog, d_dtb = _gdn_intra_bwd(qkv_u, proj, a_log, dt_bias, t_inv, cots, S)
    dqkv_a, d_conv = _gdn_prep_bwd(proj, convw, dqkv_u.reshape(3, S // CHUNK, GDN_HEADS, CHUNK, GDN_HD), S)
    segs = [(dqkv_a, COL_A), (dgate_a, COL_A + 3 * GDN_W), (dqb, COL_B), (dkb, COL_B + SWA_W), (dvb, COL_B + 2 * SWA_W),
            (dpg, COL_G)]
    grad_x, h1, d_gmpre = _proj_bwd(x, dx1, g_mix_pre, wcat, segs, S)
    g_segs = [_wgrad(h1, a, S, "wgrad_in_%d" % i) for i, (a, _) in enumerate(segs)]
    g_in = jnp.concatenate([g_segs[0], g_segs[1], g_segs[5][:, :2 * GDN_HEADS], g_segs[2], g_segs[3], g_segs[4]], axis=1)
    g_out = jnp.concatenate([_wgrad(oa, dmix, S, "wgrad_out_a"), _wgrad(ob, dmix, S, "wgrad_out_b")], axis=0)
    g_gate = _wgrad(h2, dgate_f, S, "wgrad_gate")
    g_up = _wgrad(h2, dup_f, S, "wgrad_up")
    g_down = _wgrad(act, df, S, "wgrad_down")
    d_rel = _rel_bias_grad(dsb, bk, bk_np)
    small = dict(a_log=d_alog[:, :GDN_HEADS], dt_bias=d_dtb[:, :GDN_HEADS], onorm_g=d_og, rel_bias=d_rel,
                 g_mix_pre=d_gmpre, g_mix_post=d_gmpost, g_ffn_pre=d_gfpre, g_ffn_post=d_gfpost)
    return loss, grad_x, (g_in, d_conv, g_out, g_gate, g_up, g_down), small


SMALL = ("a_log", "dt_bias", "onorm_g", "rel_bias", "g_mix_pre", "g_mix_post", "g_ffn_pre", "g_ffn_post")
PACK_ROWS = 8


def _pack_small(d):
    rest = jnp.concatenate([d["onorm_g"].reshape(-1), d["a_log"].reshape(-1), d["dt_bias"].reshape(-1),
                            d["rel_bias"].reshape(-1)])
    rest = jnp.concatenate([rest, jnp.zeros((D_MODEL - rest.shape[0],), f32)])
    rows = [d["g_mix_pre"].reshape(-1), d["g_mix_post"].reshape(-1), d["g_ffn_pre"].reshape(-1),
            d["g_ffn_post"].reshape(-1), rest]
    return jnp.concatenate([jnp.stack(rows), jnp.zeros((PACK_ROWS - len(rows), D_MODEL), f32)], axis=0)


def _unpack_small(p):
    o = GDN_HD
    return dict(g_mix_pre=p[0:1], g_mix_post=p[1:2], g_ffn_pre=p[2:3], g_ffn_post=p[3:4],
                onorm_g=p[4:5, :o], a_log=p[4:5, o:o + 4], dt_bias=p[4:5, o + 4:o + 8],
                rel_bias=p[4, o + 8:o + 8 + NUM_BUCKETS * SWA_HEADS].reshape(NUM_BUCKETS, SWA_HEADS))


def kernel(x, w_in, conv_w, a_log, dt_bias, onorm_g, rel_bias, w_out, g_mix_pre, g_mix_post, w_gate, w_up, w_down, g_ffn_pre, g_ffn_post, loss_target, m_w_in, m_conv_w, m_a_log, m_dt_bias, m_onorm_g, m_rel_bias, m_w_out, m_g_mix_pre, m_g_mix_post, m_w_gate, m_w_up, m_w_down, m_g_ffn_pre, m_g_ffn_post, v_w_in, v_conv_w, v_a_log, v_dt_bias, v_onorm_g, v_rel_bias, v_w_out, v_g_mix_pre, v_g_mix_post, v_w_gate, v_w_up, v_w_down, v_g_ffn_pre, v_g_ffn_post):
    big = ("w_in", "conv_w", "w_out", "w_gate", "w_up", "w_down")
    w_sh = dict(w_in=w_in[0], conv_w=conv_w[0], w_out=w_out[0], w_gate=w_gate[0], w_up=w_up[0], w_down=w_down[0])
    m_sh = dict(w_in=m_w_in[0], conv_w=m_conv_w[0], w_out=m_w_out[0], w_gate=m_w_gate[0], w_up=m_w_up[0], w_down=m_w_down[0])
    v_sh = dict(w_in=v_w_in[0], conv_w=v_conv_w[0], w_out=v_w_out[0], w_gate=v_w_gate[0], w_up=v_w_up[0], w_down=v_w_down[0])
    w_small = dict(a_log=a_log, dt_bias=dt_bias, onorm_g=onorm_g, rel_bias=rel_bias, g_mix_pre=g_mix_pre,
                   g_mix_post=g_mix_post, g_ffn_pre=g_ffn_pre, g_ffn_post=g_ffn_post)
    m_small = dict(a_log=m_a_log, dt_bias=m_dt_bias, onorm_g=m_onorm_g, rel_bias=m_rel_bias, g_mix_pre=m_g_mix_pre,
                   g_mix_post=m_g_mix_post, g_ffn_pre=m_g_ffn_pre, g_ffn_post=m_g_ffn_post)
    v_small = dict(a_log=v_a_log, dt_bias=v_dt_bias, onorm_g=v_onorm_g, rel_bias=v_rel_bias, g_mix_pre=v_g_mix_pre,
                   g_mix_post=v_g_mix_post, g_ffn_pre=v_g_ffn_pre, g_ffn_post=v_g_ffn_post)

    shards = [w_sh[k] if k == "conv_w" else w_sh[k].astype(bf16) for k in big]
    g_in, g_conv, g_out, g_gate, g_up, g_down = _all_gather(shards)
    wfull = g_in.transpose(1, 0, 2).reshape(D_MODEL, IN_COLS)
    n_a = 4 * GDN_W
    wcat = jnp.concatenate([wfull[:, :n_a], wfull[:, n_a + 2 * GDN_HEADS:], wfull[:, n_a:n_a + 2 * GDN_HEADS],
                            jnp.zeros((D_MODEL, LANE - 2 * GDN_HEADS), bf16)], axis=1)
    convw = g_conv.transpose(1, 0, 2).reshape(4, 3 * GDN_W)
    wout = g_out.reshape(D_MODEL, D_MODEL)
    wgate = g_gate.transpose(1, 0, 2).reshape(D_MODEL, D_FF)
    wup = g_up.transpose(1, 0, 2).reshape(D_MODEL, D_FF)
    wdown = g_down.reshape(D_FF, D_MODEL)

    loss_p, grad_x, gbig, gsmall = _local_step(x[0], loss_target[0], wcat, convw, wout, wgate, wup, wdown,
                                               a_log, dt_bias, onorm_g, rel_bias, g_mix_pre, g_mix_post, g_ffn_pre, g_ffn_post)
    gw_in, gw_conv, gw_out, gw_gate, gw_up, gw_down = gbig

    cols = lambda g: g.reshape(g.shape[0], N_DEV, g.shape[1] // N_DEV).transpose(1, 0, 2)
    rows = lambda g: g.reshape(N_DEV, g.shape[0] // N_DEV, g.shape[1])
    blocked = [cols(gw_in).astype(bf16), cols(gw_conv), rows(gw_out).astype(bf16), cols(gw_gate).astype(bf16),
               cols(gw_up).astype(bf16), rows(gw_down).astype(bf16)]
    recv = _grad_exchange(blocked, [_pack_small(gsmall)])

    outs = {}
    for k, r in zip(big, recv[:len(big)]):
        outs[k] = _adamw(r, w_sh[k], m_sh[k], v_sh[k], "adamw_" + k)
    sm = _adamw(recv[-1], _pack_small(w_small), _pack_small(m_small), _pack_small(v_small), "adamw_small")
    sm = [_unpack_small(t) for t in sm]
    for k in SMALL:
        outs[k] = tuple(t[k].reshape(w_small[k].shape) for t in sm)

    loss = lax.psum(loss_p[0, 0], ("x", "y", "c"))
    order = ("w_in", "conv_w", "a_log", "dt_bias", "onorm_g", "rel_bias", "w_out", "g_mix_pre", "g_mix_post", "w_gate",
             "w_up", "w_down", "g_ffn_pre", "g_ffn_post")
    lead = lambda k, t: t[None] if k in big else t
    res = [loss, grad_x[None]]
    for i in range(4):
        res += [lead(k, outs[k][i]) for k in order]
    return tuple(res)
```

```python
import functools
import math

import numpy as np
import jax
import jax.numpy as jnp
from jax import lax
from jax.experimental import pallas as pl
from jax.experimental.pallas import tpu as pltpu

f32 = jnp.float32
bf16 = jnp.bfloat16
SDS = jax.ShapeDtypeStruct

D_MODEL = 1024
GDN_HEADS = 4
GDN_HD = 128
GDN_W = 512
CHUNK = 64
SWA_HEADS = 8
SWA_HD = 64
SWA_W = 512
D_FF = 2816
IN_COLS = 3592
PATTERNS = ((128, 1), (512, 4), (2048, 16))
SWA_BLK = 128
NUM_BUCKETS = 32
MAX_DISTANCE = 2048
RMS_EPS = 1e-6
NEG = -1e30
N_DEV = 8

COL_A = 0
COL_B = 2048
COL_G = 3584
NCOL = 3712
LANE = 128

ADAM_LR, ADAM_B1, ADAM_B2, ADAM_EPS, ADAM_WD, ADAM_STEP = 0.001, 0.9, 0.999, 1e-08, 0.01, 10

VMEM_LIMIT = 56 * 1024 * 1024

HI = lax.Precision.HIGHEST
HIGH = lax.Precision.HIGH


def _cp(n_grid=0, vmem=None):
    kw = {}
    if n_grid:
        kw["dimension_semantics"] = ("arbitrary",) * n_grid
    if vmem:
        kw["vmem_limit_bytes"] = vmem
    return pltpu.CompilerParams(**kw)


def _dot(a, b):
    return jnp.dot(a, b, preferred_element_type=f32)


def _dot_nt(a, b):
    return lax.dot_general(a, b, (((1,), (1,)), ((), ())), preferred_element_type=f32)


def _dot_tn(a, b):
    return lax.dot_general(a, b, (((0,), (0,)), ((), ())), preferred_element_type=f32)


def _dot_hi(a, b):
    return jnp.dot(a, b, precision=HI, preferred_element_type=f32)


def _sigmoid(x):
    return 1.0 / (1.0 + jnp.exp(-x))


def _softplus(x):
    return jnp.maximum(x, 0.0) + jnp.log(1.0 + jnp.exp(-jnp.abs(x)))


def _const_spec(shape):
    nd = len(shape)
    return pl.BlockSpec(shape, lambda *_: (0,) * nd)


def _resident_spec(shape):
    nd = len(shape)
    return pl.BlockSpec(shape, lambda *_: (0,) * nd, pipeline_mode=pl.Buffered(1))


def _t5_bucket_np(dist):
    max_exact = NUM_BUCKETS // 2
    d = np.maximum(dist, 1).astype(np.float32)
    log_b = max_exact + (np.log(d / np.float32(max_exact)) / np.float32(math.log(MAX_DISTANCE / max_exact))
                         * np.float32(NUM_BUCKETS - max_exact)).astype(np.int32)
    return np.where(dist < max_exact, dist, np.minimum(log_b, NUM_BUCKETS - 1)).astype(np.int32)


def _bucket_tables():
    w = SWA_BLK
    qi = np.arange(w)[:, None]
    kj = np.arange(w)[None, :]
    out = np.zeros((len(PATTERNS), 2, w, w), np.int32)
    for p, (_, dil) in enumerate(PATTERNS):
        steps = _t5_bucket_np(np.arange(w + 1) * dil)
        rel_prev = qi + w - kj
        rel_cur = qi - kj
        out[p, 0] = np.where(rel_prev <= w, steps[np.clip(rel_prev, 0, w)], -1)
        out[p, 1] = np.where(rel_cur >= 0, steps[np.clip(rel_cur, 0, w)], -1)
    return out


def _bias_tables(rel_bias, bk):
    def body(rb_ref, bk_ref, o_ref):
        b_idx = bk_ref[0, 0]
        for h in range(SWA_HEADS):
            def lp(b, acc):
                return jnp.where(b_idx == b, rb_ref[b, h], acc)
            o_ref[0, h, 0] = lax.fori_loop(0, NUM_BUCKETS, lp, jnp.full((SWA_BLK, SWA_BLK), NEG, f32))

    return pl.pallas_call(
        body, name="bias_tables", grid=(3, 2),
        in_specs=[pl.BlockSpec(memory_space=pltpu.SMEM),
                  pl.BlockSpec((1, 1, SWA_BLK, SWA_BLK), lambda p, t: (p, t, 0, 0))],
        out_specs=pl.BlockSpec((1, SWA_HEADS, 1, SWA_BLK, SWA_BLK), lambda p, t: (p, 0, t, 0, 0)),
        out_shape=SDS((3, SWA_HEADS, 2, SWA_BLK, SWA_BLK), f32),
        compiler_params=_cp(2),
    )(rel_bias, bk)


def _rel_bias_grad(dsb, bk, bk_np):
    present = [[sorted(set(int(v) for v in np.unique(bk_np[p, t]) if v >= 0)) for t in range(2)] for p in range(3)]

    def body(ds_ref, bk_ref, o_ref):
        row = lax.broadcasted_iota(jnp.int32, (NUM_BUCKETS, LANE), 0)
        col = lax.broadcasted_iota(jnp.int32, (NUM_BUCKETS, SWA_HEADS), 1)
        out = jnp.zeros((NUM_BUCKETS, SWA_HEADS), f32)
        for hp in range(4):
            for hh in range(2):
                acc = jnp.zeros((NUM_BUCKETS, LANE), f32)
                for p in range(3):
                    for t in range(2):
                        tile = ds_ref[hp, p, hh, t]
                        b_idx = bk_ref[p, t]
                        for b in present[p][t]:
                            part = jnp.sum(jnp.where(b_idx == b, tile, 0.0), axis=0, keepdims=True)
                            acc = acc + jnp.where(row == b, part, 0.0)
                tot = jnp.sum(acc, axis=1, keepdims=True)
                out = out + jnp.where(col == 2 * hp + hh, tot, 0.0)
        o_ref[...] = out

    return pl.pallas_call(body, name="rel_bias_grad", out_shape=SDS((NUM_BUCKETS, SWA_HEADS), f32),
                          compiler_params=_cp(0, 32 * 1024 * 1024))(dsb, bk)


def _proj_fwd(x, g_pre, wcat, after, S):
    TS = 256

    def body(x_ref, g_ref, w_ref, after_ref, o_ref):
        xv = x_ref[...]
        r = lax.rsqrt(jnp.mean(xv * xv, axis=-1, keepdims=True) + RMS_EPS)
        h = (xv * r * g_ref[...]).astype(bf16)
        o_ref[...] = _dot(h, w_ref[...])

    return pl.pallas_call(
        body, name="proj_fwd", grid=(S // TS,),
        in_specs=[pl.BlockSpec((TS, D_MODEL), lambda i: (i, 0)), _const_spec((1, D_MODEL)),
                  _resident_spec((D_MODEL, NCOL)), _ANY],
        out_specs=pl.BlockSpec((TS, NCOL), lambda i: (i, 0)),
        out_shape=SDS((S, NCOL), f32),
        compiler_params=_cp(1, VMEM_LIMIT),
    )(x, g_pre, wcat, after)


CONV_RT = 256
HALO = 8


CONV_NC = CONV_RT // CHUNK


def _gdn_prep(proj, conv_w, S):
    def body(p_ref, cw_ref, o_ref, xs_ref):
        t = pl.program_id(0)
        xs_ref[pl.ds(0, HALO), :] = jnp.zeros((HALO, LANE), f32)
        xs_ref[pl.ds(HALO, S), :] = p_ref[...]
        w = cw_ref[...]
        is_qk = t < 2
        scale = jnp.where(t == 0, GDN_HD ** -0.5, 1.0).astype(f32)

        def lp(c, carry):
            st = pl.multiple_of(c * CONV_RT, CONV_RT)
            pre = xs_ref[pl.ds(st + HALO - 3, CONV_RT), :] * w[0:1, :]
            for i in range(1, 4):
                pre = pre + xs_ref[pl.ds(st + HALO - 3 + i, CONV_RT), :] * w[i:i + 1, :]
            s = pre * _sigmoid(pre)
            nrm = s * lax.rsqrt(jnp.sum(s * s, axis=-1, keepdims=True) + 1e-6) * scale
            out = jnp.where(is_qk, nrm, s)
            for i in range(CONV_NC):
                o_ref[0, c * CONV_NC + i, 0] = out[i * CHUNK:(i + 1) * CHUNK]
            return carry

        lax.fori_loop(0, S // CONV_RT, lp, 0)

    return pl.pallas_call(
        body, name="gdn_prep", grid=(3, GDN_HEADS),
        in_specs=[pl.BlockSpec((S, LANE), lambda t, h: (0, t * GDN_HEADS + h)),
                  pl.BlockSpec((4, LANE), lambda t, h: (0, t * GDN_HEADS + h))],
        out_specs=pl.BlockSpec((1, S // CHUNK, 1, CHUNK, GDN_HD), lambda t, h: (t, 0, h, 0, 0)),
        out_shape=SDS((3, S // CHUNK, GDN_HEADS, CHUNK, GDN_HD), f32),
        scratch_shapes=[pltpu.VMEM((S + HALO, LANE), f32)],
        compiler_params=_cp(2, VMEM_LIMIT),
    )(proj, conv_w)


def _gdn_prep_bwd(proj, conv_w, dqkv, S):
    def body(p_ref, cw_ref, d_ref, dx_ref, dw_ref, xs_ref, dp_ref):
        t = pl.program_id(0)
        xs_ref[pl.ds(0, HALO), :] = jnp.zeros((HALO, LANE), f32)
        xs_ref[pl.ds(HALO, S), :] = p_ref[...]
        dp_ref[pl.ds(S, HALO), :] = jnp.zeros((HALO, LANE), f32)
        w = cw_ref[...]
        is_qk = t < 2
        scale = jnp.where(t == 0, GDN_HD ** -0.5, 1.0).astype(f32)

        def lp1(c, dw):
            st = pl.multiple_of(c * CONV_RT, CONV_RT)
            taps = [xs_ref[pl.ds(st + HALO - 3 + i, CONV_RT), :] for i in range(4)]
            pre = taps[0] * w[0:1, :]
            for i in range(1, 4):
                pre = pre + taps[i] * w[i:i + 1, :]
            sg = _sigmoid(pre)
            s = pre * sg
            d_out = jnp.concatenate([d_ref[0, c * CONV_NC + i, 0] for i in range(CONV_NC)], axis=0)
            rn = lax.rsqrt(jnp.sum(s * s, axis=-1, keepdims=True) + 1e-6)
            n = s * rn
            dn = d_out * scale
            ds_qk = rn * (dn - n * jnp.sum(dn * n, axis=-1, keepdims=True))
            ds = jnp.where(is_qk, ds_qk, d_out)
            dpre = ds * (sg * (1.0 + pre * (1.0 - sg)))
            dp_ref[pl.ds(st, CONV_RT), :] = dpre
            return tuple(dw[i] + jnp.sum(dpre * taps[i], axis=0, keepdims=True) for i in range(4))

        z = jnp.zeros((1, LANE), f32)
        dw = lax.fori_loop(0, S // CONV_RT, lp1, (z, z, z, z))
        for i in range(4):
            dw_ref[pl.ds(i, 1), :] = dw[i]

        def lp2(c, carry):
            st = pl.multiple_of(c * CONV_RT, CONV_RT)
            dx = dp_ref[pl.ds(st, CONV_RT), :] * w[3:4, :]
            for i in range(3):
                dx = dx + dp_ref[pl.ds(st + 3 - i, CONV_RT), :] * w[i:i + 1, :]
            dx_ref[pl.ds(st, CONV_RT), :] = dx.astype(bf16)
            return carry

        lax.fori_loop(0, S // CONV_RT, lp2, 0)

    col = lambda rows: pl.BlockSpec((rows, LANE), lambda t, h: (0, t * GDN_HEADS + h))
    return pl.pallas_call(
        body, name="gdn_prep_bwd", grid=(3, GDN_HEADS),
        in_specs=[col(S), col(4), pl.BlockSpec((1, S // CHUNK, 1, CHUNK, GDN_HD), lambda t, h: (t, 0, h, 0, 0))],
        out_specs=[col(S), col(4)],
        out_shape=[SDS((S, 3 * GDN_W), bf16), SDS((4, 3 * GDN_W), f32)],
        scratch_shapes=[pltpu.VMEM((S + HALO, LANE), f32), pltpu.VMEM((S + HALO, LANE), f32)],
        compiler_params=_cp(2, VMEM_LIMIT),
    )(proj, conv_w, dqkv)


def _bdot(a, b, prec=None):
    return lax.dot_general(a, b, (((2,), (1,)), ((0,), (0,))), precision=prec, preferred_element_type=f32)


def _bdot_nt(a, b, prec=None):
    return lax.dot_general(a, b, (((2,), (2,)), ((0,), (0,))), precision=prec, preferred_element_type=f32)


def _bdot_tn(a, b, prec=None):
    return lax.dot_general(a, b, (((1,), (1,)), ((0,), (0,))), precision=prec, preferred_element_type=f32)


@jax.custom_vjp
def _tri_inv_saved(a, t):
    return t


def _tri_inv_saved_fwd(a, t):
    return t, t


def _tri_inv_saved_bwd(t, dt):
    return -_bdot_tn(t, _bdot_nt(dt, t, HIGH), HIGH), jnp.zeros_like(t)


_tri_inv_saved.defvjp(_tri_inv_saved_fwd, _tri_inv_saved_bwd)


def _gdn_intra(q, k, v, bl, al, a_log, dt_bias, t_saved=None):
    nb = q.shape[0]
    c = CHUNK
    ii = lax.broadcasted_iota(jnp.int32, (c, c), 0)
    jj = lax.broadcasted_iota(jnp.int32, (c, c), 1)
    eye = ii == jj
    tril = ii >= jj
    strict = ii > jj
    ones = jnp.ones((nb, c, c), f32)
    eye_f = eye.astype(f32)

    beta = _sigmoid(bl)
    g = -jnp.exp(a_log) * _softplus(al + dt_bias)
    g_row = _bdot(ones, jnp.where(eye, g, 0.0), HI)
    gc = jnp.sum(jnp.where(tril, g_row, 0.0), axis=2, keepdims=True)
    gc_row = _bdot(ones, jnp.where(eye, gc, 0.0), HI)
    decay = jnp.where(tril, jnp.exp(jnp.where(tril, gc - gc_row, 0.0)), 0.0)
    last = lax.broadcasted_iota(jnp.int32, (c, 1), 0) == c - 1
    gc_last = jnp.sum(jnp.where(last, gc, 0.0), axis=1, keepdims=True)
    e_gc = jnp.exp(gc)

    kb = k * beta
    k16 = k.astype(bf16)
    a = jnp.where(strict, _bdot_nt(kb.astype(bf16), k16) * decay, 0.0)
    if t_saved is None:
        xp = -a
        t_inv = eye_f + xp
        for _ in range(5):
            xp = _bdot(xp, xp, HIGH)
            t_inv = _bdot(t_inv, eye_f + xp, HIGH)
    else:
        t_inv = _tri_inv_saved(a, t_saved)
    t16 = t_inv.astype(bf16)
    u = _bdot(t16, (v * beta).astype(bf16))
    w = _bdot(t16, (kb * e_gc).astype(bf16))
    attn = jnp.where(tril, _bdot_nt(q.astype(bf16), k16) * decay, 0.0)
    gam = jnp.broadcast_to(jnp.exp(gc_last), (nb, 1, GDN_HD))
    return u, w, attn, q * e_gc, k * jnp.exp(gc_last - gc), gam, t_inv


GDN_TB = 256
GDN_NC = GDN_TB // CHUNK
GDN_NU = GDN_NC * GDN_HEADS


def _gdn_unit_inputs(qkv_ref, pg_ref, al_ref, db_ref):
    pg = pg_ref[...]
    bl = jnp.stack([pg[cl * CHUNK:(cl + 1) * CHUNK, h:h + 1] for cl in range(GDN_NC) for h in range(GDN_HEADS)])
    al = jnp.stack([pg[cl * CHUNK:(cl + 1) * CHUNK, GDN_HEADS + h:GDN_HEADS + h + 1]
                    for cl in range(GDN_NC) for h in range(GDN_HEADS)])
    a_log = jnp.stack([jnp.full((1, 1), al_ref[0, h], f32) for _ in range(GDN_NC) for h in range(GDN_HEADS)])
    dt_b = jnp.stack([jnp.full((1, 1), db_ref[0, h], f32) for _ in range(GDN_NC) for h in range(GDN_HEADS)])
    return qkv_ref[0], qkv_ref[1], qkv_ref[2], bl, al, a_log, dt_b


def _unit_spec(*tail):
    nd = len(tail)
    return pl.BlockSpec((GDN_NU,) + tail, lambda i: (i,) + (0,) * nd)


def _gdn_intra_shapes(S):
    nu = S // CHUNK * GDN_HEADS
    row = SDS((nu, CHUNK, GDN_HD), f32)
    return [row, row, SDS((nu, CHUNK, CHUNK), f32), row, row, SDS((nu, 1, GDN_HD), f32)]


_GDN_INTRA_SPECS = lambda: [_unit_spec(CHUNK, GDN_HD), _unit_spec(CHUNK, GDN_HD), _unit_spec(CHUNK, CHUNK),
                            _unit_spec(CHUNK, GDN_HD), _unit_spec(CHUNK, GDN_HD), _unit_spec(1, GDN_HD)]


def _gdn_intra_fwd(qkv_u, proj, a_log, dt_bias, S):
    def body(qkv_ref, pg_ref, al_ref, db_ref, *outs):
        res = _gdn_intra(*_gdn_unit_inputs(qkv_ref, pg_ref, al_ref, db_ref))
        for o_ref, r in zip(outs, res):
            o_ref[...] = r

    nu = S // CHUNK * GDN_HEADS
    *intra, t_inv = pl.pallas_call(
        body, name="gdn_intra_fwd", grid=(S // GDN_TB,),
        in_specs=[pl.BlockSpec((3, GDN_NU, CHUNK, GDN_HD), lambda i: (0, i, 0, 0)),
                  pl.BlockSpec((GDN_TB, LANE), lambda i: (i, COL_G // LANE)),
                  pl.BlockSpec(memory_space=pltpu.SMEM), pl.BlockSpec(memory_space=pltpu.SMEM)],
        out_specs=_GDN_INTRA_SPECS() + [_unit_spec(CHUNK, CHUNK)],
        out_shape=_gdn_intra_shapes(S) + [SDS((nu, CHUNK, CHUNK), f32)],
        compiler_params=_cp(1, VMEM_LIMIT),
    )(qkv_u, proj, a_log, dt_bias)
    return intra, t_inv


def _gdn_intra_bwd(qkv_u, proj, a_log, dt_bias, t_inv, cots, S):
    def body(qkv_ref, pg_ref, al_ref, db_ref, t_ref, du_ref, dw_ref, da_ref, dqd_ref, dkd_ref, dgm_ref,
             dqkv_ref, dpg_ref, dal_ref, ddb_ref):
        @pl.when(pl.program_id(0) == 0)
        def _():
            dal_ref[...] = jnp.zeros_like(dal_ref)
            ddb_ref[...] = jnp.zeros_like(ddb_ref)

        t_saved = t_ref[...]
        _, vjp = jax.vjp(lambda *a: _gdn_intra(*a, t_saved=t_saved)[:6], *_gdn_unit_inputs(qkv_ref, pg_ref, al_ref, db_ref))
        dq, dk, dv, dbl, dal, da, ddb = vjp((du_ref[...], dw_ref[...], da_ref[...], dqd_ref[...], dkd_ref[...], dgm_ref[...]))
        dqkv_ref[0] = dq
        dqkv_ref[1] = dk
        dqkv_ref[2] = dv
        lane = lax.broadcasted_iota(jnp.int32, (CHUNK, LANE), 1)
        lane1 = lax.broadcasted_iota(jnp.int32, (1, LANE), 1)
        da_tot = jnp.zeros((1, LANE), f32)
        ddb_tot = jnp.zeros((1, LANE), f32)
        for cl in range(GDN_NC):
            dpg = jnp.zeros((CHUNK, LANE), f32)
            for h in range(GDN_HEADS):
                b = cl * GDN_HEADS + h
                dpg = dpg + jnp.where(lane == h, dbl[b], 0.0) + jnp.where(lane == GDN_HEADS + h, dal[b], 0.0)
                da_tot = da_tot + jnp.where(lane1 == h, da[b], 0.0)
                ddb_tot = ddb_tot + jnp.where(lane1 == h, ddb[b], 0.0)
            dpg_ref[cl * CHUNK:(cl + 1) * CHUNK, :] = dpg.astype(bf16)
        dal_ref[...] += da_tot
        ddb_ref[...] += ddb_tot

    acc = _const_spec((1, LANE))
    nu = S // CHUNK * GDN_HEADS
    return pl.pallas_call(
        body, name="gdn_intra_bwd", grid=(S // GDN_TB,),
        in_specs=[pl.BlockSpec((3, GDN_NU, CHUNK, GDN_HD), lambda i: (0, i, 0, 0)),
                  pl.BlockSpec((GDN_TB, LANE), lambda i: (i, COL_G // LANE)),
                  pl.BlockSpec(memory_space=pltpu.SMEM), pl.BlockSpec(memory_space=pltpu.SMEM),
                  _unit_spec(CHUNK, CHUNK)] + _GDN_INTRA_SPECS(),
        out_specs=[pl.BlockSpec((3, GDN_NU, CHUNK, GDN_HD), lambda i: (0, i, 0, 0)),
                   pl.BlockSpec((GDN_TB, LANE), lambda i: (i, 0)), acc, acc],
        out_shape=[SDS((3, nu, CHUNK, GDN_HD), f32), SDS((S, LANE), bf16), SDS((1, LANE), f32), SDS((1, LANE), f32)],
        compiler_params=_cp(1, VMEM_LIMIT),
    )(qkv_u, proj, a_log, dt_bias, t_inv, *cots)


def _gdn_scan_fwd(intra, proj, onorm_g, S):
    def body(u_ref, w_ref, at_ref, qd_ref, kd_ref, gm_ref, gate_ref, og_ref, out_ref, st_ref, s_scr):
        @pl.when(pl.program_id(0) == 0)
        def _():
            s_scr[...] = jnp.zeros_like(s_scr)

        og = og_ref[...]
        s = s_scr[...]
        for cl in range(GDN_NC):
            us = slice(cl * GDN_HEADS, (cl + 1) * GDN_HEADS)
            rows = slice(cl * CHUNK, (cl + 1) * CHUNK)
            st_ref[us] = s
            s16 = s.astype(bf16)
            vn = u_ref[us] - _bdot(w_ref[us].astype(bf16), s16)
            vn16 = vn.astype(bf16)
            o = _bdot(qd_ref[us].astype(bf16), s16) + _bdot(at_ref[us].astype(bf16), vn16)
            s = s * gm_ref[us] + _bdot_tn(kd_ref[us].astype(bf16), vn16)
            for h in range(GDN_HEADS):
                oh = o[h]
                gt = gate_ref[rows, h * GDN_HD:(h + 1) * GDN_HD]
                on = oh * lax.rsqrt(jnp.mean(oh * oh, axis=-1, keepdims=True) + RMS_EPS) * og
                out_ref[rows, h * GDN_HD:(h + 1) * GDN_HD] = on * (gt * _sigmoid(gt))
        s_scr[...] = s

    nu = S // CHUNK * GDN_HEADS
    return pl.pallas_call(
        body, name="gdn_scan_fwd", grid=(S // GDN_TB,),
        in_specs=_GDN_INTRA_SPECS() + [pl.BlockSpec((GDN_TB, GDN_W), lambda i: (i, 3)), _const_spec((1, GDN_HD))],
        out_specs=[pl.BlockSpec((GDN_TB, GDN_W), lambda i: (i, 0)), _unit_spec(GDN_HD, GDN_HD)],
        out_shape=[SDS((S, GDN_W), f32), SDS((nu, GDN_HD, GDN_HD), f32)],
        scratch_shapes=[pltpu.VMEM((GDN_HEADS, GDN_HD, GDN_HD), f32)],
        compiler_params=_cp(1, VMEM_LIMIT),
    )(*intra, proj, onorm_g)


def _gdn_scan_bwd(intra, states, proj, d_oab, onorm_g, after, S):
    n_steps = S // GDN_TB

    def body(u_ref, w_ref, at_ref, qd_ref, kd_ref, gm_ref, st_ref, gate_ref, do_ref, og_ref, after_ref,
             du_ref, dw_ref, dat_ref, dqd_ref, dkd_ref, dgm_ref, dgate_ref, dog_ref, ds_scr):
        @pl.when(pl.program_id(0) == 0)
        def _():
            ds_scr[...] = jnp.zeros_like(ds_scr)
            dog_ref[...] = jnp.zeros_like(dog_ref)

        og = og_ref[...]
        ii = lax.broadcasted_iota(jnp.int32, (CHUNK, CHUNK), 0)
        jj = lax.broadcasted_iota(jnp.int32, (CHUNK, CHUNK), 1)
        tril = ii >= jj
        ds = ds_scr[...]
        dog = jnp.zeros((1, GDN_HD), f32)
        for cl in reversed(range(GDN_NC)):
            us = slice(cl * GDN_HEADS, (cl + 1) * GDN_HEADS)
            rows = slice(cl * CHUNK, (cl + 1) * CHUNK)
            s0 = st_ref[us]
            s016 = s0.astype(bf16)
            w16 = w_ref[us].astype(bf16)
            qd16 = qd_ref[us].astype(bf16)
            kd16 = kd_ref[us].astype(bf16)
            at16 = at_ref[us].astype(bf16)
            vn = u_ref[us] - _bdot(w16, s016)
            vn16 = vn.astype(bf16)
            o = _bdot(qd16, s016) + _bdot(at16, vn16)
            do_h = []
            for h in range(GDN_HEADS):
                oh = o[h]
                lanes = slice(h * GDN_HD, (h + 1) * GDN_HD)
                gt = gate_ref[rows, lanes]
                d_out = do_ref[rows, lanes]
                r = lax.rsqrt(jnp.mean(oh * oh, axis=-1, keepdims=True) + RMS_EPS)
                n = oh * r
                sg = _sigmoid(gt)
                silu = gt * sg
                dog = dog + jnp.sum(d_out * n * silu, axis=0, keepdims=True)
                dgate_ref[rows, lanes] = (d_out * n * og * (sg * (1.0 + gt * (1.0 - sg)))).astype(bf16)
                dn = d_out * og * silu
                do_h.append(r * (dn - n * jnp.mean(dn * n, axis=-1, keepdims=True)))
            do16 = jnp.stack(do_h).astype(bf16)
            ds16 = ds.astype(bf16)
            dvn = _bdot_tn(at16, do16) + _bdot(kd16, ds16)
            dvn16 = dvn.astype(bf16)
            du_ref[us] = dvn
            dw_ref[us] = -_bdot_nt(dvn16, s016)
            dat_ref[us] = jnp.where(tril, _bdot_nt(do16, vn16), 0.0)
            dqd_ref[us] = _bdot_nt(do16, s016)
            dkd_ref[us] = _bdot_nt(vn16, ds16)
            dgm_ref[us] = jnp.sum(s0 * ds, axis=1, keepdims=True)
            ds = _bdot_tn(qd16, do16) + ds * gm_ref[us] - _bdot_tn(w16, dvn16)
        ds_scr[...] = ds
        dog_ref[...] += dog

    def unit(*tail):
        nd = len(tail)
        return pl.BlockSpec((GDN_NU,) + tail, lambda i: (n_steps - 1 - i,) + (0,) * nd)

    intra_specs = [unit(CHUNK, GDN_HD), unit(CHUNK, GDN_HD), unit(CHUNK, CHUNK), unit(CHUNK, GDN_HD),
                   unit(CHUNK, GDN_HD), unit(1, GDN_HD)]
    tok = lambda c: pl.BlockSpec((GDN_TB, GDN_W), lambda i: (n_steps - 1 - i, c))
    return pl.pallas_call(
        body, name="gdn_scan_bwd", grid=(n_steps,),
        in_specs=intra_specs + [unit(GDN_HD, GDN_HD), tok(3), tok(0), _const_spec((1, GDN_HD)), _ANY],
        out_specs=intra_specs + [tok(0), _const_spec((1, GDN_HD))],
        out_shape=_gdn_intra_shapes(S) + [SDS((S, GDN_W), bf16), SDS((1, GDN_HD), f32)],
        scratch_shapes=[pltpu.VMEM((GDN_HEADS, GDN_HD, GDN_HD), f32)],
        compiler_params=_cp(1, VMEM_LIMIT),
    )(*intra, states, proj, d_oab, onorm_g, after)


SWA_UNROLL = 4


def _swa_tiles(q_ref, k_ref, v_ref, it, d, nb_log2, S):
    nb = 1 << nb_log2
    r = lax.shift_right_logical(it, nb_log2)
    blk = lax.bitwise_and(it, nb - 1)
    qs = blk * (SWA_BLK * d) + r
    ps = jnp.maximum(blk - 1, 0) * (SWA_BLK * d) + r
    if d > 1:
        rows_c, rows_p = pl.ds(qs, SWA_BLK, stride=d), pl.ds(ps, SWA_BLK, stride=d)
    else:
        rows_c, rows_p = pl.ds(pl.multiple_of(qs, SWA_BLK), SWA_BLK), pl.ds(pl.multiple_of(ps, SWA_BLK), SWA_BLK)
    return rows_c, rows_p, blk > 0


def _swa_fwd(proj, bt, S):
    scale = SWA_HD ** -0.5

    def body(q_ref, k_ref, v_ref, bt_ref, o_ref, lse_ref, m_scr, l_scr, acc_scr):
        lane = lax.broadcasted_iota(jnp.int32, (SWA_BLK, LANE), 1)
        h0 = lane < SWA_HD
        m_scr[...] = jnp.full((S, LANE), NEG, f32)
        l_scr[...] = jnp.zeros((S, LANE), f32)
        acc_scr[...] = jnp.zeros((S, LANE), f32)
        for p, (_, d) in enumerate(PATTERNS):
            nb_log2 = int(math.log2(S // d // SWA_BLK))

            def tile(it, p=p, d=d, nb_log2=nb_log2):
                rows_c, rows_p, has_prev = _swa_tiles(q_ref, k_ref, v_ref, it, d, nb_log2, S)
                q = q_ref[rows_c, :]
                kc = k_ref[rows_c, :].astype(bf16)
                vc = v_ref[rows_c, :].astype(bf16)
                kp = k_ref[rows_p, :].astype(bf16)
                vp = v_ref[rows_p, :].astype(bf16)
                m_old = m_scr[rows_c, :]
                l_old = l_scr[rows_c, :]
                acc_old = acc_scr[rows_c, :]
                res = []
                for h in range(2):
                    mh = h0 if h == 0 else jnp.logical_not(h0)
                    qh = jnp.where(mh, q, 0.0).astype(bf16)
                    s_c = _dot_nt(qh, kc) * scale + bt_ref[p, h, 1]
                    s_p = jnp.where(has_prev, _dot_nt(qh, kp) * scale + bt_ref[p, h, 0], NEG)
                    mo = m_old[:, h * SWA_HD:h * SWA_HD + 1]
                    lo = l_old[:, h * SWA_HD:h * SWA_HD + 1]
                    mn = jnp.maximum(mo, jnp.maximum(jnp.max(s_c, axis=1, keepdims=True), jnp.max(s_p, axis=1, keepdims=True)))
                    alpha = jnp.exp(mo - mn)
                    p_c = jnp.exp(s_c - mn)
                    p_p = jnp.exp(s_p - mn)
                    ln = alpha * lo + jnp.sum(p_c, axis=1, keepdims=True) + jnp.sum(p_p, axis=1, keepdims=True)
                    pv = _dot(p_c.astype(bf16), vc) + _dot(p_p.astype(bf16), vp)
                    res.append((mn, ln, alpha, pv))
                return (rows_c, jnp.where(h0, res[0][0], res[1][0]), jnp.where(h0, res[0][1], res[1][1]),
                        acc_old * jnp.where(h0, res[0][2], res[1][2]) + jnp.where(h0, res[0][3], res[1][3]))

            def lp(i, carry, tile=tile):
                done = [tile(i * SWA_UNROLL + u) for u in range(SWA_UNROLL)]
                for rows_c, m_new, l_new, acc_new in done:
                    m_scr[rows_c, :] = m_new
                    l_scr[rows_c, :] = l_new
                    acc_scr[rows_c, :] = acc_new
                return carry

            lax.fori_loop(0, S // SWA_BLK // SWA_UNROLL, lp, 0)
        l_all = l_scr[...]
        o_ref[...] = acc_scr[...] / l_all
        lse_ref[...] = m_scr[...] + jnp.log(l_all)

    qb = COL_B // LANE
    return pl.pallas_call(
        body, name="swa_fwd", grid=(4,),
        in_specs=[pl.BlockSpec((S, LANE), lambda hp: (0, qb + hp)), pl.BlockSpec((S, LANE), lambda hp: (0, qb + 4 + hp)),
                  pl.BlockSpec((S, LANE), lambda hp: (0, qb + 8 + hp)),
                  pl.BlockSpec((3, 2, 2, SWA_BLK, SWA_BLK), lambda hp: (0, hp, 0, 0, 0))],
        out_specs=[pl.BlockSpec((S, LANE), lambda hp: (0, hp)), pl.BlockSpec((S, LANE), lambda hp: (0, hp))],
        out_shape=[SDS((S, SWA_W), f32), SDS((S, SWA_W), f32)],
        scratch_shapes=[pltpu.VMEM((S, LANE), f32)] * 3,
        compiler_params=_cp(1, VMEM_LIMIT),
    )(proj, proj, proj, bt)


def _swa_bwd(proj, bt, ob, lse, d_oab, after, S):
    scale = SWA_HD ** -0.5

    def body(q_ref, k_ref, v_ref, bt_ref, o_ref, lse_ref, do_ref, after_ref, dq_ref, dk_ref, dv_ref, dsb_ref,
             dq_scr, dk_scr, dv_scr):
        lane = lax.broadcasted_iota(jnp.int32, (SWA_BLK, LANE), 1)
        h0 = lane < SWA_HD
        dq_scr[...] = jnp.zeros((S, LANE), f32)
        dk_scr[...] = jnp.zeros((S, LANE), f32)
        dv_scr[...] = jnp.zeros((S, LANE), f32)
        dsb_ref[...] = jnp.zeros_like(dsb_ref)
        for p, (_, d) in enumerate(PATTERNS):
            nb_log2 = int(math.log2(S // d // SWA_BLK))

            def tile(it, p=p, d=d, nb_log2=nb_log2):
                rows_c, rows_p, has_prev = _swa_tiles(q_ref, k_ref, v_ref, it, d, nb_log2, S)
                q = q_ref[rows_c, :]
                kc = k_ref[rows_c, :].astype(bf16)
                vc = v_ref[rows_c, :].astype(bf16)
                kp = k_ref[rows_p, :].astype(bf16)
                vp = v_ref[rows_p, :].astype(bf16)
                do = do_ref[rows_c, :]
                lse_t = lse_ref[rows_c, :]
                dlt = do * o_ref[rows_c, :]
                dq_t = jnp.zeros((SWA_BLK, LANE), f32)
                dkc_t = jnp.zeros((SWA_BLK, LANE), f32)
                dkp_t = jnp.zeros((SWA_BLK, LANE), f32)
                dvc_t = jnp.zeros((SWA_BLK, LANE), f32)
                dvp_t = jnp.zeros((SWA_BLK, LANE), f32)
                ds_all = []
                for h in range(2):
                    mh = h0 if h == 0 else jnp.logical_not(h0)
                    qh = jnp.where(mh, q, 0.0).astype(bf16)
                    doh = jnp.where(mh, do, 0.0).astype(bf16)
                    delta = jnp.sum(jnp.where(mh, dlt, 0.0), axis=1, keepdims=True)
                    lse_h = lse_t[:, h * SWA_HD:h * SWA_HD + 1]
                    s_c = _dot_nt(qh, kc) * scale + bt_ref[p, h, 1]
                    s_p = jnp.where(has_prev, _dot_nt(qh, kp) * scale + bt_ref[p, h, 0], NEG)
                    p_c = jnp.exp(s_c - lse_h)
                    p_p = jnp.exp(s_p - lse_h)
                    ds_c = p_c * (_dot_nt(doh, vc) - delta)
                    ds_p = p_p * (_dot_nt(doh, vp) - delta)
                    ds_all.append((ds_p, ds_c))
                    dsc16 = ds_c.astype(bf16)
                    dsp16 = ds_p.astype(bf16)
                    dq_t = dq_t + jnp.where(mh, (_dot(dsc16, kc) + _dot(dsp16, kp)) * scale, 0.0)
                    dkc_t = dkc_t + _dot_tn(dsc16, qh) * scale
                    dkp_t = dkp_t + _dot_tn(dsp16, qh) * scale
                    dvc_t = dvc_t + _dot_tn(p_c.astype(bf16), doh)
                    dvp_t = dvp_t + _dot_tn(p_p.astype(bf16), doh)
                return rows_c, rows_p, dq_t, dkc_t, dkp_t, dvc_t, dvp_t, ds_all

            def lp(i, carry, tile=tile, p=p):
                done = [tile(i * SWA_UNROLL + u) for u in range(SWA_UNROLL)]
                for h in range(2):
                    for t in range(2):
                        tot = done[0][7][h][t]
                        for dn in done[1:]:
                            tot = tot + dn[7][h][t]
                        dsb_ref[0, p, h, t] += tot
                for rows_c, rows_p, dq_t, dkc_t, dkp_t, dvc_t, dvp_t, _ in done:
                    dq_scr[rows_c, :] = dq_scr[rows_c, :] + dq_t
                    dk_scr[rows_c, :] = dk_scr[rows_c, :] + dkc_t
                    dv_scr[rows_c, :] = dv_scr[rows_c, :] + dvc_t
                    dk_scr[rows_p, :] = dk_scr[rows_p, :] + dkp_t
                    dv_scr[rows_p, :] = dv_scr[rows_p, :] + dvp_t
                return carry

            lax.fori_loop(0, S // SWA_BLK // SWA_UNROLL, lp, 0)
        dq_ref[...] = dq_scr[...].astype(bf16)
        dk_ref[...] = dk_scr[...].astype(bf16)
        dv_ref[...] = dv_scr[...].astype(bf16)

    qb = COL_B // LANE
    col = lambda c: pl.BlockSpec((S, LANE), lambda hp, c=c: (0, c + hp))
    return pl.pallas_call(
        body, name="swa_bwd", grid=(4,),
        in_specs=[col(qb), col(qb + 4), col(qb + 8),
                  pl.BlockSpec((3, 2, 2, SWA_BLK, SWA_BLK), lambda hp: (0, hp, 0, 0, 0)),
                  col(0), col(0), col(4), _ANY],
        out_specs=[col(0), col(0), col(0),
                   pl.BlockSpec((1, 3, 2, 2, SWA_BLK, SWA_BLK), lambda hp: (hp, 0, 0, 0, 0, 0))],
        out_shape=[SDS((S, SWA_W), bf16)] * 3 + [SDS((4, 3, 2, 2, SWA_BLK, SWA_BLK), f32)],
        scratch_shapes=[pltpu.VMEM((S, LANE), f32)] * 3,
        compiler_params=_cp(1, VMEM_LIMIT),
    )(proj, proj, proj, bt, ob, lse, d_oab, after)


def _mix_fwd(oa, ob, w_out, x, g_post, S):
    TS = 512

    def body(oa_ref, ob_ref, w_ref, x_ref, g_ref, mix_ref, x1_ref):
        mix = _dot(oa_ref[...].astype(bf16), w_ref[0:GDN_W, :]) + _dot(ob_ref[...].astype(bf16), w_ref[GDN_W:D_MODEL, :])
        r = lax.rsqrt(jnp.mean(mix * mix, axis=-1, keepdims=True) + RMS_EPS)
        mix_ref[...] = mix
        x1_ref[...] = x_ref[...] + mix * r * g_ref[...]

    row = lambda w: pl.BlockSpec((TS, w), lambda i: (i, 0))
    return pl.pallas_call(
        body, name="mix_fwd", grid=(S // TS,),
        in_specs=[row(GDN_W), row(SWA_W), _resident_spec((D_MODEL, D_MODEL)), row(D_MODEL), _const_spec((1, D_MODEL))],
        out_specs=[row(D_MODEL), row(D_MODEL)],
        out_shape=[SDS((S, D_MODEL), f32), SDS((S, D_MODEL), f32)],
        compiler_params=_cp(1, VMEM_LIMIT),
    )(oa, ob, w_out, x, g_post)


def _mix_bwd(dx1, mix, g_post, w_out, S):
    TS = 512

    def body(dx1_ref, mix_ref, g_ref, w_ref, dmix_ref, doab_ref, dg_ref):
        @pl.when(pl.program_id(0) == 0)
        def _():
            dg_ref[...] = jnp.zeros_like(dg_ref)

        mix = mix_ref[...]
        dz = dx1_ref[...]
        r = lax.rsqrt(jnp.mean(mix * mix, axis=-1, keepdims=True) + RMS_EPS)
        n = mix * r
        dg_ref[...] += jnp.sum(dz * n, axis=0, keepdims=True)
        dn = dz * g_ref[...]
        dmix = (r * (dn - n * jnp.mean(dn * n, axis=-1, keepdims=True))).astype(bf16)
        dmix_ref[...] = dmix
        doab_ref[...] = _dot_nt(dmix, w_ref[...])

    row = lambda: pl.BlockSpec((TS, D_MODEL), lambda i: (i, 0))
    return pl.pallas_call(
        body, name="mix_bwd", grid=(S // TS,),
        in_specs=[row(), row(), _const_spec((1, D_MODEL)), _resident_spec((D_MODEL, D_MODEL))],
        out_specs=[row(), row(), _const_spec((1, D_MODEL))],
        out_shape=[SDS((S, D_MODEL), bf16), SDS((S, D_MODEL), f32), SDS((1, D_MODEL), f32)],
        compiler_params=_cp(1, VMEM_LIMIT),
    )(dx1, mix, g_post, w_out)


FFN_TS = 256
FFN_CH = 1408


def _ffn(x1, tgt, g_pre, g_post, wg, wu, wd, S):
    def body(x1_ref, t_ref, gp_ref, gq_ref, wg_ref, wu_ref, wd_ref,
             dx1_ref, h2_ref, act_ref, dgate_ref, dup_ref, df_ref, loss_ref, dgp_ref, dgq_ref, gate_scr, up_scr):
        @pl.when(pl.program_id(0) == 0)
        def _():
            loss_ref[...] = jnp.zeros_like(loss_ref)
            dgp_ref[...] = jnp.zeros_like(dgp_ref)
            dgq_ref[...] = jnp.zeros_like(dgq_ref)

        x1v = x1_ref[...]
        gp = gp_ref[...]
        gq = gq_ref[...]
        r2 = lax.rsqrt(jnp.mean(x1v * x1v, axis=-1, keepdims=True) + RMS_EPS)
        n2 = x1v * r2
        h2 = (n2 * gp).astype(bf16)
        h2_ref[...] = h2
        f = jnp.zeros((FFN_TS, D_MODEL), f32)
        for c in range(D_FF // FFN_CH):
            cs = slice(c * FFN_CH, (c + 1) * FFN_CH)
            gate = _dot(h2, wg_ref[:, cs])
            up = _dot(h2, wu_ref[:, cs])
            gate_scr[:, cs] = gate
            up_scr[:, cs] = up
            act = (gate * _sigmoid(gate) * up).astype(bf16)
            act_ref[:, cs] = act
            f = f + _dot(act, wd_ref[cs, :])
        r3 = lax.rsqrt(jnp.mean(f * f, axis=-1, keepdims=True) + RMS_EPS)
        n3 = f * r3
        err = x1v + n3 * gq - t_ref[...]
        loss_ref[...] += 0.5 * jnp.sum(jnp.mean(err * err, axis=-1, keepdims=True), axis=0, keepdims=True)
        dy = err * (1.0 / D_MODEL)
        dgq_ref[...] += jnp.sum(dy * n3, axis=0, keepdims=True)
        dn3 = dy * gq
        df = (r3 * (dn3 - n3 * jnp.mean(dn3 * n3, axis=-1, keepdims=True))).astype(bf16)
        df_ref[...] = df
        dh2 = jnp.zeros((FFN_TS, D_MODEL), f32)
        for c in range(D_FF // FFN_CH):
            cs = slice(c * FFN_CH, (c + 1) * FFN_CH)
            gate = gate_scr[:, cs]
            up = up_scr[:, cs]
            dact = _dot_nt(df, wd_ref[cs, :])
            sg = _sigmoid(gate)
            dup = (dact * gate * sg).astype(bf16)
            dgate = (dact * up * (sg * (1.0 + gate * (1.0 - sg)))).astype(bf16)
            dup_ref[:, cs] = dup
            dgate_ref[:, cs] = dgate
            dh2 = dh2 + _dot_nt(dgate, wg_ref[:, cs]) + _dot_nt(dup, wu_ref[:, cs])
        dgp_ref[...] += jnp.sum(dh2 * n2, axis=0, keepdims=True)
        dn2 = dh2 * gp
        dx1_ref[...] = dy + r2 * (dn2 - n2 * jnp.mean(dn2 * n2, axis=-1, keepdims=True))

    row = lambda w: pl.BlockSpec((FFN_TS, w), lambda i: (i, 0))
    vec = _const_spec((1, D_MODEL))
    return pl.pallas_call(
        body, name="ffn_fwd_bwd", grid=(S // FFN_TS,),
        in_specs=[row(D_MODEL), row(D_MODEL), vec, vec, _resident_spec((D_MODEL, D_FF)), _resident_spec((D_MODEL, D_FF)),
                  _resident_spec((D_FF, D_MODEL))],
        out_specs=[row(D_MODEL), row(D_MODEL), row(D_FF), row(D_FF), row(D_FF), row(D_MODEL), _const_spec((1, LANE)), vec, vec],
        out_shape=[SDS((S, D_MODEL), f32), SDS((S, D_MODEL), bf16), SDS((S, D_FF), bf16), SDS((S, D_FF), bf16),
                   SDS((S, D_FF), bf16), SDS((S, D_MODEL), bf16), SDS((1, LANE), f32), SDS((1, D_MODEL), f32),
                   SDS((1, D_MODEL), f32)],
        scratch_shapes=[pltpu.VMEM((FFN_TS, D_FF), f32), pltpu.VMEM((FFN_TS, D_FF), f32)],
        compiler_params=_cp(1, VMEM_LIMIT),
    )(x1, tgt, g_pre, g_post, wg, wu, wd)


def _proj_bwd(x, dx1, g_pre, wcat, segs, S):
    TS = 256
    n = len(segs)
    cols = [(c0, a.shape[1]) for a, c0 in segs]

    def body(*refs):
        x_ref, dx1_ref, g_ref, w_ref = refs[:4]
        seg_refs = refs[4:4 + n]
        gx_ref, h1_ref, dg_ref = refs[4 + n:]

        @pl.when(pl.program_id(0) == 0)
        def _():
            dg_ref[...] = jnp.zeros_like(dg_ref)

        dh = jnp.zeros((TS, D_MODEL), f32)
        for s_ref, (c0, w) in zip(seg_refs, cols):
            dh = dh + _dot_nt(s_ref[...], w_ref[:, c0:c0 + w])
        xv = x_ref[...]
        g = g_ref[...]
        r = lax.rsqrt(jnp.mean(xv * xv, axis=-1, keepdims=True) + RMS_EPS)
        nx = xv * r
        h1_ref[...] = (nx * g).astype(bf16)
        dg_ref[...] += jnp.sum(dh * nx, axis=0, keepdims=True)
        dn = dh * g
        gx_ref[...] = dx1_ref[...] + r * (dn - nx * jnp.mean(dn * nx, axis=-1, keepdims=True))

    row = lambda w: pl.BlockSpec((TS, w), lambda i: (i, 0))
    return pl.pallas_call(
        body, name="proj_bwd", grid=(S // TS,),
        in_specs=[row(D_MODEL), row(D_MODEL), _const_spec((1, D_MODEL)), _resident_spec((D_MODEL, NCOL))]
                 + [row(w) for _, w in cols],
        out_specs=[row(D_MODEL), row(D_MODEL), _const_spec((1, D_MODEL))],
        out_shape=[SDS((S, D_MODEL), f32), SDS((S, D_MODEL), bf16), SDS((1, D_MODEL), f32)],
        compiler_params=_cp(1, VMEM_LIMIT),
    )(x, dx1, g_pre, wcat, *[a for a, _ in segs])


def _wgrad(a, b, S, name):
    TS = 512
    K = a.shape[1]
    N = b.shape[1]
    TN = next(t for t in (512, 1408, N) if N % t == 0)

    def body(a_ref, b_ref, o_ref):
        @pl.when(pl.program_id(1) == 0)
        def _():
            o_ref[...] = jnp.zeros_like(o_ref)

        o_ref[...] += _dot_tn(a_ref[...].astype(bf16), b_ref[...])

    return pl.pallas_call(
        body, name=name, grid=(N // TN, S // TS),
        in_specs=[pl.BlockSpec((TS, K), lambda j, s: (s, 0)), pl.BlockSpec((TS, TN), lambda j, s: (s, j))],
        out_specs=pl.BlockSpec((K, TN), lambda j, s: (0, j)),
        out_shape=SDS((K, N), f32),
        compiler_params=_cp(2, VMEM_LIMIT),
    )(a, b)


def _adamw(recv, w, m, v, name):
    R, C = w.shape
    TR = 256 if R % 256 == 0 else R
    c1 = 1.0 / (1.0 - ADAM_B1 ** ADAM_STEP)
    c2 = 1.0 / (1.0 - ADAM_B2 ** ADAM_STEP)

    def body(r_ref, w_ref, m_ref, v_ref, g_out, d_out, m_out, v_out):
        g = r_ref[0].astype(f32)
        for s in range(1, N_DEV):
            g = g + r_ref[s].astype(f32)
        mn = ADAM_B1 * m_ref[...] + (1.0 - ADAM_B1) * g
        vn = ADAM_B2 * v_ref[...] + (1.0 - ADAM_B2) * (g * g)
        g_out[...] = g
        m_out[...] = mn
        v_out[...] = vn
        d_out[...] = -ADAM_LR * ((mn * c1) / (jnp.sqrt(vn * c2) + ADAM_EPS) + ADAM_WD * w_ref[...])

    blk = pl.BlockSpec((TR, C), lambda i: (i, 0))
    return pl.pallas_call(
        body, name=name, grid=(R // TR,),
        in_specs=[pl.BlockSpec((N_DEV, TR, C), lambda i: (0, i, 0)), blk, blk, blk],
        out_specs=[blk, blk, blk, blk],
        out_shape=[SDS((R, C), f32)] * 4,
        compiler_params=_cp(1, VMEM_LIMIT),
    )(recv, w, m, v)


MESH = pl.DeviceIdType.MESH
_ANY = pl.BlockSpec(memory_space=pl.ANY)


def _flip(v, d):
    return 1 - v if d else v


def _all_gather(shards):
    n = len(shards)

    def body(*refs):
        ins = refs[:n]
        outs = refs[n:2 * n]
        send_sems, recv_sems, local_sems = refs[2 * n:]
        x, y, c = lax.axis_index("x"), lax.axis_index("y"), lax.axis_index("c")
        me, sibling = (x, y, c), (x, y, 1 - c)
        chips = [(1 - x, y), (x, 1 - y), (1 - x, 1 - y)]

        def slot(px, py, pc):
            return 4 * px + 2 * py + pc

        def copy(a, k, block, to, src=None):
            dst = outs[a].at[slot(*block)]
            return pltpu.make_async_remote_copy(src_ref=dst if src is None else src, dst_ref=dst,
                                                send_sem=send_sems.at[a, k], recv_sem=recv_sems.at[a, k],
                                                device_id=to, device_id_type=MESH)

        mine, first, passed = [], [], []
        for a in range(n):
            cp = pltpu.make_async_copy(ins[a], outs[a].at[slot(*me)], local_sems.at[a])
            cp.start()
            mine.append(cp)
            fs = [copy(a, 0, me, sibling, src=ins[a])]
            fs += [copy(a, 1 + j, me, (*chip, c), src=ins[a]) for j, chip in enumerate(chips)]
            for cp in fs:
                cp.start()
            first += fs
        for j, chip in enumerate(chips):
            for a in range(n):
                copy(a, 1 + j, (*chip, c), me).wait_recv()
                cp = copy(a, 4 + j, (*chip, c), sibling)
                cp.start()
                passed.append(cp)
        for a in range(n):
            copy(a, 0, sibling, me).wait_recv()
            for j, chip in enumerate(chips):
                copy(a, 4 + j, (*chip, 1 - c), me).wait_recv()
        for cp in first + passed:
            cp.wait_send()
        for cp in mine:
            cp.wait()

    return pl.pallas_call(
        body, name="weight_all_gather",
        in_specs=[_ANY] * n, out_specs=[_ANY] * n,
        out_shape=[SDS((N_DEV,) + s.shape, s.dtype) for s in shards],
        scratch_shapes=[pltpu.SemaphoreType.DMA((n, 7)), pltpu.SemaphoreType.DMA((n, 7)), pltpu.SemaphoreType.DMA((n,))],
        compiler_params=pltpu.CompilerParams(has_side_effects=True),
    )(*shards)


def _grad_exchange(blocked, whole):
    arrs = list(blocked) + list(whole)
    n, nb = len(arrs), len(blocked)
    rel = [(dx, dy, dc) for dx in (0, 1) for dy in (0, 1) for dc in (0, 1) if dx or dy or dc]

    def body(*refs):
        ins = refs[:n]
        outs = refs[n:2 * n]
        send_sems, recv_sems, local_sems = refs[2 * n:]
        x, y, c = lax.axis_index("x"), lax.axis_index("y"), lax.axis_index("c")
        me = 4 * x + 2 * y + c
        sends, locs = [], []
        for a in range(n):
            cp = pltpu.make_async_copy(ins[a].at[me] if a < nb else ins[a], outs[a].at[me], local_sems.at[a])
            cp.start()
            locs.append(cp)
            for k, (dx, dy, dc) in enumerate(rel):
                peer = (_flip(x, dx), _flip(y, dy), _flip(c, dc))
                pidx = 4 * peer[0] + 2 * peer[1] + peer[2]
                cp = pltpu.make_async_remote_copy(src_ref=ins[a].at[pidx] if a < nb else ins[a], dst_ref=outs[a].at[me],
                                                  send_sem=send_sems.at[a, k], recv_sem=recv_sems.at[a, k],
                                                  device_id=peer, device_id_type=MESH)
                cp.start()
                sends.append(cp)
        for a in range(n):
            for k, (dx, dy, dc) in enumerate(rel):
                peer = (_flip(x, dx), _flip(y, dy), _flip(c, dc))
                pidx = 4 * peer[0] + 2 * peer[1] + peer[2]
                pltpu.make_async_remote_copy(src_ref=outs[a].at[pidx], dst_ref=outs[a].at[pidx],
                                             send_sem=send_sems.at[a, k], recv_sem=recv_sems.at[a, k],
                                             device_id=peer, device_id_type=MESH).wait_recv()
        for cp in sends:
            cp.wait_send()
        for cp in locs:
            cp.wait()

    shapes = [SDS(a.shape, a.dtype) for a in blocked] + [SDS((N_DEV,) + a.shape, a.dtype) for a in whole]
    return pl.pallas_call(
        body, name="grad_exchange",
        in_specs=[_ANY] * n, out_specs=[_ANY] * n, out_shape=shapes,
        scratch_shapes=[pltpu.SemaphoreType.DMA((n, 7)), pltpu.SemaphoreType.DMA((n, 7)), pltpu.SemaphoreType.DMA((n,))],
        compiler_params=pltpu.CompilerParams(has_side_effects=True),
    )(*arrs)


_HBM = pl.BlockSpec(memory_space=pltpu.HBM)
_SEM = pl.BlockSpec(memory_space=pltpu.SEMAPHORE)
_REL = [(dx, dy, dc) for dx in (0, 1) for dy in (0, 1) for dc in (0, 1) if dx or dy or dc]


N_PEER = len(_REL)
_EFFECT = pltpu.SideEffectType.DATAFLOW_SIDE_EFFECTING


def _peer_copies(src, land, send_sems, recv_sems, blocked, as_receiver):
    x, y, c = lax.axis_index("x"), lax.axis_index("y"), lax.axis_index("c")
    me = 4 * x + 2 * y + c
    cps = []
    for k, (dx, dy, dc) in enumerate(_REL):
        peer = (_flip(x, dx), _flip(y, dy), _flip(c, dc))
        pidx = 4 * peer[0] + 2 * peer[1] + peer[2]
        cps.append(pltpu.make_async_remote_copy(
            src_ref=src.at[pidx] if blocked else src, dst_ref=land.at[pidx if as_receiver else me],
            send_sem=send_sems[k], recv_sem=recv_sems[k], device_id=peer, device_id_type=MESH))
    return cps


def _exchange_start(src, after, blocked, name):
    land = lax.empty(src.shape if blocked else (N_DEV,) + src.shape, src.dtype)

    def body(src_ref, land_ref, after_ref, *outs):
        send_sems, recv_sems = outs[:N_PEER], outs[N_PEER:2 * N_PEER]
        token = outs[-1]
        for cp in _peer_copies(src_ref, land_ref, send_sems, recv_sems, blocked, False):
            cp.start()
        token[...] = jnp.zeros_like(token)

    res = pl.pallas_call(
        body, name=name,
        in_specs=[_HBM, _HBM, _ANY],
        out_specs=[_SEM] * (2 * N_PEER) + [_HBM, _HBM, pl.BlockSpec(memory_space=pltpu.VMEM)],
        out_shape=[pltpu.SemaphoreType.DMA(())] * (2 * N_PEER)
                  + [pltpu.HBM(src.shape, src.dtype), pltpu.HBM(land.shape, land.dtype), SDS((8, LANE), f32)],
        input_output_aliases={0: 2 * N_PEER, 1: 2 * N_PEER + 1},
        compiler_params=pltpu.CompilerParams(has_side_effects=_EFFECT),
    )(pltpu.with_memory_space_constraint(src, pltpu.HBM), pltpu.with_memory_space_constraint(land, pltpu.HBM), after)
    return list(res[:2 * N_PEER]), res[2 * N_PEER], res[2 * N_PEER + 1], res[-1]


def _exchange_wait(sems, src, land, after, blocked, name):
    def body(src_ref, land_ref, *rest):
        sem_refs = rest[:2 * N_PEER]
        for cp in _peer_copies(src_ref, land_ref, sem_refs[:N_PEER], sem_refs[N_PEER:], blocked, True):
            cp.wait_send()
            cp.wait_recv()

    return pl.pallas_call(
        body, name=name,
        in_specs=[_HBM, _HBM] + [_SEM] * (2 * N_PEER) + [_ANY],
        out_specs=[_HBM, _HBM],
        out_shape=[pltpu.HBM(src.shape, src.dtype), pltpu.HBM(land.shape, land.dtype)],
        input_output_aliases={0: 0, 1: 1},
        compiler_params=pltpu.CompilerParams(has_side_effects=_EFFECT),
    )(src, land, *sems, after)[1]


def _local_step(x, tgt, wcat, convw, late_weights, early_grads, token, a_log, dt_bias, onorm_g, rel_bias,
                g_mix_pre, g_mix_post, g_ffn_pre, g_ffn_post):
    S = x.shape[0]
    bk_np = _bucket_tables()
    bk = jnp.asarray(bk_np)
    bt = _bias_tables(rel_bias, bk)
    proj = _proj_fwd(x, g_mix_pre, wcat, token, S)
    nu = S // CHUNK * GDN_HEADS
    qkv_u = _gdn_prep(proj, convw, S).reshape(3, nu, CHUNK, GDN_HD)
    intra, t_inv = _gdn_intra_fwd(qkv_u, proj, a_log, dt_bias, S)
    oa, states = _gdn_scan_fwd(intra, proj, onorm_g, S)
    ob, lse = _swa_fwd(proj, bt, S)
    wout, wgate, wup, wdown = late_weights(ob)
    mix, x1 = _mix_fwd(oa, ob, wout, x, g_mix_post, S)
    dx1, h2, act, dgate_f, dup_f, df, loss, d_gfpre, d_gfpost = _ffn(x1, tgt, g_ffn_pre, g_ffn_post, wgate, wup, wdown, S)
    g_gate = _wgrad(h2, dgate_f, S, "wgrad_gate")
    g_up = _wgrad(h2, dup_f, S, "wgrad_up")
    g_down = _wgrad(act, df, S, "wgrad_down")
    dmix, d_oab, d_gmpost = _mix_bwd(dx1, mix, g_mix_post, wout, S)
    g_out = jnp.concatenate([_wgrad(oa, dmix, S, "wgrad_out_a"), _wgrad(ob, dmix, S, "wgrad_out_b")], axis=0)
    token = early_grads(g_out, g_gate, g_up, g_down)
    dqb, dkb, dvb, dsb = _swa_bwd(proj, bt, ob, lse, d_oab, token, S)
    *cots, dgate_a, d_og = _gdn_scan_bwd(intra, states, proj, d_oab, onorm_g, token, S)
    dqkv_u, dpg, d_alog, d_dtb = _gdn_intra_bwd(qkv_u, proj, a_log, dt_bias, t_inv, cots, S)
    dqkv_a, d_conv = _gdn_prep_bwd(proj, convw, dqkv_u.reshape(3, S // CHUNK, GDN_HEADS, CHUNK, GDN_HD), S)
    segs = [(dqkv_a, COL_A), (dgate_a, COL_A + 3 * GDN_W), (dqb, COL_B), (dkb, COL_B + SWA_W), (dvb, COL_B + 2 * SWA_W),
            (dpg, COL_G)]
    grad_x, h1, d_gmpre = _proj_bwd(x, dx1, g_mix_pre, wcat, segs, S)
    g_segs = [_wgrad(h1, a, S, "wgrad_in_%d" % i) for i, (a, _) in enumerate(segs)]
    g_in = jnp.concatenate([g_segs[0], g_segs[1], g_segs[5][:, :2 * GDN_HEADS], g_segs[2], g_segs[3], g_segs[4]], axis=1)
    d_rel = _rel_bias_grad(dsb, bk, bk_np)
    small = dict(a_log=d_alog[:, :GDN_HEADS], dt_bias=d_dtb[:, :GDN_HEADS], onorm_g=d_og, rel_bias=d_rel,
                 g_mix_pre=d_gmpre, g_mix_post=d_gmpost, g_ffn_pre=d_gfpre, g_ffn_post=d_gfpost)
    return loss, grad_x, (g_in, d_conv), small


SMALL = ("a_log", "dt_bias", "onorm_g", "rel_bias", "g_mix_pre", "g_mix_post", "g_ffn_pre", "g_ffn_post")
PACK_ROWS = 8


def _pack_small(d, loss=None):
    rest = jnp.concatenate([d["onorm_g"].reshape(-1), d["a_log"].reshape(-1), d["dt_bias"].reshape(-1),
                            d["rel_bias"].reshape(-1)])
    rest = jnp.concatenate([rest, jnp.zeros((D_MODEL - rest.shape[0],), f32)])
    extra = jnp.zeros((D_MODEL,), f32) if loss is None else jnp.concatenate([loss.reshape(1), jnp.zeros((D_MODEL - 1,), f32)])
    rows = [d["g_mix_pre"].reshape(-1), d["g_mix_post"].reshape(-1), d["g_ffn_pre"].reshape(-1),
            d["g_ffn_post"].reshape(-1), rest, extra]
    return jnp.concatenate([jnp.stack(rows), jnp.zeros((PACK_ROWS - len(rows), D_MODEL), f32)], axis=0)


def _unpack_small(p):
    o = GDN_HD
    return dict(g_mix_pre=p[0:1], g_mix_post=p[1:2], g_ffn_pre=p[2:3], g_ffn_post=p[3:4],
                onorm_g=p[4:5, :o], a_log=p[4:5, o:o + 4], dt_bias=p[4:5, o + 4:o + 8],
                rel_bias=p[4, o + 8:o + 8 + NUM_BUCKETS * SWA_HEADS].reshape(NUM_BUCKETS, SWA_HEADS))


def kernel(x, w_in, conv_w, a_log, dt_bias, onorm_g, rel_bias, w_out, g_mix_pre, g_mix_post, w_gate, w_up, w_down, g_ffn_pre, g_ffn_post, loss_target, m_w_in, m_conv_w, m_a_log, m_dt_bias, m_onorm_g, m_rel_bias, m_w_out, m_g_mix_pre, m_g_mix_post, m_w_gate, m_w_up, m_w_down, m_g_ffn_pre, m_g_ffn_post, v_w_in, v_conv_w, v_a_log, v_dt_bias, v_onorm_g, v_rel_bias, v_w_out, v_g_mix_pre, v_g_mix_post, v_w_gate, v_w_up, v_w_down, v_g_ffn_pre, v_g_ffn_post):
    big = ("w_in", "conv_w", "w_out", "w_gate", "w_up", "w_down")
    w_sh = dict(w_in=w_in[0], conv_w=conv_w[0], w_out=w_out[0], w_gate=w_gate[0], w_up=w_up[0], w_down=w_down[0])
    m_sh = dict(w_in=m_w_in[0], conv_w=m_conv_w[0], w_out=m_w_out[0], w_gate=m_w_gate[0], w_up=m_w_up[0], w_down=m_w_down[0])
    v_sh = dict(w_in=v_w_in[0], conv_w=v_conv_w[0], w_out=v_w_out[0], w_gate=v_w_gate[0], w_up=v_w_up[0], w_down=v_w_down[0])
    w_small = dict(a_log=a_log, dt_bias=dt_bias, onorm_g=onorm_g, rel_bias=rel_bias, g_mix_pre=g_mix_pre,
                   g_mix_post=g_mix_post, g_ffn_pre=g_ffn_pre, g_ffn_post=g_ffn_post)
    m_small = dict(a_log=m_a_log, dt_bias=m_dt_bias, onorm_g=m_onorm_g, rel_bias=m_rel_bias, g_mix_pre=m_g_mix_pre,
                   g_mix_post=m_g_mix_post, g_ffn_pre=m_g_ffn_pre, g_ffn_post=m_g_ffn_post)
    v_small = dict(a_log=v_a_log, dt_bias=v_dt_bias, onorm_g=v_onorm_g, rel_bias=v_rel_bias, g_mix_pre=v_g_mix_pre,
                   g_mix_post=v_g_mix_post, g_ffn_pre=v_g_ffn_pre, g_ffn_post=v_g_ffn_post)

    me = 4 * lax.axis_index("x") + 2 * lax.axis_index("y") + lax.axis_index("c")
    own = lambda full, part: lax.dynamic_update_index_in_dim(full, part, me, 0)
    cols = lambda g: g.reshape(g.shape[0], N_DEV, g.shape[1] // N_DEV).transpose(1, 0, 2)
    rows = lambda g: g.reshape(N_DEV, g.shape[0] // N_DEV, g.shape[1])
    late = ("w_out", "w_gate", "w_up", "w_down")

    pack = lambda parts: jnp.concatenate([p.reshape(p.shape[:-2] + (-1, D_MODEL)) for p in parts], axis=-2)
    pack_rows = np.cumsum([0] + [w_sh[k].size // D_MODEL for k in late])

    def unpack(buf):
        return [buf[:, pack_rows[i]:pack_rows[i + 1]].reshape((N_DEV,) + w_sh[k].shape) for i, k in enumerate(late)]

    late_src = pack([w_sh[k].astype(bf16) for k in late])
    g_in, g_conv = _all_gather([w_sh["w_in"].astype(bf16), w_sh["conv_w"]])
    g_sems, g_src, g_land, g_token = _exchange_start(late_src, g_conv, False, "late_weights_start")
    wfull = g_in.transpose(1, 0, 2).reshape(D_MODEL, IN_COLS)
    n_a = 4 * GDN_W
    wcat = jnp.concatenate([wfull[:, :n_a], wfull[:, n_a + 2 * GDN_HEADS:], wfull[:, n_a:n_a + 2 * GDN_HEADS],
                            jnp.zeros((D_MODEL, LANE - 2 * GDN_HEADS), bf16)], axis=1)
    convw = g_conv.transpose(1, 0, 2).reshape(4, 3 * GDN_W)

    def late_weights(after):
        land = _exchange_wait(g_sems, g_src, g_land, after, False, "late_weights_wait")
        g_out, g_gate, g_up, g_down = unpack(own(land, late_src))
        return (g_out.reshape(D_MODEL, D_MODEL), g_gate.transpose(1, 0, 2).reshape(D_MODEL, D_FF),
                g_up.transpose(1, 0, 2).reshape(D_MODEL, D_FF), g_down.reshape(D_FF, D_MODEL))

    early = {}

    def early_grads(gw_out, gw_gate, gw_up, gw_down):
        src = pack([rows(gw_out), cols(gw_gate), cols(gw_up), rows(gw_down)]).astype(bf16)
        early["sems"], early["src"], early["land"], token = _exchange_start(src, gw_out, True, "late_grads_start")
        return token

    loss_p, grad_x, (gw_in, gw_conv), gsmall = _local_step(
        x[0], loss_target[0], wcat, convw, late_weights, early_grads, g_token,
        a_log, dt_bias, onorm_g, rel_bias, g_mix_pre, g_mix_post, g_ffn_pre, g_ffn_post)

    r_in, r_conv, r_small = _grad_exchange([cols(gw_in).astype(bf16), cols(gw_conv)], [_pack_small(gsmall, loss_p[0, 0])])
    land = _exchange_wait(early["sems"], early["src"], early["land"], grad_x, True, "late_grads_wait")
    recv = dict(w_in=r_in, conv_w=r_conv)
    recv.update(zip(late, unpack(own(land, lax.dynamic_index_in_dim(early["src"], me, 0, keepdims=False)))))

    outs = {}
    for k in big:
        outs[k] = _adamw(recv[k], w_sh[k], m_sh[k], v_sh[k], "adamw_" + k)
    sm = _adamw(r_small, _pack_small(w_small), _pack_small(m_small), _pack_small(v_small), "adamw_small")
    loss = sm[0][5, 0]
    sm = [_unpack_small(t) for t in sm]
    for k in SMALL:
        outs[k] = tuple(t[k].reshape(w_small[k].shape) for t in sm)

    order = ("w_in", "conv_w", "a_log", "dt_bias", "onorm_g", "rel_bias", "w_out", "g_mix_pre", "g_mix_post", "w_gate",
             "w_up", "w_down", "g_ffn_pre", "g_ffn_post")
    lead = lambda k, t: t[None] if k in big else t
    res = [loss, grad_x[None]]
    for i in range(4):
        res += [lead(k, outs[k][i]) for k in order]
    return tuple(res)
```

```python
import functools
import math

import numpy as np
import jax
import jax.numpy as jnp
from jax import lax
from jax.experimental import pallas as pl
from jax.experimental.pallas import tpu as pltpu

f32 = jnp.float32
bf16 = jnp.bfloat16
SDS = jax.ShapeDtypeStruct

D_MODEL = 1024
GDN_HEADS = 4
GDN_HD = 128
GDN_W = 512
CHUNK = 64
SWA_HEADS = 8
SWA_HD = 64
SWA_W = 512
D_FF = 2816
IN_COLS = 3592
PATTERNS = ((128, 1), (512, 4), (2048, 16))
SWA_BLK = 128
NUM_BUCKETS = 32
MAX_DISTANCE = 2048
RMS_EPS = 1e-6
NEG = -1e30
N_DEV = 8

COL_A = 0
COL_B = 2048
COL_G = 3584
NCOL = 3712
LANE = 128

ADAM_LR, ADAM_B1, ADAM_B2, ADAM_EPS, ADAM_WD, ADAM_STEP = 0.001, 0.9, 0.999, 1e-08, 0.01, 10

VMEM_LIMIT = 56 * 1024 * 1024

HI = lax.Precision.HIGHEST
HIGH = lax.Precision.HIGH


def _cp(n_grid=0, vmem=None):
    kw = {}
    if n_grid:
        kw["dimension_semantics"] = ("arbitrary",) * n_grid
    if vmem:
        kw["vmem_limit_bytes"] = vmem
    return pltpu.CompilerParams(**kw)


def _dot(a, b):
    return jnp.dot(a, b, preferred_element_type=f32)


def _dot_nt(a, b):
    return lax.dot_general(a, b, (((1,), (1,)), ((), ())), preferred_element_type=f32)


def _dot_tn(a, b):
    return lax.dot_general(a, b, (((0,), (0,)), ((), ())), preferred_element_type=f32)


def _dot_hi(a, b):
    return jnp.dot(a, b, precision=HI, preferred_element_type=f32)


def _sigmoid(x):
    return 1.0 / (1.0 + jnp.exp(-x))


def _softplus(x):
    return jnp.maximum(x, 0.0) + jnp.log(1.0 + jnp.exp(-jnp.abs(x)))


def _const_spec(shape):
    nd = len(shape)
    return pl.BlockSpec(shape, lambda *_: (0,) * nd)


def _resident_spec(shape):
    nd = len(shape)
    return pl.BlockSpec(shape, lambda *_: (0,) * nd, pipeline_mode=pl.Buffered(1))


def _t5_bucket_np(dist):
    max_exact = NUM_BUCKETS // 2
    d = np.maximum(dist, 1).astype(np.float32)
    log_b = max_exact + (np.log(d / np.float32(max_exact)) / np.float32(math.log(MAX_DISTANCE / max_exact))
                         * np.float32(NUM_BUCKETS - max_exact)).astype(np.int32)
    return np.where(dist < max_exact, dist, np.minimum(log_b, NUM_BUCKETS - 1)).astype(np.int32)


def _bucket_tables():
    w = SWA_BLK
    qi = np.arange(w)[:, None]
    kj = np.arange(w)[None, :]
    out = np.zeros((len(PATTERNS), 2, w, w), np.int32)
    for p, (_, dil) in enumerate(PATTERNS):
        steps = _t5_bucket_np(np.arange(w + 1) * dil)
        rel_prev = qi + w - kj
        rel_cur = qi - kj
        out[p, 0] = np.where(rel_prev <= w, steps[np.clip(rel_prev, 0, w)], -1)
        out[p, 1] = np.where(rel_cur >= 0, steps[np.clip(rel_cur, 0, w)], -1)
    return out


def _bias_tables(rel_bias, bk):
    def body(rb_ref, bk_ref, o_ref):
        b_idx = bk_ref[0, 0]
        for h in range(SWA_HEADS):
            def lp(b, acc):
                return jnp.where(b_idx == b, rb_ref[b, h], acc)
            o_ref[0, h, 0] = lax.fori_loop(0, NUM_BUCKETS, lp, jnp.full((SWA_BLK, SWA_BLK), NEG, f32))

    return pl.pallas_call(
        body, name="bias_tables", grid=(3, 2),
        in_specs=[pl.BlockSpec(memory_space=pltpu.SMEM),
                  pl.BlockSpec((1, 1, SWA_BLK, SWA_BLK), lambda p, t: (p, t, 0, 0))],
        out_specs=pl.BlockSpec((1, SWA_HEADS, 1, SWA_BLK, SWA_BLK), lambda p, t: (p, 0, t, 0, 0)),
        out_shape=SDS((3, SWA_HEADS, 2, SWA_BLK, SWA_BLK), f32),
        compiler_params=_cp(2),
    )(rel_bias, bk)


def _rel_bias_grad(dsb, bk, bk_np):
    present = [[sorted(set(int(v) for v in np.unique(bk_np[p, t]) if v >= 0)) for t in range(2)] for p in range(3)]

    def body(ds_ref, bk_ref, o_ref):
        row = lax.broadcasted_iota(jnp.int32, (NUM_BUCKETS, LANE), 0)
        col = lax.broadcasted_iota(jnp.int32, (NUM_BUCKETS, SWA_HEADS), 1)
        out = jnp.zeros((NUM_BUCKETS, SWA_HEADS), f32)
        for hp in range(4):
            for hh in range(2):
                acc = jnp.zeros((NUM_BUCKETS, LANE), f32)
                for p in range(3):
                    for t in range(2):
                        tile = ds_ref[hp, p, hh, t]
                        b_idx = bk_ref[p, t]
                        for b in present[p][t]:
                            part = jnp.sum(jnp.where(b_idx == b, tile, 0.0), axis=0, keepdims=True)
                            acc = acc + jnp.where(row == b, part, 0.0)
                tot = jnp.sum(acc, axis=1, keepdims=True)
                out = out + jnp.where(col == 2 * hp + hh, tot, 0.0)
        o_ref[...] = out

    return pl.pallas_call(body, name="rel_bias_grad", out_shape=SDS((NUM_BUCKETS, SWA_HEADS), f32),
                          compiler_params=_cp(0, 32 * 1024 * 1024))(dsb, bk)


def _proj_fwd(x, g_pre, wcat, after, S):
    TS = 256

    def body(x_ref, g_ref, w_ref, after_ref, o_ref):
        xv = x_ref[...]
        r = lax.rsqrt(jnp.mean(xv * xv, axis=-1, keepdims=True) + RMS_EPS)
        h = (xv * r * g_ref[...]).astype(bf16)
        o_ref[...] = _dot(h, w_ref[...])

    return pl.pallas_call(
        body, name="proj_fwd", grid=(S // TS,),
        in_specs=[pl.BlockSpec((TS, D_MODEL), lambda i: (i, 0)), _const_spec((1, D_MODEL)),
                  _resident_spec((D_MODEL, NCOL)), _ANY],
        out_specs=pl.BlockSpec((TS, NCOL), lambda i: (i, 0)),
        out_shape=SDS((S, NCOL), f32),
        compiler_params=_cp(1, VMEM_LIMIT),
    )(x, g_pre, wcat, after)


CONV_RT = 256
HALO = 8


CONV_NC = CONV_RT // CHUNK


def _gdn_prep(proj, conv_w, S):
    def body(p_ref, cw_ref, o_ref, xs_ref):
        t = pl.program_id(0)
        xs_ref[pl.ds(0, HALO), :] = jnp.zeros((HALO, LANE), f32)
        xs_ref[pl.ds(HALO, S), :] = p_ref[...]
        w = cw_ref[...]
        is_qk = t < 2
        scale = jnp.where(t == 0, GDN_HD ** -0.5, 1.0).astype(f32)

        def lp(c, carry):
            st = pl.multiple_of(c * CONV_RT, CONV_RT)
            pre = xs_ref[pl.ds(st + HALO - 3, CONV_RT), :] * w[0:1, :]
            for i in range(1, 4):
                pre = pre + xs_ref[pl.ds(st + HALO - 3 + i, CONV_RT), :] * w[i:i + 1, :]
            s = pre * _sigmoid(pre)
            nrm = s * lax.rsqrt(jnp.sum(s * s, axis=-1, keepdims=True) + 1e-6) * scale
            out = jnp.where(is_qk, nrm, s)
            for i in range(CONV_NC):
                o_ref[0, c * CONV_NC + i, 0] = out[i * CHUNK:(i + 1) * CHUNK]
            return carry

        lax.fori_loop(0, S // CONV_RT, lp, 0)

    return pl.pallas_call(
        body, name="gdn_prep", grid=(3, GDN_HEADS),
        in_specs=[pl.BlockSpec((S, LANE), lambda t, h: (0, t * GDN_HEADS + h)),
                  pl.BlockSpec((4, LANE), lambda t, h: (0, t * GDN_HEADS + h))],
        out_specs=pl.BlockSpec((1, S // CHUNK, 1, CHUNK, GDN_HD), lambda t, h: (t, 0, h, 0, 0)),
        out_shape=SDS((3, S // CHUNK, GDN_HEADS, CHUNK, GDN_HD), f32),
        scratch_shapes=[pltpu.VMEM((S + HALO, LANE), f32)],
        compiler_params=_cp(2, VMEM_LIMIT),
    )(proj, conv_w)


def _gdn_prep_bwd(proj, conv_w, dqkv, S):
    def body(p_ref, cw_ref, d_ref, dx_ref, dw_ref, xs_ref, dp_ref):
        t = pl.program_id(0)
        xs_ref[pl.ds(0, HALO), :] = jnp.zeros((HALO, LANE), f32)
        xs_ref[pl.ds(HALO, S), :] = p_ref[...]
        dp_ref[pl.ds(S, HALO), :] = jnp.zeros((HALO, LANE), f32)
        w = cw_ref[...]
        is_qk = t < 2
        scale = jnp.where(t == 0, GDN_HD ** -0.5, 1.0).astype(f32)

        def lp1(c, dw):
            st = pl.multiple_of(c * CONV_RT, CONV_RT)
            taps = [xs_ref[pl.ds(st + HALO - 3 + i, CONV_RT), :] for i in range(4)]
            pre = taps[0] * w[0:1, :]
            for i in range(1, 4):
                pre = pre + taps[i] * w[i:i + 1, :]
            sg = _sigmoid(pre)
            s = pre * sg
            d_out = jnp.concatenate([d_ref[0, c * CONV_NC + i, 0] for i in range(CONV_NC)], axis=0)
            rn = lax.rsqrt(jnp.sum(s * s, axis=-1, keepdims=True) + 1e-6)
            n = s * rn
            dn = d_out * scale
            ds_qk = rn * (dn - n * jnp.sum(dn * n, axis=-1, keepdims=True))
            ds = jnp.where(is_qk, ds_qk, d_out)
            dpre = ds * (sg * (1.0 + pre * (1.0 - sg)))
            dp_ref[pl.ds(st, CONV_RT), :] = dpre
            return tuple(dw[i] + jnp.sum(dpre * taps[i], axis=0, keepdims=True) for i in range(4))

        z = jnp.zeros((1, LANE), f32)
        dw = lax.fori_loop(0, S // CONV_RT, lp1, (z, z, z, z))
        for i in range(4):
            dw_ref[pl.ds(i, 1), :] = dw[i]

        def lp2(c, carry):
            st = pl.multiple_of(c * CONV_RT, CONV_RT)
            dx = dp_ref[pl.ds(st, CONV_RT), :] * w[3:4, :]
            for i in range(3):
                dx = dx + dp_ref[pl.ds(st + 3 - i, CONV_RT), :] * w[i:i + 1, :]
            dx_ref[pl.ds(st, CONV_RT), :] = dx.astype(bf16)
            return carry

        lax.fori_loop(0, S // CONV_RT, lp2, 0)

    col = lambda rows: pl.BlockSpec((rows, LANE), lambda t, h: (0, t * GDN_HEADS + h))
    return pl.pallas_call(
        body, name="gdn_prep_bwd", grid=(3, GDN_HEADS),
        in_specs=[col(S), col(4), pl.BlockSpec((1, S // CHUNK, 1, CHUNK, GDN_HD), lambda t, h: (t, 0, h, 0, 0))],
        out_specs=[col(S), col(4)],
        out_shape=[SDS((S, 3 * GDN_W), bf16), SDS((4, 3 * GDN_W), f32)],
        scratch_shapes=[pltpu.VMEM((S + HALO, LANE), f32), pltpu.VMEM((S + HALO, LANE), f32)],
        compiler_params=_cp(2, VMEM_LIMIT),
    )(proj, conv_w, dqkv)


def _bdot(a, b, prec=None):
    return lax.dot_general(a, b, (((2,), (1,)), ((0,), (0,))), precision=prec, preferred_element_type=f32)


def _bdot_nt(a, b, prec=None):
    return lax.dot_general(a, b, (((2,), (2,)), ((0,), (0,))), precision=prec, preferred_element_type=f32)


def _bdot_tn(a, b, prec=None):
    return lax.dot_general(a, b, (((1,), (1,)), ((0,), (0,))), precision=prec, preferred_element_type=f32)


@jax.custom_vjp
def _tri_inv_saved(a, t):
    return t


def _tri_inv_saved_fwd(a, t):
    return t, t


def _tri_inv_saved_bwd(t, dt):
    return -_bdot_tn(t, _bdot_nt(dt, t, HIGH), HIGH), jnp.zeros_like(t)


_tri_inv_saved.defvjp(_tri_inv_saved_fwd, _tri_inv_saved_bwd)


def _gdn_intra(q, k, v, bl, al, a_log, dt_bias, t_saved=None):
    nb = q.shape[0]
    c = CHUNK
    ii = lax.broadcasted_iota(jnp.int32, (c, c), 0)
    jj = lax.broadcasted_iota(jnp.int32, (c, c), 1)
    eye = ii == jj
    tril = ii >= jj
    strict = ii > jj
    ones = jnp.ones((nb, c, c), f32)
    eye_f = eye.astype(f32)

    beta = _sigmoid(bl)
    g = -jnp.exp(a_log) * _softplus(al + dt_bias)
    g_row = _bdot(ones, jnp.where(eye, g, 0.0), HI)
    gc = jnp.sum(jnp.where(tril, g_row, 0.0), axis=2, keepdims=True)
    gc_row = _bdot(ones, jnp.where(eye, gc, 0.0), HI)
    decay = jnp.where(tril, jnp.exp(jnp.where(tril, gc - gc_row, 0.0)), 0.0)
    last = lax.broadcasted_iota(jnp.int32, (c, 1), 0) == c - 1
    gc_last = jnp.sum(jnp.where(last, gc, 0.0), axis=1, keepdims=True)
    e_gc = jnp.exp(gc)

    kb = k * beta
    k16 = k.astype(bf16)
    a = jnp.where(strict, _bdot_nt(kb.astype(bf16), k16) * decay, 0.0)
    if t_saved is None:
        xp = -a
        t_inv = eye_f + xp
        for _ in range(5):
            xp = _bdot(xp, xp, HIGH)
            t_inv = _bdot(t_inv, eye_f + xp, HIGH)
    else:
        t_inv = _tri_inv_saved(a, t_saved)
    t16 = t_inv.astype(bf16)
    u = _bdot(t16, (v * beta).astype(bf16))
    w = _bdot(t16, (kb * e_gc).astype(bf16))
    attn = jnp.where(tril, _bdot_nt(q.astype(bf16), k16) * decay, 0.0)
    gam = jnp.broadcast_to(jnp.exp(gc_last), (nb, 1, GDN_HD))
    return u, w, attn, q * e_gc, k * jnp.exp(gc_last - gc), gam, t_inv


GDN_TB = 256
GDN_NC = GDN_TB // CHUNK
GDN_NU = GDN_NC * GDN_HEADS


def _gdn_unit_inputs(qkv_ref, pg_ref, al_ref, db_ref):
    pg = pg_ref[...]
    bl = jnp.stack([pg[cl * CHUNK:(cl + 1) * CHUNK, h:h + 1] for cl in range(GDN_NC) for h in range(GDN_HEADS)])
    al = jnp.stack([pg[cl * CHUNK:(cl + 1) * CHUNK, GDN_HEADS + h:GDN_HEADS + h + 1]
                    for cl in range(GDN_NC) for h in range(GDN_HEADS)])
    a_log = jnp.stack([jnp.full((1, 1), al_ref[0, h], f32) for _ in range(GDN_NC) for h in range(GDN_HEADS)])
    dt_b = jnp.stack([jnp.full((1, 1), db_ref[0, h], f32) for _ in range(GDN_NC) for h in range(GDN_HEADS)])
    return qkv_ref[0], qkv_ref[1], qkv_ref[2], bl, al, a_log, dt_b


def _unit_spec(*tail):
    nd = len(tail)
    return pl.BlockSpec((GDN_NU,) + tail, lambda i: (i,) + (0,) * nd)


def _gdn_intra_shapes(S):
    nu = S // CHUNK * GDN_HEADS
    row = SDS((nu, CHUNK, GDN_HD), f32)
    return [row, row, SDS((nu, CHUNK, CHUNK), f32), row, row, SDS((nu, 1, GDN_HD), f32)]


_GDN_INTRA_SPECS = lambda: [_unit_spec(CHUNK, GDN_HD), _unit_spec(CHUNK, GDN_HD), _unit_spec(CHUNK, CHUNK),
                            _unit_spec(CHUNK, GDN_HD), _unit_spec(CHUNK, GDN_HD), _unit_spec(1, GDN_HD)]


def _gdn_intra_fwd(qkv_u, proj, a_log, dt_bias, S):
    def body(qkv_ref, pg_ref, al_ref, db_ref, *outs):
        res = _gdn_intra(*_gdn_unit_inputs(qkv_ref, pg_ref, al_ref, db_ref))
        for o_ref, r in zip(outs, res):
            o_ref[...] = r

    nu = S // CHUNK * GDN_HEADS
    *intra, t_inv = pl.pallas_call(
        body, name="gdn_intra_fwd", grid=(S // GDN_TB,),
        in_specs=[pl.BlockSpec((3, GDN_NU, CHUNK, GDN_HD), lambda i: (0, i, 0, 0)),
                  pl.BlockSpec((GDN_TB, LANE), lambda i: (i, COL_G // LANE)),
                  pl.BlockSpec(memory_space=pltpu.SMEM), pl.BlockSpec(memory_space=pltpu.SMEM)],
        out_specs=_GDN_INTRA_SPECS() + [_unit_spec(CHUNK, CHUNK)],
        out_shape=_gdn_intra_shapes(S) + [SDS((nu, CHUNK, CHUNK), f32)],
        compiler_params=_cp(1, VMEM_LIMIT),
    )(qkv_u, proj, a_log, dt_bias)
    return intra, t_inv


def _gdn_intra_bwd(qkv_u, proj, a_log, dt_bias, t_inv, cots, S):
    def body(qkv_ref, pg_ref, al_ref, db_ref, t_ref, du_ref, dw_ref, da_ref, dqd_ref, dkd_ref, dgm_ref,
             dqkv_ref, dpg_ref, dal_ref, ddb_ref):
        @pl.when(pl.program_id(0) == 0)
        def _():
            dal_ref[...] = jnp.zeros_like(dal_ref)
            ddb_ref[...] = jnp.zeros_like(ddb_ref)

        t_saved = t_ref[...]
        _, vjp = jax.vjp(lambda *a: _gdn_intra(*a, t_saved=t_saved)[:6], *_gdn_unit_inputs(qkv_ref, pg_ref, al_ref, db_ref))
        dq, dk, dv, dbl, dal, da, ddb = vjp((du_ref[...], dw_ref[...], da_ref[...], dqd_ref[...], dkd_ref[...], dgm_ref[...]))
        dqkv_ref[0] = dq
        dqkv_ref[1] = dk
        dqkv_ref[2] = dv
        lane = lax.broadcasted_iota(jnp.int32, (CHUNK, LANE), 1)
        lane1 = lax.broadcasted_iota(jnp.int32, (1, LANE), 1)
        da_tot = jnp.zeros((1, LANE), f32)
        ddb_tot = jnp.zeros((1, LANE), f32)
        for cl in range(GDN_NC):
            dpg = jnp.zeros((CHUNK, LANE), f32)
            for h in range(GDN_HEADS):
                b = cl * GDN_HEADS + h
                dpg = dpg + jnp.where(lane == h, dbl[b], 0.0) + jnp.where(lane == GDN_HEADS + h, dal[b], 0.0)
                da_tot = da_tot + jnp.where(lane1 == h, da[b], 0.0)
                ddb_tot = ddb_tot + jnp.where(lane1 == h, ddb[b], 0.0)
            dpg_ref[cl * CHUNK:(cl + 1) * CHUNK, :] = dpg.astype(bf16)
        dal_ref[...] += da_tot
        ddb_ref[...] += ddb_tot

    acc = _const_spec((1, LANE))
    nu = S // CHUNK * GDN_HEADS
    return pl.pallas_call(
        body, name="gdn_intra_bwd", grid=(S // GDN_TB,),
        in_specs=[pl.BlockSpec((3, GDN_NU, CHUNK, GDN_HD), lambda i: (0, i, 0, 0)),
                  pl.BlockSpec((GDN_TB, LANE), lambda i: (i, COL_G // LANE)),
                  pl.BlockSpec(memory_space=pltpu.SMEM), pl.BlockSpec(memory_space=pltpu.SMEM),
                  _unit_spec(CHUNK, CHUNK)] + _GDN_INTRA_SPECS(),
        out_specs=[pl.BlockSpec((3, GDN_NU, CHUNK, GDN_HD), lambda i: (0, i, 0, 0)),
                   pl.BlockSpec((GDN_TB, LANE), lambda i: (i, 0)), acc, acc],
        out_shape=[SDS((3, nu, CHUNK, GDN_HD), f32), SDS((S, LANE), bf16), SDS((1, LANE), f32), SDS((1, LANE), f32)],
        compiler_params=_cp(1, VMEM_LIMIT),
    )(qkv_u, proj, a_log, dt_bias, t_inv, *cots)


def _gdn_scan_fwd(intra, proj, onorm_g, S):
    def body(u_ref, w_ref, at_ref, qd_ref, kd_ref, gm_ref, gate_ref, og_ref, out_ref, st_ref, s_scr):
        @pl.when(pl.program_id(0) == 0)
        def _():
            s_scr[...] = jnp.zeros_like(s_scr)

        og = og_ref[...]
        s = s_scr[...]
        for cl in range(GDN_NC):
            us = slice(cl * GDN_HEADS, (cl + 1) * GDN_HEADS)
            rows = slice(cl * CHUNK, (cl + 1) * CHUNK)
            st_ref[us] = s
            s16 = s.astype(bf16)
            vn = u_ref[us] - _bdot(w_ref[us].astype(bf16), s16)
            vn16 = vn.astype(bf16)
            o = _bdot(qd_ref[us].astype(bf16), s16) + _bdot(at_ref[us].astype(bf16), vn16)
            s = s * gm_ref[us] + _bdot_tn(kd_ref[us].astype(bf16), vn16)
            for h in range(GDN_HEADS):
                oh = o[h]
                gt = gate_ref[rows, h * GDN_HD:(h + 1) * GDN_HD]
                on = oh * lax.rsqrt(jnp.mean(oh * oh, axis=-1, keepdims=True) + RMS_EPS) * og
                out_ref[rows, h * GDN_HD:(h + 1) * GDN_HD] = on * (gt * _sigmoid(gt))
        s_scr[...] = s

    nu = S // CHUNK * GDN_HEADS
    return pl.pallas_call(
        body, name="gdn_scan_fwd", grid=(S // GDN_TB,),
        in_specs=_GDN_INTRA_SPECS() + [pl.BlockSpec((GDN_TB, GDN_W), lambda i: (i, 3)), _const_spec((1, GDN_HD))],
        out_specs=[pl.BlockSpec((GDN_TB, GDN_W), lambda i: (i, 0)), _unit_spec(GDN_HD, GDN_HD)],
        out_shape=[SDS((S, GDN_W), f32), SDS((nu, GDN_HD, GDN_HD), f32)],
        scratch_shapes=[pltpu.VMEM((GDN_HEADS, GDN_HD, GDN_HD), f32)],
        compiler_params=_cp(1, VMEM_LIMIT),
    )(*intra, proj, onorm_g)


def _gdn_scan_bwd(intra, states, proj, d_oab, onorm_g, after, S):
    n_steps = S // GDN_TB

    def body(u_ref, w_ref, at_ref, qd_ref, kd_ref, gm_ref, st_ref, gate_ref, do_ref, og_ref, after_ref,
             du_ref, dw_ref, dat_ref, dqd_ref, dkd_ref, dgm_ref, dgate_ref, dog_ref, ds_scr):
        @pl.when(pl.program_id(0) == 0)
        def _():
            ds_scr[...] = jnp.zeros_like(ds_scr)
            dog_ref[...] = jnp.zeros_like(dog_ref)

        og = og_ref[...]
        ii = lax.broadcasted_iota(jnp.int32, (CHUNK, CHUNK), 0)
        jj = lax.broadcasted_iota(jnp.int32, (CHUNK, CHUNK), 1)
        tril = ii >= jj
        ds = ds_scr[...]
        dog = jnp.zeros((1, GDN_HD), f32)
        for cl in reversed(range(GDN_NC)):
            us = slice(cl * GDN_HEADS, (cl + 1) * GDN_HEADS)
            rows = slice(cl * CHUNK, (cl + 1) * CHUNK)
            s0 = st_ref[us]
            s016 = s0.astype(bf16)
            w16 = w_ref[us].astype(bf16)
            qd16 = qd_ref[us].astype(bf16)
            kd16 = kd_ref[us].astype(bf16)
            at16 = at_ref[us].astype(bf16)
            vn = u_ref[us] - _bdot(w16, s016)
            vn16 = vn.astype(bf16)
            o = _bdot(qd16, s016) + _bdot(at16, vn16)
            do_h = []
            for h in range(GDN_HEADS):
                oh = o[h]
                lanes = slice(h * GDN_HD, (h + 1) * GDN_HD)
                gt = gate_ref[rows, lanes]
                d_out = do_ref[rows, lanes]
                r = lax.rsqrt(jnp.mean(oh * oh, axis=-1, keepdims=True) + RMS_EPS)
                n = oh * r
                sg = _sigmoid(gt)
                silu = gt * sg
                dog = dog + jnp.sum(d_out * n * silu, axis=0, keepdims=True)
                dgate_ref[rows, lanes] = (d_out * n * og * (sg * (1.0 + gt * (1.0 - sg)))).astype(bf16)
                dn = d_out * og * silu
                do_h.append(r * (dn - n * jnp.mean(dn * n, axis=-1, keepdims=True)))
            do16 = jnp.stack(do_h).astype(bf16)
            ds16 = ds.astype(bf16)
            dvn = _bdot_tn(at16, do16) + _bdot(kd16, ds16)
            dvn16 = dvn.astype(bf16)
            du_ref[us] = dvn
            dw_ref[us] = -_bdot_nt(dvn16, s016)
            dat_ref[us] = jnp.where(tril, _bdot_nt(do16, vn16), 0.0)
            dqd_ref[us] = _bdot_nt(do16, s016)
            dkd_ref[us] = _bdot_nt(vn16, ds16)
            dgm_ref[us] = jnp.sum(s0 * ds, axis=1, keepdims=True)
            ds = _bdot_tn(qd16, do16) + ds * gm_ref[us] - _bdot_tn(w16, dvn16)
        ds_scr[...] = ds
        dog_ref[...] += dog

    def unit(*tail):
        nd = len(tail)
        return pl.BlockSpec((GDN_NU,) + tail, lambda i: (n_steps - 1 - i,) + (0,) * nd)

    intra_specs = [unit(CHUNK, GDN_HD), unit(CHUNK, GDN_HD), unit(CHUNK, CHUNK), unit(CHUNK, GDN_HD),
                   unit(CHUNK, GDN_HD), unit(1, GDN_HD)]
    tok = lambda c: pl.BlockSpec((GDN_TB, GDN_W), lambda i: (n_steps - 1 - i, c))
    return pl.pallas_call(
        body, name="gdn_scan_bwd", grid=(n_steps,),
        in_specs=intra_specs + [unit(GDN_HD, GDN_HD), tok(3), tok(0), _const_spec((1, GDN_HD)), _ANY],
        out_specs=intra_specs + [tok(0), _const_spec((1, GDN_HD))],
        out_shape=_gdn_intra_shapes(S) + [SDS((S, GDN_W), bf16), SDS((1, GDN_HD), f32)],
        scratch_shapes=[pltpu.VMEM((GDN_HEADS, GDN_HD, GDN_HD), f32)],
        compiler_params=_cp(1, VMEM_LIMIT),
    )(*intra, states, proj, d_oab, onorm_g, after)


SWA_UNROLL = 4


def _swa_tiles(q_ref, k_ref, v_ref, it, d, nb_log2, S):
    nb = 1 << nb_log2
    r = lax.shift_right_logical(it, nb_log2)
    blk = lax.bitwise_and(it, nb - 1)
    qs = blk * (SWA_BLK * d) + r
    ps = jnp.maximum(blk - 1, 0) * (SWA_BLK * d) + r
    if d > 1:
        rows_c, rows_p = pl.ds(qs, SWA_BLK, stride=d), pl.ds(ps, SWA_BLK, stride=d)
    else:
        rows_c, rows_p = pl.ds(pl.multiple_of(qs, SWA_BLK), SWA_BLK), pl.ds(pl.multiple_of(ps, SWA_BLK), SWA_BLK)
    return rows_c, rows_p, blk > 0


def _swa_fwd(proj, bt, S):
    scale = SWA_HD ** -0.5

    def body(q_ref, k_ref, v_ref, bt_ref, o_ref, lse_ref, m_scr, l_scr, acc_scr):
        lane = lax.broadcasted_iota(jnp.int32, (SWA_BLK, LANE), 1)
        h0 = lane < SWA_HD
        m_scr[...] = jnp.full((S, LANE), NEG, f32)
        l_scr[...] = jnp.zeros((S, LANE), f32)
        acc_scr[...] = jnp.zeros((S, LANE), f32)
        for p, (_, d) in enumerate(PATTERNS):
            nb_log2 = int(math.log2(S // d // SWA_BLK))

            def tile(it, p=p, d=d, nb_log2=nb_log2):
                rows_c, rows_p, has_prev = _swa_tiles(q_ref, k_ref, v_ref, it, d, nb_log2, S)
                q = q_ref[rows_c, :]
                kc = k_ref[rows_c, :].astype(bf16)
                vc = v_ref[rows_c, :].astype(bf16)
                kp = k_ref[rows_p, :].astype(bf16)
                vp = v_ref[rows_p, :].astype(bf16)
                m_old = m_scr[rows_c, :]
                l_old = l_scr[rows_c, :]
                acc_old = acc_scr[rows_c, :]
                res = []
                for h in range(2):
                    mh = h0 if h == 0 else jnp.logical_not(h0)
                    qh = jnp.where(mh, q, 0.0).astype(bf16)
                    s_c = _dot_nt(qh, kc) * scale + bt_ref[p, h, 1]
                    s_p = jnp.where(has_prev, _dot_nt(qh, kp) * scale + bt_ref[p, h, 0], NEG)
                    mo = m_old[:, h * SWA_HD:h * SWA_HD + 1]
                    lo = l_old[:, h * SWA_HD:h * SWA_HD + 1]
                    mn = jnp.maximum(mo, jnp.maximum(jnp.max(s_c, axis=1, keepdims=True), jnp.max(s_p, axis=1, keepdims=True)))
                    alpha = jnp.exp(mo - mn)
                    p_c = jnp.exp(s_c - mn)
                    p_p = jnp.exp(s_p - mn)
                    ln = alpha * lo + jnp.sum(p_c, axis=1, keepdims=True) + jnp.sum(p_p, axis=1, keepdims=True)
                    pv = _dot(p_c.astype(bf16), vc) + _dot(p_p.astype(bf16), vp)
                    res.append((mn, ln, alpha, pv))
                return (rows_c, jnp.where(h0, res[0][0], res[1][0]), jnp.where(h0, res[0][1], res[1][1]),
                        acc_old * jnp.where(h0, res[0][2], res[1][2]) + jnp.where(h0, res[0][3], res[1][3]))

            def lp(i, carry, tile=tile):
                done = [tile(i * SWA_UNROLL + u) for u in range(SWA_UNROLL)]
                for rows_c, m_new, l_new, acc_new in done:
                    m_scr[rows_c, :] = m_new
                    l_scr[rows_c, :] = l_new
                    acc_scr[rows_c, :] = acc_new
                return carry

            lax.fori_loop(0, S // SWA_BLK // SWA_UNROLL, lp, 0)
        l_all = l_scr[...]
        o_ref[...] = acc_scr[...] / l_all
        lse_ref[...] = m_scr[...] + jnp.log(l_all)

    qb = COL_B // LANE
    return pl.pallas_call(
        body, name="swa_fwd", grid=(4,),
        in_specs=[pl.BlockSpec((S, LANE), lambda hp: (0, qb + hp)), pl.BlockSpec((S, LANE), lambda hp: (0, qb + 4 + hp)),
                  pl.BlockSpec((S, LANE), lambda hp: (0, qb + 8 + hp)),
                  pl.BlockSpec((3, 2, 2, SWA_BLK, SWA_BLK), lambda hp: (0, hp, 0, 0, 0))],
        out_specs=[pl.BlockSpec((S, LANE), lambda hp: (0, hp)), pl.BlockSpec((S, LANE), lambda hp: (0, hp))],
        out_shape=[SDS((S, SWA_W), f32), SDS((S, SWA_W), f32)],
        scratch_shapes=[pltpu.VMEM((S, LANE), f32)] * 3,
        compiler_params=_cp(1, VMEM_LIMIT),
    )(proj, proj, proj, bt)


def _swa_bwd(proj, bt, ob, lse, d_oab, after, S):
    scale = SWA_HD ** -0.5

    def body(q_ref, k_ref, v_ref, bt_ref, o_ref, lse_ref, do_ref, after_ref, dq_ref, dk_ref, dv_ref, dsb_ref,
             dq_scr, dk_scr, dv_scr):
        lane = lax.broadcasted_iota(jnp.int32, (SWA_BLK, LANE), 1)
        h0 = lane < SWA_HD
        dq_scr[...] = jnp.zeros((S, LANE), f32)
        dk_scr[...] = jnp.zeros((S, LANE), f32)
        dv_scr[...] = jnp.zeros((S, LANE), f32)
        dsb_ref[...] = jnp.zeros_like(dsb_ref)
        for p, (_, d) in enumerate(PATTERNS):
            nb_log2 = int(math.log2(S // d // SWA_BLK))

            def tile(it, p=p, d=d, nb_log2=nb_log2):
                rows_c, rows_p, has_prev = _swa_tiles(q_ref, k_ref, v_ref, it, d, nb_log2, S)
                q = q_ref[rows_c, :]
                kc = k_ref[rows_c, :].astype(bf16)
                vc = v_ref[rows_c, :].astype(bf16)
                kp = k_ref[rows_p, :].astype(bf16)
                vp = v_ref[rows_p, :].astype(bf16)
                do = do_ref[rows_c, :]
                lse_t = lse_ref[rows_c, :]
                dlt = do * o_ref[rows_c, :]
                dq_t = jnp.zeros((SWA_BLK, LANE), f32)
                dkc_t = jnp.zeros((SWA_BLK, LANE), f32)
                dkp_t = jnp.zeros((SWA_BLK, LANE), f32)
                dvc_t = jnp.zeros((SWA_BLK, LANE), f32)
                dvp_t = jnp.zeros((SWA_BLK, LANE), f32)
                ds_all = []
                for h in range(2):
                    mh = h0 if h == 0 else jnp.logical_not(h0)
                    qh = jnp.where(mh, q, 0.0).astype(bf16)
                    doh = jnp.where(mh, do, 0.0).astype(bf16)
                    delta = jnp.sum(jnp.where(mh, dlt, 0.0), axis=1, keepdims=True)
                    lse_h = lse_t[:, h * SWA_HD:h * SWA_HD + 1]
                    s_c = _dot_nt(qh, kc) * scale + bt_ref[p, h, 1]
                    s_p = jnp.where(has_prev, _dot_nt(qh, kp) * scale + bt_ref[p, h, 0], NEG)
                    p_c = jnp.exp(s_c - lse_h)
                    p_p = jnp.exp(s_p - lse_h)
                    ds_c = p_c * (_dot_nt(doh, vc) - delta)
                    ds_p = p_p * (_dot_nt(doh, vp) - delta)
                    ds_all.append((ds_p, ds_c))
                    dsc16 = ds_c.astype(bf16)
                    dsp16 = ds_p.astype(bf16)
                    dq_t = dq_t + jnp.where(mh, (_dot(dsc16, kc) + _dot(dsp16, kp)) * scale, 0.0)
                    dkc_t = dkc_t + _dot_tn(dsc16, qh) * scale
                    dkp_t = dkp_t + _dot_tn(dsp16, qh) * scale
                    dvc_t = dvc_t + _dot_tn(p_c.astype(bf16), doh)
                    dvp_t = dvp_t + _dot_tn(p_p.astype(bf16), doh)
                return rows_c, rows_p, dq_t, dkc_t, dkp_t, dvc_t, dvp_t, ds_all

            def lp(i, carry, tile=tile, p=p):
                done = [tile(i * SWA_UNROLL + u) for u in range(SWA_UNROLL)]
                for h in range(2):
                    for t in range(2):
                        tot = done[0][7][h][t]
                        for dn in done[1:]:
                            tot = tot + dn[7][h][t]
                        dsb_ref[0, p, h, t] += tot
                for rows_c, rows_p, dq_t, dkc_t, dkp_t, dvc_t, dvp_t, _ in done:
                    dq_scr[rows_c, :] = dq_scr[rows_c, :] + dq_t
                    dk_scr[rows_c, :] = dk_scr[rows_c, :] + dkc_t
                    dv_scr[rows_c, :] = dv_scr[rows_c, :] + dvc_t
                    dk_scr[rows_p, :] = dk_scr[rows_p, :] + dkp_t
                    dv_scr[rows_p, :] = dv_scr[rows_p, :] + dvp_t
                return carry

            lax.fori_loop(0, S // SWA_BLK // SWA_UNROLL, lp, 0)
        dq_ref[...] = dq_scr[...].astype(bf16)
        dk_ref[...] = dk_scr[...].astype(bf16)
        dv_ref[...] = dv_scr[...].astype(bf16)

    qb = COL_B // LANE
    col = lambda c: pl.BlockSpec((S, LANE), lambda hp, c=c: (0, c + hp))
    return pl.pallas_call(
        body, name="swa_bwd", grid=(4,),
        in_specs=[col(qb), col(qb + 4), col(qb + 8),
                  pl.BlockSpec((3, 2, 2, SWA_BLK, SWA_BLK), lambda hp: (0, hp, 0, 0, 0)),
                  col(0), col(0), col(4), _ANY],
        out_specs=[col(0), col(0), col(0),
                   pl.BlockSpec((1, 3, 2, 2, SWA_BLK, SWA_BLK), lambda hp: (hp, 0, 0, 0, 0, 0))],
        out_shape=[SDS((S, SWA_W), bf16)] * 3 + [SDS((4, 3, 2, 2, SWA_BLK, SWA_BLK), f32)],
        scratch_shapes=[pltpu.VMEM((S, LANE), f32)] * 3,
        compiler_params=_cp(1, VMEM_LIMIT),
    )(proj, proj, proj, bt, ob, lse, d_oab, after)


def _mix_fwd(oa, ob, w_out, x, g_post, S):
    TS = 512

    def body(oa_ref, ob_ref, w_ref, x_ref, g_ref, mix_ref, x1_ref):
        mix = _dot(oa_ref[...].astype(bf16), w_ref[0:GDN_W, :]) + _dot(ob_ref[...].astype(bf16), w_ref[GDN_W:D_MODEL, :])
        r = lax.rsqrt(jnp.mean(mix * mix, axis=-1, keepdims=True) + RMS_EPS)
        mix_ref[...] = mix
        x1_ref[...] = x_ref[...] + mix * r * g_ref[...]

    row = lambda w: pl.BlockSpec((TS, w), lambda i: (i, 0))
    return pl.pallas_call(
        body, name="mix_fwd", grid=(S // TS,),
        in_specs=[row(GDN_W), row(SWA_W), _resident_spec((D_MODEL, D_MODEL)), row(D_MODEL), _const_spec((1, D_MODEL))],
        out_specs=[row(D_MODEL), row(D_MODEL)],
        out_shape=[SDS((S, D_MODEL), f32), SDS((S, D_MODEL), f32)],
        compiler_params=_cp(1, VMEM_LIMIT),
    )(oa, ob, w_out, x, g_post)


def _mix_bwd(dx1, mix, g_post, w_out, S):
    TS = 512

    def body(dx1_ref, mix_ref, g_ref, w_ref, dmix_ref, doab_ref, dg_ref):
        @pl.when(pl.program_id(0) == 0)
        def _():
            dg_ref[...] = jnp.zeros_like(dg_ref)

        mix = mix_ref[...]
        dz = dx1_ref[...]
        r = lax.rsqrt(jnp.mean(mix * mix, axis=-1, keepdims=True) + RMS_EPS)
        n = mix * r
        dg_ref[...] += jnp.sum(dz * n, axis=0, keepdims=True)
        dn = dz * g_ref[...]
        dmix = (r * (dn - n * jnp.mean(dn * n, axis=-1, keepdims=True))).astype(bf16)
        dmix_ref[...] = dmix
        doab_ref[...] = _dot_nt(dmix, w_ref[...])

    row = lambda: pl.BlockSpec((TS, D_MODEL), lambda i: (i, 0))
    return pl.pallas_call(
        body, name="mix_bwd", grid=(S // TS,),
        in_specs=[row(), row(), _const_spec((1, D_MODEL)), _resident_spec((D_MODEL, D_MODEL))],
        out_specs=[row(), row(), _const_spec((1, D_MODEL))],
        out_shape=[SDS((S, D_MODEL), bf16), SDS((S, D_MODEL), f32), SDS((1, D_MODEL), f32)],
        compiler_params=_cp(1, VMEM_LIMIT),
    )(dx1, mix, g_post, w_out)


FFN_TS = 256
FFN_CH = 1408


def _ffn(x1, tgt, g_pre, g_post, wg, wu, wd, S):
    def body(x1_ref, t_ref, gp_ref, gq_ref, wg_ref, wu_ref, wd_ref,
             dx1_ref, h2_ref, act_ref, dgate_ref, dup_ref, df_ref, loss_ref, dgp_ref, dgq_ref, gate_scr, up_scr):
        @pl.when(pl.program_id(0) == 0)
        def _():
            loss_ref[...] = jnp.zeros_like(loss_ref)
            dgp_ref[...] = jnp.zeros_like(dgp_ref)
            dgq_ref[...] = jnp.zeros_like(dgq_ref)

        x1v = x1_ref[...]
        gp = gp_ref[...]
        gq = gq_ref[...]
        r2 = lax.rsqrt(jnp.mean(x1v * x1v, axis=-1, keepdims=True) + RMS_EPS)
        n2 = x1v * r2
        h2 = (n2 * gp).astype(bf16)
        h2_ref[...] = h2
        f = jnp.zeros((FFN_TS, D_MODEL), f32)
        for c in range(D_FF // FFN_CH):
            cs = slice(c * FFN_CH, (c + 1) * FFN_CH)
            gate = _dot(h2, wg_ref[:, cs])
            up = _dot(h2, wu_ref[:, cs])
            gate_scr[:, cs] = gate
            up_scr[:, cs] = up
            act = (gate * _sigmoid(gate) * up).astype(bf16)
            act_ref[:, cs] = act
            f = f + _dot(act, wd_ref[cs, :])
        r3 = lax.rsqrt(jnp.mean(f * f, axis=-1, keepdims=True) + RMS_EPS)
        n3 = f * r3
        err = x1v + n3 * gq - t_ref[...]
        loss_ref[...] += 0.5 * jnp.sum(jnp.mean(err * err, axis=-1, keepdims=True), axis=0, keepdims=True)
        dy = err * (1.0 / D_MODEL)
        dgq_ref[...] += jnp.sum(dy * n3, axis=0, keepdims=True)
        dn3 = dy * gq
        df = (r3 * (dn3 - n3 * jnp.mean(dn3 * n3, axis=-1, keepdims=True))).astype(bf16)
        df_ref[...] = df
        dh2 = jnp.zeros((FFN_TS, D_MODEL), f32)
        for c in range(D_FF // FFN_CH):
            cs = slice(c * FFN_CH, (c + 1) * FFN_CH)
            gate = gate_scr[:, cs]
            up = up_scr[:, cs]
            dact = _dot_nt(df, wd_ref[cs, :])
            sg = _sigmoid(gate)
            dup = (dact * gate * sg).astype(bf16)
            dgate = (dact * up * (sg * (1.0 + gate * (1.0 - sg)))).astype(bf16)
            dup_ref[:, cs] = dup
            dgate_ref[:, cs] = dgate
            dh2 = dh2 + _dot_nt(dgate, wg_ref[:, cs]) + _dot_nt(dup, wu_ref[:, cs])
        dgp_ref[...] += jnp.sum(dh2 * n2, axis=0, keepdims=True)
        dn2 = dh2 * gp
        dx1_ref[...] = dy + r2 * (dn2 - n2 * jnp.mean(dn2 * n2, axis=-1, keepdims=True))

    row = lambda w: pl.BlockSpec((FFN_TS, w), lambda i: (i, 0))
    vec = _const_spec((1, D_MODEL))
    return pl.pallas_call(
        body, name="ffn_fwd_bwd", grid=(S // FFN_TS,),
        in_specs=[row(D_MODEL), row(D_MODEL), vec, vec, _resident_spec((D_MODEL, D_FF)), _resident_spec((D_MODEL, D_FF)),
                  _resident_spec((D_FF, D_MODEL))],
        out_specs=[row(D_MODEL), row(D_MODEL), row(D_FF), row(D_FF), row(D_FF), row(D_MODEL), _const_spec((1, LANE)), vec, vec],
        out_shape=[SDS((S, D_MODEL), f32), SDS((S, D_MODEL), bf16), SDS((S, D_FF), bf16), SDS((S, D_FF), bf16),
                   SDS((S, D_FF), bf16), SDS((S, D_MODEL), bf16), SDS((1, LANE), f32), SDS((1, D_MODEL), f32),
                   SDS((1, D_MODEL), f32)],
        scratch_shapes=[pltpu.VMEM((FFN_TS, D_FF), f32), pltpu.VMEM((FFN_TS, D_FF), f32)],
        compiler_params=_cp(1, VMEM_LIMIT),
    )(x1, tgt, g_pre, g_post, wg, wu, wd)


def _proj_bwd(x, dx1, g_pre, wcat, segs, S):
    TS = 256
    n = len(segs)
    cols = [(c0, a.shape[1]) for a, c0 in segs]

    def body(*refs):
        x_ref, dx1_ref, g_ref, w_ref = refs[:4]
        seg_refs = refs[4:4 + n]
        gx_ref, h1_ref, dg_ref = refs[4 + n:]

        @pl.when(pl.program_id(0) == 0)
        def _():
            dg_ref[...] = jnp.zeros_like(dg_ref)

        dh = jnp.zeros((TS, D_MODEL), f32)
        for s_ref, (c0, w) in zip(seg_refs, cols):
            dh = dh + _dot_nt(s_ref[...], w_ref[:, c0:c0 + w])
        xv = x_ref[...]
        g = g_ref[...]
        r = lax.rsqrt(jnp.mean(xv * xv, axis=-1, keepdims=True) + RMS_EPS)
        nx = xv * r
        h1_ref[...] = (nx * g).astype(bf16)
        dg_ref[...] += jnp.sum(dh * nx, axis=0, keepdims=True)
        dn = dh * g
        gx_ref[...] = dx1_ref[...] + r * (dn - nx * jnp.mean(dn * nx, axis=-1, keepdims=True))

    row = lambda w: pl.BlockSpec((TS, w), lambda i: (i, 0))
    return pl.pallas_call(
        body, name="proj_bwd", grid=(S // TS,),
        in_specs=[row(D_MODEL), row(D_MODEL), _const_spec((1, D_MODEL)), _resident_spec((D_MODEL, NCOL))]
                 + [row(w) for _, w in cols],
        out_specs=[row(D_MODEL), row(D_MODEL), _const_spec((1, D_MODEL))],
        out_shape=[SDS((S, D_MODEL), f32), SDS((S, D_MODEL), bf16), SDS((1, D_MODEL), f32)],
        compiler_params=_cp(1, VMEM_LIMIT),
    )(x, dx1, g_pre, wcat, *[a for a, _ in segs])


def _wgrad(a, b, S, name, col_blocks=False):
    TS = 512
    K = a.shape[1]
    N = b.shape[1]
    TN = next(t for t in (512, 1408, N) if N % t == 0)
    cb = N // N_DEV
    nblk = TN // cb if col_blocks else 0

    def body(a_ref, b_ref, o_ref, acc):
        @pl.when(pl.program_id(1) == 0)
        def _():
            acc[...] = jnp.zeros_like(acc)

        acc[...] += _dot_tn(a_ref[...].astype(bf16), b_ref[...])

        @pl.when(pl.program_id(1) == pl.num_programs(1) - 1)
        def _():
            if col_blocks:
                for i in range(nblk):
                    o_ref[i] = acc[:, i * cb:(i + 1) * cb].astype(bf16)
            else:
                o_ref[...] = acc[...].astype(bf16)

    if col_blocks:
        out_spec = pl.BlockSpec((nblk, K, cb), lambda j, s: (j, 0, 0))
        out_shape = SDS((N_DEV, K, cb), bf16)
    else:
        out_spec = pl.BlockSpec((K, TN), lambda j, s: (0, j))
        out_shape = SDS((K, N), bf16)
    return pl.pallas_call(
        body, name=name, grid=(N // TN, S // TS),
        in_specs=[pl.BlockSpec((TS, K), lambda j, s: (s, 0)), pl.BlockSpec((TS, TN), lambda j, s: (s, j))],
        out_specs=out_spec, out_shape=out_shape,
        scratch_shapes=[pltpu.VMEM((K, TN), f32)],
        compiler_params=_cp(2, VMEM_LIMIT),
    )(a, b)


def _w_in_pieces():
    n_a, n_g = 4 * GDN_W, 2 * GDN_HEADS
    cb = IN_COLS // N_DEV
    bounds = [(0, n_a, COL_A), (n_a, n_a + n_g, COL_G), (n_a + n_g, IN_COLS, COL_B)]
    out = []
    for j in range(N_DEV):
        lo, hi = j * cb, (j + 1) * cb
        for s0, s1, dst in bounds:
            a, b = max(lo, s0), min(hi, s1)
            if a < b:
                out.append((j, a - lo, b - a, dst + a - s0))
    return out


def _wcat_from_blocks(g_in):
    TR = 256
    cb = IN_COLS // N_DEV
    pieces = _w_in_pieces()

    def body(w_ref, o_ref):
        o_ref[:, COL_G:NCOL] = jnp.zeros((TR, NCOL - COL_G), bf16)
        for j, off, w, dst in pieces:
            o_ref[:, dst:dst + w] = w_ref[j, :, off:off + w]

    return pl.pallas_call(
        body, name="wcat_from_blocks", grid=(D_MODEL // TR,),
        in_specs=[pl.BlockSpec((N_DEV, TR, cb), lambda i: (0, i, 0))],
        out_specs=pl.BlockSpec((TR, NCOL), lambda i: (i, 0)),
        out_shape=SDS((D_MODEL, NCOL), bf16),
        compiler_params=_cp(1, VMEM_LIMIT),
    )(g_in)


def _full_from_col_blocks(g):
    n, R, C = g.shape
    TR = 256

    def body(w_ref, o_ref):
        for j in range(n):
            o_ref[:, j * C:(j + 1) * C] = w_ref[j]

    return pl.pallas_call(
        body, name="full_from_col_blocks", grid=(R // TR,),
        in_specs=[pl.BlockSpec((n, TR, C), lambda i: (0, i, 0))],
        out_specs=pl.BlockSpec((TR, n * C), lambda i: (i, 0)),
        out_shape=SDS((R, n * C), g.dtype),
        compiler_params=_cp(1, VMEM_LIMIT),
    )(g)


def _wgrad_in(h1, segs, S):
    TS = 512
    n = len(segs)
    cols = [(c0, a.shape[1]) for a, c0 in segs]
    cb = IN_COLS // N_DEV
    pieces = _w_in_pieces()

    def body(*refs):
        h_ref = refs[0]
        seg_refs = refs[1:1 + n]
        o_ref, acc = refs[1 + n], refs[2 + n]

        @pl.when(pl.program_id(0) == 0)
        def _():
            acc[...] = jnp.zeros_like(acc)

        h = h_ref[...]
        for s_ref, (c0, w) in zip(seg_refs, cols):
            acc[:, c0:c0 + w] += _dot_tn(h, s_ref[...])

        @pl.when(pl.program_id(0) == pl.num_programs(0) - 1)
        def _():
            for j, off, w, src in pieces:
                o_ref[j, :, off:off + w] = acc[:, src:src + w].astype(bf16)

    row = lambda w: pl.BlockSpec((TS, w), lambda i: (i, 0))
    return pl.pallas_call(
        body, name="wgrad_in", grid=(S // TS,),
        in_specs=[row(D_MODEL)] + [row(w) for _, w in cols],
        out_specs=_const_spec((N_DEV, D_MODEL, cb)),
        out_shape=SDS((N_DEV, D_MODEL, cb), bf16),
        scratch_shapes=[pltpu.VMEM((D_MODEL, NCOL), f32)],
        compiler_params=_cp(1, VMEM_LIMIT),
    )(h1, *[a for a, _ in segs])


def _adamw(recv, w, m, v, name):
    R, C = w.shape
    TR = 256 if R % 256 == 0 else R
    c1 = 1.0 / (1.0 - ADAM_B1 ** ADAM_STEP)
    c2 = 1.0 / (1.0 - ADAM_B2 ** ADAM_STEP)

    def body(r_ref, w_ref, m_ref, v_ref, g_out, d_out, m_out, v_out):
        g = r_ref[0].astype(f32)
        for s in range(1, N_DEV):
            g = g + r_ref[s].astype(f32)
        mn = ADAM_B1 * m_ref[...] + (1.0 - ADAM_B1) * g
        vn = ADAM_B2 * v_ref[...] + (1.0 - ADAM_B2) * (g * g)
        g_out[...] = g
        m_out[...] = mn
        v_out[...] = vn
        d_out[...] = -ADAM_LR * ((mn * c1) / (jnp.sqrt(vn * c2) + ADAM_EPS) + ADAM_WD * w_ref[...])

    blk = pl.BlockSpec((TR, C), lambda i: (i, 0))
    return pl.pallas_call(
        body, name=name, grid=(R // TR,),
        in_specs=[pl.BlockSpec((N_DEV, TR, C), lambda i: (0, i, 0)), blk, blk, blk],
        out_specs=[blk, blk, blk, blk],
        out_shape=[SDS((R, C), f32)] * 4,
        compiler_params=_cp(1, VMEM_LIMIT),
    )(recv, w, m, v)


MESH = pl.DeviceIdType.MESH
_ANY = pl.BlockSpec(memory_space=pl.ANY)


def _flip(v, d):
    return 1 - v if d else v


def _all_gather(shards):
    n = len(shards)

    def body(*refs):
        ins = refs[:n]
        outs = refs[n:2 * n]
        send_sems, recv_sems, local_sems = refs[2 * n:]
        x, y, c = lax.axis_index("x"), lax.axis_index("y"), lax.axis_index("c")
        me, sibling = (x, y, c), (x, y, 1 - c)
        chips = [(1 - x, y), (x, 1 - y), (1 - x, 1 - y)]

        def slot(px, py, pc):
            return 4 * px + 2 * py + pc

        def copy(a, k, block, to, src=None):
            dst = outs[a].at[slot(*block)]
            return pltpu.make_async_remote_copy(src_ref=dst if src is None else src, dst_ref=dst,
                                                send_sem=send_sems.at[a, k], recv_sem=recv_sems.at[a, k],
                                                device_id=to, device_id_type=MESH)

        mine, first, passed = [], [], []
        for a in range(n):
            cp = pltpu.make_async_copy(ins[a], outs[a].at[slot(*me)], local_sems.at[a])
            cp.start()
            mine.append(cp)
            fs = [copy(a, 0, me, sibling, src=ins[a])]
            fs += [copy(a, 1 + j, me, (*chip, c), src=ins[a]) for j, chip in enumerate(chips)]
            for cp in fs:
                cp.start()
            first += fs
        for j, chip in enumerate(chips):
            for a in range(n):
                copy(a, 1 + j, (*chip, c), me).wait_recv()
                cp = copy(a, 4 + j, (*chip, c), sibling)
                cp.start()
                passed.append(cp)
        for a in range(n):
            copy(a, 0, sibling, me).wait_recv()
            for j, chip in enumerate(chips):
                copy(a, 4 + j, (*chip, 1 - c), me).wait_recv()
        for cp in first + passed:
            cp.wait_send()
        for cp in mine:
            cp.wait()

    return pl.pallas_call(
        body, name="weight_all_gather",
        in_specs=[_ANY] * n, out_specs=[_ANY] * n,
        out_shape=[SDS((N_DEV,) + s.shape, s.dtype) for s in shards],
        scratch_shapes=[pltpu.SemaphoreType.DMA((n, 7)), pltpu.SemaphoreType.DMA((n, 7)), pltpu.SemaphoreType.DMA((n,))],
        compiler_params=pltpu.CompilerParams(has_side_effects=True),
    )(*shards)


def _grad_exchange(blocked, whole):
    arrs = list(blocked) + list(whole)
    n, nb = len(arrs), len(blocked)
    rel = [(dx, dy, dc) for dx in (0, 1) for dy in (0, 1) for dc in (0, 1) if dx or dy or dc]

    def body(*refs):
        ins = refs[:n]
        outs = refs[n:2 * n]
        send_sems, recv_sems, local_sems = refs[2 * n:]
        x, y, c = lax.axis_index("x"), lax.axis_index("y"), lax.axis_index("c")
        me = 4 * x + 2 * y + c
        sends, locs = [], []
        for a in range(n):
            cp = pltpu.make_async_copy(ins[a].at[me] if a < nb else ins[a], outs[a].at[me], local_sems.at[a])
            cp.start()
            locs.append(cp)
            for k, (dx, dy, dc) in enumerate(rel):
                peer = (_flip(x, dx), _flip(y, dy), _flip(c, dc))
                pidx = 4 * peer[0] + 2 * peer[1] + peer[2]
                cp = pltpu.make_async_remote_copy(src_ref=ins[a].at[pidx] if a < nb else ins[a], dst_ref=outs[a].at[me],
                                                  send_sem=send_sems.at[a, k], recv_sem=recv_sems.at[a, k],
                                                  device_id=peer, device_id_type=MESH)
                cp.start()
                sends.append(cp)
        for a in range(n):
            for k, (dx, dy, dc) in enumerate(rel):
                peer = (_flip(x, dx), _flip(y, dy), _flip(c, dc))
                pidx = 4 * peer[0] + 2 * peer[1] + peer[2]
                pltpu.make_async_remote_copy(src_ref=outs[a].at[pidx], dst_ref=outs[a].at[pidx],
                                             send_sem=send_sems.at[a, k], recv_sem=recv_sems.at[a, k],
                                             device_id=peer, device_id_type=MESH).wait_recv()
        for cp in sends:
            cp.wait_send()
        for cp in locs:
            cp.wait()

    shapes = [SDS(a.shape, a.dtype) for a in blocked] + [SDS((N_DEV,) + a.shape, a.dtype) for a in whole]
    return pl.pallas_call(
        body, name="grad_exchange",
        in_specs=[_ANY] * n, out_specs=[_ANY] * n, out_shape=shapes,
        scratch_shapes=[pltpu.SemaphoreType.DMA((n, 7)), pltpu.SemaphoreType.DMA((n, 7)), pltpu.SemaphoreType.DMA((n,))],
        compiler_params=pltpu.CompilerParams(has_side_effects=True),
    )(*arrs)


_HBM = pl.BlockSpec(memory_space=pltpu.HBM)
_SEM = pl.BlockSpec(memory_space=pltpu.SEMAPHORE)
_REL = [(dx, dy, dc) for dx in (0, 1) for dy in (0, 1) for dc in (0, 1) if dx or dy or dc]


N_PEER = len(_REL)
_EFFECT = pltpu.SideEffectType.DATAFLOW_SIDE_EFFECTING


def _peer_copies(srcs, lands, send_sems, recv_sems, blocked, as_receiver):
    x, y, c = lax.axis_index("x"), lax.axis_index("y"), lax.axis_index("c")
    me = 4 * x + 2 * y + c
    cps = []
    for a in range(len(srcs)):
        for k, (dx, dy, dc) in enumerate(_REL):
            peer = (_flip(x, dx), _flip(y, dy), _flip(c, dc))
            pidx = 4 * peer[0] + 2 * peer[1] + peer[2]
            cps.append(pltpu.make_async_remote_copy(
                src_ref=srcs[a].at[pidx] if blocked else srcs[a], dst_ref=lands[a].at[pidx if as_receiver else me],
                send_sem=send_sems[a * N_PEER + k], recv_sem=recv_sems[a * N_PEER + k],
                device_id=peer, device_id_type=MESH))
    return cps


def _exchange_start(srcs, after, blocked, name):
    n = len(srcs)
    ns = n * N_PEER
    lands = [lax.empty(s.shape if blocked else (N_DEV,) + s.shape, s.dtype) for s in srcs]

    def body(*refs):
        ins, lnd = refs[:n], refs[n:2 * n]
        outs = refs[2 * n + 1:]
        for cp in _peer_copies(ins, lnd, outs[:ns], outs[ns:2 * ns], blocked, False):
            cp.start()
        outs[-1][...] = jnp.zeros_like(outs[-1])

    res = pl.pallas_call(
        body, name=name,
        in_specs=[_HBM] * (2 * n) + [_ANY],
        out_specs=[_SEM] * (2 * ns) + [_HBM] * (2 * n) + [pl.BlockSpec(memory_space=pltpu.VMEM)],
        out_shape=[pltpu.SemaphoreType.DMA(())] * (2 * ns) + [pltpu.HBM(s.shape, s.dtype) for s in srcs]
                  + [pltpu.HBM(l.shape, l.dtype) for l in lands] + [SDS((8, LANE), f32)],
        input_output_aliases={i: 2 * ns + i for i in range(2 * n)},
        compiler_params=pltpu.CompilerParams(has_side_effects=_EFFECT),
    )(*[pltpu.with_memory_space_constraint(s, pltpu.HBM) for s in srcs],
      *[pltpu.with_memory_space_constraint(l, pltpu.HBM) for l in lands], after)
    return list(res[:2 * ns]), list(res[2 * ns:2 * ns + n]), list(res[2 * ns + n:2 * ns + 2 * n]), res[-1]


def _exchange_wait(sems, srcs, lands, after, blocked, name):
    n = len(srcs)
    ns = n * N_PEER

    def body(*refs):
        ins, lnd = refs[:n], refs[n:2 * n]
        sem_refs = refs[2 * n:2 * n + 2 * ns]
        for cp in _peer_copies(ins, lnd, sem_refs[:ns], sem_refs[ns:], blocked, True):
            cp.wait_send()
            cp.wait_recv()

    res = pl.pallas_call(
        body, name=name,
        in_specs=[_HBM] * (2 * n) + [_SEM] * (2 * ns) + [_ANY],
        out_specs=[_HBM] * (2 * n),
        out_shape=[pltpu.HBM(s.shape, s.dtype) for s in srcs] + [pltpu.HBM(l.shape, l.dtype) for l in lands],
        input_output_aliases={i: i for i in range(2 * n)},
        compiler_params=pltpu.CompilerParams(has_side_effects=_EFFECT),
    )(*srcs, *lands, *sems, after)
    return list(res[n:])


def _local_step(x, tgt, wcat, convw, late_weights, early_grads, token, a_log, dt_bias, onorm_g, rel_bias,
                g_mix_pre, g_mix_post, g_ffn_pre, g_ffn_post):
    S = x.shape[0]
    bk_np = _bucket_tables()
    bk = jnp.asarray(bk_np)
    bt = _bias_tables(rel_bias, bk)
    proj = _proj_fwd(x, g_mix_pre, wcat, token, S)
    nu = S // CHUNK * GDN_HEADS
    qkv_u = _gdn_prep(proj, convw, S).reshape(3, nu, CHUNK, GDN_HD)
    intra, t_inv = _gdn_intra_fwd(qkv_u, proj, a_log, dt_bias, S)
    oa, states = _gdn_scan_fwd(intra, proj, onorm_g, S)
    ob, lse = _swa_fwd(proj, bt, S)
    wout, wgate, wup, wdown = late_weights(ob)
    mix, x1 = _mix_fwd(oa, ob, wout, x, g_mix_post, S)
    dx1, h2, act, dgate_f, dup_f, df, loss, d_gfpre, d_gfpost = _ffn(x1, tgt, g_ffn_pre, g_ffn_post, wgate, wup, wdown, S)
    g_gate = _wgrad(h2, dgate_f, S, "wgrad_gate", col_blocks=True)
    g_up = _wgrad(h2, dup_f, S, "wgrad_up", col_blocks=True)
    g_down = _wgrad(act, df, S, "wgrad_down").reshape(N_DEV, D_FF // N_DEV, D_MODEL)
    dmix, d_oab, d_gmpost = _mix_bwd(dx1, mix, g_mix_post, wout, S)
    g_out = jnp.concatenate([_wgrad(oa, dmix, S, "wgrad_out_a"), _wgrad(ob, dmix, S, "wgrad_out_b")], axis=0)
    token = early_grads(g_out.reshape(N_DEV, D_MODEL // N_DEV, D_MODEL), g_gate, g_up, g_down)
    dqb, dkb, dvb, dsb = _swa_bwd(proj, bt, ob, lse, d_oab, token, S)
    *cots, dgate_a, d_og = _gdn_scan_bwd(intra, states, proj, d_oab, onorm_g, token, S)
    dqkv_u, dpg, d_alog, d_dtb = _gdn_intra_bwd(qkv_u, proj, a_log, dt_bias, t_inv, cots, S)
    dqkv_a, d_conv = _gdn_prep_bwd(proj, convw, dqkv_u.reshape(3, S // CHUNK, GDN_HEADS, CHUNK, GDN_HD), S)
    segs = [(dqkv_a, COL_A), (dgate_a, COL_A + 3 * GDN_W), (dqb, COL_B), (dkb, COL_B + SWA_W), (dvb, COL_B + 2 * SWA_W),
            (dpg, COL_G)]
    grad_x, h1, d_gmpre = _proj_bwd(x, dx1, g_mix_pre, wcat, segs, S)
    g_in = _wgrad_in(h1, segs, S)
    d_rel = _rel_bias_grad(dsb, bk, bk_np)
    small = dict(a_log=d_alog[:, :GDN_HEADS], dt_bias=d_dtb[:, :GDN_HEADS], onorm_g=d_og, rel_bias=d_rel,
                 g_mix_pre=d_gmpre, g_mix_post=d_gmpost, g_ffn_pre=d_gfpre, g_ffn_post=d_gfpost)
    return loss, grad_x, (g_in, d_conv), small


SMALL = ("a_log", "dt_bias", "onorm_g", "rel_bias", "g_mix_pre", "g_mix_post", "g_ffn_pre", "g_ffn_post")
PACK_ROWS = 8


def _pack_small(d, loss=None):
    rest = jnp.concatenate([d["onorm_g"].reshape(-1), d["a_log"].reshape(-1), d["dt_bias"].reshape(-1),
                            d["rel_bias"].reshape(-1)])
    rest = jnp.concatenate([rest, jnp.zeros((D_MODEL - rest.shape[0],), f32)])
    extra = jnp.zeros((D_MODEL,), f32) if loss is None else jnp.concatenate([loss.reshape(1), jnp.zeros((D_MODEL - 1,), f32)])
    rows = [d["g_mix_pre"].reshape(-1), d["g_mix_post"].reshape(-1), d["g_ffn_pre"].reshape(-1),
            d["g_ffn_post"].reshape(-1), rest, extra]
    return jnp.concatenate([jnp.stack(rows), jnp.zeros((PACK_ROWS - len(rows), D_MODEL), f32)], axis=0)


def _unpack_small(p):
    o = GDN_HD
    return dict(g_mix_pre=p[0:1], g_mix_post=p[1:2], g_ffn_pre=p[2:3], g_ffn_post=p[3:4],
                onorm_g=p[4:5, :o], a_log=p[4:5, o:o + 4], dt_bias=p[4:5, o + 4:o + 8],
                rel_bias=p[4, o + 8:o + 8 + NUM_BUCKETS * SWA_HEADS].reshape(NUM_BUCKETS, SWA_HEADS))


def kernel(x, w_in, conv_w, a_log, dt_bias, onorm_g, rel_bias, w_out, g_mix_pre, g_mix_post, w_gate, w_up, w_down, g_ffn_pre, g_ffn_post, loss_target, m_w_in, m_conv_w, m_a_log, m_dt_bias, m_onorm_g, m_rel_bias, m_w_out, m_g_mix_pre, m_g_mix_post, m_w_gate, m_w_up, m_w_down, m_g_ffn_pre, m_g_ffn_post, v_w_in, v_conv_w, v_a_log, v_dt_bias, v_onorm_g, v_rel_bias, v_w_out, v_g_mix_pre, v_g_mix_post, v_w_gate, v_w_up, v_w_down, v_g_ffn_pre, v_g_ffn_post):
    big = ("w_in", "conv_w", "w_out", "w_gate", "w_up", "w_down")
    w_sh = dict(w_in=w_in[0], conv_w=conv_w[0], w_out=w_out[0], w_gate=w_gate[0], w_up=w_up[0], w_down=w_down[0])
    m_sh = dict(w_in=m_w_in[0], conv_w=m_conv_w[0], w_out=m_w_out[0], w_gate=m_w_gate[0], w_up=m_w_up[0], w_down=m_w_down[0])
    v_sh = dict(w_in=v_w_in[0], conv_w=v_conv_w[0], w_out=v_w_out[0], w_gate=v_w_gate[0], w_up=v_w_up[0], w_down=v_w_down[0])
    w_small = dict(a_log=a_log, dt_bias=dt_bias, onorm_g=onorm_g, rel_bias=rel_bias, g_mix_pre=g_mix_pre,
                   g_mix_post=g_mix_post, g_ffn_pre=g_ffn_pre, g_ffn_post=g_ffn_post)
    m_small = dict(a_log=m_a_log, dt_bias=m_dt_bias, onorm_g=m_onorm_g, rel_bias=m_rel_bias, g_mix_pre=m_g_mix_pre,
                   g_mix_post=m_g_mix_post, g_ffn_pre=m_g_ffn_pre, g_ffn_post=m_g_ffn_post)
    v_small = dict(a_log=v_a_log, dt_bias=v_dt_bias, onorm_g=v_onorm_g, rel_bias=v_rel_bias, g_mix_pre=v_g_mix_pre,
                   g_mix_post=v_g_mix_post, g_ffn_pre=v_g_ffn_pre, g_ffn_post=v_g_ffn_post)

    me = 4 * lax.axis_index("x") + 2 * lax.axis_index("y") + lax.axis_index("c")
    own = lambda full, part: lax.dynamic_update_index_in_dim(full, part, me, 0)
    cols = lambda g: g.reshape(g.shape[0], N_DEV, g.shape[1] // N_DEV).transpose(1, 0, 2)
    late = ("w_out", "w_gate", "w_up", "w_down")

    late_src = [w_sh[k].astype(bf16) for k in late]
    g_in, g_conv = _all_gather([w_sh["w_in"].astype(bf16), w_sh["conv_w"]])
    g_sems, g_src, g_land, g_token = _exchange_start(late_src, g_conv, False, "late_weights_start")
    wcat = _wcat_from_blocks(g_in)
    convw = g_conv.transpose(1, 0, 2).reshape(4, 3 * GDN_W)

    def late_weights(after):
        lands = _exchange_wait(g_sems, g_src, g_land, after, False, "late_weights_wait")
        g_out, g_gate, g_up, g_down = [own(l, s) for l, s in zip(lands, late_src)]
        return (g_out.reshape(D_MODEL, D_MODEL), _full_from_col_blocks(g_gate), _full_from_col_blocks(g_up),
                g_down.reshape(D_FF, D_MODEL))

    early = {}

    def early_grads(*blocks):
        early["sems"], early["src"], early["land"], token = _exchange_start(list(blocks), blocks[0], True, "late_grads_start")
        return token

    loss_p, grad_x, (gw_in, gw_conv), gsmall = _local_step(
        x[0], loss_target[0], wcat, convw, late_weights, early_grads, g_token,
        a_log, dt_bias, onorm_g, rel_bias, g_mix_pre, g_mix_post, g_ffn_pre, g_ffn_post)

    r_in, r_conv, r_small = _grad_exchange([gw_in, cols(gw_conv)], [_pack_small(gsmall, loss_p[0, 0])])
    lands = _exchange_wait(early["sems"], early["src"], early["land"], grad_x, True, "late_grads_wait")
    recv = dict(w_in=r_in, conv_w=r_conv)
    for k, l, s in zip(late, lands, early["src"]):
        recv[k] = own(l, lax.dynamic_index_in_dim(s, me, 0, keepdims=False))

    outs = {}
    for k in big:
        outs[k] = _adamw(recv[k], w_sh[k], m_sh[k], v_sh[k], "adamw_" + k)
    sm = _adamw(r_small, _pack_small(w_small), _pack_small(m_small), _pack_small(v_small), "adamw_small")
    loss = sm[0][5, 0]
    sm = [_unpack_small(t) for t in sm]
    for k in SMALL:
        outs[k] = tuple(t[k].reshape(w_small[k].shape) for t in sm)

    order = ("w_in", "conv_w", "a_log", "dt_bias", "onorm_g", "rel_bias", "w_out", "g_mix_pre", "g_mix_post", "w_gate",
             "w_up", "w_down", "g_ffn_pre", "g_ffn_post")
    lead = lambda k, t: t[None] if k in big else t
    res = [loss, grad_x[None]]
    for i in range(4):
        res += [lead(k, outs[k][i]) for k in order]
    return tuple(res)
```

```python
import functools
import math

import numpy as np
import jax
import jax.numpy as jnp
from jax import lax
from jax.experimental import pallas as pl
from jax.experimental.pallas import tpu as pltpu

f32 = jnp.float32
bf16 = jnp.bfloat16
SDS = jax.ShapeDtypeStruct

D_MODEL = 1024
GDN_HEADS = 4
GDN_HD = 128
GDN_W = 512
CHUNK = 64
SWA_HEADS = 8
SWA_HD = 64
SWA_W = 512
D_FF = 2816
IN_COLS = 3592
PATTERNS = ((128, 1), (512, 4), (2048, 16))
SWA_BLK = 128
NUM_BUCKETS = 32
MAX_DISTANCE = 2048
RMS_EPS = 1e-6
NEG = -1e30
N_DEV = 8

COL_A = 0
COL_B = 2048
COL_G = 3584
NCOL = 3712
LANE = 128

ADAM_LR, ADAM_B1, ADAM_B2, ADAM_EPS, ADAM_WD, ADAM_STEP = 0.001, 0.9, 0.999, 1e-08, 0.01, 10

VMEM_LIMIT = 56 * 1024 * 1024

HI = lax.Precision.HIGHEST
HIGH = lax.Precision.HIGH


def _cp(n_grid=0, vmem=None):
    kw = {}
    if n_grid:
        kw["dimension_semantics"] = ("arbitrary",) * n_grid
    if vmem:
        kw["vmem_limit_bytes"] = vmem
    return pltpu.CompilerParams(**kw)


def _dot(a, b):
    return jnp.dot(a, b, preferred_element_type=f32)


def _dot_nt(a, b):
    return lax.dot_general(a, b, (((1,), (1,)), ((), ())), preferred_element_type=f32)


def _dot_tn(a, b):
    return lax.dot_general(a, b, (((0,), (0,)), ((), ())), preferred_element_type=f32)


def _dot_hi(a, b):
    return jnp.dot(a, b, precision=HI, preferred_element_type=f32)


def _sigmoid(x):
    return 1.0 / (1.0 + jnp.exp(-x))


def _softplus(x):
    return jnp.maximum(x, 0.0) + jnp.log(1.0 + jnp.exp(-jnp.abs(x)))


def _const_spec(shape):
    nd = len(shape)
    return pl.BlockSpec(shape, lambda *_: (0,) * nd)


def _resident_spec(shape):
    nd = len(shape)
    return pl.BlockSpec(shape, lambda *_: (0,) * nd, pipeline_mode=pl.Buffered(1))


def _t5_bucket_np(dist):
    max_exact = NUM_BUCKETS // 2
    d = np.maximum(dist, 1).astype(np.float32)
    log_b = max_exact + (np.log(d / np.float32(max_exact)) / np.float32(math.log(MAX_DISTANCE / max_exact))
                         * np.float32(NUM_BUCKETS - max_exact)).astype(np.int32)
    return np.where(dist < max_exact, dist, np.minimum(log_b, NUM_BUCKETS - 1)).astype(np.int32)


def _bucket_tables():
    w = SWA_BLK
    qi = np.arange(w)[:, None]
    kj = np.arange(w)[None, :]
    rel = np.where(kj <= qi, qi - kj, qi + w - kj)
    out = np.zeros((len(PATTERNS), w, w), np.int32)
    for p, (_, dil) in enumerate(PATTERNS):
        steps = _t5_bucket_np(np.arange(w + 1) * dil)
        assert steps[w] == steps[w - 1]
        out[p] = steps[rel]
    return out


def _bias_tables(rel_bias, bk):
    def body(rb_ref, bk_ref, o_ref):
        b_idx = bk_ref[0]
        for h in range(SWA_HEADS):
            def lp(b, acc):
                return jnp.where(b_idx == b, rb_ref[b, h], acc)
            o_ref[0, h] = lax.fori_loop(0, NUM_BUCKETS, lp, jnp.zeros((SWA_BLK, SWA_BLK), f32))

    return pl.pallas_call(
        body, name="bias_tables", grid=(3,),
        in_specs=[pl.BlockSpec(memory_space=pltpu.SMEM), pl.BlockSpec((1, SWA_BLK, SWA_BLK), lambda p: (p, 0, 0))],
        out_specs=pl.BlockSpec((1, SWA_HEADS, SWA_BLK, SWA_BLK), lambda p: (p, 0, 0, 0)),
        out_shape=SDS((3, SWA_HEADS, SWA_BLK, SWA_BLK), f32),
        compiler_params=_cp(1),
    )(rel_bias, bk)


def _rel_bias_grad(dsb, bk, bk_np):
    present = [sorted(set(int(v) for v in np.unique(bk_np[p]))) for p in range(3)]

    def body(ds_ref, bk_ref, o_ref):
        row = lax.broadcasted_iota(jnp.int32, (NUM_BUCKETS, LANE), 0)
        col = lax.broadcasted_iota(jnp.int32, (NUM_BUCKETS, SWA_HEADS), 1)
        out = jnp.zeros((NUM_BUCKETS, SWA_HEADS), f32)
        for hp in range(4):
            for hh in range(2):
                acc = jnp.zeros((NUM_BUCKETS, LANE), f32)
                for p in range(3):
                    tile = ds_ref[hp, p, hh]
                    b_idx = bk_ref[p]
                    for b in present[p]:
                        part = jnp.sum(jnp.where(b_idx == b, tile, 0.0), axis=0, keepdims=True)
                        acc = acc + jnp.where(row == b, part, 0.0)
                tot = jnp.sum(acc, axis=1, keepdims=True)
                out = out + jnp.where(col == 2 * hp + hh, tot, 0.0)
        o_ref[...] = out

    return pl.pallas_call(body, name="rel_bias_grad", out_shape=SDS((NUM_BUCKETS, SWA_HEADS), f32),
                          compiler_params=_cp(0, 32 * 1024 * 1024))(dsb, bk)


def _proj_fwd(x, g_pre, wcat, after, S):
    TS = 256

    def body(x_ref, g_ref, w_ref, after_ref, o_ref):
        xv = x_ref[...]
        r = lax.rsqrt(jnp.mean(xv * xv, axis=-1, keepdims=True) + RMS_EPS)
        h = (xv * r * g_ref[...]).astype(bf16)
        o_ref[...] = _dot(h, w_ref[...])

    return pl.pallas_call(
        body, name="proj_fwd", grid=(S // TS,),
        in_specs=[pl.BlockSpec((TS, D_MODEL), lambda i: (i, 0)), _const_spec((1, D_MODEL)),
                  _resident_spec((D_MODEL, NCOL)), _ANY],
        out_specs=pl.BlockSpec((TS, NCOL), lambda i: (i, 0)),
        out_shape=SDS((S, NCOL), f32),
        compiler_params=_cp(1, VMEM_LIMIT),
    )(x, g_pre, wcat, after)


CONV_RT = 256
HALO = 8


CONV_NC = CONV_RT // CHUNK


def _gdn_prep(proj, conv_w, S):
    def body(p_ref, cw_ref, o_ref, xs_ref):
        t = pl.program_id(0)
        xs_ref[pl.ds(0, HALO), :] = jnp.zeros((HALO, LANE), f32)
        xs_ref[pl.ds(HALO, S), :] = p_ref[...]
        w = cw_ref[...]
        is_qk = t < 2
        scale = jnp.where(t == 0, GDN_HD ** -0.5, 1.0).astype(f32)

        def lp(c, carry):
            st = pl.multiple_of(c * CONV_RT, CONV_RT)
            pre = xs_ref[pl.ds(st + HALO - 3, CONV_RT), :] * w[0:1, :]
            for i in range(1, 4):
                pre = pre + xs_ref[pl.ds(st + HALO - 3 + i, CONV_RT), :] * w[i:i + 1, :]
            s = pre * _sigmoid(pre)
            nrm = s * lax.rsqrt(jnp.sum(s * s, axis=-1, keepdims=True) + 1e-6) * scale
            out = jnp.where(is_qk, nrm, s)
            for i in range(CONV_NC):
                o_ref[0, c * CONV_NC + i, 0] = out[i * CHUNK:(i + 1) * CHUNK]
            return carry

        lax.fori_loop(0, S // CONV_RT, lp, 0)

    return pl.pallas_call(
        body, name="gdn_prep", grid=(3, GDN_HEADS),
        in_specs=[pl.BlockSpec((S, LANE), lambda t, h: (0, t * GDN_HEADS + h)),
                  pl.BlockSpec((4, LANE), lambda t, h: (0, t * GDN_HEADS + h))],
        out_specs=pl.BlockSpec((1, S // CHUNK, 1, CHUNK, GDN_HD), lambda t, h: (t, 0, h, 0, 0)),
        out_shape=SDS((3, S // CHUNK, GDN_HEADS, CHUNK, GDN_HD), f32),
        scratch_shapes=[pltpu.VMEM((S + HALO, LANE), f32)],
        compiler_params=_cp(2, VMEM_LIMIT),
    )(proj, conv_w)


def _gdn_prep_bwd(proj, conv_w, dqkv, S):
    def body(p_ref, cw_ref, d_ref, dx_ref, dw_ref, xs_ref, dp_ref):
        t = pl.program_id(0)
        xs_ref[pl.ds(0, HALO), :] = jnp.zeros((HALO, LANE), f32)
        xs_ref[pl.ds(HALO, S), :] = p_ref[...]
        dp_ref[pl.ds(S, HALO), :] = jnp.zeros((HALO, LANE), f32)
        w = cw_ref[...]
        is_qk = t < 2
        scale = jnp.where(t == 0, GDN_HD ** -0.5, 1.0).astype(f32)

        def lp1(c, dw):
            st = pl.multiple_of(c * CONV_RT, CONV_RT)
            taps = [xs_ref[pl.ds(st + HALO - 3 + i, CONV_RT), :] for i in range(4)]
            pre = taps[0] * w[0:1, :]
            for i in range(1, 4):
                pre = pre + taps[i] * w[i:i + 1, :]
            sg = _sigmoid(pre)
            s = pre * sg
            d_out = jnp.concatenate([d_ref[0, c * CONV_NC + i, 0] for i in range(CONV_NC)], axis=0)
            rn = lax.rsqrt(jnp.sum(s * s, axis=-1, keepdims=True) + 1e-6)
            n = s * rn
            dn = d_out * scale
            ds_qk = rn * (dn - n * jnp.sum(dn * n, axis=-1, keepdims=True))
            ds = jnp.where(is_qk, ds_qk, d_out)
            dpre = ds * (sg * (1.0 + pre * (1.0 - sg)))
            dp_ref[pl.ds(st, CONV_RT), :] = dpre
            return tuple(dw[i] + jnp.sum(dpre * taps[i], axis=0, keepdims=True) for i in range(4))

        z = jnp.zeros((1, LANE), f32)
        dw = lax.fori_loop(0, S // CONV_RT, lp1, (z, z, z, z))
        for i in range(4):
            dw_ref[pl.ds(i, 1), :] = dw[i]

        def lp2(c, carry):
            st = pl.multiple_of(c * CONV_RT, CONV_RT)
            dx = dp_ref[pl.ds(st, CONV_RT), :] * w[3:4, :]
            for i in range(3):
                dx = dx + dp_ref[pl.ds(st + 3 - i, CONV_RT), :] * w[i:i + 1, :]
            dx_ref[pl.ds(st, CONV_RT), :] = dx.astype(bf16)
            return carry

        lax.fori_loop(0, S // CONV_RT, lp2, 0)

    col = lambda rows: pl.BlockSpec((rows, LANE), lambda t, h: (0, t * GDN_HEADS + h))
    return pl.pallas_call(
        body, name="gdn_prep_bwd", grid=(3, GDN_HEADS),
        in_specs=[col(S), col(4), pl.BlockSpec((1, S // CHUNK, 1, CHUNK, GDN_HD), lambda t, h: (t, 0, h, 0, 0))],
        out_specs=[col(S), col(4)],
        out_shape=[SDS((S, 3 * GDN_W), bf16), SDS((4, 3 * GDN_W), f32)],
        scratch_shapes=[pltpu.VMEM((S + HALO, LANE), f32), pltpu.VMEM((S + HALO, LANE), f32)],
        compiler_params=_cp(2, VMEM_LIMIT),
    )(proj, conv_w, dqkv)


def _bdot(a, b, prec=None):
    return lax.dot_general(a, b, (((2,), (1,)), ((0,), (0,))), precision=prec, preferred_element_type=f32)


def _bdot_nt(a, b, prec=None):
    return lax.dot_general(a, b, (((2,), (2,)), ((0,), (0,))), precision=prec, preferred_element_type=f32)


def _bdot_tn(a, b, prec=None):
    return lax.dot_general(a, b, (((1,), (1,)), ((0,), (0,))), precision=prec, preferred_element_type=f32)


@jax.custom_vjp
def _tri_inv_saved(a, t):
    return t


def _tri_inv_saved_fwd(a, t):
    return t, t


def _tri_inv_saved_bwd(t, dt):
    return -_bdot_tn(t, _bdot_nt(dt, t, HIGH), HIGH), jnp.zeros_like(t)


_tri_inv_saved.defvjp(_tri_inv_saved_fwd, _tri_inv_saved_bwd)


def _gdn_intra(q, k, v, bl, al, a_log, dt_bias, t_saved=None):
    nb = q.shape[0]
    c = CHUNK
    ii = lax.broadcasted_iota(jnp.int32, (c, c), 0)
    jj = lax.broadcasted_iota(jnp.int32, (c, c), 1)
    eye = ii == jj
    tril = ii >= jj
    strict = ii > jj
    ones = jnp.ones((nb, c, c), f32)
    eye_f = eye.astype(f32)

    beta = _sigmoid(bl)
    g = -jnp.exp(a_log) * _softplus(al + dt_bias)
    g_row = _bdot(ones, jnp.where(eye, g, 0.0), HI)
    gc = jnp.sum(jnp.where(tril, g_row, 0.0), axis=2, keepdims=True)
    gc_row = _bdot(ones, jnp.where(eye, gc, 0.0), HI)
    decay = jnp.where(tril, jnp.exp(jnp.where(tril, gc - gc_row, 0.0)), 0.0)
    last = lax.broadcasted_iota(jnp.int32, (c, 1), 0) == c - 1
    gc_last = jnp.sum(jnp.where(last, gc, 0.0), axis=1, keepdims=True)
    e_gc = jnp.exp(gc)

    kb = k * beta
    k16 = k.astype(bf16)
    a = jnp.where(strict, _bdot_nt(kb.astype(bf16), k16) * decay, 0.0)
    if t_saved is None:
        xp = -a
        t_inv = eye_f + xp
        for _ in range(5):
            xp = _bdot(xp, xp, HIGH)
            t_inv = _bdot(t_inv, eye_f + xp, HIGH)
    else:
        t_inv = _tri_inv_saved(a, t_saved)
    t16 = t_inv.astype(bf16)
    u = _bdot(t16, (v * beta).astype(bf16))
    w = _bdot(t16, (kb * e_gc).astype(bf16))
    attn = jnp.where(tril, _bdot_nt(q.astype(bf16), k16) * decay, 0.0)
    gam = jnp.broadcast_to(jnp.exp(gc_last), (nb, 1, GDN_HD))
    return u, w, attn, q * e_gc, k * jnp.exp(gc_last - gc), gam, t_inv


GDN_TB = 256
GDN_NC = GDN_TB // CHUNK
GDN_NU = GDN_NC * GDN_HEADS


def _gdn_unit_inputs(qkv_ref, pg_ref, al_ref, db_ref):
    pg = pg_ref[...]
    bl = jnp.stack([pg[cl * CHUNK:(cl + 1) * CHUNK, h:h + 1] for cl in range(GDN_NC) for h in range(GDN_HEADS)])
    al = jnp.stack([pg[cl * CHUNK:(cl + 1) * CHUNK, GDN_HEADS + h:GDN_HEADS + h + 1]
                    for cl in range(GDN_NC) for h in range(GDN_HEADS)])
    a_log = jnp.stack([jnp.full((1, 1), al_ref[0, h], f32) for _ in range(GDN_NC) for h in range(GDN_HEADS)])
    dt_b = jnp.stack([jnp.full((1, 1), db_ref[0, h], f32) for _ in range(GDN_NC) for h in range(GDN_HEADS)])
    return qkv_ref[0], qkv_ref[1], qkv_ref[2], bl, al, a_log, dt_b


def _unit_spec(*tail):
    nd = len(tail)
    return pl.BlockSpec((GDN_NU,) + tail, lambda i: (i,) + (0,) * nd)


def _gdn_intra_shapes(S):
    nu = S // CHUNK * GDN_HEADS
    row = SDS((nu, CHUNK, GDN_HD), f32)
    return [row, row, SDS((nu, CHUNK, CHUNK), f32), row, row, SDS((nu, 1, GDN_HD), f32)]


_GDN_INTRA_SPECS = lambda: [_unit_spec(CHUNK, GDN_HD), _unit_spec(CHUNK, GDN_HD), _unit_spec(CHUNK, CHUNK),
                            _unit_spec(CHUNK, GDN_HD), _unit_spec(CHUNK, GDN_HD), _unit_spec(1, GDN_HD)]


def _gdn_intra_fwd(qkv_u, proj, a_log, dt_bias, S):
    def body(qkv_ref, pg_ref, al_ref, db_ref, *outs):
        res = _gdn_intra(*_gdn_unit_inputs(qkv_ref, pg_ref, al_ref, db_ref))
        for o_ref, r in zip(outs, res):
            o_ref[...] = r

    nu = S // CHUNK * GDN_HEADS
    *intra, t_inv = pl.pallas_call(
        body, name="gdn_intra_fwd", grid=(S // GDN_TB,),
        in_specs=[pl.BlockSpec((3, GDN_NU, CHUNK, GDN_HD), lambda i: (0, i, 0, 0)),
                  pl.BlockSpec((GDN_TB, LANE), lambda i: (i, COL_G // LANE)),
                  pl.BlockSpec(memory_space=pltpu.SMEM), pl.BlockSpec(memory_space=pltpu.SMEM)],
        out_specs=_GDN_INTRA_SPECS() + [_unit_spec(CHUNK, CHUNK)],
        out_shape=_gdn_intra_shapes(S) + [SDS((nu, CHUNK, CHUNK), f32)],
        compiler_params=_cp(1, VMEM_LIMIT),
    )(qkv_u, proj, a_log, dt_bias)
    return intra, t_inv


def _gdn_intra_bwd(qkv_u, proj, a_log, dt_bias, t_inv, cots, S):
    def body(qkv_ref, pg_ref, al_ref, db_ref, t_ref, du_ref, dw_ref, da_ref, dqd_ref, dkd_ref, dgm_ref,
             dqkv_ref, dpg_ref, dal_ref, ddb_ref):
        @pl.when(pl.program_id(0) == 0)
        def _():
            dal_ref[...] = jnp.zeros_like(dal_ref)
            ddb_ref[...] = jnp.zeros_like(ddb_ref)

        t_saved = t_ref[...]
        _, vjp = jax.vjp(lambda *a: _gdn_intra(*a, t_saved=t_saved)[:6], *_gdn_unit_inputs(qkv_ref, pg_ref, al_ref, db_ref))
        dq, dk, dv, dbl, dal, da, ddb = vjp((du_ref[...], dw_ref[...], da_ref[...], dqd_ref[...], dkd_ref[...], dgm_ref[...]))
        dqkv_ref[0] = dq
        dqkv_ref[1] = dk
        dqkv_ref[2] = dv
        lane = lax.broadcasted_iota(jnp.int32, (CHUNK, LANE), 1)
        lane1 = lax.broadcasted_iota(jnp.int32, (1, LANE), 1)
        da_tot = jnp.zeros((1, LANE), f32)
        ddb_tot = jnp.zeros((1, LANE), f32)
        for cl in range(GDN_NC):
            dpg = jnp.zeros((CHUNK, LANE), f32)
            for h in range(GDN_HEADS):
                b = cl * GDN_HEADS + h
                dpg = dpg + jnp.where(lane == h, dbl[b], 0.0) + jnp.where(lane == GDN_HEADS + h, dal[b], 0.0)
                da_tot = da_tot + jnp.where(lane1 == h, da[b], 0.0)
                ddb_tot = ddb_tot + jnp.where(lane1 == h, ddb[b], 0.0)
            dpg_ref[cl * CHUNK:(cl + 1) * CHUNK, :] = dpg.astype(bf16)
        dal_ref[...] += da_tot
        ddb_ref[...] += ddb_tot

    acc = _const_spec((1, LANE))
    nu = S // CHUNK * GDN_HEADS
    return pl.pallas_call(
        body, name="gdn_intra_bwd", grid=(S // GDN_TB,),
        in_specs=[pl.BlockSpec((3, GDN_NU, CHUNK, GDN_HD), lambda i: (0, i, 0, 0)),
                  pl.BlockSpec((GDN_TB, LANE), lambda i: (i, COL_G // LANE)),
                  pl.BlockSpec(memory_space=pltpu.SMEM), pl.BlockSpec(memory_space=pltpu.SMEM),
                  _unit_spec(CHUNK, CHUNK)] + _GDN_INTRA_SPECS(),
        out_specs=[pl.BlockSpec((3, GDN_NU, CHUNK, GDN_HD), lambda i: (0, i, 0, 0)),
                   pl.BlockSpec((GDN_TB, LANE), lambda i: (i, 0)), acc, acc],
        out_shape=[SDS((3, nu, CHUNK, GDN_HD), f32), SDS((S, LANE), bf16), SDS((1, LANE), f32), SDS((1, LANE), f32)],
        compiler_params=_cp(1, VMEM_LIMIT),
    )(qkv_u, proj, a_log, dt_bias, t_inv, *cots)


def _gdn_scan_fwd(intra, proj, onorm_g, S):
    def body(u_ref, w_ref, at_ref, qd_ref, kd_ref, gm_ref, gate_ref, og_ref, out_ref, st_ref, s_scr):
        @pl.when(pl.program_id(0) == 0)
        def _():
            s_scr[...] = jnp.zeros_like(s_scr)

        og = og_ref[...]
        s = s_scr[...]
        for cl in range(GDN_NC):
            us = slice(cl * GDN_HEADS, (cl + 1) * GDN_HEADS)
            rows = slice(cl * CHUNK, (cl + 1) * CHUNK)
            st_ref[us] = s
            s16 = s.astype(bf16)
            vn = u_ref[us] - _bdot(w_ref[us].astype(bf16), s16)
            vn16 = vn.astype(bf16)
            o = _bdot(qd_ref[us].astype(bf16), s16) + _bdot(at_ref[us].astype(bf16), vn16)
            s = s * gm_ref[us] + _bdot_tn(kd_ref[us].astype(bf16), vn16)
            for h in range(GDN_HEADS):
                oh = o[h]
                gt = gate_ref[rows, h * GDN_HD:(h + 1) * GDN_HD]
                on = oh * lax.rsqrt(jnp.mean(oh * oh, axis=-1, keepdims=True) + RMS_EPS) * og
                out_ref[rows, h * GDN_HD:(h + 1) * GDN_HD] = on * (gt * _sigmoid(gt))
        s_scr[...] = s

    nu = S // CHUNK * GDN_HEADS
    return pl.pallas_call(
        body, name="gdn_scan_fwd", grid=(S // GDN_TB,),
        in_specs=_GDN_INTRA_SPECS() + [pl.BlockSpec((GDN_TB, GDN_W), lambda i: (i, 3)), _const_spec((1, GDN_HD))],
        out_specs=[pl.BlockSpec((GDN_TB, GDN_W), lambda i: (i, 0)), _unit_spec(GDN_HD, GDN_HD)],
        out_shape=[SDS((S, GDN_W), f32), SDS((nu, GDN_HD, GDN_HD), f32)],
        scratch_shapes=[pltpu.VMEM((GDN_HEADS, GDN_HD, GDN_HD), f32)],
        compiler_params=_cp(1, VMEM_LIMIT),
    )(*intra, proj, onorm_g)


def _gdn_scan_bwd(intra, states, proj, d_oab, onorm_g, after, S):
    n_steps = S // GDN_TB

    def body(u_ref, w_ref, at_ref, qd_ref, kd_ref, gm_ref, st_ref, gate_ref, do_ref, og_ref, after_ref,
             du_ref, dw_ref, dat_ref, dqd_ref, dkd_ref, dgm_ref, dgate_ref, dog_ref, ds_scr):
        @pl.when(pl.program_id(0) == 0)
        def _():
            ds_scr[...] = jnp.zeros_like(ds_scr)
            dog_ref[...] = jnp.zeros_like(dog_ref)

        og = og_ref[...]
        ii = lax.broadcasted_iota(jnp.int32, (CHUNK, CHUNK), 0)
        jj = lax.broadcasted_iota(jnp.int32, (CHUNK, CHUNK), 1)
        tril = ii >= jj
        ds = ds_scr[...]
        dog = jnp.zeros((1, GDN_HD), f32)
        for cl in reversed(range(GDN_NC)):
            us = slice(cl * GDN_HEADS, (cl + 1) * GDN_HEADS)
            rows = slice(cl * CHUNK, (cl + 1) * CHUNK)
            s0 = st_ref[us]
            s016 = s0.astype(bf16)
            w16 = w_ref[us].astype(bf16)
            qd16 = qd_ref[us].astype(bf16)
            kd16 = kd_ref[us].astype(bf16)
            at16 = at_ref[us].astype(bf16)
            vn = u_ref[us] - _bdot(w16, s016)
            vn16 = vn.astype(bf16)
            o = _bdot(qd16, s016) + _bdot(at16, vn16)
            do_h = []
            for h in range(GDN_HEADS):
                oh = o[h]
                lanes = slice(h * GDN_HD, (h + 1) * GDN_HD)
                gt = gate_ref[rows, lanes]
                d_out = do_ref[rows, lanes]
                r = lax.rsqrt(jnp.mean(oh * oh, axis=-1, keepdims=True) + RMS_EPS)
                n = oh * r
                sg = _sigmoid(gt)
                silu = gt * sg
                dog = dog + jnp.sum(d_out * n * silu, axis=0, keepdims=True)
                dgate_ref[rows, lanes] = (d_out * n * og * (sg * (1.0 + gt * (1.0 - sg)))).astype(bf16)
                dn = d_out * og * silu
                do_h.append(r * (dn - n * jnp.mean(dn * n, axis=-1, keepdims=True)))
            do16 = jnp.stack(do_h).astype(bf16)
            ds16 = ds.astype(bf16)
            dvn = _bdot_tn(at16, do16) + _bdot(kd16, ds16)
            dvn16 = dvn.astype(bf16)
            du_ref[us] = dvn
            dw_ref[us] = -_bdot_nt(dvn16, s016)
            dat_ref[us] = jnp.where(tril, _bdot_nt(do16, vn16), 0.0)
            dqd_ref[us] = _bdot_nt(do16, s016)
            dkd_ref[us] = _bdot_nt(vn16, ds16)
            dgm_ref[us] = jnp.sum(s0 * ds, axis=1, keepdims=True)
            ds = _bdot_tn(qd16, do16) + ds * gm_ref[us] - _bdot_tn(w16, dvn16)
        ds_scr[...] = ds
        dog_ref[...] += dog

    def unit(*tail):
        nd = len(tail)
        return pl.BlockSpec((GDN_NU,) + tail, lambda i: (n_steps - 1 - i,) + (0,) * nd)

    intra_specs = [unit(CHUNK, GDN_HD), unit(CHUNK, GDN_HD), unit(CHUNK, CHUNK), unit(CHUNK, GDN_HD),
                   unit(CHUNK, GDN_HD), unit(1, GDN_HD)]
    tok = lambda c: pl.BlockSpec((GDN_TB, GDN_W), lambda i: (n_steps - 1 - i, c))
    return pl.pallas_call(
        body, name="gdn_scan_bwd", grid=(n_steps,),
        in_specs=intra_specs + [unit(GDN_HD, GDN_HD), tok(3), tok(0), _const_spec((1, GDN_HD)), _ANY],
        out_specs=intra_specs + [tok(0), _const_spec((1, GDN_HD))],
        out_shape=_gdn_intra_shapes(S) + [SDS((S, GDN_W), bf16), SDS((1, GDN_HD), f32)],
        scratch_shapes=[pltpu.VMEM((GDN_HEADS, GDN_HD, GDN_HD), f32)],
        compiler_params=_cp(1, VMEM_LIMIT),
    )(*intra, states, proj, d_oab, onorm_g, after)


SWA_UNROLL = 4


def _swa_tiles(q_ref, k_ref, v_ref, it, d, nb_log2, S):
    nb = 1 << nb_log2
    r = lax.shift_right_logical(it, nb_log2)
    blk = lax.bitwise_and(it, nb - 1)
    qs = blk * (SWA_BLK * d) + r
    ps = jnp.maximum(blk - 1, 0) * (SWA_BLK * d) + r
    if d > 1:
        rows_c, rows_p = pl.ds(qs, SWA_BLK, stride=d), pl.ds(ps, SWA_BLK, stride=d)
    else:
        rows_c, rows_p = pl.ds(pl.multiple_of(qs, SWA_BLK), SWA_BLK), pl.ds(pl.multiple_of(ps, SWA_BLK), SWA_BLK)
    return rows_c, rows_p, blk > 0


def _swa_fwd(proj, bt, S):
    scale = SWA_HD ** -0.5

    def body(q_ref, k_ref, v_ref, bt_ref, o_ref, lse0_ref, lse1_ref, m0_scr, m1_scr, a0_scr, a1_scr):
        lane = lax.broadcasted_iota(jnp.int32, (SWA_BLK, LANE), 1)
        h0 = lane < SWA_HD
        qi = lax.broadcasted_iota(jnp.int32, (SWA_BLK, SWA_BLK), 0)
        kj = lax.broadcasted_iota(jnp.int32, (SWA_BLK, SWA_BLK), 1)
        lower = kj <= qi
        ones16 = jnp.ones((LANE, SWA_BLK), bf16)
        m_scrs = (m0_scr, m1_scr)
        a_scrs = (a0_scr, a1_scr)
        for r in m_scrs:
            r[...] = jnp.full((S, LANE), NEG, f32)
        for r in a_scrs:
            r[...] = jnp.zeros((S, LANE), f32)
        for p, (_, d) in enumerate(PATTERNS):
            nb_log2 = int(math.log2(S // d // SWA_BLK))

            def lp(i, carry, p=p, d=d, nb_log2=nb_log2):
                heads = [h0, jnp.logical_not(h0)]
                tiles = []
                for u in range(SWA_UNROLL):
                    rows_c, rows_p, has_prev = _swa_tiles(q_ref, k_ref, v_ref, i * SWA_UNROLL + u, d, nb_log2, S)
                    q = q_ref[rows_c, :]
                    kc = k_ref[rows_c, :].astype(bf16)
                    kp_f = k_ref[rows_p, :]
                    kp = kp_f.astype(bf16)
                    logits = []
                    for mh in heads:
                        q_h = jnp.where(mh, q, 0.0)
                        qh = q_h.astype(bf16)
                        logits.append((_dot_nt(qh, kc), _dot_nt(qh, kp), _dot((q_h * kp_f).astype(bf16), ones16)))
                    tiles.append((rows_c, rows_p, has_prev, logits))
                probs = []
                for rows_c, rows_p, has_prev, logits in tiles:
                    keep = jnp.logical_or(lower, has_prev)
                    per_head = []
                    for h, (s_c, s_p, far) in enumerate(logits):
                        s = jnp.where(keep, jnp.where(lower, s_c, s_p) * scale + bt_ref[p, h], NEG)
                        s_far = jnp.where(has_prev, far * scale + bt_ref[p, h, SWA_BLK - 1:SWA_BLK, 0:1], NEG)
                        mo = m_scrs[h][rows_c, :]
                        mn = jnp.maximum(jnp.maximum(mo, s_far), jnp.max(s, axis=1, keepdims=True))
                        pm = jnp.exp(s - mn)
                        per_head.append((mn, jnp.exp(mo - mn), jnp.exp(s_far - mn), jnp.where(lower, pm, 0.0).astype(bf16),
                                         jnp.where(lower, 0.0, pm).astype(bf16)))
                    probs.append(per_head)
                acc_old = [(a0_scr[t[0], :], a1_scr[t[0], :]) for t in tiles]
                done = []
                for (rows_c, rows_p, _, _), per_head, old in zip(tiles, probs, acc_old):
                    vc = v_ref[rows_c, :]
                    vp = v_ref[rows_p, :]
                    acc_new = []
                    for h, (mn, alpha, p_far, pc16, pp16) in enumerate(per_head):
                        vpa = jnp.where(heads[h], vp, 1.0)
                        pv = _dot(pc16, jnp.where(heads[h], vc, 1.0).astype(bf16)) + _dot(pp16, vpa.astype(bf16)) + p_far * vpa
                        acc_new.append(alpha * old[h] + pv)
                    done.append((rows_c, per_head[0][0], per_head[1][0], acc_new[0], acc_new[1]))
                for rows_c, m0_new, m1_new, a0_new, a1_new in done:
                    m0_scr[rows_c, :] = m0_new
                    m1_scr[rows_c, :] = m1_new
                    a0_scr[rows_c, :] = a0_new
                    a1_scr[rows_c, :] = a1_new
                return carry

            lax.fori_loop(0, S // SWA_BLK // SWA_UNROLL, lp, 0)

        def fin(c, carry):
            rows = pl.ds(pl.multiple_of(c * SWA_BLK, SWA_BLK), SWA_BLK)
            a0 = a0_scr[rows, :]
            a1 = a1_scr[rows, :]
            l0 = jnp.where(h0, pltpu.roll(a0, SWA_HD, 1), a0)
            l1 = jnp.where(h0, a1, pltpu.roll(a1, SWA_HD, 1))
            o_ref[rows, :] = jnp.where(h0, a0 / l0, a1 / l1)
            lse0_ref[rows, :] = m0_scr[rows, :] + jnp.log(l0)
            lse1_ref[rows, :] = m1_scr[rows, :] + jnp.log(l1)
            return carry

        lax.fori_loop(0, S // SWA_BLK, fin, 0)

    qb = COL_B // LANE
    col = lambda c: pl.BlockSpec((S, LANE), lambda hp, c=c: (0, c + hp))
    return pl.pallas_call(
        body, name="swa_fwd", grid=(4,),
        in_specs=[col(qb), col(qb + 4), col(qb + 8), pl.BlockSpec((3, 2, SWA_BLK, SWA_BLK), lambda hp: (0, hp, 0, 0))],
        out_specs=[col(0), col(0), col(0)],
        out_shape=[SDS((S, SWA_W), f32)] * 3,
        scratch_shapes=[pltpu.VMEM((S, LANE), f32)] * 4,
        compiler_params=_cp(1, VMEM_LIMIT),
    )(proj, proj, proj, bt)


def _swa_bwd(proj, bt, ob, lse0, lse1, d_oab, after, S):
    scale = SWA_HD ** -0.5

    def body(q_ref, k_ref, v_ref, bt_ref, o_ref, lse0_ref, lse1_ref, do_ref, after_ref, dq_ref, dk_ref, dv_ref, dsb_ref,
             dq_scr, dk_scr, dv_scr):
        lane = lax.broadcasted_iota(jnp.int32, (SWA_BLK, LANE), 1)
        h0 = lane < SWA_HD
        qi = lax.broadcasted_iota(jnp.int32, (SWA_BLK, SWA_BLK), 0)
        kj = lax.broadcasted_iota(jnp.int32, (SWA_BLK, SWA_BLK), 1)
        lower = kj <= qi
        eye = kj == qi
        rel127 = jnp.logical_or(kj == qi + 1, jnp.logical_and(qi == SWA_BLK - 1, kj == 0))
        ones16 = jnp.ones((LANE, SWA_BLK), bf16)
        lse_refs = (lse0_ref, lse1_ref)
        dq_scr[...] = jnp.zeros((S, LANE), f32)
        dk_scr[...] = jnp.zeros((S, LANE), f32)
        dv_scr[...] = jnp.zeros((S, LANE), f32)
        dsb_ref[...] = jnp.zeros_like(dsb_ref)

        def row_sum(a):
            hi = a.astype(bf16)
            return _dot(hi, ones16) + _dot((a - hi.astype(f32)).astype(bf16), ones16)

        for p, (_, d) in enumerate(PATTERNS):
            nb_log2 = int(math.log2(S // d // SWA_BLK))

            def lp(i, carry, p=p, d=d, nb_log2=nb_log2):
                heads = [h0, jnp.logical_not(h0)]
                tiles = []
                for u in range(SWA_UNROLL):
                    rows_c, rows_p, has_prev = _swa_tiles(q_ref, k_ref, v_ref, i * SWA_UNROLL + u, d, nb_log2, S)
                    q = q_ref[rows_c, :]
                    kc = k_ref[rows_c, :].astype(bf16)
                    kp_f = k_ref[rows_p, :]
                    kp = kp_f.astype(bf16)
                    vc = v_ref[rows_c, :].astype(bf16)
                    vp_f = v_ref[rows_p, :]
                    vp = vp_f.astype(bf16)
                    do = do_ref[rows_c, :]
                    dlt = do * o_ref[rows_c, :]
                    per_head = []
                    for mh in heads:
                        q_h = jnp.where(mh, q, 0.0)
                        do_h = jnp.where(mh, do, 0.0)
                        qh = q_h.astype(bf16)
                        doh = do_h.astype(bf16)
                        per_head.append((qh, doh, _dot_nt(qh, kc), _dot_nt(qh, kp), _dot((q_h * kp_f).astype(bf16), ones16),
                                         _dot_nt(doh, vc), _dot_nt(doh, vp), _dot((do_h * vp_f).astype(bf16), ones16),
                                         row_sum(jnp.where(mh, dlt, 0.0))))
                    tiles.append((rows_c, rows_p, has_prev, kc, kp, per_head))
                grads = []
                for rows_c, rows_p, has_prev, kc, kp, per_head in tiles:
                    keep = jnp.logical_or(lower, has_prev)
                    out = []
                    for h, (qh, doh, s_c, s_p, far, dp_c, dp_p, dp_far, delta) in enumerate(per_head):
                        lse_h = lse_refs[h][rows_c, :]
                        s = jnp.where(keep, jnp.where(lower, s_c, s_p) * scale + bt_ref[p, h], NEG)
                        s_far = jnp.where(has_prev, far * scale + bt_ref[p, h, SWA_BLK - 1:SWA_BLK, 0:1], NEG)
                        pm = jnp.exp(s - lse_h)
                        p_far = jnp.exp(s_far - lse_h)
                        dsm = pm * (jnp.where(lower, dp_c, dp_p) - delta)
                        ds_far = p_far * (dp_far - delta)
                        out.append((dsm + jnp.where(rel127, ds_far, 0.0),
                                    jnp.where(lower, dsm, 0.0).astype(bf16),
                                    jnp.where(lower, jnp.where(eye, ds_far, 0.0), dsm).astype(bf16),
                                    jnp.where(lower, pm, 0.0).astype(bf16),
                                    jnp.where(lower, jnp.where(eye, p_far, 0.0), pm).astype(bf16)))
                    grads.append(out)
                done = []
                for (rows_c, rows_p, _, kc, kp, per_head), out in zip(tiles, grads):
                    dq_t = dkc_t = dkp_t = dvc_t = dvp_t = None
                    add = lambda acc, t: t if acc is None else acc + t
                    for h, (_, dsc16, dsp16, pc16, pp16) in enumerate(out):
                        qh, doh = per_head[h][0], per_head[h][1]
                        dq_t = add(dq_t, jnp.where(heads[h], (_dot(dsc16, kc) + _dot(dsp16, kp)) * scale, 0.0))
                        dkc_t = add(dkc_t, _dot_tn(dsc16, qh) * scale)
                        dkp_t = add(dkp_t, _dot_tn(dsp16, qh) * scale)
                        dvc_t = add(dvc_t, _dot_tn(pc16, doh))
                        dvp_t = add(dvp_t, _dot_tn(pp16, doh))
                    done.append((rows_c, rows_p, dq_t, dkc_t, dkp_t, dvc_t, dvp_t))
                for h in range(2):
                    tot = grads[0][h][0]
                    for g in grads[1:]:
                        tot = tot + g[h][0]
                    dsb_ref[0, p, h] += tot
                for rows_c, rows_p, dq_t, dkc_t, dkp_t, dvc_t, dvp_t in done:
                    dq_scr[rows_c, :] = dq_scr[rows_c, :] + dq_t
                    dk_scr[rows_c, :] = dk_scr[rows_c, :] + dkc_t
                    dv_scr[rows_c, :] = dv_scr[rows_c, :] + dvc_t
                    dk_scr[rows_p, :] = dk_scr[rows_p, :] + dkp_t
                    dv_scr[rows_p, :] = dv_scr[rows_p, :] + dvp_t
                return carry

            lax.fori_loop(0, S // SWA_BLK // SWA_UNROLL, lp, 0)
        dq_ref[...] = dq_scr[...].astype(bf16)
        dk_ref[...] = dk_scr[...].astype(bf16)
        dv_ref[...] = dv_scr[...].astype(bf16)

    qb = COL_B // LANE
    col = lambda c: pl.BlockSpec((S, LANE), lambda hp, c=c: (0, c + hp))
    return pl.pallas_call(
        body, name="swa_bwd", grid=(4,),
        in_specs=[col(qb), col(qb + 4), col(qb + 8),
                  pl.BlockSpec((3, 2, SWA_BLK, SWA_BLK), lambda hp: (0, hp, 0, 0)),
                  col(0), col(0), col(0), col(4), _ANY],
        out_specs=[col(0), col(0), col(0),
                   pl.BlockSpec((1, 3, 2, SWA_BLK, SWA_BLK), lambda hp: (hp, 0, 0, 0, 0))],
        out_shape=[SDS((S, SWA_W), bf16)] * 3 + [SDS((4, 3, 2, SWA_BLK, SWA_BLK), f32)],
        scratch_shapes=[pltpu.VMEM((S, LANE), f32)] * 3,
        compiler_params=_cp(1, VMEM_LIMIT),
    )(proj, proj, proj, bt, ob, lse0, lse1, d_oab, after)


def _mix_fwd(oa, ob, w_out, x, g_post, S):
    TS = 512

    def body(oa_ref, ob_ref, w_ref, x_ref, g_ref, mix_ref, x1_ref):
        mix = _dot(oa_ref[...].astype(bf16), w_ref[0:GDN_W, :]) + _dot(ob_ref[...].astype(bf16), w_ref[GDN_W:D_MODEL, :])
        r = lax.rsqrt(jnp.mean(mix * mix, axis=-1, keepdims=True) + RMS_EPS)
        mix_ref[...] = mix
        x1_ref[...] = x_ref[...] + mix * r * g_ref[...]

    row = lambda w: pl.BlockSpec((TS, w), lambda i: (i, 0))
    return pl.pallas_call(
        body, name="mix_fwd", grid=(S // TS,),
        in_specs=[row(GDN_W), row(SWA_W), _resident_spec((D_MODEL, D_MODEL)), row(D_MODEL), _const_spec((1, D_MODEL))],
        out_specs=[row(D_MODEL), row(D_MODEL)],
        out_shape=[SDS((S, D_MODEL), f32), SDS((S, D_MODEL), f32)],
        compiler_params=_cp(1, VMEM_LIMIT),
    )(oa, ob, w_out, x, g_post)


def _mix_bwd(dx1, mix, g_post, w_out, S):
    TS = 512

    def body(dx1_ref, mix_ref, g_ref, w_ref, dmix_ref, doab_ref, dg_ref):
        @pl.when(pl.program_id(0) == 0)
        def _():
            dg_ref[...] = jnp.zeros_like(dg_ref)

        mix = mix_ref[...]
        dz = dx1_ref[...]
        r = lax.rsqrt(jnp.mean(mix * mix, axis=-1, keepdims=True) + RMS_EPS)
        n = mix * r
        dg_ref[...] += jnp.sum(dz * n, axis=0, keepdims=True)
        dn = dz * g_ref[...]
        dmix = (r * (dn - n * jnp.mean(dn * n, axis=-1, keepdims=True))).astype(bf16)
        dmix_ref[...] = dmix
        doab_ref[...] = _dot_nt(dmix, w_ref[...])

    row = lambda: pl.BlockSpec((TS, D_MODEL), lambda i: (i, 0))
    return pl.pallas_call(
        body, name="mix_bwd", grid=(S // TS,),
        in_specs=[row(), row(), _const_spec((1, D_MODEL)), _resident_spec((D_MODEL, D_MODEL))],
        out_specs=[row(), row(), _const_spec((1, D_MODEL))],
        out_shape=[SDS((S, D_MODEL), bf16), SDS((S, D_MODEL), f32), SDS((1, D_MODEL), f32)],
        compiler_params=_cp(1, VMEM_LIMIT),
    )(dx1, mix, g_post, w_out)


FFN_TS = 256
FFN_CH = 1408


def _ffn(x1, tgt, g_pre, g_post, wg, wu, wd, S):
    def body(x1_ref, t_ref, gp_ref, gq_ref, wg_ref, wu_ref, wd_ref,
             dx1_ref, h2_ref, act_ref, dgate_ref, dup_ref, df_ref, loss_ref, dgp_ref, dgq_ref, gate_scr, up_scr):
        @pl.when(pl.program_id(0) == 0)
        def _():
            loss_ref[...] = jnp.zeros_like(loss_ref)
            dgp_ref[...] = jnp.zeros_like(dgp_ref)
            dgq_ref[...] = jnp.zeros_like(dgq_ref)

        x1v = x1_ref[...]
        gp = gp_ref[...]
        gq = gq_ref[...]
        r2 = lax.rsqrt(jnp.mean(x1v * x1v, axis=-1, keepdims=True) + RMS_EPS)
        n2 = x1v * r2
        h2 = (n2 * gp).astype(bf16)
        h2_ref[...] = h2
        f = jnp.zeros((FFN_TS, D_MODEL), f32)
        for c in range(D_FF // FFN_CH):
            cs = slice(c * FFN_CH, (c + 1) * FFN_CH)
            gate = _dot(h2, wg_ref[:, cs])
            up = _dot(h2, wu_ref[:, cs])
            gate_scr[:, cs] = gate
            up_scr[:, cs] = up
            act = (gate * _sigmoid(gate) * up).astype(bf16)
            act_ref[:, cs] = act
            f = f + _dot(act, wd_ref[cs, :])
        r3 = lax.rsqrt(jnp.mean(f * f, axis=-1, keepdims=True) + RMS_EPS)
        n3 = f * r3
        err = x1v + n3 * gq - t_ref[...]
        loss_ref[...] += 0.5 * jnp.sum(jnp.mean(err * err, axis=-1, keepdims=True), axis=0, keepdims=True)
        dy = err * (1.0 / D_MODEL)
        dgq_ref[...] += jnp.sum(dy * n3, axis=0, keepdims=True)
        dn3 = dy * gq
        df = (r3 * (dn3 - n3 * jnp.mean(dn3 * n3, axis=-1, keepdims=True))).astype(bf16)
        df_ref[...] = df
        dh2 = jnp.zeros((FFN_TS, D_MODEL), f32)
        for c in range(D_FF // FFN_CH):
            cs = slice(c * FFN_CH, (c + 1) * FFN_CH)
            gate = gate_scr[:, cs]
            up = up_scr[:, cs]
            dact = _dot_nt(df, wd_ref[cs, :])
            sg = _sigmoid(gate)
            dup = (dact * gate * sg).astype(bf16)
            dgate = (dact * up * (sg * (1.0 + gate * (1.0 - sg)))).astype(bf16)
            dup_ref[:, cs] = dup
            dgate_ref[:, cs] = dgate
            dh2 = dh2 + _dot_nt(dgate, wg_ref[:, cs]) + _dot_nt(dup, wu_ref[:, cs])
        dgp_ref[...] += jnp.sum(dh2 * n2, axis=0, keepdims=True)
        dn2 = dh2 * gp
        dx1_ref[...] = dy + r2 * (dn2 - n2 * jnp.mean(dn2 * n2, axis=-1, keepdims=True))

    row = lambda w: pl.BlockSpec((FFN_TS, w), lambda i: (i, 0))
    vec = _const_spec((1, D_MODEL))
    return pl.pallas_call(
        body, name="ffn_fwd_bwd", grid=(S // FFN_TS,),
        in_specs=[row(D_MODEL), row(D_MODEL), vec, vec, _resident_spec((D_MODEL, D_FF)), _resident_spec((D_MODEL, D_FF)),
                  _resident_spec((D_FF, D_MODEL))],
        out_specs=[row(D_MODEL), row(D_MODEL), row(D_FF), row(D_FF), row(D_FF), row(D_MODEL), _const_spec((1, LANE)), vec, vec],
        out_shape=[SDS((S, D_MODEL), f32), SDS((S, D_MODEL), bf16), SDS((S, D_FF), bf16), SDS((S, D_FF), bf16),
                   SDS((S, D_FF), bf16), SDS((S, D_MODEL), bf16), SDS((1, LANE), f32), SDS((1, D_MODEL), f32),
                   SDS((1, D_MODEL), f32)],
        scratch_shapes=[pltpu.VMEM((FFN_TS, D_FF), f32), pltpu.VMEM((FFN_TS, D_FF), f32)],
        compiler_params=_cp(1, VMEM_LIMIT),
    )(x1, tgt, g_pre, g_post, wg, wu, wd)


def _proj_bwd(x, dx1, g_pre, wcat, segs, S):
    TS = 256
    n = len(segs)
    cols = [(c0, a.shape[1]) for a, c0 in segs]

    def body(*refs):
        x_ref, dx1_ref, g_ref, w_ref = refs[:4]
        seg_refs = refs[4:4 + n]
        gx_ref, h1_ref, dg_ref = refs[4 + n:]

        @pl.when(pl.program_id(0) == 0)
        def _():
            dg_ref[...] = jnp.zeros_like(dg_ref)

        dh = jnp.zeros((TS, D_MODEL), f32)
        for s_ref, (c0, w) in zip(seg_refs, cols):
            dh = dh + _dot_nt(s_ref[...], w_ref[:, c0:c0 + w])
        xv = x_ref[...]
        g = g_ref[...]
        r = lax.rsqrt(jnp.mean(xv * xv, axis=-1, keepdims=True) + RMS_EPS)
        nx = xv * r
        h1_ref[...] = (nx * g).astype(bf16)
        dg_ref[...] += jnp.sum(dh * nx, axis=0, keepdims=True)
        dn = dh * g
        gx_ref[...] = dx1_ref[...] + r * (dn - nx * jnp.mean(dn * nx, axis=-1, keepdims=True))

    row = lambda w: pl.BlockSpec((TS, w), lambda i: (i, 0))
    return pl.pallas_call(
        body, name="proj_bwd", grid=(S // TS,),
        in_specs=[row(D_MODEL), row(D_MODEL), _const_spec((1, D_MODEL)), _resident_spec((D_MODEL, NCOL))]
                 + [row(w) for _, w in cols],
        out_specs=[row(D_MODEL), row(D_MODEL), _const_spec((1, D_MODEL))],
        out_shape=[SDS((S, D_MODEL), f32), SDS((S, D_MODEL), bf16), SDS((1, D_MODEL), f32)],
        compiler_params=_cp(1, VMEM_LIMIT),
    )(x, dx1, g_pre, wcat, *[a for a, _ in segs])


def _wgrad(a, b, S, name, col_blocks=False):
    TS = 512
    K = a.shape[1]
    N = b.shape[1]
    TN = next(t for t in (512, 1408, N) if N % t == 0)
    cb = N // N_DEV
    nblk = TN // cb if col_blocks else 0

    def body(a_ref, b_ref, o_ref, acc):
        @pl.when(pl.program_id(1) == 0)
        def _():
            acc[...] = jnp.zeros_like(acc)

        acc[...] += _dot_tn(a_ref[...].astype(bf16), b_ref[...])

        @pl.when(pl.program_id(1) == pl.num_programs(1) - 1)
        def _():
            if col_blocks:
                for i in range(nblk):
                    o_ref[i] = acc[:, i * cb:(i + 1) * cb].astype(bf16)
            else:
                o_ref[...] = acc[...].astype(bf16)

    if col_blocks:
        out_spec = pl.BlockSpec((nblk, K, cb), lambda j, s: (j, 0, 0))
        out_shape = SDS((N_DEV, K, cb), bf16)
    else:
        out_spec = pl.BlockSpec((K, TN), lambda j, s: (0, j))
        out_shape = SDS((K, N), bf16)
    return pl.pallas_call(
        body, name=name, grid=(N // TN, S // TS),
        in_specs=[pl.BlockSpec((TS, K), lambda j, s: (s, 0)), pl.BlockSpec((TS, TN), lambda j, s: (s, j))],
        out_specs=out_spec, out_shape=out_shape,
        scratch_shapes=[pltpu.VMEM((K, TN), f32)],
        compiler_params=_cp(2, VMEM_LIMIT),
    )(a, b)


def _w_in_pieces():
    n_a, n_g = 4 * GDN_W, 2 * GDN_HEADS
    cb = IN_COLS // N_DEV
    bounds = [(0, n_a, COL_A), (n_a, n_a + n_g, COL_G), (n_a + n_g, IN_COLS, COL_B)]
    out = []
    for j in range(N_DEV):
        lo, hi = j * cb, (j + 1) * cb
        for s0, s1, dst in bounds:
            a, b = max(lo, s0), min(hi, s1)
            if a < b:
                out.append((j, a - lo, b - a, dst + a - s0))
    return out


def _wcat_from_blocks(g_in):
    TR = 256
    cb = IN_COLS // N_DEV
    pieces = _w_in_pieces()

    def body(w_ref, o_ref):
        o_ref[:, COL_G:NCOL] = jnp.zeros((TR, NCOL - COL_G), bf16)
        for j, off, w, dst in pieces:
            o_ref[:, dst:dst + w] = w_ref[j, :, off:off + w]

    return pl.pallas_call(
        body, name="wcat_from_blocks", grid=(D_MODEL // TR,),
        in_specs=[pl.BlockSpec((N_DEV, TR, cb), lambda i: (0, i, 0))],
        out_specs=pl.BlockSpec((TR, NCOL), lambda i: (i, 0)),
        out_shape=SDS((D_MODEL, NCOL), bf16),
        compiler_params=_cp(1, VMEM_LIMIT),
    )(g_in)


def _full_from_col_blocks(g):
    n, R, C = g.shape
    TR = 256

    def body(w_ref, o_ref):
        for j in range(n):
            o_ref[:, j * C:(j + 1) * C] = w_ref[j]

    return pl.pallas_call(
        body, name="full_from_col_blocks", grid=(R // TR,),
        in_specs=[pl.BlockSpec((n, TR, C), lambda i: (0, i, 0))],
        out_specs=pl.BlockSpec((TR, n * C), lambda i: (i, 0)),
        out_shape=SDS((R, n * C), g.dtype),
        compiler_params=_cp(1, VMEM_LIMIT),
    )(g)


def _wgrad_in(h1, segs, S):
    TS = 512
    n = len(segs)
    cols = [(c0, a.shape[1]) for a, c0 in segs]
    cb = IN_COLS // N_DEV
    pieces = _w_in_pieces()

    def body(*refs):
        h_ref = refs[0]
        seg_refs = refs[1:1 + n]
        o_ref, acc = refs[1 + n], refs[2 + n]

        @pl.when(pl.program_id(0) == 0)
        def _():
            acc[...] = jnp.zeros_like(acc)

        h = h_ref[...]
        for s_ref, (c0, w) in zip(seg_refs, cols):
            acc[:, c0:c0 + w] += _dot_tn(h, s_ref[...])

        @pl.when(pl.program_id(0) == pl.num_programs(0) - 1)
        def _():
            for j, off, w, src in pieces:
                o_ref[j, :, off:off + w] = acc[:, src:src + w].astype(bf16)

    row = lambda w: pl.BlockSpec((TS, w), lambda i: (i, 0))
    return pl.pallas_call(
        body, name="wgrad_in", grid=(S // TS,),
        in_specs=[row(D_MODEL)] + [row(w) for _, w in cols],
        out_specs=_const_spec((N_DEV, D_MODEL, cb)),
        out_shape=SDS((N_DEV, D_MODEL, cb), bf16),
        scratch_shapes=[pltpu.VMEM((D_MODEL, NCOL), f32)],
        compiler_params=_cp(1, VMEM_LIMIT),
    )(h1, *[a for a, _ in segs])


def _adamw(recv, w, m, v, name):
    R, C = w.shape
    TR = 256 if R % 256 == 0 else R
    c1 = 1.0 / (1.0 - ADAM_B1 ** ADAM_STEP)
    c2 = 1.0 / (1.0 - ADAM_B2 ** ADAM_STEP)

    def body(r_ref, w_ref, m_ref, v_ref, g_out, d_out, m_out, v_out):
        g = r_ref[0].astype(f32)
        for s in range(1, N_DEV):
            g = g + r_ref[s].astype(f32)
        mn = ADAM_B1 * m_ref[...] + (1.0 - ADAM_B1) * g
        vn = ADAM_B2 * v_ref[...] + (1.0 - ADAM_B2) * (g * g)
        g_out[...] = g
        m_out[...] = mn
        v_out[...] = vn
        d_out[...] = -ADAM_LR * ((mn * c1) / (jnp.sqrt(vn * c2) + ADAM_EPS) + ADAM_WD * w_ref[...])

    blk = pl.BlockSpec((TR, C), lambda i: (i, 0))
    return pl.pallas_call(
        body, name=name, grid=(R // TR,),
        in_specs=[pl.BlockSpec((N_DEV, TR, C), lambda i: (0, i, 0)), blk, blk, blk],
        out_specs=[blk, blk, blk, blk],
        out_shape=[SDS((R, C), f32)] * 4,
        compiler_params=_cp(1, VMEM_LIMIT),
    )(recv, w, m, v)


MESH = pl.DeviceIdType.MESH
_ANY = pl.BlockSpec(memory_space=pl.ANY)


def _flip(v, d):
    return 1 - v if d else v


def _all_gather(shards):
    n = len(shards)

    def body(*refs):
        ins = refs[:n]
        outs = refs[n:2 * n]
        send_sems, recv_sems, local_sems = refs[2 * n:]
        x, y, c = lax.axis_index("x"), lax.axis_index("y"), lax.axis_index("c")
        me, sibling = (x, y, c), (x, y, 1 - c)
        chips = [(1 - x, y), (x, 1 - y), (1 - x, 1 - y)]

        def slot(px, py, pc):
            return 4 * px + 2 * py + pc

        def copy(a, k, block, to, src=None):
            dst = outs[a].at[slot(*block)]
            return pltpu.make_async_remote_copy(src_ref=dst if src is None else src, dst_ref=dst,
                                                send_sem=send_sems.at[a, k], recv_sem=recv_sems.at[a, k],
                                                device_id=to, device_id_type=MESH)

        mine, first, passed = [], [], []
        for a in range(n):
            cp = pltpu.make_async_copy(ins[a], outs[a].at[slot(*me)], local_sems.at[a])
            cp.start()
            mine.append(cp)
            fs = [copy(a, 0, me, sibling, src=ins[a])]
            fs += [copy(a, 1 + j, me, (*chip, c), src=ins[a]) for j, chip in enumerate(chips)]
            for cp in fs:
                cp.start()
            first += fs
        for j, chip in enumerate(chips):
            for a in range(n):
                copy(a, 1 + j, (*chip, c), me).wait_recv()
                cp = copy(a, 4 + j, (*chip, c), sibling)
                cp.start()
                passed.append(cp)
        for a in range(n):
            copy(a, 0, sibling, me).wait_recv()
            for j, chip in enumerate(chips):
                copy(a, 4 + j, (*chip, 1 - c), me).wait_recv()
        for cp in first + passed:
            cp.wait_send()
        for cp in mine:
            cp.wait()

    return pl.pallas_call(
        body, name="weight_all_gather",
        in_specs=[_ANY] * n, out_specs=[_ANY] * n,
        out_shape=[SDS((N_DEV,) + s.shape, s.dtype) for s in shards],
        scratch_shapes=[pltpu.SemaphoreType.DMA((n, 7)), pltpu.SemaphoreType.DMA((n, 7)), pltpu.SemaphoreType.DMA((n,))],
        compiler_params=pltpu.CompilerParams(has_side_effects=True),
    )(*shards)


def _grad_exchange(blocked, whole):
    arrs = list(blocked) + list(whole)
    n, nb = len(arrs), len(blocked)
    rel = [(dx, dy, dc) for dx in (0, 1) for dy in (0, 1) for dc in (0, 1) if dx or dy or dc]

    def body(*refs):
        ins = refs[:n]
        outs = refs[n:2 * n]
        send_sems, recv_sems, local_sems = refs[2 * n:]
        x, y, c = lax.axis_index("x"), lax.axis_index("y"), lax.axis_index("c")
        me = 4 * x + 2 * y + c
        sends, locs = [], []
        for a in range(n):
            cp = pltpu.make_async_copy(ins[a].at[me] if a < nb else ins[a], outs[a].at[me], local_sems.at[a])
            cp.start()
            locs.append(cp)
            for k, (dx, dy, dc) in enumerate(rel):
                peer = (_flip(x, dx), _flip(y, dy), _flip(c, dc))
                pidx = 4 * peer[0] + 2 * peer[1] + peer[2]
                cp = pltpu.make_async_remote_copy(src_ref=ins[a].at[pidx] if a < nb else ins[a], dst_ref=outs[a].at[me],
                                                  send_sem=send_sems.at[a, k], recv_sem=recv_sems.at[a, k],
                                                  device_id=peer, device_id_type=MESH)
                cp.start()
                sends.append(cp)
        for a in range(n):
            for k, (dx, dy, dc) in enumerate(rel):
                peer = (_flip(x, dx), _flip(y, dy), _flip(c, dc))
                pidx = 4 * peer[0] + 2 * peer[1] + peer[2]
                pltpu.make_async_remote_copy(src_ref=outs[a].at[pidx], dst_ref=outs[a].at[pidx],
                                             send_sem=send_sems.at[a, k], recv_sem=recv_sems.at[a, k],
                                             device_id=peer, device_id_type=MESH).wait_recv()
        for cp in sends:
            cp.wait_send()
        for cp in locs:
            cp.wait()

    shapes = [SDS(a.shape, a.dtype) for a in blocked] + [SDS((N_DEV,) + a.shape, a.dtype) for a in whole]
    return pl.pallas_call(
        body, name="grad_exchange",
        in_specs=[_ANY] * n, out_specs=[_ANY] * n, out_shape=shapes,
        scratch_shapes=[pltpu.SemaphoreType.DMA((n, 7)), pltpu.SemaphoreType.DMA((n, 7)), pltpu.SemaphoreType.DMA((n,))],
        compiler_params=pltpu.CompilerParams(has_side_effects=True),
    )(*arrs)


_HBM = pl.BlockSpec(memory_space=pltpu.HBM)
_SEM = pl.BlockSpec(memory_space=pltpu.SEMAPHORE)
_REL = [(dx, dy, dc) for dx in (0, 1) for dy in (0, 1) for dc in (0, 1) if dx or dy or dc]


N_PEER = len(_REL)
_EFFECT = pltpu.SideEffectType.DATAFLOW_SIDE_EFFECTING


def _peer_copies(srcs, lands, send_sems, recv_sems, blocked, as_receiver):
    x, y, c = lax.axis_index("x"), lax.axis_index("y"), lax.axis_index("c")
    me = 4 * x + 2 * y + c
    cps = []
    for a in range(len(srcs)):
        for k, (dx, dy, dc) in enumerate(_REL):
            peer = (_flip(x, dx), _flip(y, dy), _flip(c, dc))
            pidx = 4 * peer[0] + 2 * peer[1] + peer[2]
            cps.append(pltpu.make_async_remote_copy(
                src_ref=srcs[a].at[pidx] if blocked else srcs[a], dst_ref=lands[a].at[pidx if as_receiver else me],
                send_sem=send_sems[a * N_PEER + k], recv_sem=recv_sems[a * N_PEER + k],
                device_id=peer, device_id_type=MESH))
    return cps


def _exchange_start(srcs, after, blocked, name):
    n = len(srcs)
    ns = n * N_PEER
    lands = [lax.empty(s.shape if blocked else (N_DEV,) + s.shape, s.dtype) for s in srcs]

    def body(*refs):
        ins, lnd = refs[:n], refs[n:2 * n]
        outs = refs[2 * n + 1:]
        for cp in _peer_copies(ins, lnd, outs[:ns], outs[ns:2 * ns], blocked, False):
            cp.start()
        outs[-1][...] = jnp.zeros_like(outs[-1])

    res = pl.pallas_call(
        body, name=name,
        in_specs=[_HBM] * (2 * n) + [_ANY],
        out_specs=[_SEM] * (2 * ns) + [_HBM] * (2 * n) + [pl.BlockSpec(memory_space=pltpu.VMEM)],
        out_shape=[pltpu.SemaphoreType.DMA(())] * (2 * ns) + [pltpu.HBM(s.shape, s.dtype) for s in srcs]
                  + [pltpu.HBM(l.shape, l.dtype) for l in lands] + [SDS((8, LANE), f32)],
        input_output_aliases={i: 2 * ns + i for i in range(2 * n)},
        compiler_params=pltpu.CompilerParams(has_side_effects=_EFFECT),
    )(*[pltpu.with_memory_space_constraint(s, pltpu.HBM) for s in srcs],
      *[pltpu.with_memory_space_constraint(l, pltpu.HBM) for l in lands], after)
    return list(res[:2 * ns]), list(res[2 * ns:2 * ns + n]), list(res[2 * ns + n:2 * ns + 2 * n]), res[-1]


def _exchange_wait(sems, srcs, lands, after, blocked, name):
    n = len(srcs)
    ns = n * N_PEER

    def body(*refs):
        ins, lnd = refs[:n], refs[n:2 * n]
        sem_refs = refs[2 * n:2 * n + 2 * ns]
        for cp in _peer_copies(ins, lnd, sem_refs[:ns], sem_refs[ns:], blocked, True):
            cp.wait_send()
            cp.wait_recv()

    res = pl.pallas_call(
        body, name=name,
        in_specs=[_HBM] * (2 * n) + [_SEM] * (2 * ns) + [_ANY],
        out_specs=[_HBM] * (2 * n),
        out_shape=[pltpu.HBM(s.shape, s.dtype) for s in srcs] + [pltpu.HBM(l.shape, l.dtype) for l in lands],
        input_output_aliases={i: i for i in range(2 * n)},
        compiler_params=pltpu.CompilerParams(has_side_effects=_EFFECT),
    )(*srcs, *lands, *sems, after)
    return list(res[n:])


def _local_step(x, tgt, wcat, convw, late_weights, early_grads, token, a_log, dt_bias, onorm_g, rel_bias,
                g_mix_pre, g_mix_post, g_ffn_pre, g_ffn_post):
    S = x.shape[0]
    bk_np = _bucket_tables()
    bk = jnp.asarray(bk_np)
    bt = _bias_tables(rel_bias, bk)
    proj = _proj_fwd(x, g_mix_pre, wcat, token, S)
    nu = S // CHUNK * GDN_HEADS
    qkv_u = _gdn_prep(proj, convw, S).reshape(3, nu, CHUNK, GDN_HD)
    intra, t_inv = _gdn_intra_fwd(qkv_u, proj, a_log, dt_bias, S)
    oa, states = _gdn_scan_fwd(intra, proj, onorm_g, S)
    ob, lse0, lse1 = _swa_fwd(proj, bt, S)
    wout, wgate, wup, wdown = late_weights(ob)
    mix, x1 = _mix_fwd(oa, ob, wout, x, g_mix_post, S)
    dx1, h2, act, dgate_f, dup_f, df, loss, d_gfpre, d_gfpost = _ffn(x1, tgt, g_ffn_pre, g_ffn_post, wgate, wup, wdown, S)
    g_gate = _wgrad(h2, dgate_f, S, "wgrad_gate", col_blocks=True)
    g_up = _wgrad(h2, dup_f, S, "wgrad_up", col_blocks=True)
    g_down = _wgrad(act, df, S, "wgrad_down").reshape(N_DEV, D_FF // N_DEV, D_MODEL)
    dmix, d_oab, d_gmpost = _mix_bwd(dx1, mix, g_mix_post, wout, S)
    g_out = jnp.concatenate([_wgrad(oa, dmix, S, "wgrad_out_a"), _wgrad(ob, dmix, S, "wgrad_out_b")], axis=0)
    token = early_grads(g_out.reshape(N_DEV, D_MODEL // N_DEV, D_MODEL), g_gate, g_up, g_down)
    dqb, dkb, dvb, dsb = _swa_bwd(proj, bt, ob, lse0, lse1, d_oab, token, S)
    *cots, dgate_a, d_og = _gdn_scan_bwd(intra, states, proj, d_oab, onorm_g, token, S)
    dqkv_u, dpg, d_alog, d_dtb = _gdn_intra_bwd(qkv_u, proj, a_log, dt_bias, t_inv, cots, S)
    dqkv_a, d_conv = _gdn_prep_bwd(proj, convw, dqkv_u.reshape(3, S // CHUNK, GDN_HEADS, CHUNK, GDN_HD), S)
    segs = [(dqkv_a, COL_A), (dgate_a, COL_A + 3 * GDN_W), (dqb, COL_B), (dkb, COL_B + SWA_W), (dvb, COL_B + 2 * SWA_W),
            (dpg, COL_G)]
    grad_x, h1, d_gmpre = _proj_bwd(x, dx1, g_mix_pre, wcat, segs, S)
    g_in = _wgrad_in(h1, segs, S)
    d_rel = _rel_bias_grad(dsb, bk, bk_np)
    small = dict(a_log=d_alog[:, :GDN_HEADS], dt_bias=d_dtb[:, :GDN_HEADS], onorm_g=d_og, rel_bias=d_rel,
                 g_mix_pre=d_gmpre, g_mix_post=d_gmpost, g_ffn_pre=d_gfpre, g_ffn_post=d_gfpost)
    return loss, grad_x, (g_in, d_conv), small


SMALL = ("a_log", "dt_bias", "onorm_g", "rel_bias", "g_mix_pre", "g_mix_post", "g_ffn_pre", "g_ffn_post")
PACK_ROWS = 8


def _pack_small(d, loss=None):
    rest = jnp.concatenate([d["onorm_g"].reshape(-1), d["a_log"].reshape(-1), d["dt_bias"].reshape(-1),
                            d["rel_bias"].reshape(-1)])
    rest = jnp.concatenate([rest, jnp.zeros((D_MODEL - rest.shape[0],), f32)])
    extra = jnp.zeros((D_MODEL,), f32) if loss is None else jnp.concatenate([loss.reshape(1), jnp.zeros((D_MODEL - 1,), f32)])
    rows = [d["g_mix_pre"].reshape(-1), d["g_mix_post"].reshape(-1), d["g_ffn_pre"].reshape(-1),
            d["g_ffn_post"].reshape(-1), rest, extra]
    return jnp.concatenate([jnp.stack(rows), jnp.zeros((PACK_ROWS - len(rows), D_MODEL), f32)], axis=0)


def _unpack_small(p):
    o = GDN_HD
    return dict(g_mix_pre=p[0:1], g_mix_post=p[1:2], g_ffn_pre=p[2:3], g_ffn_post=p[3:4],
                onorm_g=p[4:5, :o], a_log=p[4:5, o:o + 4], dt_bias=p[4:5, o + 4:o + 8],
                rel_bias=p[4, o + 8:o + 8 + NUM_BUCKETS * SWA_HEADS].reshape(NUM_BUCKETS, SWA_HEADS))


def kernel(x, w_in, conv_w, a_log, dt_bias, onorm_g, rel_bias, w_out, g_mix_pre, g_mix_post, w_gate, w_up, w_down, g_ffn_pre, g_ffn_post, loss_target, m_w_in, m_conv_w, m_a_log, m_dt_bias, m_onorm_g, m_rel_bias, m_w_out, m_g_mix_pre, m_g_mix_post, m_w_gate, m_w_up, m_w_down, m_g_ffn_pre, m_g_ffn_post, v_w_in, v_conv_w, v_a_log, v_dt_bias, v_onorm_g, v_rel_bias, v_w_out, v_g_mix_pre, v_g_mix_post, v_w_gate, v_w_up, v_w_down, v_g_ffn_pre, v_g_ffn_post):
    big = ("w_in", "conv_w", "w_out", "w_gate", "w_up", "w_down")
    w_sh = dict(w_in=w_in[0], conv_w=conv_w[0], w_out=w_out[0], w_gate=w_gate[0], w_up=w_up[0], w_down=w_down[0])
    m_sh = dict(w_in=m_w_in[0], conv_w=m_conv_w[0], w_out=m_w_out[0], w_gate=m_w_gate[0], w_up=m_w_up[0], w_down=m_w_down[0])
    v_sh = dict(w_in=v_w_in[0], conv_w=v_conv_w[0], w_out=v_w_out[0], w_gate=v_w_gate[0], w_up=v_w_up[0], w_down=v_w_down[0])
    w_small = dict(a_log=a_log, dt_bias=dt_bias, onorm_g=onorm_g, rel_bias=rel_bias, g_mix_pre=g_mix_pre,
                   g_mix_post=g_mix_post, g_ffn_pre=g_ffn_pre, g_ffn_post=g_ffn_post)
    m_small = dict(a_log=m_a_log, dt_bias=m_dt_bias, onorm_g=m_onorm_g, rel_bias=m_rel_bias, g_mix_pre=m_g_mix_pre,
                   g_mix_post=m_g_mix_post, g_ffn_pre=m_g_ffn_pre, g_ffn_post=m_g_ffn_post)
    v_small = dict(a_log=v_a_log, dt_bias=v_dt_bias, onorm_g=v_onorm_g, rel_bias=v_rel_bias, g_mix_pre=v_g_mix_pre,
                   g_mix_post=v_g_mix_post, g_ffn_pre=v_g_ffn_pre, g_ffn_post=v_g_ffn_post)

    me = 4 * lax.axis_index("x") + 2 * lax.axis_index("y") + lax.axis_index("c")
    own = lambda full, part: lax.dynamic_update_index_in_dim(full, part, me, 0)
    cols = lambda g: g.reshape(g.shape[0], N_DEV, g.shape[1] // N_DEV).transpose(1, 0, 2)
    late = ("w_out", "w_gate", "w_up", "w_down")

    late_src = [w_sh[k].astype(bf16) for k in late]
    g_in, g_conv = _all_gather([w_sh["w_in"].astype(bf16), w_sh["conv_w"]])
    g_sems, g_src, g_land, g_token = _exchange_start(late_src, g_conv, False, "late_weights_start")
    wcat = _wcat_from_blocks(g_in)
    convw = g_conv.transpose(1, 0, 2).reshape(4, 3 * GDN_W)

    def late_weights(after):
        lands = _exchange_wait(g_sems, g_src, g_land, after, False, "late_weights_wait")
        g_out, g_gate, g_up, g_down = [own(l, s) for l, s in zip(lands, late_src)]
        return (g_out.reshape(D_MODEL, D_MODEL), _full_from_col_blocks(g_gate), _full_from_col_blocks(g_up),
                g_down.reshape(D_FF, D_MODEL))

    early = {}

    def early_grads(*blocks):
        early["sems"], early["src"], early["land"], token = _exchange_start(list(blocks), blocks[0], True, "late_grads_start")
        return token

    loss_p, grad_x, (gw_in, gw_conv), gsmall = _local_step(
        x[0], loss_target[0], wcat, convw, late_weights, early_grads, g_token,
        a_log, dt_bias, onorm_g, rel_bias, g_mix_pre, g_mix_post, g_ffn_pre, g_ffn_post)

    r_in, r_conv, r_small = _grad_exchange([gw_in, cols(gw_conv)], [_pack_small(gsmall, loss_p[0, 0])])
    lands = _exchange_wait(early["sems"], early["src"], early["land"], grad_x, True, "late_grads_wait")
    recv = dict(w_in=r_in, conv_w=r_conv)
    for k, l, s in zip(late, lands, early["src"]):
        recv[k] = own(l, lax.dynamic_index_in_dim(s, me, 0, keepdims=False))

    outs = {}
    for k in big:
        outs[k] = _adamw(recv[k], w_sh[k], m_sh[k], v_sh[k], "adamw_" + k)
    sm = _adamw(r_small, _pack_small(w_small), _pack_small(m_small), _pack_small(v_small), "adamw_small")
    loss = sm[0][5, 0]
    sm = [_unpack_small(t) for t in sm]
    for k in SMALL:
        outs[k] = tuple(t[k].reshape(w_small[k].shape) for t in sm)

    order = ("w_in", "conv_w", "a_log", "dt_bias", "onorm_g", "rel_bias", "w_out", "g_mix_pre", "g_mix_post", "w_gate",
             "w_up", "w_down", "g_ffn_pre", "g_ffn_post")
    lead = lambda k, t: t[None] if k in big else t
    res = [loss, grad_x[None]]
    for i in range(4):
        res += [lead(k, outs[k][i]) for k in order]
    return tuple(res)
```

```python
import functools
import math

import numpy as np
import jax
import jax.numpy as jnp
from jax import lax
from jax.experimental import pallas as pl
from jax.experimental.pallas import tpu as pltpu

f32 = jnp.float32
bf16 = jnp.bfloat16
SDS = jax.ShapeDtypeStruct

D_MODEL = 1024
GDN_HEADS = 4
GDN_HD = 128
GDN_W = 512
CHUNK = 64
SWA_HEADS = 8
SWA_HD = 64
SWA_W = 512
D_FF = 2816
IN_COLS = 3592
PATTERNS = ((128, 1), (512, 4), (2048, 16))
SWA_BLK = 128
NUM_BUCKETS = 32
MAX_DISTANCE = 2048
RMS_EPS = 1e-6
NEG = -1e30
N_DEV = 8

COL_A = 0
COL_B = 2048
COL_G = 3584
NCOL = 3712
LANE = 128

ADAM_LR, ADAM_B1, ADAM_B2, ADAM_EPS, ADAM_WD, ADAM_STEP = 0.001, 0.9, 0.999, 1e-08, 0.01, 10

VMEM_LIMIT = 56 * 1024 * 1024

HI = lax.Precision.HIGHEST
HIGH = lax.Precision.HIGH


def _cp(n_grid=0, vmem=None):
    kw = {}
    if n_grid:
        kw["dimension_semantics"] = ("arbitrary",) * n_grid
    if vmem:
        kw["vmem_limit_bytes"] = vmem
    return pltpu.CompilerParams(**kw)


def _dot(a, b):
    return jnp.dot(a, b, preferred_element_type=f32)


def _dot_nt(a, b):
    return lax.dot_general(a, b, (((1,), (1,)), ((), ())), preferred_element_type=f32)


def _dot_tn(a, b):
    return lax.dot_general(a, b, (((0,), (0,)), ((), ())), preferred_element_type=f32)


def _dot_hi(a, b):
    return jnp.dot(a, b, precision=HI, preferred_element_type=f32)


def _sigmoid(x):
    return 1.0 / (1.0 + jnp.exp(-x))


def _softplus(x):
    return jnp.maximum(x, 0.0) + jnp.log(1.0 + jnp.exp(-jnp.abs(x)))


def _const_spec(shape):
    nd = len(shape)
    return pl.BlockSpec(shape, lambda *_: (0,) * nd)


def _resident_spec(shape):
    nd = len(shape)
    return pl.BlockSpec(shape, lambda *_: (0,) * nd, pipeline_mode=pl.Buffered(1))


def _t5_bucket_np(dist):
    max_exact = NUM_BUCKETS // 2
    d = np.maximum(dist, 1).astype(np.float32)
    log_b = max_exact + (np.log(d / np.float32(max_exact)) / np.float32(math.log(MAX_DISTANCE / max_exact))
                         * np.float32(NUM_BUCKETS - max_exact)).astype(np.int32)
    return np.where(dist < max_exact, dist, np.minimum(log_b, NUM_BUCKETS - 1)).astype(np.int32)


def _bucket_tables():
    w = SWA_BLK
    qi = np.arange(w)[:, None]
    kj = np.arange(w)[None, :]
    rel = np.where(kj <= qi, qi - kj, qi + w - kj)
    out = np.zeros((len(PATTERNS), w, w), np.int32)
    for p, (_, dil) in enumerate(PATTERNS):
        steps = _t5_bucket_np(np.arange(w + 1) * dil)
        assert steps[w] == steps[w - 1]
        out[p] = steps[rel]
    return out


def _bias_tables(rel_bias, bk):
    def body(rb_ref, bk_ref, o_ref):
        b_idx = bk_ref[0]
        for h in range(SWA_HEADS):
            def lp(b, acc):
                return jnp.where(b_idx == b, rb_ref[b, h], acc)
            o_ref[0, h] = lax.fori_loop(0, NUM_BUCKETS, lp, jnp.zeros((SWA_BLK, SWA_BLK), f32))

    return pl.pallas_call(
        body, name="bias_tables", grid=(3,),
        in_specs=[pl.BlockSpec(memory_space=pltpu.SMEM), pl.BlockSpec((1, SWA_BLK, SWA_BLK), lambda p: (p, 0, 0))],
        out_specs=pl.BlockSpec((1, SWA_HEADS, SWA_BLK, SWA_BLK), lambda p: (p, 0, 0, 0)),
        out_shape=SDS((3, SWA_HEADS, SWA_BLK, SWA_BLK), f32),
        compiler_params=_cp(1),
    )(rel_bias, bk)


def _rel_bias_grad(dsb, bk, bk_np):
    present = [sorted(set(int(v) for v in np.unique(bk_np[p]))) for p in range(3)]

    def body(ds_ref, bk_ref, o_ref):
        row = lax.broadcasted_iota(jnp.int32, (NUM_BUCKETS, LANE), 0)
        col = lax.broadcasted_iota(jnp.int32, (NUM_BUCKETS, SWA_HEADS), 1)
        out = jnp.zeros((NUM_BUCKETS, SWA_HEADS), f32)
        for hp in range(4):
            for hh in range(2):
                acc = jnp.zeros((NUM_BUCKETS, LANE), f32)
                for p in range(3):
                    tile = ds_ref[hp, p, hh]
                    b_idx = bk_ref[p]
                    for b in present[p]:
                        part = jnp.sum(jnp.where(b_idx == b, tile, 0.0), axis=0, keepdims=True)
                        acc = acc + jnp.where(row == b, part, 0.0)
                tot = jnp.sum(acc, axis=1, keepdims=True)
                out = out + jnp.where(col == 2 * hp + hh, tot, 0.0)
        o_ref[...] = out

    return pl.pallas_call(body, name="rel_bias_grad", out_shape=SDS((NUM_BUCKETS, SWA_HEADS), f32),
                          compiler_params=_cp(0, 32 * 1024 * 1024))(dsb, bk)


def _proj_fwd(x, g_pre, wcat, after, S):
    TS = 256

    def body(x_ref, g_ref, w_ref, after_ref, o_ref, h_ref):
        xv = x_ref[...]
        r = lax.rsqrt(jnp.mean(xv * xv, axis=-1, keepdims=True) + RMS_EPS)
        h = (xv * r * g_ref[...]).astype(bf16)
        h_ref[...] = h
        o_ref[...] = _dot(h, w_ref[...])

    return pl.pallas_call(
        body, name="proj_fwd", grid=(S // TS,),
        in_specs=[pl.BlockSpec((TS, D_MODEL), lambda i: (i, 0)), _const_spec((1, D_MODEL)),
                  _resident_spec((D_MODEL, NCOL)), _ANY],
        out_specs=[pl.BlockSpec((TS, NCOL), lambda i: (i, 0)), pl.BlockSpec((TS, D_MODEL), lambda i: (i, 0))],
        out_shape=[SDS((S, NCOL), f32), SDS((S, D_MODEL), bf16)],
        compiler_params=_cp(1, VMEM_LIMIT),
    )(x, g_pre, wcat, after)


CONV_RT = 256
HALO = 8


CONV_NC = CONV_RT // CHUNK


def _gdn_prep(proj, conv_w, S):
    def body(p_ref, cw_ref, o_ref, xs_ref):
        t = pl.program_id(0)
        xs_ref[pl.ds(0, HALO), :] = jnp.zeros((HALO, LANE), f32)
        xs_ref[pl.ds(HALO, S), :] = p_ref[...]
        w = cw_ref[...]
        is_qk = t < 2
        scale = jnp.where(t == 0, GDN_HD ** -0.5, 1.0).astype(f32)

        def lp(c, carry):
            st = pl.multiple_of(c * CONV_RT, CONV_RT)
            pre = xs_ref[pl.ds(st + HALO - 3, CONV_RT), :] * w[0:1, :]
            for i in range(1, 4):
                pre = pre + xs_ref[pl.ds(st + HALO - 3 + i, CONV_RT), :] * w[i:i + 1, :]
            s = pre * _sigmoid(pre)
            nrm = s * lax.rsqrt(jnp.sum(s * s, axis=-1, keepdims=True) + 1e-6) * scale
            out = jnp.where(is_qk, nrm, s)
            for i in range(CONV_NC):
                o_ref[0, c * CONV_NC + i, 0] = out[i * CHUNK:(i + 1) * CHUNK]
            return carry

        lax.fori_loop(0, S // CONV_RT, lp, 0)

    return pl.pallas_call(
        body, name="gdn_prep", grid=(3, GDN_HEADS),
        in_specs=[pl.BlockSpec((S, LANE), lambda t, h: (0, t * GDN_HEADS + h)),
                  pl.BlockSpec((4, LANE), lambda t, h: (0, t * GDN_HEADS + h))],
        out_specs=pl.BlockSpec((1, S // CHUNK, 1, CHUNK, GDN_HD), lambda t, h: (t, 0, h, 0, 0)),
        out_shape=SDS((3, S // CHUNK, GDN_HEADS, CHUNK, GDN_HD), f32),
        scratch_shapes=[pltpu.VMEM((S + HALO, LANE), f32)],
        compiler_params=_cp(2, VMEM_LIMIT),
    )(proj, conv_w)


def _gdn_prep_bwd(proj, conv_w, dqkv, S):
    def body(p_ref, cw_ref, d_ref, dx_ref, dw_ref, xs_ref, dp_ref):
        t = pl.program_id(0)
        xs_ref[pl.ds(0, HALO), :] = jnp.zeros((HALO, LANE), f32)
        xs_ref[pl.ds(HALO, S), :] = p_ref[...]
        dp_ref[pl.ds(S, HALO), :] = jnp.zeros((HALO, LANE), f32)
        w = cw_ref[...]
        is_qk = t < 2
        scale = jnp.where(t == 0, GDN_HD ** -0.5, 1.0).astype(f32)

        def lp1(c, dw):
            st = pl.multiple_of(c * CONV_RT, CONV_RT)
            taps = [xs_ref[pl.ds(st + HALO - 3 + i, CONV_RT), :] for i in range(4)]
            pre = taps[0] * w[0:1, :]
            for i in range(1, 4):
                pre = pre + taps[i] * w[i:i + 1, :]
            sg = _sigmoid(pre)
            s = pre * sg
            d_out = jnp.concatenate([d_ref[0, c * CONV_NC + i, 0] for i in range(CONV_NC)], axis=0)
            rn = lax.rsqrt(jnp.sum(s * s, axis=-1, keepdims=True) + 1e-6)
            n = s * rn
            dn = d_out * scale
            ds_qk = rn * (dn - n * jnp.sum(dn * n, axis=-1, keepdims=True))
            ds = jnp.where(is_qk, ds_qk, d_out)
            dpre = ds * (sg * (1.0 + pre * (1.0 - sg)))
            dp_ref[pl.ds(st, CONV_RT), :] = dpre
            return tuple(dw[i] + jnp.sum(dpre * taps[i], axis=0, keepdims=True) for i in range(4))

        z = jnp.zeros((1, LANE), f32)
        dw = lax.fori_loop(0, S // CONV_RT, lp1, (z, z, z, z))
        for i in range(4):
            dw_ref[pl.ds(i, 1), :] = dw[i]

        def lp2(c, carry):
            st = pl.multiple_of(c * CONV_RT, CONV_RT)
            dx = dp_ref[pl.ds(st, CONV_RT), :] * w[3:4, :]
            for i in range(3):
                dx = dx + dp_ref[pl.ds(st + 3 - i, CONV_RT), :] * w[i:i + 1, :]
            dx_ref[pl.ds(st, CONV_RT), :] = dx.astype(bf16)
            return carry

        lax.fori_loop(0, S // CONV_RT, lp2, 0)

    col = lambda rows: pl.BlockSpec((rows, LANE), lambda t, h: (0, t * GDN_HEADS + h))
    return pl.pallas_call(
        body, name="gdn_prep_bwd", grid=(3, GDN_HEADS),
        in_specs=[col(S), col(4), pl.BlockSpec((1, S // CHUNK, 1, CHUNK, GDN_HD), lambda t, h: (t, 0, h, 0, 0))],
        out_specs=[col(S), col(4)],
        out_shape=[SDS((S, 3 * GDN_W), bf16), SDS((4, 3 * GDN_W), f32)],
        scratch_shapes=[pltpu.VMEM((S + HALO, LANE), f32), pltpu.VMEM((S + HALO, LANE), f32)],
        compiler_params=_cp(2, VMEM_LIMIT),
    )(proj, conv_w, dqkv)


def _bdot(a, b, prec=None):
    return lax.dot_general(a, b, (((2,), (1,)), ((0,), (0,))), precision=prec, preferred_element_type=f32)


def _bdot_nt(a, b, prec=None):
    return lax.dot_general(a, b, (((2,), (2,)), ((0,), (0,))), precision=prec, preferred_element_type=f32)


def _bdot_tn(a, b, prec=None):
    return lax.dot_general(a, b, (((1,), (1,)), ((0,), (0,))), precision=prec, preferred_element_type=f32)


@jax.custom_vjp
def _tri_inv_saved(a, t):
    return t


def _tri_inv_saved_fwd(a, t):
    return t, t


def _tri_inv_saved_bwd(t, dt):
    return -_bdot_tn(t, _bdot_nt(dt, t, HIGH), HIGH), jnp.zeros_like(t)


_tri_inv_saved.defvjp(_tri_inv_saved_fwd, _tri_inv_saved_bwd)


def _gdn_intra(q, k, v, bl, al, a_log, dt_bias, t_saved=None):
    nb = q.shape[0]
    c = CHUNK
    ii = lax.broadcasted_iota(jnp.int32, (c, c), 0)
    jj = lax.broadcasted_iota(jnp.int32, (c, c), 1)
    eye = ii == jj
    tril = ii >= jj
    strict = ii > jj
    ones = jnp.ones((nb, c, c), f32)
    eye_f = eye.astype(f32)

    beta = _sigmoid(bl)
    g = -jnp.exp(a_log) * _softplus(al + dt_bias)
    g_row = _bdot(ones, jnp.where(eye, g, 0.0), HI)
    gc = jnp.sum(jnp.where(tril, g_row, 0.0), axis=2, keepdims=True)
    gc_row = _bdot(ones, jnp.where(eye, gc, 0.0), HI)
    decay = jnp.where(tril, jnp.exp(jnp.where(tril, gc - gc_row, 0.0)), 0.0)
    last = lax.broadcasted_iota(jnp.int32, (c, 1), 0) == c - 1
    gc_last = jnp.sum(jnp.where(last, gc, 0.0), axis=1, keepdims=True)
    e_gc = jnp.exp(gc)

    kb = k * beta
    k16 = k.astype(bf16)
    a = jnp.where(strict, _bdot_nt(kb.astype(bf16), k16) * decay, 0.0)
    if t_saved is None:
        xp = -a
        t_inv = eye_f + xp
        for _ in range(5):
            xp = _bdot(xp, xp, HIGH)
            t_inv = _bdot(t_inv, eye_f + xp, HIGH)
    else:
        t_inv = _tri_inv_saved(a, t_saved)
    t16 = t_inv.astype(bf16)
    u = _bdot(t16, (v * beta).astype(bf16))
    w = _bdot(t16, (kb * e_gc).astype(bf16))
    attn = jnp.where(tril, _bdot_nt(q.astype(bf16), k16) * decay, 0.0)
    gam = jnp.broadcast_to(jnp.exp(gc_last), (nb, 1, GDN_HD))
    return u, w, attn, q * e_gc, k * jnp.exp(gc_last - gc), gam, t_inv


GDN_TB = 256
GDN_NC = GDN_TB // CHUNK
GDN_NU = GDN_NC * GDN_HEADS


def _gdn_unit_inputs(qkv_ref, pg_ref, al_ref, db_ref):
    pg = pg_ref[...]
    bl = jnp.stack([pg[cl * CHUNK:(cl + 1) * CHUNK, h:h + 1] for cl in range(GDN_NC) for h in range(GDN_HEADS)])
    al = jnp.stack([pg[cl * CHUNK:(cl + 1) * CHUNK, GDN_HEADS + h:GDN_HEADS + h + 1]
                    for cl in range(GDN_NC) for h in range(GDN_HEADS)])
    a_log = jnp.stack([jnp.full((1, 1), al_ref[0, h], f32) for _ in range(GDN_NC) for h in range(GDN_HEADS)])
    dt_b = jnp.stack([jnp.full((1, 1), db_ref[0, h], f32) for _ in range(GDN_NC) for h in range(GDN_HEADS)])
    return qkv_ref[0], qkv_ref[1], qkv_ref[2], bl, al, a_log, dt_b


def _unit_spec(*tail):
    nd = len(tail)
    return pl.BlockSpec((GDN_NU,) + tail, lambda i: (i,) + (0,) * nd)


def _gdn_intra_shapes(S):
    nu = S // CHUNK * GDN_HEADS
    row = SDS((nu, CHUNK, GDN_HD), f32)
    return [row, row, SDS((nu, CHUNK, CHUNK), f32), row, row, SDS((nu, 1, GDN_HD), f32)]


_GDN_INTRA_SPECS = lambda: [_unit_spec(CHUNK, GDN_HD), _unit_spec(CHUNK, GDN_HD), _unit_spec(CHUNK, CHUNK),
                            _unit_spec(CHUNK, GDN_HD), _unit_spec(CHUNK, GDN_HD), _unit_spec(1, GDN_HD)]


def _gdn_intra_fwd(qkv_u, proj, a_log, dt_bias, S):
    def body(qkv_ref, pg_ref, al_ref, db_ref, *outs):
        res = _gdn_intra(*_gdn_unit_inputs(qkv_ref, pg_ref, al_ref, db_ref))
        for o_ref, r in zip(outs, res):
            o_ref[...] = r

    nu = S // CHUNK * GDN_HEADS
    *intra, t_inv = pl.pallas_call(
        body, name="gdn_intra_fwd", grid=(S // GDN_TB,),
        in_specs=[pl.BlockSpec((3, GDN_NU, CHUNK, GDN_HD), lambda i: (0, i, 0, 0)),
                  pl.BlockSpec((GDN_TB, LANE), lambda i: (i, COL_G // LANE)),
                  pl.BlockSpec(memory_space=pltpu.SMEM), pl.BlockSpec(memory_space=pltpu.SMEM)],
        out_specs=_GDN_INTRA_SPECS() + [_unit_spec(CHUNK, CHUNK)],
        out_shape=_gdn_intra_shapes(S) + [SDS((nu, CHUNK, CHUNK), f32)],
        compiler_params=_cp(1, VMEM_LIMIT),
    )(qkv_u, proj, a_log, dt_bias)
    return intra, t_inv


def _gdn_intra_bwd(qkv_u, proj, a_log, dt_bias, t_inv, cots, S):
    def body(qkv_ref, pg_ref, al_ref, db_ref, t_ref, du_ref, dw_ref, da_ref, dqd_ref, dkd_ref, dgm_ref,
             dqkv_ref, dpg_ref, dal_ref, ddb_ref):
        @pl.when(pl.program_id(0) == 0)
        def _():
            dal_ref[...] = jnp.zeros_like(dal_ref)
            ddb_ref[...] = jnp.zeros_like(ddb_ref)

        t_saved = t_ref[...]
        _, vjp = jax.vjp(lambda *a: _gdn_intra(*a, t_saved=t_saved)[:6], *_gdn_unit_inputs(qkv_ref, pg_ref, al_ref, db_ref))
        dq, dk, dv, dbl, dal, da, ddb = vjp((du_ref[...], dw_ref[...], da_ref[...], dqd_ref[...], dkd_ref[...], dgm_ref[...]))
        dqkv_ref[0] = dq
        dqkv_ref[1] = dk
        dqkv_ref[2] = dv
        lane = lax.broadcasted_iota(jnp.int32, (CHUNK, LANE), 1)
        lane1 = lax.broadcasted_iota(jnp.int32, (1, LANE), 1)
        da_tot = jnp.zeros((1, LANE), f32)
        ddb_tot = jnp.zeros((1, LANE), f32)
        for cl in range(GDN_NC):
            dpg = jnp.zeros((CHUNK, LANE), f32)
            for h in range(GDN_HEADS):
                b = cl * GDN_HEADS + h
                dpg = dpg + jnp.where(lane == h, dbl[b], 0.0) + jnp.where(lane == GDN_HEADS + h, dal[b], 0.0)
                da_tot = da_tot + jnp.where(lane1 == h, da[b], 0.0)
                ddb_tot = ddb_tot + jnp.where(lane1 == h, ddb[b], 0.0)
            dpg_ref[cl * CHUNK:(cl + 1) * CHUNK, :] = dpg.astype(bf16)
        dal_ref[...] += da_tot
        ddb_ref[...] += ddb_tot

    acc = _const_spec((1, LANE))
    nu = S // CHUNK * GDN_HEADS
    return pl.pallas_call(
        body, name="gdn_intra_bwd", grid=(S // GDN_TB,),
        in_specs=[pl.BlockSpec((3, GDN_NU, CHUNK, GDN_HD), lambda i: (0, i, 0, 0)),
                  pl.BlockSpec((GDN_TB, LANE), lambda i: (i, COL_G // LANE)),
                  pl.BlockSpec(memory_space=pltpu.SMEM), pl.BlockSpec(memory_space=pltpu.SMEM),
                  _unit_spec(CHUNK, CHUNK)] + _GDN_INTRA_SPECS(),
        out_specs=[pl.BlockSpec((3, GDN_NU, CHUNK, GDN_HD), lambda i: (0, i, 0, 0)),
                   pl.BlockSpec((GDN_TB, LANE), lambda i: (i, 0)), acc, acc],
        out_shape=[SDS((3, nu, CHUNK, GDN_HD), f32), SDS((S, LANE), bf16), SDS((1, LANE), f32), SDS((1, LANE), f32)],
        compiler_params=_cp(1, VMEM_LIMIT),
    )(qkv_u, proj, a_log, dt_bias, t_inv, *cots)


def _gdn_scan_fwd(intra, proj, onorm_g, S):
    def body(u_ref, w_ref, at_ref, qd_ref, kd_ref, gm_ref, gate_ref, og_ref, out_ref, st_ref, s_scr):
        @pl.when(pl.program_id(0) == 0)
        def _():
            s_scr[...] = jnp.zeros_like(s_scr)

        og = og_ref[...]
        s = s_scr[...]
        for cl in range(GDN_NC):
            us = slice(cl * GDN_HEADS, (cl + 1) * GDN_HEADS)
            rows = slice(cl * CHUNK, (cl + 1) * CHUNK)
            st_ref[us] = s
            s16 = s.astype(bf16)
            vn = u_ref[us] - _bdot(w_ref[us].astype(bf16), s16)
            vn16 = vn.astype(bf16)
            o = _bdot(qd_ref[us].astype(bf16), s16) + _bdot(at_ref[us].astype(bf16), vn16)
            s = s * gm_ref[us] + _bdot_tn(kd_ref[us].astype(bf16), vn16)
            for h in range(GDN_HEADS):
                oh = o[h]
                gt = gate_ref[rows, h * GDN_HD:(h + 1) * GDN_HD]
                on = oh * lax.rsqrt(jnp.mean(oh * oh, axis=-1, keepdims=True) + RMS_EPS) * og
                out_ref[rows, h * GDN_HD:(h + 1) * GDN_HD] = on * (gt * _sigmoid(gt))
        s_scr[...] = s

    nu = S // CHUNK * GDN_HEADS
    return pl.pallas_call(
        body, name="gdn_scan_fwd", grid=(S // GDN_TB,),
        in_specs=_GDN_INTRA_SPECS() + [pl.BlockSpec((GDN_TB, GDN_W), lambda i: (i, 3)), _const_spec((1, GDN_HD))],
        out_specs=[pl.BlockSpec((GDN_TB, GDN_W), lambda i: (i, 0)), _unit_spec(GDN_HD, GDN_HD)],
        out_shape=[SDS((S, GDN_W), f32), SDS((nu, GDN_HD, GDN_HD), f32)],
        scratch_shapes=[pltpu.VMEM((GDN_HEADS, GDN_HD, GDN_HD), f32)],
        compiler_params=_cp(1, VMEM_LIMIT),
    )(*intra, proj, onorm_g)


def _gdn_scan_bwd(intra, states, proj, d_oab, onorm_g, after, S):
    n_steps = S // GDN_TB

    def body(u_ref, w_ref, at_ref, qd_ref, kd_ref, gm_ref, st_ref, gate_ref, do_ref, og_ref, after_ref,
             du_ref, dw_ref, dat_ref, dqd_ref, dkd_ref, dgm_ref, dgate_ref, dog_ref, ds_scr):
        @pl.when(pl.program_id(0) == 0)
        def _():
            ds_scr[...] = jnp.zeros_like(ds_scr)
            dog_ref[...] = jnp.zeros_like(dog_ref)

        og = og_ref[...]
        ii = lax.broadcasted_iota(jnp.int32, (CHUNK, CHUNK), 0)
        jj = lax.broadcasted_iota(jnp.int32, (CHUNK, CHUNK), 1)
        tril = ii >= jj
        ds = ds_scr[...]
        dog = jnp.zeros((1, GDN_HD), f32)
        for cl in reversed(range(GDN_NC)):
            us = slice(cl * GDN_HEADS, (cl + 1) * GDN_HEADS)
            rows = slice(cl * CHUNK, (cl + 1) * CHUNK)
            s0 = st_ref[us]
            s016 = s0.astype(bf16)
            w16 = w_ref[us].astype(bf16)
            qd16 = qd_ref[us].astype(bf16)
            kd16 = kd_ref[us].astype(bf16)
            at16 = at_ref[us].astype(bf16)
            vn = u_ref[us] - _bdot(w16, s016)
            vn16 = vn.astype(bf16)
            o = _bdot(qd16, s016) + _bdot(at16, vn16)
            do_h = []
            for h in range(GDN_HEADS):
                oh = o[h]
                lanes = slice(h * GDN_HD, (h + 1) * GDN_HD)
                gt = gate_ref[rows, lanes]
                d_out = do_ref[rows, lanes]
                r = lax.rsqrt(jnp.mean(oh * oh, axis=-1, keepdims=True) + RMS_EPS)
                n = oh * r
                sg = _sigmoid(gt)
                silu = gt * sg
                dog = dog + jnp.sum(d_out * n * silu, axis=0, keepdims=True)
                dgate_ref[rows, lanes] = (d_out * n * og * (sg * (1.0 + gt * (1.0 - sg)))).astype(bf16)
                dn = d_out * og * silu
                do_h.append(r * (dn - n * jnp.mean(dn * n, axis=-1, keepdims=True)))
            do16 = jnp.stack(do_h).astype(bf16)
            ds16 = ds.astype(bf16)
            dvn = _bdot_tn(at16, do16) + _bdot(kd16, ds16)
            dvn16 = dvn.astype(bf16)
            du_ref[us] = dvn
            dw_ref[us] = -_bdot_nt(dvn16, s016)
            dat_ref[us] = jnp.where(tril, _bdot_nt(do16, vn16), 0.0)
            dqd_ref[us] = _bdot_nt(do16, s016)
            dkd_ref[us] = _bdot_nt(vn16, ds16)
            dgm_ref[us] = jnp.sum(s0 * ds, axis=1, keepdims=True)
            ds = _bdot_tn(qd16, do16) + ds * gm_ref[us] - _bdot_tn(w16, dvn16)
        ds_scr[...] = ds
        dog_ref[...] += dog

    def unit(*tail):
        nd = len(tail)
        return pl.BlockSpec((GDN_NU,) + tail, lambda i: (n_steps - 1 - i,) + (0,) * nd)

    intra_specs = [unit(CHUNK, GDN_HD), unit(CHUNK, GDN_HD), unit(CHUNK, CHUNK), unit(CHUNK, GDN_HD),
                   unit(CHUNK, GDN_HD), unit(1, GDN_HD)]
    tok = lambda c: pl.BlockSpec((GDN_TB, GDN_W), lambda i: (n_steps - 1 - i, c))
    return pl.pallas_call(
        body, name="gdn_scan_bwd", grid=(n_steps,),
        in_specs=intra_specs + [unit(GDN_HD, GDN_HD), tok(3), tok(0), _const_spec((1, GDN_HD)), _ANY],
        out_specs=intra_specs + [tok(0), _const_spec((1, GDN_HD))],
        out_shape=_gdn_intra_shapes(S) + [SDS((S, GDN_W), bf16), SDS((1, GDN_HD), f32)],
        scratch_shapes=[pltpu.VMEM((GDN_HEADS, GDN_HD, GDN_HD), f32)],
        compiler_params=_cp(1, VMEM_LIMIT),
    )(*intra, states, proj, d_oab, onorm_g, after)


SWA_UNROLL = 4


def _swa_tiles(q_ref, k_ref, v_ref, it, d, nb_log2, S):
    nb = 1 << nb_log2
    r = lax.shift_right_logical(it, nb_log2)
    blk = lax.bitwise_and(it, nb - 1)
    qs = blk * (SWA_BLK * d) + r
    ps = jnp.maximum(blk - 1, 0) * (SWA_BLK * d) + r
    if d > 1:
        rows_c, rows_p = pl.ds(qs, SWA_BLK, stride=d), pl.ds(ps, SWA_BLK, stride=d)
    else:
        rows_c, rows_p = pl.ds(pl.multiple_of(qs, SWA_BLK), SWA_BLK), pl.ds(pl.multiple_of(ps, SWA_BLK), SWA_BLK)
    return rows_c, rows_p, blk > 0


def _swa_fwd(proj, bt, S):
    scale = SWA_HD ** -0.5

    def body(q_ref, k_ref, v_ref, bt_ref, o_ref, lse0_ref, lse1_ref, m0_scr, m1_scr, a0_scr, a1_scr):
        lane = lax.broadcasted_iota(jnp.int32, (SWA_BLK, LANE), 1)
        h0 = lane < SWA_HD
        qi = lax.broadcasted_iota(jnp.int32, (SWA_BLK, SWA_BLK), 0)
        kj = lax.broadcasted_iota(jnp.int32, (SWA_BLK, SWA_BLK), 1)
        lower = kj <= qi
        ones16 = jnp.ones((LANE, SWA_BLK), bf16)
        m_scrs = (m0_scr, m1_scr)
        a_scrs = (a0_scr, a1_scr)
        for p, (_, d) in reversed(list(enumerate(PATTERNS))):
            nb_log2 = int(math.log2(S // d // SWA_BLK))
            first = p == len(PATTERNS) - 1

            def lp(i, carry, p=p, d=d, nb_log2=nb_log2, first=first):
                heads = [h0, jnp.logical_not(h0)]
                tiles = []
                for u in range(SWA_UNROLL):
                    rows_c, rows_p, has_prev = _swa_tiles(q_ref, k_ref, v_ref, i * SWA_UNROLL + u, d, nb_log2, S)
                    q = q_ref[rows_c, :]
                    kc = k_ref[rows_c, :].astype(bf16)
                    kp_f = k_ref[rows_p, :]
                    kp = kp_f.astype(bf16)
                    logits = []
                    for mh in heads:
                        q_h = jnp.where(mh, q, 0.0)
                        qh = q_h.astype(bf16)
                        logits.append((_dot_nt(qh, kc), _dot_nt(qh, kp), _dot((q_h * kp_f).astype(bf16), ones16)))
                    tiles.append((rows_c, rows_p, has_prev, logits))
                probs = []
                for rows_c, rows_p, has_prev, logits in tiles:
                    keep = jnp.logical_or(lower, has_prev)
                    per_head = []
                    for h, (s_c, s_p, far) in enumerate(logits):
                        s = jnp.where(keep, jnp.where(lower, s_c, s_p) * scale + bt_ref[p, h], NEG)
                        s_far = jnp.where(has_prev, far * scale + bt_ref[p, h, SWA_BLK - 1:SWA_BLK, 0:1], NEG)
                        if first:
                            mn = jnp.maximum(s_far, jnp.max(s, axis=1, keepdims=True))
                            alpha = None
                        else:
                            mo = m_scrs[h][rows_c, :]
                            mn = jnp.maximum(jnp.maximum(mo, s_far), jnp.max(s, axis=1, keepdims=True))
                            alpha = jnp.exp(mo - mn)
                        pm = jnp.exp(s - mn)
                        per_head.append((mn, alpha, jnp.exp(s_far - mn), jnp.where(lower, pm, 0.0).astype(bf16),
                                         jnp.where(lower, 0.0, pm).astype(bf16)))
                    probs.append(per_head)
                acc_old = [None if first else (a0_scr[t[0], :], a1_scr[t[0], :]) for t in tiles]
                done = []
                for (rows_c, rows_p, _, _), per_head, old in zip(tiles, probs, acc_old):
                    vc = v_ref[rows_c, :]
                    vp = v_ref[rows_p, :]
                    acc_new = []
                    for h, (mn, alpha, p_far, pc16, pp16) in enumerate(per_head):
                        vpa = jnp.where(heads[h], vp, 1.0)
                        pv = _dot(pc16, jnp.where(heads[h], vc, 1.0).astype(bf16)) + _dot(pp16, vpa.astype(bf16)) + p_far * vpa
                        acc_new.append(pv if first else alpha * old[h] + pv)
                    done.append((rows_c, per_head[0][0], per_head[1][0], acc_new[0], acc_new[1]))
                for rows_c, m0_new, m1_new, a0_new, a1_new in done:
                    m0_scr[rows_c, :] = m0_new
                    m1_scr[rows_c, :] = m1_new
                    a0_scr[rows_c, :] = a0_new
                    a1_scr[rows_c, :] = a1_new
                return carry

            lax.fori_loop(0, S // SWA_BLK // SWA_UNROLL, lp, 0)

        def fin(c, carry):
            rows = pl.ds(pl.multiple_of(c * SWA_BLK, SWA_BLK), SWA_BLK)
            a0 = a0_scr[rows, :]
            a1 = a1_scr[rows, :]
            l0 = jnp.where(h0, pltpu.roll(a0, SWA_HD, 1), a0)
            l1 = jnp.where(h0, a1, pltpu.roll(a1, SWA_HD, 1))
            o_ref[rows, :] = jnp.where(h0, a0 / l0, a1 / l1)
            lse0_ref[rows, :] = m0_scr[rows, :] + jnp.log(l0)
            lse1_ref[rows, :] = m1_scr[rows, :] + jnp.log(l1)
            return carry

        lax.fori_loop(0, S // SWA_BLK, fin, 0)

    qb = COL_B // LANE
    col = lambda c: pl.BlockSpec((S, LANE), lambda hp, c=c: (0, c + hp))
    return pl.pallas_call(
        body, name="swa_fwd", grid=(4,),
        in_specs=[col(qb), col(qb + 4), col(qb + 8), pl.BlockSpec((3, 2, SWA_BLK, SWA_BLK), lambda hp: (0, hp, 0, 0))],
        out_specs=[col(0), col(0), col(0)],
        out_shape=[SDS((S, SWA_W), f32)] * 3,
        scratch_shapes=[pltpu.VMEM((S, LANE), f32)] * 4,
        compiler_params=_cp(1, VMEM_LIMIT),
    )(proj, proj, proj, bt)


def _swa_bwd(proj, bt, ob, lse0, lse1, d_oab, after, S):
    scale = SWA_HD ** -0.5

    def body(q_ref, k_ref, v_ref, bt_ref, o_ref, lse0_ref, lse1_ref, do_ref, after_ref, dq_ref, dk_ref, dv_ref, dsb_ref,
             dq_scr, dk_scr, dv_scr):
        lane = lax.broadcasted_iota(jnp.int32, (SWA_BLK, LANE), 1)
        h0 = lane < SWA_HD
        qi = lax.broadcasted_iota(jnp.int32, (SWA_BLK, SWA_BLK), 0)
        kj = lax.broadcasted_iota(jnp.int32, (SWA_BLK, SWA_BLK), 1)
        lower = kj <= qi
        eye = kj == qi
        rel127 = jnp.logical_or(kj == qi + 1, jnp.logical_and(qi == SWA_BLK - 1, kj == 0))
        ones16 = jnp.ones((LANE, SWA_BLK), bf16)
        lse_refs = (lse0_ref, lse1_ref)
        dq_scr[...] = jnp.zeros((S, LANE), f32)
        dk_scr[...] = jnp.zeros((S, LANE), f32)
        dv_scr[...] = jnp.zeros((S, LANE), f32)
        dsb_ref[...] = jnp.zeros_like(dsb_ref)

        def row_sum(a):
            hi = a.astype(bf16)
            return _dot(hi, ones16) + _dot((a - hi.astype(f32)).astype(bf16), ones16)

        for p, (_, d) in enumerate(PATTERNS):
            nb_log2 = int(math.log2(S // d // SWA_BLK))

            def lp(i, carry, p=p, d=d, nb_log2=nb_log2):
                heads = [h0, jnp.logical_not(h0)]
                tiles = []
                for u in range(SWA_UNROLL):
                    rows_c, rows_p, has_prev = _swa_tiles(q_ref, k_ref, v_ref, i * SWA_UNROLL + u, d, nb_log2, S)
                    q = q_ref[rows_c, :]
                    kc = k_ref[rows_c, :].astype(bf16)
                    kp_f = k_ref[rows_p, :]
                    kp = kp_f.astype(bf16)
                    vc = v_ref[rows_c, :].astype(bf16)
                    vp_f = v_ref[rows_p, :]
                    vp = vp_f.astype(bf16)
                    do = do_ref[rows_c, :]
                    dlt = do * o_ref[rows_c, :]
                    per_head = []
                    for mh in heads:
                        q_h = jnp.where(mh, q, 0.0)
                        do_h = jnp.where(mh, do, 0.0)
                        qh = q_h.astype(bf16)
                        doh = do_h.astype(bf16)
                        per_head.append((qh, doh, _dot_nt(qh, kc), _dot_nt(qh, kp), _dot((q_h * kp_f).astype(bf16), ones16),
                                         _dot_nt(doh, vc), _dot_nt(doh, vp), _dot((do_h * vp_f).astype(bf16), ones16),
                                         row_sum(jnp.where(mh, dlt, 0.0))))
                    tiles.append((rows_c, rows_p, has_prev, kc, kp, per_head))
                grads = []
                for rows_c, rows_p, has_prev, kc, kp, per_head in tiles:
                    keep = jnp.logical_or(lower, has_prev)
                    out = []
                    for h, (qh, doh, s_c, s_p, far, dp_c, dp_p, dp_far, delta) in enumerate(per_head):
                        lse_h = lse_refs[h][rows_c, :]
                        s = jnp.where(keep, jnp.where(lower, s_c, s_p) * scale + bt_ref[p, h], NEG)
                        s_far = jnp.where(has_prev, far * scale + bt_ref[p, h, SWA_BLK - 1:SWA_BLK, 0:1], NEG)
                        pm = jnp.exp(s - lse_h)
                        p_far = jnp.exp(s_far - lse_h)
                        dsm = pm * (jnp.where(lower, dp_c, dp_p) - delta)
                        ds_far = p_far * (dp_far - delta)
                        out.append((dsm + jnp.where(rel127, ds_far, 0.0),
                                    jnp.where(lower, dsm, 0.0).astype(bf16),
                                    jnp.where(lower, jnp.where(eye, ds_far, 0.0), dsm).astype(bf16),
                                    jnp.where(lower, pm, 0.0).astype(bf16),
                                    jnp.where(lower, jnp.where(eye, p_far, 0.0), pm).astype(bf16)))
                    grads.append(out)
                done = []
                for (rows_c, rows_p, _, kc, kp, per_head), out in zip(tiles, grads):
                    dq_t = dkc_t = dkp_t = dvc_t = dvp_t = None
                    add = lambda acc, t: t if acc is None else acc + t
                    for h, (_, dsc16, dsp16, pc16, pp16) in enumerate(out):
                        qh, doh = per_head[h][0], per_head[h][1]
                        dq_t = add(dq_t, jnp.where(heads[h], (_dot(dsc16, kc) + _dot(dsp16, kp)) * scale, 0.0))
                        dkc_t = add(dkc_t, _dot_tn(dsc16, qh) * scale)
                        dkp_t = add(dkp_t, _dot_tn(dsp16, qh) * scale)
                        dvc_t = add(dvc_t, _dot_tn(pc16, doh))
                        dvp_t = add(dvp_t, _dot_tn(pp16, doh))
                    done.append((rows_c, rows_p, dq_t, dkc_t, dkp_t, dvc_t, dvp_t))
                for h in range(2):
                    tot = grads[0][h][0]
                    for g in grads[1:]:
                        tot = tot + g[h][0]
                    dsb_ref[0, p, h] += tot
                for rows_c, rows_p, dq_t, dkc_t, dkp_t, dvc_t, dvp_t in done:
                    dq_scr[rows_c, :] = dq_scr[rows_c, :] + dq_t
                    dk_scr[rows_c, :] = dk_scr[rows_c, :] + dkc_t
                    dv_scr[rows_c, :] = dv_scr[rows_c, :] + dvc_t
                    dk_scr[rows_p, :] = dk_scr[rows_p, :] + dkp_t
                    dv_scr[rows_p, :] = dv_scr[rows_p, :] + dvp_t
                return carry

            lax.fori_loop(0, S // SWA_BLK // SWA_UNROLL, lp, 0)
        dq_ref[...] = dq_scr[...].astype(bf16)
        dk_ref[...] = dk_scr[...].astype(bf16)
        dv_ref[...] = dv_scr[...].astype(bf16)

    qb = COL_B // LANE
    col = lambda c: pl.BlockSpec((S, LANE), lambda hp, c=c: (0, c + hp))
    return pl.pallas_call(
        body, name="swa_bwd", grid=(4,),
        in_specs=[col(qb), col(qb + 4), col(qb + 8),
                  pl.BlockSpec((3, 2, SWA_BLK, SWA_BLK), lambda hp: (0, hp, 0, 0)),
                  col(0), col(0), col(0), col(4), _ANY],
        out_specs=[col(0), col(0), col(0),
                   pl.BlockSpec((1, 3, 2, SWA_BLK, SWA_BLK), lambda hp: (hp, 0, 0, 0, 0))],
        out_shape=[SDS((S, SWA_W), bf16)] * 3 + [SDS((4, 3, 2, SWA_BLK, SWA_BLK), f32)],
        scratch_shapes=[pltpu.VMEM((S, LANE), f32)] * 3,
        compiler_params=_cp(1, VMEM_LIMIT),
    )(proj, proj, proj, bt, ob, lse0, lse1, d_oab, after)


def _mix_fwd(oa, ob, w_out, x, g_post, S):
    TS = 512

    def body(oa_ref, ob_ref, w_ref, x_ref, g_ref, mix_ref, x1_ref):
        mix = _dot(oa_ref[...].astype(bf16), w_ref[0:GDN_W, :]) + _dot(ob_ref[...].astype(bf16), w_ref[GDN_W:D_MODEL, :])
        r = lax.rsqrt(jnp.mean(mix * mix, axis=-1, keepdims=True) + RMS_EPS)
        mix_ref[...] = mix
        x1_ref[...] = x_ref[...] + mix * r * g_ref[...]

    row = lambda w: pl.BlockSpec((TS, w), lambda i: (i, 0))
    return pl.pallas_call(
        body, name="mix_fwd", grid=(S // TS,),
        in_specs=[row(GDN_W), row(SWA_W), _resident_spec((D_MODEL, D_MODEL)), row(D_MODEL), _const_spec((1, D_MODEL))],
        out_specs=[row(D_MODEL), row(D_MODEL)],
        out_shape=[SDS((S, D_MODEL), f32), SDS((S, D_MODEL), f32)],
        compiler_params=_cp(1, VMEM_LIMIT),
    )(oa, ob, w_out, x, g_post)


def _mix_bwd(dx1, mix, g_post, w_out, S):
    TS = 512

    def body(dx1_ref, mix_ref, g_ref, w_ref, dmix_ref, doab_ref, dg_ref):
        @pl.when(pl.program_id(0) == 0)
        def _():
            dg_ref[...] = jnp.zeros_like(dg_ref)

        mix = mix_ref[...]
        dz = dx1_ref[...]
        r = lax.rsqrt(jnp.mean(mix * mix, axis=-1, keepdims=True) + RMS_EPS)
        n = mix * r
        dg_ref[...] += jnp.sum(dz * n, axis=0, keepdims=True)
        dn = dz * g_ref[...]
        dmix = (r * (dn - n * jnp.mean(dn * n, axis=-1, keepdims=True))).astype(bf16)
        dmix_ref[...] = dmix
        doab_ref[...] = _dot_nt(dmix, w_ref[...])

    row = lambda: pl.BlockSpec((TS, D_MODEL), lambda i: (i, 0))
    return pl.pallas_call(
        body, name="mix_bwd", grid=(S // TS,),
        in_specs=[row(), row(), _const_spec((1, D_MODEL)), _resident_spec((D_MODEL, D_MODEL))],
        out_specs=[row(), row(), _const_spec((1, D_MODEL))],
        out_shape=[SDS((S, D_MODEL), bf16), SDS((S, D_MODEL), f32), SDS((1, D_MODEL), f32)],
        compiler_params=_cp(1, VMEM_LIMIT),
    )(dx1, mix, g_post, w_out)


FFN_TS = 256
FFN_CH = 1408


def _ffn(x1, tgt, g_pre, g_post, wg, wu, wd, S):
    def body(x1_ref, t_ref, gp_ref, gq_ref, wg_ref, wu_ref, wd_ref,
             dx1_ref, h2_ref, act_ref, dgate_ref, dup_ref, df_ref, loss_ref, dgp_ref, dgq_ref, gate_scr, up_scr):
        @pl.when(pl.program_id(0) == 0)
        def _():
            loss_ref[...] = jnp.zeros_like(loss_ref)
            dgp_ref[...] = jnp.zeros_like(dgp_ref)
            dgq_ref[...] = jnp.zeros_like(dgq_ref)

        x1v = x1_ref[...]
        gp = gp_ref[...]
        gq = gq_ref[...]
        r2 = lax.rsqrt(jnp.mean(x1v * x1v, axis=-1, keepdims=True) + RMS_EPS)
        n2 = x1v * r2
        h2 = (n2 * gp).astype(bf16)
        h2_ref[...] = h2
        f = jnp.zeros((FFN_TS, D_MODEL), f32)
        for c in range(D_FF // FFN_CH):
            cs = slice(c * FFN_CH, (c + 1) * FFN_CH)
            gate = _dot(h2, wg_ref[:, cs])
            up = _dot(h2, wu_ref[:, cs])
            gate_scr[:, cs] = gate
            up_scr[:, cs] = up
            act = (gate * _sigmoid(gate) * up).astype(bf16)
            act_ref[:, cs] = act
            f = f + _dot(act, wd_ref[cs, :])
        r3 = lax.rsqrt(jnp.mean(f * f, axis=-1, keepdims=True) + RMS_EPS)
        n3 = f * r3
        err = x1v + n3 * gq - t_ref[...]
        loss_ref[...] += 0.5 * jnp.sum(jnp.mean(err * err, axis=-1, keepdims=True), axis=0, keepdims=True)
        dy = err * (1.0 / D_MODEL)
        dgq_ref[...] += jnp.sum(dy * n3, axis=0, keepdims=True)
        dn3 = dy * gq
        df = (r3 * (dn3 - n3 * jnp.mean(dn3 * n3, axis=-1, keepdims=True))).astype(bf16)
        df_ref[...] = df
        dh2 = jnp.zeros((FFN_TS, D_MODEL), f32)
        for c in range(D_FF // FFN_CH):
            cs = slice(c * FFN_CH, (c + 1) * FFN_CH)
            gate = gate_scr[:, cs]
            up = up_scr[:, cs]
            dact = _dot_nt(df, wd_ref[cs, :])
            sg = _sigmoid(gate)
            dup = (dact * gate * sg).astype(bf16)
            dgate = (dact * up * (sg * (1.0 + gate * (1.0 - sg)))).astype(bf16)
            dup_ref[:, cs] = dup
            dgate_ref[:, cs] = dgate
            dh2 = dh2 + _dot_nt(dgate, wg_ref[:, cs]) + _dot_nt(dup, wu_ref[:, cs])
        dgp_ref[...] += jnp.sum(dh2 * n2, axis=0, keepdims=True)
        dn2 = dh2 * gp
        dx1_ref[...] = dy + r2 * (dn2 - n2 * jnp.mean(dn2 * n2, axis=-1, keepdims=True))

    row = lambda w: pl.BlockSpec((FFN_TS, w), lambda i: (i, 0))
    vec = _const_spec((1, D_MODEL))
    return pl.pallas_call(
        body, name="ffn_fwd_bwd", grid=(S // FFN_TS,),
        in_specs=[row(D_MODEL), row(D_MODEL), vec, vec, _resident_spec((D_MODEL, D_FF)), _resident_spec((D_MODEL, D_FF)),
                  _resident_spec((D_FF, D_MODEL))],
        out_specs=[row(D_MODEL), row(D_MODEL), row(D_FF), row(D_FF), row(D_FF), row(D_MODEL), _const_spec((1, LANE)), vec, vec],
        out_shape=[SDS((S, D_MODEL), f32), SDS((S, D_MODEL), bf16), SDS((S, D_FF), bf16), SDS((S, D_FF), bf16),
                   SDS((S, D_FF), bf16), SDS((S, D_MODEL), bf16), SDS((1, LANE), f32), SDS((1, D_MODEL), f32),
                   SDS((1, D_MODEL), f32)],
        scratch_shapes=[pltpu.VMEM((FFN_TS, D_FF), f32), pltpu.VMEM((FFN_TS, D_FF), f32)],
        compiler_params=_cp(1, VMEM_LIMIT),
    )(x1, tgt, g_pre, g_post, wg, wu, wd)


def _proj_bwd(x, dx1, g_pre, wcat, segs, after, S):
    TS = 256
    n = len(segs)
    cols = [(c0, a.shape[1]) for a, c0 in segs]

    def body(*refs):
        x_ref, dx1_ref, g_ref, w_ref = refs[:4]
        seg_refs = refs[4:4 + n]
        gx_ref, dg_ref = refs[5 + n:]

        @pl.when(pl.program_id(0) == 0)
        def _():
            dg_ref[...] = jnp.zeros_like(dg_ref)

        dh = jnp.zeros((TS, D_MODEL), f32)
        for s_ref, (c0, w) in zip(seg_refs, cols):
            dh = dh + _dot_nt(s_ref[...], w_ref[:, c0:c0 + w])
        xv = x_ref[...]
        g = g_ref[...]
        r = lax.rsqrt(jnp.mean(xv * xv, axis=-1, keepdims=True) + RMS_EPS)
        nx = xv * r
        dg_ref[...] += jnp.sum(dh * nx, axis=0, keepdims=True)
        dn = dh * g
        gx_ref[...] = dx1_ref[...] + r * (dn - nx * jnp.mean(dn * nx, axis=-1, keepdims=True))

    row = lambda w: pl.BlockSpec((TS, w), lambda i: (i, 0))
    return pl.pallas_call(
        body, name="proj_bwd", grid=(S // TS,),
        in_specs=[row(D_MODEL), row(D_MODEL), _const_spec((1, D_MODEL)), _resident_spec((D_MODEL, NCOL))]
                 + [row(w) for _, w in cols] + [_ANY],
        out_specs=[row(D_MODEL), _const_spec((1, D_MODEL))],
        out_shape=[SDS((S, D_MODEL), f32), SDS((1, D_MODEL), f32)],
        compiler_params=_cp(1, VMEM_LIMIT),
    )(x, dx1, g_pre, wcat, *[a for a, _ in segs], after)


def _wgrad(a, b, S, name, col_blocks=False):
    TS = 512
    K = a.shape[1]
    N = b.shape[1]
    TN = next(t for t in (512, 1408, N) if N % t == 0)
    cb = N // N_DEV
    nblk = TN // cb if col_blocks else 0

    def body(a_ref, b_ref, o_ref, acc):
        @pl.when(pl.program_id(1) == 0)
        def _():
            acc[...] = jnp.zeros_like(acc)

        acc[...] += _dot_tn(a_ref[...].astype(bf16), b_ref[...])

        @pl.when(pl.program_id(1) == pl.num_programs(1) - 1)
        def _():
            if col_blocks:
                for i in range(nblk):
                    o_ref[i] = acc[:, i * cb:(i + 1) * cb].astype(bf16)
            else:
                o_ref[...] = acc[...].astype(bf16)

    if col_blocks:
        out_spec = pl.BlockSpec((nblk, K, cb), lambda j, s: (j, 0, 0))
        out_shape = SDS((N_DEV, K, cb), bf16)
    else:
        out_spec = pl.BlockSpec((K, TN), lambda j, s: (0, j))
        out_shape = SDS((K, N), bf16)
    return pl.pallas_call(
        body, name=name, grid=(N // TN, S // TS),
        in_specs=[pl.BlockSpec((TS, K), lambda j, s: (s, 0)), pl.BlockSpec((TS, TN), lambda j, s: (s, j))],
        out_specs=out_spec, out_shape=out_shape,
        scratch_shapes=[pltpu.VMEM((K, TN), f32)],
        compiler_params=_cp(2, VMEM_LIMIT),
    )(a, b)


def _w_in_pieces():
    n_a, n_g = 4 * GDN_W, 2 * GDN_HEADS
    cb = IN_COLS // N_DEV
    bounds = [(0, n_a, COL_A), (n_a, n_a + n_g, COL_G), (n_a + n_g, IN_COLS, COL_B)]
    out = []
    for j in range(N_DEV):
        lo, hi = j * cb, (j + 1) * cb
        for s0, s1, dst in bounds:
            a, b = max(lo, s0), min(hi, s1)
            if a < b:
                out.append((j, a - lo, b - a, dst + a - s0))
    return out


def _wcat_from_blocks(g_in):
    TR = 256
    cb = IN_COLS // N_DEV
    pieces = _w_in_pieces()

    def body(w_ref, o_ref):
        o_ref[:, COL_G:NCOL] = jnp.zeros((TR, NCOL - COL_G), bf16)
        for j, off, w, dst in pieces:
            o_ref[:, dst:dst + w] = w_ref[j, :, off:off + w]

    return pl.pallas_call(
        body, name="wcat_from_blocks", grid=(D_MODEL // TR,),
        in_specs=[pl.BlockSpec((N_DEV, TR, cb), lambda i: (0, i, 0))],
        out_specs=pl.BlockSpec((TR, NCOL), lambda i: (i, 0)),
        out_shape=SDS((D_MODEL, NCOL), bf16),
        compiler_params=_cp(1, VMEM_LIMIT),
    )(g_in)


def _full_from_col_blocks(g):
    n, R, C = g.shape
    TR = 256

    def body(w_ref, o_ref):
        for j in range(n):
            o_ref[:, j * C:(j + 1) * C] = w_ref[j]

    return pl.pallas_call(
        body, name="full_from_col_blocks", grid=(R // TR,),
        in_specs=[pl.BlockSpec((n, TR, C), lambda i: (0, i, 0))],
        out_specs=pl.BlockSpec((TR, n * C), lambda i: (i, 0)),
        out_shape=SDS((R, n * C), g.dtype),
        compiler_params=_cp(1, VMEM_LIMIT),
    )(g)


def _wgrad_in(h1, segs, S):
    TS = 512
    n = len(segs)
    cols = [(c0, a.shape[1]) for a, c0 in segs]
    cb = IN_COLS // N_DEV
    pieces = _w_in_pieces()

    def body(*refs):
        h_ref = refs[0]
        seg_refs = refs[1:1 + n]
        o_ref, acc = refs[1 + n], refs[2 + n]

        @pl.when(pl.program_id(0) == 0)
        def _():
            acc[...] = jnp.zeros_like(acc)

        h = h_ref[...]
        for s_ref, (c0, w) in zip(seg_refs, cols):
            acc[:, c0:c0 + w] += _dot_tn(h, s_ref[...])

        @pl.when(pl.program_id(0) == pl.num_programs(0) - 1)
        def _():
            for j, off, w, src in pieces:
                o_ref[j, :, off:off + w] = acc[:, src:src + w].astype(bf16)

    row = lambda w: pl.BlockSpec((TS, w), lambda i: (i, 0))
    return pl.pallas_call(
        body, name="wgrad_in", grid=(S // TS,),
        in_specs=[row(D_MODEL)] + [row(w) for _, w in cols],
        out_specs=_const_spec((N_DEV, D_MODEL, cb)),
        out_shape=SDS((N_DEV, D_MODEL, cb), bf16),
        scratch_shapes=[pltpu.VMEM((D_MODEL, NCOL), f32)],
        compiler_params=_cp(1, VMEM_LIMIT),
    )(h1, *[a for a, _ in segs])


def _adamw(recv, w, m, v, name):
    R, C = w.shape
    TR = 256 if R % 256 == 0 else R
    c1 = 1.0 / (1.0 - ADAM_B1 ** ADAM_STEP)
    c2 = 1.0 / (1.0 - ADAM_B2 ** ADAM_STEP)

    def body(r_ref, w_ref, m_ref, v_ref, g_out, d_out, m_out, v_out):
        g = r_ref[0].astype(f32)
        for s in range(1, N_DEV):
            g = g + r_ref[s].astype(f32)
        mn = ADAM_B1 * m_ref[...] + (1.0 - ADAM_B1) * g
        vn = ADAM_B2 * v_ref[...] + (1.0 - ADAM_B2) * (g * g)
        g_out[...] = g
        m_out[...] = mn
        v_out[...] = vn
        d_out[...] = -ADAM_LR * ((mn * c1) / (jnp.sqrt(vn * c2) + ADAM_EPS) + ADAM_WD * w_ref[...])

    blk = pl.BlockSpec((TR, C), lambda i: (i, 0))
    return pl.pallas_call(
        body, name=name, grid=(R // TR,),
        in_specs=[pl.BlockSpec((N_DEV, TR, C), lambda i: (0, i, 0)), blk, blk, blk],
        out_specs=[blk, blk, blk, blk],
        out_shape=[SDS((R, C), f32)] * 4,
        compiler_params=_cp(1, VMEM_LIMIT),
    )(recv, w, m, v)


MESH = pl.DeviceIdType.MESH
_ANY = pl.BlockSpec(memory_space=pl.ANY)


def _flip(v, d):
    return 1 - v if d else v


def _all_gather(shards):
    n = len(shards)

    def body(*refs):
        ins = refs[:n]
        outs = refs[n:2 * n]
        send_sems, recv_sems, local_sems = refs[2 * n:]
        x, y, c = lax.axis_index("x"), lax.axis_index("y"), lax.axis_index("c")
        me, sibling = (x, y, c), (x, y, 1 - c)
        chips = [(1 - x, y), (x, 1 - y), (1 - x, 1 - y)]

        def slot(px, py, pc):
            return 4 * px + 2 * py + pc

        def copy(a, k, block, to, src=None):
            dst = outs[a].at[slot(*block)]
            return pltpu.make_async_remote_copy(src_ref=dst if src is None else src, dst_ref=dst,
                                                send_sem=send_sems.at[a, k], recv_sem=recv_sems.at[a, k],
                                                device_id=to, device_id_type=MESH)

        mine, first, passed = [], [], []
        for a in range(n):
            cp = pltpu.make_async_copy(ins[a], outs[a].at[slot(*me)], local_sems.at[a])
            cp.start()
            mine.append(cp)
            fs = [copy(a, 0, me, sibling, src=ins[a])]
            fs += [copy(a, 1 + j, me, (*chip, c), src=ins[a]) for j, chip in enumerate(chips)]
            for cp in fs:
                cp.start()
            first += fs
        for j, chip in enumerate(chips):
            for a in range(n):
                copy(a, 1 + j, (*chip, c), me).wait_recv()
                cp = copy(a, 4 + j, (*chip, c), sibling)
                cp.start()
                passed.append(cp)
        for a in range(n):
            copy(a, 0, sibling, me).wait_recv()
            for j, chip in enumerate(chips):
                copy(a, 4 + j, (*chip, 1 - c), me).wait_recv()
        for cp in first + passed:
            cp.wait_send()
        for cp in mine:
            cp.wait()

    return pl.pallas_call(
        body, name="weight_all_gather",
        in_specs=[_ANY] * n, out_specs=[_ANY] * n,
        out_shape=[SDS((N_DEV,) + s.shape, s.dtype) for s in shards],
        scratch_shapes=[pltpu.SemaphoreType.DMA((n, 7)), pltpu.SemaphoreType.DMA((n, 7)), pltpu.SemaphoreType.DMA((n,))],
        compiler_params=pltpu.CompilerParams(has_side_effects=True),
    )(*shards)


def _grad_exchange(blocked, whole):
    arrs = list(blocked) + list(whole)
    n, nb = len(arrs), len(blocked)
    rel = [(dx, dy, dc) for dx in (0, 1) for dy in (0, 1) for dc in (0, 1) if dx or dy or dc]

    def body(*refs):
        ins = refs[:n]
        outs = refs[n:2 * n]
        send_sems, recv_sems, local_sems = refs[2 * n:]
        x, y, c = lax.axis_index("x"), lax.axis_index("y"), lax.axis_index("c")
        me = 4 * x + 2 * y + c
        sends, locs = [], []
        for a in range(n):
            cp = pltpu.make_async_copy(ins[a].at[me] if a < nb else ins[a], outs[a].at[me], local_sems.at[a])
            cp.start()
            locs.append(cp)
            for k, (dx, dy, dc) in enumerate(rel):
                peer = (_flip(x, dx), _flip(y, dy), _flip(c, dc))
                pidx = 4 * peer[0] + 2 * peer[1] + peer[2]
                cp = pltpu.make_async_remote_copy(src_ref=ins[a].at[pidx] if a < nb else ins[a], dst_ref=outs[a].at[me],
                                                  send_sem=send_sems.at[a, k], recv_sem=recv_sems.at[a, k],
                                                  device_id=peer, device_id_type=MESH)
                cp.start()
                sends.append(cp)
        for a in range(n):
            for k, (dx, dy, dc) in enumerate(rel):
                peer = (_flip(x, dx), _flip(y, dy), _flip(c, dc))
                pidx = 4 * peer[0] + 2 * peer[1] + peer[2]
                pltpu.make_async_remote_copy(src_ref=outs[a].at[pidx], dst_ref=outs[a].at[pidx],
                                             send_sem=send_sems.at[a, k], recv_sem=recv_sems.at[a, k],
                                             device_id=peer, device_id_type=MESH).wait_recv()
        for cp in sends:
            cp.wait_send()
        for cp in locs:
            cp.wait()

    shapes = [SDS(a.shape, a.dtype) for a in blocked] + [SDS((N_DEV,) + a.shape, a.dtype) for a in whole]
    return pl.pallas_call(
        body, name="grad_exchange",
        in_specs=[_ANY] * n, out_specs=[_ANY] * n, out_shape=shapes,
        scratch_shapes=[pltpu.SemaphoreType.DMA((n, 7)), pltpu.SemaphoreType.DMA((n, 7)), pltpu.SemaphoreType.DMA((n,))],
        compiler_params=pltpu.CompilerParams(has_side_effects=True),
    )(*arrs)


_HBM = pl.BlockSpec(memory_space=pltpu.HBM)
_SEM = pl.BlockSpec(memory_space=pltpu.SEMAPHORE)
_REL = [(dx, dy, dc) for dx in (0, 1) for dy in (0, 1) for dc in (0, 1) if dx or dy or dc]


N_PEER = len(_REL)
_EFFECT = pltpu.SideEffectType.DATAFLOW_SIDE_EFFECTING


def _peer_copies(srcs, lands, send_sems, recv_sems, blocked, as_receiver):
    x, y, c = lax.axis_index("x"), lax.axis_index("y"), lax.axis_index("c")
    me = 4 * x + 2 * y + c
    cps = []
    for a in range(len(srcs)):
        for k, (dx, dy, dc) in enumerate(_REL):
            peer = (_flip(x, dx), _flip(y, dy), _flip(c, dc))
            pidx = 4 * peer[0] + 2 * peer[1] + peer[2]
            cps.append(pltpu.make_async_remote_copy(
                src_ref=srcs[a].at[pidx] if blocked else srcs[a], dst_ref=lands[a].at[pidx if as_receiver else me],
                send_sem=send_sems[a * N_PEER + k], recv_sem=recv_sems[a * N_PEER + k],
                device_id=peer, device_id_type=MESH))
    return cps


def _exchange_start(srcs, after, blocked, name):
    n = len(srcs)
    ns = n * N_PEER
    lands = [lax.empty(s.shape if blocked else (N_DEV,) + s.shape, s.dtype) for s in srcs]

    def body(*refs):
        ins, lnd = refs[:n], refs[n:2 * n]
        outs = refs[2 * n + 1:]
        for cp in _peer_copies(ins, lnd, outs[:ns], outs[ns:2 * ns], blocked, False):
            cp.start()
        outs[-1][...] = jnp.zeros_like(outs[-1])

    res = pl.pallas_call(
        body, name=name,
        in_specs=[_HBM] * (2 * n) + [_ANY],
        out_specs=[_SEM] * (2 * ns) + [_HBM] * (2 * n) + [pl.BlockSpec(memory_space=pltpu.VMEM)],
        out_shape=[pltpu.SemaphoreType.DMA(())] * (2 * ns) + [pltpu.HBM(s.shape, s.dtype) for s in srcs]
                  + [pltpu.HBM(l.shape, l.dtype) for l in lands] + [SDS((8, LANE), f32)],
        input_output_aliases={i: 2 * ns + i for i in range(2 * n)},
        compiler_params=pltpu.CompilerParams(has_side_effects=_EFFECT),
    )(*[pltpu.with_memory_space_constraint(s, pltpu.HBM) for s in srcs],
      *[pltpu.with_memory_space_constraint(l, pltpu.HBM) for l in lands], after)
    return list(res[:2 * ns]), list(res[2 * ns:2 * ns + n]), list(res[2 * ns + n:2 * ns + 2 * n]), res[-1]


def _exchange_wait(sems, srcs, lands, after, blocked, name):
    n = len(srcs)
    ns = n * N_PEER

    def body(*refs):
        ins, lnd = refs[:n], refs[n:2 * n]
        sem_refs = refs[2 * n:2 * n + 2 * ns]
        for cp in _peer_copies(ins, lnd, sem_refs[:ns], sem_refs[ns:], blocked, True):
            cp.wait_send()
            cp.wait_recv()

    res = pl.pallas_call(
        body, name=name,
        in_specs=[_HBM] * (2 * n) + [_SEM] * (2 * ns) + [_ANY],
        out_specs=[_HBM] * (2 * n),
        out_shape=[pltpu.HBM(s.shape, s.dtype) for s in srcs] + [pltpu.HBM(l.shape, l.dtype) for l in lands],
        input_output_aliases={i: i for i in range(2 * n)},
        compiler_params=pltpu.CompilerParams(has_side_effects=_EFFECT),
    )(*srcs, *lands, *sems, after)
    return list(res[n:])


def _local_step(x, tgt, wcat, convw, late_weights, early_grads, last_grads, token, a_log, dt_bias, onorm_g, rel_bias,
                g_mix_pre, g_mix_post, g_ffn_pre, g_ffn_post):
    S = x.shape[0]
    bk_np = _bucket_tables()
    bk = jnp.asarray(bk_np)
    bt = _bias_tables(rel_bias, bk)
    proj, h1 = _proj_fwd(x, g_mix_pre, wcat, token, S)
    nu = S // CHUNK * GDN_HEADS
    qkv_u = _gdn_prep(proj, convw, S).reshape(3, nu, CHUNK, GDN_HD)
    intra, t_inv = _gdn_intra_fwd(qkv_u, proj, a_log, dt_bias, S)
    oa, states = _gdn_scan_fwd(intra, proj, onorm_g, S)
    ob, lse0, lse1 = _swa_fwd(proj, bt, S)
    wout, wgate, wup, wdown = late_weights(ob)
    mix, x1 = _mix_fwd(oa, ob, wout, x, g_mix_post, S)
    dx1, h2, act, dgate_f, dup_f, df, loss, d_gfpre, d_gfpost = _ffn(x1, tgt, g_ffn_pre, g_ffn_post, wgate, wup, wdown, S)
    g_gate = _wgrad(h2, dgate_f, S, "wgrad_gate", col_blocks=True)
    g_up = _wgrad(h2, dup_f, S, "wgrad_up", col_blocks=True)
    g_down = _wgrad(act, df, S, "wgrad_down").reshape(N_DEV, D_FF // N_DEV, D_MODEL)
    dmix, d_oab, d_gmpost = _mix_bwd(dx1, mix, g_mix_post, wout, S)
    g_out = jnp.concatenate([_wgrad(oa, dmix, S, "wgrad_out_a"), _wgrad(ob, dmix, S, "wgrad_out_b")], axis=0)
    token = early_grads(g_out.reshape(N_DEV, D_MODEL // N_DEV, D_MODEL), g_gate, g_up, g_down)
    dqb, dkb, dvb, dsb = _swa_bwd(proj, bt, ob, lse0, lse1, d_oab, token, S)
    *cots, dgate_a, d_og = _gdn_scan_bwd(intra, states, proj, d_oab, onorm_g, token, S)
    dqkv_u, dpg, d_alog, d_dtb = _gdn_intra_bwd(qkv_u, proj, a_log, dt_bias, t_inv, cots, S)
    dqkv_a, d_conv = _gdn_prep_bwd(proj, convw, dqkv_u.reshape(3, S // CHUNK, GDN_HEADS, CHUNK, GDN_HD), S)
    segs = [(dqkv_a, COL_A), (dgate_a, COL_A + 3 * GDN_W), (dqb, COL_B), (dkb, COL_B + SWA_W), (dvb, COL_B + 2 * SWA_W),
            (dpg, COL_G)]
    token = last_grads(_wgrad_in(h1, segs, S), d_conv)
    grad_x, d_gmpre = _proj_bwd(x, dx1, g_mix_pre, wcat, segs, token, S)
    d_rel = _rel_bias_grad(dsb, bk, bk_np)
    small = dict(a_log=d_alog[:, :GDN_HEADS], dt_bias=d_dtb[:, :GDN_HEADS], onorm_g=d_og, rel_bias=d_rel,
                 g_mix_pre=d_gmpre, g_mix_post=d_gmpost, g_ffn_pre=d_gfpre, g_ffn_post=d_gfpost)
    return loss, grad_x, small


SMALL = ("a_log", "dt_bias", "onorm_g", "rel_bias", "g_mix_pre", "g_mix_post", "g_ffn_pre", "g_ffn_post")
PACK_ROWS = 8


def _pack_small(d, loss=None):
    rest = jnp.concatenate([d["onorm_g"].reshape(-1), d["a_log"].reshape(-1), d["dt_bias"].reshape(-1),
                            d["rel_bias"].reshape(-1)])
    rest = jnp.concatenate([rest, jnp.zeros((D_MODEL - rest.shape[0],), f32)])
    extra = jnp.zeros((D_MODEL,), f32) if loss is None else jnp.concatenate([loss.reshape(1), jnp.zeros((D_MODEL - 1,), f32)])
    rows = [d["g_mix_pre"].reshape(-1), d["g_mix_post"].reshape(-1), d["g_ffn_pre"].reshape(-1),
            d["g_ffn_post"].reshape(-1), rest, extra]
    return jnp.concatenate([jnp.stack(rows), jnp.zeros((PACK_ROWS - len(rows), D_MODEL), f32)], axis=0)


def _unpack_small(p):
    o = GDN_HD
    return dict(g_mix_pre=p[0:1], g_mix_post=p[1:2], g_ffn_pre=p[2:3], g_ffn_post=p[3:4],
                onorm_g=p[4:5, :o], a_log=p[4:5, o:o + 4], dt_bias=p[4:5, o + 4:o + 8],
                rel_bias=p[4, o + 8:o + 8 + NUM_BUCKETS * SWA_HEADS].reshape(NUM_BUCKETS, SWA_HEADS))


def kernel(x, w_in, conv_w, a_log, dt_bias, onorm_g, rel_bias, w_out, g_mix_pre, g_mix_post, w_gate, w_up, w_down, g_ffn_pre, g_ffn_post, loss_target, m_w_in, m_conv_w, m_a_log, m_dt_bias, m_onorm_g, m_rel_bias, m_w_out, m_g_mix_pre, m_g_mix_post, m_w_gate, m_w_up, m_w_down, m_g_ffn_pre, m_g_ffn_post, v_w_in, v_conv_w, v_a_log, v_dt_bias, v_onorm_g, v_rel_bias, v_w_out, v_g_mix_pre, v_g_mix_post, v_w_gate, v_w_up, v_w_down, v_g_ffn_pre, v_g_ffn_post):
    big = ("w_in", "conv_w", "w_out", "w_gate", "w_up", "w_down")
    w_sh = dict(w_in=w_in[0], conv_w=conv_w[0], w_out=w_out[0], w_gate=w_gate[0], w_up=w_up[0], w_down=w_down[0])
    m_sh = dict(w_in=m_w_in[0], conv_w=m_conv_w[0], w_out=m_w_out[0], w_gate=m_w_gate[0], w_up=m_w_up[0], w_down=m_w_down[0])
    v_sh = dict(w_in=v_w_in[0], conv_w=v_conv_w[0], w_out=v_w_out[0], w_gate=v_w_gate[0], w_up=v_w_up[0], w_down=v_w_down[0])
    w_small = dict(a_log=a_log, dt_bias=dt_bias, onorm_g=onorm_g, rel_bias=rel_bias, g_mix_pre=g_mix_pre,
                   g_mix_post=g_mix_post, g_ffn_pre=g_ffn_pre, g_ffn_post=g_ffn_post)
    m_small = dict(a_log=m_a_log, dt_bias=m_dt_bias, onorm_g=m_onorm_g, rel_bias=m_rel_bias, g_mix_pre=m_g_mix_pre,
                   g_mix_post=m_g_mix_post, g_ffn_pre=m_g_ffn_pre, g_ffn_post=m_g_ffn_post)
    v_small = dict(a_log=v_a_log, dt_bias=v_dt_bias, onorm_g=v_onorm_g, rel_bias=v_rel_bias, g_mix_pre=v_g_mix_pre,
                   g_mix_post=v_g_mix_post, g_ffn_pre=v_g_ffn_pre, g_ffn_post=v_g_ffn_post)

    me = 4 * lax.axis_index("x") + 2 * lax.axis_index("y") + lax.axis_index("c")
    own = lambda full, part: lax.dynamic_update_index_in_dim(full, part, me, 0)
    cols = lambda g: g.reshape(g.shape[0], N_DEV, g.shape[1] // N_DEV).transpose(1, 0, 2)
    late = ("w_out", "w_gate", "w_up", "w_down")

    late_src = [w_sh[k].astype(bf16) for k in late]
    g_in, g_conv = _all_gather([w_sh["w_in"].astype(bf16), w_sh["conv_w"]])
    g_sems, g_src, g_land, g_token = _exchange_start(late_src, g_conv, False, "late_weights_start")
    wcat = _wcat_from_blocks(g_in)
    convw = g_conv.transpose(1, 0, 2).reshape(4, 3 * GDN_W)

    def late_weights(after):
        lands = _exchange_wait(g_sems, g_src, g_land, after, False, "late_weights_wait")
        g_out, g_gate, g_up, g_down = [own(l, s) for l, s in zip(lands, late_src)]
        return (g_out.reshape(D_MODEL, D_MODEL), _full_from_col_blocks(g_gate), _full_from_col_blocks(g_up),
                g_down.reshape(D_FF, D_MODEL))

    early, last = {}, {}

    def early_grads(*blocks):
        early["sems"], early["src"], early["land"], token = _exchange_start(list(blocks), blocks[0], True, "late_grads_start")
        return token

    def last_grads(gw_in, gw_conv):
        src = [gw_in, cols(gw_conv)]
        last["sems"], last["src"], last["land"], token = _exchange_start(src, gw_in, True, "last_grads_start")
        return token

    loss_p, grad_x, gsmall = _local_step(
        x[0], loss_target[0], wcat, convw, late_weights, early_grads, last_grads, g_token,
        a_log, dt_bias, onorm_g, rel_bias, g_mix_pre, g_mix_post, g_ffn_pre, g_ffn_post)

    (r_small,) = _grad_exchange([], [_pack_small(gsmall, loss_p[0, 0])])
    recv = {}
    for names, ex, after, name in ((late, early, grad_x, "late_grads_wait"), (("w_in", "conv_w"), last, r_small, "last_grads_wait")):
        lands = _exchange_wait(ex["sems"], ex["src"], ex["land"], after, True, name)
        for k, l, s in zip(names, lands, ex["src"]):
            recv[k] = own(l, lax.dynamic_index_in_dim(s, me, 0, keepdims=False))

    outs = {}
    for k in big:
        outs[k] = _adamw(recv[k], w_sh[k], m_sh[k], v_sh[k], "adamw_" + k)
    sm = _adamw(r_small, _pack_small(w_small), _pack_small(m_small), _pack_small(v_small), "adamw_small")
    loss = sm[0][5, 0]
    sm = [_unpack_small(t) for t in sm]
    for k in SMALL:
        outs[k] = tuple(t[k].reshape(w_small[k].shape) for t in sm)

    order = ("w_in", "conv_w", "a_log", "dt_bias", "onorm_g", "rel_bias", "w_out", "g_mix_pre", "g_mix_post", "w_gate",
             "w_up", "w_down", "g_ffn_pre", "g_ffn_post")
    lead = lambda k, t: t[None] if k in big else t
    res = [loss, grad_x[None]]
    for i in range(4):
        res += [lead(k, outs[k][i]) for k in order]
    return tuple(res)
```

```python
import functools
import math

import numpy as np
import jax
import jax.numpy as jnp
from jax import lax
from jax.experimental import pallas as pl
from jax.experimental.pallas import tpu as pltpu

f32 = jnp.float32
bf16 = jnp.bfloat16
SDS = jax.ShapeDtypeStruct

D_MODEL = 1024
GDN_HEADS = 4
GDN_HD = 128
GDN_W = 512
CHUNK = 64
SWA_HEADS = 8
SWA_HD = 64
SWA_W = 512
D_FF = 2816
IN_COLS = 3592
PATTERNS = ((128, 1), (512, 4), (2048, 16))
SWA_BLK = 128
NUM_BUCKETS = 32
MAX_DISTANCE = 2048
RMS_EPS = 1e-6
NEG = -1e30
N_DEV = 8

COL_A = 0
COL_B = 2048
COL_G = 3584
NCOL = 3712
LANE = 128

ADAM_LR, ADAM_B1, ADAM_B2, ADAM_EPS, ADAM_WD, ADAM_STEP = 0.001, 0.9, 0.999, 1e-08, 0.01, 10

VMEM_LIMIT = 56 * 1024 * 1024

HI = lax.Precision.HIGHEST
HIGH = lax.Precision.HIGH


def _cp(n_grid=0, vmem=None):
    kw = {}
    if n_grid:
        kw["dimension_semantics"] = ("arbitrary",) * n_grid
    if vmem:
        kw["vmem_limit_bytes"] = vmem
    return pltpu.CompilerParams(**kw)


def _dot(a, b):
    return jnp.dot(a, b, preferred_element_type=f32)


def _dot_nt(a, b):
    return lax.dot_general(a, b, (((1,), (1,)), ((), ())), preferred_element_type=f32)


def _dot_tn(a, b):
    return lax.dot_general(a, b, (((0,), (0,)), ((), ())), preferred_element_type=f32)


def _dot_hi(a, b):
    return jnp.dot(a, b, precision=HI, preferred_element_type=f32)


def _sigmoid(x):
    return 1.0 / (1.0 + jnp.exp(-x))


def _softplus(x):
    return jnp.maximum(x, 0.0) + jnp.log(1.0 + jnp.exp(-jnp.abs(x)))


def _const_spec(shape):
    nd = len(shape)
    return pl.BlockSpec(shape, lambda *_: (0,) * nd)


def _resident_spec(shape):
    nd = len(shape)
    return pl.BlockSpec(shape, lambda *_: (0,) * nd, pipeline_mode=pl.Buffered(1))


def _t5_bucket_np(dist):
    max_exact = NUM_BUCKETS // 2
    d = np.maximum(dist, 1).astype(np.float32)
    log_b = max_exact + (np.log(d / np.float32(max_exact)) / np.float32(math.log(MAX_DISTANCE / max_exact))
                         * np.float32(NUM_BUCKETS - max_exact)).astype(np.int32)
    return np.where(dist < max_exact, dist, np.minimum(log_b, NUM_BUCKETS - 1)).astype(np.int32)


def _bucket_tables():
    w = SWA_BLK
    qi = np.arange(w)[:, None]
    kj = np.arange(w)[None, :]
    rel = np.where(kj <= qi, qi - kj, qi + w - kj)
    out = np.zeros((len(PATTERNS), w, w), np.int32)
    for p, (_, dil) in enumerate(PATTERNS):
        steps = _t5_bucket_np(np.arange(w + 1) * dil)
        assert steps[w] == steps[w - 1]
        out[p] = steps[rel]
    return out


def _bias_tables(rel_bias, bk):
    def body(rb_ref, bk_ref, o_ref):
        b_idx = bk_ref[0]
        for h in range(SWA_HEADS):
            def lp(b, acc):
                return jnp.where(b_idx == b, rb_ref[b, h], acc)
            o_ref[0, h] = lax.fori_loop(0, NUM_BUCKETS, lp, jnp.zeros((SWA_BLK, SWA_BLK), f32))

    return pl.pallas_call(
        body, name="bias_tables", grid=(3,),
        in_specs=[pl.BlockSpec(memory_space=pltpu.SMEM), pl.BlockSpec((1, SWA_BLK, SWA_BLK), lambda p: (p, 0, 0))],
        out_specs=pl.BlockSpec((1, SWA_HEADS, SWA_BLK, SWA_BLK), lambda p: (p, 0, 0, 0)),
        out_shape=SDS((3, SWA_HEADS, SWA_BLK, SWA_BLK), f32),
        compiler_params=_cp(1),
    )(rel_bias, bk)


def _rel_bias_grad(dsb, bk, bk_np):
    present = [sorted(set(int(v) for v in np.unique(bk_np[p]))) for p in range(3)]

    def body(ds_ref, bk_ref, o_ref):
        row = lax.broadcasted_iota(jnp.int32, (NUM_BUCKETS, LANE), 0)
        col = lax.broadcasted_iota(jnp.int32, (NUM_BUCKETS, SWA_HEADS), 1)
        out = jnp.zeros((NUM_BUCKETS, SWA_HEADS), f32)
        for hp in range(4):
            for hh in range(2):
                acc = jnp.zeros((NUM_BUCKETS, LANE), f32)
                for p in range(3):
                    tile = ds_ref[hp, p, hh]
                    b_idx = bk_ref[p]
                    for b in present[p]:
                        part = jnp.sum(jnp.where(b_idx == b, tile, 0.0), axis=0, keepdims=True)
                        acc = acc + jnp.where(row == b, part, 0.0)
                tot = jnp.sum(acc, axis=1, keepdims=True)
                out = out + jnp.where(col == 2 * hp + hh, tot, 0.0)
        o_ref[...] = out

    return pl.pallas_call(body, name="rel_bias_grad", out_shape=SDS((NUM_BUCKETS, SWA_HEADS), f32),
                          compiler_params=_cp(0, 32 * 1024 * 1024))(dsb, bk)


def _proj_fwd(x, g_pre, wcat, after, S):
    TS = 256

    def body(x_ref, g_ref, w_ref, after_ref, o_ref, h_ref):
        xv = x_ref[...]
        r = lax.rsqrt(jnp.mean(xv * xv, axis=-1, keepdims=True) + RMS_EPS)
        h = (xv * r * g_ref[...]).astype(bf16)
        h_ref[...] = h
        o_ref[...] = _dot(h, w_ref[...])

    return pl.pallas_call(
        body, name="proj_fwd", grid=(S // TS,),
        in_specs=[pl.BlockSpec((TS, D_MODEL), lambda i: (i, 0)), _const_spec((1, D_MODEL)),
                  _resident_spec((D_MODEL, NCOL)), _ANY],
        out_specs=[pl.BlockSpec((TS, NCOL), lambda i: (i, 0)), pl.BlockSpec((TS, D_MODEL), lambda i: (i, 0))],
        out_shape=[SDS((S, NCOL), f32), SDS((S, D_MODEL), bf16)],
        compiler_params=_cp(1, VMEM_LIMIT),
    )(x, g_pre, wcat, after)


CONV_RT = 256
HALO = 8


CONV_NC = CONV_RT // CHUNK


def _gdn_prep(proj, conv_w, S):
    def body(p_ref, cw_ref, o_ref, xs_ref):
        t = pl.program_id(0)
        xs_ref[pl.ds(0, HALO), :] = jnp.zeros((HALO, LANE), f32)
        xs_ref[pl.ds(HALO, S), :] = p_ref[...]
        w = cw_ref[...]
        is_qk = t < 2
        scale = jnp.where(t == 0, GDN_HD ** -0.5, 1.0).astype(f32)

        def lp(c, carry):
            st = pl.multiple_of(c * CONV_RT, CONV_RT)
            pre = xs_ref[pl.ds(st + HALO - 3, CONV_RT), :] * w[0:1, :]
            for i in range(1, 4):
                pre = pre + xs_ref[pl.ds(st + HALO - 3 + i, CONV_RT), :] * w[i:i + 1, :]
            s = pre * _sigmoid(pre)
            nrm = s * lax.rsqrt(jnp.sum(s * s, axis=-1, keepdims=True) + 1e-6) * scale
            out = jnp.where(is_qk, nrm, s)
            for i in range(CONV_NC):
                o_ref[0, c * CONV_NC + i, 0] = out[i * CHUNK:(i + 1) * CHUNK]
            return carry

        lax.fori_loop(0, S // CONV_RT, lp, 0)

    return pl.pallas_call(
        body, name="gdn_prep", grid=(3, GDN_HEADS),
        in_specs=[pl.BlockSpec((S, LANE), lambda t, h: (0, t * GDN_HEADS + h)),
                  pl.BlockSpec((4, LANE), lambda t, h: (0, t * GDN_HEADS + h))],
        out_specs=pl.BlockSpec((1, S // CHUNK, 1, CHUNK, GDN_HD), lambda t, h: (t, 0, h, 0, 0)),
        out_shape=SDS((3, S // CHUNK, GDN_HEADS, CHUNK, GDN_HD), f32),
        scratch_shapes=[pltpu.VMEM((S + HALO, LANE), f32)],
        compiler_params=_cp(2, VMEM_LIMIT),
    )(proj, conv_w)


def _gdn_prep_bwd(proj, conv_w, dqkv, S):
    def body(p_ref, cw_ref, d_ref, dx_ref, dw_ref, xs_ref, dp_ref):
        t = pl.program_id(0)
        xs_ref[pl.ds(0, HALO), :] = jnp.zeros((HALO, LANE), f32)
        xs_ref[pl.ds(HALO, S), :] = p_ref[...]
        dp_ref[pl.ds(S, HALO), :] = jnp.zeros((HALO, LANE), f32)
        w = cw_ref[...]
        is_qk = t < 2
        scale = jnp.where(t == 0, GDN_HD ** -0.5, 1.0).astype(f32)

        def lp1(c, dw):
            st = pl.multiple_of(c * CONV_RT, CONV_RT)
            taps = [xs_ref[pl.ds(st + HALO - 3 + i, CONV_RT), :] for i in range(4)]
            pre = taps[0] * w[0:1, :]
            for i in range(1, 4):
                pre = pre + taps[i] * w[i:i + 1, :]
            sg = _sigmoid(pre)
            s = pre * sg
            d_out = jnp.concatenate([d_ref[0, c * CONV_NC + i, 0] for i in range(CONV_NC)], axis=0)
            rn = lax.rsqrt(jnp.sum(s * s, axis=-1, keepdims=True) + 1e-6)
            n = s * rn
            dn = d_out * scale
            ds_qk = rn * (dn - n * jnp.sum(dn * n, axis=-1, keepdims=True))
            ds = jnp.where(is_qk, ds_qk, d_out)
            dpre = ds * (sg * (1.0 + pre * (1.0 - sg)))
            dp_ref[pl.ds(st, CONV_RT), :] = dpre
            return tuple(dw[i] + jnp.sum(dpre * taps[i], axis=0, keepdims=True) for i in range(4))

        z = jnp.zeros((1, LANE), f32)
        dw = lax.fori_loop(0, S // CONV_RT, lp1, (z, z, z, z))
        for i in range(4):
            dw_ref[pl.ds(i, 1), :] = dw[i]

        def lp2(c, carry):
            st = pl.multiple_of(c * CONV_RT, CONV_RT)
            dx = dp_ref[pl.ds(st, CONV_RT), :] * w[3:4, :]
            for i in range(3):
                dx = dx + dp_ref[pl.ds(st + 3 - i, CONV_RT), :] * w[i:i + 1, :]
            dx_ref[pl.ds(st, CONV_RT), :] = dx.astype(bf16)
            return carry

        lax.fori_loop(0, S // CONV_RT, lp2, 0)

    col = lambda rows: pl.BlockSpec((rows, LANE), lambda t, h: (0, t * GDN_HEADS + h))
    return pl.pallas_call(
        body, name="gdn_prep_bwd", grid=(3, GDN_HEADS),
        in_specs=[col(S), col(4), pl.BlockSpec((1, S // CHUNK, 1, CHUNK, GDN_HD), lambda t, h: (t, 0, h, 0, 0))],
        out_specs=[col(S), col(4)],
        out_shape=[SDS((S, 3 * GDN_W), bf16), SDS((4, 3 * GDN_W), f32)],
        scratch_shapes=[pltpu.VMEM((S + HALO, LANE), f32), pltpu.VMEM((S + HALO, LANE), f32)],
        compiler_params=_cp(2, VMEM_LIMIT),
    )(proj, conv_w, dqkv)


def _bdot(a, b, prec=None):
    return lax.dot_general(a, b, (((2,), (1,)), ((0,), (0,))), precision=prec, preferred_element_type=f32)


def _bdot_nt(a, b, prec=None):
    return lax.dot_general(a, b, (((2,), (2,)), ((0,), (0,))), precision=prec, preferred_element_type=f32)


def _bdot_tn(a, b, prec=None):
    return lax.dot_general(a, b, (((1,), (1,)), ((0,), (0,))), precision=prec, preferred_element_type=f32)


@jax.custom_vjp
def _tri_inv_saved(a, t):
    return t


def _tri_inv_saved_fwd(a, t):
    return t, t


def _tri_inv_saved_bwd(t, dt):
    return -_bdot_tn(t, _bdot_nt(dt, t, HIGH), HIGH), jnp.zeros_like(t)


_tri_inv_saved.defvjp(_tri_inv_saved_fwd, _tri_inv_saved_bwd)


def _gdn_intra(q, k, v, bl, al, a_log, dt_bias, t_saved=None):
    nb = q.shape[0]
    c = CHUNK
    ii = lax.broadcasted_iota(jnp.int32, (c, c), 0)
    jj = lax.broadcasted_iota(jnp.int32, (c, c), 1)
    eye = ii == jj
    tril = ii >= jj
    strict = ii > jj
    ones = jnp.ones((nb, c, c), f32)
    eye_f = eye.astype(f32)

    beta = _sigmoid(bl)
    g = -jnp.exp(a_log) * _softplus(al + dt_bias)
    g_row = _bdot(ones, jnp.where(eye, g, 0.0), HI)
    gc = jnp.sum(jnp.where(tril, g_row, 0.0), axis=2, keepdims=True)
    gc_row = _bdot(ones, jnp.where(eye, gc, 0.0), HI)
    decay = jnp.where(tril, jnp.exp(jnp.where(tril, gc - gc_row, 0.0)), 0.0)
    last = lax.broadcasted_iota(jnp.int32, (c, 1), 0) == c - 1
    gc_last = jnp.sum(jnp.where(last, gc, 0.0), axis=1, keepdims=True)
    e_gc = jnp.exp(gc)

    kb = k * beta
    k16 = k.astype(bf16)
    a = jnp.where(strict, _bdot_nt(kb.astype(bf16), k16) * decay, 0.0)
    if t_saved is None:
        xp = -a
        t_inv = eye_f + xp
        for _ in range(5):
            xp = _bdot(xp, xp, HIGH)
            t_inv = _bdot(t_inv, eye_f + xp, HIGH)
    else:
        t_inv = _tri_inv_saved(a, t_saved)
    t16 = t_inv.astype(bf16)
    u = _bdot(t16, (v * beta).astype(bf16))
    w = _bdot(t16, (kb * e_gc).astype(bf16))
    attn = jnp.where(tril, _bdot_nt(q.astype(bf16), k16) * decay, 0.0)
    gam = jnp.broadcast_to(jnp.exp(gc_last), (nb, 1, GDN_HD))
    return u, w, attn, q * e_gc, k * jnp.exp(gc_last - gc), gam, t_inv


GDN_TB = 256
GDN_NC = GDN_TB // CHUNK
GDN_NU = GDN_NC * GDN_HEADS


def _gdn_unit_inputs(qkv_ref, pg_ref, al_ref, db_ref):
    pg = pg_ref[...]
    bl = jnp.stack([pg[cl * CHUNK:(cl + 1) * CHUNK, h:h + 1] for cl in range(GDN_NC) for h in range(GDN_HEADS)])
    al = jnp.stack([pg[cl * CHUNK:(cl + 1) * CHUNK, GDN_HEADS + h:GDN_HEADS + h + 1]
                    for cl in range(GDN_NC) for h in range(GDN_HEADS)])
    a_log = jnp.stack([jnp.full((1, 1), al_ref[0, h], f32) for _ in range(GDN_NC) for h in range(GDN_HEADS)])
    dt_b = jnp.stack([jnp.full((1, 1), db_ref[0, h], f32) for _ in range(GDN_NC) for h in range(GDN_HEADS)])
    return qkv_ref[0], qkv_ref[1], qkv_ref[2], bl, al, a_log, dt_b


def _unit_spec(*tail):
    nd = len(tail)
    return pl.BlockSpec((GDN_NU,) + tail, lambda i: (i,) + (0,) * nd)


def _gdn_intra_shapes(S):
    nu = S // CHUNK * GDN_HEADS
    row = SDS((nu, CHUNK, GDN_HD), f32)
    return [row, row, SDS((nu, CHUNK, CHUNK), f32), row, row, SDS((nu, 1, GDN_HD), f32)]


_GDN_INTRA_SPECS = lambda: [_unit_spec(CHUNK, GDN_HD), _unit_spec(CHUNK, GDN_HD), _unit_spec(CHUNK, CHUNK),
                            _unit_spec(CHUNK, GDN_HD), _unit_spec(CHUNK, GDN_HD), _unit_spec(1, GDN_HD)]


def _gdn_intra_fwd(qkv_u, proj, a_log, dt_bias, S):
    def body(qkv_ref, pg_ref, al_ref, db_ref, *outs):
        res = _gdn_intra(*_gdn_unit_inputs(qkv_ref, pg_ref, al_ref, db_ref))
        for o_ref, r in zip(outs, res):
            o_ref[...] = r

    nu = S // CHUNK * GDN_HEADS
    *intra, t_inv = pl.pallas_call(
        body, name="gdn_intra_fwd", grid=(S // GDN_TB,),
        in_specs=[pl.BlockSpec((3, GDN_NU, CHUNK, GDN_HD), lambda i: (0, i, 0, 0)),
                  pl.BlockSpec((GDN_TB, LANE), lambda i: (i, COL_G // LANE)),
                  pl.BlockSpec(memory_space=pltpu.SMEM), pl.BlockSpec(memory_space=pltpu.SMEM)],
        out_specs=_GDN_INTRA_SPECS() + [_unit_spec(CHUNK, CHUNK)],
        out_shape=_gdn_intra_shapes(S) + [SDS((nu, CHUNK, CHUNK), f32)],
        compiler_params=_cp(1, VMEM_LIMIT),
    )(qkv_u, proj, a_log, dt_bias)
    return intra, t_inv


def _gdn_intra_bwd(qkv_u, proj, a_log, dt_bias, t_inv, cots, S):
    def body(qkv_ref, pg_ref, al_ref, db_ref, t_ref, du_ref, dw_ref, da_ref, dqd_ref, dkd_ref, dgm_ref,
             dqkv_ref, dpg_ref, dal_ref, ddb_ref):
        @pl.when(pl.program_id(0) == 0)
        def _():
            dal_ref[...] = jnp.zeros_like(dal_ref)
            ddb_ref[...] = jnp.zeros_like(ddb_ref)

        t_saved = t_ref[...]
        _, vjp = jax.vjp(lambda *a: _gdn_intra(*a, t_saved=t_saved)[:6], *_gdn_unit_inputs(qkv_ref, pg_ref, al_ref, db_ref))
        dq, dk, dv, dbl, dal, da, ddb = vjp((du_ref[...], dw_ref[...], da_ref[...], dqd_ref[...], dkd_ref[...], dgm_ref[...]))
        dqkv_ref[0] = dq
        dqkv_ref[1] = dk
        dqkv_ref[2] = dv
        lane = lax.broadcasted_iota(jnp.int32, (CHUNK, LANE), 1)
        lane1 = lax.broadcasted_iota(jnp.int32, (1, LANE), 1)
        da_tot = jnp.zeros((1, LANE), f32)
        ddb_tot = jnp.zeros((1, LANE), f32)
        for cl in range(GDN_NC):
            dpg = jnp.zeros((CHUNK, LANE), f32)
            for h in range(GDN_HEADS):
                b = cl * GDN_HEADS + h
                dpg = dpg + jnp.where(lane == h, dbl[b], 0.0) + jnp.where(lane == GDN_HEADS + h, dal[b], 0.0)
                da_tot = da_tot + jnp.where(lane1 == h, da[b], 0.0)
                ddb_tot = ddb_tot + jnp.where(lane1 == h, ddb[b], 0.0)
            dpg_ref[cl * CHUNK:(cl + 1) * CHUNK, :] = dpg.astype(bf16)
        dal_ref[...] += da_tot
        ddb_ref[...] += ddb_tot

    acc = _const_spec((1, LANE))
    nu = S // CHUNK * GDN_HEADS
    return pl.pallas_call(
        body, name="gdn_intra_bwd", grid=(S // GDN_TB,),
        in_specs=[pl.BlockSpec((3, GDN_NU, CHUNK, GDN_HD), lambda i: (0, i, 0, 0)),
                  pl.BlockSpec((GDN_TB, LANE), lambda i: (i, COL_G // LANE)),
                  pl.BlockSpec(memory_space=pltpu.SMEM), pl.BlockSpec(memory_space=pltpu.SMEM),
                  _unit_spec(CHUNK, CHUNK)] + _GDN_INTRA_SPECS(),
        out_specs=[pl.BlockSpec((3, GDN_NU, CHUNK, GDN_HD), lambda i: (0, i, 0, 0)),
                   pl.BlockSpec((GDN_TB, LANE), lambda i: (i, 0)), acc, acc],
        out_shape=[SDS((3, nu, CHUNK, GDN_HD), f32), SDS((S, LANE), bf16), SDS((1, LANE), f32), SDS((1, LANE), f32)],
        compiler_params=_cp(1, VMEM_LIMIT),
    )(qkv_u, proj, a_log, dt_bias, t_inv, *cots)


def _gdn_scan_fwd(intra, proj, onorm_g, S):
    def body(u_ref, w_ref, at_ref, qd_ref, kd_ref, gm_ref, gate_ref, og_ref, out_ref, st_ref, s_scr):
        @pl.when(pl.program_id(0) == 0)
        def _():
            s_scr[...] = jnp.zeros_like(s_scr)

        og = og_ref[...]
        s = s_scr[...]
        for cl in range(GDN_NC):
            us = slice(cl * GDN_HEADS, (cl + 1) * GDN_HEADS)
            rows = slice(cl * CHUNK, (cl + 1) * CHUNK)
            st_ref[us] = s
            s16 = s.astype(bf16)
            vn = u_ref[us] - _bdot(w_ref[us].astype(bf16), s16)
            vn16 = vn.astype(bf16)
            o = _bdot(qd_ref[us].astype(bf16), s16) + _bdot(at_ref[us].astype(bf16), vn16)
            s = s * gm_ref[us] + _bdot_tn(kd_ref[us].astype(bf16), vn16)
            for h in range(GDN_HEADS):
                oh = o[h]
                gt = gate_ref[rows, h * GDN_HD:(h + 1) * GDN_HD]
                on = oh * lax.rsqrt(jnp.mean(oh * oh, axis=-1, keepdims=True) + RMS_EPS) * og
                out_ref[rows, h * GDN_HD:(h + 1) * GDN_HD] = on * (gt * _sigmoid(gt))
        s_scr[...] = s

    nu = S // CHUNK * GDN_HEADS
    return pl.pallas_call(
        body, name="gdn_scan_fwd", grid=(S // GDN_TB,),
        in_specs=_GDN_INTRA_SPECS() + [pl.BlockSpec((GDN_TB, GDN_W), lambda i: (i, 3)), _const_spec((1, GDN_HD))],
        out_specs=[pl.BlockSpec((GDN_TB, GDN_W), lambda i: (i, 0)), _unit_spec(GDN_HD, GDN_HD)],
        out_shape=[SDS((S, GDN_W), f32), SDS((nu, GDN_HD, GDN_HD), f32)],
        scratch_shapes=[pltpu.VMEM((GDN_HEADS, GDN_HD, GDN_HD), f32)],
        compiler_params=_cp(1, VMEM_LIMIT),
    )(*intra, proj, onorm_g)


def _gdn_scan_bwd(intra, states, proj, d_oab, onorm_g, after, S):
    n_steps = S // GDN_TB

    def body(u_ref, w_ref, at_ref, qd_ref, kd_ref, gm_ref, st_ref, gate_ref, do_ref, og_ref, after_ref,
             du_ref, dw_ref, dat_ref, dqd_ref, dkd_ref, dgm_ref, dgate_ref, dog_ref, ds_scr):
        @pl.when(pl.program_id(0) == 0)
        def _():
            ds_scr[...] = jnp.zeros_like(ds_scr)
            dog_ref[...] = jnp.zeros_like(dog_ref)

        og = og_ref[...]
        ii = lax.broadcasted_iota(jnp.int32, (CHUNK, CHUNK), 0)
        jj = lax.broadcasted_iota(jnp.int32, (CHUNK, CHUNK), 1)
        tril = ii >= jj
        ds = ds_scr[...]
        dog = jnp.zeros((1, GDN_HD), f32)
        for cl in reversed(range(GDN_NC)):
            us = slice(cl * GDN_HEADS, (cl + 1) * GDN_HEADS)
            rows = slice(cl * CHUNK, (cl + 1) * CHUNK)
            s0 = st_ref[us]
            s016 = s0.astype(bf16)
            w16 = w_ref[us].astype(bf16)
            qd16 = qd_ref[us].astype(bf16)
            kd16 = kd_ref[us].astype(bf16)
            at16 = at_ref[us].astype(bf16)
            vn = u_ref[us] - _bdot(w16, s016)
            vn16 = vn.astype(bf16)
            o = _bdot(qd16, s016) + _bdot(at16, vn16)
            do_h = []
            for h in range(GDN_HEADS):
                oh = o[h]
                lanes = slice(h * GDN_HD, (h + 1) * GDN_HD)
                gt = gate_ref[rows, lanes]
                d_out = do_ref[rows, lanes]
                r = lax.rsqrt(jnp.mean(oh * oh, axis=-1, keepdims=True) + RMS_EPS)
                n = oh * r
                sg = _sigmoid(gt)
                silu = gt * sg
                dog = dog + jnp.sum(d_out * n * silu, axis=0, keepdims=True)
                dgate_ref[rows, lanes] = (d_out * n * og * (sg * (1.0 + gt * (1.0 - sg)))).astype(bf16)
                dn = d_out * og * silu
                do_h.append(r * (dn - n * jnp.mean(dn * n, axis=-1, keepdims=True)))
            do16 = jnp.stack(do_h).astype(bf16)
            ds16 = ds.astype(bf16)
            dvn = _bdot_tn(at16, do16) + _bdot(kd16, ds16)
            dvn16 = dvn.astype(bf16)
            du_ref[us] = dvn
            dw_ref[us] = -_bdot_nt(dvn16, s016)
            dat_ref[us] = jnp.where(tril, _bdot_nt(do16, vn16), 0.0)
            dqd_ref[us] = _bdot_nt(do16, s016)
            dkd_ref[us] = _bdot_nt(vn16, ds16)
            dgm_ref[us] = jnp.sum(s0 * ds, axis=1, keepdims=True)
            ds = _bdot_tn(qd16, do16) + ds * gm_ref[us] - _bdot_tn(w16, dvn16)
        ds_scr[...] = ds
        dog_ref[...] += dog

    def unit(*tail):
        nd = len(tail)
        return pl.BlockSpec((GDN_NU,) + tail, lambda i: (n_steps - 1 - i,) + (0,) * nd)

    intra_specs = [unit(CHUNK, GDN_HD), unit(CHUNK, GDN_HD), unit(CHUNK, CHUNK), unit(CHUNK, GDN_HD),
                   unit(CHUNK, GDN_HD), unit(1, GDN_HD)]
    tok = lambda c: pl.BlockSpec((GDN_TB, GDN_W), lambda i: (n_steps - 1 - i, c))
    return pl.pallas_call(
        body, name="gdn_scan_bwd", grid=(n_steps,),
        in_specs=intra_specs + [unit(GDN_HD, GDN_HD), tok(3), tok(0), _const_spec((1, GDN_HD)), _ANY],
        out_specs=intra_specs + [tok(0), _const_spec((1, GDN_HD))],
        out_shape=_gdn_intra_shapes(S) + [SDS((S, GDN_W), bf16), SDS((1, GDN_HD), f32)],
        scratch_shapes=[pltpu.VMEM((GDN_HEADS, GDN_HD, GDN_HD), f32)],
        compiler_params=_cp(1, VMEM_LIMIT),
    )(*intra, states, proj, d_oab, onorm_g, after)


SWA_UNROLL = 4


def _swa_tiles(q_ref, k_ref, v_ref, it, d, nb_log2, S):
    nb = 1 << nb_log2
    r = lax.shift_right_logical(it, nb_log2)
    blk = lax.bitwise_and(it, nb - 1)
    qs = blk * (SWA_BLK * d) + r
    ps = jnp.maximum(blk - 1, 0) * (SWA_BLK * d) + r
    if d > 1:
        rows_c, rows_p = pl.ds(qs, SWA_BLK, stride=d), pl.ds(ps, SWA_BLK, stride=d)
    else:
        rows_c, rows_p = pl.ds(pl.multiple_of(qs, SWA_BLK), SWA_BLK), pl.ds(pl.multiple_of(ps, SWA_BLK), SWA_BLK)
    return rows_c, rows_p, blk > 0


def _swa_prev_modes(nb):
    if nb >= SWA_UNROLL:
        return ["load"] + ["reuse"] * (SWA_UNROLL - 1)
    return ["none" if u % nb == 0 else "reuse" for u in range(SWA_UNROLL)]


def _swa_fwd(proj, bt, S):
    scale = SWA_HD ** -0.5

    def body(q_ref, k_ref, v_ref, bt_ref, o_ref, lse0_ref, lse1_ref, m0_scr, m1_scr, a0_scr, a1_scr):
        lane = lax.broadcasted_iota(jnp.int32, (SWA_BLK, LANE), 1)
        h0 = lane < SWA_HD
        qi = lax.broadcasted_iota(jnp.int32, (SWA_BLK, SWA_BLK), 0)
        kj = lax.broadcasted_iota(jnp.int32, (SWA_BLK, SWA_BLK), 1)
        lower = kj <= qi
        ones16 = jnp.ones((LANE, SWA_BLK), bf16)
        m_scrs = (m0_scr, m1_scr)
        a_scrs = (a0_scr, a1_scr)
        for p, (_, d) in reversed(list(enumerate(PATTERNS))):
            nb_log2 = int(math.log2(S // d // SWA_BLK))
            first = p == len(PATTERNS) - 1

            def lp(i, carry, p=p, d=d, nb_log2=nb_log2, first=first):
                heads = [h0, jnp.logical_not(h0)]
                modes = _swa_prev_modes(1 << nb_log2)
                tiles = []
                kc_f = None
                for u in range(SWA_UNROLL):
                    rows_c, rows_p, has_prev = _swa_tiles(q_ref, k_ref, v_ref, i * SWA_UNROLL + u, d, nb_log2, S)
                    kp_f = {"load": lambda: k_ref[rows_p, :], "reuse": lambda: kc_f, "none": lambda: None}[modes[u]]()
                    has_prev = {"load": has_prev, "reuse": True, "none": False}[modes[u]]
                    q = q_ref[rows_c, :]
                    kc_f = k_ref[rows_c, :]
                    kc = kc_f.astype(bf16)
                    logits = []
                    for mh in heads:
                        q_h = jnp.where(mh, q, 0.0)
                        qh = q_h.astype(bf16)
                        if kp_f is None:
                            logits.append((_dot_nt(qh, kc), None, None))
                        else:
                            logits.append((_dot_nt(qh, kc), _dot_nt(qh, kp_f.astype(bf16)), _dot((q_h * kp_f).astype(bf16), ones16)))
                    tiles.append((rows_c, rows_p, has_prev, logits))
                probs = []
                for rows_c, rows_p, has_prev, logits in tiles:
                    per_head = []
                    for h, (s_c, s_p, far) in enumerate(logits):
                        if has_prev is False:
                            s = jnp.where(lower, s_c * scale + bt_ref[p, h], NEG)
                            s_far = None
                        else:
                            s = jnp.where(lower, s_c, s_p) * scale + bt_ref[p, h]
                            s_far = far * scale + bt_ref[p, h, SWA_BLK - 1:SWA_BLK, 0:1]
                            if has_prev is not True:
                                s = jnp.where(jnp.logical_or(lower, has_prev), s, NEG)
                                s_far = jnp.where(has_prev, s_far, NEG)
                        mn = jnp.max(s, axis=1, keepdims=True)
                        if s_far is not None:
                            mn = jnp.maximum(s_far, mn)
                        alpha = None
                        if not first:
                            mo = m_scrs[h][rows_c, :]
                            mn = jnp.maximum(mo, mn)
                            alpha = jnp.exp(mo - mn)
                        mn = jnp.broadcast_to(mn, (SWA_BLK, LANE))
                        pm = jnp.exp(s - mn)
                        per_head.append((mn, alpha, None if s_far is None else jnp.exp(s_far - mn),
                                         jnp.where(lower, pm, 0.0).astype(bf16),
                                         None if s_far is None else jnp.where(lower, 0.0, pm).astype(bf16)))
                    probs.append(per_head)
                acc_old = [None if first else (a0_scr[t[0], :], a1_scr[t[0], :]) for t in tiles]
                done = []
                vc = None
                for u, ((rows_c, rows_p, _, _), per_head, old) in enumerate(zip(tiles, probs, acc_old)):
                    vp = {"load": lambda: v_ref[rows_p, :], "reuse": lambda: vc, "none": lambda: None}[modes[u]]()
                    vc = v_ref[rows_c, :]
                    acc_new = []
                    for h, (mn, alpha, p_far, pc16, pp16) in enumerate(per_head):
                        pv = _dot(pc16, jnp.where(heads[h], vc, 1.0).astype(bf16))
                        if pp16 is not None:
                            vpa = jnp.where(heads[h], vp, 1.0)
                            pv = pv + _dot(pp16, vpa.astype(bf16)) + p_far * vpa
                        acc_new.append(pv if first else alpha * old[h] + pv)
                    done.append((rows_c, per_head[0][0], per_head[1][0], acc_new[0], acc_new[1]))
                for rows_c, m0_new, m1_new, a0_new, a1_new in done:
                    m0_scr[rows_c, :] = m0_new
                    m1_scr[rows_c, :] = m1_new
                    a0_scr[rows_c, :] = a0_new
                    a1_scr[rows_c, :] = a1_new
                return carry

            lax.fori_loop(0, S // SWA_BLK // SWA_UNROLL, lp, 0)

        def fin(c, carry):
            rows = pl.ds(pl.multiple_of(c * SWA_BLK, SWA_BLK), SWA_BLK)
            a0 = a0_scr[rows, :]
            a1 = a1_scr[rows, :]
            l0 = jnp.where(h0, pltpu.roll(a0, SWA_HD, 1), a0)
            l1 = jnp.where(h0, a1, pltpu.roll(a1, SWA_HD, 1))
            o_ref[rows, :] = jnp.where(h0, a0 / l0, a1 / l1)
            lse0_ref[rows, :] = m0_scr[rows, :] + jnp.log(l0)
            lse1_ref[rows, :] = m1_scr[rows, :] + jnp.log(l1)
            return carry

        lax.fori_loop(0, S // SWA_BLK, fin, 0)

    qb = COL_B // LANE
    col = lambda c: pl.BlockSpec((S, LANE), lambda hp, c=c: (0, c + hp))
    return pl.pallas_call(
        body, name="swa_fwd", grid=(4,),
        in_specs=[col(qb), col(qb + 4), col(qb + 8), pl.BlockSpec((3, 2, SWA_BLK, SWA_BLK), lambda hp: (0, hp, 0, 0))],
        out_specs=[col(0), col(0), col(0)],
        out_shape=[SDS((S, SWA_W), f32)] * 3,
        scratch_shapes=[pltpu.VMEM((S, LANE), f32)] * 4,
        compiler_params=_cp(1, VMEM_LIMIT),
    )(proj, proj, proj, bt)


def _swa_bwd(proj, bt, ob, lse0, lse1, d_oab, after, S):
    scale = SWA_HD ** -0.5

    def body(q_ref, k_ref, v_ref, bt_ref, o_ref, lse0_ref, lse1_ref, do_ref, after_ref, dq_ref, dk_ref, dv_ref, dsb_ref,
             dq_scr, dk_scr, dv_scr, nd_scr):
        lane = lax.broadcasted_iota(jnp.int32, (SWA_BLK, LANE), 1)
        h0 = lane < SWA_HD
        qi = lax.broadcasted_iota(jnp.int32, (SWA_BLK, SWA_BLK), 0)
        kj = lax.broadcasted_iota(jnp.int32, (SWA_BLK, SWA_BLK), 1)
        lower = kj <= qi
        eye = kj == qi
        rel127 = jnp.logical_or(kj == qi + 1, jnp.logical_and(qi == SWA_BLK - 1, kj == 0))
        ones16 = jnp.ones((LANE, SWA_BLK), bf16)
        lse_refs = (lse0_ref, lse1_ref)
        dk_scr[...] = jnp.zeros((S, LANE), f32)
        dv_scr[...] = jnp.zeros((S, LANE), f32)
        dsb_ref[...] = jnp.zeros_like(dsb_ref)

        def row_sum(a):
            hi = a.astype(bf16)
            return _dot(hi, ones16) + _dot((a - hi.astype(f32)).astype(bf16), ones16)

        def prep(c, carry):
            rows = pl.ds(pl.multiple_of(c * SWA_BLK, SWA_BLK), SWA_BLK)
            dlt = do_ref[rows, :] * o_ref[rows, :]
            d0 = row_sum(jnp.where(h0, dlt, 0.0))
            d1 = row_sum(jnp.where(h0, 0.0, dlt))
            nd_scr[rows, :] = jnp.where(h0, d1, d0) * (-1.0 / SWA_HD)
            return carry

        lax.fori_loop(0, S // SWA_BLK, prep, 0)

        for p, (_, d) in reversed(list(enumerate(PATTERNS))):
            nb_log2 = int(math.log2(S // d // SWA_BLK))
            first = p == len(PATTERNS) - 1

            def lp(i, carry, p=p, d=d, nb_log2=nb_log2, first=first):
                heads = [h0, jnp.logical_not(h0)]
                modes = _swa_prev_modes(1 << nb_log2)
                tiles = []
                kc_f = vc_f = None
                for u in range(SWA_UNROLL):
                    rows_c, rows_p, has_prev = _swa_tiles(q_ref, k_ref, v_ref, i * SWA_UNROLL + u, d, nb_log2, S)
                    kp_f = {"load": lambda: k_ref[rows_p, :], "reuse": lambda: kc_f, "none": lambda: None}[modes[u]]()
                    vp_f = {"load": lambda: v_ref[rows_p, :], "reuse": lambda: vc_f, "none": lambda: None}[modes[u]]()
                    has_prev = {"load": has_prev, "reuse": True, "none": False}[modes[u]]
                    q = q_ref[rows_c, :]
                    kc_f = k_ref[rows_c, :]
                    vc_f = v_ref[rows_c, :]
                    kc = kc_f.astype(bf16)
                    kp = None if kp_f is None else kp_f.astype(bf16)
                    do = do_ref[rows_c, :]
                    nd = nd_scr[rows_c, :]
                    per_head = []
                    for mh in heads:
                        q_h = jnp.where(mh, q, 0.0)
                        do_a = jnp.where(mh, do, nd)
                        qh = q_h.astype(bf16)
                        doa = do_a.astype(bf16)
                        doh = jnp.where(mh, do, 0.0).astype(bf16)
                        dd_c = _dot_nt(doa, jnp.where(mh, vc_f, 1.0).astype(bf16))
                        if kp_f is None:
                            per_head.append((qh, doh, _dot_nt(qh, kc), None, None, dd_c, None, None))
                        else:
                            vpa = jnp.where(mh, vp_f, 1.0)
                            per_head.append((qh, doh, _dot_nt(qh, kc), _dot_nt(qh, kp), _dot((q_h * kp_f).astype(bf16), ones16),
                                             dd_c, _dot_nt(doa, vpa.astype(bf16)), _dot((do_a * vpa).astype(bf16), ones16)))
                    tiles.append((rows_c, rows_p, has_prev, kc, kp, per_head))
                grads = []
                for rows_c, rows_p, has_prev, kc, kp, per_head in tiles:
                    out = []
                    for h, (qh, doh, s_c, s_p, far, dd_c, dd_p, dd_far) in enumerate(per_head):
                        lse_h = lse_refs[h][rows_c, :]
                        if has_prev is False:
                            pm = jnp.exp(jnp.where(lower, s_c * scale + bt_ref[p, h], NEG) - lse_h)
                            dsm = pm * dd_c
                            out.append((dsm, dsm.astype(bf16), None, pm.astype(bf16), None))
                            continue
                        s = jnp.where(lower, s_c, s_p) * scale + bt_ref[p, h]
                        s_far = far * scale + bt_ref[p, h, SWA_BLK - 1:SWA_BLK, 0:1]
                        if has_prev is not True:
                            s = jnp.where(jnp.logical_or(lower, has_prev), s, NEG)
                            s_far = jnp.where(has_prev, s_far, NEG)
                        pm = jnp.exp(s - lse_h)
                        p_far = jnp.exp(s_far - lse_h)
                        dsm = pm * jnp.where(lower, dd_c, dd_p)
                        ds_far = p_far * dd_far
                        out.append((dsm + jnp.where(rel127, ds_far, 0.0),
                                    jnp.where(lower, dsm, 0.0).astype(bf16),
                                    jnp.where(lower, jnp.where(eye, ds_far, 0.0), dsm).astype(bf16),
                                    jnp.where(lower, pm, 0.0).astype(bf16),
                                    jnp.where(lower, jnp.where(eye, p_far, 0.0), pm).astype(bf16)))
                    grads.append(out)
                done = []
                add = lambda acc, t: t if acc is None else acc + t
                for (rows_c, rows_p, _, kc, kp, per_head), out in zip(tiles, grads):
                    dq_t = dkc_t = dkp_t = dvc_t = dvp_t = None
                    for h, (_, dsc16, dsp16, pc16, pp16) in enumerate(out):
                        qh, doh = per_head[h][0], per_head[h][1]
                        dq_h = _dot(dsc16, kc)
                        dkc_t = add(dkc_t, _dot_tn(dsc16, qh) * scale)
                        dvc_t = add(dvc_t, _dot_tn(pc16, doh))
                        if dsp16 is not None:
                            dq_h = dq_h + _dot(dsp16, kp)
                            dkp_t = add(dkp_t, _dot_tn(dsp16, qh) * scale)
                            dvp_t = add(dvp_t, _dot_tn(pp16, doh))
                        dq_t = add(dq_t, jnp.where(heads[h], dq_h * scale, 0.0))
                    done.append([rows_c, rows_p, dq_t, dkc_t, dkp_t, dvc_t, dvp_t])
                for u in range(1, SWA_UNROLL):
                    if modes[u] == "reuse":
                        done[u - 1][3] = done[u - 1][3] + done[u][4]
                        done[u - 1][5] = done[u - 1][5] + done[u][6]
                for h in range(2):
                    tot = grads[0][h][0]
                    for g in grads[1:]:
                        tot = tot + g[h][0]
                    dsb_ref[0, p, h] += tot
                for u, (rows_c, rows_p, dq_t, dkc_t, dkp_t, dvc_t, dvp_t) in enumerate(done):
                    dq_scr[rows_c, :] = dq_t if first else dq_scr[rows_c, :] + dq_t
                    dk_scr[rows_c, :] = dk_scr[rows_c, :] + dkc_t
                    dv_scr[rows_c, :] = dv_scr[rows_c, :] + dvc_t
                    if modes[u] == "load":
                        dk_scr[rows_p, :] = dk_scr[rows_p, :] + dkp_t
                        dv_scr[rows_p, :] = dv_scr[rows_p, :] + dvp_t
                return carry

            lax.fori_loop(0, S // SWA_BLK // SWA_UNROLL, lp, 0)
        dq_ref[...] = dq_scr[...].astype(bf16)
        dk_ref[...] = dk_scr[...].astype(bf16)
        dv_ref[...] = dv_scr[...].astype(bf16)

    qb = COL_B // LANE
    col = lambda c: pl.BlockSpec((S, LANE), lambda hp, c=c: (0, c + hp))
    return pl.pallas_call(
        body, name="swa_bwd", grid=(4,),
        in_specs=[col(qb), col(qb + 4), col(qb + 8),
                  pl.BlockSpec((3, 2, SWA_BLK, SWA_BLK), lambda hp: (0, hp, 0, 0)),
                  col(0), col(0), col(0), col(4), _ANY],
        out_specs=[col(0), col(0), col(0),
                   pl.BlockSpec((1, 3, 2, SWA_BLK, SWA_BLK), lambda hp: (hp, 0, 0, 0, 0))],
        out_shape=[SDS((S, SWA_W), bf16)] * 3 + [SDS((4, 3, 2, SWA_BLK, SWA_BLK), f32)],
        scratch_shapes=[pltpu.VMEM((S, LANE), f32)] * 4,
        compiler_params=_cp(1, VMEM_LIMIT),
    )(proj, proj, proj, bt, ob, lse0, lse1, d_oab, after)


def _mix_fwd(oa, ob, w_out, x, g_post, S):
    TS = 512

    def body(oa_ref, ob_ref, w_ref, x_ref, g_ref, mix_ref, x1_ref):
        mix = _dot(oa_ref[...].astype(bf16), w_ref[0:GDN_W, :]) + _dot(ob_ref[...].astype(bf16), w_ref[GDN_W:D_MODEL, :])
        r = lax.rsqrt(jnp.mean(mix * mix, axis=-1, keepdims=True) + RMS_EPS)
        mix_ref[...] = mix
        x1_ref[...] = x_ref[...] + mix * r * g_ref[...]

    row = lambda w: pl.BlockSpec((TS, w), lambda i: (i, 0))
    return pl.pallas_call(
        body, name="mix_fwd", grid=(S // TS,),
        in_specs=[row(GDN_W), row(SWA_W), _resident_spec((D_MODEL, D_MODEL)), row(D_MODEL), _const_spec((1, D_MODEL))],
        out_specs=[row(D_MODEL), row(D_MODEL)],
        out_shape=[SDS((S, D_MODEL), f32), SDS((S, D_MODEL), f32)],
        compiler_params=_cp(1, VMEM_LIMIT),
    )(oa, ob, w_out, x, g_post)


def _mix_bwd(dx1, mix, g_post, w_out, S):
    TS = 512

    def body(dx1_ref, mix_ref, g_ref, w_ref, dmix_ref, doab_ref, dg_ref):
        @pl.when(pl.program_id(0) == 0)
        def _():
            dg_ref[...] = jnp.zeros_like(dg_ref)

        mix = mix_ref[...]
        dz = dx1_ref[...]
        r = lax.rsqrt(jnp.mean(mix * mix, axis=-1, keepdims=True) + RMS_EPS)
        n = mix * r
        dg_ref[...] += jnp.sum(dz * n, axis=0, keepdims=True)
        dn = dz * g_ref[...]
        dmix = (r * (dn - n * jnp.mean(dn * n, axis=-1, keepdims=True))).astype(bf16)
        dmix_ref[...] = dmix
        doab_ref[...] = _dot_nt(dmix, w_ref[...])

    row = lambda: pl.BlockSpec((TS, D_MODEL), lambda i: (i, 0))
    return pl.pallas_call(
        body, name="mix_bwd", grid=(S // TS,),
        in_specs=[row(), row(), _const_spec((1, D_MODEL)), _resident_spec((D_MODEL, D_MODEL))],
        out_specs=[row(), row(), _const_spec((1, D_MODEL))],
        out_shape=[SDS((S, D_MODEL), bf16), SDS((S, D_MODEL), f32), SDS((1, D_MODEL), f32)],
        compiler_params=_cp(1, VMEM_LIMIT),
    )(dx1, mix, g_post, w_out)


FFN_TS = 256
FFN_CH = 1408


def _ffn(x1, tgt, g_pre, g_post, wg, wu, wd, S):
    def body(x1_ref, t_ref, gp_ref, gq_ref, wg_ref, wu_ref, wd_ref,
             dx1_ref, h2_ref, act_ref, dgate_ref, dup_ref, df_ref, loss_ref, dgp_ref, dgq_ref, gate_scr, up_scr):
        @pl.when(pl.program_id(0) == 0)
        def _():
            loss_ref[...] = jnp.zeros_like(loss_ref)
            dgp_ref[...] = jnp.zeros_like(dgp_ref)
            dgq_ref[...] = jnp.zeros_like(dgq_ref)

        x1v = x1_ref[...]
        gp = gp_ref[...]
        gq = gq_ref[...]
        r2 = lax.rsqrt(jnp.mean(x1v * x1v, axis=-1, keepdims=True) + RMS_EPS)
        n2 = x1v * r2
        h2 = (n2 * gp).astype(bf16)
        h2_ref[...] = h2
        f = jnp.zeros((FFN_TS, D_MODEL), f32)
        for c in range(D_FF // FFN_CH):
            cs = slice(c * FFN_CH, (c + 1) * FFN_CH)
            gate = _dot(h2, wg_ref[:, cs])
            up = _dot(h2, wu_ref[:, cs])
            gate_scr[:, cs] = gate
            up_scr[:, cs] = up
            act = (gate * _sigmoid(gate) * up).astype(bf16)
            act_ref[:, cs] = act
            f = f + _dot(act, wd_ref[cs, :])
        r3 = lax.rsqrt(jnp.mean(f * f, axis=-1, keepdims=True) + RMS_EPS)
        n3 = f * r3
        err = x1v + n3 * gq - t_ref[...]
        loss_ref[...] += 0.5 * jnp.sum(jnp.mean(err * err, axis=-1, keepdims=True), axis=0, keepdims=True)
        dy = err * (1.0 / D_MODEL)
        dgq_ref[...] += jnp.sum(dy * n3, axis=0, keepdims=True)
        dn3 = dy * gq
        df = (r3 * (dn3 - n3 * jnp.mean(dn3 * n3, axis=-1, keepdims=True))).astype(bf16)
        df_ref[...] = df
        dh2 = jnp.zeros((FFN_TS, D_MODEL), f32)
        for c in range(D_FF // FFN_CH):
            cs = slice(c * FFN_CH, (c + 1) * FFN_CH)
            gate = gate_scr[:, cs]
            up = up_scr[:, cs]
            dact = _dot_nt(df, wd_ref[cs, :])
            sg = _sigmoid(gate)
            dup = (dact * gate * sg).astype(bf16)
            dgate = (dact * up * (sg * (1.0 + gate * (1.0 - sg)))).astype(bf16)
            dup_ref[:, cs] = dup
            dgate_ref[:, cs] = dgate
            dh2 = dh2 + _dot_nt(dgate, wg_ref[:, cs]) + _dot_nt(dup, wu_ref[:, cs])
        dgp_ref[...] += jnp.sum(dh2 * n2, axis=0, keepdims=True)
        dn2 = dh2 * gp
        dx1_ref[...] = dy + r2 * (dn2 - n2 * jnp.mean(dn2 * n2, axis=-1, keepdims=True))

    row = lambda w: pl.BlockSpec((FFN_TS, w), lambda i: (i, 0))
    vec = _const_spec((1, D_MODEL))
    return pl.pallas_call(
        body, name="ffn_fwd_bwd", grid=(S // FFN_TS,),
        in_specs=[row(D_MODEL), row(D_MODEL), vec, vec, _resident_spec((D_MODEL, D_FF)), _resident_spec((D_MODEL, D_FF)),
                  _resident_spec((D_FF, D_MODEL))],
        out_specs=[row(D_MODEL), row(D_MODEL), row(D_FF), row(D_FF), row(D_FF), row(D_MODEL), _const_spec((1, LANE)), vec, vec],
        out_shape=[SDS((S, D_MODEL), f32), SDS((S, D_MODEL), bf16), SDS((S, D_FF), bf16), SDS((S, D_FF), bf16),
                   SDS((S, D_FF), bf16), SDS((S, D_MODEL), bf16), SDS((1, LANE), f32), SDS((1, D_MODEL), f32),
                   SDS((1, D_MODEL), f32)],
        scratch_shapes=[pltpu.VMEM((FFN_TS, D_FF), f32), pltpu.VMEM((FFN_TS, D_FF), f32)],
        compiler_params=_cp(1, VMEM_LIMIT),
    )(x1, tgt, g_pre, g_post, wg, wu, wd)


def _proj_bwd(x, dx1, g_pre, wcat, segs, after, S):
    TS = 256
    n = len(segs)
    cols = [(c0, a.shape[1]) for a, c0 in segs]

    def body(*refs):
        x_ref, dx1_ref, g_ref, w_ref = refs[:4]
        seg_refs = refs[4:4 + n]
        gx_ref, dg_ref = refs[5 + n:]

        @pl.when(pl.program_id(0) == 0)
        def _():
            dg_ref[...] = jnp.zeros_like(dg_ref)

        dh = jnp.zeros((TS, D_MODEL), f32)
        for s_ref, (c0, w) in zip(seg_refs, cols):
            dh = dh + _dot_nt(s_ref[...], w_ref[:, c0:c0 + w])
        xv = x_ref[...]
        g = g_ref[...]
        r = lax.rsqrt(jnp.mean(xv * xv, axis=-1, keepdims=True) + RMS_EPS)
        nx = xv * r
        dg_ref[...] += jnp.sum(dh * nx, axis=0, keepdims=True)
        dn = dh * g
        gx_ref[...] = dx1_ref[...] + r * (dn - nx * jnp.mean(dn * nx, axis=-1, keepdims=True))

    row = lambda w: pl.BlockSpec((TS, w), lambda i: (i, 0))
    return pl.pallas_call(
        body, name="proj_bwd", grid=(S // TS,),
        in_specs=[row(D_MODEL), row(D_MODEL), _const_spec((1, D_MODEL)), _resident_spec((D_MODEL, NCOL))]
                 + [row(w) for _, w in cols] + [_ANY],
        out_specs=[row(D_MODEL), _const_spec((1, D_MODEL))],
        out_shape=[SDS((S, D_MODEL), f32), SDS((1, D_MODEL), f32)],
        compiler_params=_cp(1, VMEM_LIMIT),
    )(x, dx1, g_pre, wcat, *[a for a, _ in segs], after)


def _wgrad(a, b, S, name, col_blocks=False):
    TS = 512
    K = a.shape[1]
    N = b.shape[1]
    TN = next(t for t in (512, 1408, N) if N % t == 0)
    cb = N // N_DEV
    nblk = TN // cb if col_blocks else 0

    def body(a_ref, b_ref, o_ref, acc):
        @pl.when(pl.program_id(1) == 0)
        def _():
            acc[...] = jnp.zeros_like(acc)

        acc[...] += _dot_tn(a_ref[...].astype(bf16), b_ref[...])

        @pl.when(pl.program_id(1) == pl.num_programs(1) - 1)
        def _():
            if col_blocks:
                for i in range(nblk):
                    o_ref[i] = acc[:, i * cb:(i + 1) * cb].astype(bf16)
            else:
                o_ref[...] = acc[...].astype(bf16)

    if col_blocks:
        out_spec = pl.BlockSpec((nblk, K, cb), lambda j, s: (j, 0, 0))
        out_shape = SDS((N_DEV, K, cb), bf16)
    else:
        out_spec = pl.BlockSpec((K, TN), lambda j, s: (0, j))
        out_shape = SDS((K, N), bf16)
    return pl.pallas_call(
        body, name=name, grid=(N // TN, S // TS),
        in_specs=[pl.BlockSpec((TS, K), lambda j, s: (s, 0)), pl.BlockSpec((TS, TN), lambda j, s: (s, j))],
        out_specs=out_spec, out_shape=out_shape,
        scratch_shapes=[pltpu.VMEM((K, TN), f32)],
        compiler_params=_cp(2, VMEM_LIMIT),
    )(a, b)


def _w_in_pieces():
    n_a, n_g = 4 * GDN_W, 2 * GDN_HEADS
    cb = IN_COLS // N_DEV
    bounds = [(0, n_a, COL_A), (n_a, n_a + n_g, COL_G), (n_a + n_g, IN_COLS, COL_B)]
    out = []
    for j in range(N_DEV):
        lo, hi = j * cb, (j + 1) * cb
        for s0, s1, dst in bounds:
            a, b = max(lo, s0), min(hi, s1)
            if a < b:
                out.append((j, a - lo, b - a, dst + a - s0))
    return out


def _wcat_from_blocks(g_in):
    TR = 256
    cb = IN_COLS // N_DEV
    pieces = _w_in_pieces()

    def body(w_ref, o_ref):
        o_ref[:, COL_G:NCOL] = jnp.zeros((TR, NCOL - COL_G), bf16)
        for j, off, w, dst in pieces:
            o_ref[:, dst:dst + w] = w_ref[j, :, off:off + w]

    return pl.pallas_call(
        body, name="wcat_from_blocks", grid=(D_MODEL // TR,),
        in_specs=[pl.BlockSpec((N_DEV, TR, cb), lambda i: (0, i, 0))],
        out_specs=pl.BlockSpec((TR, NCOL), lambda i: (i, 0)),
        out_shape=SDS((D_MODEL, NCOL), bf16),
        compiler_params=_cp(1, VMEM_LIMIT),
    )(g_in)


def _full_from_col_blocks(g):
    n, R, C = g.shape
    TR = 256

    def body(w_ref, o_ref):
        for j in range(n):
            o_ref[:, j * C:(j + 1) * C] = w_ref[j]

    return pl.pallas_call(
        body, name="full_from_col_blocks", grid=(R // TR,),
        in_specs=[pl.BlockSpec((n, TR, C), lambda i: (0, i, 0))],
        out_specs=pl.BlockSpec((TR, n * C), lambda i: (i, 0)),
        out_shape=SDS((R, n * C), g.dtype),
        compiler_params=_cp(1, VMEM_LIMIT),
    )(g)


def _wgrad_in(h1, segs, S):
    TS = 512
    n = len(segs)
    cols = [(c0, a.shape[1]) for a, c0 in segs]
    cb = IN_COLS // N_DEV
    pieces = _w_in_pieces()

    def body(*refs):
        h_ref = refs[0]
        seg_refs = refs[1:1 + n]
        o_ref, acc = refs[1 + n], refs[2 + n]

        @pl.when(pl.program_id(0) == 0)
        def _():
            acc[...] = jnp.zeros_like(acc)

        h = h_ref[...]
        for s_ref, (c0, w) in zip(seg_refs, cols):
            acc[:, c0:c0 + w] += _dot_tn(h, s_ref[...])

        @pl.when(pl.program_id(0) == pl.num_programs(0) - 1)
        def _():
            for j, off, w, src in pieces:
                o_ref[j, :, off:off + w] = acc[:, src:src + w].astype(bf16)

    row = lambda w: pl.BlockSpec((TS, w), lambda i: (i, 0))
    return pl.pallas_call(
        body, name="wgrad_in", grid=(S // TS,),
        in_specs=[row(D_MODEL)] + [row(w) for _, w in cols],
        out_specs=_const_spec((N_DEV, D_MODEL, cb)),
        out_shape=SDS((N_DEV, D_MODEL, cb), bf16),
        scratch_shapes=[pltpu.VMEM((D_MODEL, NCOL), f32)],
        compiler_params=_cp(1, VMEM_LIMIT),
    )(h1, *[a for a, _ in segs])


def _adamw(recv, w, m, v, name):
    R, C = w.shape
    TR = 256 if R % 256 == 0 else R
    c1 = 1.0 / (1.0 - ADAM_B1 ** ADAM_STEP)
    c2 = 1.0 / (1.0 - ADAM_B2 ** ADAM_STEP)

    def body(r_ref, w_ref, m_ref, v_ref, g_out, d_out, m_out, v_out):
        g = r_ref[0].astype(f32)
        for s in range(1, N_DEV):
            g = g + r_ref[s].astype(f32)
        mn = ADAM_B1 * m_ref[...] + (1.0 - ADAM_B1) * g
        vn = ADAM_B2 * v_ref[...] + (1.0 - ADAM_B2) * (g * g)
        g_out[...] = g
        m_out[...] = mn
        v_out[...] = vn
        d_out[...] = -ADAM_LR * ((mn * c1) / (jnp.sqrt(vn * c2) + ADAM_EPS) + ADAM_WD * w_ref[...])

    blk = pl.BlockSpec((TR, C), lambda i: (i, 0))
    return pl.pallas_call(
        body, name=name, grid=(R // TR,),
        in_specs=[pl.BlockSpec((N_DEV, TR, C), lambda i: (0, i, 0)), blk, blk, blk],
        out_specs=[blk, blk, blk, blk],
        out_shape=[SDS((R, C), f32)] * 4,
        compiler_params=_cp(1, VMEM_LIMIT),
    )(recv, w, m, v)


MESH = pl.DeviceIdType.MESH
_ANY = pl.BlockSpec(memory_space=pl.ANY)


def _flip(v, d):
    return 1 - v if d else v


def _all_gather(shards):
    n = len(shards)

    def body(*refs):
        ins = refs[:n]
        outs = refs[n:2 * n]
        send_sems, recv_sems, local_sems = refs[2 * n:]
        x, y, c = lax.axis_index("x"), lax.axis_index("y"), lax.axis_index("c")
        me, sibling = (x, y, c), (x, y, 1 - c)
        chips = [(1 - x, y), (x, 1 - y), (1 - x, 1 - y)]

        def slot(px, py, pc):
            return 4 * px + 2 * py + pc

        def copy(a, k, block, to, src=None):
            dst = outs[a].at[slot(*block)]
            return pltpu.make_async_remote_copy(src_ref=dst if src is None else src, dst_ref=dst,
                                                send_sem=send_sems.at[a, k], recv_sem=recv_sems.at[a, k],
                                                device_id=to, device_id_type=MESH)

        mine, first, passed = [], [], []
        for a in range(n):
            cp = pltpu.make_async_copy(ins[a], outs[a].at[slot(*me)], local_sems.at[a])
            cp.start()
            mine.append(cp)
            fs = [copy(a, 0, me, sibling, src=ins[a])]
            fs += [copy(a, 1 + j, me, (*chip, c), src=ins[a]) for j, chip in enumerate(chips)]
            for cp in fs:
                cp.start()
            first += fs
        for j, chip in enumerate(chips):
            for a in range(n):
                copy(a, 1 + j, (*chip, c), me).wait_recv()
                cp = copy(a, 4 + j, (*chip, c), sibling)
                cp.start()
                passed.append(cp)
        for a in range(n):
            copy(a, 0, sibling, me).wait_recv()
            for j, chip in enumerate(chips):
                copy(a, 4 + j, (*chip, 1 - c), me).wait_recv()
        for cp in first + passed:
            cp.wait_send()
        for cp in mine:
            cp.wait()

    return pl.pallas_call(
        body, name="weight_all_gather",
        in_specs=[_ANY] * n, out_specs=[_ANY] * n,
        out_shape=[SDS((N_DEV,) + s.shape, s.dtype) for s in shards],
        scratch_shapes=[pltpu.SemaphoreType.DMA((n, 7)), pltpu.SemaphoreType.DMA((n, 7)), pltpu.SemaphoreType.DMA((n,))],
        compiler_params=pltpu.CompilerParams(has_side_effects=True),
    )(*shards)


def _grad_exchange(blocked, whole):
    arrs = list(blocked) + list(whole)
    n, nb = len(arrs), len(blocked)
    rel = [(dx, dy, dc) for dx in (0, 1) for dy in (0, 1) for dc in (0, 1) if dx or dy or dc]

    def body(*refs):
        ins = refs[:n]
        outs = refs[n:2 * n]
        send_sems, recv_sems, local_sems = refs[2 * n:]
        x, y, c = lax.axis_index("x"), lax.axis_index("y"), lax.axis_index("c")
        me = 4 * x + 2 * y + c
        sends, locs = [], []
        for a in range(n):
            cp = pltpu.make_async_copy(ins[a].at[me] if a < nb else ins[a], outs[a].at[me], local_sems.at[a])
            cp.start()
            locs.append(cp)
            for k, (dx, dy, dc) in enumerate(rel):
                peer = (_flip(x, dx), _flip(y, dy), _flip(c, dc))
                pidx = 4 * peer[0] + 2 * peer[1] + peer[2]
                cp = pltpu.make_async_remote_copy(src_ref=ins[a].at[pidx] if a < nb else ins[a], dst_ref=outs[a].at[me],
                                                  send_sem=send_sems.at[a, k], recv_sem=recv_sems.at[a, k],
                                                  device_id=peer, device_id_type=MESH)
                cp.start()
                sends.append(cp)
        for a in range(n):
            for k, (dx, dy, dc) in enumerate(rel):
                peer = (_flip(x, dx), _flip(y, dy), _flip(c, dc))
                pidx = 4 * peer[0] + 2 * peer[1] + peer[2]
                pltpu.make_async_remote_copy(src_ref=outs[a].at[pidx], dst_ref=outs[a].at[pidx],
                                             send_sem=send_sems.at[a, k], recv_sem=recv_sems.at[a, k],
                                             device_id=peer, device_id_type=MESH).wait_recv()
        for cp in sends:
            cp.wait_send()
        for cp in locs:
            cp.wait()

    shapes = [SDS(a.shape, a.dtype) for a in blocked] + [SDS((N_DEV,) + a.shape, a.dtype) for a in whole]
    return pl.pallas_call(
        body, name="grad_exchange",
        in_specs=[_ANY] * n, out_specs=[_ANY] * n, out_shape=shapes,
        scratch_shapes=[pltpu.SemaphoreType.DMA((n, 7)), pltpu.SemaphoreType.DMA((n, 7)), pltpu.SemaphoreType.DMA((n,))],
        compiler_params=pltpu.CompilerParams(has_side_effects=True),
    )(*arrs)


_HBM = pl.BlockSpec(memory_space=pltpu.HBM)
_SEM = pl.BlockSpec(memory_space=pltpu.SEMAPHORE)
_REL = [(dx, dy, dc) for dx in (0, 1) for dy in (0, 1) for dc in (0, 1) if dx or dy or dc]


N_PEER = len(_REL)
_EFFECT = pltpu.SideEffectType.DATAFLOW_SIDE_EFFECTING


def _peer_copies(srcs, lands, send_sems, recv_sems, blocked, as_receiver):
    x, y, c = lax.axis_index("x"), lax.axis_index("y"), lax.axis_index("c")
    me = 4 * x + 2 * y + c
    cps = []
    for a in range(len(srcs)):
        for k, (dx, dy, dc) in enumerate(_REL):
            peer = (_flip(x, dx), _flip(y, dy), _flip(c, dc))
            pidx = 4 * peer[0] + 2 * peer[1] + peer[2]
            cps.append(pltpu.make_async_remote_copy(
                src_ref=srcs[a].at[pidx] if blocked else srcs[a], dst_ref=lands[a].at[pidx if as_receiver else me],
                send_sem=send_sems[a * N_PEER + k], recv_sem=recv_sems[a * N_PEER + k],
                device_id=peer, device_id_type=MESH))
    return cps


def _exchange_start(srcs, after, blocked, name):
    n = len(srcs)
    ns = n * N_PEER
    lands = [lax.empty(s.shape if blocked else (N_DEV,) + s.shape, s.dtype) for s in srcs]

    def body(*refs):
        ins, lnd = refs[:n], refs[n:2 * n]
        outs = refs[2 * n + 1:]
        for cp in _peer_copies(ins, lnd, outs[:ns], outs[ns:2 * ns], blocked, False):
            cp.start()
        outs[-1][...] = jnp.zeros_like(outs[-1])

    res = pl.pallas_call(
        body, name=name,
        in_specs=[_HBM] * (2 * n) + [_ANY],
        out_specs=[_SEM] * (2 * ns) + [_HBM] * (2 * n) + [pl.BlockSpec(memory_space=pltpu.VMEM)],
        out_shape=[pltpu.SemaphoreType.DMA(())] * (2 * ns) + [pltpu.HBM(s.shape, s.dtype) for s in srcs]
                  + [pltpu.HBM(l.shape, l.dtype) for l in lands] + [SDS((8, LANE), f32)],
        input_output_aliases={i: 2 * ns + i for i in range(2 * n)},
        compiler_params=pltpu.CompilerParams(has_side_effects=_EFFECT),
    )(*[pltpu.with_memory_space_constraint(s, pltpu.HBM) for s in srcs],
      *[pltpu.with_memory_space_constraint(l, pltpu.HBM) for l in lands], after)
    return list(res[:2 * ns]), list(res[2 * ns:2 * ns + n]), list(res[2 * ns + n:2 * ns + 2 * n]), res[-1]


def _exchange_wait(sems, srcs, lands, after, blocked, name):
    n = len(srcs)
    ns = n * N_PEER

    def body(*refs):
        ins, lnd = refs[:n], refs[n:2 * n]
        sem_refs = refs[2 * n:2 * n + 2 * ns]
        for cp in _peer_copies(ins, lnd, sem_refs[:ns], sem_refs[ns:], blocked, True):
            cp.wait_send()
            cp.wait_recv()

    res = pl.pallas_call(
        body, name=name,
        in_specs=[_HBM] * (2 * n) + [_SEM] * (2 * ns) + [_ANY],
        out_specs=[_HBM] * (2 * n),
        out_shape=[pltpu.HBM(s.shape, s.dtype) for s in srcs] + [pltpu.HBM(l.shape, l.dtype) for l in lands],
        input_output_aliases={i: i for i in range(2 * n)},
        compiler_params=pltpu.CompilerParams(has_side_effects=_EFFECT),
    )(*srcs, *lands, *sems, after)
    return list(res[n:])


def _local_step(x, tgt, wcat, convw, late_weights, early_grads, last_grads, token, a_log, dt_bias, onorm_g, rel_bias,
                g_mix_pre, g_mix_post, g_ffn_pre, g_ffn_post):
    S = x.shape[0]
    bk_np = _bucket_tables()
    bk = jnp.asarray(bk_np)
    bt = _bias_tables(rel_bias, bk)
    proj, h1 = _proj_fwd(x, g_mix_pre, wcat, token, S)
    nu = S // CHUNK * GDN_HEADS
    qkv_u = _gdn_prep(proj, convw, S).reshape(3, nu, CHUNK, GDN_HD)
    intra, t_inv = _gdn_intra_fwd(qkv_u, proj, a_log, dt_bias, S)
    oa, states = _gdn_scan_fwd(intra, proj, onorm_g, S)
    ob, lse0, lse1 = _swa_fwd(proj, bt, S)
    wout, wgate, wup, wdown = late_weights(ob)
    mix, x1 = _mix_fwd(oa, ob, wout, x, g_mix_post, S)
    dx1, h2, act, dgate_f, dup_f, df, loss, d_gfpre, d_gfpost = _ffn(x1, tgt, g_ffn_pre, g_ffn_post, wgate, wup, wdown, S)
    g_gate = _wgrad(h2, dgate_f, S, "wgrad_gate", col_blocks=True)
    g_up = _wgrad(h2, dup_f, S, "wgrad_up", col_blocks=True)
    g_down = _wgrad(act, df, S, "wgrad_down").reshape(N_DEV, D_FF // N_DEV, D_MODEL)
    dmix, d_oab, d_gmpost = _mix_bwd(dx1, mix, g_mix_post, wout, S)
    g_out = jnp.concatenate([_wgrad(oa, dmix, S, "wgrad_out_a"), _wgrad(ob, dmix, S, "wgrad_out_b")], axis=0)
    token = early_grads(g_out.reshape(N_DEV, D_MODEL // N_DEV, D_MODEL), g_gate, g_up, g_down)
    dqb, dkb, dvb, dsb = _swa_bwd(proj, bt, ob, lse0, lse1, d_oab, token, S)
    *cots, dgate_a, d_og = _gdn_scan_bwd(intra, states, proj, d_oab, onorm_g, token, S)
    dqkv_u, dpg, d_alog, d_dtb = _gdn_intra_bwd(qkv_u, proj, a_log, dt_bias, t_inv, cots, S)
    dqkv_a, d_conv = _gdn_prep_bwd(proj, convw, dqkv_u.reshape(3, S // CHUNK, GDN_HEADS, CHUNK, GDN_HD), S)
    segs = [(dqkv_a, COL_A), (dgate_a, COL_A + 3 * GDN_W), (dqb, COL_B), (dkb, COL_B + SWA_W), (dvb, COL_B + 2 * SWA_W),
            (dpg, COL_G)]
    token = last_grads(_wgrad_in(h1, segs, S), d_conv)
    grad_x, d_gmpre = _proj_bwd(x, dx1, g_mix_pre, wcat, segs, token, S)
    d_rel = _rel_bias_grad(dsb, bk, bk_np)
    small = dict(a_log=d_alog[:, :GDN_HEADS], dt_bias=d_dtb[:, :GDN_HEADS], onorm_g=d_og, rel_bias=d_rel,
                 g_mix_pre=d_gmpre, g_mix_post=d_gmpost, g_ffn_pre=d_gfpre, g_ffn_post=d_gfpost)
    return loss, grad_x, small


SMALL = ("a_log", "dt_bias", "onorm_g", "rel_bias", "g_mix_pre", "g_mix_post", "g_ffn_pre", "g_ffn_post")
PACK_ROWS = 8


def _pack_small(d, loss=None):
    rest = jnp.concatenate([d["onorm_g"].reshape(-1), d["a_log"].reshape(-1), d["dt_bias"].reshape(-1),
                            d["rel_bias"].reshape(-1)])
    rest = jnp.concatenate([rest, jnp.zeros((D_MODEL - rest.shape[0],), f32)])
    extra = jnp.zeros((D_MODEL,), f32) if loss is None else jnp.concatenate([loss.reshape(1), jnp.zeros((D_MODEL - 1,), f32)])
    rows = [d["g_mix_pre"].reshape(-1), d["g_mix_post"].reshape(-1), d["g_ffn_pre"].reshape(-1),
            d["g_ffn_post"].reshape(-1), rest, extra]
    return jnp.concatenate([jnp.stack(rows), jnp.zeros((PACK_ROWS - len(rows), D_MODEL), f32)], axis=0)


def _unpack_small(p):
    o = GDN_HD
    return dict(g_mix_pre=p[0:1], g_mix_post=p[1:2], g_ffn_pre=p[2:3], g_ffn_post=p[3:4],
                onorm_g=p[4:5, :o], a_log=p[4:5, o:o + 4], dt_bias=p[4:5, o + 4:o + 8],
                rel_bias=p[4, o + 8:o + 8 + NUM_BUCKETS * SWA_HEADS].reshape(NUM_BUCKETS, SWA_HEADS))


def kernel(x, w_in, conv_w, a_log, dt_bias, onorm_g, rel_bias, w_out, g_mix_pre, g_mix_post, w_gate, w_up, w_down, g_ffn_pre, g_ffn_post, loss_target, m_w_in, m_conv_w, m_a_log, m_dt_bias, m_onorm_g, m_rel_bias, m_w_out, m_g_mix_pre, m_g_mix_post, m_w_gate, m_w_up, m_w_down, m_g_ffn_pre, m_g_ffn_post, v_w_in, v_conv_w, v_a_log, v_dt_bias, v_onorm_g, v_rel_bias, v_w_out, v_g_mix_pre, v_g_mix_post, v_w_gate, v_w_up, v_w_down, v_g_ffn_pre, v_g_ffn_post):
    big = ("w_in", "conv_w", "w_out", "w_gate", "w_up", "w_down")
    w_sh = dict(w_in=w_in[0], conv_w=conv_w[0], w_out=w_out[0], w_gate=w_gate[0], w_up=w_up[0], w_down=w_down[0])
    m_sh = dict(w_in=m_w_in[0], conv_w=m_conv_w[0], w_out=m_w_out[0], w_gate=m_w_gate[0], w_up=m_w_up[0], w_down=m_w_down[0])
    v_sh = dict(w_in=v_w_in[0], conv_w=v_conv_w[0], w_out=v_w_out[0], w_gate=v_w_gate[0], w_up=v_w_up[0], w_down=v_w_down[0])
    w_small = dict(a_log=a_log, dt_bias=dt_bias, onorm_g=onorm_g, rel_bias=rel_bias, g_mix_pre=g_mix_pre,
                   g_mix_post=g_mix_post, g_ffn_pre=g_ffn_pre, g_ffn_post=g_ffn_post)
    m_small = dict(a_log=m_a_log, dt_bias=m_dt_bias, onorm_g=m_onorm_g, rel_bias=m_rel_bias, g_mix_pre=m_g_mix_pre,
                   g_mix_post=m_g_mix_post, g_ffn_pre=m_g_ffn_pre, g_ffn_post=m_g_ffn_post)
    v_small = dict(a_log=v_a_log, dt_bias=v_dt_bias, onorm_g=v_onorm_g, rel_bias=v_rel_bias, g_mix_pre=v_g_mix_pre,
                   g_mix_post=v_g_mix_post, g_ffn_pre=v_g_ffn_pre, g_ffn_post=v_g_ffn_post)

    me = 4 * lax.axis_index("x") + 2 * lax.axis_index("y") + lax.axis_index("c")
    own = lambda full, part: lax.dynamic_update_index_in_dim(full, part, me, 0)
    cols = lambda g: g.reshape(g.shape[0], N_DEV, g.shape[1] // N_DEV).transpose(1, 0, 2)
    late = ("w_out", "w_gate", "w_up", "w_down")

    late_src = [w_sh[k].astype(bf16) for k in late]
    g_in, g_conv = _all_gather([w_sh["w_in"].astype(bf16), w_sh["conv_w"]])
    g_sems, g_src, g_land, g_token = _exchange_start(late_src, g_conv, False, "late_weights_start")
    wcat = _wcat_from_blocks(g_in)
    convw = g_conv.transpose(1, 0, 2).reshape(4, 3 * GDN_W)

    def late_weights(after):
        lands = _exchange_wait(g_sems, g_src, g_land, after, False, "late_weights_wait")
        g_out, g_gate, g_up, g_down = [own(l, s) for l, s in zip(lands, late_src)]
        return (g_out.reshape(D_MODEL, D_MODEL), _full_from_col_blocks(g_gate), _full_from_col_blocks(g_up),
                g_down.reshape(D_FF, D_MODEL))

    early, last = {}, {}

    def early_grads(*blocks):
        early["sems"], early["src"], early["land"], token = _exchange_start(list(blocks), blocks[0], True, "late_grads_start")
        return token

    def last_grads(gw_in, gw_conv):
        src = [gw_in, cols(gw_conv)]
        last["sems"], last["src"], last["land"], token = _exchange_start(src, gw_in, True, "last_grads_start")
        return token

    loss_p, grad_x, gsmall = _local_step(
        x[0], loss_target[0], wcat, convw, late_weights, early_grads, last_grads, g_token,
        a_log, dt_bias, onorm_g, rel_bias, g_mix_pre, g_mix_post, g_ffn_pre, g_ffn_post)

    (r_small,) = _grad_exchange([], [_pack_small(gsmall, loss_p[0, 0])])
    recv = {}
    for names, ex, after, name in ((late, early, grad_x, "late_grads_wait"), (("w_in", "conv_w"), last, r_small, "last_grads_wait")):
        lands = _exchange_wait(ex["sems"], ex["src"], ex["land"], after, True, name)
        for k, l, s in zip(names, lands, ex["src"]):
            recv[k] = own(l, lax.dynamic_index_in_dim(s, me, 0, keepdims=False))

    outs = {}
    for k in big:
        outs[k] = _adamw(recv[k], w_sh[k], m_sh[k], v_sh[k], "adamw_" + k)
    sm = _adamw(r_small, _pack_small(w_small), _pack_small(m_small), _pack_small(v_small), "adamw_small")
    loss = sm[0][5, 0]
    sm = [_unpack_small(t) for t in sm]
    for k in SMALL:
        outs[k] = tuple(t[k].reshape(w_small[k].shape) for t in sm)

    order = ("w_in", "conv_w", "a_log", "dt_bias", "onorm_g", "rel_bias", "w_out", "g_mix_pre", "g_mix_post", "w_gate",
             "w_up", "w_down", "g_ffn_pre", "g_ffn_post")
    lead = lambda k, t: t[None] if k in big else t
    res = [loss, grad_x[None]]
    for i in range(4):
        res += [lead(k, outs[k][i]) for k in order]
    return tuple(res)
```

```python
import functools
import math

import numpy as np
import jax
import jax.numpy as jnp
from jax import lax
from jax.experimental import pallas as pl
from jax.experimental.pallas import tpu as pltpu

f32 = jnp.float32
bf16 = jnp.bfloat16
SDS = jax.ShapeDtypeStruct

D_MODEL = 1024
GDN_HEADS = 4
GDN_HD = 128
GDN_W = 512
CHUNK = 64
SWA_HEADS = 8
SWA_HD = 64
SWA_W = 512
D_FF = 2816
IN_COLS = 3592
PATTERNS = ((128, 1), (512, 4), (2048, 16))
SWA_BLK = 128
NUM_BUCKETS = 32
MAX_DISTANCE = 2048
RMS_EPS = 1e-6
NEG = -1e30
N_DEV = 8

COL_A = 0
COL_B = 2048
COL_G = 3584
NCOL = 3712
LANE = 128

ADAM_LR, ADAM_B1, ADAM_B2, ADAM_EPS, ADAM_WD, ADAM_STEP = 0.001, 0.9, 0.999, 1e-08, 0.01, 10

VMEM_LIMIT = 56 * 1024 * 1024

HI = lax.Precision.HIGHEST
HIGH = lax.Precision.HIGH


def _cp(n_grid=0, vmem=None):
    kw = {}
    if n_grid:
        kw["dimension_semantics"] = ("arbitrary",) * n_grid
    if vmem:
        kw["vmem_limit_bytes"] = vmem
    return pltpu.CompilerParams(**kw)


def _dot(a, b):
    return jnp.dot(a, b, preferred_element_type=f32)


def _dot_nt(a, b):
    return lax.dot_general(a, b, (((1,), (1,)), ((), ())), preferred_element_type=f32)


def _dot_tn(a, b):
    return lax.dot_general(a, b, (((0,), (0,)), ((), ())), preferred_element_type=f32)


def _dot_hi(a, b):
    return jnp.dot(a, b, precision=HI, preferred_element_type=f32)


def _sigmoid(x):
    return 1.0 / (1.0 + jnp.exp(-x))


def _softplus(x):
    return jnp.maximum(x, 0.0) + jnp.log(1.0 + jnp.exp(-jnp.abs(x)))


def _const_spec(shape):
    nd = len(shape)
    return pl.BlockSpec(shape, lambda *_: (0,) * nd)


def _resident_spec(shape):
    nd = len(shape)
    return pl.BlockSpec(shape, lambda *_: (0,) * nd, pipeline_mode=pl.Buffered(1))


def _t5_bucket_np(dist):
    max_exact = NUM_BUCKETS // 2
    d = np.maximum(dist, 1).astype(np.float32)
    log_b = max_exact + (np.log(d / np.float32(max_exact)) / np.float32(math.log(MAX_DISTANCE / max_exact))
                         * np.float32(NUM_BUCKETS - max_exact)).astype(np.int32)
    return np.where(dist < max_exact, dist, np.minimum(log_b, NUM_BUCKETS - 1)).astype(np.int32)


def _bucket_tables():
    w = SWA_BLK
    qi = np.arange(w)[:, None]
    kj = np.arange(w)[None, :]
    rel = np.where(kj <= qi, qi - kj, qi + w - kj)
    out = np.zeros((len(PATTERNS), w, w), np.int32)
    for p, (_, dil) in enumerate(PATTERNS):
        steps = _t5_bucket_np(np.arange(w + 1) * dil)
        assert steps[w] == steps[w - 1]
        out[p] = steps[rel]
    return out


def _bias_tables(rel_bias, bk):
    def body(rb_ref, bk_ref, o_ref):
        b_idx = bk_ref[0]
        for h in range(SWA_HEADS):
            def lp(b, acc):
                return jnp.where(b_idx == b, rb_ref[b, h], acc)
            o_ref[0, h] = lax.fori_loop(0, NUM_BUCKETS, lp, jnp.zeros((SWA_BLK, SWA_BLK), f32))

    return pl.pallas_call(
        body, name="bias_tables", grid=(3,),
        in_specs=[pl.BlockSpec(memory_space=pltpu.SMEM), pl.BlockSpec((1, SWA_BLK, SWA_BLK), lambda p: (p, 0, 0))],
        out_specs=pl.BlockSpec((1, SWA_HEADS, SWA_BLK, SWA_BLK), lambda p: (p, 0, 0, 0)),
        out_shape=SDS((3, SWA_HEADS, SWA_BLK, SWA_BLK), f32),
        compiler_params=_cp(1),
    )(rel_bias, bk)


def _rel_bias_grad(dsb, bk, bk_np):
    present = [sorted(set(int(v) for v in np.unique(bk_np[p]))) for p in range(3)]

    def body(ds_ref, bk_ref, o_ref):
        row = lax.broadcasted_iota(jnp.int32, (NUM_BUCKETS, LANE), 0)
        col = lax.broadcasted_iota(jnp.int32, (NUM_BUCKETS, SWA_HEADS), 1)
        out = jnp.zeros((NUM_BUCKETS, SWA_HEADS), f32)
        for hp in range(4):
            for hh in range(2):
                acc = jnp.zeros((NUM_BUCKETS, LANE), f32)
                for p in range(3):
                    tile = ds_ref[hp, p, hh]
                    b_idx = bk_ref[p]
                    for b in present[p]:
                        part = jnp.sum(jnp.where(b_idx == b, tile, 0.0), axis=0, keepdims=True)
                        acc = acc + jnp.where(row == b, part, 0.0)
                tot = jnp.sum(acc, axis=1, keepdims=True)
                out = out + jnp.where(col == 2 * hp + hh, tot, 0.0)
        o_ref[...] = out

    return pl.pallas_call(body, name="rel_bias_grad", out_shape=SDS((NUM_BUCKETS, SWA_HEADS), f32),
                          compiler_params=_cp(0, 32 * 1024 * 1024))(dsb, bk)


def _proj_fwd(x, g_pre, wcat, after, S):
    TS = 256

    def body(x_ref, g_ref, w_ref, after_ref, o_ref, h_ref):
        xv = x_ref[...]
        r = lax.rsqrt(jnp.mean(xv * xv, axis=-1, keepdims=True) + RMS_EPS)
        h = (xv * r * g_ref[...]).astype(bf16)
        h_ref[...] = h
        o_ref[...] = _dot(h, w_ref[...])

    return pl.pallas_call(
        body, name="proj_fwd", grid=(S // TS,),
        in_specs=[pl.BlockSpec((TS, D_MODEL), lambda i: (i, 0)), _const_spec((1, D_MODEL)),
                  _resident_spec((D_MODEL, NCOL)), _ANY],
        out_specs=[pl.BlockSpec((TS, NCOL), lambda i: (i, 0)), pl.BlockSpec((TS, D_MODEL), lambda i: (i, 0))],
        out_shape=[SDS((S, NCOL), f32), SDS((S, D_MODEL), bf16)],
        compiler_params=_cp(1, VMEM_LIMIT),
    )(x, g_pre, wcat, after)


CONV_RT = 256
HALO = 8


CONV_NC = CONV_RT // CHUNK


def _gdn_prep(proj, conv_w, S):
    def body(p_ref, cw_ref, o_ref, xs_ref):
        t = pl.program_id(0)
        xs_ref[pl.ds(0, HALO), :] = jnp.zeros((HALO, LANE), f32)
        xs_ref[pl.ds(HALO, S), :] = p_ref[...]
        w = cw_ref[...]
        is_qk = t < 2
        scale = jnp.where(t == 0, GDN_HD ** -0.5, 1.0).astype(f32)

        def lp(c, carry):
            st = pl.multiple_of(c * CONV_RT, CONV_RT)
            pre = xs_ref[pl.ds(st + HALO - 3, CONV_RT), :] * w[0:1, :]
            for i in range(1, 4):
                pre = pre + xs_ref[pl.ds(st + HALO - 3 + i, CONV_RT), :] * w[i:i + 1, :]
            s = pre * _sigmoid(pre)
            nrm = s * lax.rsqrt(jnp.sum(s * s, axis=-1, keepdims=True) + 1e-6) * scale
            out = jnp.where(is_qk, nrm, s)
            for i in range(CONV_NC):
                o_ref[0, c * CONV_NC + i, 0] = out[i * CHUNK:(i + 1) * CHUNK]
            return carry

        lax.fori_loop(0, S // CONV_RT, lp, 0)

    return pl.pallas_call(
        body, name="gdn_prep", grid=(3, GDN_HEADS),
        in_specs=[pl.BlockSpec((S, LANE), lambda t, h: (0, t * GDN_HEADS + h)),
                  pl.BlockSpec((4, LANE), lambda t, h: (0, t * GDN_HEADS + h))],
        out_specs=pl.BlockSpec((1, S // CHUNK, 1, CHUNK, GDN_HD), lambda t, h: (t, 0, h, 0, 0)),
        out_shape=SDS((3, S // CHUNK, GDN_HEADS, CHUNK, GDN_HD), f32),
        scratch_shapes=[pltpu.VMEM((S + HALO, LANE), f32)],
        compiler_params=_cp(2, VMEM_LIMIT),
    )(proj, conv_w)


def _gdn_prep_bwd(proj, conv_w, dqkv, S):
    def body(p_ref, cw_ref, d_ref, dx_ref, dw_ref, xs_ref, dp_ref):
        t = pl.program_id(0)
        xs_ref[pl.ds(0, HALO), :] = jnp.zeros((HALO, LANE), f32)
        xs_ref[pl.ds(HALO, S), :] = p_ref[...]
        dp_ref[pl.ds(S, HALO), :] = jnp.zeros((HALO, LANE), f32)
        w = cw_ref[...]
        is_qk = t < 2
        scale = jnp.where(t == 0, GDN_HD ** -0.5, 1.0).astype(f32)

        def lp1(c, dw):
            st = pl.multiple_of(c * CONV_RT, CONV_RT)
            taps = [xs_ref[pl.ds(st + HALO - 3 + i, CONV_RT), :] for i in range(4)]
            pre = taps[0] * w[0:1, :]
            for i in range(1, 4):
                pre = pre + taps[i] * w[i:i + 1, :]
            sg = _sigmoid(pre)
            s = pre * sg
            d_out = jnp.concatenate([d_ref[0, c * CONV_NC + i, 0] for i in range(CONV_NC)], axis=0)
            rn = lax.rsqrt(jnp.sum(s * s, axis=-1, keepdims=True) + 1e-6)
            n = s * rn
            dn = d_out * scale
            ds_qk = rn * (dn - n * jnp.sum(dn * n, axis=-1, keepdims=True))
            ds = jnp.where(is_qk, ds_qk, d_out)
            dpre = ds * (sg * (1.0 + pre * (1.0 - sg)))
            dp_ref[pl.ds(st, CONV_RT), :] = dpre
            return tuple(dw[i] + jnp.sum(dpre * taps[i], axis=0, keepdims=True) for i in range(4))

        z = jnp.zeros((1, LANE), f32)
        dw = lax.fori_loop(0, S // CONV_RT, lp1, (z, z, z, z))
        for i in range(4):
            dw_ref[pl.ds(i, 1), :] = dw[i]

        def lp2(c, carry):
            st = pl.multiple_of(c * CONV_RT, CONV_RT)
            dx = dp_ref[pl.ds(st, CONV_RT), :] * w[3:4, :]
            for i in range(3):
                dx = dx + dp_ref[pl.ds(st + 3 - i, CONV_RT), :] * w[i:i + 1, :]
            dx_ref[pl.ds(st, CONV_RT), :] = dx.astype(bf16)
            return carry

        lax.fori_loop(0, S // CONV_RT, lp2, 0)

    col = lambda rows: pl.BlockSpec((rows, LANE), lambda t, h: (0, t * GDN_HEADS + h))
    return pl.pallas_call(
        body, name="gdn_prep_bwd", grid=(3, GDN_HEADS),
        in_specs=[col(S), col(4), pl.BlockSpec((1, S // CHUNK, 1, CHUNK, GDN_HD), lambda t, h: (t, 0, h, 0, 0))],
        out_specs=[col(S), col(4)],
        out_shape=[SDS((S, 3 * GDN_W), bf16), SDS((4, 3 * GDN_W), f32)],
        scratch_shapes=[pltpu.VMEM((S + HALO, LANE), f32), pltpu.VMEM((S + HALO, LANE), f32)],
        compiler_params=_cp(2, VMEM_LIMIT),
    )(proj, conv_w, dqkv)


def _bdot(a, b, prec=None):
    return lax.dot_general(a, b, (((2,), (1,)), ((0,), (0,))), precision=prec, preferred_element_type=f32)


def _bdot_nt(a, b, prec=None):
    return lax.dot_general(a, b, (((2,), (2,)), ((0,), (0,))), precision=prec, preferred_element_type=f32)


def _bdot_tn(a, b, prec=None):
    return lax.dot_general(a, b, (((1,), (1,)), ((0,), (0,))), precision=prec, preferred_element_type=f32)


@jax.custom_vjp
def _tri_inv_saved(a, t):
    return t


def _tri_inv_saved_fwd(a, t):
    return t, t


def _tri_inv_saved_bwd(t, dt):
    return -_bdot_tn(t, _bdot_nt(dt, t, HIGH), HIGH), jnp.zeros_like(t)


_tri_inv_saved.defvjp(_tri_inv_saved_fwd, _tri_inv_saved_bwd)


def _gdn_intra(q, k, v, beta, g, t_saved=None):
    nb = q.shape[0]
    c = CHUNK
    ii = lax.broadcasted_iota(jnp.int32, (c, c), 0)
    jj = lax.broadcasted_iota(jnp.int32, (c, c), 1)
    eye = ii == jj
    tril = ii >= jj
    strict = ii > jj
    ones = jnp.ones((nb, c, c), f32)
    eye_f = eye.astype(f32)

    g_row = _bdot(ones, jnp.where(eye, g, 0.0), HI)
    gc = jnp.sum(jnp.where(tril, g_row, 0.0), axis=2, keepdims=True)
    gc_row = _bdot(ones, jnp.where(eye, gc, 0.0), HI)
    decay = jnp.where(tril, jnp.exp(jnp.where(tril, gc - gc_row, 0.0)), 0.0)
    last = lax.broadcasted_iota(jnp.int32, (c, 1), 0) == c - 1
    gc_last = jnp.sum(jnp.where(last, gc, 0.0), axis=1, keepdims=True)
    e_gc = jnp.exp(gc)

    kb = k * beta
    k16 = k.astype(bf16)
    a = jnp.where(strict, _bdot_nt(kb.astype(bf16), k16) * decay, 0.0)
    if t_saved is None:
        xp = -a
        t_inv = eye_f + xp
        for _ in range(5):
            xp = _bdot(xp, xp, HIGH)
            t_inv = _bdot(t_inv, eye_f + xp, HIGH)
    else:
        t_inv = _tri_inv_saved(a, t_saved)
    t16 = t_inv.astype(bf16)
    u = _bdot(t16, (v * beta).astype(bf16))
    w = _bdot(t16, (kb * e_gc).astype(bf16))
    attn = jnp.where(tril, _bdot_nt(q.astype(bf16), k16) * decay, 0.0)
    gam = jnp.broadcast_to(jnp.exp(gc_last), (nb, 1, GDN_HD))
    return u, w, attn, q * e_gc, k * jnp.exp(gc_last - gc), gam, t_inv


GDN_TB = 256
GDN_NC = GDN_TB // CHUNK
GDN_NU = GDN_NC * GDN_HEADS


def _gdn_gates(pg_ref, al_ref, db_ref):
    lane1 = lax.broadcasted_iota(jnp.int32, (1, LANE), 1)
    a_lane = jnp.zeros((1, LANE), f32)
    b_lane = jnp.zeros((1, LANE), f32)
    for h in range(GDN_HEADS):
        a_lane = jnp.where(lane1 == GDN_HEADS + h, al_ref[0, h], a_lane)
        b_lane = jnp.where(lane1 == GDN_HEADS + h, db_ref[0, h], b_lane)
    pg = pg_ref[...]
    z = pg + b_lane
    return _sigmoid(pg), -jnp.exp(a_lane) * _softplus(z), z, a_lane


def _gdn_unit_inputs(qkv_ref, beta_all, g_all):
    units = [(cl, h) for cl in range(GDN_NC) for h in range(GDN_HEADS)]
    beta = jnp.stack([beta_all[cl * CHUNK:(cl + 1) * CHUNK, h:h + 1] for cl, h in units])
    g = jnp.stack([g_all[cl * CHUNK:(cl + 1) * CHUNK, GDN_HEADS + h:GDN_HEADS + h + 1] for cl, h in units])
    return qkv_ref[0], qkv_ref[1], qkv_ref[2], beta, g


def _unit_spec(*tail):
    nd = len(tail)
    return pl.BlockSpec((GDN_NU,) + tail, lambda i: (i,) + (0,) * nd)


def _gdn_intra_shapes(S):
    nu = S // CHUNK * GDN_HEADS
    row = SDS((nu, CHUNK, GDN_HD), f32)
    return [row, row, SDS((nu, CHUNK, CHUNK), f32), row, row, SDS((nu, 1, GDN_HD), f32)]


_GDN_INTRA_SPECS = lambda: [_unit_spec(CHUNK, GDN_HD), _unit_spec(CHUNK, GDN_HD), _unit_spec(CHUNK, CHUNK),
                            _unit_spec(CHUNK, GDN_HD), _unit_spec(CHUNK, GDN_HD), _unit_spec(1, GDN_HD)]


def _gdn_intra_fwd(qkv_u, proj, a_log, dt_bias, S):
    def body(qkv_ref, pg_ref, al_ref, db_ref, *outs):
        beta_all, g_all, _, _ = _gdn_gates(pg_ref, al_ref, db_ref)
        res = _gdn_intra(*_gdn_unit_inputs(qkv_ref, beta_all, g_all))
        for o_ref, r in zip(outs, res):
            o_ref[...] = r

    nu = S // CHUNK * GDN_HEADS
    *intra, t_inv = pl.pallas_call(
        body, name="gdn_intra_fwd", grid=(S // GDN_TB,),
        in_specs=[pl.BlockSpec((3, GDN_NU, CHUNK, GDN_HD), lambda i: (0, i, 0, 0)),
                  pl.BlockSpec((GDN_TB, LANE), lambda i: (i, COL_G // LANE)),
                  pl.BlockSpec(memory_space=pltpu.SMEM), pl.BlockSpec(memory_space=pltpu.SMEM)],
        out_specs=_GDN_INTRA_SPECS() + [_unit_spec(CHUNK, CHUNK)],
        out_shape=_gdn_intra_shapes(S) + [SDS((nu, CHUNK, CHUNK), f32)],
        compiler_params=_cp(1, VMEM_LIMIT),
    )(qkv_u, proj, a_log, dt_bias)
    return intra, t_inv


def _gdn_intra_bwd(qkv_u, proj, a_log, dt_bias, t_inv, cots, S):
    def body(qkv_ref, pg_ref, al_ref, db_ref, t_ref, du_ref, dw_ref, da_ref, dqd_ref, dkd_ref, dgm_ref,
             dqkv_ref, dpg_ref, dal_ref, ddb_ref):
        @pl.when(pl.program_id(0) == 0)
        def _():
            dal_ref[...] = jnp.zeros_like(dal_ref)
            ddb_ref[...] = jnp.zeros_like(ddb_ref)

        t_saved = t_ref[...]
        beta_all, g_all, z, a_lane = _gdn_gates(pg_ref, al_ref, db_ref)
        _, vjp = jax.vjp(lambda *a: _gdn_intra(*a, t_saved=t_saved)[:6], *_gdn_unit_inputs(qkv_ref, beta_all, g_all))
        dq, dk, dv, dbeta, dg = vjp((du_ref[...], dw_ref[...], da_ref[...], dqd_ref[...], dkd_ref[...], dgm_ref[...]))
        dqkv_ref[0] = dq
        dqkv_ref[1] = dk
        dqkv_ref[2] = dv
        lane = lax.broadcasted_iota(jnp.int32, (CHUNK, LANE), 1)
        rows = []
        for cl in range(GDN_NC):
            t = jnp.zeros((CHUNK, LANE), f32)
            for h in range(GDN_HEADS):
                b = cl * GDN_HEADS + h
                t = t + jnp.where(lane == h, dbeta[b], 0.0) + jnp.where(lane == GDN_HEADS + h, dg[b], 0.0)
            rows.append(t)
        d_all = jnp.concatenate(rows, axis=0)
        is_beta = lax.broadcasted_iota(jnp.int32, (GDN_TB, LANE), 1) < GDN_HEADS
        dz = d_all * (-jnp.exp(a_lane)) * _sigmoid(z)
        dpg_ref[...] = jnp.where(is_beta, d_all * beta_all * (1.0 - beta_all), dz).astype(bf16)
        dal_ref[...] += jnp.sum(jnp.where(is_beta, 0.0, d_all * g_all), axis=0, keepdims=True)
        ddb_ref[...] += jnp.sum(jnp.where(is_beta, 0.0, dz), axis=0, keepdims=True)

    acc = _const_spec((1, LANE))
    nu = S // CHUNK * GDN_HEADS
    return pl.pallas_call(
        body, name="gdn_intra_bwd", grid=(S // GDN_TB,),
        in_specs=[pl.BlockSpec((3, GDN_NU, CHUNK, GDN_HD), lambda i: (0, i, 0, 0)),
                  pl.BlockSpec((GDN_TB, LANE), lambda i: (i, COL_G // LANE)),
                  pl.BlockSpec(memory_space=pltpu.SMEM), pl.BlockSpec(memory_space=pltpu.SMEM),
                  _unit_spec(CHUNK, CHUNK)] + _GDN_INTRA_SPECS(),
        out_specs=[pl.BlockSpec((3, GDN_NU, CHUNK, GDN_HD), lambda i: (0, i, 0, 0)),
                   pl.BlockSpec((GDN_TB, LANE), lambda i: (i, 0)), acc, acc],
        out_shape=[SDS((3, nu, CHUNK, GDN_HD), f32), SDS((S, LANE), bf16), SDS((1, LANE), f32), SDS((1, LANE), f32)],
        compiler_params=_cp(1, VMEM_LIMIT),
    )(qkv_u, proj, a_log, dt_bias, t_inv, *cots)


def _gdn_scan_fwd(intra, proj, onorm_g, S):
    def body(u_ref, w_ref, at_ref, qd_ref, kd_ref, gm_ref, gate_ref, og_ref, out_ref, st_ref, s_scr):
        @pl.when(pl.program_id(0) == 0)
        def _():
            s_scr[...] = jnp.zeros_like(s_scr)

        og = og_ref[...]
        s = s_scr[...]
        for cl in range(GDN_NC):
            us = slice(cl * GDN_HEADS, (cl + 1) * GDN_HEADS)
            rows = slice(cl * CHUNK, (cl + 1) * CHUNK)
            st_ref[us] = s
            s16 = s.astype(bf16)
            vn = u_ref[us] - _bdot(w_ref[us].astype(bf16), s16)
            vn16 = vn.astype(bf16)
            o = _bdot(qd_ref[us].astype(bf16), s16) + _bdot(at_ref[us].astype(bf16), vn16)
            s = s * gm_ref[us] + _bdot_tn(kd_ref[us].astype(bf16), vn16)
            for h in range(GDN_HEADS):
                oh = o[h]
                gt = gate_ref[rows, h * GDN_HD:(h + 1) * GDN_HD]
                on = oh * lax.rsqrt(jnp.mean(oh * oh, axis=-1, keepdims=True) + RMS_EPS) * og
                out_ref[rows, h * GDN_HD:(h + 1) * GDN_HD] = on * (gt * _sigmoid(gt))
        s_scr[...] = s

    nu = S // CHUNK * GDN_HEADS
    return pl.pallas_call(
        body, name="gdn_scan_fwd", grid=(S // GDN_TB,),
        in_specs=_GDN_INTRA_SPECS() + [pl.BlockSpec((GDN_TB, GDN_W), lambda i: (i, 3)), _const_spec((1, GDN_HD))],
        out_specs=[pl.BlockSpec((GDN_TB, GDN_W), lambda i: (i, 0)), _unit_spec(GDN_HD, GDN_HD)],
        out_shape=[SDS((S, GDN_W), f32), SDS((nu, GDN_HD, GDN_HD), f32)],
        scratch_shapes=[pltpu.VMEM((GDN_HEADS, GDN_HD, GDN_HD), f32)],
        compiler_params=_cp(1, VMEM_LIMIT),
    )(*intra, proj, onorm_g)


def _gdn_scan_bwd(intra, states, proj, d_oab, onorm_g, after, S):
    n_steps = S // GDN_TB

    def body(u_ref, w_ref, at_ref, qd_ref, kd_ref, gm_ref, st_ref, gate_ref, do_ref, og_ref, after_ref,
             du_ref, dw_ref, dat_ref, dqd_ref, dkd_ref, dgm_ref, dgate_ref, dog_ref, ds_scr):
        @pl.when(pl.program_id(0) == 0)
        def _():
            ds_scr[...] = jnp.zeros_like(ds_scr)
            dog_ref[...] = jnp.zeros_like(dog_ref)

        og = og_ref[...]
        ii = lax.broadcasted_iota(jnp.int32, (CHUNK, CHUNK), 0)
        jj = lax.broadcasted_iota(jnp.int32, (CHUNK, CHUNK), 1)
        tril = ii >= jj
        ds = ds_scr[...]
        dog = jnp.zeros((1, GDN_HD), f32)
        for cl in reversed(range(GDN_NC)):
            us = slice(cl * GDN_HEADS, (cl + 1) * GDN_HEADS)
            rows = slice(cl * CHUNK, (cl + 1) * CHUNK)
            s0 = st_ref[us]
            s016 = s0.astype(bf16)
            w16 = w_ref[us].astype(bf16)
            qd16 = qd_ref[us].astype(bf16)
            kd16 = kd_ref[us].astype(bf16)
            at16 = at_ref[us].astype(bf16)
            vn = u_ref[us] - _bdot(w16, s016)
            vn16 = vn.astype(bf16)
            o = _bdot(qd16, s016) + _bdot(at16, vn16)
            do_h = []
            for h in range(GDN_HEADS):
                oh = o[h]
                lanes = slice(h * GDN_HD, (h + 1) * GDN_HD)
                gt = gate_ref[rows, lanes]
                d_out = do_ref[rows, lanes]
                r = lax.rsqrt(jnp.mean(oh * oh, axis=-1, keepdims=True) + RMS_EPS)
                n = oh * r
                sg = _sigmoid(gt)
                silu = gt * sg
                dog = dog + jnp.sum(d_out * n * silu, axis=0, keepdims=True)
                dgate_ref[rows, lanes] = (d_out * n * og * (sg * (1.0 + gt * (1.0 - sg)))).astype(bf16)
                dn = d_out * og * silu
                do_h.append(r * (dn - n * jnp.mean(dn * n, axis=-1, keepdims=True)))
            do16 = jnp.stack(do_h).astype(bf16)
            ds16 = ds.astype(bf16)
            dvn = _bdot_tn(at16, do16) + _bdot(kd16, ds16)
            dvn16 = dvn.astype(bf16)
            du_ref[us] = dvn
            dw_ref[us] = -_bdot_nt(dvn16, s016)
            dat_ref[us] = jnp.where(tril, _bdot_nt(do16, vn16), 0.0)
            dqd_ref[us] = _bdot_nt(do16, s016)
            dkd_ref[us] = _bdot_nt(vn16, ds16)
            dgm_ref[us] = jnp.sum(s0 * ds, axis=1, keepdims=True)
            ds = _bdot_tn(qd16, do16) + ds * gm_ref[us] - _bdot_tn(w16, dvn16)
        ds_scr[...] = ds
        dog_ref[...] += dog

    def unit(*tail):
        nd = len(tail)
        return pl.BlockSpec((GDN_NU,) + tail, lambda i: (n_steps - 1 - i,) + (0,) * nd)

    intra_specs = [unit(CHUNK, GDN_HD), unit(CHUNK, GDN_HD), unit(CHUNK, CHUNK), unit(CHUNK, GDN_HD),
                   unit(CHUNK, GDN_HD), unit(1, GDN_HD)]
    tok = lambda c: pl.BlockSpec((GDN_TB, GDN_W), lambda i: (n_steps - 1 - i, c))
    return pl.pallas_call(
        body, name="gdn_scan_bwd", grid=(n_steps,),
        in_specs=intra_specs + [unit(GDN_HD, GDN_HD), tok(3), tok(0), _const_spec((1, GDN_HD)), _ANY],
        out_specs=intra_specs + [tok(0), _const_spec((1, GDN_HD))],
        out_shape=_gdn_intra_shapes(S) + [SDS((S, GDN_W), bf16), SDS((1, GDN_HD), f32)],
        scratch_shapes=[pltpu.VMEM((GDN_HEADS, GDN_HD, GDN_HD), f32)],
        compiler_params=_cp(1, VMEM_LIMIT),
    )(*intra, states, proj, d_oab, onorm_g, after)


SWA_UNROLL = 4


def _swa_tiles(q_ref, k_ref, v_ref, it, d, nb_log2, S):
    nb = 1 << nb_log2
    r = lax.shift_right_logical(it, nb_log2)
    blk = lax.bitwise_and(it, nb - 1)
    qs = blk * (SWA_BLK * d) + r
    ps = jnp.maximum(blk - 1, 0) * (SWA_BLK * d) + r
    if d > 1:
        rows_c, rows_p = pl.ds(qs, SWA_BLK, stride=d), pl.ds(ps, SWA_BLK, stride=d)
    else:
        rows_c, rows_p = pl.ds(pl.multiple_of(qs, SWA_BLK), SWA_BLK), pl.ds(pl.multiple_of(ps, SWA_BLK), SWA_BLK)
    return rows_c, rows_p, blk > 0


def _swa_prev_modes(nb):
    if nb >= SWA_UNROLL:
        return ["load"] + ["reuse"] * (SWA_UNROLL - 1)
    return ["none" if u % nb == 0 else "reuse" for u in range(SWA_UNROLL)]


def _swa_fwd(proj, bt, S):
    scale = SWA_HD ** -0.5

    def body(q_ref, k_ref, v_ref, bt_ref, o_ref, lse0_ref, lse1_ref, m0_scr, m1_scr, a0_scr, a1_scr):
        lane = lax.broadcasted_iota(jnp.int32, (SWA_BLK, LANE), 1)
        h0 = lane < SWA_HD
        qi = lax.broadcasted_iota(jnp.int32, (SWA_BLK, SWA_BLK), 0)
        kj = lax.broadcasted_iota(jnp.int32, (SWA_BLK, SWA_BLK), 1)
        lower = kj <= qi
        ones16 = jnp.ones((LANE, SWA_BLK), bf16)
        m_scrs = (m0_scr, m1_scr)
        a_scrs = (a0_scr, a1_scr)
        for p, (_, d) in reversed(list(enumerate(PATTERNS))):
            nb_log2 = int(math.log2(S // d // SWA_BLK))
            first = p == len(PATTERNS) - 1

            def lp(i, carry, p=p, d=d, nb_log2=nb_log2, first=first):
                heads = [h0, jnp.logical_not(h0)]
                modes = _swa_prev_modes(1 << nb_log2)
                tiles = []
                kc_f = None
                for u in range(SWA_UNROLL):
                    rows_c, rows_p, has_prev = _swa_tiles(q_ref, k_ref, v_ref, i * SWA_UNROLL + u, d, nb_log2, S)
                    kp_f = {"load": lambda: k_ref[rows_p, :], "reuse": lambda: kc_f, "none": lambda: None}[modes[u]]()
                    has_prev = {"load": has_prev, "reuse": True, "none": False}[modes[u]]
                    q = q_ref[rows_c, :]
                    kc_f = k_ref[rows_c, :]
                    kc = kc_f.astype(bf16)
                    logits = []
                    for mh in heads:
                        q_h = jnp.where(mh, q, 0.0)
                        qh = q_h.astype(bf16)
                        if kp_f is None:
                            logits.append((_dot_nt(qh, kc), None, None))
                        else:
                            logits.append((_dot_nt(qh, kc), _dot_nt(qh, kp_f.astype(bf16)), _dot((q_h * kp_f).astype(bf16), ones16)))
                    tiles.append((rows_c, rows_p, has_prev, logits))
                probs = []
                for rows_c, rows_p, has_prev, logits in tiles:
                    per_head = []
                    for h, (s_c, s_p, far) in enumerate(logits):
                        if has_prev is False:
                            s = jnp.where(lower, s_c * scale + bt_ref[p, h], NEG)
                            s_far = None
                        else:
                            s = jnp.where(lower, s_c, s_p) * scale + bt_ref[p, h]
                            s_far = far * scale + bt_ref[p, h, SWA_BLK - 1:SWA_BLK, 0:1]
                            if has_prev is not True:
                                s = jnp.where(jnp.logical_or(lower, has_prev), s, NEG)
                                s_far = jnp.where(has_prev, s_far, NEG)
                        mn = jnp.max(s, axis=1, keepdims=True)
                        if s_far is not None:
                            mn = jnp.maximum(s_far, mn)
                        alpha = None
                        if not first:
                            mo = m_scrs[h][rows_c, :]
                            mn = jnp.maximum(mo, mn)
                            alpha = jnp.exp(mo - mn)
                        mn = jnp.broadcast_to(mn, (SWA_BLK, LANE))
                        pm = jnp.exp(s - mn)
                        per_head.append((mn, alpha, None if s_far is None else jnp.exp(s_far - mn),
                                         jnp.where(lower, pm, 0.0).astype(bf16),
                                         None if s_far is None else jnp.where(lower, 0.0, pm).astype(bf16)))
                    probs.append(per_head)
                acc_old = [None if first else (a0_scr[t[0], :], a1_scr[t[0], :]) for t in tiles]
                done = []
                vc = None
                for u, ((rows_c, rows_p, _, _), per_head, old) in enumerate(zip(tiles, probs, acc_old)):
                    vp = {"load": lambda: v_ref[rows_p, :], "reuse": lambda: vc, "none": lambda: None}[modes[u]]()
                    vc = v_ref[rows_c, :]
                    acc_new = []
                    for h, (mn, alpha, p_far, pc16, pp16) in enumerate(per_head):
                        pv = _dot(pc16, jnp.where(heads[h], vc, 1.0).astype(bf16))
                        if pp16 is not None:
                            vpa = jnp.where(heads[h], vp, 1.0)
                            pv = pv + _dot(pp16, vpa.astype(bf16)) + p_far * vpa
                        acc_new.append(pv if first else alpha * old[h] + pv)
                    done.append((rows_c, per_head[0][0], per_head[1][0], acc_new[0], acc_new[1]))
                for rows_c, m0_new, m1_new, a0_new, a1_new in done:
                    m0_scr[rows_c, :] = m0_new
                    m1_scr[rows_c, :] = m1_new
                    a0_scr[rows_c, :] = a0_new
                    a1_scr[rows_c, :] = a1_new
                return carry

            lax.fori_loop(0, S // SWA_BLK // SWA_UNROLL, lp, 0)

        def fin(c, carry):
            rows = pl.ds(pl.multiple_of(c * SWA_BLK, SWA_BLK), SWA_BLK)
            a0 = a0_scr[rows, :]
            a1 = a1_scr[rows, :]
            l0 = jnp.where(h0, pltpu.roll(a0, SWA_HD, 1), a0)
            l1 = jnp.where(h0, a1, pltpu.roll(a1, SWA_HD, 1))
            o_ref[rows, :] = jnp.where(h0, a0 / l0, a1 / l1)
            lse0_ref[rows, :] = m0_scr[rows, :] + jnp.log(l0)
            lse1_ref[rows, :] = m1_scr[rows, :] + jnp.log(l1)
            return carry

        lax.fori_loop(0, S // SWA_BLK, fin, 0)

    qb = COL_B // LANE
    col = lambda c: pl.BlockSpec((S, LANE), lambda hp, c=c: (0, c + hp))
    return pl.pallas_call(
        body, name="swa_fwd", grid=(4,),
        in_specs=[col(qb), col(qb + 4), col(qb + 8), pl.BlockSpec((3, 2, SWA_BLK, SWA_BLK), lambda hp: (0, hp, 0, 0))],
        out_specs=[col(0), col(0), col(0)],
        out_shape=[SDS((S, SWA_W), f32)] * 3,
        scratch_shapes=[pltpu.VMEM((S, LANE), f32)] * 4,
        compiler_params=_cp(1, VMEM_LIMIT),
    )(proj, proj, proj, bt)


def _swa_bwd(proj, bt, ob, lse0, lse1, d_oab, after, S):
    scale = SWA_HD ** -0.5

    def body(q_ref, k_ref, v_ref, bt_ref, o_ref, lse0_ref, lse1_ref, do_ref, after_ref, dq_ref, dk_ref, dv_ref, dsb_ref,
             dq_scr, dk_scr, dv_scr, nd_scr):
        lane = lax.broadcasted_iota(jnp.int32, (SWA_BLK, LANE), 1)
        h0 = lane < SWA_HD
        qi = lax.broadcasted_iota(jnp.int32, (SWA_BLK, SWA_BLK), 0)
        kj = lax.broadcasted_iota(jnp.int32, (SWA_BLK, SWA_BLK), 1)
        lower = kj <= qi
        eye = kj == qi
        rel127 = jnp.logical_or(kj == qi + 1, jnp.logical_and(qi == SWA_BLK - 1, kj == 0))
        ones16 = jnp.ones((LANE, SWA_BLK), bf16)
        lse_refs = (lse0_ref, lse1_ref)
        dk_scr[...] = jnp.zeros((S, LANE), f32)
        dv_scr[...] = jnp.zeros((S, LANE), f32)
        dsb_ref[...] = jnp.zeros_like(dsb_ref)

        def row_sum(a):
            hi = a.astype(bf16)
            return _dot(hi, ones16) + _dot((a - hi.astype(f32)).astype(bf16), ones16)

        def prep(c, carry):
            rows = pl.ds(pl.multiple_of(c * SWA_BLK, SWA_BLK), SWA_BLK)
            dlt = do_ref[rows, :] * o_ref[rows, :]
            d0 = row_sum(jnp.where(h0, dlt, 0.0))
            d1 = row_sum(jnp.where(h0, 0.0, dlt))
            nd_scr[rows, :] = jnp.where(h0, d1, d0) * (-1.0 / SWA_HD)
            return carry

        lax.fori_loop(0, S // SWA_BLK, prep, 0)

        for p, (_, d) in reversed(list(enumerate(PATTERNS))):
            nb_log2 = int(math.log2(S // d // SWA_BLK))
            first = p == len(PATTERNS) - 1

            def lp(i, carry, p=p, d=d, nb_log2=nb_log2, first=first):
                heads = [h0, jnp.logical_not(h0)]
                modes = _swa_prev_modes(1 << nb_log2)
                tiles = []
                kc_f = vc_f = None
                for u in range(SWA_UNROLL):
                    rows_c, rows_p, has_prev = _swa_tiles(q_ref, k_ref, v_ref, i * SWA_UNROLL + u, d, nb_log2, S)
                    kp_f = {"load": lambda: k_ref[rows_p, :], "reuse": lambda: kc_f, "none": lambda: None}[modes[u]]()
                    vp_f = {"load": lambda: v_ref[rows_p, :], "reuse": lambda: vc_f, "none": lambda: None}[modes[u]]()
                    has_prev = {"load": has_prev, "reuse": True, "none": False}[modes[u]]
                    q = q_ref[rows_c, :]
                    kc_f = k_ref[rows_c, :]
                    vc_f = v_ref[rows_c, :]
                    kc = kc_f.astype(bf16)
                    kp = None if kp_f is None else kp_f.astype(bf16)
                    do = do_ref[rows_c, :]
                    nd = nd_scr[rows_c, :]
                    per_head = []
                    for mh in heads:
                        q_h = jnp.where(mh, q, 0.0)
                        do_a = jnp.where(mh, do, nd)
                        qh = q_h.astype(bf16)
                        doa = do_a.astype(bf16)
                        doh = jnp.where(mh, do, 0.0).astype(bf16)
                        dd_c = _dot_nt(doa, jnp.where(mh, vc_f, 1.0).astype(bf16))
                        if kp_f is None:
                            per_head.append((qh, doh, _dot_nt(qh, kc), None, None, dd_c, None, None))
                        else:
                            vpa = jnp.where(mh, vp_f, 1.0)
                            per_head.append((qh, doh, _dot_nt(qh, kc), _dot_nt(qh, kp), _dot((q_h * kp_f).astype(bf16), ones16),
                                             dd_c, _dot_nt(doa, vpa.astype(bf16)), _dot((do_a * vpa).astype(bf16), ones16)))
                    tiles.append((rows_c, rows_p, has_prev, kc, kp, per_head))
                grads = []
                for rows_c, rows_p, has_prev, kc, kp, per_head in tiles:
                    out = []
                    for h, (qh, doh, s_c, s_p, far, dd_c, dd_p, dd_far) in enumerate(per_head):
                        lse_h = lse_refs[h][rows_c, :]
                        if has_prev is False:
                            pm = jnp.exp(jnp.where(lower, s_c * scale + bt_ref[p, h], NEG) - lse_h)
                            dsm = pm * dd_c
                            out.append((dsm, dsm.astype(bf16), None, pm.astype(bf16), None))
                            continue
                        s = jnp.where(lower, s_c, s_p) * scale + bt_ref[p, h]
                        s_far = far * scale + bt_ref[p, h, SWA_BLK - 1:SWA_BLK, 0:1]
                        if has_prev is not True:
                            s = jnp.where(jnp.logical_or(lower, has_prev), s, NEG)
                            s_far = jnp.where(has_prev, s_far, NEG)
                        pm = jnp.exp(s - lse_h)
                        p_far = jnp.exp(s_far - lse_h)
                        dsm = pm * jnp.where(lower, dd_c, dd_p)
                        ds_far = p_far * dd_far
                        out.append((dsm + jnp.where(rel127, ds_far, 0.0),
                                    jnp.where(lower, dsm, 0.0).astype(bf16),
                                    jnp.where(lower, jnp.where(eye, ds_far, 0.0), dsm).astype(bf16),
                                    jnp.where(lower, pm, 0.0).astype(bf16),
                                    jnp.where(lower, jnp.where(eye, p_far, 0.0), pm).astype(bf16)))
                    grads.append(out)
                done = []
                add = lambda acc, t: t if acc is None else acc + t
                for (rows_c, rows_p, _, kc, kp, per_head), out in zip(tiles, grads):
                    dq_t = dkc_t = dkp_t = dvc_t = dvp_t = None
                    for h, (_, dsc16, dsp16, pc16, pp16) in enumerate(out):
                        qh, doh = per_head[h][0], per_head[h][1]
                        dq_h = _dot(dsc16, kc)
                        dkc_t = add(dkc_t, _dot_tn(dsc16, qh) * scale)
                        dvc_t = add(dvc_t, _dot_tn(pc16, doh))
                        if dsp16 is not None:
                            dq_h = dq_h + _dot(dsp16, kp)
                            dkp_t = add(dkp_t, _dot_tn(dsp16, qh) * scale)
                            dvp_t = add(dvp_t, _dot_tn(pp16, doh))
                        dq_t = add(dq_t, jnp.where(heads[h], dq_h * scale, 0.0))
                    done.append([rows_c, rows_p, dq_t, dkc_t, dkp_t, dvc_t, dvp_t])
                for u in range(1, SWA_UNROLL):
                    if modes[u] == "reuse":
                        done[u - 1][3] = done[u - 1][3] + done[u][4]
                        done[u - 1][5] = done[u - 1][5] + done[u][6]
                for h in range(2):
                    tot = grads[0][h][0]
                    for g in grads[1:]:
                        tot = tot + g[h][0]
                    dsb_ref[0, p, h] += tot
                for u, (rows_c, rows_p, dq_t, dkc_t, dkp_t, dvc_t, dvp_t) in enumerate(done):
                    dq_scr[rows_c, :] = dq_t if first else dq_scr[rows_c, :] + dq_t
                    dk_scr[rows_c, :] = dk_scr[rows_c, :] + dkc_t
                    dv_scr[rows_c, :] = dv_scr[rows_c, :] + dvc_t
                    if modes[u] == "load":
                        dk_scr[rows_p, :] = dk_scr[rows_p, :] + dkp_t
                        dv_scr[rows_p, :] = dv_scr[rows_p, :] + dvp_t
                return carry

            lax.fori_loop(0, S // SWA_BLK // SWA_UNROLL, lp, 0)
        dq_ref[...] = dq_scr[...].astype(bf16)
        dk_ref[...] = dk_scr[...].astype(bf16)
        dv_ref[...] = dv_scr[...].astype(bf16)

    qb = COL_B // LANE
    col = lambda c: pl.BlockSpec((S, LANE), lambda hp, c=c: (0, c + hp))
    return pl.pallas_call(
        body, name="swa_bwd", grid=(4,),
        in_specs=[col(qb), col(qb + 4), col(qb + 8),
                  pl.BlockSpec((3, 2, SWA_BLK, SWA_BLK), lambda hp: (0, hp, 0, 0)),
                  col(0), col(0), col(0), col(4), _ANY],
        out_specs=[col(0), col(0), col(0),
                   pl.BlockSpec((1, 3, 2, SWA_BLK, SWA_BLK), lambda hp: (hp, 0, 0, 0, 0))],
        out_shape=[SDS((S, SWA_W), bf16)] * 3 + [SDS((4, 3, 2, SWA_BLK, SWA_BLK), f32)],
        scratch_shapes=[pltpu.VMEM((S, LANE), f32)] * 4,
        compiler_params=_cp(1, VMEM_LIMIT),
    )(proj, proj, proj, bt, ob, lse0, lse1, d_oab, after)


def _mix_fwd(oa, ob, w_out, x, g_post, S):
    TS = 512

    def body(oa_ref, ob_ref, w_ref, x_ref, g_ref, mix_ref, x1_ref):
        mix = _dot(oa_ref[...].astype(bf16), w_ref[0:GDN_W, :]) + _dot(ob_ref[...].astype(bf16), w_ref[GDN_W:D_MODEL, :])
        r = lax.rsqrt(jnp.mean(mix * mix, axis=-1, keepdims=True) + RMS_EPS)
        mix_ref[...] = mix
        x1_ref[...] = x_ref[...] + mix * r * g_ref[...]

    row = lambda w: pl.BlockSpec((TS, w), lambda i: (i, 0))
    return pl.pallas_call(
        body, name="mix_fwd", grid=(S // TS,),
        in_specs=[row(GDN_W), row(SWA_W), _resident_spec((D_MODEL, D_MODEL)), row(D_MODEL), _const_spec((1, D_MODEL))],
        out_specs=[row(D_MODEL), row(D_MODEL)],
        out_shape=[SDS((S, D_MODEL), f32), SDS((S, D_MODEL), f32)],
        compiler_params=_cp(1, VMEM_LIMIT),
    )(oa, ob, w_out, x, g_post)


def _mix_bwd(dx1, mix, g_post, w_out, S):
    TS = 512

    def body(dx1_ref, mix_ref, g_ref, w_ref, dmix_ref, doab_ref, dg_ref):
        @pl.when(pl.program_id(0) == 0)
        def _():
            dg_ref[...] = jnp.zeros_like(dg_ref)

        mix = mix_ref[...]
        dz = dx1_ref[...]
        r = lax.rsqrt(jnp.mean(mix * mix, axis=-1, keepdims=True) + RMS_EPS)
        n = mix * r
        dg_ref[...] += jnp.sum(dz * n, axis=0, keepdims=True)
        dn = dz * g_ref[...]
        dmix = (r * (dn - n * jnp.mean(dn * n, axis=-1, keepdims=True))).astype(bf16)
        dmix_ref[...] = dmix
        doab_ref[...] = _dot_nt(dmix, w_ref[...])

    row = lambda: pl.BlockSpec((TS, D_MODEL), lambda i: (i, 0))
    return pl.pallas_call(
        body, name="mix_bwd", grid=(S // TS,),
        in_specs=[row(), row(), _const_spec((1, D_MODEL)), _resident_spec((D_MODEL, D_MODEL))],
        out_specs=[row(), row(), _const_spec((1, D_MODEL))],
        out_shape=[SDS((S, D_MODEL), bf16), SDS((S, D_MODEL), f32), SDS((1, D_MODEL), f32)],
        compiler_params=_cp(1, VMEM_LIMIT),
    )(dx1, mix, g_post, w_out)


FFN_TS = 256
FFN_CH = 1408


def _ffn(x1, tgt, g_pre, g_post, wg, wu, wd, S):
    def body(x1_ref, t_ref, gp_ref, gq_ref, wg_ref, wu_ref, wd_ref,
             dx1_ref, h2_ref, act_ref, dgate_ref, dup_ref, df_ref, loss_ref, dgp_ref, dgq_ref, gate_scr, up_scr):
        @pl.when(pl.program_id(0) == 0)
        def _():
            loss_ref[...] = jnp.zeros_like(loss_ref)
            dgp_ref[...] = jnp.zeros_like(dgp_ref)
            dgq_ref[...] = jnp.zeros_like(dgq_ref)

        x1v = x1_ref[...]
        gp = gp_ref[...]
        gq = gq_ref[...]
        r2 = lax.rsqrt(jnp.mean(x1v * x1v, axis=-1, keepdims=True) + RMS_EPS)
        n2 = x1v * r2
        h2 = (n2 * gp).astype(bf16)
        h2_ref[...] = h2
        f = jnp.zeros((FFN_TS, D_MODEL), f32)
        for c in range(D_FF // FFN_CH):
            cs = slice(c * FFN_CH, (c + 1) * FFN_CH)
            gate = _dot(h2, wg_ref[:, cs])
            up = _dot(h2, wu_ref[:, cs])
            gate_scr[:, cs] = gate
            up_scr[:, cs] = up
            act = (gate * _sigmoid(gate) * up).astype(bf16)
            act_ref[:, cs] = act
            f = f + _dot(act, wd_ref[cs, :])
        r3 = lax.rsqrt(jnp.mean(f * f, axis=-1, keepdims=True) + RMS_EPS)
        n3 = f * r3
        err = x1v + n3 * gq - t_ref[...]
        loss_ref[...] += 0.5 * jnp.sum(jnp.mean(err * err, axis=-1, keepdims=True), axis=0, keepdims=True)
        dy = err * (1.0 / D_MODEL)
        dgq_ref[...] += jnp.sum(dy * n3, axis=0, keepdims=True)
        dn3 = dy * gq
        df = (r3 * (dn3 - n3 * jnp.mean(dn3 * n3, axis=-1, keepdims=True))).astype(bf16)
        df_ref[...] = df
        dh2 = jnp.zeros((FFN_TS, D_MODEL), f32)
        for c in range(D_FF // FFN_CH):
            cs = slice(c * FFN_CH, (c + 1) * FFN_CH)
            gate = gate_scr[:, cs]
            up = up_scr[:, cs]
            dact = _dot_nt(df, wd_ref[cs, :])
            sg = _sigmoid(gate)
            dup = (dact * gate * sg).astype(bf16)
            dgate = (dact * up * (sg * (1.0 + gate * (1.0 - sg)))).astype(bf16)
            dup_ref[:, cs] = dup
            dgate_ref[:, cs] = dgate
            dh2 = dh2 + _dot_nt(dgate, wg_ref[:, cs]) + _dot_nt(dup, wu_ref[:, cs])
        dgp_ref[...] += jnp.sum(dh2 * n2, axis=0, keepdims=True)
        dn2 = dh2 * gp
        dx1_ref[...] = dy + r2 * (dn2 - n2 * jnp.mean(dn2 * n2, axis=-1, keepdims=True))

    row = lambda w: pl.BlockSpec((FFN_TS, w), lambda i: (i, 0))
    vec = _const_spec((1, D_MODEL))
    return pl.pallas_call(
        body, name="ffn_fwd_bwd", grid=(S // FFN_TS,),
        in_specs=[row(D_MODEL), row(D_MODEL), vec, vec, _resident_spec((D_MODEL, D_FF)), _resident_spec((D_MODEL, D_FF)),
                  _resident_spec((D_FF, D_MODEL))],
        out_specs=[row(D_MODEL), row(D_MODEL), row(D_FF), row(D_FF), row(D_FF), row(D_MODEL), _const_spec((1, LANE)), vec, vec],
        out_shape=[SDS((S, D_MODEL), f32), SDS((S, D_MODEL), bf16), SDS((S, D_FF), bf16), SDS((S, D_FF), bf16),
                   SDS((S, D_FF), bf16), SDS((S, D_MODEL), bf16), SDS((1, LANE), f32), SDS((1, D_MODEL), f32),
                   SDS((1, D_MODEL), f32)],
        scratch_shapes=[pltpu.VMEM((FFN_TS, D_FF), f32), pltpu.VMEM((FFN_TS, D_FF), f32)],
        compiler_params=_cp(1, VMEM_LIMIT),
    )(x1, tgt, g_pre, g_post, wg, wu, wd)


def _proj_bwd(x, dx1, g_pre, wcat, segs, after, S):
    TS = 256
    n = len(segs)
    cols = [(c0, a.shape[1]) for a, c0 in segs]

    def body(*refs):
        x_ref, dx1_ref, g_ref, w_ref = refs[:4]
        seg_refs = refs[4:4 + n]
        gx_ref, dg_ref = refs[5 + n:]

        @pl.when(pl.program_id(0) == 0)
        def _():
            dg_ref[...] = jnp.zeros_like(dg_ref)

        dh = jnp.zeros((TS, D_MODEL), f32)
        for s_ref, (c0, w) in zip(seg_refs, cols):
            dh = dh + _dot_nt(s_ref[...], w_ref[:, c0:c0 + w])
        xv = x_ref[...]
        g = g_ref[...]
        r = lax.rsqrt(jnp.mean(xv * xv, axis=-1, keepdims=True) + RMS_EPS)
        nx = xv * r
        dg_ref[...] += jnp.sum(dh * nx, axis=0, keepdims=True)
        dn = dh * g
        gx_ref[...] = dx1_ref[...] + r * (dn - nx * jnp.mean(dn * nx, axis=-1, keepdims=True))

    row = lambda w: pl.BlockSpec((TS, w), lambda i: (i, 0))
    return pl.pallas_call(
        body, name="proj_bwd", grid=(S // TS,),
        in_specs=[row(D_MODEL), row(D_MODEL), _const_spec((1, D_MODEL)), _resident_spec((D_MODEL, NCOL))]
                 + [row(w) for _, w in cols] + [_ANY],
        out_specs=[row(D_MODEL), _const_spec((1, D_MODEL))],
        out_shape=[SDS((S, D_MODEL), f32), SDS((1, D_MODEL), f32)],
        compiler_params=_cp(1, VMEM_LIMIT),
    )(x, dx1, g_pre, wcat, *[a for a, _ in segs], after)


def _wgrad(a, b, S, name, col_blocks=False):
    TS = 512
    K = a.shape[1]
    N = b.shape[1]
    TN = next(t for t in (512, 1408, N) if N % t == 0)
    cb = N // N_DEV
    nblk = TN // cb if col_blocks else 0

    def body(a_ref, b_ref, o_ref, acc):
        @pl.when(pl.program_id(1) == 0)
        def _():
            acc[...] = jnp.zeros_like(acc)

        acc[...] += _dot_tn(a_ref[...].astype(bf16), b_ref[...])

        @pl.when(pl.program_id(1) == pl.num_programs(1) - 1)
        def _():
            if col_blocks:
                for i in range(nblk):
                    o_ref[i] = acc[:, i * cb:(i + 1) * cb].astype(bf16)
            else:
                o_ref[...] = acc[...].astype(bf16)

    if col_blocks:
        out_spec = pl.BlockSpec((nblk, K, cb), lambda j, s: (j, 0, 0))
        out_shape = SDS((N_DEV, K, cb), bf16)
    else:
        out_spec = pl.BlockSpec((K, TN), lambda j, s: (0, j))
        out_shape = SDS((K, N), bf16)
    return pl.pallas_call(
        body, name=name, grid=(N // TN, S // TS),
        in_specs=[pl.BlockSpec((TS, K), lambda j, s: (s, 0)), pl.BlockSpec((TS, TN), lambda j, s: (s, j))],
        out_specs=out_spec, out_shape=out_shape,
        scratch_shapes=[pltpu.VMEM((K, TN), f32)],
        compiler_params=_cp(2, VMEM_LIMIT),
    )(a, b)


def _w_in_pieces():
    n_a, n_g = 4 * GDN_W, 2 * GDN_HEADS
    cb = IN_COLS // N_DEV
    bounds = [(0, n_a, COL_A), (n_a, n_a + n_g, COL_G), (n_a + n_g, IN_COLS, COL_B)]
    out = []
    for j in range(N_DEV):
        lo, hi = j * cb, (j + 1) * cb
        for s0, s1, dst in bounds:
            a, b = max(lo, s0), min(hi, s1)
            if a < b:
                out.append((j, a - lo, b - a, dst + a - s0))
    return out


def _wcat_from_blocks(g_in):
    TR = 256
    cb = IN_COLS // N_DEV
    pieces = _w_in_pieces()

    def body(w_ref, o_ref):
        o_ref[:, COL_G:NCOL] = jnp.zeros((TR, NCOL - COL_G), bf16)
        for j, off, w, dst in pieces:
            o_ref[:, dst:dst + w] = w_ref[j, :, off:off + w]

    return pl.pallas_call(
        body, name="wcat_from_blocks", grid=(D_MODEL // TR,),
        in_specs=[pl.BlockSpec((N_DEV, TR, cb), lambda i: (0, i, 0))],
        out_specs=pl.BlockSpec((TR, NCOL), lambda i: (i, 0)),
        out_shape=SDS((D_MODEL, NCOL), bf16),
        compiler_params=_cp(1, VMEM_LIMIT),
    )(g_in)


def _full_from_col_blocks(g):
    n, R, C = g.shape
    TR = 256

    def body(w_ref, o_ref):
        for j in range(n):
            o_ref[:, j * C:(j + 1) * C] = w_ref[j]

    return pl.pallas_call(
        body, name="full_from_col_blocks", grid=(R // TR,),
        in_specs=[pl.BlockSpec((n, TR, C), lambda i: (0, i, 0))],
        out_specs=pl.BlockSpec((TR, n * C), lambda i: (i, 0)),
        out_shape=SDS((R, n * C), g.dtype),
        compiler_params=_cp(1, VMEM_LIMIT),
    )(g)


def _wgrad_in(h1, segs, S):
    TS = 512
    n = len(segs)
    cols = [(c0, a.shape[1]) for a, c0 in segs]
    cb = IN_COLS // N_DEV
    pieces = _w_in_pieces()

    def body(*refs):
        h_ref = refs[0]
        seg_refs = refs[1:1 + n]
        o_ref, acc = refs[1 + n], refs[2 + n]

        @pl.when(pl.program_id(0) == 0)
        def _():
            acc[...] = jnp.zeros_like(acc)

        h = h_ref[...]
        for s_ref, (c0, w) in zip(seg_refs, cols):
            acc[:, c0:c0 + w] += _dot_tn(h, s_ref[...])

        @pl.when(pl.program_id(0) == pl.num_programs(0) - 1)
        def _():
            for j, off, w, src in pieces:
                o_ref[j, :, off:off + w] = acc[:, src:src + w].astype(bf16)

    row = lambda w: pl.BlockSpec((TS, w), lambda i: (i, 0))
    return pl.pallas_call(
        body, name="wgrad_in", grid=(S // TS,),
        in_specs=[row(D_MODEL)] + [row(w) for _, w in cols],
        out_specs=_const_spec((N_DEV, D_MODEL, cb)),
        out_shape=SDS((N_DEV, D_MODEL, cb), bf16),
        scratch_shapes=[pltpu.VMEM((D_MODEL, NCOL), f32)],
        compiler_params=_cp(1, VMEM_LIMIT),
    )(h1, *[a for a, _ in segs])


def _adamw(recv, w, m, v, name):
    R, C = w.shape
    TR = 256 if R % 256 == 0 else R
    c1 = 1.0 / (1.0 - ADAM_B1 ** ADAM_STEP)
    c2 = 1.0 / (1.0 - ADAM_B2 ** ADAM_STEP)

    def body(r_ref, w_ref, m_ref, v_ref, g_out, d_out, m_out, v_out):
        g = r_ref[0].astype(f32)
        for s in range(1, N_DEV):
            g = g + r_ref[s].astype(f32)
        mn = ADAM_B1 * m_ref[...] + (1.0 - ADAM_B1) * g
        vn = ADAM_B2 * v_ref[...] + (1.0 - ADAM_B2) * (g * g)
        g_out[...] = g
        m_out[...] = mn
        v_out[...] = vn
        d_out[...] = -ADAM_LR * ((mn * c1) / (jnp.sqrt(vn * c2) + ADAM_EPS) + ADAM_WD * w_ref[...])

    blk = pl.BlockSpec((TR, C), lambda i: (i, 0))
    return pl.pallas_call(
        body, name=name, grid=(R // TR,),
        in_specs=[pl.BlockSpec((N_DEV, TR, C), lambda i: (0, i, 0)), blk, blk, blk],
        out_specs=[blk, blk, blk, blk],
        out_shape=[SDS((R, C), f32)] * 4,
        compiler_params=_cp(1, VMEM_LIMIT),
    )(recv, w, m, v)


MESH = pl.DeviceIdType.MESH
_ANY = pl.BlockSpec(memory_space=pl.ANY)


def _flip(v, d):
    return 1 - v if d else v


def _all_gather(shards):
    n = len(shards)

    def body(*refs):
        ins = refs[:n]
        outs = refs[n:2 * n]
        send_sems, recv_sems, local_sems = refs[2 * n:]
        x, y, c = lax.axis_index("x"), lax.axis_index("y"), lax.axis_index("c")
        me, sibling = (x, y, c), (x, y, 1 - c)
        chips = [(1 - x, y), (x, 1 - y), (1 - x, 1 - y)]

        def slot(px, py, pc):
            return 4 * px + 2 * py + pc

        def copy(a, k, block, to, src=None):
            dst = outs[a].at[slot(*block)]
            return pltpu.make_async_remote_copy(src_ref=dst if src is None else src, dst_ref=dst,
                                                send_sem=send_sems.at[a, k], recv_sem=recv_sems.at[a, k],
                                                device_id=to, device_id_type=MESH)

        mine, first, passed = [], [], []
        for a in range(n):
            cp = pltpu.make_async_copy(ins[a], outs[a].at[slot(*me)], local_sems.at[a])
            cp.start()
            mine.append(cp)
            fs = [copy(a, 0, me, sibling, src=ins[a])]
            fs += [copy(a, 1 + j, me, (*chip, c), src=ins[a]) for j, chip in enumerate(chips)]
            for cp in fs:
                cp.start()
            first += fs
        for j, chip in enumerate(chips):
            for a in range(n):
                copy(a, 1 + j, (*chip, c), me).wait_recv()
                cp = copy(a, 4 + j, (*chip, c), sibling)
                cp.start()
                passed.append(cp)
        for a in range(n):
            copy(a, 0, sibling, me).wait_recv()
            for j, chip in enumerate(chips):
                copy(a, 4 + j, (*chip, 1 - c), me).wait_recv()
        for cp in first + passed:
            cp.wait_send()
        for cp in mine:
            cp.wait()

    return pl.pallas_call(
        body, name="weight_all_gather",
        in_specs=[_ANY] * n, out_specs=[_ANY] * n,
        out_shape=[SDS((N_DEV,) + s.shape, s.dtype) for s in shards],
        scratch_shapes=[pltpu.SemaphoreType.DMA((n, 7)), pltpu.SemaphoreType.DMA((n, 7)), pltpu.SemaphoreType.DMA((n,))],
        compiler_params=pltpu.CompilerParams(has_side_effects=True),
    )(*shards)


def _grad_exchange(blocked, whole):
    arrs = list(blocked) + list(whole)
    n, nb = len(arrs), len(blocked)
    rel = [(dx, dy, dc) for dx in (0, 1) for dy in (0, 1) for dc in (0, 1) if dx or dy or dc]

    def body(*refs):
        ins = refs[:n]
        outs = refs[n:2 * n]
        send_sems, recv_sems, local_sems = refs[2 * n:]
        x, y, c = lax.axis_index("x"), lax.axis_index("y"), lax.axis_index("c")
        me = 4 * x + 2 * y + c
        sends, locs = [], []
        for a in range(n):
            cp = pltpu.make_async_copy(ins[a].at[me] if a < nb else ins[a], outs[a].at[me], local_sems.at[a])
            cp.start()
            locs.append(cp)
            for k, (dx, dy, dc) in enumerate(rel):
                peer = (_flip(x, dx), _flip(y, dy), _flip(c, dc))
                pidx = 4 * peer[0] + 2 * peer[1] + peer[2]
                cp = pltpu.make_async_remote_copy(src_ref=ins[a].at[pidx] if a < nb else ins[a], dst_ref=outs[a].at[me],
                                                  send_sem=send_sems.at[a, k], recv_sem=recv_sems.at[a, k],
                                                  device_id=peer, device_id_type=MESH)
                cp.start()
                sends.append(cp)
        for a in range(n):
            for k, (dx, dy, dc) in enumerate(rel):
                peer = (_flip(x, dx), _flip(y, dy), _flip(c, dc))
                pidx = 4 * peer[0] + 2 * peer[1] + peer[2]
                pltpu.make_async_remote_copy(src_ref=outs[a].at[pidx], dst_ref=outs[a].at[pidx],
                                             send_sem=send_sems.at[a, k], recv_sem=recv_sems.at[a, k],
                                             device_id=peer, device_id_type=MESH).wait_recv()
        for cp in sends:
            cp.wait_send()
        for cp in locs:
            cp.wait()

    shapes = [SDS(a.shape, a.dtype) for a in blocked] + [SDS((N_DEV,) + a.shape, a.dtype) for a in whole]
    return pl.pallas_call(
        body, name="grad_exchange",
        in_specs=[_ANY] * n, out_specs=[_ANY] * n, out_shape=shapes,
        scratch_shapes=[pltpu.SemaphoreType.DMA((n, 7)), pltpu.SemaphoreType.DMA((n, 7)), pltpu.SemaphoreType.DMA((n,))],
        compiler_params=pltpu.CompilerParams(has_side_effects=True),
    )(*arrs)


_HBM = pl.BlockSpec(memory_space=pltpu.HBM)
_SEM = pl.BlockSpec(memory_space=pltpu.SEMAPHORE)
_REL = [(dx, dy, dc) for dx in (0, 1) for dy in (0, 1) for dc in (0, 1) if dx or dy or dc]


N_PEER = len(_REL)
_EFFECT = pltpu.SideEffectType.DATAFLOW_SIDE_EFFECTING


def _peer_copies(srcs, lands, send_sems, recv_sems, blocked, as_receiver):
    x, y, c = lax.axis_index("x"), lax.axis_index("y"), lax.axis_index("c")
    me = 4 * x + 2 * y + c
    cps = []
    for a in range(len(srcs)):
        for k, (dx, dy, dc) in enumerate(_REL):
            peer = (_flip(x, dx), _flip(y, dy), _flip(c, dc))
            pidx = 4 * peer[0] + 2 * peer[1] + peer[2]
            cps.append(pltpu.make_async_remote_copy(
                src_ref=srcs[a].at[pidx] if blocked else srcs[a], dst_ref=lands[a].at[pidx if as_receiver else me],
                send_sem=send_sems[a * N_PEER + k], recv_sem=recv_sems[a * N_PEER + k],
                device_id=peer, device_id_type=MESH))
    return cps


def _exchange_start(srcs, after, blocked, name):
    n = len(srcs)
    ns = n * N_PEER
    lands = [lax.empty(s.shape if blocked else (N_DEV,) + s.shape, s.dtype) for s in srcs]

    def body(*refs):
        ins, lnd = refs[:n], refs[n:2 * n]
        outs = refs[2 * n + 1:]
        for cp in _peer_copies(ins, lnd, outs[:ns], outs[ns:2 * ns], blocked, False):
            cp.start()
        outs[-1][...] = jnp.zeros_like(outs[-1])

    res = pl.pallas_call(
        body, name=name,
        in_specs=[_HBM] * (2 * n) + [_ANY],
        out_specs=[_SEM] * (2 * ns) + [_HBM] * (2 * n) + [pl.BlockSpec(memory_space=pltpu.VMEM)],
        out_shape=[pltpu.SemaphoreType.DMA(())] * (2 * ns) + [pltpu.HBM(s.shape, s.dtype) for s in srcs]
                  + [pltpu.HBM(l.shape, l.dtype) for l in lands] + [SDS((8, LANE), f32)],
        input_output_aliases={i: 2 * ns + i for i in range(2 * n)},
        compiler_params=pltpu.CompilerParams(has_side_effects=_EFFECT),
    )(*[pltpu.with_memory_space_constraint(s, pltpu.HBM) for s in srcs],
      *[pltpu.with_memory_space_constraint(l, pltpu.HBM) for l in lands], after)
    return list(res[:2 * ns]), list(res[2 * ns:2 * ns + n]), list(res[2 * ns + n:2 * ns + 2 * n]), res[-1]


def _exchange_wait(sems, srcs, lands, after, blocked, name):
    n = len(srcs)
    ns = n * N_PEER

    def body(*refs):
        ins, lnd = refs[:n], refs[n:2 * n]
        sem_refs = refs[2 * n:2 * n + 2 * ns]
        for cp in _peer_copies(ins, lnd, sem_refs[:ns], sem_refs[ns:], blocked, True):
            cp.wait_send()
            cp.wait_recv()

    res = pl.pallas_call(
        body, name=name,
        in_specs=[_HBM] * (2 * n) + [_SEM] * (2 * ns) + [_ANY],
        out_specs=[_HBM] * (2 * n),
        out_shape=[pltpu.HBM(s.shape, s.dtype) for s in srcs] + [pltpu.HBM(l.shape, l.dtype) for l in lands],
        input_output_aliases={i: i for i in range(2 * n)},
        compiler_params=pltpu.CompilerParams(has_side_effects=_EFFECT),
    )(*srcs, *lands, *sems, after)
    return list(res[:n]), list(res[n:])


def _local_step(x, tgt, wcat, convw, late_weights, early_grads, last_grads, token, a_log, dt_bias, onorm_g, rel_bias,
                g_mix_pre, g_mix_post, g_ffn_pre, g_ffn_post):
    S = x.shape[0]
    bk_np = _bucket_tables()
    bk = jnp.asarray(bk_np)
    bt = _bias_tables(rel_bias, bk)
    proj, h1 = _proj_fwd(x, g_mix_pre, wcat, token, S)
    nu = S // CHUNK * GDN_HEADS
    qkv_u = _gdn_prep(proj, convw, S).reshape(3, nu, CHUNK, GDN_HD)
    intra, t_inv = _gdn_intra_fwd(qkv_u, proj, a_log, dt_bias, S)
    oa, states = _gdn_scan_fwd(intra, proj, onorm_g, S)
    ob, lse0, lse1 = _swa_fwd(proj, bt, S)
    wout, ffn_weights = late_weights(ob)
    mix, x1 = _mix_fwd(oa, ob, wout, x, g_mix_post, S)
    wgate, wup, wdown = ffn_weights(x1)
    dx1, h2, act, dgate_f, dup_f, df, loss, d_gfpre, d_gfpost = _ffn(x1, tgt, g_ffn_pre, g_ffn_post, wgate, wup, wdown, S)
    g_gate = _wgrad(h2, dgate_f, S, "wgrad_gate", col_blocks=True)
    g_up = _wgrad(h2, dup_f, S, "wgrad_up", col_blocks=True)
    g_down = _wgrad(act, df, S, "wgrad_down").reshape(N_DEV, D_FF // N_DEV, D_MODEL)
    dmix, d_oab, d_gmpost = _mix_bwd(dx1, mix, g_mix_post, wout, S)
    g_out = jnp.concatenate([_wgrad(oa, dmix, S, "wgrad_out_a"), _wgrad(ob, dmix, S, "wgrad_out_b")], axis=0)
    token = early_grads(g_out.reshape(N_DEV, D_MODEL // N_DEV, D_MODEL), g_gate, g_up, g_down)
    dqb, dkb, dvb, dsb = _swa_bwd(proj, bt, ob, lse0, lse1, d_oab, token, S)
    *cots, dgate_a, d_og = _gdn_scan_bwd(intra, states, proj, d_oab, onorm_g, token, S)
    dqkv_u, dpg, d_alog, d_dtb = _gdn_intra_bwd(qkv_u, proj, a_log, dt_bias, t_inv, cots, S)
    dqkv_a, d_conv = _gdn_prep_bwd(proj, convw, dqkv_u.reshape(3, S // CHUNK, GDN_HEADS, CHUNK, GDN_HD), S)
    segs = [(dqkv_a, COL_A), (dgate_a, COL_A + 3 * GDN_W), (dqb, COL_B), (dkb, COL_B + SWA_W), (dvb, COL_B + 2 * SWA_W),
            (dpg, COL_G)]
    token = last_grads(_wgrad_in(h1, segs, S), d_conv)
    grad_x, d_gmpre = _proj_bwd(x, dx1, g_mix_pre, wcat, segs, token, S)
    d_rel = _rel_bias_grad(dsb, bk, bk_np)
    small = dict(a_log=d_alog[:, GDN_HEADS:2 * GDN_HEADS], dt_bias=d_dtb[:, GDN_HEADS:2 * GDN_HEADS], onorm_g=d_og, rel_bias=d_rel,
                 g_mix_pre=d_gmpre, g_mix_post=d_gmpost, g_ffn_pre=d_gfpre, g_ffn_post=d_gfpost)
    return loss, grad_x, small


SMALL = ("a_log", "dt_bias", "onorm_g", "rel_bias", "g_mix_pre", "g_mix_post", "g_ffn_pre", "g_ffn_post")
PACK_ROWS = 8


def _pack_small(d, loss=None):
    rest = jnp.concatenate([d["onorm_g"].reshape(-1), d["a_log"].reshape(-1), d["dt_bias"].reshape(-1),
                            d["rel_bias"].reshape(-1)])
    rest = jnp.concatenate([rest, jnp.zeros((D_MODEL - rest.shape[0],), f32)])
    extra = jnp.zeros((D_MODEL,), f32) if loss is None else jnp.concatenate([loss.reshape(1), jnp.zeros((D_MODEL - 1,), f32)])
    rows = [d["g_mix_pre"].reshape(-1), d["g_mix_post"].reshape(-1), d["g_ffn_pre"].reshape(-1),
            d["g_ffn_post"].reshape(-1), rest, extra]
    return jnp.concatenate([jnp.stack(rows), jnp.zeros((PACK_ROWS - len(rows), D_MODEL), f32)], axis=0)


def _unpack_small(p):
    o = GDN_HD
    return dict(g_mix_pre=p[0:1], g_mix_post=p[1:2], g_ffn_pre=p[2:3], g_ffn_post=p[3:4],
                onorm_g=p[4:5, :o], a_log=p[4:5, o:o + 4], dt_bias=p[4:5, o + 4:o + 8],
                rel_bias=p[4, o + 8:o + 8 + NUM_BUCKETS * SWA_HEADS].reshape(NUM_BUCKETS, SWA_HEADS))


def kernel(x, w_in, conv_w, a_log, dt_bias, onorm_g, rel_bias, w_out, g_mix_pre, g_mix_post, w_gate, w_up, w_down, g_ffn_pre, g_ffn_post, loss_target, m_w_in, m_conv_w, m_a_log, m_dt_bias, m_onorm_g, m_rel_bias, m_w_out, m_g_mix_pre, m_g_mix_post, m_w_gate, m_w_up, m_w_down, m_g_ffn_pre, m_g_ffn_post, v_w_in, v_conv_w, v_a_log, v_dt_bias, v_onorm_g, v_rel_bias, v_w_out, v_g_mix_pre, v_g_mix_post, v_w_gate, v_w_up, v_w_down, v_g_ffn_pre, v_g_ffn_post):
    big = ("w_in", "conv_w", "w_out", "w_gate", "w_up", "w_down")
    w_sh = dict(w_in=w_in[0], conv_w=conv_w[0], w_out=w_out[0], w_gate=w_gate[0], w_up=w_up[0], w_down=w_down[0])
    m_sh = dict(w_in=m_w_in[0], conv_w=m_conv_w[0], w_out=m_w_out[0], w_gate=m_w_gate[0], w_up=m_w_up[0], w_down=m_w_down[0])
    v_sh = dict(w_in=v_w_in[0], conv_w=v_conv_w[0], w_out=v_w_out[0], w_gate=v_w_gate[0], w_up=v_w_up[0], w_down=v_w_down[0])
    w_small = dict(a_log=a_log, dt_bias=dt_bias, onorm_g=onorm_g, rel_bias=rel_bias, g_mix_pre=g_mix_pre,
                   g_mix_post=g_mix_post, g_ffn_pre=g_ffn_pre, g_ffn_post=g_ffn_post)
    m_small = dict(a_log=m_a_log, dt_bias=m_dt_bias, onorm_g=m_onorm_g, rel_bias=m_rel_bias, g_mix_pre=m_g_mix_pre,
                   g_mix_post=m_g_mix_post, g_ffn_pre=m_g_ffn_pre, g_ffn_post=m_g_ffn_post)
    v_small = dict(a_log=v_a_log, dt_bias=v_dt_bias, onorm_g=v_onorm_g, rel_bias=v_rel_bias, g_mix_pre=v_g_mix_pre,
                   g_mix_post=v_g_mix_post, g_ffn_pre=v_g_ffn_pre, g_ffn_post=v_g_ffn_post)

    me = 4 * lax.axis_index("x") + 2 * lax.axis_index("y") + lax.axis_index("c")
    own = lambda full, part: lax.dynamic_update_index_in_dim(full, part, me, 0)
    cols = lambda g: g.reshape(g.shape[0], N_DEV, g.shape[1] // N_DEV).transpose(1, 0, 2)
    late = ("w_out", "w_gate", "w_up", "w_down")

    late_src = [w_sh[k].astype(bf16) for k in late]
    g_in, g_conv = _all_gather([w_sh["w_in"].astype(bf16), w_sh["conv_w"]])
    g_sems, g_src, g_land, g_token = _exchange_start(late_src, g_conv, False, "late_weights_start")
    wcat = _wcat_from_blocks(g_in)
    convw = g_conv.transpose(1, 0, 2).reshape(4, 3 * GDN_W)

    def late_weights(after):
        pick = lambda idx: [g_sems[half * len(late) * N_PEER + a * N_PEER + k] for half in (0, 1) for a in idx for k in range(N_PEER)]
        (s_out,), (l_out,) = _exchange_wait(pick([0]), g_src[:1], g_land[:1], after, False, "w_out_wait")

        def ffn_weights(after2):
            srcs, lands = _exchange_wait(pick([1, 2, 3]), g_src[1:], g_land[1:], after2, False, "ffn_weights_wait")
            g_gate, g_up, g_down = [own(l, s) for l, s in zip(lands, srcs)]
            return _full_from_col_blocks(g_gate), _full_from_col_blocks(g_up), g_down.reshape(D_FF, D_MODEL)

        return own(l_out, s_out).reshape(D_MODEL, D_MODEL), ffn_weights

    early, last = {}, {}

    def early_grads(*blocks):
        early["sems"], early["src"], early["land"], token = _exchange_start(list(blocks), blocks[0], True, "late_grads_start")
        return token

    def last_grads(gw_in, gw_conv):
        src = [gw_in, cols(gw_conv)]
        last["sems"], last["src"], last["land"], token = _exchange_start(src, gw_in, True, "last_grads_start")
        return token

    loss_p, grad_x, gsmall = _local_step(
        x[0], loss_target[0], wcat, convw, late_weights, early_grads, last_grads, g_token,
        a_log, dt_bias, onorm_g, rel_bias, g_mix_pre, g_mix_post, g_ffn_pre, g_ffn_post)

    (r_small,) = _grad_exchange([], [_pack_small(gsmall, loss_p[0, 0])])
    recv = {}
    for names, ex, after, name in ((late, early, grad_x, "late_grads_wait"), (("w_in", "conv_w"), last, r_small, "last_grads_wait")):
        srcs, lands = _exchange_wait(ex["sems"], ex["src"], ex["land"], after, True, name)
        for k, l, s in zip(names, lands, srcs):
            recv[k] = own(l, lax.dynamic_index_in_dim(s, me, 0, keepdims=False))

    outs = {}
    for k in big:
        outs[k] = _adamw(recv[k], w_sh[k], m_sh[k], v_sh[k], "adamw_" + k)
    sm = _adamw(r_small, _pack_small(w_small), _pack_small(m_small), _pack_small(v_small), "adamw_small")
    loss = sm[0][5, 0]
    sm = [_unpack_small(t) for t in sm]
    for k in SMALL:
        outs[k] = tuple(t[k].reshape(w_small[k].shape) for t in sm)

    order = ("w_in", "conv_w", "a_log", "dt_bias", "onorm_g", "rel_bias", "w_out", "g_mix_pre", "g_mix_post", "w_gate",
             "w_up", "w_down", "g_ffn_pre", "g_ffn_post")
    lead = lambda k, t: t[None] if k in big else t
    res = [loss, grad_x[None]]
    for i in range(4):
        res += [lead(k, outs[k][i]) for k in order]
    return tuple(res)
```

```python
import functools
import math

import numpy as np
import jax
import jax.numpy as jnp
from jax import lax
from jax.experimental import pallas as pl
from jax.experimental.pallas import tpu as pltpu

f32 = jnp.float32
bf16 = jnp.bfloat16
SDS = jax.ShapeDtypeStruct

D_MODEL = 1024
GDN_HEADS = 4
GDN_HD = 128
GDN_W = 512
CHUNK = 64
SWA_HEADS = 8
SWA_HD = 64
SWA_W = 512
D_FF = 2816
IN_COLS = 3592
PATTERNS = ((128, 1), (512, 4), (2048, 16))
SWA_BLK = 128
NUM_BUCKETS = 32
MAX_DISTANCE = 2048
RMS_EPS = 1e-6
NEG = -1e30
N_DEV = 8

COL_A = 0
COL_B = 2048
COL_G = 3584
NCOL = 3712
LANE = 128

ADAM_LR, ADAM_B1, ADAM_B2, ADAM_EPS, ADAM_WD, ADAM_STEP = 0.001, 0.9, 0.999, 1e-08, 0.01, 10

VMEM_LIMIT = 56 * 1024 * 1024

HI = lax.Precision.HIGHEST
HIGH = lax.Precision.HIGH


def _cp(n_grid=0, vmem=None):
    kw = {}
    if n_grid:
        kw["dimension_semantics"] = ("arbitrary",) * n_grid
    if vmem:
        kw["vmem_limit_bytes"] = vmem
    return pltpu.CompilerParams(**kw)


def _dot(a, b):
    return jnp.dot(a, b, preferred_element_type=f32)


def _dot_nt(a, b):
    return lax.dot_general(a, b, (((1,), (1,)), ((), ())), preferred_element_type=f32)


def _dot_tn(a, b):
    return lax.dot_general(a, b, (((0,), (0,)), ((), ())), preferred_element_type=f32)


def _dot_hi(a, b):
    return jnp.dot(a, b, precision=HI, preferred_element_type=f32)


def _sigmoid(x):
    return 0.5 * jnp.tanh(0.5 * x) + 0.5


def _softplus(x):
    return jnp.maximum(x, 0.0) + jnp.log(1.0 + jnp.exp(-jnp.abs(x)))


def _const_spec(shape):
    nd = len(shape)
    return pl.BlockSpec(shape, lambda *_: (0,) * nd)


def _resident_spec(shape):
    nd = len(shape)
    return pl.BlockSpec(shape, lambda *_: (0,) * nd, pipeline_mode=pl.Buffered(1))


def _t5_bucket_np(dist):
    max_exact = NUM_BUCKETS // 2
    d = np.maximum(dist, 1).astype(np.float32)
    log_b = max_exact + (np.log(d / np.float32(max_exact)) / np.float32(math.log(MAX_DISTANCE / max_exact))
                         * np.float32(NUM_BUCKETS - max_exact)).astype(np.int32)
    return np.where(dist < max_exact, dist, np.minimum(log_b, NUM_BUCKETS - 1)).astype(np.int32)


def _bucket_tables():
    w = SWA_BLK
    qi = np.arange(w)[:, None]
    kj = np.arange(w)[None, :]
    rel = np.where(kj <= qi, qi - kj, qi + w - kj)
    out = np.zeros((len(PATTERNS), w, w), np.int32)
    for p, (_, dil) in enumerate(PATTERNS):
        steps = _t5_bucket_np(np.arange(w + 1) * dil)
        assert steps[w] == steps[w - 1]
        out[p] = steps[rel]
    return out


def _bias_tables(rel_bias, bk):
    def body(rb_ref, bk_ref, o_ref):
        b_idx = bk_ref[0]
        for h in range(SWA_HEADS):
            def lp(b, acc):
                return jnp.where(b_idx == b, rb_ref[b, h], acc)
            o_ref[0, h] = lax.fori_loop(0, NUM_BUCKETS, lp, jnp.zeros((SWA_BLK, SWA_BLK), f32))

    return pl.pallas_call(
        body, name="bias_tables", grid=(3,),
        in_specs=[pl.BlockSpec(memory_space=pltpu.SMEM), pl.BlockSpec((1, SWA_BLK, SWA_BLK), lambda p: (p, 0, 0))],
        out_specs=pl.BlockSpec((1, SWA_HEADS, SWA_BLK, SWA_BLK), lambda p: (p, 0, 0, 0)),
        out_shape=SDS((3, SWA_HEADS, SWA_BLK, SWA_BLK), f32),
        compiler_params=_cp(1),
    )(rel_bias, bk)


def _rel_bias_grad(dsb, bk, bk_np):
    present = [sorted(set(int(v) for v in np.unique(bk_np[p]))) for p in range(3)]

    def body(ds_ref, bk_ref, o_ref):
        row = lax.broadcasted_iota(jnp.int32, (NUM_BUCKETS, LANE), 0)
        col = lax.broadcasted_iota(jnp.int32, (NUM_BUCKETS, SWA_HEADS), 1)
        out = jnp.zeros((NUM_BUCKETS, SWA_HEADS), f32)
        for hp in range(4):
            for hh in range(2):
                acc = jnp.zeros((NUM_BUCKETS, LANE), f32)
                for p in range(3):
                    tile = ds_ref[hp, p, hh]
                    b_idx = bk_ref[p]
                    for b in present[p]:
                        part = jnp.sum(jnp.where(b_idx == b, tile, 0.0), axis=0, keepdims=True)
                        acc = acc + jnp.where(row == b, part, 0.0)
                tot = jnp.sum(acc, axis=1, keepdims=True)
                out = out + jnp.where(col == 2 * hp + hh, tot, 0.0)
        o_ref[...] = out

    return pl.pallas_call(body, name="rel_bias_grad", out_shape=SDS((NUM_BUCKETS, SWA_HEADS), f32),
                          compiler_params=_cp(0, 32 * 1024 * 1024))(dsb, bk)


def _proj_fwd(x, g_pre, wcat, after, S):
    TS = 256

    def body(x_ref, g_ref, w_ref, after_ref, o_ref, h_ref):
        xv = x_ref[...]
        r = lax.rsqrt(jnp.mean(xv * xv, axis=-1, keepdims=True) + RMS_EPS)
        h = (xv * r * g_ref[...]).astype(bf16)
        h_ref[...] = h
        o_ref[...] = _dot(h, w_ref[...])

    return pl.pallas_call(
        body, name="proj_fwd", grid=(S // TS,),
        in_specs=[pl.BlockSpec((TS, D_MODEL), lambda i: (i, 0)), _const_spec((1, D_MODEL)),
                  _resident_spec((D_MODEL, NCOL)), _ANY],
        out_specs=[pl.BlockSpec((TS, NCOL), lambda i: (i, 0)), pl.BlockSpec((TS, D_MODEL), lambda i: (i, 0))],
        out_shape=[SDS((S, NCOL), f32), SDS((S, D_MODEL), bf16)],
        compiler_params=_cp(1, VMEM_LIMIT),
    )(x, g_pre, wcat, after)


CONV_RT = 256
HALO = 8


CONV_NC = CONV_RT // CHUNK


def _gdn_prep(proj, conv_w, S):
    def body(p_ref, cw_ref, o_ref, xs_ref):
        t = pl.program_id(0)
        xs_ref[pl.ds(0, HALO), :] = jnp.zeros((HALO, LANE), f32)
        xs_ref[pl.ds(HALO, S), :] = p_ref[...]
        w = cw_ref[...]
        is_qk = t < 2
        scale = jnp.where(t == 0, GDN_HD ** -0.5, 1.0).astype(f32)

        def lp(c, carry):
            st = pl.multiple_of(c * CONV_RT, CONV_RT)
            pre = xs_ref[pl.ds(st + HALO - 3, CONV_RT), :] * w[0:1, :]
            for i in range(1, 4):
                pre = pre + xs_ref[pl.ds(st + HALO - 3 + i, CONV_RT), :] * w[i:i + 1, :]
            s = pre * _sigmoid(pre)
            nrm = s * lax.rsqrt(jnp.sum(s * s, axis=-1, keepdims=True) + 1e-6) * scale
            out = jnp.where(is_qk, nrm, s)
            for i in range(CONV_NC):
                o_ref[0, c * CONV_NC + i, 0] = out[i * CHUNK:(i + 1) * CHUNK]
            return carry

        lax.fori_loop(0, S // CONV_RT, lp, 0)

    return pl.pallas_call(
        body, name="gdn_prep", grid=(3, GDN_HEADS),
        in_specs=[pl.BlockSpec((S, LANE), lambda t, h: (0, t * GDN_HEADS + h)),
                  pl.BlockSpec((4, LANE), lambda t, h: (0, t * GDN_HEADS + h))],
        out_specs=pl.BlockSpec((1, S // CHUNK, 1, CHUNK, GDN_HD), lambda t, h: (t, 0, h, 0, 0)),
        out_shape=SDS((3, S // CHUNK, GDN_HEADS, CHUNK, GDN_HD), f32),
        scratch_shapes=[pltpu.VMEM((S + HALO, LANE), f32)],
        compiler_params=_cp(2, VMEM_LIMIT),
    )(proj, conv_w)


def _gdn_prep_bwd(proj, conv_w, dqkv, S):
    def body(p_ref, cw_ref, d_ref, dx_ref, dw_ref, xs_ref, dp_ref):
        t = pl.program_id(0)
        xs_ref[pl.ds(0, HALO), :] = jnp.zeros((HALO, LANE), f32)
        xs_ref[pl.ds(HALO, S), :] = p_ref[...]
        dp_ref[pl.ds(S, HALO), :] = jnp.zeros((HALO, LANE), f32)
        w = cw_ref[...]
        is_qk = t < 2
        scale = jnp.where(t == 0, GDN_HD ** -0.5, 1.0).astype(f32)

        def lp1(c, dw):
            st = pl.multiple_of(c * CONV_RT, CONV_RT)
            taps = [xs_ref[pl.ds(st + HALO - 3 + i, CONV_RT), :] for i in range(4)]
            pre = taps[0] * w[0:1, :]
            for i in range(1, 4):
                pre = pre + taps[i] * w[i:i + 1, :]
            sg = _sigmoid(pre)
            s = pre * sg
            d_out = jnp.concatenate([d_ref[0, c * CONV_NC + i, 0] for i in range(CONV_NC)], axis=0)
            rn = lax.rsqrt(jnp.sum(s * s, axis=-1, keepdims=True) + 1e-6)
            n = s * rn
            dn = d_out * scale
            ds_qk = rn * (dn - n * jnp.sum(dn * n, axis=-1, keepdims=True))
            ds = jnp.where(is_qk, ds_qk, d_out)
            dpre = ds * (sg * (1.0 + pre * (1.0 - sg)))
            dp_ref[pl.ds(st, CONV_RT), :] = dpre
            return tuple(dw[i] + jnp.sum(dpre * taps[i], axis=0, keepdims=True) for i in range(4))

        z = jnp.zeros((1, LANE), f32)
        dw = lax.fori_loop(0, S // CONV_RT, lp1, (z, z, z, z))
        for i in range(4):
            dw_ref[pl.ds(i, 1), :] = dw[i]

        def lp2(c, carry):
            st = pl.multiple_of(c * CONV_RT, CONV_RT)
            dx = dp_ref[pl.ds(st, CONV_RT), :] * w[3:4, :]
            for i in range(3):
                dx = dx + dp_ref[pl.ds(st + 3 - i, CONV_RT), :] * w[i:i + 1, :]
            dx_ref[pl.ds(st, CONV_RT), :] = dx.astype(bf16)
            return carry

        lax.fori_loop(0, S // CONV_RT, lp2, 0)

    col = lambda rows: pl.BlockSpec((rows, LANE), lambda t, h: (0, t * GDN_HEADS + h))
    return pl.pallas_call(
        body, name="gdn_prep_bwd", grid=(3, GDN_HEADS),
        in_specs=[col(S), col(4), pl.BlockSpec((1, S // CHUNK, 1, CHUNK, GDN_HD), lambda t, h: (t, 0, h, 0, 0))],
        out_specs=[col(S), col(4)],
        out_shape=[SDS((S, 3 * GDN_W), bf16), SDS((4, 3 * GDN_W), f32)],
        scratch_shapes=[pltpu.VMEM((S + HALO, LANE), f32), pltpu.VMEM((S + HALO, LANE), f32)],
        compiler_params=_cp(2, VMEM_LIMIT),
    )(proj, conv_w, dqkv)


def _bdot(a, b, prec=None):
    return lax.dot_general(a, b, (((2,), (1,)), ((0,), (0,))), precision=prec, preferred_element_type=f32)


def _bdot_nt(a, b, prec=None):
    return lax.dot_general(a, b, (((2,), (2,)), ((0,), (0,))), precision=prec, preferred_element_type=f32)


def _bdot_tn(a, b, prec=None):
    return lax.dot_general(a, b, (((1,), (1,)), ((0,), (0,))), precision=prec, preferred_element_type=f32)


@jax.custom_vjp
def _tri_inv_saved(a, t):
    return t


def _tri_inv_saved_fwd(a, t):
    return t, t


def _tri_inv_saved_bwd(t, dt):
    return -_bdot_tn(t, _bdot_nt(dt, t, HIGH), HIGH), jnp.zeros_like(t)


_tri_inv_saved.defvjp(_tri_inv_saved_fwd, _tri_inv_saved_bwd)


def _gdn_intra(q, k, v, beta, g, t_saved=None):
    nb = q.shape[0]
    c = CHUNK
    ii = lax.broadcasted_iota(jnp.int32, (c, c), 0)
    jj = lax.broadcasted_iota(jnp.int32, (c, c), 1)
    eye = ii == jj
    tril = ii >= jj
    strict = ii > jj
    ones = jnp.ones((nb, c, c), f32)
    eye_f = eye.astype(f32)

    g_row = _bdot(ones, jnp.where(eye, g, 0.0), HI)
    gc = jnp.sum(jnp.where(tril, g_row, 0.0), axis=2, keepdims=True)
    gc_row = _bdot(ones, jnp.where(eye, gc, 0.0), HI)
    decay = jnp.where(tril, jnp.exp(jnp.where(tril, gc - gc_row, 0.0)), 0.0)
    last = lax.broadcasted_iota(jnp.int32, (c, 1), 0) == c - 1
    gc_last = jnp.sum(jnp.where(last, gc, 0.0), axis=1, keepdims=True)
    e_gc = jnp.exp(gc)

    kb = k * beta
    k16 = k.astype(bf16)
    a = jnp.where(strict, _bdot_nt(kb.astype(bf16), k16) * decay, 0.0)
    if t_saved is None:
        xp = -a
        t_inv = eye_f + xp
        for _ in range(5):
            xp = _bdot(xp, xp, HIGH)
            t_inv = _bdot(t_inv, eye_f + xp, HIGH)
    else:
        t_inv = _tri_inv_saved(a, t_saved)
    t16 = t_inv.astype(bf16)
    u = _bdot(t16, (v * beta).astype(bf16))
    w = _bdot(t16, (kb * e_gc).astype(bf16))
    attn = jnp.where(tril, _bdot_nt(q.astype(bf16), k16) * decay, 0.0)
    gam = jnp.broadcast_to(jnp.exp(gc_last), (nb, 1, GDN_HD))
    return u, w, attn, q * e_gc, k * jnp.exp(gc_last - gc), gam, t_inv


GDN_TB = 256
GDN_NC = GDN_TB // CHUNK
GDN_NU = GDN_NC * GDN_HEADS


def _gdn_gates(pg_ref, al_ref, db_ref):
    lane1 = lax.broadcasted_iota(jnp.int32, (1, LANE), 1)
    a_lane = jnp.zeros((1, LANE), f32)
    b_lane = jnp.zeros((1, LANE), f32)
    for h in range(GDN_HEADS):
        a_lane = jnp.where(lane1 == GDN_HEADS + h, al_ref[0, h], a_lane)
        b_lane = jnp.where(lane1 == GDN_HEADS + h, db_ref[0, h], b_lane)
    pg = pg_ref[...]
    z = pg + b_lane
    return _sigmoid(pg), -jnp.exp(a_lane) * _softplus(z), z, a_lane


def _gdn_unit_inputs(qkv_ref, beta_all, g_all):
    units = [(cl, h) for cl in range(GDN_NC) for h in range(GDN_HEADS)]
    beta = jnp.stack([beta_all[cl * CHUNK:(cl + 1) * CHUNK, h:h + 1] for cl, h in units])
    g = jnp.stack([g_all[cl * CHUNK:(cl + 1) * CHUNK, GDN_HEADS + h:GDN_HEADS + h + 1] for cl, h in units])
    return qkv_ref[0], qkv_ref[1], qkv_ref[2], beta, g


def _unit_spec(*tail):
    nd = len(tail)
    return pl.BlockSpec((GDN_NU,) + tail, lambda i: (i,) + (0,) * nd)


def _gdn_intra_shapes(S):
    nu = S // CHUNK * GDN_HEADS
    row = SDS((nu, CHUNK, GDN_HD), f32)
    return [row, row, SDS((nu, CHUNK, CHUNK), f32), row, row, SDS((nu, 1, GDN_HD), f32)]


_GDN_INTRA_SPECS = lambda: [_unit_spec(CHUNK, GDN_HD), _unit_spec(CHUNK, GDN_HD), _unit_spec(CHUNK, CHUNK),
                            _unit_spec(CHUNK, GDN_HD), _unit_spec(CHUNK, GDN_HD), _unit_spec(1, GDN_HD)]


def _gdn_intra_fwd(qkv_u, proj, a_log, dt_bias, S):
    def body(qkv_ref, pg_ref, al_ref, db_ref, *outs):
        beta_all, g_all, _, _ = _gdn_gates(pg_ref, al_ref, db_ref)
        res = _gdn_intra(*_gdn_unit_inputs(qkv_ref, beta_all, g_all))
        for o_ref, r in zip(outs, res):
            o_ref[...] = r

    nu = S // CHUNK * GDN_HEADS
    *intra, t_inv = pl.pallas_call(
        body, name="gdn_intra_fwd", grid=(S // GDN_TB,),
        in_specs=[pl.BlockSpec((3, GDN_NU, CHUNK, GDN_HD), lambda i: (0, i, 0, 0)),
                  pl.BlockSpec((GDN_TB, LANE), lambda i: (i, COL_G // LANE)),
                  pl.BlockSpec(memory_space=pltpu.SMEM), pl.BlockSpec(memory_space=pltpu.SMEM)],
        out_specs=_GDN_INTRA_SPECS() + [_unit_spec(CHUNK, CHUNK)],
        out_shape=_gdn_intra_shapes(S) + [SDS((nu, CHUNK, CHUNK), f32)],
        compiler_params=_cp(1, VMEM_LIMIT),
    )(qkv_u, proj, a_log, dt_bias)
    return intra, t_inv


def _gdn_intra_bwd(qkv_u, proj, a_log, dt_bias, t_inv, cots, S):
    def body(qkv_ref, pg_ref, al_ref, db_ref, t_ref, du_ref, dw_ref, da_ref, dqd_ref, dkd_ref, dgm_ref,
             dqkv_ref, dpg_ref, dal_ref, ddb_ref):
        @pl.when(pl.program_id(0) == 0)
        def _():
            dal_ref[...] = jnp.zeros_like(dal_ref)
            ddb_ref[...] = jnp.zeros_like(ddb_ref)

        t_saved = t_ref[...]
        beta_all, g_all, z, a_lane = _gdn_gates(pg_ref, al_ref, db_ref)
        _, vjp = jax.vjp(lambda *a: _gdn_intra(*a, t_saved=t_saved)[:6], *_gdn_unit_inputs(qkv_ref, beta_all, g_all))
        dq, dk, dv, dbeta, dg = vjp((du_ref[...], dw_ref[...], da_ref[...], dqd_ref[...], dkd_ref[...], dgm_ref[...]))
        dqkv_ref[0] = dq
        dqkv_ref[1] = dk
        dqkv_ref[2] = dv
        lane = lax.broadcasted_iota(jnp.int32, (CHUNK, LANE), 1)
        rows = []
        for cl in range(GDN_NC):
            t = jnp.zeros((CHUNK, LANE), f32)
            for h in range(GDN_HEADS):
                b = cl * GDN_HEADS + h
                t = t + jnp.where(lane == h, dbeta[b], 0.0) + jnp.where(lane == GDN_HEADS + h, dg[b], 0.0)
            rows.append(t)
        d_all = jnp.concatenate(rows, axis=0)
        is_beta = lax.broadcasted_iota(jnp.int32, (GDN_TB, LANE), 1) < GDN_HEADS
        dz = d_all * (-jnp.exp(a_lane)) * _sigmoid(z)
        dpg_ref[...] = jnp.where(is_beta, d_all * beta_all * (1.0 - beta_all), dz).astype(bf16)
        dal_ref[...] += jnp.sum(jnp.where(is_beta, 0.0, d_all * g_all), axis=0, keepdims=True)
        ddb_ref[...] += jnp.sum(jnp.where(is_beta, 0.0, dz), axis=0, keepdims=True)

    acc = _const_spec((1, LANE))
    nu = S // CHUNK * GDN_HEADS
    return pl.pallas_call(
        body, name="gdn_intra_bwd", grid=(S // GDN_TB,),
        in_specs=[pl.BlockSpec((3, GDN_NU, CHUNK, GDN_HD), lambda i: (0, i, 0, 0)),
                  pl.BlockSpec((GDN_TB, LANE), lambda i: (i, COL_G // LANE)),
                  pl.BlockSpec(memory_space=pltpu.SMEM), pl.BlockSpec(memory_space=pltpu.SMEM),
                  _unit_spec(CHUNK, CHUNK)] + _GDN_INTRA_SPECS(),
        out_specs=[pl.BlockSpec((3, GDN_NU, CHUNK, GDN_HD), lambda i: (0, i, 0, 0)),
                   pl.BlockSpec((GDN_TB, LANE), lambda i: (i, 0)), acc, acc],
        out_shape=[SDS((3, nu, CHUNK, GDN_HD), f32), SDS((S, LANE), bf16), SDS((1, LANE), f32), SDS((1, LANE), f32)],
        compiler_params=_cp(1, VMEM_LIMIT),
    )(qkv_u, proj, a_log, dt_bias, t_inv, *cots)


def _gdn_scan_fwd(intra, proj, onorm_g, S):
    def body(u_ref, w_ref, at_ref, qd_ref, kd_ref, gm_ref, gate_ref, og_ref, out_ref, st_ref, s_scr):
        @pl.when(pl.program_id(0) == 0)
        def _():
            s_scr[...] = jnp.zeros_like(s_scr)

        og = og_ref[...]
        s = s_scr[...]
        chain = []
        for cl in range(GDN_NC):
            us = slice(cl * GDN_HEADS, (cl + 1) * GDN_HEADS)
            st_ref[us] = s
            s16 = s.astype(bf16)
            vn16 = (u_ref[us] - _bdot(w_ref[us].astype(bf16), s16)).astype(bf16)
            chain.append((us, s16, vn16))
            s = s * gm_ref[us] + _bdot_tn(kd_ref[us].astype(bf16), vn16)
        s_scr[...] = s
        for cl, (us, s16, vn16) in enumerate(chain):
            rows = slice(cl * CHUNK, (cl + 1) * CHUNK)
            o = _bdot(qd_ref[us].astype(bf16), s16) + _bdot(at_ref[us].astype(bf16), vn16)
            for h in range(GDN_HEADS):
                oh = o[h]
                gt = gate_ref[rows, h * GDN_HD:(h + 1) * GDN_HD]
                on = oh * lax.rsqrt(jnp.mean(oh * oh, axis=-1, keepdims=True) + RMS_EPS) * og
                out_ref[rows, h * GDN_HD:(h + 1) * GDN_HD] = on * (gt * _sigmoid(gt))

    nu = S // CHUNK * GDN_HEADS
    return pl.pallas_call(
        body, name="gdn_scan_fwd", grid=(S // GDN_TB,),
        in_specs=_GDN_INTRA_SPECS() + [pl.BlockSpec((GDN_TB, GDN_W), lambda i: (i, 3)), _const_spec((1, GDN_HD))],
        out_specs=[pl.BlockSpec((GDN_TB, GDN_W), lambda i: (i, 0)), _unit_spec(GDN_HD, GDN_HD)],
        out_shape=[SDS((S, GDN_W), f32), SDS((nu, GDN_HD, GDN_HD), f32)],
        scratch_shapes=[pltpu.VMEM((GDN_HEADS, GDN_HD, GDN_HD), f32)],
        compiler_params=_cp(1, VMEM_LIMIT),
    )(*intra, proj, onorm_g)


def _gdn_scan_bwd(intra, states, proj, d_oab, onorm_g, after, S):
    n_steps = S // GDN_TB

    def body(u_ref, w_ref, at_ref, qd_ref, kd_ref, gm_ref, st_ref, gate_ref, do_ref, og_ref, after_ref,
             du_ref, dw_ref, dat_ref, dqd_ref, dkd_ref, dgm_ref, dgate_ref, dog_ref, ds_scr):
        @pl.when(pl.program_id(0) == 0)
        def _():
            ds_scr[...] = jnp.zeros_like(ds_scr)
            dog_ref[...] = jnp.zeros_like(dog_ref)

        og = og_ref[...]
        ii = lax.broadcasted_iota(jnp.int32, (CHUNK, CHUNK), 0)
        jj = lax.broadcasted_iota(jnp.int32, (CHUNK, CHUNK), 1)
        tril = ii >= jj
        dog = jnp.zeros((1, GDN_HD), f32)
        pre = []
        for cl in range(GDN_NC):
            us = slice(cl * GDN_HEADS, (cl + 1) * GDN_HEADS)
            rows = slice(cl * CHUNK, (cl + 1) * CHUNK)
            s016 = st_ref[us].astype(bf16)
            w16 = w_ref[us].astype(bf16)
            qd16 = qd_ref[us].astype(bf16)
            at16 = at_ref[us].astype(bf16)
            vn16 = (u_ref[us] - _bdot(w16, s016)).astype(bf16)
            o = _bdot(qd16, s016) + _bdot(at16, vn16)
            do_h = []
            for h in range(GDN_HEADS):
                oh = o[h]
                lanes = slice(h * GDN_HD, (h + 1) * GDN_HD)
                gt = gate_ref[rows, lanes]
                d_out = do_ref[rows, lanes]
                r = lax.rsqrt(jnp.mean(oh * oh, axis=-1, keepdims=True) + RMS_EPS)
                n = oh * r
                sg = _sigmoid(gt)
                silu = gt * sg
                dog = dog + jnp.sum(d_out * n * silu, axis=0, keepdims=True)
                dgate_ref[rows, lanes] = (d_out * n * og * (sg * (1.0 + gt * (1.0 - sg)))).astype(bf16)
                dn = d_out * og * silu
                do_h.append(r * (dn - n * jnp.mean(dn * n, axis=-1, keepdims=True)))
            do16 = jnp.stack(do_h).astype(bf16)
            pre.append((us, s016, w16, vn16, do16, _bdot_tn(at16, do16), _bdot_tn(qd16, do16)))
        ds = ds_scr[...]
        chain = [None] * GDN_NC
        for cl in reversed(range(GDN_NC)):
            us, s016, w16, vn16, do16, at_do, qd_do = pre[cl]
            ds16 = ds.astype(bf16)
            dvn = at_do + _bdot(kd_ref[us].astype(bf16), ds16)
            dvn16 = dvn.astype(bf16)
            chain[cl] = (ds, ds16, dvn, dvn16)
            ds = qd_do + ds * gm_ref[us] - _bdot_tn(w16, dvn16)
        ds_scr[...] = ds
        for cl in range(GDN_NC):
            us, s016, w16, vn16, do16, _, _ = pre[cl]
            ds_in, ds16, dvn, dvn16 = chain[cl]
            du_ref[us] = dvn
            dw_ref[us] = -_bdot_nt(dvn16, s016)
            dat_ref[us] = jnp.where(tril, _bdot_nt(do16, vn16), 0.0)
            dqd_ref[us] = _bdot_nt(do16, s016)
            dkd_ref[us] = _bdot_nt(vn16, ds16)
            dgm_ref[us] = jnp.sum(st_ref[us] * ds_in, axis=1, keepdims=True)
        dog_ref[...] += dog

    def unit(*tail):
        nd = len(tail)
        return pl.BlockSpec((GDN_NU,) + tail, lambda i: (n_steps - 1 - i,) + (0,) * nd)

    intra_specs = [unit(CHUNK, GDN_HD), unit(CHUNK, GDN_HD), unit(CHUNK, CHUNK), unit(CHUNK, GDN_HD),
                   unit(CHUNK, GDN_HD), unit(1, GDN_HD)]
    tok = lambda c: pl.BlockSpec((GDN_TB, GDN_W), lambda i: (n_steps - 1 - i, c))
    return pl.pallas_call(
        body, name="gdn_scan_bwd", grid=(n_steps,),
        in_specs=intra_specs + [unit(GDN_HD, GDN_HD), tok(3), tok(0), _const_spec((1, GDN_HD)), _ANY],
        out_specs=intra_specs + [tok(0), _const_spec((1, GDN_HD))],
        out_shape=_gdn_intra_shapes(S) + [SDS((S, GDN_W), bf16), SDS((1, GDN_HD), f32)],
        scratch_shapes=[pltpu.VMEM((GDN_HEADS, GDN_HD, GDN_HD), f32)],
        compiler_params=_cp(1, VMEM_LIMIT),
    )(*intra, states, proj, d_oab, onorm_g, after)


SWA_UNROLL = 4


def _swa_tiles(q_ref, k_ref, v_ref, it, d, nb_log2, S):
    nb = 1 << nb_log2
    r = lax.shift_right_logical(it, nb_log2)
    blk = lax.bitwise_and(it, nb - 1)
    qs = blk * (SWA_BLK * d) + r
    ps = jnp.maximum(blk - 1, 0) * (SWA_BLK * d) + r
    if d > 1:
        rows_c, rows_p = pl.ds(qs, SWA_BLK, stride=d), pl.ds(ps, SWA_BLK, stride=d)
    else:
        rows_c, rows_p = pl.ds(pl.multiple_of(qs, SWA_BLK), SWA_BLK), pl.ds(pl.multiple_of(ps, SWA_BLK), SWA_BLK)
    return rows_c, rows_p, blk > 0


def _swa_prev_modes(nb):
    if nb >= SWA_UNROLL:
        return ["load"] + ["reuse"] * (SWA_UNROLL - 1)
    return ["none" if u % nb == 0 else "reuse" for u in range(SWA_UNROLL)]


def _swa_fwd(proj, bt, S):
    scale = SWA_HD ** -0.5

    def body(q_ref, k_ref, v_ref, bt_ref, o_ref, lse0_ref, lse1_ref, m0_scr, m1_scr, a0_scr, a1_scr):
        lane = lax.broadcasted_iota(jnp.int32, (SWA_BLK, LANE), 1)
        h0 = lane < SWA_HD
        qi = lax.broadcasted_iota(jnp.int32, (SWA_BLK, SWA_BLK), 0)
        kj = lax.broadcasted_iota(jnp.int32, (SWA_BLK, SWA_BLK), 1)
        lower = kj <= qi
        ones16 = jnp.ones((LANE, SWA_BLK), bf16)
        m_scrs = (m0_scr, m1_scr)
        a_scrs = (a0_scr, a1_scr)
        for p, (_, d) in reversed(list(enumerate(PATTERNS))):
            nb_log2 = int(math.log2(S // d // SWA_BLK))
            first = p == len(PATTERNS) - 1

            def lp(i, carry, p=p, d=d, nb_log2=nb_log2, first=first):
                heads = [h0, jnp.logical_not(h0)]
                modes = _swa_prev_modes(1 << nb_log2)
                tiles = []
                kc_f = None
                for u in range(SWA_UNROLL):
                    rows_c, rows_p, has_prev = _swa_tiles(q_ref, k_ref, v_ref, i * SWA_UNROLL + u, d, nb_log2, S)
                    kp_f = {"load": lambda: k_ref[rows_p, :], "reuse": lambda: kc_f, "none": lambda: None}[modes[u]]()
                    has_prev = {"load": has_prev, "reuse": True, "none": False}[modes[u]]
                    q = q_ref[rows_c, :]
                    kc_f = k_ref[rows_c, :]
                    kc = kc_f.astype(bf16)
                    logits = []
                    for mh in heads:
                        q_h = jnp.where(mh, q, 0.0)
                        qh = q_h.astype(bf16)
                        if kp_f is None:
                            logits.append((_dot_nt(qh, kc), None, None))
                        else:
                            logits.append((_dot_nt(qh, kc), _dot_nt(qh, kp_f.astype(bf16)), _dot((q_h * kp_f).astype(bf16), ones16)))
                    tiles.append((rows_c, rows_p, has_prev, logits))
                probs = []
                for rows_c, rows_p, has_prev, logits in tiles:
                    per_head = []
                    for h, (s_c, s_p, far) in enumerate(logits):
                        if has_prev is False:
                            s = jnp.where(lower, s_c * scale + bt_ref[p, h], NEG)
                            s_far = None
                        else:
                            s = jnp.where(lower, s_c, s_p) * scale + bt_ref[p, h]
                            s_far = far * scale + bt_ref[p, h, SWA_BLK - 1:SWA_BLK, 0:1]
                            if has_prev is not True:
                                s = jnp.where(jnp.logical_or(lower, has_prev), s, NEG)
                                s_far = jnp.where(has_prev, s_far, NEG)
                        mn = jnp.max(s, axis=1, keepdims=True)
                        if s_far is not None:
                            mn = jnp.maximum(s_far, mn)
                        alpha = None
                        if not first:
                            mo = m_scrs[h][rows_c, :]
                            mn = jnp.maximum(mo, mn)
                            alpha = jnp.exp(mo - mn)
                        mn = jnp.broadcast_to(mn, (SWA_BLK, LANE))
                        pm = jnp.exp(s - mn)
                        per_head.append((mn, alpha, None if s_far is None else jnp.exp(s_far - mn),
                                         jnp.where(lower, pm, 0.0).astype(bf16),
                                         None if s_far is None else jnp.where(lower, 0.0, pm).astype(bf16)))
                    probs.append(per_head)
                acc_old = [None if first else (a0_scr[t[0], :], a1_scr[t[0], :]) for t in tiles]
                done = []
                vc = None
                for u, ((rows_c, rows_p, _, _), per_head, old) in enumerate(zip(tiles, probs, acc_old)):
                    vp = {"load": lambda: v_ref[rows_p, :], "reuse": lambda: vc, "none": lambda: None}[modes[u]]()
                    vc = v_ref[rows_c, :]
                    acc_new = []
                    for h, (mn, alpha, p_far, pc16, pp16) in enumerate(per_head):
                        pv = _dot(pc16, jnp.where(heads[h], vc, 1.0).astype(bf16))
                        if pp16 is not None:
                            vpa = jnp.where(heads[h], vp, 1.0)
                            pv = pv + _dot(pp16, vpa.astype(bf16)) + p_far * vpa
                        acc_new.append(pv if first else alpha * old[h] + pv)
                    done.append((rows_c, per_head[0][0], per_head[1][0], acc_new[0], acc_new[1]))
                for rows_c, m0_new, m1_new, a0_new, a1_new in done:
                    m0_scr[rows_c, :] = m0_new
                    m1_scr[rows_c, :] = m1_new
                    a0_scr[rows_c, :] = a0_new
                    a1_scr[rows_c, :] = a1_new
                return carry

            lax.fori_loop(0, S // SWA_BLK // SWA_UNROLL, lp, 0)

        def fin(c, carry):
            rows = pl.ds(pl.multiple_of(c * SWA_BLK, SWA_BLK), SWA_BLK)
            a0 = a0_scr[rows, :]
            a1 = a1_scr[rows, :]
            l0 = jnp.where(h0, pltpu.roll(a0, SWA_HD, 1), a0)
            l1 = jnp.where(h0, a1, pltpu.roll(a1, SWA_HD, 1))
            o_ref[rows, :] = jnp.where(h0, a0 / l0, a1 / l1)
            lse0_ref[rows, :] = m0_scr[rows, :] + jnp.log(l0)
            lse1_ref[rows, :] = m1_scr[rows, :] + jnp.log(l1)
            return carry

        lax.fori_loop(0, S // SWA_BLK, fin, 0)

    qb = COL_B // LANE
    col = lambda c: pl.BlockSpec((S, LANE), lambda hp, c=c: (0, c + hp))
    return pl.pallas_call(
        body, name="swa_fwd", grid=(4,),
        in_specs=[col(qb), col(qb + 4), col(qb + 8), pl.BlockSpec((3, 2, SWA_BLK, SWA_BLK), lambda hp: (0, hp, 0, 0))],
        out_specs=[col(0), col(0), col(0)],
        out_shape=[SDS((S, SWA_W), f32)] * 3,
        scratch_shapes=[pltpu.VMEM((S, LANE), f32)] * 4,
        compiler_params=_cp(1, VMEM_LIMIT),
    )(proj, proj, proj, bt)


def _swa_bwd(proj, bt, ob, lse0, lse1, d_oab, after, S):
    scale = SWA_HD ** -0.5

    def body(q_ref, k_ref, v_ref, bt_ref, o_ref, lse0_ref, lse1_ref, do_ref, after_ref, dq_ref, dk_ref, dv_ref, dsb_ref,
             dq_scr, dk_scr, dv_scr, nd_scr):
        lane = lax.broadcasted_iota(jnp.int32, (SWA_BLK, LANE), 1)
        h0 = lane < SWA_HD
        qi = lax.broadcasted_iota(jnp.int32, (SWA_BLK, SWA_BLK), 0)
        kj = lax.broadcasted_iota(jnp.int32, (SWA_BLK, SWA_BLK), 1)
        lower = kj <= qi
        eye = kj == qi
        rel127 = jnp.logical_or(kj == qi + 1, jnp.logical_and(qi == SWA_BLK - 1, kj == 0))
        ones16 = jnp.ones((LANE, SWA_BLK), bf16)
        lse_refs = (lse0_ref, lse1_ref)
        dk_scr[...] = jnp.zeros((S, LANE), f32)
        dv_scr[...] = jnp.zeros((S, LANE), f32)
        dsb_ref[...] = jnp.zeros_like(dsb_ref)

        def row_sum(a):
            hi = a.astype(bf16)
            return _dot(hi, ones16) + _dot((a - hi.astype(f32)).astype(bf16), ones16)

        def prep(c, carry):
            rows = pl.ds(pl.multiple_of(c * SWA_BLK, SWA_BLK), SWA_BLK)
            dlt = do_ref[rows, :] * o_ref[rows, :]
            d0 = row_sum(jnp.where(h0, dlt, 0.0))
            d1 = row_sum(jnp.where(h0, 0.0, dlt))
            nd_scr[rows, :] = jnp.where(h0, d1, d0) * (-1.0 / SWA_HD)
            return carry

        lax.fori_loop(0, S // SWA_BLK, prep, 0)

        for p, (_, d) in reversed(list(enumerate(PATTERNS))):
            nb_log2 = int(math.log2(S // d // SWA_BLK))
            first = p == len(PATTERNS) - 1

            def lp(i, carry, p=p, d=d, nb_log2=nb_log2, first=first):
                heads = [h0, jnp.logical_not(h0)]
                modes = _swa_prev_modes(1 << nb_log2)
                tiles = []
                kc_f = vc_f = None
                for u in range(SWA_UNROLL):
                    rows_c, rows_p, has_prev = _swa_tiles(q_ref, k_ref, v_ref, i * SWA_UNROLL + u, d, nb_log2, S)
                    kp_f = {"load": lambda: k_ref[rows_p, :], "reuse": lambda: kc_f, "none": lambda: None}[modes[u]]()
                    vp_f = {"load": lambda: v_ref[rows_p, :], "reuse": lambda: vc_f, "none": lambda: None}[modes[u]]()
                    has_prev = {"load": has_prev, "reuse": True, "none": False}[modes[u]]
                    q = q_ref[rows_c, :]
                    kc_f = k_ref[rows_c, :]
                    vc_f = v_ref[rows_c, :]
                    kc = kc_f.astype(bf16)
                    kp = None if kp_f is None else kp_f.astype(bf16)
                    do = do_ref[rows_c, :]
                    nd = nd_scr[rows_c, :]
                    per_head = []
                    for mh in heads:
                        q_h = jnp.where(mh, q, 0.0)
                        do_a = jnp.where(mh, do, nd)
                        qh = q_h.astype(bf16)
                        doa = do_a.astype(bf16)
                        doh = jnp.where(mh, do, 0.0).astype(bf16)
                        dd_c = _dot_nt(doa, jnp.where(mh, vc_f, 1.0).astype(bf16))
                        if kp_f is None:
                            per_head.append((qh, doh, _dot_nt(qh, kc), None, None, dd_c, None, None))
                        else:
                            vpa = jnp.where(mh, vp_f, 1.0)
                            per_head.append((qh, doh, _dot_nt(qh, kc), _dot_nt(qh, kp), _dot((q_h * kp_f).astype(bf16), ones16),
                                             dd_c, _dot_nt(doa, vpa.astype(bf16)), _dot((do_a * vpa).astype(bf16), ones16)))
                    tiles.append((rows_c, rows_p, has_prev, kc, kp, per_head))
                grads = []
                for rows_c, rows_p, has_prev, kc, kp, per_head in tiles:
                    out = []
                    for h, (qh, doh, s_c, s_p, far, dd_c, dd_p, dd_far) in enumerate(per_head):
                        lse_h = lse_refs[h][rows_c, :]
                        if has_prev is False:
                            pm = jnp.exp(jnp.where(lower, s_c * scale + bt_ref[p, h], NEG) - lse_h)
                            dsm = pm * dd_c
                            out.append((dsm, dsm.astype(bf16), None, pm.astype(bf16), None))
                            continue
                        s = jnp.where(lower, s_c, s_p) * scale + bt_ref[p, h]
                        s_far = far * scale + bt_ref[p, h, SWA_BLK - 1:SWA_BLK, 0:1]
                        if has_prev is not True:
                            s = jnp.where(jnp.logical_or(lower, has_prev), s, NEG)
                            s_far = jnp.where(has_prev, s_far, NEG)
                        pm = jnp.exp(s - lse_h)
                        p_far = jnp.exp(s_far - lse_h)
                        dsm = pm * jnp.where(lower, dd_c, dd_p)
                        ds_far = p_far * dd_far
                        out.append((dsm + jnp.where(rel127, ds_far, 0.0),
                                    jnp.where(lower, dsm, 0.0).astype(bf16),
                                    jnp.where(lower, jnp.where(eye, ds_far, 0.0), dsm).astype(bf16),
                                    jnp.where(lower, pm, 0.0).astype(bf16),
                                    jnp.where(lower, jnp.where(eye, p_far, 0.0), pm).astype(bf16)))
                    grads.append(out)
                done = []
                add = lambda acc, t: t if acc is None else acc + t
                for (rows_c, rows_p, _, kc, kp, per_head), out in zip(tiles, grads):
                    dq_t = dkc_t = dkp_t = dvc_t = dvp_t = None
                    for h, (_, dsc16, dsp16, pc16, pp16) in enumerate(out):
                        qh, doh = per_head[h][0], per_head[h][1]
                        dq_h = _dot(dsc16, kc)
                        dkc_t = add(dkc_t, _dot_tn(dsc16, qh) * scale)
                        dvc_t = add(dvc_t, _dot_tn(pc16, doh))
                        if dsp16 is not None:
                            dq_h = dq_h + _dot(dsp16, kp)
                            dkp_t = add(dkp_t, _dot_tn(dsp16, qh) * scale)
                            dvp_t = add(dvp_t, _dot_tn(pp16, doh))
                        dq_t = add(dq_t, jnp.where(heads[h], dq_h * scale, 0.0))
                    done.append([rows_c, rows_p, dq_t, dkc_t, dkp_t, dvc_t, dvp_t])
                for u in range(1, SWA_UNROLL):
                    if modes[u] == "reuse":
                        done[u - 1][3] = done[u - 1][3] + done[u][4]
                        done[u - 1][5] = done[u - 1][5] + done[u][6]
                for h in range(2):
                    tot = grads[0][h][0]
                    for g in grads[1:]:
                        tot = tot + g[h][0]
                    dsb_ref[0, p, h] += tot
                for u, (rows_c, rows_p, dq_t, dkc_t, dkp_t, dvc_t, dvp_t) in enumerate(done):
                    dq_scr[rows_c, :] = dq_t if first else dq_scr[rows_c, :] + dq_t
                    dk_scr[rows_c, :] = dk_scr[rows_c, :] + dkc_t
                    dv_scr[rows_c, :] = dv_scr[rows_c, :] + dvc_t
                    if modes[u] == "load":
                        dk_scr[rows_p, :] = dk_scr[rows_p, :] + dkp_t
                        dv_scr[rows_p, :] = dv_scr[rows_p, :] + dvp_t
                return carry

            lax.fori_loop(0, S // SWA_BLK // SWA_UNROLL, lp, 0)
        dq_ref[...] = dq_scr[...].astype(bf16)
        dk_ref[...] = dk_scr[...].astype(bf16)
        dv_ref[...] = dv_scr[...].astype(bf16)

    qb = COL_B // LANE
    col = lambda c: pl.BlockSpec((S, LANE), lambda hp, c=c: (0, c + hp))
    return pl.pallas_call(
        body, name="swa_bwd", grid=(4,),
        in_specs=[col(qb), col(qb + 4), col(qb + 8),
                  pl.BlockSpec((3, 2, SWA_BLK, SWA_BLK), lambda hp: (0, hp, 0, 0)),
                  col(0), col(0), col(0), col(4), _ANY],
        out_specs=[col(0), col(0), col(0),
                   pl.BlockSpec((1, 3, 2, SWA_BLK, SWA_BLK), lambda hp: (hp, 0, 0, 0, 0))],
        out_shape=[SDS((S, SWA_W), bf16)] * 3 + [SDS((4, 3, 2, SWA_BLK, SWA_BLK), f32)],
        scratch_shapes=[pltpu.VMEM((S, LANE), f32)] * 4,
        compiler_params=_cp(1, VMEM_LIMIT),
    )(proj, proj, proj, bt, ob, lse0, lse1, d_oab, after)


def _mix_fwd(oa, ob, w_out, x, g_post, S):
    TS = 512

    def body(oa_ref, ob_ref, w_ref, x_ref, g_ref, mix_ref, x1_ref):
        mix = _dot(oa_ref[...].astype(bf16), w_ref[0:GDN_W, :]) + _dot(ob_ref[...].astype(bf16), w_ref[GDN_W:D_MODEL, :])
        r = lax.rsqrt(jnp.mean(mix * mix, axis=-1, keepdims=True) + RMS_EPS)
        mix_ref[...] = mix
        x1_ref[...] = x_ref[...] + mix * r * g_ref[...]

    row = lambda w: pl.BlockSpec((TS, w), lambda i: (i, 0))
    return pl.pallas_call(
        body, name="mix_fwd", grid=(S // TS,),
        in_specs=[row(GDN_W), row(SWA_W), _resident_spec((D_MODEL, D_MODEL)), row(D_MODEL), _const_spec((1, D_MODEL))],
        out_specs=[row(D_MODEL), row(D_MODEL)],
        out_shape=[SDS((S, D_MODEL), f32), SDS((S, D_MODEL), f32)],
        compiler_params=_cp(1, VMEM_LIMIT),
    )(oa, ob, w_out, x, g_post)


def _mix_bwd(dx1, mix, g_post, w_out, S):
    TS = 512

    def body(dx1_ref, mix_ref, g_ref, w_ref, dmix_ref, doab_ref, dg_ref):
        @pl.when(pl.program_id(0) == 0)
        def _():
            dg_ref[...] = jnp.zeros_like(dg_ref)

        mix = mix_ref[...]
        dz = dx1_ref[...]
        r = lax.rsqrt(jnp.mean(mix * mix, axis=-1, keepdims=True) + RMS_EPS)
        n = mix * r
        dg_ref[...] += jnp.sum(dz * n, axis=0, keepdims=True)
        dn = dz * g_ref[...]
        dmix = (r * (dn - n * jnp.mean(dn * n, axis=-1, keepdims=True))).astype(bf16)
        dmix_ref[...] = dmix
        doab_ref[...] = _dot_nt(dmix, w_ref[...])

    row = lambda: pl.BlockSpec((TS, D_MODEL), lambda i: (i, 0))
    return pl.pallas_call(
        body, name="mix_bwd", grid=(S // TS,),
        in_specs=[row(), row(), _const_spec((1, D_MODEL)), _resident_spec((D_MODEL, D_MODEL))],
        out_specs=[row(), row(), _const_spec((1, D_MODEL))],
        out_shape=[SDS((S, D_MODEL), bf16), SDS((S, D_MODEL), f32), SDS((1, D_MODEL), f32)],
        compiler_params=_cp(1, VMEM_LIMIT),
    )(dx1, mix, g_post, w_out)


FFN_TS = 256
FFN_CH = 1408


def _ffn(x1, tgt, g_pre, g_post, wg, wu, wd, S):
    def body(x1_ref, t_ref, gp_ref, gq_ref, wg_ref, wu_ref, wd_ref,
             dx1_ref, h2_ref, act_ref, dgate_ref, dup_ref, df_ref, loss_ref, dgp_ref, dgq_ref, gate_scr, up_scr):
        @pl.when(pl.program_id(0) == 0)
        def _():
            loss_ref[...] = jnp.zeros_like(loss_ref)
            dgp_ref[...] = jnp.zeros_like(dgp_ref)
            dgq_ref[...] = jnp.zeros_like(dgq_ref)

        x1v = x1_ref[...]
        gp = gp_ref[...]
        gq = gq_ref[...]
        r2 = lax.rsqrt(jnp.mean(x1v * x1v, axis=-1, keepdims=True) + RMS_EPS)
        n2 = x1v * r2
        h2 = (n2 * gp).astype(bf16)
        h2_ref[...] = h2
        chunks = [slice(c * FFN_CH, (c + 1) * FFN_CH) for c in range(D_FF // FFN_CH)]
        for cs in chunks:
            gate_scr[:, cs] = _dot(h2, wg_ref[:, cs])
            up_scr[:, cs] = _dot(h2, wu_ref[:, cs])
        acts = []
        for cs in chunks:
            gate = gate_scr[:, cs]
            act = (gate * _sigmoid(gate) * up_scr[:, cs]).astype(bf16)
            act_ref[:, cs] = act
            acts.append(act)
        f = _dot(acts[0], wd_ref[chunks[0], :])
        for act, cs in zip(acts[1:], chunks[1:]):
            f = f + _dot(act, wd_ref[cs, :])
        r3 = lax.rsqrt(jnp.mean(f * f, axis=-1, keepdims=True) + RMS_EPS)
        n3 = f * r3
        err = x1v + n3 * gq - t_ref[...]
        loss_ref[...] += 0.5 * jnp.sum(jnp.mean(err * err, axis=-1, keepdims=True), axis=0, keepdims=True)
        dy = err * (1.0 / D_MODEL)
        dgq_ref[...] += jnp.sum(dy * n3, axis=0, keepdims=True)
        dn3 = dy * gq
        df = (r3 * (dn3 - n3 * jnp.mean(dn3 * n3, axis=-1, keepdims=True))).astype(bf16)
        df_ref[...] = df
        dacts = [_dot_nt(df, wd_ref[cs, :]) for cs in chunks]
        dgs = []
        for dact, cs in zip(dacts, chunks):
            gate = gate_scr[:, cs]
            sg = _sigmoid(gate)
            dup = (dact * gate * sg).astype(bf16)
            dgate = (dact * up_scr[:, cs] * (sg * (1.0 + gate * (1.0 - sg)))).astype(bf16)
            dup_ref[:, cs] = dup
            dgate_ref[:, cs] = dgate
            dgs.append((dgate, dup))
        dh2 = None
        for (dgate, dup), cs in zip(dgs, chunks):
            t = _dot_nt(dgate, wg_ref[:, cs]) + _dot_nt(dup, wu_ref[:, cs])
            dh2 = t if dh2 is None else dh2 + t
        dgp_ref[...] += jnp.sum(dh2 * n2, axis=0, keepdims=True)
        dn2 = dh2 * gp
        dx1_ref[...] = dy + r2 * (dn2 - n2 * jnp.mean(dn2 * n2, axis=-1, keepdims=True))

    row = lambda w: pl.BlockSpec((FFN_TS, w), lambda i: (i, 0))
    vec = _const_spec((1, D_MODEL))
    return pl.pallas_call(
        body, name="ffn_fwd_bwd", grid=(S // FFN_TS,),
        in_specs=[row(D_MODEL), row(D_MODEL), vec, vec, _resident_spec((D_MODEL, D_FF)), _resident_spec((D_MODEL, D_FF)),
                  _resident_spec((D_FF, D_MODEL))],
        out_specs=[row(D_MODEL), row(D_MODEL), row(D_FF), row(D_FF), row(D_FF), row(D_MODEL), _const_spec((1, LANE)), vec, vec],
        out_shape=[SDS((S, D_MODEL), f32), SDS((S, D_MODEL), bf16), SDS((S, D_FF), bf16), SDS((S, D_FF), bf16),
                   SDS((S, D_FF), bf16), SDS((S, D_MODEL), bf16), SDS((1, LANE), f32), SDS((1, D_MODEL), f32),
                   SDS((1, D_MODEL), f32)],
        scratch_shapes=[pltpu.VMEM((FFN_TS, D_FF), f32), pltpu.VMEM((FFN_TS, D_FF), f32)],
        compiler_params=_cp(1, VMEM_LIMIT),
    )(x1, tgt, g_pre, g_post, wg, wu, wd)


def _proj_bwd(x, dx1, g_pre, wcat, segs, after, S):
    TS = 256
    n = len(segs)
    cols = [(c0, a.shape[1]) for a, c0 in segs]

    def body(*refs):
        x_ref, dx1_ref, g_ref, w_ref = refs[:4]
        seg_refs = refs[4:4 + n]
        gx_ref, dg_ref = refs[5 + n:]

        @pl.when(pl.program_id(0) == 0)
        def _():
            dg_ref[...] = jnp.zeros_like(dg_ref)

        dh = jnp.zeros((TS, D_MODEL), f32)
        for s_ref, (c0, w) in zip(seg_refs, cols):
            dh = dh + _dot_nt(s_ref[...], w_ref[:, c0:c0 + w])
        xv = x_ref[...]
        g = g_ref[...]
        r = lax.rsqrt(jnp.mean(xv * xv, axis=-1, keepdims=True) + RMS_EPS)
        nx = xv * r
        dg_ref[...] += jnp.sum(dh * nx, axis=0, keepdims=True)
        dn = dh * g
        gx_ref[...] = dx1_ref[...] + r * (dn - nx * jnp.mean(dn * nx, axis=-1, keepdims=True))

    row = lambda w: pl.BlockSpec((TS, w), lambda i: (i, 0))
    return pl.pallas_call(
        body, name="proj_bwd", grid=(S // TS,),
        in_specs=[row(D_MODEL), row(D_MODEL), _const_spec((1, D_MODEL)), _resident_spec((D_MODEL, NCOL))]
                 + [row(w) for _, w in cols] + [_ANY],
        out_specs=[row(D_MODEL), _const_spec((1, D_MODEL))],
        out_shape=[SDS((S, D_MODEL), f32), SDS((1, D_MODEL), f32)],
        compiler_params=_cp(1, VMEM_LIMIT),
    )(x, dx1, g_pre, wcat, *[a for a, _ in segs], after)


def _wgrad(a, b, S, name, col_blocks=False):
    TS = 1024
    K = a.shape[1]
    N = b.shape[1]
    TN = next(t for t in (512, 1408, N) if N % t == 0)
    cb = N // N_DEV
    nblk = TN // cb if col_blocks else 0

    def body(a_ref, b_ref, o_ref, acc):
        @pl.when(pl.program_id(1) == 0)
        def _():
            acc[...] = jnp.zeros_like(acc)

        acc[...] += _dot_tn(a_ref[...].astype(bf16), b_ref[...])

        @pl.when(pl.program_id(1) == pl.num_programs(1) - 1)
        def _():
            if col_blocks:
                for i in range(nblk):
                    o_ref[i] = acc[:, i * cb:(i + 1) * cb].astype(bf16)
            else:
                o_ref[...] = acc[...].astype(bf16)

    if col_blocks:
        out_spec = pl.BlockSpec((nblk, K, cb), lambda j, s: (j, 0, 0))
        out_shape = SDS((N_DEV, K, cb), bf16)
    else:
        out_spec = pl.BlockSpec((K, TN), lambda j, s: (0, j))
        out_shape = SDS((K, N), bf16)
    return pl.pallas_call(
        body, name=name, grid=(N // TN, S // TS),
        in_specs=[pl.BlockSpec((TS, K), lambda j, s: (s, 0)), pl.BlockSpec((TS, TN), lambda j, s: (s, j))],
        out_specs=out_spec, out_shape=out_shape,
        scratch_shapes=[pltpu.VMEM((K, TN), f32)],
        compiler_params=_cp(2, VMEM_LIMIT),
    )(a, b)


def _w_in_pieces():
    n_a, n_g = 4 * GDN_W, 2 * GDN_HEADS
    cb = IN_COLS // N_DEV
    bounds = [(0, n_a, COL_A), (n_a, n_a + n_g, COL_G), (n_a + n_g, IN_COLS, COL_B)]
    out = []
    for j in range(N_DEV):
        lo, hi = j * cb, (j + 1) * cb
        for s0, s1, dst in bounds:
            a, b = max(lo, s0), min(hi, s1)
            if a < b:
                out.append((j, a - lo, b - a, dst + a - s0))
    return out


def _wcat_from_blocks(g_in):
    TR = 256
    cb = IN_COLS // N_DEV
    pieces = _w_in_pieces()

    def body(w_ref, o_ref):
        o_ref[:, COL_G:NCOL] = jnp.zeros((TR, NCOL - COL_G), bf16)
        for j, off, w, dst in pieces:
            o_ref[:, dst:dst + w] = w_ref[j, :, off:off + w]

    return pl.pallas_call(
        body, name="wcat_from_blocks", grid=(D_MODEL // TR,),
        in_specs=[pl.BlockSpec((N_DEV, TR, cb), lambda i: (0, i, 0))],
        out_specs=pl.BlockSpec((TR, NCOL), lambda i: (i, 0)),
        out_shape=SDS((D_MODEL, NCOL), bf16),
        compiler_params=_cp(1, VMEM_LIMIT),
    )(g_in)


def _full_from_col_blocks(g):
    n, R, C = g.shape
    TR = 256

    def body(w_ref, o_ref):
        for j in range(n):
            o_ref[:, j * C:(j + 1) * C] = w_ref[j]

    return pl.pallas_call(
        body, name="full_from_col_blocks", grid=(R // TR,),
        in_specs=[pl.BlockSpec((n, TR, C), lambda i: (0, i, 0))],
        out_specs=pl.BlockSpec((TR, n * C), lambda i: (i, 0)),
        out_shape=SDS((R, n * C), g.dtype),
        compiler_params=_cp(1, VMEM_LIMIT),
    )(g)


def _wgrad_in(h1, segs, S):
    TS = 1024
    n = len(segs)
    cols = [(c0, a.shape[1]) for a, c0 in segs]
    cb = IN_COLS // N_DEV
    pieces = _w_in_pieces()

    def body(*refs):
        h_ref = refs[0]
        seg_refs = refs[1:1 + n]
        o_ref, acc = refs[1 + n], refs[2 + n]

        @pl.when(pl.program_id(0) == 0)
        def _():
            acc[...] = jnp.zeros_like(acc)

        h = h_ref[...]
        for s_ref, (c0, w) in zip(seg_refs, cols):
            acc[:, c0:c0 + w] += _dot_tn(h, s_ref[...])

        @pl.when(pl.program_id(0) == pl.num_programs(0) - 1)
        def _():
            for j, off, w, src in pieces:
                o_ref[j, :, off:off + w] = acc[:, src:src + w].astype(bf16)

    row = lambda w: pl.BlockSpec((TS, w), lambda i: (i, 0))
    return pl.pallas_call(
        body, name="wgrad_in", grid=(S // TS,),
        in_specs=[row(D_MODEL)] + [row(w) for _, w in cols],
        out_specs=_const_spec((N_DEV, D_MODEL, cb)),
        out_shape=SDS((N_DEV, D_MODEL, cb), bf16),
        scratch_shapes=[pltpu.VMEM((D_MODEL, NCOL), f32)],
        compiler_params=_cp(1, VMEM_LIMIT),
    )(h1, *[a for a, _ in segs])


def _adamw(recv, w, m, v, name):
    R, C = w.shape
    TR = 256 if R % 256 == 0 else R
    c1 = 1.0 / (1.0 - ADAM_B1 ** ADAM_STEP)
    c2 = 1.0 / (1.0 - ADAM_B2 ** ADAM_STEP)

    def body(r_ref, w_ref, m_ref, v_ref, g_out, d_out, m_out, v_out):
        g = r_ref[0].astype(f32)
        for s in range(1, N_DEV):
            g = g + r_ref[s].astype(f32)
        mn = ADAM_B1 * m_ref[...] + (1.0 - ADAM_B1) * g
        vn = ADAM_B2 * v_ref[...] + (1.0 - ADAM_B2) * (g * g)
        g_out[...] = g
        m_out[...] = mn
        v_out[...] = vn
        d_out[...] = -ADAM_LR * ((mn * c1) / (jnp.sqrt(vn * c2) + ADAM_EPS) + ADAM_WD * w_ref[...])

    blk = pl.BlockSpec((TR, C), lambda i: (i, 0))
    return pl.pallas_call(
        body, name=name, grid=(R // TR,),
        in_specs=[pl.BlockSpec((N_DEV, TR, C), lambda i: (0, i, 0)), blk, blk, blk],
        out_specs=[blk, blk, blk, blk],
        out_shape=[SDS((R, C), f32)] * 4,
        compiler_params=_cp(1, VMEM_LIMIT),
    )(recv, w, m, v)


MESH = pl.DeviceIdType.MESH
_ANY = pl.BlockSpec(memory_space=pl.ANY)


def _flip(v, d):
    return 1 - v if d else v


def _all_gather(shards):
    n = len(shards)

    def body(*refs):
        ins = refs[:n]
        outs = refs[n:2 * n]
        send_sems, recv_sems, local_sems = refs[2 * n:]
        x, y, c = lax.axis_index("x"), lax.axis_index("y"), lax.axis_index("c")
        me, sibling = (x, y, c), (x, y, 1 - c)
        chips = [(1 - x, y), (x, 1 - y), (1 - x, 1 - y)]

        def slot(px, py, pc):
            return 4 * px + 2 * py + pc

        def copy(a, k, block, to, src=None):
            dst = outs[a].at[slot(*block)]
            return pltpu.make_async_remote_copy(src_ref=dst if src is None else src, dst_ref=dst,
                                                send_sem=send_sems.at[a, k], recv_sem=recv_sems.at[a, k],
                                                device_id=to, device_id_type=MESH)

        mine, first, passed = [], [], []
        for a in range(n):
            cp = pltpu.make_async_copy(ins[a], outs[a].at[slot(*me)], local_sems.at[a])
            cp.start()
            mine.append(cp)
            fs = [copy(a, 0, me, sibling, src=ins[a])]
            fs += [copy(a, 1 + j, me, (*chip, c), src=ins[a]) for j, chip in enumerate(chips)]
            for cp in fs:
                cp.start()
            first += fs
        for j, chip in enumerate(chips):
            for a in range(n):
                copy(a, 1 + j, (*chip, c), me).wait_recv()
                cp = copy(a, 4 + j, (*chip, c), sibling)
                cp.start()
                passed.append(cp)
        for a in range(n):
            copy(a, 0, sibling, me).wait_recv()
            for j, chip in enumerate(chips):
                copy(a, 4 + j, (*chip, 1 - c), me).wait_recv()
        for cp in first + passed:
            cp.wait_send()
        for cp in mine:
            cp.wait()

    return pl.pallas_call(
        body, name="weight_all_gather",
        in_specs=[_ANY] * n, out_specs=[_ANY] * n,
        out_shape=[SDS((N_DEV,) + s.shape, s.dtype) for s in shards],
        scratch_shapes=[pltpu.SemaphoreType.DMA((n, 7)), pltpu.SemaphoreType.DMA((n, 7)), pltpu.SemaphoreType.DMA((n,))],
        compiler_params=pltpu.CompilerParams(has_side_effects=True),
    )(*shards)


def _grad_exchange(blocked, whole):
    arrs = list(blocked) + list(whole)
    n, nb = len(arrs), len(blocked)
    rel = [(dx, dy, dc) for dx in (0, 1) for dy in (0, 1) for dc in (0, 1) if dx or dy or dc]

    def body(*refs):
        ins = refs[:n]
        outs = refs[n:2 * n]
        send_sems, recv_sems, local_sems = refs[2 * n:]
        x, y, c = lax.axis_index("x"), lax.axis_index("y"), lax.axis_index("c")
        me = 4 * x + 2 * y + c
        sends, locs = [], []
        for a in range(n):
            cp = pltpu.make_async_copy(ins[a].at[me] if a < nb else ins[a], outs[a].at[me], local_sems.at[a])
            cp.start()
            locs.append(cp)
            for k, (dx, dy, dc) in enumerate(rel):
                peer = (_flip(x, dx), _flip(y, dy), _flip(c, dc))
                pidx = 4 * peer[0] + 2 * peer[1] + peer[2]
                cp = pltpu.make_async_remote_copy(src_ref=ins[a].at[pidx] if a < nb else ins[a], dst_ref=outs[a].at[me],
                                                  send_sem=send_sems.at[a, k], recv_sem=recv_sems.at[a, k],
                                                  device_id=peer, device_id_type=MESH)
                cp.start()
                sends.append(cp)
        for a in range(n):
            for k, (dx, dy, dc) in enumerate(rel):
                peer = (_flip(x, dx), _flip(y, dy), _flip(c, dc))
                pidx = 4 * peer[0] + 2 * peer[1] + peer[2]
                pltpu.make_async_remote_copy(src_ref=outs[a].at[pidx], dst_ref=outs[a].at[pidx],
                                             send_sem=send_sems.at[a, k], recv_sem=recv_sems.at[a, k],
                                             device_id=peer, device_id_type=MESH).wait_recv()
        for cp in sends:
            cp.wait_send()
        for cp in locs:
            cp.wait()

    shapes = [SDS(a.shape, a.dtype) for a in blocked] + [SDS((N_DEV,) + a.shape, a.dtype) for a in whole]
    return pl.pallas_call(
        body, name="grad_exchange",
        in_specs=[_ANY] * n, out_specs=[_ANY] * n, out_shape=shapes,
        scratch_shapes=[pltpu.SemaphoreType.DMA((n, 7)), pltpu.SemaphoreType.DMA((n, 7)), pltpu.SemaphoreType.DMA((n,))],
        compiler_params=pltpu.CompilerParams(has_side_effects=True),
    )(*arrs)


_HBM = pl.BlockSpec(memory_space=pltpu.HBM)
_SEM = pl.BlockSpec(memory_space=pltpu.SEMAPHORE)
_REL = [(dx, dy, dc) for dx in (0, 1) for dy in (0, 1) for dc in (0, 1) if dx or dy or dc]


N_PEER = len(_REL)
_EFFECT = pltpu.SideEffectType.DATAFLOW_SIDE_EFFECTING


def _peer_copies(srcs, lands, send_sems, recv_sems, blocked, as_receiver):
    x, y, c = lax.axis_index("x"), lax.axis_index("y"), lax.axis_index("c")
    me = 4 * x + 2 * y + c
    cps = []
    for a in range(len(srcs)):
        for k, (dx, dy, dc) in enumerate(_REL):
            peer = (_flip(x, dx), _flip(y, dy), _flip(c, dc))
            pidx = 4 * peer[0] + 2 * peer[1] + peer[2]
            cps.append(pltpu.make_async_remote_copy(
                src_ref=srcs[a].at[pidx] if blocked else srcs[a], dst_ref=lands[a].at[pidx if as_receiver else me],
                send_sem=send_sems[a * N_PEER + k], recv_sem=recv_sems[a * N_PEER + k],
                device_id=peer, device_id_type=MESH))
    return cps


def _exchange_start(srcs, after, blocked, name):
    n = len(srcs)
    ns = n * N_PEER
    lands = [lax.empty(s.shape if blocked else (N_DEV,) + s.shape, s.dtype) for s in srcs]

    def body(*refs):
        ins, lnd = refs[:n], refs[n:2 * n]
        outs = refs[2 * n + 1:]
        for cp in _peer_copies(ins, lnd, outs[:ns], outs[ns:2 * ns], blocked, False):
            cp.start()
        outs[-1][...] = jnp.zeros_like(outs[-1])

    res = pl.pallas_call(
        body, name=name,
        in_specs=[_HBM] * (2 * n) + [_ANY],
        out_specs=[_SEM] * (2 * ns) + [_HBM] * (2 * n) + [pl.BlockSpec(memory_space=pltpu.VMEM)],
        out_shape=[pltpu.SemaphoreType.DMA(())] * (2 * ns) + [pltpu.HBM(s.shape, s.dtype) for s in srcs]
                  + [pltpu.HBM(l.shape, l.dtype) for l in lands] + [SDS((8, LANE), f32)],
        input_output_aliases={i: 2 * ns + i for i in range(2 * n)},
        compiler_params=pltpu.CompilerParams(has_side_effects=_EFFECT),
    )(*[pltpu.with_memory_space_constraint(s, pltpu.HBM) for s in srcs],
      *[pltpu.with_memory_space_constraint(l, pltpu.HBM) for l in lands], after)
    return list(res[:2 * ns]), list(res[2 * ns:2 * ns + n]), list(res[2 * ns + n:2 * ns + 2 * n]), res[-1]


def _exchange_wait(sems, srcs, lands, after, blocked, name):
    n = len(srcs)
    ns = n * N_PEER

    def body(*refs):
        ins, lnd = refs[:n], refs[n:2 * n]
        sem_refs = refs[2 * n:2 * n + 2 * ns]
        for cp in _peer_copies(ins, lnd, sem_refs[:ns], sem_refs[ns:], blocked, True):
            cp.wait_send()
            cp.wait_recv()

    res = pl.pallas_call(
        body, name=name,
        in_specs=[_HBM] * (2 * n) + [_SEM] * (2 * ns) + [_ANY],
        out_specs=[_HBM] * (2 * n),
        out_shape=[pltpu.HBM(s.shape, s.dtype) for s in srcs] + [pltpu.HBM(l.shape, l.dtype) for l in lands],
        input_output_aliases={i: i for i in range(2 * n)},
        compiler_params=pltpu.CompilerParams(has_side_effects=_EFFECT),
    )(*srcs, *lands, *sems, after)
    return list(res[:n]), list(res[n:])


def _local_step(x, tgt, wcat, convw, late_weights, early_grads, last_grads, token, a_log, dt_bias, onorm_g, rel_bias,
                g_mix_pre, g_mix_post, g_ffn_pre, g_ffn_post):
    S = x.shape[0]
    bk_np = _bucket_tables()
    bk = jnp.asarray(bk_np)
    bt = _bias_tables(rel_bias, bk)
    proj, h1 = _proj_fwd(x, g_mix_pre, wcat, token, S)
    nu = S // CHUNK * GDN_HEADS
    qkv_u = _gdn_prep(proj, convw, S).reshape(3, nu, CHUNK, GDN_HD)
    intra, t_inv = _gdn_intra_fwd(qkv_u, proj, a_log, dt_bias, S)
    oa, states = _gdn_scan_fwd(intra, proj, onorm_g, S)
    ob, lse0, lse1 = _swa_fwd(proj, bt, S)
    wout, ffn_weights = late_weights(ob)
    mix, x1 = _mix_fwd(oa, ob, wout, x, g_mix_post, S)
    wgate, wup, wdown = ffn_weights(x1)
    dx1, h2, act, dgate_f, dup_f, df, loss, d_gfpre, d_gfpost = _ffn(x1, tgt, g_ffn_pre, g_ffn_post, wgate, wup, wdown, S)
    g_gate = _wgrad(h2, dgate_f, S, "wgrad_gate", col_blocks=True)
    g_up = _wgrad(h2, dup_f, S, "wgrad_up", col_blocks=True)
    g_down = _wgrad(act, df, S, "wgrad_down").reshape(N_DEV, D_FF // N_DEV, D_MODEL)
    dmix, d_oab, d_gmpost = _mix_bwd(dx1, mix, g_mix_post, wout, S)
    g_out = jnp.concatenate([_wgrad(oa, dmix, S, "wgrad_out_a"), _wgrad(ob, dmix, S, "wgrad_out_b")], axis=0)
    token = early_grads(g_out.reshape(N_DEV, D_MODEL // N_DEV, D_MODEL), g_gate, g_up, g_down)
    dqb, dkb, dvb, dsb = _swa_bwd(proj, bt, ob, lse0, lse1, d_oab, token, S)
    *cots, dgate_a, d_og = _gdn_scan_bwd(intra, states, proj, d_oab, onorm_g, token, S)
    dqkv_u, dpg, d_alog, d_dtb = _gdn_intra_bwd(qkv_u, proj, a_log, dt_bias, t_inv, cots, S)
    dqkv_a, d_conv = _gdn_prep_bwd(proj, convw, dqkv_u.reshape(3, S // CHUNK, GDN_HEADS, CHUNK, GDN_HD), S)
    segs = [(dqkv_a, COL_A), (dgate_a, COL_A + 3 * GDN_W), (dqb, COL_B), (dkb, COL_B + SWA_W), (dvb, COL_B + 2 * SWA_W),
            (dpg, COL_G)]
    token = last_grads(_wgrad_in(h1, segs, S), d_conv)
    grad_x, d_gmpre = _proj_bwd(x, dx1, g_mix_pre, wcat, segs, token, S)
    d_rel = _rel_bias_grad(dsb, bk, bk_np)
    small = dict(a_log=d_alog[:, GDN_HEADS:2 * GDN_HEADS], dt_bias=d_dtb[:, GDN_HEADS:2 * GDN_HEADS], onorm_g=d_og, rel_bias=d_rel,
                 g_mix_pre=d_gmpre, g_mix_post=d_gmpost, g_ffn_pre=d_gfpre, g_ffn_post=d_gfpost)
    return loss, grad_x, small


SMALL = ("a_log", "dt_bias", "onorm_g", "rel_bias", "g_mix_pre", "g_mix_post", "g_ffn_pre", "g_ffn_post")
PACK_ROWS = 8


def _pack_small(d, loss=None):
    rest = jnp.concatenate([d["onorm_g"].reshape(-1), d["a_log"].reshape(-1), d["dt_bias"].reshape(-1),
                            d["rel_bias"].reshape(-1)])
    rest = jnp.concatenate([rest, jnp.zeros((D_MODEL - rest.shape[0],), f32)])
    extra = jnp.zeros((D_MODEL,), f32) if loss is None else jnp.concatenate([loss.reshape(1), jnp.zeros((D_MODEL - 1,), f32)])
    rows = [d["g_mix_pre"].reshape(-1), d["g_mix_post"].reshape(-1), d["g_ffn_pre"].reshape(-1),
            d["g_ffn_post"].reshape(-1), rest, extra]
    return jnp.concatenate([jnp.stack(rows), jnp.zeros((PACK_ROWS - len(rows), D_MODEL), f32)], axis=0)


def _unpack_small(p):
    o = GDN_HD
    return dict(g_mix_pre=p[0:1], g_mix_post=p[1:2], g_ffn_pre=p[2:3], g_ffn_post=p[3:4],
                onorm_g=p[4:5, :o], a_log=p[4:5, o:o + 4], dt_bias=p[4:5, o + 4:o + 8],
                rel_bias=p[4, o + 8:o + 8 + NUM_BUCKETS * SWA_HEADS].reshape(NUM_BUCKETS, SWA_HEADS))


def kernel(x, w_in, conv_w, a_log, dt_bias, onorm_g, rel_bias, w_out, g_mix_pre, g_mix_post, w_gate, w_up, w_down, g_ffn_pre, g_ffn_post, loss_target, m_w_in, m_conv_w, m_a_log, m_dt_bias, m_onorm_g, m_rel_bias, m_w_out, m_g_mix_pre, m_g_mix_post, m_w_gate, m_w_up, m_w_down, m_g_ffn_pre, m_g_ffn_post, v_w_in, v_conv_w, v_a_log, v_dt_bias, v_onorm_g, v_rel_bias, v_w_out, v_g_mix_pre, v_g_mix_post, v_w_gate, v_w_up, v_w_down, v_g_ffn_pre, v_g_ffn_post):
    big = ("w_in", "conv_w", "w_out", "w_gate", "w_up", "w_down")
    w_sh = dict(w_in=w_in[0], conv_w=conv_w[0], w_out=w_out[0], w_gate=w_gate[0], w_up=w_up[0], w_down=w_down[0])
    m_sh = dict(w_in=m_w_in[0], conv_w=m_conv_w[0], w_out=m_w_out[0], w_gate=m_w_gate[0], w_up=m_w_up[0], w_down=m_w_down[0])
    v_sh = dict(w_in=v_w_in[0], conv_w=v_conv_w[0], w_out=v_w_out[0], w_gate=v_w_gate[0], w_up=v_w_up[0], w_down=v_w_down[0])
    w_small = dict(a_log=a_log, dt_bias=dt_bias, onorm_g=onorm_g, rel_bias=rel_bias, g_mix_pre=g_mix_pre,
                   g_mix_post=g_mix_post, g_ffn_pre=g_ffn_pre, g_ffn_post=g_ffn_post)
    m_small = dict(a_log=m_a_log, dt_bias=m_dt_bias, onorm_g=m_onorm_g, rel_bias=m_rel_bias, g_mix_pre=m_g_mix_pre,
                   g_mix_post=m_g_mix_post, g_ffn_pre=m_g_ffn_pre, g_ffn_post=m_g_ffn_post)
    v_small = dict(a_log=v_a_log, dt_bias=v_dt_bias, onorm_g=v_onorm_g, rel_bias=v_rel_bias, g_mix_pre=v_g_mix_pre,
                   g_mix_post=v_g_mix_post, g_ffn_pre=v_g_ffn_pre, g_ffn_post=v_g_ffn_post)

    me = 4 * lax.axis_index("x") + 2 * lax.axis_index("y") + lax.axis_index("c")
    own = lambda full, part: lax.dynamic_update_index_in_dim(full, part, me, 0)
    cols = lambda g: g.reshape(g.shape[0], N_DEV, g.shape[1] // N_DEV).transpose(1, 0, 2)
    late = ("w_out", "w_gate", "w_up", "w_down")

    late_src = [w_sh[k].astype(bf16) for k in late]
    g_in, g_conv = _all_gather([w_sh["w_in"].astype(bf16), w_sh["conv_w"]])
    g_sems, g_src, g_land, g_token = _exchange_start(late_src, g_conv, False, "late_weights_start")
    wcat = _wcat_from_blocks(g_in)
    convw = g_conv.transpose(1, 0, 2).reshape(4, 3 * GDN_W)

    def late_weights(after):
        pick = lambda idx: [g_sems[half * len(late) * N_PEER + a * N_PEER + k] for half in (0, 1) for a in idx for k in range(N_PEER)]
        (s_out,), (l_out,) = _exchange_wait(pick([0]), g_src[:1], g_land[:1], after, False, "w_out_wait")

        def ffn_weights(after2):
            srcs, lands = _exchange_wait(pick([1, 2, 3]), g_src[1:], g_land[1:], after2, False, "ffn_weights_wait")
            g_gate, g_up, g_down = [own(l, s) for l, s in zip(lands, srcs)]
            return _full_from_col_blocks(g_gate), _full_from_col_blocks(g_up), g_down.reshape(D_FF, D_MODEL)

        return own(l_out, s_out).reshape(D_MODEL, D_MODEL), ffn_weights

    early, last = {}, {}

    def early_grads(*blocks):
        early["sems"], early["src"], early["land"], token = _exchange_start(list(blocks), blocks[0], True, "late_grads_start")
        return token

    def last_grads(gw_in, gw_conv):
        src = [gw_in, cols(gw_conv)]
        last["sems"], last["src"], last["land"], token = _exchange_start(src, gw_in, True, "last_grads_start")
        return token

    loss_p, grad_x, gsmall = _local_step(
        x[0], loss_target[0], wcat, convw, late_weights, early_grads, last_grads, g_token,
        a_log, dt_bias, onorm_g, rel_bias, g_mix_pre, g_mix_post, g_ffn_pre, g_ffn_post)

    (r_small,) = _grad_exchange([], [_pack_small(gsmall, loss_p[0, 0])])
    recv = {}
    for names, ex, after, name in ((late, early, grad_x, "late_grads_wait"), (("w_in", "conv_w"), last, r_small, "last_grads_wait")):
        srcs, lands = _exchange_wait(ex["sems"], ex["src"], ex["land"], after, True, name)
        for k, l, s in zip(names, lands, srcs):
            recv[k] = own(l, lax.dynamic_index_in_dim(s, me, 0, keepdims=False))

    outs = {}
    for k in big:
        outs[k] = _adamw(recv[k], w_sh[k], m_sh[k], v_sh[k], "adamw_" + k)
    sm = _adamw(r_small, _pack_small(w_small), _pack_small(m_small), _pack_small(v_small), "adamw_small")
    loss = sm[0][5, 0]
    sm = [_unpack_small(t) for t in sm]
    for k in SMALL:
        outs[k] = tuple(t[k].reshape(w_small[k].shape) for t in sm)

    order = ("w_in", "conv_w", "a_log", "dt_bias", "onorm_g", "rel_bias", "w_out", "g_mix_pre", "g_mix_post", "w_gate",
             "w_up", "w_down", "g_ffn_pre", "g_ffn_post")
    lead = lambda k, t: t[None] if k in big else t
    res = [loss, grad_x[None]]
    for i in range(4):
        res += [lead(k, outs[k][i]) for k in order]
    return tuple(res)
```

```python
import functools
import math

import numpy as np
import jax
import jax.numpy as jnp
from jax import lax
from jax.experimental import pallas as pl
from jax.experimental.pallas import tpu as pltpu

f32 = jnp.float32
bf16 = jnp.bfloat16
SDS = jax.ShapeDtypeStruct

D_MODEL = 1024
GDN_HEADS = 4
GDN_HD = 128
GDN_W = 512
CHUNK = 64
SWA_HEADS = 8
SWA_HD = 64
SWA_W = 512
D_FF = 2816
IN_COLS = 3592
PATTERNS = ((128, 1), (512, 4), (2048, 16))
SWA_BLK = 128
NUM_BUCKETS = 32
MAX_DISTANCE = 2048
RMS_EPS = 1e-6
NEG = -1e30
N_DEV = 8

COL_A = 0
COL_B = 2048
COL_G = 3584
NCOL = 3712
LANE = 128

ADAM_LR, ADAM_B1, ADAM_B2, ADAM_EPS, ADAM_WD, ADAM_STEP = 0.001, 0.9, 0.999, 1e-08, 0.01, 10

VMEM_LIMIT = 56 * 1024 * 1024

HI = lax.Precision.HIGHEST
HIGH = lax.Precision.HIGH


def _cp(n_grid=0, vmem=None):
    kw = {}
    if n_grid:
        kw["dimension_semantics"] = ("arbitrary",) * n_grid
    if vmem:
        kw["vmem_limit_bytes"] = vmem
    return pltpu.CompilerParams(**kw)


def _dot(a, b):
    return jnp.dot(a, b, preferred_element_type=f32)


def _dot_nt(a, b):
    return lax.dot_general(a, b, (((1,), (1,)), ((), ())), preferred_element_type=f32)


def _dot_tn(a, b):
    return lax.dot_general(a, b, (((0,), (0,)), ((), ())), preferred_element_type=f32)


def _dot_hi(a, b):
    return jnp.dot(a, b, precision=HI, preferred_element_type=f32)


def _sigmoid(x):
    return 0.5 * jnp.tanh(0.5 * x) + 0.5


def _softplus(x):
    return jnp.maximum(x, 0.0) + jnp.log(1.0 + jnp.exp(-jnp.abs(x)))


def _const_spec(shape):
    nd = len(shape)
    return pl.BlockSpec(shape, lambda *_: (0,) * nd)


def _resident_spec(shape):
    nd = len(shape)
    return pl.BlockSpec(shape, lambda *_: (0,) * nd, pipeline_mode=pl.Buffered(1))


def _t5_bucket_np(dist):
    max_exact = NUM_BUCKETS // 2
    d = np.maximum(dist, 1).astype(np.float32)
    log_b = max_exact + (np.log(d / np.float32(max_exact)) / np.float32(math.log(MAX_DISTANCE / max_exact))
                         * np.float32(NUM_BUCKETS - max_exact)).astype(np.int32)
    return np.where(dist < max_exact, dist, np.minimum(log_b, NUM_BUCKETS - 1)).astype(np.int32)


def _bucket_tables():
    w = SWA_BLK
    qi = np.arange(w)[:, None]
    kj = np.arange(w)[None, :]
    rel = np.where(kj <= qi, qi - kj, qi + w - kj)
    out = np.zeros((len(PATTERNS), w, w), np.int32)
    for p, (_, dil) in enumerate(PATTERNS):
        steps = _t5_bucket_np(np.arange(w + 1) * dil)
        assert steps[w] == steps[w - 1]
        out[p] = steps[rel]
    return out


def _bias_tables(rel_bias, bk):
    def body(rb_ref, bk_ref, o_ref):
        b_idx = bk_ref[0]
        for h in range(SWA_HEADS):
            def lp(b, acc):
                return jnp.where(b_idx == b, rb_ref[b, h], acc)
            o_ref[0, h] = lax.fori_loop(0, NUM_BUCKETS, lp, jnp.zeros((SWA_BLK, SWA_BLK), f32))

    return pl.pallas_call(
        body, name="bias_tables", grid=(3,),
        in_specs=[pl.BlockSpec(memory_space=pltpu.SMEM), pl.BlockSpec((1, SWA_BLK, SWA_BLK), lambda p: (p, 0, 0))],
        out_specs=pl.BlockSpec((1, SWA_HEADS, SWA_BLK, SWA_BLK), lambda p: (p, 0, 0, 0)),
        out_shape=SDS((3, SWA_HEADS, SWA_BLK, SWA_BLK), f32),
        compiler_params=_cp(1),
    )(rel_bias, bk)


def _rel_bias_grad(dsb, bk, bk_np):
    present = [sorted(set(int(v) for v in np.unique(bk_np[p]))) for p in range(3)]

    def body(ds_ref, bk_ref, o_ref):
        row = lax.broadcasted_iota(jnp.int32, (NUM_BUCKETS, LANE), 0)
        col = lax.broadcasted_iota(jnp.int32, (NUM_BUCKETS, SWA_HEADS), 1)
        out = jnp.zeros((NUM_BUCKETS, SWA_HEADS), f32)
        for hp in range(4):
            for hh in range(2):
                acc = jnp.zeros((NUM_BUCKETS, LANE), f32)
                for p in range(3):
                    tile = ds_ref[hp, p, hh]
                    b_idx = bk_ref[p]
                    for b in present[p]:
                        part = jnp.sum(jnp.where(b_idx == b, tile, 0.0), axis=0, keepdims=True)
                        acc = acc + jnp.where(row == b, part, 0.0)
                tot = jnp.sum(acc, axis=1, keepdims=True)
                out = out + jnp.where(col == 2 * hp + hh, tot, 0.0)
        o_ref[...] = out

    return pl.pallas_call(body, name="rel_bias_grad", out_shape=SDS((NUM_BUCKETS, SWA_HEADS), f32),
                          compiler_params=_cp(0, 32 * 1024 * 1024))(dsb, bk)


def _proj_fwd(x, g_pre, wcat, after, S):
    TS = 256

    def body(x_ref, g_ref, w_ref, after_ref, o_ref, h_ref):
        xv = x_ref[...]
        r = lax.rsqrt(jnp.mean(xv * xv, axis=-1, keepdims=True) + RMS_EPS)
        h = (xv * r * g_ref[...]).astype(bf16)
        h_ref[...] = h
        o_ref[...] = _dot(h, w_ref[...])

    return pl.pallas_call(
        body, name="proj_fwd", grid=(S // TS,),
        in_specs=[pl.BlockSpec((TS, D_MODEL), lambda i: (i, 0)), _const_spec((1, D_MODEL)),
                  _resident_spec((D_MODEL, NCOL)), _ANY],
        out_specs=[pl.BlockSpec((TS, NCOL), lambda i: (i, 0)), pl.BlockSpec((TS, D_MODEL), lambda i: (i, 0))],
        out_shape=[SDS((S, NCOL), f32), SDS((S, D_MODEL), bf16)],
        compiler_params=_cp(1, VMEM_LIMIT),
    )(x, g_pre, wcat, after)


CONV_RT = 256
HALO = 8


CONV_NC = CONV_RT // CHUNK


def _gdn_prep(proj, conv_w, S):
    def body(p_ref, cw_ref, o_ref, xs_ref):
        t = pl.program_id(0)
        xs_ref[pl.ds(0, HALO), :] = jnp.zeros((HALO, LANE), f32)
        xs_ref[pl.ds(HALO, S), :] = p_ref[...]
        w = cw_ref[...]
        is_qk = t < 2
        scale = jnp.where(t == 0, GDN_HD ** -0.5, 1.0).astype(f32)

        def lp(c, carry):
            st = pl.multiple_of(c * CONV_RT, CONV_RT)
            pre = xs_ref[pl.ds(st + HALO - 3, CONV_RT), :] * w[0:1, :]
            for i in range(1, 4):
                pre = pre + xs_ref[pl.ds(st + HALO - 3 + i, CONV_RT), :] * w[i:i + 1, :]
            s = pre * _sigmoid(pre)
            nrm = s * lax.rsqrt(jnp.sum(s * s, axis=-1, keepdims=True) + 1e-6) * scale
            out = jnp.where(is_qk, nrm, s)
            for i in range(CONV_NC):
                o_ref[0, c * CONV_NC + i, 0] = out[i * CHUNK:(i + 1) * CHUNK]
            return carry

        lax.fori_loop(0, S // CONV_RT, lp, 0)

    return pl.pallas_call(
        body, name="gdn_prep", grid=(3, GDN_HEADS),
        in_specs=[pl.BlockSpec((S, LANE), lambda t, h: (0, t * GDN_HEADS + h)),
                  pl.BlockSpec((4, LANE), lambda t, h: (0, t * GDN_HEADS + h))],
        out_specs=pl.BlockSpec((1, S // CHUNK, 1, CHUNK, GDN_HD), lambda t, h: (t, 0, h, 0, 0)),
        out_shape=SDS((3, S // CHUNK, GDN_HEADS, CHUNK, GDN_HD), f32),
        scratch_shapes=[pltpu.VMEM((S + HALO, LANE), f32)],
        compiler_params=_cp(2, VMEM_LIMIT),
    )(proj, conv_w)


def _gdn_prep_bwd(proj, conv_w, dqkv, S):
    def body(p_ref, cw_ref, d_ref, dx_ref, dw_ref, xs_ref, dp_ref):
        t = pl.program_id(0)
        xs_ref[pl.ds(0, HALO), :] = jnp.zeros((HALO, LANE), f32)
        xs_ref[pl.ds(HALO, S), :] = p_ref[...]
        dp_ref[pl.ds(S, HALO), :] = jnp.zeros((HALO, LANE), f32)
        w = cw_ref[...]
        is_qk = t < 2
        scale = jnp.where(t == 0, GDN_HD ** -0.5, 1.0).astype(f32)

        def lp1(c, dw):
            st = pl.multiple_of(c * CONV_RT, CONV_RT)
            taps = [xs_ref[pl.ds(st + HALO - 3 + i, CONV_RT), :] for i in range(4)]
            pre = taps[0] * w[0:1, :]
            for i in range(1, 4):
                pre = pre + taps[i] * w[i:i + 1, :]
            sg = _sigmoid(pre)
            s = pre * sg
            d_out = jnp.concatenate([d_ref[0, c * CONV_NC + i, 0] for i in range(CONV_NC)], axis=0)
            rn = lax.rsqrt(jnp.sum(s * s, axis=-1, keepdims=True) + 1e-6)
            n = s * rn
            dn = d_out * scale
            ds_qk = rn * (dn - n * jnp.sum(dn * n, axis=-1, keepdims=True))
            ds = jnp.where(is_qk, ds_qk, d_out)
            dpre = ds * (sg * (1.0 + pre * (1.0 - sg)))
            dp_ref[pl.ds(st, CONV_RT), :] = dpre
            return tuple(dw[i] + jnp.sum(dpre * taps[i], axis=0, keepdims=True) for i in range(4))

        z = jnp.zeros((1, LANE), f32)
        dw = lax.fori_loop(0, S // CONV_RT, lp1, (z, z, z, z))
        for i in range(4):
            dw_ref[pl.ds(i, 1), :] = dw[i]

        def lp2(c, carry):
            st = pl.multiple_of(c * CONV_RT, CONV_RT)
            dx = dp_ref[pl.ds(st, CONV_RT), :] * w[3:4, :]
            for i in range(3):
                dx = dx + dp_ref[pl.ds(st + 3 - i, CONV_RT), :] * w[i:i + 1, :]
            dx_ref[pl.ds(st, CONV_RT), :] = dx.astype(bf16)
            return carry

        lax.fori_loop(0, S // CONV_RT, lp2, 0)

    col = lambda rows: pl.BlockSpec((rows, LANE), lambda t, h: (0, t * GDN_HEADS + h))
    return pl.pallas_call(
        body, name="gdn_prep_bwd", grid=(3, GDN_HEADS),
        in_specs=[col(S), col(4), pl.BlockSpec((1, S // CHUNK, 1, CHUNK, GDN_HD), lambda t, h: (t, 0, h, 0, 0))],
        out_specs=[col(S), col(4)],
        out_shape=[SDS((S, 3 * GDN_W), bf16), SDS((4, 3 * GDN_W), f32)],
        scratch_shapes=[pltpu.VMEM((S + HALO, LANE), f32), pltpu.VMEM((S + HALO, LANE), f32)],
        compiler_params=_cp(2, VMEM_LIMIT),
    )(proj, conv_w, dqkv)


def _bdot(a, b, prec=None):
    return lax.dot_general(a, b, (((2,), (1,)), ((0,), (0,))), precision=prec, preferred_element_type=f32)


def _bdot_nt(a, b, prec=None):
    return lax.dot_general(a, b, (((2,), (2,)), ((0,), (0,))), precision=prec, preferred_element_type=f32)


def _bdot_tn(a, b, prec=None):
    return lax.dot_general(a, b, (((1,), (1,)), ((0,), (0,))), precision=prec, preferred_element_type=f32)


@jax.custom_vjp
def _tri_inv_saved(a, t):
    return t


def _tri_inv_saved_fwd(a, t):
    return t, t


def _tri_inv_saved_bwd(t, dt):
    return -_bdot_tn(t, _bdot_nt(dt, t, HIGH), HIGH), jnp.zeros_like(t)


_tri_inv_saved.defvjp(_tri_inv_saved_fwd, _tri_inv_saved_bwd)


def _gdn_intra(q, k, v, beta, g, t_saved=None):
    nb = q.shape[0]
    c = CHUNK
    ii = lax.broadcasted_iota(jnp.int32, (c, c), 0)
    jj = lax.broadcasted_iota(jnp.int32, (c, c), 1)
    eye = ii == jj
    tril = ii >= jj
    strict = ii > jj
    ones = jnp.ones((nb, c, c), f32)
    eye_f = eye.astype(f32)

    g_row = _bdot(ones, jnp.where(eye, g, 0.0), HI)
    gc = jnp.sum(jnp.where(tril, g_row, 0.0), axis=2, keepdims=True)
    gc_row = _bdot(ones, jnp.where(eye, gc, 0.0), HI)
    decay = jnp.where(tril, jnp.exp(jnp.where(tril, gc - gc_row, 0.0)), 0.0)
    last = lax.broadcasted_iota(jnp.int32, (c, 1), 0) == c - 1
    gc_last = jnp.sum(jnp.where(last, gc, 0.0), axis=1, keepdims=True)
    e_gc = jnp.exp(gc)

    kb = k * beta
    k16 = k.astype(bf16)
    a = jnp.where(strict, _bdot_nt(kb.astype(bf16), k16) * decay, 0.0)
    if t_saved is None:
        xp = -a
        t_inv = eye_f + xp
        for _ in range(5):
            xp = _bdot(xp, xp, HIGH)
            t_inv = _bdot(t_inv, eye_f + xp, HIGH)
    else:
        t_inv = _tri_inv_saved(a, t_saved)
    t16 = t_inv.astype(bf16)
    u = _bdot(t16, (v * beta).astype(bf16))
    w = _bdot(t16, (kb * e_gc).astype(bf16))
    attn = jnp.where(tril, _bdot_nt(q.astype(bf16), k16) * decay, 0.0)
    gam = jnp.broadcast_to(jnp.exp(gc_last), (nb, 1, GDN_HD))
    return u, w, attn, q * e_gc, k * jnp.exp(gc_last - gc), gam, t_inv


GDN_TB = 256
GDN_NC = GDN_TB // CHUNK
GDN_NU = GDN_NC * GDN_HEADS


def _gdn_gates(pg_ref, al_ref, db_ref):
    lane1 = lax.broadcasted_iota(jnp.int32, (1, LANE), 1)
    a_lane = jnp.zeros((1, LANE), f32)
    b_lane = jnp.zeros((1, LANE), f32)
    for h in range(GDN_HEADS):
        a_lane = jnp.where(lane1 == GDN_HEADS + h, al_ref[0, h], a_lane)
        b_lane = jnp.where(lane1 == GDN_HEADS + h, db_ref[0, h], b_lane)
    pg = pg_ref[...]
    z = pg + b_lane
    return _sigmoid(pg), -jnp.exp(a_lane) * _softplus(z), z, a_lane


def _gdn_unit_inputs(qkv_ref, beta_all, g_all):
    units = [(cl, h) for cl in range(GDN_NC) for h in range(GDN_HEADS)]
    beta = jnp.stack([beta_all[cl * CHUNK:(cl + 1) * CHUNK, h:h + 1] for cl, h in units])
    g = jnp.stack([g_all[cl * CHUNK:(cl + 1) * CHUNK, GDN_HEADS + h:GDN_HEADS + h + 1] for cl, h in units])
    return qkv_ref[0], qkv_ref[1], qkv_ref[2], beta, g


def _unit_spec(*tail):
    nd = len(tail)
    return pl.BlockSpec((GDN_NU,) + tail, lambda i: (i,) + (0,) * nd)


def _gdn_intra_shapes(S):
    nu = S // CHUNK * GDN_HEADS
    row = SDS((nu, CHUNK, GDN_HD), f32)
    return [row, row, SDS((nu, CHUNK, CHUNK), f32), row, row, SDS((nu, 1, GDN_HD), f32)]


_GDN_INTRA_SPECS = lambda: [_unit_spec(CHUNK, GDN_HD), _unit_spec(CHUNK, GDN_HD), _unit_spec(CHUNK, CHUNK),
                            _unit_spec(CHUNK, GDN_HD), _unit_spec(CHUNK, GDN_HD), _unit_spec(1, GDN_HD)]


def _gdn_intra_fwd(qkv_u, proj, a_log, dt_bias, S):
    def body(qkv_ref, pg_ref, al_ref, db_ref, *outs):
        beta_all, g_all, _, _ = _gdn_gates(pg_ref, al_ref, db_ref)
        res = _gdn_intra(*_gdn_unit_inputs(qkv_ref, beta_all, g_all))
        for o_ref, r in zip(outs, res):
            o_ref[...] = r

    nu = S // CHUNK * GDN_HEADS
    *intra, t_inv = pl.pallas_call(
        body, name="gdn_intra_fwd", grid=(S // GDN_TB,),
        in_specs=[pl.BlockSpec((3, GDN_NU, CHUNK, GDN_HD), lambda i: (0, i, 0, 0)),
                  pl.BlockSpec((GDN_TB, LANE), lambda i: (i, COL_G // LANE)),
                  pl.BlockSpec(memory_space=pltpu.SMEM), pl.BlockSpec(memory_space=pltpu.SMEM)],
        out_specs=_GDN_INTRA_SPECS() + [_unit_spec(CHUNK, CHUNK)],
        out_shape=_gdn_intra_shapes(S) + [SDS((nu, CHUNK, CHUNK), f32)],
        compiler_params=_cp(1, VMEM_LIMIT),
    )(qkv_u, proj, a_log, dt_bias)
    return intra, t_inv


def _gdn_intra_bwd(qkv_u, proj, a_log, dt_bias, t_inv, cots, S):
    def body(qkv_ref, pg_ref, al_ref, db_ref, t_ref, du_ref, dw_ref, da_ref, dqd_ref, dkd_ref, dgm_ref,
             dqkv_ref, dpg_ref, dal_ref, ddb_ref):
        @pl.when(pl.program_id(0) == 0)
        def _():
            dal_ref[...] = jnp.zeros_like(dal_ref)
            ddb_ref[...] = jnp.zeros_like(ddb_ref)

        t_saved = t_ref[...]
        beta_all, g_all, z, a_lane = _gdn_gates(pg_ref, al_ref, db_ref)
        _, vjp = jax.vjp(lambda *a: _gdn_intra(*a, t_saved=t_saved)[:6], *_gdn_unit_inputs(qkv_ref, beta_all, g_all))
        dq, dk, dv, dbeta, dg = vjp((du_ref[...], dw_ref[...], da_ref[...], dqd_ref[...], dkd_ref[...], dgm_ref[...]))
        dqkv_ref[0] = dq
        dqkv_ref[1] = dk
        dqkv_ref[2] = dv
        lane = lax.broadcasted_iota(jnp.int32, (CHUNK, LANE), 1)
        rows = []
        for cl in range(GDN_NC):
            t = jnp.zeros((CHUNK, LANE), f32)
            for h in range(GDN_HEADS):
                b = cl * GDN_HEADS + h
                t = t + jnp.where(lane == h, dbeta[b], 0.0) + jnp.where(lane == GDN_HEADS + h, dg[b], 0.0)
            rows.append(t)
        d_all = jnp.concatenate(rows, axis=0)
        is_beta = lax.broadcasted_iota(jnp.int32, (GDN_TB, LANE), 1) < GDN_HEADS
        dz = d_all * (-jnp.exp(a_lane)) * _sigmoid(z)
        dpg_ref[...] = jnp.where(is_beta, d_all * beta_all * (1.0 - beta_all), dz).astype(bf16)
        dal_ref[...] += jnp.sum(jnp.where(is_beta, 0.0, d_all * g_all), axis=0, keepdims=True)
        ddb_ref[...] += jnp.sum(jnp.where(is_beta, 0.0, dz), axis=0, keepdims=True)

    acc = _const_spec((1, LANE))
    nu = S // CHUNK * GDN_HEADS
    return pl.pallas_call(
        body, name="gdn_intra_bwd", grid=(S // GDN_TB,),
        in_specs=[pl.BlockSpec((3, GDN_NU, CHUNK, GDN_HD), lambda i: (0, i, 0, 0)),
                  pl.BlockSpec((GDN_TB, LANE), lambda i: (i, COL_G // LANE)),
                  pl.BlockSpec(memory_space=pltpu.SMEM), pl.BlockSpec(memory_space=pltpu.SMEM),
                  _unit_spec(CHUNK, CHUNK)] + _GDN_INTRA_SPECS(),
        out_specs=[pl.BlockSpec((3, GDN_NU, CHUNK, GDN_HD), lambda i: (0, i, 0, 0)),
                   pl.BlockSpec((GDN_TB, LANE), lambda i: (i, 0)), acc, acc],
        out_shape=[SDS((3, nu, CHUNK, GDN_HD), f32), SDS((S, LANE), bf16), SDS((1, LANE), f32), SDS((1, LANE), f32)],
        compiler_params=_cp(1, VMEM_LIMIT),
    )(qkv_u, proj, a_log, dt_bias, t_inv, *cots)


def _gdn_scan_fwd(intra, proj, onorm_g, S):
    def body(u_ref, w_ref, at_ref, qd_ref, kd_ref, gm_ref, gate_ref, og_ref, out_ref, st_ref, s_scr):
        @pl.when(pl.program_id(0) == 0)
        def _():
            s_scr[...] = jnp.zeros_like(s_scr)

        og = og_ref[...]
        s = s_scr[...]
        chain = []
        for cl in range(GDN_NC):
            us = slice(cl * GDN_HEADS, (cl + 1) * GDN_HEADS)
            st_ref[us] = s
            s16 = s.astype(bf16)
            vn16 = (u_ref[us] - _bdot(w_ref[us].astype(bf16), s16)).astype(bf16)
            chain.append((us, s16, vn16))
            s = s * gm_ref[us] + _bdot_tn(kd_ref[us].astype(bf16), vn16)
        s_scr[...] = s
        for cl, (us, s16, vn16) in enumerate(chain):
            rows = slice(cl * CHUNK, (cl + 1) * CHUNK)
            o = _bdot(qd_ref[us].astype(bf16), s16) + _bdot(at_ref[us].astype(bf16), vn16)
            for h in range(GDN_HEADS):
                oh = o[h]
                gt = gate_ref[rows, h * GDN_HD:(h + 1) * GDN_HD]
                on = oh * lax.rsqrt(jnp.mean(oh * oh, axis=-1, keepdims=True) + RMS_EPS) * og
                out_ref[rows, h * GDN_HD:(h + 1) * GDN_HD] = on * (gt * _sigmoid(gt))

    nu = S // CHUNK * GDN_HEADS
    return pl.pallas_call(
        body, name="gdn_scan_fwd", grid=(S // GDN_TB,),
        in_specs=_GDN_INTRA_SPECS() + [pl.BlockSpec((GDN_TB, GDN_W), lambda i: (i, 3)), _const_spec((1, GDN_HD))],
        out_specs=[pl.BlockSpec((GDN_TB, GDN_W), lambda i: (i, 0)), _unit_spec(GDN_HD, GDN_HD)],
        out_shape=[SDS((S, GDN_W), f32), SDS((nu, GDN_HD, GDN_HD), f32)],
        scratch_shapes=[pltpu.VMEM((GDN_HEADS, GDN_HD, GDN_HD), f32)],
        compiler_params=_cp(1, VMEM_LIMIT),
    )(*intra, proj, onorm_g)


def _gdn_scan_bwd(intra, states, proj, d_oab, onorm_g, after, S):
    n_steps = S // GDN_TB

    def body(u_ref, w_ref, at_ref, qd_ref, kd_ref, gm_ref, st_ref, gate_ref, do_ref, og_ref, after_ref,
             du_ref, dw_ref, dat_ref, dqd_ref, dkd_ref, dgm_ref, dgate_ref, dog_ref, ds_scr):
        @pl.when(pl.program_id(0) == 0)
        def _():
            ds_scr[...] = jnp.zeros_like(ds_scr)
            dog_ref[...] = jnp.zeros_like(dog_ref)

        og = og_ref[...]
        ii = lax.broadcasted_iota(jnp.int32, (CHUNK, CHUNK), 0)
        jj = lax.broadcasted_iota(jnp.int32, (CHUNK, CHUNK), 1)
        tril = ii >= jj
        dog = jnp.zeros((1, GDN_HD), f32)
        pre = []
        for cl in range(GDN_NC):
            us = slice(cl * GDN_HEADS, (cl + 1) * GDN_HEADS)
            rows = slice(cl * CHUNK, (cl + 1) * CHUNK)
            s016 = st_ref[us].astype(bf16)
            w16 = w_ref[us].astype(bf16)
            qd16 = qd_ref[us].astype(bf16)
            at16 = at_ref[us].astype(bf16)
            vn16 = (u_ref[us] - _bdot(w16, s016)).astype(bf16)
            o = _bdot(qd16, s016) + _bdot(at16, vn16)
            do_h = []
            for h in range(GDN_HEADS):
                oh = o[h]
                lanes = slice(h * GDN_HD, (h + 1) * GDN_HD)
                gt = gate_ref[rows, lanes]
                d_out = do_ref[rows, lanes]
                r = lax.rsqrt(jnp.mean(oh * oh, axis=-1, keepdims=True) + RMS_EPS)
                n = oh * r
                sg = _sigmoid(gt)
                silu = gt * sg
                dog = dog + jnp.sum(d_out * n * silu, axis=0, keepdims=True)
                dgate_ref[rows, lanes] = (d_out * n * og * (sg * (1.0 + gt * (1.0 - sg)))).astype(bf16)
                dn = d_out * og * silu
                do_h.append(r * (dn - n * jnp.mean(dn * n, axis=-1, keepdims=True)))
            do16 = jnp.stack(do_h).astype(bf16)
            pre.append((us, s016, w16, vn16, do16, _bdot_tn(at16, do16), _bdot_tn(qd16, do16)))
        ds = ds_scr[...]
        chain = [None] * GDN_NC
        for cl in reversed(range(GDN_NC)):
            us, s016, w16, vn16, do16, at_do, qd_do = pre[cl]
            ds16 = ds.astype(bf16)
            dvn = at_do + _bdot(kd_ref[us].astype(bf16), ds16)
            dvn16 = dvn.astype(bf16)
            chain[cl] = (ds, ds16, dvn, dvn16)
            ds = qd_do + ds * gm_ref[us] - _bdot_tn(w16, dvn16)
        ds_scr[...] = ds
        for cl in range(GDN_NC):
            us, s016, w16, vn16, do16, _, _ = pre[cl]
            ds_in, ds16, dvn, dvn16 = chain[cl]
            du_ref[us] = dvn
            dw_ref[us] = -_bdot_nt(dvn16, s016)
            dat_ref[us] = jnp.where(tril, _bdot_nt(do16, vn16), 0.0)
            dqd_ref[us] = _bdot_nt(do16, s016)
            dkd_ref[us] = _bdot_nt(vn16, ds16)
            dgm_ref[us] = jnp.sum(st_ref[us] * ds_in, axis=1, keepdims=True)
        dog_ref[...] += dog

    def unit(*tail):
        nd = len(tail)
        return pl.BlockSpec((GDN_NU,) + tail, lambda i: (n_steps - 1 - i,) + (0,) * nd)

    intra_specs = [unit(CHUNK, GDN_HD), unit(CHUNK, GDN_HD), unit(CHUNK, CHUNK), unit(CHUNK, GDN_HD),
                   unit(CHUNK, GDN_HD), unit(1, GDN_HD)]
    tok = lambda c: pl.BlockSpec((GDN_TB, GDN_W), lambda i: (n_steps - 1 - i, c))
    return pl.pallas_call(
        body, name="gdn_scan_bwd", grid=(n_steps,),
        in_specs=intra_specs + [unit(GDN_HD, GDN_HD), tok(3), tok(0), _const_spec((1, GDN_HD)), _ANY],
        out_specs=intra_specs + [tok(0), _const_spec((1, GDN_HD))],
        out_shape=_gdn_intra_shapes(S) + [SDS((S, GDN_W), bf16), SDS((1, GDN_HD), f32)],
        scratch_shapes=[pltpu.VMEM((GDN_HEADS, GDN_HD, GDN_HD), f32)],
        compiler_params=_cp(1, VMEM_LIMIT),
    )(*intra, states, proj, d_oab, onorm_g, after)


SWA_UNROLL = 4


def _swa_tiles(q_ref, k_ref, v_ref, it, d, nb_log2, S):
    nb = 1 << nb_log2
    r = lax.shift_right_logical(it, nb_log2)
    blk = lax.bitwise_and(it, nb - 1)
    qs = blk * (SWA_BLK * d) + r
    ps = jnp.maximum(blk - 1, 0) * (SWA_BLK * d) + r
    if d > 1:
        rows_c, rows_p = pl.ds(qs, SWA_BLK, stride=d), pl.ds(ps, SWA_BLK, stride=d)
    else:
        rows_c, rows_p = pl.ds(pl.multiple_of(qs, SWA_BLK), SWA_BLK), pl.ds(pl.multiple_of(ps, SWA_BLK), SWA_BLK)
    return rows_c, rows_p, blk > 0


def _swa_prev_modes(nb):
    if nb >= SWA_UNROLL:
        return ["load"] + ["reuse"] * (SWA_UNROLL - 1)
    return ["none" if u % nb == 0 else "reuse" for u in range(SWA_UNROLL)]


def _swa_fwd(proj, bt, S):
    scale = SWA_HD ** -0.5

    def body(q_ref, k_ref, v_ref, bt_ref, o_ref, lse0_ref, lse1_ref, m0_scr, m1_scr, a0_scr, a1_scr):
        lane = lax.broadcasted_iota(jnp.int32, (SWA_BLK, LANE), 1)
        h0 = lane < SWA_HD
        qi = lax.broadcasted_iota(jnp.int32, (SWA_BLK, SWA_BLK), 0)
        kj = lax.broadcasted_iota(jnp.int32, (SWA_BLK, SWA_BLK), 1)
        lower = kj <= qi
        ones16 = jnp.ones((LANE, SWA_BLK), bf16)
        m_scrs = (m0_scr, m1_scr)
        a_scrs = (a0_scr, a1_scr)
        for p, (_, d) in reversed(list(enumerate(PATTERNS))):
            nb_log2 = int(math.log2(S // d // SWA_BLK))
            first = p == len(PATTERNS) - 1

            def lp(i, carry, p=p, d=d, nb_log2=nb_log2, first=first):
                heads = [h0, jnp.logical_not(h0)]
                modes = _swa_prev_modes(1 << nb_log2)
                tiles = []
                kc_f = None
                for u in range(SWA_UNROLL):
                    rows_c, rows_p, has_prev = _swa_tiles(q_ref, k_ref, v_ref, i * SWA_UNROLL + u, d, nb_log2, S)
                    kp_f = {"load": lambda: k_ref[rows_p, :], "reuse": lambda: kc_f, "none": lambda: None}[modes[u]]()
                    has_prev = {"load": has_prev, "reuse": True, "none": False}[modes[u]]
                    q = q_ref[rows_c, :]
                    kc_f = k_ref[rows_c, :]
                    kc = kc_f.astype(bf16)
                    logits = []
                    for mh in heads:
                        q_h = jnp.where(mh, q, 0.0)
                        qh = q_h.astype(bf16)
                        if kp_f is None:
                            logits.append((_dot_nt(qh, kc), None, None))
                        else:
                            logits.append((_dot_nt(qh, kc), _dot_nt(qh, kp_f.astype(bf16)), _dot((q_h * kp_f).astype(bf16), ones16)))
                    tiles.append((rows_c, rows_p, has_prev, logits))
                probs = []
                for rows_c, rows_p, has_prev, logits in tiles:
                    per_head = []
                    for h, (s_c, s_p, far) in enumerate(logits):
                        if has_prev is False:
                            s = jnp.where(lower, s_c * scale + bt_ref[p, h], NEG)
                            s_far = None
                        else:
                            s = jnp.where(lower, s_c, s_p) * scale + bt_ref[p, h]
                            s_far = far * scale + bt_ref[p, h, SWA_BLK - 1:SWA_BLK, 0:1]
                            if has_prev is not True:
                                s = jnp.where(jnp.logical_or(lower, has_prev), s, NEG)
                                s_far = jnp.where(has_prev, s_far, NEG)
                        mn = jnp.max(s, axis=1, keepdims=True)
                        if s_far is not None:
                            mn = jnp.maximum(s_far, mn)
                        alpha = None
                        if not first:
                            mo = m_scrs[h][rows_c, :]
                            mn = jnp.maximum(mo, mn)
                            alpha = jnp.exp(mo - mn)
                        mn = jnp.broadcast_to(mn, (SWA_BLK, LANE))
                        pm = jnp.exp(s - mn)
                        per_head.append((mn, alpha, None if s_far is None else jnp.exp(s_far - mn),
                                         jnp.where(lower, pm, 0.0).astype(bf16),
                                         None if s_far is None else jnp.where(lower, 0.0, pm).astype(bf16)))
                    probs.append(per_head)
                acc_old = [None if first else (a0_scr[t[0], :], a1_scr[t[0], :]) for t in tiles]
                done = []
                vc = None
                for u, ((rows_c, rows_p, _, _), per_head, old) in enumerate(zip(tiles, probs, acc_old)):
                    vp = {"load": lambda: v_ref[rows_p, :], "reuse": lambda: vc, "none": lambda: None}[modes[u]]()
                    vc = v_ref[rows_c, :]
                    acc_new = []
                    for h, (mn, alpha, p_far, pc16, pp16) in enumerate(per_head):
                        pv = _dot(pc16, jnp.where(heads[h], vc, 1.0).astype(bf16))
                        if pp16 is not None:
                            vpa = jnp.where(heads[h], vp, 1.0)
                            pv = pv + _dot(pp16, vpa.astype(bf16)) + p_far * vpa
                        acc_new.append(pv if first else alpha * old[h] + pv)
                    done.append((rows_c, per_head[0][0], per_head[1][0], acc_new[0], acc_new[1]))
                for rows_c, m0_new, m1_new, a0_new, a1_new in done:
                    m0_scr[rows_c, :] = m0_new
                    m1_scr[rows_c, :] = m1_new
                    a0_scr[rows_c, :] = a0_new
                    a1_scr[rows_c, :] = a1_new
                return carry

            lax.fori_loop(0, S // SWA_BLK // SWA_UNROLL, lp, 0)

        def fin(c, carry):
            rows = pl.ds(pl.multiple_of(c * SWA_BLK, SWA_BLK), SWA_BLK)
            a0 = a0_scr[rows, :]
            a1 = a1_scr[rows, :]
            l0 = jnp.where(h0, pltpu.roll(a0, SWA_HD, 1), a0)
            l1 = jnp.where(h0, a1, pltpu.roll(a1, SWA_HD, 1))
            o_ref[rows, :] = jnp.where(h0, a0 / l0, a1 / l1)
            lse0_ref[rows, :] = m0_scr[rows, :] + jnp.log(l0)
            lse1_ref[rows, :] = m1_scr[rows, :] + jnp.log(l1)
            return carry

        lax.fori_loop(0, S // SWA_BLK, fin, 0)

    qb = COL_B // LANE
    col = lambda c: pl.BlockSpec((S, LANE), lambda hp, c=c: (0, c + hp))
    return pl.pallas_call(
        body, name="swa_fwd", grid=(4,),
        in_specs=[col(qb), col(qb + 4), col(qb + 8), pl.BlockSpec((3, 2, SWA_BLK, SWA_BLK), lambda hp: (0, hp, 0, 0))],
        out_specs=[col(0), col(0), col(0)],
        out_shape=[SDS((S, SWA_W), f32)] * 3,
        scratch_shapes=[pltpu.VMEM((S, LANE), f32)] * 4,
        compiler_params=_cp(1, VMEM_LIMIT),
    )(proj, proj, proj, bt)


def _swa_bwd(proj, bt, nd, lse0, lse1, d_oab, after, S):
    scale = SWA_HD ** -0.5

    def body(q_ref, k_ref, v_ref, bt_ref, nd_scr, lse0_ref, lse1_ref, do_ref, after_ref, dq_ref, dk_ref, dv_ref, dsb_ref,
             dq_scr, dk_scr, dv_scr):
        lane = lax.broadcasted_iota(jnp.int32, (SWA_BLK, LANE), 1)
        h0 = lane < SWA_HD
        qi = lax.broadcasted_iota(jnp.int32, (SWA_BLK, SWA_BLK), 0)
        kj = lax.broadcasted_iota(jnp.int32, (SWA_BLK, SWA_BLK), 1)
        lower = kj <= qi
        eye = kj == qi
        rel127 = jnp.logical_or(kj == qi + 1, jnp.logical_and(qi == SWA_BLK - 1, kj == 0))
        ones16 = jnp.ones((LANE, SWA_BLK), bf16)
        lse_refs = (lse0_ref, lse1_ref)
        dk_scr[...] = jnp.zeros((S, LANE), f32)
        dv_scr[...] = jnp.zeros((S, LANE), f32)
        dsb_ref[...] = jnp.zeros_like(dsb_ref)

        for p, (_, d) in reversed(list(enumerate(PATTERNS))):
            nb_log2 = int(math.log2(S // d // SWA_BLK))
            first = p == len(PATTERNS) - 1

            def lp(i, carry, p=p, d=d, nb_log2=nb_log2, first=first):
                heads = [h0, jnp.logical_not(h0)]
                modes = _swa_prev_modes(1 << nb_log2)
                tiles = []
                kc_f = vc_f = None
                for u in range(SWA_UNROLL):
                    rows_c, rows_p, has_prev = _swa_tiles(q_ref, k_ref, v_ref, i * SWA_UNROLL + u, d, nb_log2, S)
                    kp_f = {"load": lambda: k_ref[rows_p, :], "reuse": lambda: kc_f, "none": lambda: None}[modes[u]]()
                    vp_f = {"load": lambda: v_ref[rows_p, :], "reuse": lambda: vc_f, "none": lambda: None}[modes[u]]()
                    has_prev = {"load": has_prev, "reuse": True, "none": False}[modes[u]]
                    q = q_ref[rows_c, :]
                    kc_f = k_ref[rows_c, :]
                    vc_f = v_ref[rows_c, :]
                    kc = kc_f.astype(bf16)
                    kp = None if kp_f is None else kp_f.astype(bf16)
                    do = do_ref[rows_c, :]
                    nd = nd_scr[rows_c, :]
                    per_head = []
                    for mh in heads:
                        q_h = jnp.where(mh, q, 0.0)
                        do_a = jnp.where(mh, do, nd)
                        qh = q_h.astype(bf16)
                        doa = do_a.astype(bf16)
                        doh = jnp.where(mh, do, 0.0).astype(bf16)
                        dd_c = _dot_nt(doa, jnp.where(mh, vc_f, 1.0).astype(bf16))
                        if kp_f is None:
                            per_head.append((qh, doh, _dot_nt(qh, kc), None, None, dd_c, None, None))
                        else:
                            vpa = jnp.where(mh, vp_f, 1.0)
                            per_head.append((qh, doh, _dot_nt(qh, kc), _dot_nt(qh, kp), _dot((q_h * kp_f).astype(bf16), ones16),
                                             dd_c, _dot_nt(doa, vpa.astype(bf16)), _dot((do_a * vpa).astype(bf16), ones16)))
                    tiles.append((rows_c, rows_p, has_prev, kc, kp, per_head))
                grads = []
                for rows_c, rows_p, has_prev, kc, kp, per_head in tiles:
                    out = []
                    for h, (qh, doh, s_c, s_p, far, dd_c, dd_p, dd_far) in enumerate(per_head):
                        lse_h = lse_refs[h][rows_c, :]
                        if has_prev is False:
                            pm = jnp.exp(jnp.where(lower, s_c * scale + bt_ref[p, h], NEG) - lse_h)
                            dsm = pm * dd_c
                            out.append((dsm, dsm.astype(bf16), None, pm.astype(bf16), None))
                            continue
                        s = jnp.where(lower, s_c, s_p) * scale + bt_ref[p, h]
                        s_far = far * scale + bt_ref[p, h, SWA_BLK - 1:SWA_BLK, 0:1]
                        if has_prev is not True:
                            s = jnp.where(jnp.logical_or(lower, has_prev), s, NEG)
                            s_far = jnp.where(has_prev, s_far, NEG)
                        pm = jnp.exp(s - lse_h)
                        p_far = jnp.exp(s_far - lse_h)
                        dsm = pm * jnp.where(lower, dd_c, dd_p)
                        ds_far = p_far * dd_far
                        out.append((dsm + jnp.where(rel127, ds_far, 0.0),
                                    jnp.where(lower, dsm, 0.0).astype(bf16),
                                    jnp.where(lower, jnp.where(eye, ds_far, 0.0), dsm).astype(bf16),
                                    jnp.where(lower, pm, 0.0).astype(bf16),
                                    jnp.where(lower, jnp.where(eye, p_far, 0.0), pm).astype(bf16)))
                    grads.append(out)
                done = []
                add = lambda acc, t: t if acc is None else acc + t
                for (rows_c, rows_p, _, kc, kp, per_head), out in zip(tiles, grads):
                    dq_t = dkc_t = dkp_t = dvc_t = dvp_t = None
                    for h, (_, dsc16, dsp16, pc16, pp16) in enumerate(out):
                        qh, doh = per_head[h][0], per_head[h][1]
                        dq_h = _dot(dsc16, kc)
                        dkc_t = add(dkc_t, _dot_tn(dsc16, qh) * scale)
                        dvc_t = add(dvc_t, _dot_tn(pc16, doh))
                        if dsp16 is not None:
                            dq_h = dq_h + _dot(dsp16, kp)
                            dkp_t = add(dkp_t, _dot_tn(dsp16, qh) * scale)
                            dvp_t = add(dvp_t, _dot_tn(pp16, doh))
                        dq_t = add(dq_t, jnp.where(heads[h], dq_h * scale, 0.0))
                    done.append([rows_c, rows_p, dq_t, dkc_t, dkp_t, dvc_t, dvp_t])
                for u in range(1, SWA_UNROLL):
                    if modes[u] == "reuse":
                        done[u - 1][3] = done[u - 1][3] + done[u][4]
                        done[u - 1][5] = done[u - 1][5] + done[u][6]
                for h in range(2):
                    tot = grads[0][h][0]
                    for g in grads[1:]:
                        tot = tot + g[h][0]
                    dsb_ref[0, p, h] += tot
                for u, (rows_c, rows_p, dq_t, dkc_t, dkp_t, dvc_t, dvp_t) in enumerate(done):
                    dq_scr[rows_c, :] = dq_t if first else dq_scr[rows_c, :] + dq_t
                    dk_scr[rows_c, :] = dk_scr[rows_c, :] + dkc_t
                    dv_scr[rows_c, :] = dv_scr[rows_c, :] + dvc_t
                    if modes[u] == "load":
                        dk_scr[rows_p, :] = dk_scr[rows_p, :] + dkp_t
                        dv_scr[rows_p, :] = dv_scr[rows_p, :] + dvp_t
                return carry

            lax.fori_loop(0, S // SWA_BLK // SWA_UNROLL, lp, 0)
        dq_ref[...] = dq_scr[...].astype(bf16)
        dk_ref[...] = dk_scr[...].astype(bf16)
        dv_ref[...] = dv_scr[...].astype(bf16)

    qb = COL_B // LANE
    col = lambda c: pl.BlockSpec((S, LANE), lambda hp, c=c: (0, c + hp))
    return pl.pallas_call(
        body, name="swa_bwd", grid=(4,),
        in_specs=[col(qb), col(qb + 4), col(qb + 8),
                  pl.BlockSpec((3, 2, SWA_BLK, SWA_BLK), lambda hp: (0, hp, 0, 0)),
                  col(0), col(0), col(0), col(4), _ANY],
        out_specs=[col(0), col(0), col(0),
                   pl.BlockSpec((1, 3, 2, SWA_BLK, SWA_BLK), lambda hp: (hp, 0, 0, 0, 0))],
        out_shape=[SDS((S, SWA_W), bf16)] * 3 + [SDS((4, 3, 2, SWA_BLK, SWA_BLK), f32)],
        scratch_shapes=[pltpu.VMEM((S, LANE), f32)] * 3,
        compiler_params=_cp(1, VMEM_LIMIT),
    )(proj, proj, proj, bt, nd, lse0, lse1, d_oab, after)


def _mix_fwd(oa, ob, w_out, x, g_post, S):
    TS = 512

    def body(oa_ref, ob_ref, w_ref, x_ref, g_ref, mix_ref, x1_ref):
        mix = _dot(oa_ref[...].astype(bf16), w_ref[0:GDN_W, :]) + _dot(ob_ref[...].astype(bf16), w_ref[GDN_W:D_MODEL, :])
        r = lax.rsqrt(jnp.mean(mix * mix, axis=-1, keepdims=True) + RMS_EPS)
        mix_ref[...] = mix
        x1_ref[...] = x_ref[...] + mix * r * g_ref[...]

    row = lambda w: pl.BlockSpec((TS, w), lambda i: (i, 0))
    return pl.pallas_call(
        body, name="mix_fwd", grid=(S // TS,),
        in_specs=[row(GDN_W), row(SWA_W), _resident_spec((D_MODEL, D_MODEL)), row(D_MODEL), _const_spec((1, D_MODEL))],
        out_specs=[row(D_MODEL), row(D_MODEL)],
        out_shape=[SDS((S, D_MODEL), f32), SDS((S, D_MODEL), f32)],
        compiler_params=_cp(1, VMEM_LIMIT),
    )(oa, ob, w_out, x, g_post)


def _mix_bwd(dx1, mix, g_post, w_out, ob, S):
    TS = 512

    def body(dx1_ref, mix_ref, g_ref, w_ref, ob_ref, dmix_ref, doab_ref, dg_ref, nd_ref):
        @pl.when(pl.program_id(0) == 0)
        def _():
            dg_ref[...] = jnp.zeros_like(dg_ref)

        mix = mix_ref[...]
        dz = dx1_ref[...]
        r = lax.rsqrt(jnp.mean(mix * mix, axis=-1, keepdims=True) + RMS_EPS)
        n = mix * r
        dg_ref[...] += jnp.sum(dz * n, axis=0, keepdims=True)
        dn = dz * g_ref[...]
        dmix = (r * (dn - n * jnp.mean(dn * n, axis=-1, keepdims=True))).astype(bf16)
        dmix_ref[...] = dmix
        doab = _dot_nt(dmix, w_ref[...])
        doab_ref[...] = doab
        hi_ = lax.shift_right_logical(lax.broadcasted_iota(jnp.int32, (SWA_W, SWA_W), 0), 6)
        hj_ = lax.shift_right_logical(lax.broadcasted_iota(jnp.int32, (SWA_W, SWA_W), 1), 6)
        swap = (hi_ == lax.bitwise_xor(hj_, 1)).astype(bf16)
        dlt = doab[:, GDN_W:] * ob_ref[...]
        hi = dlt.astype(bf16)
        nd_ref[...] = (_dot(hi, swap) + _dot((dlt - hi.astype(f32)).astype(bf16), swap)) * (-1.0 / SWA_HD)

    row = lambda w=D_MODEL: pl.BlockSpec((TS, w), lambda i: (i, 0))
    return pl.pallas_call(
        body, name="mix_bwd", grid=(S // TS,),
        in_specs=[row(), row(), _const_spec((1, D_MODEL)), _resident_spec((D_MODEL, D_MODEL)), row(SWA_W)],
        out_specs=[row(), row(), _const_spec((1, D_MODEL)), row(SWA_W)],
        out_shape=[SDS((S, D_MODEL), bf16), SDS((S, D_MODEL), f32), SDS((1, D_MODEL), f32), SDS((S, SWA_W), f32)],
        compiler_params=_cp(1, VMEM_LIMIT),
    )(dx1, mix, g_post, w_out, ob)


FFN_TS = 256
FFN_CH = 1408


def _ffn(x1, tgt, g_pre, g_post, wg, wu, wd, S):
    def body(x1_ref, t_ref, gp_ref, gq_ref, wg_ref, wu_ref, wd_ref,
             dx1_ref, h2_ref, act_ref, dgate_ref, dup_ref, df_ref, loss_ref, dgp_ref, dgq_ref, gate_scr, up_scr):
        @pl.when(pl.program_id(0) == 0)
        def _():
            loss_ref[...] = jnp.zeros_like(loss_ref)
            dgp_ref[...] = jnp.zeros_like(dgp_ref)
            dgq_ref[...] = jnp.zeros_like(dgq_ref)

        x1v = x1_ref[...]
        gp = gp_ref[...]
        gq = gq_ref[...]
        r2 = lax.rsqrt(jnp.mean(x1v * x1v, axis=-1, keepdims=True) + RMS_EPS)
        n2 = x1v * r2
        h2 = (n2 * gp).astype(bf16)
        h2_ref[...] = h2
        chunks = [slice(c * FFN_CH, (c + 1) * FFN_CH) for c in range(D_FF // FFN_CH)]
        for cs in chunks:
            gate_scr[:, cs] = _dot(h2, wg_ref[:, cs])
            up_scr[:, cs] = _dot(h2, wu_ref[:, cs])
        acts = []
        for cs in chunks:
            gate = gate_scr[:, cs]
            act = (gate * _sigmoid(gate) * up_scr[:, cs]).astype(bf16)
            act_ref[:, cs] = act
            acts.append(act)
        f = _dot(acts[0], wd_ref[chunks[0], :])
        for act, cs in zip(acts[1:], chunks[1:]):
            f = f + _dot(act, wd_ref[cs, :])
        r3 = lax.rsqrt(jnp.mean(f * f, axis=-1, keepdims=True) + RMS_EPS)
        n3 = f * r3
        err = x1v + n3 * gq - t_ref[...]
        loss_ref[...] += 0.5 * jnp.sum(jnp.mean(err * err, axis=-1, keepdims=True), axis=0, keepdims=True)
        dy = err * (1.0 / D_MODEL)
        dgq_ref[...] += jnp.sum(dy * n3, axis=0, keepdims=True)
        dn3 = dy * gq
        df = (r3 * (dn3 - n3 * jnp.mean(dn3 * n3, axis=-1, keepdims=True))).astype(bf16)
        df_ref[...] = df
        dacts = [_dot_nt(df, wd_ref[cs, :]) for cs in chunks]
        dgs = []
        for dact, cs in zip(dacts, chunks):
            gate = gate_scr[:, cs]
            sg = _sigmoid(gate)
            dup = (dact * gate * sg).astype(bf16)
            dgate = (dact * up_scr[:, cs] * (sg * (1.0 + gate * (1.0 - sg)))).astype(bf16)
            dup_ref[:, cs] = dup
            dgate_ref[:, cs] = dgate
            dgs.append((dgate, dup))
        dh2 = None
        for (dgate, dup), cs in zip(dgs, chunks):
            t = _dot_nt(dgate, wg_ref[:, cs]) + _dot_nt(dup, wu_ref[:, cs])
            dh2 = t if dh2 is None else dh2 + t
        dgp_ref[...] += jnp.sum(dh2 * n2, axis=0, keepdims=True)
        dn2 = dh2 * gp
        dx1_ref[...] = dy + r2 * (dn2 - n2 * jnp.mean(dn2 * n2, axis=-1, keepdims=True))

    row = lambda w: pl.BlockSpec((FFN_TS, w), lambda i: (i, 0))
    vec = _const_spec((1, D_MODEL))
    return pl.pallas_call(
        body, name="ffn_fwd_bwd", grid=(S // FFN_TS,),
        in_specs=[row(D_MODEL), row(D_MODEL), vec, vec, _resident_spec((D_MODEL, D_FF)), _resident_spec((D_MODEL, D_FF)),
                  _resident_spec((D_FF, D_MODEL))],
        out_specs=[row(D_MODEL), row(D_MODEL), row(D_FF), row(D_FF), row(D_FF), row(D_MODEL), _const_spec((1, LANE)), vec, vec],
        out_shape=[SDS((S, D_MODEL), f32), SDS((S, D_MODEL), bf16), SDS((S, D_FF), bf16), SDS((S, D_FF), bf16),
                   SDS((S, D_FF), bf16), SDS((S, D_MODEL), bf16), SDS((1, LANE), f32), SDS((1, D_MODEL), f32),
                   SDS((1, D_MODEL), f32)],
        scratch_shapes=[pltpu.VMEM((FFN_TS, D_FF), f32), pltpu.VMEM((FFN_TS, D_FF), f32)],
        compiler_params=_cp(1, VMEM_LIMIT),
    )(x1, tgt, g_pre, g_post, wg, wu, wd)


def _proj_bwd(x, dx1, g_pre, wcat, segs, after, S):
    TS = 256
    n = len(segs)
    cols = [(c0, a.shape[1]) for a, c0 in segs]

    def body(*refs):
        x_ref, dx1_ref, g_ref, w_ref = refs[:4]
        seg_refs = refs[4:4 + n]
        gx_ref, dg_ref = refs[5 + n:]

        @pl.when(pl.program_id(0) == 0)
        def _():
            dg_ref[...] = jnp.zeros_like(dg_ref)

        dh = jnp.zeros((TS, D_MODEL), f32)
        for s_ref, (c0, w) in zip(seg_refs, cols):
            dh = dh + _dot_nt(s_ref[...], w_ref[:, c0:c0 + w])
        xv = x_ref[...]
        g = g_ref[...]
        r = lax.rsqrt(jnp.mean(xv * xv, axis=-1, keepdims=True) + RMS_EPS)
        nx = xv * r
        dg_ref[...] += jnp.sum(dh * nx, axis=0, keepdims=True)
        dn = dh * g
        gx_ref[...] = dx1_ref[...] + r * (dn - nx * jnp.mean(dn * nx, axis=-1, keepdims=True))

    row = lambda w: pl.BlockSpec((TS, w), lambda i: (i, 0))
    return pl.pallas_call(
        body, name="proj_bwd", grid=(S // TS,),
        in_specs=[row(D_MODEL), row(D_MODEL), _const_spec((1, D_MODEL)), _resident_spec((D_MODEL, NCOL))]
                 + [row(w) for _, w in cols] + [_ANY],
        out_specs=[row(D_MODEL), _const_spec((1, D_MODEL))],
        out_shape=[SDS((S, D_MODEL), f32), SDS((1, D_MODEL), f32)],
        compiler_params=_cp(1, VMEM_LIMIT),
    )(x, dx1, g_pre, wcat, *[a for a, _ in segs], after)


def _wgrad(a, b, S, name, col_blocks=False):
    TS = 1024
    K = a.shape[1]
    N = b.shape[1]
    TN = next(t for t in (512, 1408, N) if N % t == 0)
    cb = N // N_DEV
    nblk = TN // cb if col_blocks else 0

    def body(a_ref, b_ref, o_ref, acc):
        @pl.when(pl.program_id(1) == 0)
        def _():
            acc[...] = jnp.zeros_like(acc)

        acc[...] += _dot_tn(a_ref[...].astype(bf16), b_ref[...])

        @pl.when(pl.program_id(1) == pl.num_programs(1) - 1)
        def _():
            if col_blocks:
                for i in range(nblk):
                    o_ref[i] = acc[:, i * cb:(i + 1) * cb].astype(bf16)
            else:
                o_ref[...] = acc[...].astype(bf16)

    if col_blocks:
        out_spec = pl.BlockSpec((nblk, K, cb), lambda j, s: (j, 0, 0))
        out_shape = SDS((N_DEV, K, cb), bf16)
    else:
        out_spec = pl.BlockSpec((K, TN), lambda j, s: (0, j))
        out_shape = SDS((K, N), bf16)
    return pl.pallas_call(
        body, name=name, grid=(N // TN, S // TS),
        in_specs=[pl.BlockSpec((TS, K), lambda j, s: (s, 0)), pl.BlockSpec((TS, TN), lambda j, s: (s, j))],
        out_specs=out_spec, out_shape=out_shape,
        scratch_shapes=[pltpu.VMEM((K, TN), f32)],
        compiler_params=_cp(2, VMEM_LIMIT),
    )(a, b)


def _w_in_pieces():
    n_a, n_g = 4 * GDN_W, 2 * GDN_HEADS
    cb = IN_COLS // N_DEV
    bounds = [(0, n_a, COL_A), (n_a, n_a + n_g, COL_G), (n_a + n_g, IN_COLS, COL_B)]
    out = []
    for j in range(N_DEV):
        lo, hi = j * cb, (j + 1) * cb
        for s0, s1, dst in bounds:
            a, b = max(lo, s0), min(hi, s1)
            if a < b:
                out.append((j, a - lo, b - a, dst + a - s0))
    return out


def _wcat_from_blocks(g_in):
    TR = 256
    cb = IN_COLS // N_DEV
    pieces = _w_in_pieces()

    def body(w_ref, o_ref):
        o_ref[:, COL_G:NCOL] = jnp.zeros((TR, NCOL - COL_G), bf16)
        for j, off, w, dst in pieces:
            o_ref[:, dst:dst + w] = w_ref[j, :, off:off + w]

    return pl.pallas_call(
        body, name="wcat_from_blocks", grid=(D_MODEL // TR,),
        in_specs=[pl.BlockSpec((N_DEV, TR, cb), lambda i: (0, i, 0))],
        out_specs=pl.BlockSpec((TR, NCOL), lambda i: (i, 0)),
        out_shape=SDS((D_MODEL, NCOL), bf16),
        compiler_params=_cp(1, VMEM_LIMIT),
    )(g_in)


def _full_from_col_blocks(g):
    n, R, C = g.shape
    TR = 256

    def body(w_ref, o_ref):
        for j in range(n):
            o_ref[:, j * C:(j + 1) * C] = w_ref[j]

    return pl.pallas_call(
        body, name="full_from_col_blocks", grid=(R // TR,),
        in_specs=[pl.BlockSpec((n, TR, C), lambda i: (0, i, 0))],
        out_specs=pl.BlockSpec((TR, n * C), lambda i: (i, 0)),
        out_shape=SDS((R, n * C), g.dtype),
        compiler_params=_cp(1, VMEM_LIMIT),
    )(g)


def _wgrad_in(h1, segs, S):
    TS = 1024
    n = len(segs)
    cols = [(c0, a.shape[1]) for a, c0 in segs]
    cb = IN_COLS // N_DEV
    pieces = _w_in_pieces()

    def body(*refs):
        h_ref = refs[0]
        seg_refs = refs[1:1 + n]
        o_ref, acc = refs[1 + n], refs[2 + n]

        @pl.when(pl.program_id(0) == 0)
        def _():
            acc[...] = jnp.zeros_like(acc)

        h = h_ref[...]
        for s_ref, (c0, w) in zip(seg_refs, cols):
            acc[:, c0:c0 + w] += _dot_tn(h, s_ref[...])

        @pl.when(pl.program_id(0) == pl.num_programs(0) - 1)
        def _():
            for j, off, w, src in pieces:
                o_ref[j, :, off:off + w] = acc[:, src:src + w].astype(bf16)

    row = lambda w: pl.BlockSpec((TS, w), lambda i: (i, 0))
    return pl.pallas_call(
        body, name="wgrad_in", grid=(S // TS,),
        in_specs=[row(D_MODEL)] + [row(w) for _, w in cols],
        out_specs=_const_spec((N_DEV, D_MODEL, cb)),
        out_shape=SDS((N_DEV, D_MODEL, cb), bf16),
        scratch_shapes=[pltpu.VMEM((D_MODEL, NCOL), f32)],
        compiler_params=_cp(1, VMEM_LIMIT),
    )(h1, *[a for a, _ in segs])


def _adamw(recv, src, me, w, m, v, name):
    R, C = w.shape
    TR = 256 if R % 256 == 0 else R
    c1 = 1.0 / (1.0 - ADAM_B1 ** ADAM_STEP)
    c2 = 1.0 / (1.0 - ADAM_B2 ** ADAM_STEP)

    def body(me_ref, r_ref, own_ref, w_ref, m_ref, v_ref, g_out, d_out, m_out, v_out):
        g = None
        for s in range(N_DEV):
            t = jnp.where(me_ref[0] == s, own_ref[0], r_ref[s]).astype(f32)
            g = t if g is None else g + t
        mn = ADAM_B1 * m_ref[...] + (1.0 - ADAM_B1) * g
        vn = ADAM_B2 * v_ref[...] + (1.0 - ADAM_B2) * (g * g)
        g_out[...] = g
        m_out[...] = mn
        v_out[...] = vn
        d_out[...] = -ADAM_LR * ((mn * c1) / (jnp.sqrt(vn * c2) + ADAM_EPS) + ADAM_WD * w_ref[...])

    blk = pl.BlockSpec((TR, C), lambda i, me_ref: (i, 0))
    return pl.pallas_call(
        body, name=name,
        grid_spec=pltpu.PrefetchScalarGridSpec(
            num_scalar_prefetch=1, grid=(R // TR,),
            in_specs=[pl.BlockSpec((N_DEV, TR, C), lambda i, me_ref: (0, i, 0)),
                      pl.BlockSpec((1, TR, C), lambda i, me_ref: (me_ref[0], i, 0)), blk, blk, blk],
            out_specs=[blk, blk, blk, blk]),
        out_shape=[SDS((R, C), f32)] * 4,
        compiler_params=_cp(1, VMEM_LIMIT),
    )(me, recv, src, w, m, v)


MESH = pl.DeviceIdType.MESH
_ANY = pl.BlockSpec(memory_space=pl.ANY)


def _flip(v, d):
    return 1 - v if d else v


def _all_gather(shards):
    n = len(shards)

    def body(*refs):
        ins = refs[:n]
        outs = refs[n:2 * n]
        send_sems, recv_sems, local_sems = refs[2 * n:]
        x, y, c = lax.axis_index("x"), lax.axis_index("y"), lax.axis_index("c")
        me, sibling = (x, y, c), (x, y, 1 - c)
        chips = [(1 - x, y), (x, 1 - y), (1 - x, 1 - y)]

        def slot(px, py, pc):
            return 4 * px + 2 * py + pc

        def copy(a, k, block, to, src=None):
            dst = outs[a].at[slot(*block)]
            return pltpu.make_async_remote_copy(src_ref=dst if src is None else src, dst_ref=dst,
                                                send_sem=send_sems.at[a, k], recv_sem=recv_sems.at[a, k],
                                                device_id=to, device_id_type=MESH)

        mine, first, passed = [], [], []
        for a in range(n):
            cp = pltpu.make_async_copy(ins[a], outs[a].at[slot(*me)], local_sems.at[a])
            cp.start()
            mine.append(cp)
            fs = [copy(a, 0, me, sibling, src=ins[a])]
            fs += [copy(a, 1 + j, me, (*chip, c), src=ins[a]) for j, chip in enumerate(chips)]
            for cp in fs:
                cp.start()
            first += fs
        for j, chip in enumerate(chips):
            for a in range(n):
                copy(a, 1 + j, (*chip, c), me).wait_recv()
                cp = copy(a, 4 + j, (*chip, c), sibling)
                cp.start()
                passed.append(cp)
        for a in range(n):
            copy(a, 0, sibling, me).wait_recv()
            for j, chip in enumerate(chips):
                copy(a, 4 + j, (*chip, 1 - c), me).wait_recv()
        for cp in first + passed:
            cp.wait_send()
        for cp in mine:
            cp.wait()

    return pl.pallas_call(
        body, name="weight_all_gather",
        in_specs=[_ANY] * n, out_specs=[_ANY] * n,
        out_shape=[SDS((N_DEV,) + s.shape, s.dtype) for s in shards],
        scratch_shapes=[pltpu.SemaphoreType.DMA((n, 7)), pltpu.SemaphoreType.DMA((n, 7)), pltpu.SemaphoreType.DMA((n,))],
        compiler_params=pltpu.CompilerParams(has_side_effects=True),
    )(*shards)


def _grad_exchange(blocked, whole):
    arrs = list(blocked) + list(whole)
    n, nb = len(arrs), len(blocked)
    rel = [(dx, dy, dc) for dx in (0, 1) for dy in (0, 1) for dc in (0, 1) if dx or dy or dc]

    def body(*refs):
        ins = refs[:n]
        outs = refs[n:2 * n]
        send_sems, recv_sems, local_sems = refs[2 * n:]
        x, y, c = lax.axis_index("x"), lax.axis_index("y"), lax.axis_index("c")
        me = 4 * x + 2 * y + c
        sends, locs = [], []
        for a in range(n):
            cp = pltpu.make_async_copy(ins[a].at[me] if a < nb else ins[a], outs[a].at[me], local_sems.at[a])
            cp.start()
            locs.append(cp)
            for k, (dx, dy, dc) in enumerate(rel):
                peer = (_flip(x, dx), _flip(y, dy), _flip(c, dc))
                pidx = 4 * peer[0] + 2 * peer[1] + peer[2]
                cp = pltpu.make_async_remote_copy(src_ref=ins[a].at[pidx] if a < nb else ins[a], dst_ref=outs[a].at[me],
                                                  send_sem=send_sems.at[a, k], recv_sem=recv_sems.at[a, k],
                                                  device_id=peer, device_id_type=MESH)
                cp.start()
                sends.append(cp)
        for a in range(n):
            for k, (dx, dy, dc) in enumerate(rel):
                peer = (_flip(x, dx), _flip(y, dy), _flip(c, dc))
                pidx = 4 * peer[0] + 2 * peer[1] + peer[2]
                pltpu.make_async_remote_copy(src_ref=outs[a].at[pidx], dst_ref=outs[a].at[pidx],
                                             send_sem=send_sems.at[a, k], recv_sem=recv_sems.at[a, k],
                                             device_id=peer, device_id_type=MESH).wait_recv()
        for cp in sends:
            cp.wait_send()
        for cp in locs:
            cp.wait()

    shapes = [SDS(a.shape, a.dtype) for a in blocked] + [SDS((N_DEV,) + a.shape, a.dtype) for a in whole]
    return pl.pallas_call(
        body, name="grad_exchange",
        in_specs=[_ANY] * n, out_specs=[_ANY] * n, out_shape=shapes,
        scratch_shapes=[pltpu.SemaphoreType.DMA((n, 7)), pltpu.SemaphoreType.DMA((n, 7)), pltpu.SemaphoreType.DMA((n,))],
        compiler_params=pltpu.CompilerParams(has_side_effects=True),
    )(*arrs)


_HBM = pl.BlockSpec(memory_space=pltpu.HBM)
_SEM = pl.BlockSpec(memory_space=pltpu.SEMAPHORE)
_REL = [(dx, dy, dc) for dx in (0, 1) for dy in (0, 1) for dc in (0, 1) if dx or dy or dc]


N_PEER = len(_REL)
_EFFECT = pltpu.SideEffectType.DATAFLOW_SIDE_EFFECTING


def _peer_copies(srcs, lands, send_sems, recv_sems, blocked, as_receiver):
    x, y, c = lax.axis_index("x"), lax.axis_index("y"), lax.axis_index("c")
    me = 4 * x + 2 * y + c
    cps = []
    for a in range(len(srcs)):
        for k, (dx, dy, dc) in enumerate(_REL):
            peer = (_flip(x, dx), _flip(y, dy), _flip(c, dc))
            pidx = 4 * peer[0] + 2 * peer[1] + peer[2]
            cps.append(pltpu.make_async_remote_copy(
                src_ref=srcs[a].at[pidx] if blocked else srcs[a], dst_ref=lands[a].at[pidx if as_receiver else me],
                send_sem=send_sems[a * N_PEER + k], recv_sem=recv_sems[a * N_PEER + k],
                device_id=peer, device_id_type=MESH))
    return cps


def _exchange_start(srcs, after, blocked, name):
    n = len(srcs)
    ns = n * N_PEER
    lands = [lax.empty(s.shape if blocked else (N_DEV,) + s.shape, s.dtype) for s in srcs]

    def body(*refs):
        ins, lnd = refs[:n], refs[n:2 * n]
        outs = refs[2 * n + 1:]
        for cp in _peer_copies(ins, lnd, outs[:ns], outs[ns:2 * ns], blocked, False):
            cp.start()
        outs[-1][...] = jnp.zeros_like(outs[-1])

    res = pl.pallas_call(
        body, name=name,
        in_specs=[_HBM] * (2 * n) + [_ANY],
        out_specs=[_SEM] * (2 * ns) + [_HBM] * (2 * n) + [pl.BlockSpec(memory_space=pltpu.VMEM)],
        out_shape=[pltpu.SemaphoreType.DMA(())] * (2 * ns) + [pltpu.HBM(s.shape, s.dtype) for s in srcs]
                  + [pltpu.HBM(l.shape, l.dtype) for l in lands] + [SDS((8, LANE), f32)],
        input_output_aliases={i: 2 * ns + i for i in range(2 * n)},
        compiler_params=pltpu.CompilerParams(has_side_effects=_EFFECT),
    )(*[pltpu.with_memory_space_constraint(s, pltpu.HBM) for s in srcs],
      *[pltpu.with_memory_space_constraint(l, pltpu.HBM) for l in lands], after)
    return list(res[:2 * ns]), list(res[2 * ns:2 * ns + n]), list(res[2 * ns + n:2 * ns + 2 * n]), res[-1]


def _exchange_wait(sems, srcs, lands, after, blocked, name):
    n = len(srcs)
    ns = n * N_PEER

    def body(*refs):
        ins, lnd = refs[:n], refs[n:2 * n]
        sem_refs = refs[2 * n:2 * n + 2 * ns]
        for cp in _peer_copies(ins, lnd, sem_refs[:ns], sem_refs[ns:], blocked, True):
            cp.wait_send()
            cp.wait_recv()

    res = pl.pallas_call(
        body, name=name,
        in_specs=[_HBM] * (2 * n) + [_SEM] * (2 * ns) + [_ANY],
        out_specs=[_HBM] * (2 * n),
        out_shape=[pltpu.HBM(s.shape, s.dtype) for s in srcs] + [pltpu.HBM(l.shape, l.dtype) for l in lands],
        input_output_aliases={i: i for i in range(2 * n)},
        compiler_params=pltpu.CompilerParams(has_side_effects=_EFFECT),
    )(*srcs, *lands, *sems, after)
    return list(res[:n]), list(res[n:])


def _local_step(x, tgt, wcat, convw, late_weights, early_grads, last_grads, token, a_log, dt_bias, onorm_g, rel_bias,
                g_mix_pre, g_mix_post, g_ffn_pre, g_ffn_post):
    S = x.shape[0]
    bk_np = _bucket_tables()
    bk = jnp.asarray(bk_np)
    bt = _bias_tables(rel_bias, bk)
    proj, h1 = _proj_fwd(x, g_mix_pre, wcat, token, S)
    nu = S // CHUNK * GDN_HEADS
    qkv_u = _gdn_prep(proj, convw, S).reshape(3, nu, CHUNK, GDN_HD)
    intra, t_inv = _gdn_intra_fwd(qkv_u, proj, a_log, dt_bias, S)
    oa, states = _gdn_scan_fwd(intra, proj, onorm_g, S)
    ob, lse0, lse1 = _swa_fwd(proj, bt, S)
    wout, ffn_weights = late_weights(ob)
    mix, x1 = _mix_fwd(oa, ob, wout, x, g_mix_post, S)
    wgate, wup, wdown = ffn_weights(x1)
    dx1, h2, act, dgate_f, dup_f, df, loss, d_gfpre, d_gfpost = _ffn(x1, tgt, g_ffn_pre, g_ffn_post, wgate, wup, wdown, S)
    g_gate = _wgrad(h2, dgate_f, S, "wgrad_gate", col_blocks=True)
    g_up = _wgrad(h2, dup_f, S, "wgrad_up", col_blocks=True)
    g_down = _wgrad(act, df, S, "wgrad_down").reshape(N_DEV, D_FF // N_DEV, D_MODEL)
    dmix, d_oab, d_gmpost, nd = _mix_bwd(dx1, mix, g_mix_post, wout, ob, S)
    g_out = jnp.concatenate([_wgrad(oa, dmix, S, "wgrad_out_a"), _wgrad(ob, dmix, S, "wgrad_out_b")], axis=0)
    token = early_grads(g_out.reshape(N_DEV, D_MODEL // N_DEV, D_MODEL), g_gate, g_up, g_down)
    dqb, dkb, dvb, dsb = _swa_bwd(proj, bt, nd, lse0, lse1, d_oab, token, S)
    *cots, dgate_a, d_og = _gdn_scan_bwd(intra, states, proj, d_oab, onorm_g, token, S)
    dqkv_u, dpg, d_alog, d_dtb = _gdn_intra_bwd(qkv_u, proj, a_log, dt_bias, t_inv, cots, S)
    dqkv_a, d_conv = _gdn_prep_bwd(proj, convw, dqkv_u.reshape(3, S // CHUNK, GDN_HEADS, CHUNK, GDN_HD), S)
    segs = [(dqkv_a, COL_A), (dgate_a, COL_A + 3 * GDN_W), (dqb, COL_B), (dkb, COL_B + SWA_W), (dvb, COL_B + 2 * SWA_W),
            (dpg, COL_G)]
    token = last_grads(_wgrad_in(h1, segs, S), d_conv)
    grad_x, d_gmpre = _proj_bwd(x, dx1, g_mix_pre, wcat, segs, token, S)
    d_rel = _rel_bias_grad(dsb, bk, bk_np)
    small = dict(a_log=d_alog[:, GDN_HEADS:2 * GDN_HEADS], dt_bias=d_dtb[:, GDN_HEADS:2 * GDN_HEADS], onorm_g=d_og, rel_bias=d_rel,
                 g_mix_pre=d_gmpre, g_mix_post=d_gmpost, g_ffn_pre=d_gfpre, g_ffn_post=d_gfpost)
    return loss, grad_x, small


SMALL = ("a_log", "dt_bias", "onorm_g", "rel_bias", "g_mix_pre", "g_mix_post", "g_ffn_pre", "g_ffn_post")
PACK_ROWS = 8


def _pack_small(d, loss=None):
    rest = jnp.concatenate([d["onorm_g"].reshape(-1), d["a_log"].reshape(-1), d["dt_bias"].reshape(-1),
                            d["rel_bias"].reshape(-1)])
    rest = jnp.concatenate([rest, jnp.zeros((D_MODEL - rest.shape[0],), f32)])
    extra = jnp.zeros((D_MODEL,), f32) if loss is None else jnp.concatenate([loss.reshape(1), jnp.zeros((D_MODEL - 1,), f32)])
    rows = [d["g_mix_pre"].reshape(-1), d["g_mix_post"].reshape(-1), d["g_ffn_pre"].reshape(-1),
            d["g_ffn_post"].reshape(-1), rest, extra]
    return jnp.concatenate([jnp.stack(rows), jnp.zeros((PACK_ROWS - len(rows), D_MODEL), f32)], axis=0)


def _unpack_small(p):
    o = GDN_HD
    return dict(g_mix_pre=p[0:1], g_mix_post=p[1:2], g_ffn_pre=p[2:3], g_ffn_post=p[3:4],
                onorm_g=p[4:5, :o], a_log=p[4:5, o:o + 4], dt_bias=p[4:5, o + 4:o + 8],
                rel_bias=p[4, o + 8:o + 8 + NUM_BUCKETS * SWA_HEADS].reshape(NUM_BUCKETS, SWA_HEADS))


def kernel(x, w_in, conv_w, a_log, dt_bias, onorm_g, rel_bias, w_out, g_mix_pre, g_mix_post, w_gate, w_up, w_down, g_ffn_pre, g_ffn_post, loss_target, m_w_in, m_conv_w, m_a_log, m_dt_bias, m_onorm_g, m_rel_bias, m_w_out, m_g_mix_pre, m_g_mix_post, m_w_gate, m_w_up, m_w_down, m_g_ffn_pre, m_g_ffn_post, v_w_in, v_conv_w, v_a_log, v_dt_bias, v_onorm_g, v_rel_bias, v_w_out, v_g_mix_pre, v_g_mix_post, v_w_gate, v_w_up, v_w_down, v_g_ffn_pre, v_g_ffn_post):
    big = ("w_in", "conv_w", "w_out", "w_gate", "w_up", "w_down")
    w_sh = dict(w_in=w_in[0], conv_w=conv_w[0], w_out=w_out[0], w_gate=w_gate[0], w_up=w_up[0], w_down=w_down[0])
    m_sh = dict(w_in=m_w_in[0], conv_w=m_conv_w[0], w_out=m_w_out[0], w_gate=m_w_gate[0], w_up=m_w_up[0], w_down=m_w_down[0])
    v_sh = dict(w_in=v_w_in[0], conv_w=v_conv_w[0], w_out=v_w_out[0], w_gate=v_w_gate[0], w_up=v_w_up[0], w_down=v_w_down[0])
    w_small = dict(a_log=a_log, dt_bias=dt_bias, onorm_g=onorm_g, rel_bias=rel_bias, g_mix_pre=g_mix_pre,
                   g_mix_post=g_mix_post, g_ffn_pre=g_ffn_pre, g_ffn_post=g_ffn_post)
    m_small = dict(a_log=m_a_log, dt_bias=m_dt_bias, onorm_g=m_onorm_g, rel_bias=m_rel_bias, g_mix_pre=m_g_mix_pre,
                   g_mix_post=m_g_mix_post, g_ffn_pre=m_g_ffn_pre, g_ffn_post=m_g_ffn_post)
    v_small = dict(a_log=v_a_log, dt_bias=v_dt_bias, onorm_g=v_onorm_g, rel_bias=v_rel_bias, g_mix_pre=v_g_mix_pre,
                   g_mix_post=v_g_mix_post, g_ffn_pre=v_g_ffn_pre, g_ffn_post=v_g_ffn_post)

    me = 4 * lax.axis_index("x") + 2 * lax.axis_index("y") + lax.axis_index("c")
    me1 = me.reshape(1).astype(jnp.int32)
    own = lambda full, part: lax.dynamic_update_index_in_dim(full, part, me, 0)
    cols = lambda g: g.reshape(g.shape[0], N_DEV, g.shape[1] // N_DEV).transpose(1, 0, 2)
    late = ("w_out", "w_gate", "w_up", "w_down")

    late_src = [w_sh[k].astype(bf16) for k in late]
    g_in, g_conv = _all_gather([w_sh["w_in"].astype(bf16), w_sh["conv_w"]])
    g_sems, g_src, g_land, g_token = _exchange_start(late_src, g_conv, False, "late_weights_start")
    wcat = _wcat_from_blocks(g_in)
    convw = g_conv.transpose(1, 0, 2).reshape(4, 3 * GDN_W)

    def late_weights(after):
        pick = lambda idx: [g_sems[half * len(late) * N_PEER + a * N_PEER + k] for half in (0, 1) for a in idx for k in range(N_PEER)]
        (s_out,), (l_out,) = _exchange_wait(pick([0]), g_src[:1], g_land[:1], after, False, "w_out_wait")

        def ffn_weights(after2):
            srcs, lands = _exchange_wait(pick([1, 2, 3]), g_src[1:], g_land[1:], after2, False, "ffn_weights_wait")
            g_gate, g_up, g_down = [own(l, s) for l, s in zip(lands, srcs)]
            return _full_from_col_blocks(g_gate), _full_from_col_blocks(g_up), g_down.reshape(D_FF, D_MODEL)

        return own(l_out, s_out).reshape(D_MODEL, D_MODEL), ffn_weights

    early, last = {}, {}

    def early_grads(*blocks):
        early["sems"], early["src"], early["land"], token = _exchange_start(list(blocks), me1, True, "late_grads_start")
        return token

    def last_grads(gw_in, gw_conv):
        src = [gw_in, cols(gw_conv)]
        last["sems"], last["src"], last["land"], token = _exchange_start(src, me1, True, "last_grads_start")
        return token

    loss_p, grad_x, gsmall = _local_step(
        x[0], loss_target[0], wcat, convw, late_weights, early_grads, last_grads, g_token,
        a_log, dt_bias, onorm_g, rel_bias, g_mix_pre, g_mix_post, g_ffn_pre, g_ffn_post)

    (r_small,) = _grad_exchange([], [_pack_small(gsmall, loss_p[0, 0])])
    outs = {}
    for names, ex, after, name in ((late, early, grad_x, "late_grads_wait"), (("w_in", "conv_w"), last, r_small, "last_grads_wait")):
        srcs, lands = _exchange_wait(ex["sems"], ex["src"], ex["land"], after, True, name)
        for k, l, s in zip(names, lands, srcs):
            outs[k] = _adamw(l, s, me1, w_sh[k], m_sh[k], v_sh[k], "adamw_" + k)
    sm = _adamw(r_small, r_small, me1, _pack_small(w_small), _pack_small(m_small), _pack_small(v_small), "adamw_small")
    loss = sm[0][5, 0]
    sm = [_unpack_small(t) for t in sm]
    for k in SMALL:
        outs[k] = tuple(t[k].reshape(w_small[k].shape) for t in sm)

    order = ("w_in", "conv_w", "a_log", "dt_bias", "onorm_g", "rel_bias", "w_out", "g_mix_pre", "g_mix_post", "w_gate",
             "w_up", "w_down", "g_ffn_pre", "g_ffn_post")
    lead = lambda k, t: t[None] if k in big else t
    res = [loss, grad_x[None]]
    for i in range(4):
        res += [lead(k, outs[k][i]) for k in order]
    return tuple(res)
```

```python
import functools
import math

import numpy as np
import jax
import jax.numpy as jnp
from jax import lax
from jax.experimental import pallas as pl
from jax.experimental.pallas import tpu as pltpu

f32 = jnp.float32
bf16 = jnp.bfloat16
SDS = jax.ShapeDtypeStruct

D_MODEL = 1024
GDN_HEADS = 4
GDN_HD = 128
GDN_W = 512
CHUNK = 64
SWA_HEADS = 8
SWA_HD = 64
SWA_W = 512
D_FF = 2816
IN_COLS = 3592
PATTERNS = ((128, 1), (512, 4), (2048, 16))
SWA_BLK = 128
NUM_BUCKETS = 32
MAX_DISTANCE = 2048
RMS_EPS = 1e-6
NEG = -1e30
N_DEV = 8

COL_A = 0
COL_B = 2048
COL_G = 3584
NCOL = 3712
LANE = 128

ADAM_LR, ADAM_B1, ADAM_B2, ADAM_EPS, ADAM_WD, ADAM_STEP = 0.001, 0.9, 0.999, 1e-08, 0.01, 10

VMEM_LIMIT = 56 * 1024 * 1024

HI = lax.Precision.HIGHEST
HIGH = lax.Precision.HIGH


def _cp(n_grid=0, vmem=None):
    kw = {}
    if n_grid:
        kw["dimension_semantics"] = ("arbitrary",) * n_grid
    if vmem:
        kw["vmem_limit_bytes"] = vmem
    return pltpu.CompilerParams(**kw)


def _dot(a, b):
    return jnp.dot(a, b, preferred_element_type=f32)


def _dot_nt(a, b):
    return lax.dot_general(a, b, (((1,), (1,)), ((), ())), preferred_element_type=f32)


def _dot_tn(a, b):
    return lax.dot_general(a, b, (((0,), (0,)), ((), ())), preferred_element_type=f32)


def _dot_hi(a, b):
    return jnp.dot(a, b, precision=HI, preferred_element_type=f32)


def _sigmoid(x):
    return 0.5 * jnp.tanh(0.5 * x) + 0.5


def _softplus(x):
    return jnp.maximum(x, 0.0) + jnp.log(1.0 + jnp.exp(-jnp.abs(x)))


def _const_spec(shape):
    nd = len(shape)
    return pl.BlockSpec(shape, lambda *_: (0,) * nd)


def _resident_spec(shape):
    nd = len(shape)
    return pl.BlockSpec(shape, lambda *_: (0,) * nd, pipeline_mode=pl.Buffered(1))


def _t5_bucket_np(dist):
    max_exact = NUM_BUCKETS // 2
    d = np.maximum(dist, 1).astype(np.float32)
    log_b = max_exact + (np.log(d / np.float32(max_exact)) / np.float32(math.log(MAX_DISTANCE / max_exact))
                         * np.float32(NUM_BUCKETS - max_exact)).astype(np.int32)
    return np.where(dist < max_exact, dist, np.minimum(log_b, NUM_BUCKETS - 1)).astype(np.int32)


def _bucket_tables():
    w = SWA_BLK
    qi = np.arange(w)[:, None]
    kj = np.arange(w)[None, :]
    rel = np.where(kj <= qi, qi - kj, qi + w - kj)
    out = np.zeros((len(PATTERNS), w, w), np.int32)
    for p, (_, dil) in enumerate(PATTERNS):
        steps = _t5_bucket_np(np.arange(w + 1) * dil)
        assert steps[w] == steps[w - 1]
        out[p] = steps[rel]
    return out


def _bias_tables(rel_bias, bk):
    def body(rb_ref, bk_ref, o_ref):
        b_idx = bk_ref[0]
        for h in range(SWA_HEADS):
            def lp(b, acc):
                return jnp.where(b_idx == b, rb_ref[b, h], acc)
            o_ref[0, h] = lax.fori_loop(0, NUM_BUCKETS, lp, jnp.zeros((SWA_BLK, SWA_BLK), f32))

    return pl.pallas_call(
        body, name="bias_tables", grid=(3,),
        in_specs=[pl.BlockSpec(memory_space=pltpu.SMEM), pl.BlockSpec((1, SWA_BLK, SWA_BLK), lambda p: (p, 0, 0))],
        out_specs=pl.BlockSpec((1, SWA_HEADS, SWA_BLK, SWA_BLK), lambda p: (p, 0, 0, 0)),
        out_shape=SDS((3, SWA_HEADS, SWA_BLK, SWA_BLK), f32),
        compiler_params=_cp(1),
    )(rel_bias, bk)


def _rel_bias_grad(dsb, bk, bk_np):
    present = [sorted(set(int(v) for v in np.unique(bk_np[p]))) for p in range(3)]

    def body(ds_ref, bk_ref, o_ref):
        row = lax.broadcasted_iota(jnp.int32, (NUM_BUCKETS, LANE), 0)
        col = lax.broadcasted_iota(jnp.int32, (NUM_BUCKETS, SWA_HEADS), 1)
        out = jnp.zeros((NUM_BUCKETS, SWA_HEADS), f32)
        for hp in range(4):
            for hh in range(2):
                acc = jnp.zeros((NUM_BUCKETS, LANE), f32)
                for p in range(3):
                    tile = ds_ref[hp, p, hh]
                    b_idx = bk_ref[p]
                    for b in present[p]:
                        part = jnp.sum(jnp.where(b_idx == b, tile, 0.0), axis=0, keepdims=True)
                        acc = acc + jnp.where(row == b, part, 0.0)
                tot = jnp.sum(acc, axis=1, keepdims=True)
                out = out + jnp.where(col == 2 * hp + hh, tot, 0.0)
        o_ref[...] = out

    return pl.pallas_call(body, name="rel_bias_grad", out_shape=SDS((NUM_BUCKETS, SWA_HEADS), f32),
                          compiler_params=_cp(0, 32 * 1024 * 1024))(dsb, bk)


def _proj_fwd(x, g_pre, wcat, after, S):
    TS = 256

    def body(x_ref, g_ref, w_ref, after_ref, o_ref, h_ref):
        xv = x_ref[...]
        r = lax.rsqrt(jnp.mean(xv * xv, axis=-1, keepdims=True) + RMS_EPS)
        h = (xv * r * g_ref[...]).astype(bf16)
        h_ref[...] = h
        o_ref[...] = _dot(h, w_ref[...])

    return pl.pallas_call(
        body, name="proj_fwd", grid=(S // TS,),
        in_specs=[pl.BlockSpec((TS, D_MODEL), lambda i: (i, 0)), _const_spec((1, D_MODEL)),
                  _resident_spec((D_MODEL, NCOL)), _ANY],
        out_specs=[pl.BlockSpec((TS, NCOL), lambda i: (i, 0)), pl.BlockSpec((TS, D_MODEL), lambda i: (i, 0))],
        out_shape=[SDS((S, NCOL), f32), SDS((S, D_MODEL), bf16)],
        compiler_params=_cp(1, VMEM_LIMIT),
    )(x, g_pre, wcat, after)


CONV_RT = 256
HALO = 8


CONV_NC = CONV_RT // CHUNK


def _gdn_prep(proj, conv_w, S):
    def body(p_ref, cw_ref, o_ref, xs_ref):
        t = pl.program_id(0)
        xs_ref[pl.ds(0, HALO), :] = jnp.zeros((HALO, LANE), f32)
        xs_ref[pl.ds(HALO, S), :] = p_ref[...]
        w = cw_ref[...]
        is_qk = t < 2
        scale = jnp.where(t == 0, GDN_HD ** -0.5, 1.0).astype(f32)

        def lp(c, carry):
            st = pl.multiple_of(c * CONV_RT, CONV_RT)
            pre = xs_ref[pl.ds(st + HALO - 3, CONV_RT), :] * w[0:1, :]
            for i in range(1, 4):
                pre = pre + xs_ref[pl.ds(st + HALO - 3 + i, CONV_RT), :] * w[i:i + 1, :]
            s = pre * _sigmoid(pre)
            nrm = s * lax.rsqrt(jnp.sum(s * s, axis=-1, keepdims=True) + 1e-6) * scale
            out = jnp.where(is_qk, nrm, s)
            for i in range(CONV_NC):
                o_ref[0, c * CONV_NC + i, 0] = out[i * CHUNK:(i + 1) * CHUNK]
            return carry

        lax.fori_loop(0, S // CONV_RT, lp, 0)

    return pl.pallas_call(
        body, name="gdn_prep", grid=(3, GDN_HEADS),
        in_specs=[pl.BlockSpec((S, LANE), lambda t, h: (0, t * GDN_HEADS + h)),
                  pl.BlockSpec((4, LANE), lambda t, h: (0, t * GDN_HEADS + h))],
        out_specs=pl.BlockSpec((1, S // CHUNK, 1, CHUNK, GDN_HD), lambda t, h: (t, 0, h, 0, 0)),
        out_shape=SDS((3, S // CHUNK, GDN_HEADS, CHUNK, GDN_HD), f32),
        scratch_shapes=[pltpu.VMEM((S + HALO, LANE), f32)],
        compiler_params=_cp(2, VMEM_LIMIT),
    )(proj, conv_w)


def _gdn_prep_bwd(proj, conv_w, dqkv, S):
    def body(p_ref, cw_ref, d_ref, dx_ref, dw_ref, xs_ref, dp_ref):
        t = pl.program_id(0)
        xs_ref[pl.ds(0, HALO), :] = jnp.zeros((HALO, LANE), f32)
        xs_ref[pl.ds(HALO, S), :] = p_ref[...]
        dp_ref[pl.ds(S, HALO), :] = jnp.zeros((HALO, LANE), f32)
        w = cw_ref[...]
        is_qk = t < 2
        scale = jnp.where(t == 0, GDN_HD ** -0.5, 1.0).astype(f32)

        def lp1(c, dw):
            st = pl.multiple_of(c * CONV_RT, CONV_RT)
            taps = [xs_ref[pl.ds(st + HALO - 3 + i, CONV_RT), :] for i in range(4)]
            pre = taps[0] * w[0:1, :]
            for i in range(1, 4):
                pre = pre + taps[i] * w[i:i + 1, :]
            sg = _sigmoid(pre)
            s = pre * sg
            d_out = jnp.concatenate([d_ref[0, c * CONV_NC + i, 0] for i in range(CONV_NC)], axis=0)
            rn = lax.rsqrt(jnp.sum(s * s, axis=-1, keepdims=True) + 1e-6)
            n = s * rn
            dn = d_out * scale
            ds_qk = rn * (dn - n * jnp.sum(dn * n, axis=-1, keepdims=True))
            ds = jnp.where(is_qk, ds_qk, d_out)
            dpre = ds * (sg * (1.0 + pre * (1.0 - sg)))
            dp_ref[pl.ds(st, CONV_RT), :] = dpre
            return tuple(dw[i] + jnp.sum(dpre * taps[i], axis=0, keepdims=True) for i in range(4))

        z = jnp.zeros((1, LANE), f32)
        dw = lax.fori_loop(0, S // CONV_RT, lp1, (z, z, z, z))
        for i in range(4):
            dw_ref[pl.ds(i, 1), :] = dw[i]

        def lp2(c, carry):
            st = pl.multiple_of(c * CONV_RT, CONV_RT)
            dx = dp_ref[pl.ds(st, CONV_RT), :] * w[3:4, :]
            for i in range(3):
                dx = dx + dp_ref[pl.ds(st + 3 - i, CONV_RT), :] * w[i:i + 1, :]
            dx_ref[pl.ds(st, CONV_RT), :] = dx.astype(bf16)
            return carry

        lax.fori_loop(0, S // CONV_RT, lp2, 0)

    col = lambda rows: pl.BlockSpec((rows, LANE), lambda t, h: (0, t * GDN_HEADS + h))
    return pl.pallas_call(
        body, name="gdn_prep_bwd", grid=(3, GDN_HEADS),
        in_specs=[col(S), col(4), pl.BlockSpec((1, S // CHUNK, 1, CHUNK, GDN_HD), lambda t, h: (t, 0, h, 0, 0))],
        out_specs=[col(S), col(4)],
        out_shape=[SDS((S, 3 * GDN_W), bf16), SDS((4, 3 * GDN_W), f32)],
        scratch_shapes=[pltpu.VMEM((S + HALO, LANE), f32), pltpu.VMEM((S + HALO, LANE), f32)],
        compiler_params=_cp(2, VMEM_LIMIT),
    )(proj, conv_w, dqkv)


def _bdot(a, b, prec=None):
    return lax.dot_general(a, b, (((2,), (1,)), ((0,), (0,))), precision=prec, preferred_element_type=f32)


def _bdot_nt(a, b, prec=None):
    return lax.dot_general(a, b, (((2,), (2,)), ((0,), (0,))), precision=prec, preferred_element_type=f32)


def _bdot_tn(a, b, prec=None):
    return lax.dot_general(a, b, (((1,), (1,)), ((0,), (0,))), precision=prec, preferred_element_type=f32)


@jax.custom_vjp
def _tri_inv_saved(a, t):
    return t


def _tri_inv_saved_fwd(a, t):
    return t, t


def _tri_inv_saved_bwd(t, dt):
    return -_bdot_tn(t, _bdot_nt(dt, t, HIGH), HIGH), jnp.zeros_like(t)


_tri_inv_saved.defvjp(_tri_inv_saved_fwd, _tri_inv_saved_bwd)


def _gdn_intra(q, k, v, beta, g, t_saved=None):
    nb = q.shape[0]
    c = CHUNK
    ii = lax.broadcasted_iota(jnp.int32, (c, c), 0)
    jj = lax.broadcasted_iota(jnp.int32, (c, c), 1)
    eye = ii == jj
    tril = ii >= jj
    strict = ii > jj
    ones = jnp.ones((nb, c, c), f32)
    eye_f = eye.astype(f32)

    g_row = _bdot(ones, jnp.where(eye, g, 0.0), HI)
    gc = jnp.sum(jnp.where(tril, g_row, 0.0), axis=2, keepdims=True)
    gc_row = _bdot(ones, jnp.where(eye, gc, 0.0), HI)
    decay = jnp.where(tril, jnp.exp(jnp.where(tril, gc - gc_row, 0.0)), 0.0)
    last = lax.broadcasted_iota(jnp.int32, (c, 1), 0) == c - 1
    gc_last = jnp.sum(jnp.where(last, gc, 0.0), axis=1, keepdims=True)
    e_gc = jnp.exp(gc)

    kb = k * beta
    k16 = k.astype(bf16)
    a = jnp.where(strict, _bdot_nt(kb.astype(bf16), k16) * decay, 0.0)
    if t_saved is None:
        xp = -a
        t_inv = eye_f + xp
        for _ in range(5):
            xp = _bdot(xp, xp, HIGH)
            t_inv = _bdot(t_inv, eye_f + xp, HIGH)
    else:
        t_inv = _tri_inv_saved(a, t_saved)
    t16 = t_inv.astype(bf16)
    u = _bdot(t16, (v * beta).astype(bf16))
    w = _bdot(t16, (kb * e_gc).astype(bf16))
    attn = jnp.where(tril, _bdot_nt(q.astype(bf16), k16) * decay, 0.0)
    gam = jnp.broadcast_to(jnp.exp(gc_last), (nb, 1, GDN_HD))
    return u, w, attn, q * e_gc, k * jnp.exp(gc_last - gc), gam, t_inv


GDN_TB = 256
GDN_NC = GDN_TB // CHUNK
GDN_NU = GDN_NC * GDN_HEADS


def _gdn_gates(pg_ref, al_ref, db_ref):
    lane1 = lax.broadcasted_iota(jnp.int32, (1, LANE), 1)
    a_lane = jnp.zeros((1, LANE), f32)
    b_lane = jnp.zeros((1, LANE), f32)
    for h in range(GDN_HEADS):
        a_lane = jnp.where(lane1 == GDN_HEADS + h, al_ref[0, h], a_lane)
        b_lane = jnp.where(lane1 == GDN_HEADS + h, db_ref[0, h], b_lane)
    pg = pg_ref[...]
    z = pg + b_lane
    return _sigmoid(pg), -jnp.exp(a_lane) * _softplus(z), z, a_lane


def _gdn_unit_inputs(qkv_ref, beta_all, g_all):
    units = [(cl, h) for cl in range(GDN_NC) for h in range(GDN_HEADS)]
    beta = jnp.stack([beta_all[cl * CHUNK:(cl + 1) * CHUNK, h:h + 1] for cl, h in units])
    g = jnp.stack([g_all[cl * CHUNK:(cl + 1) * CHUNK, GDN_HEADS + h:GDN_HEADS + h + 1] for cl, h in units])
    return qkv_ref[0], qkv_ref[1], qkv_ref[2], beta, g


def _unit_spec(*tail):
    nd = len(tail)
    return pl.BlockSpec((GDN_NU,) + tail, lambda i: (i,) + (0,) * nd)


def _gdn_intra_shapes(S):
    nu = S // CHUNK * GDN_HEADS
    row = SDS((nu, CHUNK, GDN_HD), f32)
    return [row, row, SDS((nu, CHUNK, CHUNK), f32), row, row, SDS((nu, 1, GDN_HD), f32)]


_GDN_INTRA_SPECS = lambda: [_unit_spec(CHUNK, GDN_HD), _unit_spec(CHUNK, GDN_HD), _unit_spec(CHUNK, CHUNK),
                            _unit_spec(CHUNK, GDN_HD), _unit_spec(CHUNK, GDN_HD), _unit_spec(1, GDN_HD)]


def _gdn_intra_fwd(qkv_u, proj, a_log, dt_bias, S):
    def body(qkv_ref, pg_ref, al_ref, db_ref, *outs):
        beta_all, g_all, _, _ = _gdn_gates(pg_ref, al_ref, db_ref)
        res = _gdn_intra(*_gdn_unit_inputs(qkv_ref, beta_all, g_all))
        for o_ref, r in zip(outs, res):
            o_ref[...] = r

    nu = S // CHUNK * GDN_HEADS
    *intra, t_inv = pl.pallas_call(
        body, name="gdn_intra_fwd", grid=(S // GDN_TB,),
        in_specs=[pl.BlockSpec((3, GDN_NU, CHUNK, GDN_HD), lambda i: (0, i, 0, 0)),
                  pl.BlockSpec((GDN_TB, LANE), lambda i: (i, COL_G // LANE)),
                  pl.BlockSpec(memory_space=pltpu.SMEM), pl.BlockSpec(memory_space=pltpu.SMEM)],
        out_specs=_GDN_INTRA_SPECS() + [_unit_spec(CHUNK, CHUNK)],
        out_shape=_gdn_intra_shapes(S) + [SDS((nu, CHUNK, CHUNK), f32)],
        compiler_params=_cp(1, VMEM_LIMIT),
    )(qkv_u, proj, a_log, dt_bias)
    return intra, t_inv


def _gdn_intra_bwd(qkv_u, proj, a_log, dt_bias, t_inv, cots, S):
    def body(qkv_ref, pg_ref, al_ref, db_ref, t_ref, du_ref, dw_ref, da_ref, dqd_ref, dkd_ref, dgm_ref,
             dqkv_ref, dpg_ref, dal_ref, ddb_ref):
        @pl.when(pl.program_id(0) == 0)
        def _():
            dal_ref[...] = jnp.zeros_like(dal_ref)
            ddb_ref[...] = jnp.zeros_like(ddb_ref)

        t_saved = t_ref[...]
        beta_all, g_all, z, a_lane = _gdn_gates(pg_ref, al_ref, db_ref)
        _, vjp = jax.vjp(lambda *a: _gdn_intra(*a, t_saved=t_saved)[:6], *_gdn_unit_inputs(qkv_ref, beta_all, g_all))
        dq, dk, dv, dbeta, dg = vjp((du_ref[...], dw_ref[...], da_ref[...], dqd_ref[...], dkd_ref[...], dgm_ref[...]))
        dqkv_ref[0] = dq
        dqkv_ref[1] = dk
        dqkv_ref[2] = dv
        lane = lax.broadcasted_iota(jnp.int32, (CHUNK, LANE), 1)
        rows = []
        for cl in range(GDN_NC):
            t = jnp.zeros((CHUNK, LANE), f32)
            for h in range(GDN_HEADS):
                b = cl * GDN_HEADS + h
                t = t + jnp.where(lane == h, dbeta[b], 0.0) + jnp.where(lane == GDN_HEADS + h, dg[b], 0.0)
            rows.append(t)
        d_all = jnp.concatenate(rows, axis=0)
        is_beta = lax.broadcasted_iota(jnp.int32, (GDN_TB, LANE), 1) < GDN_HEADS
        dz = d_all * (-jnp.exp(a_lane)) * _sigmoid(z)
        dpg_ref[...] = jnp.where(is_beta, d_all * beta_all * (1.0 - beta_all), dz).astype(bf16)
        dal_ref[...] += jnp.sum(jnp.where(is_beta, 0.0, d_all * g_all), axis=0, keepdims=True)
        ddb_ref[...] += jnp.sum(jnp.where(is_beta, 0.0, dz), axis=0, keepdims=True)

    acc = _const_spec((1, LANE))
    nu = S // CHUNK * GDN_HEADS
    return pl.pallas_call(
        body, name="gdn_intra_bwd", grid=(S // GDN_TB,),
        in_specs=[pl.BlockSpec((3, GDN_NU, CHUNK, GDN_HD), lambda i: (0, i, 0, 0)),
                  pl.BlockSpec((GDN_TB, LANE), lambda i: (i, COL_G // LANE)),
                  pl.BlockSpec(memory_space=pltpu.SMEM), pl.BlockSpec(memory_space=pltpu.SMEM),
                  _unit_spec(CHUNK, CHUNK)] + _GDN_INTRA_SPECS(),
        out_specs=[pl.BlockSpec((3, GDN_NU, CHUNK, GDN_HD), lambda i: (0, i, 0, 0)),
                   pl.BlockSpec((GDN_TB, LANE), lambda i: (i, 0)), acc, acc],
        out_shape=[SDS((3, nu, CHUNK, GDN_HD), f32), SDS((S, LANE), bf16), SDS((1, LANE), f32), SDS((1, LANE), f32)],
        compiler_params=_cp(1, VMEM_LIMIT),
    )(qkv_u, proj, a_log, dt_bias, t_inv, *cots)


def _gdn_scan_fwd(intra, proj, onorm_g, S):
    def body(u_ref, w_ref, at_ref, qd_ref, kd_ref, gm_ref, gate_ref, og_ref, out_ref, st_ref, s_scr):
        @pl.when(pl.program_id(0) == 0)
        def _():
            s_scr[...] = jnp.zeros_like(s_scr)

        og = og_ref[...]
        s = s_scr[...]
        chain = []
        for cl in range(GDN_NC):
            us = slice(cl * GDN_HEADS, (cl + 1) * GDN_HEADS)
            st_ref[us] = s
            s16 = s.astype(bf16)
            vn16 = (u_ref[us] - _bdot(w_ref[us].astype(bf16), s16)).astype(bf16)
            chain.append((us, s16, vn16))
            s = s * gm_ref[us] + _bdot_tn(kd_ref[us].astype(bf16), vn16)
        s_scr[...] = s
        for cl, (us, s16, vn16) in enumerate(chain):
            rows = slice(cl * CHUNK, (cl + 1) * CHUNK)
            o = _bdot(qd_ref[us].astype(bf16), s16) + _bdot(at_ref[us].astype(bf16), vn16)
            for h in range(GDN_HEADS):
                oh = o[h]
                gt = gate_ref[rows, h * GDN_HD:(h + 1) * GDN_HD]
                on = oh * lax.rsqrt(jnp.mean(oh * oh, axis=-1, keepdims=True) + RMS_EPS) * og
                out_ref[rows, h * GDN_HD:(h + 1) * GDN_HD] = on * (gt * _sigmoid(gt))

    nu = S // CHUNK * GDN_HEADS
    return pl.pallas_call(
        body, name="gdn_scan_fwd", grid=(S // GDN_TB,),
        in_specs=_GDN_INTRA_SPECS() + [pl.BlockSpec((GDN_TB, GDN_W), lambda i: (i, 3)), _const_spec((1, GDN_HD))],
        out_specs=[pl.BlockSpec((GDN_TB, GDN_W), lambda i: (i, 0)), _unit_spec(GDN_HD, GDN_HD)],
        out_shape=[SDS((S, GDN_W), f32), SDS((nu, GDN_HD, GDN_HD), f32)],
        scratch_shapes=[pltpu.VMEM((GDN_HEADS, GDN_HD, GDN_HD), f32)],
        compiler_params=_cp(1, VMEM_LIMIT),
    )(*intra, proj, onorm_g)


def _gdn_scan_bwd(intra, states, proj, d_oab, onorm_g, after, S):
    n_steps = S // GDN_TB

    def body(u_ref, w_ref, at_ref, qd_ref, kd_ref, gm_ref, st_ref, gate_ref, do_ref, og_ref, after_ref,
             du_ref, dw_ref, dat_ref, dqd_ref, dkd_ref, dgm_ref, dgate_ref, dog_ref, ds_scr):
        @pl.when(pl.program_id(0) == 0)
        def _():
            ds_scr[...] = jnp.zeros_like(ds_scr)
            dog_ref[...] = jnp.zeros_like(dog_ref)

        og = og_ref[...]
        ii = lax.broadcasted_iota(jnp.int32, (CHUNK, CHUNK), 0)
        jj = lax.broadcasted_iota(jnp.int32, (CHUNK, CHUNK), 1)
        tril = ii >= jj
        dog = jnp.zeros((1, GDN_HD), f32)
        pre = []
        for cl in range(GDN_NC):
            us = slice(cl * GDN_HEADS, (cl + 1) * GDN_HEADS)
            rows = slice(cl * CHUNK, (cl + 1) * CHUNK)
            s016 = st_ref[us].astype(bf16)
            w16 = w_ref[us].astype(bf16)
            qd16 = qd_ref[us].astype(bf16)
            at16 = at_ref[us].astype(bf16)
            vn16 = (u_ref[us] - _bdot(w16, s016)).astype(bf16)
            o = _bdot(qd16, s016) + _bdot(at16, vn16)
            do_h = []
            for h in range(GDN_HEADS):
                oh = o[h]
                lanes = slice(h * GDN_HD, (h + 1) * GDN_HD)
                gt = gate_ref[rows, lanes]
                d_out = do_ref[rows, lanes]
                r = lax.rsqrt(jnp.mean(oh * oh, axis=-1, keepdims=True) + RMS_EPS)
                n = oh * r
                sg = _sigmoid(gt)
                silu = gt * sg
                dog = dog + jnp.sum(d_out * n * silu, axis=0, keepdims=True)
                dgate_ref[rows, lanes] = (d_out * n * og * (sg * (1.0 + gt * (1.0 - sg)))).astype(bf16)
                dn = d_out * og * silu
                do_h.append(r * (dn - n * jnp.mean(dn * n, axis=-1, keepdims=True)))
            do16 = jnp.stack(do_h).astype(bf16)
            pre.append((us, s016, w16, vn16, do16, _bdot_tn(at16, do16), _bdot_tn(qd16, do16)))
        ds = ds_scr[...]
        chain = [None] * GDN_NC
        for cl in reversed(range(GDN_NC)):
            us, s016, w16, vn16, do16, at_do, qd_do = pre[cl]
            ds16 = ds.astype(bf16)
            dvn = at_do + _bdot(kd_ref[us].astype(bf16), ds16)
            dvn16 = dvn.astype(bf16)
            chain[cl] = (ds, ds16, dvn, dvn16)
            ds = qd_do + ds * gm_ref[us] - _bdot_tn(w16, dvn16)
        ds_scr[...] = ds
        for cl in range(GDN_NC):
            us, s016, w16, vn16, do16, _, _ = pre[cl]
            ds_in, ds16, dvn, dvn16 = chain[cl]
            du_ref[us] = dvn
            dw_ref[us] = -_bdot_nt(dvn16, s016)
            dat_ref[us] = jnp.where(tril, _bdot_nt(do16, vn16), 0.0)
            dqd_ref[us] = _bdot_nt(do16, s016)
            dkd_ref[us] = _bdot_nt(vn16, ds16)
            dgm_ref[us] = jnp.sum(st_ref[us] * ds_in, axis=1, keepdims=True)
        dog_ref[...] += dog

    def unit(*tail):
        nd = len(tail)
        return pl.BlockSpec((GDN_NU,) + tail, lambda i: (n_steps - 1 - i,) + (0,) * nd)

    intra_specs = [unit(CHUNK, GDN_HD), unit(CHUNK, GDN_HD), unit(CHUNK, CHUNK), unit(CHUNK, GDN_HD),
                   unit(CHUNK, GDN_HD), unit(1, GDN_HD)]
    tok = lambda c: pl.BlockSpec((GDN_TB, GDN_W), lambda i: (n_steps - 1 - i, c))
    return pl.pallas_call(
        body, name="gdn_scan_bwd", grid=(n_steps,),
        in_specs=intra_specs + [unit(GDN_HD, GDN_HD), tok(3), tok(0), _const_spec((1, GDN_HD)), _ANY],
        out_specs=intra_specs + [tok(0), _const_spec((1, GDN_HD))],
        out_shape=_gdn_intra_shapes(S) + [SDS((S, GDN_W), bf16), SDS((1, GDN_HD), f32)],
        scratch_shapes=[pltpu.VMEM((GDN_HEADS, GDN_HD, GDN_HD), f32)],
        compiler_params=_cp(1, VMEM_LIMIT),
    )(*intra, states, proj, d_oab, onorm_g, after)


SWA_UNROLL = 4


def _swa_tiles(q_ref, k_ref, v_ref, it, d, nb_log2, S):
    nb = 1 << nb_log2
    r = lax.shift_right_logical(it, nb_log2)
    blk = lax.bitwise_and(it, nb - 1)
    qs = blk * (SWA_BLK * d) + r
    ps = jnp.maximum(blk - 1, 0) * (SWA_BLK * d) + r
    if d > 1:
        rows_c, rows_p = pl.ds(qs, SWA_BLK, stride=d), pl.ds(ps, SWA_BLK, stride=d)
    else:
        rows_c, rows_p = pl.ds(pl.multiple_of(qs, SWA_BLK), SWA_BLK), pl.ds(pl.multiple_of(ps, SWA_BLK), SWA_BLK)
    return rows_c, rows_p, blk > 0


def _swa_prev_modes(nb):
    if nb >= SWA_UNROLL:
        return ["load"] + ["reuse"] * (SWA_UNROLL - 1)
    return ["none" if u % nb == 0 else "reuse" for u in range(SWA_UNROLL)]


def _swa_fwd(proj, bt, S):
    scale = SWA_HD ** -0.5

    def body(q_ref, k_ref, v_ref, bt_ref, o_ref, lse0_ref, lse1_ref, m0_scr, m1_scr, a0_scr, a1_scr):
        lane = lax.broadcasted_iota(jnp.int32, (SWA_BLK, LANE), 1)
        h0 = lane < SWA_HD
        qi = lax.broadcasted_iota(jnp.int32, (SWA_BLK, SWA_BLK), 0)
        kj = lax.broadcasted_iota(jnp.int32, (SWA_BLK, SWA_BLK), 1)
        lower = kj <= qi
        ones16 = jnp.ones((LANE, SWA_BLK), bf16)
        m_scrs = (m0_scr, m1_scr)
        a_scrs = (a0_scr, a1_scr)
        for p, (_, d) in reversed(list(enumerate(PATTERNS))):
            nb_log2 = int(math.log2(S // d // SWA_BLK))
            first = p == len(PATTERNS) - 1

            def lp(i, carry, p=p, d=d, nb_log2=nb_log2, first=first):
                heads = [h0, jnp.logical_not(h0)]
                modes = _swa_prev_modes(1 << nb_log2)
                tiles = []
                kc_f = None
                for u in range(SWA_UNROLL):
                    rows_c, rows_p, has_prev = _swa_tiles(q_ref, k_ref, v_ref, i * SWA_UNROLL + u, d, nb_log2, S)
                    kp_f = {"load": lambda: k_ref[rows_p, :], "reuse": lambda: kc_f, "none": lambda: None}[modes[u]]()
                    has_prev = {"load": has_prev, "reuse": True, "none": False}[modes[u]]
                    q = q_ref[rows_c, :]
                    kc_f = k_ref[rows_c, :]
                    kc = kc_f.astype(bf16)
                    logits = []
                    for mh in heads:
                        q_h = jnp.where(mh, q, 0.0)
                        qh = q_h.astype(bf16)
                        if kp_f is None:
                            logits.append((_dot_nt(qh, kc), None, None))
                        else:
                            logits.append((_dot_nt(qh, kc), _dot_nt(qh, kp_f.astype(bf16)), _dot((q_h * kp_f).astype(bf16), ones16)))
                    tiles.append((rows_c, rows_p, has_prev, logits))
                probs = []
                for rows_c, rows_p, has_prev, logits in tiles:
                    per_head = []
                    for h, (s_c, s_p, far) in enumerate(logits):
                        if has_prev is False:
                            s = jnp.where(lower, s_c * scale + bt_ref[p, h], NEG)
                            s_far = None
                        else:
                            s = jnp.where(lower, s_c, s_p) * scale + bt_ref[p, h]
                            s_far = far * scale + bt_ref[p, h, SWA_BLK - 1:SWA_BLK, 0:1]
                            if has_prev is not True:
                                s = jnp.where(jnp.logical_or(lower, has_prev), s, NEG)
                                s_far = jnp.where(has_prev, s_far, NEG)
                        mn = jnp.max(s, axis=1, keepdims=True)
                        if s_far is not None:
                            mn = jnp.maximum(s_far, mn)
                        alpha = None
                        if not first:
                            mo = m_scrs[h][rows_c, :]
                            mn = jnp.maximum(mo, mn)
                            alpha = jnp.exp(mo - mn)
                        mn = jnp.broadcast_to(mn, (SWA_BLK, LANE))
                        pm = jnp.exp(s - mn)
                        per_head.append((mn, alpha, None if s_far is None else jnp.exp(s_far - mn),
                                         jnp.where(lower, pm, 0.0).astype(bf16),
                                         None if s_far is None else jnp.where(lower, 0.0, pm).astype(bf16)))
                    probs.append(per_head)
                acc_old = [None if first else (a0_scr[t[0], :], a1_scr[t[0], :]) for t in tiles]
                done = []
                vc = None
                for u, ((rows_c, rows_p, _, _), per_head, old) in enumerate(zip(tiles, probs, acc_old)):
                    vp = {"load": lambda: v_ref[rows_p, :], "reuse": lambda: vc, "none": lambda: None}[modes[u]]()
                    vc = v_ref[rows_c, :]
                    acc_new = []
                    for h, (mn, alpha, p_far, pc16, pp16) in enumerate(per_head):
                        pv = _dot(pc16, jnp.where(heads[h], vc, 1.0).astype(bf16))
                        if pp16 is not None:
                            vpa = jnp.where(heads[h], vp, 1.0)
                            pv = pv + _dot(pp16, vpa.astype(bf16)) + p_far * vpa
                        acc_new.append(pv if first else alpha * old[h] + pv)
                    done.append((rows_c, per_head[0][0], per_head[1][0], acc_new[0], acc_new[1]))
                for rows_c, m0_new, m1_new, a0_new, a1_new in done:
                    m0_scr[rows_c, :] = m0_new
                    m1_scr[rows_c, :] = m1_new
                    a0_scr[rows_c, :] = a0_new
                    a1_scr[rows_c, :] = a1_new
                return carry

            lax.fori_loop(0, S // SWA_BLK // SWA_UNROLL, lp, 0)

        def fin(c, carry):
            rows = pl.ds(pl.multiple_of(c * SWA_BLK, SWA_BLK), SWA_BLK)
            a0 = a0_scr[rows, :]
            a1 = a1_scr[rows, :]
            l0 = jnp.where(h0, pltpu.roll(a0, SWA_HD, 1), a0)
            l1 = jnp.where(h0, a1, pltpu.roll(a1, SWA_HD, 1))
            o_ref[rows, :] = jnp.where(h0, a0 / l0, a1 / l1)
            lse0_ref[rows, :] = m0_scr[rows, :] + jnp.log(l0)
            lse1_ref[rows, :] = m1_scr[rows, :] + jnp.log(l1)
            return carry

        lax.fori_loop(0, S // SWA_BLK, fin, 0)

    qb = COL_B // LANE
    col = lambda c: pl.BlockSpec((S, LANE), lambda hp, c=c: (0, c + hp))
    return pl.pallas_call(
        body, name="swa_fwd", grid=(4,),
        in_specs=[col(qb), col(qb + 4), col(qb + 8), pl.BlockSpec((3, 2, SWA_BLK, SWA_BLK), lambda hp: (0, hp, 0, 0))],
        out_specs=[col(0), col(0), col(0)],
        out_shape=[SDS((S, SWA_W), f32)] * 3,
        scratch_shapes=[pltpu.VMEM((S, LANE), f32)] * 4,
        compiler_params=_cp(1, VMEM_LIMIT),
    )(proj, proj, proj, bt)


def _swa_bwd(proj, bt, nd, lse0, lse1, d_oab, after, S):
    scale = SWA_HD ** -0.5

    def body(q_ref, k_ref, v_ref, bt_ref, nd_scr, lse0_ref, lse1_ref, do_ref, after_ref, dq_ref, dk_ref, dv_ref, dsb_ref,
             dq_scr, dk_scr, dv_scr):
        lane = lax.broadcasted_iota(jnp.int32, (SWA_BLK, LANE), 1)
        h0 = lane < SWA_HD
        qi = lax.broadcasted_iota(jnp.int32, (SWA_BLK, SWA_BLK), 0)
        kj = lax.broadcasted_iota(jnp.int32, (SWA_BLK, SWA_BLK), 1)
        lower = kj <= qi
        eye = kj == qi
        rel127 = jnp.logical_or(kj == qi + 1, jnp.logical_and(qi == SWA_BLK - 1, kj == 0))
        ones16 = jnp.ones((LANE, SWA_BLK), bf16)
        lse_refs = (lse0_ref, lse1_ref)
        dk_scr[...] = jnp.zeros((S, LANE), f32)
        dv_scr[...] = jnp.zeros((S, LANE), f32)
        dsb_ref[...] = jnp.zeros_like(dsb_ref)

        for p, (_, d) in reversed(list(enumerate(PATTERNS))):
            nb_log2 = int(math.log2(S // d // SWA_BLK))
            first = p == len(PATTERNS) - 1

            def lp(i, carry, p=p, d=d, nb_log2=nb_log2, first=first):
                heads = [h0, jnp.logical_not(h0)]
                modes = _swa_prev_modes(1 << nb_log2)
                tiles = []
                kc_f = vc_f = None
                for u in range(SWA_UNROLL):
                    rows_c, rows_p, has_prev = _swa_tiles(q_ref, k_ref, v_ref, i * SWA_UNROLL + u, d, nb_log2, S)
                    kp_f = {"load": lambda: k_ref[rows_p, :], "reuse": lambda: kc_f, "none": lambda: None}[modes[u]]()
                    vp_f = {"load": lambda: v_ref[rows_p, :], "reuse": lambda: vc_f, "none": lambda: None}[modes[u]]()
                    has_prev = {"load": has_prev, "reuse": True, "none": False}[modes[u]]
                    q = q_ref[rows_c, :]
                    kc_f = k_ref[rows_c, :]
                    vc_f = v_ref[rows_c, :]
                    kc = kc_f.astype(bf16)
                    kp = None if kp_f is None else kp_f.astype(bf16)
                    do = do_ref[rows_c, :]
                    nd = nd_scr[rows_c, :]
                    per_head = []
                    for mh in heads:
                        q_h = jnp.where(mh, q, 0.0)
                        do_a = jnp.where(mh, do, nd)
                        qh = q_h.astype(bf16)
                        doa = do_a.astype(bf16)
                        doh = jnp.where(mh, do, 0.0).astype(bf16)
                        dd_c = _dot_nt(doa, jnp.where(mh, vc_f, 1.0).astype(bf16))
                        if kp_f is None:
                            per_head.append((qh, doh, _dot_nt(qh, kc), None, None, dd_c, None, None))
                        else:
                            vpa = jnp.where(mh, vp_f, 1.0)
                            per_head.append((qh, doh, _dot_nt(qh, kc), _dot_nt(qh, kp), _dot((q_h * kp_f).astype(bf16), ones16),
                                             dd_c, _dot_nt(doa, vpa.astype(bf16)), _dot((do_a * vpa).astype(bf16), ones16)))
                    tiles.append((rows_c, rows_p, has_prev, kc, kp, per_head))
                grads = []
                for rows_c, rows_p, has_prev, kc, kp, per_head in tiles:
                    out = []
                    for h, (qh, doh, s_c, s_p, far, dd_c, dd_p, dd_far) in enumerate(per_head):
                        lse_h = lse_refs[h][rows_c, :]
                        if has_prev is False:
                            pm = jnp.exp(jnp.where(lower, s_c * scale + bt_ref[p, h], NEG) - lse_h)
                            dsm = pm * dd_c
                            out.append((dsm, dsm.astype(bf16), None, pm.astype(bf16), None))
                            continue
                        s = jnp.where(lower, s_c, s_p) * scale + bt_ref[p, h]
                        s_far = far * scale + bt_ref[p, h, SWA_BLK - 1:SWA_BLK, 0:1]
                        if has_prev is not True:
                            s = jnp.where(jnp.logical_or(lower, has_prev), s, NEG)
                            s_far = jnp.where(has_prev, s_far, NEG)
                        pm = jnp.exp(s - lse_h)
                        p_far = jnp.exp(s_far - lse_h)
                        dsm = pm * jnp.where(lower, dd_c, dd_p)
                        ds_far = p_far * dd_far
                        out.append((dsm + jnp.where(rel127, ds_far, 0.0),
                                    jnp.where(lower, dsm, 0.0).astype(bf16),
                                    jnp.where(lower, jnp.where(eye, ds_far, 0.0), dsm).astype(bf16),
                                    jnp.where(lower, pm, 0.0).astype(bf16),
                                    jnp.where(lower, jnp.where(eye, p_far, 0.0), pm).astype(bf16)))
                    grads.append(out)
                done = []
                add = lambda acc, t: t if acc is None else acc + t
                for (rows_c, rows_p, _, kc, kp, per_head), out in zip(tiles, grads):
                    dq_t = dkc_t = dkp_t = dvc_t = dvp_t = None
                    for h, (_, dsc16, dsp16, pc16, pp16) in enumerate(out):
                        qh, doh = per_head[h][0], per_head[h][1]
                        dq_h = _dot(dsc16, kc)
                        dkc_t = add(dkc_t, _dot_tn(dsc16, qh) * scale)
                        dvc_t = add(dvc_t, _dot_tn(pc16, doh))
                        if dsp16 is not None:
                            dq_h = dq_h + _dot(dsp16, kp)
                            dkp_t = add(dkp_t, _dot_tn(dsp16, qh) * scale)
                            dvp_t = add(dvp_t, _dot_tn(pp16, doh))
                        dq_t = add(dq_t, jnp.where(heads[h], dq_h * scale, 0.0))
                    done.append([rows_c, rows_p, dq_t, dkc_t, dkp_t, dvc_t, dvp_t])
                for u in range(1, SWA_UNROLL):
                    if modes[u] == "reuse":
                        done[u - 1][3] = done[u - 1][3] + done[u][4]
                        done[u - 1][5] = done[u - 1][5] + done[u][6]
                for h in range(2):
                    tot = grads[0][h][0]
                    for g in grads[1:]:
                        tot = tot + g[h][0]
                    dsb_ref[0, p, h] += tot
                for u, (rows_c, rows_p, dq_t, dkc_t, dkp_t, dvc_t, dvp_t) in enumerate(done):
                    dq_scr[rows_c, :] = dq_t if first else dq_scr[rows_c, :] + dq_t
                    dk_scr[rows_c, :] = dk_scr[rows_c, :] + dkc_t
                    dv_scr[rows_c, :] = dv_scr[rows_c, :] + dvc_t
                    if modes[u] == "load":
                        dk_scr[rows_p, :] = dk_scr[rows_p, :] + dkp_t
                        dv_scr[rows_p, :] = dv_scr[rows_p, :] + dvp_t
                return carry

            lax.fori_loop(0, S // SWA_BLK // SWA_UNROLL, lp, 0)
        dq_ref[...] = dq_scr[...].astype(bf16)
        dk_ref[...] = dk_scr[...].astype(bf16)
        dv_ref[...] = dv_scr[...].astype(bf16)

    qb = COL_B // LANE
    col = lambda c: pl.BlockSpec((S, LANE), lambda hp, c=c: (0, c + hp))
    return pl.pallas_call(
        body, name="swa_bwd", grid=(4,),
        in_specs=[col(qb), col(qb + 4), col(qb + 8),
                  pl.BlockSpec((3, 2, SWA_BLK, SWA_BLK), lambda hp: (0, hp, 0, 0)),
                  col(0), col(0), col(0), col(4), _ANY],
        out_specs=[col(0), col(0), col(0),
                   pl.BlockSpec((1, 3, 2, SWA_BLK, SWA_BLK), lambda hp: (hp, 0, 0, 0, 0))],
        out_shape=[SDS((S, SWA_W), bf16)] * 3 + [SDS((4, 3, 2, SWA_BLK, SWA_BLK), f32)],
        scratch_shapes=[pltpu.VMEM((S, LANE), f32)] * 3,
        compiler_params=_cp(1, VMEM_LIMIT),
    )(proj, proj, proj, bt, nd, lse0, lse1, d_oab, after)


def _mix_fwd(oa, ob, w_out, x, g_post, S):
    TS = 512

    def body(oa_ref, ob_ref, w_ref, x_ref, g_ref, mix_ref, x1_ref):
        mix = _dot(oa_ref[...].astype(bf16), w_ref[0:GDN_W, :]) + _dot(ob_ref[...].astype(bf16), w_ref[GDN_W:D_MODEL, :])
        r = lax.rsqrt(jnp.mean(mix * mix, axis=-1, keepdims=True) + RMS_EPS)
        mix_ref[...] = mix
        x1_ref[...] = x_ref[...] + mix * r * g_ref[...]

    row = lambda w: pl.BlockSpec((TS, w), lambda i: (i, 0))
    return pl.pallas_call(
        body, name="mix_fwd", grid=(S // TS,),
        in_specs=[row(GDN_W), row(SWA_W), _resident_spec((D_MODEL, D_MODEL)), row(D_MODEL), _const_spec((1, D_MODEL))],
        out_specs=[row(D_MODEL), row(D_MODEL)],
        out_shape=[SDS((S, D_MODEL), f32), SDS((S, D_MODEL), f32)],
        compiler_params=_cp(1, VMEM_LIMIT),
    )(oa, ob, w_out, x, g_post)


def _mix_bwd(dx1, mix, g_post, w_out, ob, S):
    TS = 512

    def body(dx1_ref, mix_ref, g_ref, w_ref, ob_ref, dmix_ref, doab_ref, dg_ref, nd_ref):
        @pl.when(pl.program_id(0) == 0)
        def _():
            dg_ref[...] = jnp.zeros_like(dg_ref)

        mix = mix_ref[...]
        dz = dx1_ref[...]
        r = lax.rsqrt(jnp.mean(mix * mix, axis=-1, keepdims=True) + RMS_EPS)
        n = mix * r
        dg_ref[...] += jnp.sum(dz * n, axis=0, keepdims=True)
        dn = dz * g_ref[...]
        dmix = (r * (dn - n * jnp.mean(dn * n, axis=-1, keepdims=True))).astype(bf16)
        dmix_ref[...] = dmix
        doab = _dot_nt(dmix, w_ref[...])
        doab_ref[...] = doab
        hi_ = lax.shift_right_logical(lax.broadcasted_iota(jnp.int32, (SWA_W, SWA_W), 0), 6)
        hj_ = lax.shift_right_logical(lax.broadcasted_iota(jnp.int32, (SWA_W, SWA_W), 1), 6)
        swap = (hi_ == lax.bitwise_xor(hj_, 1)).astype(bf16)
        dlt = doab[:, GDN_W:] * ob_ref[...]
        hi = dlt.astype(bf16)
        nd_ref[...] = (_dot(hi, swap) + _dot((dlt - hi.astype(f32)).astype(bf16), swap)) * (-1.0 / SWA_HD)

    row = lambda w=D_MODEL: pl.BlockSpec((TS, w), lambda i: (i, 0))
    return pl.pallas_call(
        body, name="mix_bwd", grid=(S // TS,),
        in_specs=[row(), row(), _const_spec((1, D_MODEL)), _resident_spec((D_MODEL, D_MODEL)), row(SWA_W)],
        out_specs=[row(), row(), _const_spec((1, D_MODEL)), row(SWA_W)],
        out_shape=[SDS((S, D_MODEL), bf16), SDS((S, D_MODEL), f32), SDS((1, D_MODEL), f32), SDS((S, SWA_W), f32)],
        compiler_params=_cp(1, VMEM_LIMIT),
    )(dx1, mix, g_post, w_out, ob)


FFN_TS = 256
FFN_CH = 1408


def _ffn(x1, tgt, g_pre, g_post, wg, wu, wd, S):
    def body(x1_ref, t_ref, gp_ref, gq_ref, wg_ref, wu_ref, wd_ref,
             dx1_ref, h2_ref, act_ref, dgate_ref, dup_ref, df_ref, loss_ref, dgp_ref, dgq_ref, gate_scr, up_scr):
        @pl.when(pl.program_id(0) == 0)
        def _():
            loss_ref[...] = jnp.zeros_like(loss_ref)
            dgp_ref[...] = jnp.zeros_like(dgp_ref)
            dgq_ref[...] = jnp.zeros_like(dgq_ref)

        x1v = x1_ref[...]
        gp = gp_ref[...]
        gq = gq_ref[...]
        r2 = lax.rsqrt(jnp.mean(x1v * x1v, axis=-1, keepdims=True) + RMS_EPS)
        n2 = x1v * r2
        h2 = (n2 * gp).astype(bf16)
        h2_ref[...] = h2
        chunks = [slice(c * FFN_CH, (c + 1) * FFN_CH) for c in range(D_FF // FFN_CH)]
        for cs in chunks:
            gate_scr[:, cs] = _dot_nt(h2, wg_ref[cs, :])
            up_scr[:, cs] = _dot_nt(h2, wu_ref[cs, :])
        acts = []
        for cs in chunks:
            gate = gate_scr[:, cs]
            act = (gate * _sigmoid(gate) * up_scr[:, cs]).astype(bf16)
            act_ref[:, cs] = act
            acts.append(act)
        f = _dot(acts[0], wd_ref[chunks[0], :])
        for act, cs in zip(acts[1:], chunks[1:]):
            f = f + _dot(act, wd_ref[cs, :])
        r3 = lax.rsqrt(jnp.mean(f * f, axis=-1, keepdims=True) + RMS_EPS)
        n3 = f * r3
        err = x1v + n3 * gq - t_ref[...]
        loss_ref[...] += 0.5 * jnp.sum(jnp.mean(err * err, axis=-1, keepdims=True), axis=0, keepdims=True)
        dy = err * (1.0 / D_MODEL)
        dgq_ref[...] += jnp.sum(dy * n3, axis=0, keepdims=True)
        dn3 = dy * gq
        df = (r3 * (dn3 - n3 * jnp.mean(dn3 * n3, axis=-1, keepdims=True))).astype(bf16)
        df_ref[...] = df
        dacts = [_dot_nt(df, wd_ref[cs, :]) for cs in chunks]
        dgs = []
        for dact, cs in zip(dacts, chunks):
            gate = gate_scr[:, cs]
            sg = _sigmoid(gate)
            dup = (dact * gate * sg).astype(bf16)
            dgate = (dact * up_scr[:, cs] * (sg * (1.0 + gate * (1.0 - sg)))).astype(bf16)
            dup_ref[:, cs] = dup
            dgate_ref[:, cs] = dgate
            dgs.append((dgate, dup))
        dh2 = None
        for (dgate, dup), cs in zip(dgs, chunks):
            t = _dot(dgate, wg_ref[cs, :]) + _dot(dup, wu_ref[cs, :])
            dh2 = t if dh2 is None else dh2 + t
        dgp_ref[...] += jnp.sum(dh2 * n2, axis=0, keepdims=True)
        dn2 = dh2 * gp
        dx1_ref[...] = dy + r2 * (dn2 - n2 * jnp.mean(dn2 * n2, axis=-1, keepdims=True))

    row = lambda w: pl.BlockSpec((FFN_TS, w), lambda i: (i, 0))
    vec = _const_spec((1, D_MODEL))
    return pl.pallas_call(
        body, name="ffn_fwd_bwd", grid=(S // FFN_TS,),
        in_specs=[row(D_MODEL), row(D_MODEL), vec, vec, _resident_spec((D_FF, D_MODEL)), _resident_spec((D_FF, D_MODEL)),
                  _resident_spec((D_FF, D_MODEL))],
        out_specs=[row(D_MODEL), row(D_MODEL), row(D_FF), row(D_FF), row(D_FF), row(D_MODEL), _const_spec((1, LANE)), vec, vec],
        out_shape=[SDS((S, D_MODEL), f32), SDS((S, D_MODEL), bf16), SDS((S, D_FF), bf16), SDS((S, D_FF), bf16),
                   SDS((S, D_FF), bf16), SDS((S, D_MODEL), bf16), SDS((1, LANE), f32), SDS((1, D_MODEL), f32),
                   SDS((1, D_MODEL), f32)],
        scratch_shapes=[pltpu.VMEM((FFN_TS, D_FF), f32), pltpu.VMEM((FFN_TS, D_FF), f32)],
        compiler_params=_cp(1, VMEM_LIMIT),
    )(x1, tgt, g_pre, g_post, wg, wu, wd)


def _proj_bwd(x, dx1, g_pre, wcat, segs, after, S):
    TS = 256
    n = len(segs)
    cols = [(c0, a.shape[1]) for a, c0 in segs]

    def body(*refs):
        x_ref, dx1_ref, g_ref, w_ref = refs[:4]
        seg_refs = refs[4:4 + n]
        gx_ref, dg_ref = refs[5 + n:]

        @pl.when(pl.program_id(0) == 0)
        def _():
            dg_ref[...] = jnp.zeros_like(dg_ref)

        dh = jnp.zeros((TS, D_MODEL), f32)
        for s_ref, (c0, w) in zip(seg_refs, cols):
            dh = dh + _dot_nt(s_ref[...], w_ref[:, c0:c0 + w])
        xv = x_ref[...]
        g = g_ref[...]
        r = lax.rsqrt(jnp.mean(xv * xv, axis=-1, keepdims=True) + RMS_EPS)
        nx = xv * r
        dg_ref[...] += jnp.sum(dh * nx, axis=0, keepdims=True)
        dn = dh * g
        gx_ref[...] = dx1_ref[...] + r * (dn - nx * jnp.mean(dn * nx, axis=-1, keepdims=True))

    row = lambda w: pl.BlockSpec((TS, w), lambda i: (i, 0))
    return pl.pallas_call(
        body, name="proj_bwd", grid=(S // TS,),
        in_specs=[row(D_MODEL), row(D_MODEL), _const_spec((1, D_MODEL)), _resident_spec((D_MODEL, NCOL))]
                 + [row(w) for _, w in cols] + [_ANY],
        out_specs=[row(D_MODEL), _const_spec((1, D_MODEL))],
        out_shape=[SDS((S, D_MODEL), f32), SDS((1, D_MODEL), f32)],
        compiler_params=_cp(1, VMEM_LIMIT),
    )(x, dx1, g_pre, wcat, *[a for a, _ in segs], after)


def _wgrad(a, b, S, name):
    TS = 1024
    K = a.shape[1]
    N = b.shape[1]
    TN = next(t for t in (512, 1408, N) if N % t == 0)

    def body(a_ref, b_ref, o_ref, acc):
        @pl.when(pl.program_id(1) == 0)
        def _():
            acc[...] = jnp.zeros_like(acc)

        acc[...] += _dot_tn(a_ref[...].astype(bf16), b_ref[...])

        @pl.when(pl.program_id(1) == pl.num_programs(1) - 1)
        def _():
            o_ref[...] = acc[...].astype(bf16)

    return pl.pallas_call(
        body, name=name, grid=(N // TN, S // TS),
        in_specs=[pl.BlockSpec((TS, K), lambda j, s: (s, 0)), pl.BlockSpec((TS, TN), lambda j, s: (s, j))],
        out_specs=pl.BlockSpec((K, TN), lambda j, s: (0, j)), out_shape=SDS((K, N), bf16),
        scratch_shapes=[pltpu.VMEM((K, TN), f32)],
        compiler_params=_cp(2, VMEM_LIMIT),
    )(a, b)


def _w_in_pieces():
    n_a, n_g = 4 * GDN_W, 2 * GDN_HEADS
    cb = IN_COLS // N_DEV
    bounds = [(0, n_a, COL_A), (n_a, n_a + n_g, COL_G), (n_a + n_g, IN_COLS, COL_B)]
    out = []
    for j in range(N_DEV):
        lo, hi = j * cb, (j + 1) * cb
        for s0, s1, dst in bounds:
            a, b = max(lo, s0), min(hi, s1)
            if a < b:
                out.append((j, a - lo, b - a, dst + a - s0))
    return out


def _wcat_from_blocks(g_in):
    TR = 256
    cb = IN_COLS // N_DEV
    pieces = _w_in_pieces()

    def body(w_ref, o_ref):
        o_ref[:, COL_G:NCOL] = jnp.zeros((TR, NCOL - COL_G), bf16)
        for j, off, w, dst in pieces:
            o_ref[:, dst:dst + w] = w_ref[j, :, off:off + w]

    return pl.pallas_call(
        body, name="wcat_from_blocks", grid=(D_MODEL // TR,),
        in_specs=[pl.BlockSpec((N_DEV, TR, cb), lambda i: (0, i, 0))],
        out_specs=pl.BlockSpec((TR, NCOL), lambda i: (i, 0)),
        out_shape=SDS((D_MODEL, NCOL), bf16),
        compiler_params=_cp(1, VMEM_LIMIT),
    )(g_in)


def _wgrad_in(h1, segs, S):
    TS = 1024
    n = len(segs)
    cols = [(c0, a.shape[1]) for a, c0 in segs]
    cb = IN_COLS // N_DEV
    pieces = _w_in_pieces()

    def body(*refs):
        h_ref = refs[0]
        seg_refs = refs[1:1 + n]
        o_ref, acc = refs[1 + n], refs[2 + n]

        @pl.when(pl.program_id(0) == 0)
        def _():
            acc[...] = jnp.zeros_like(acc)

        h = h_ref[...]
        for s_ref, (c0, w) in zip(seg_refs, cols):
            acc[:, c0:c0 + w] += _dot_tn(h, s_ref[...])

        @pl.when(pl.program_id(0) == pl.num_programs(0) - 1)
        def _():
            for j, off, w, src in pieces:
                o_ref[j, :, off:off + w] = acc[:, src:src + w].astype(bf16)

    row = lambda w: pl.BlockSpec((TS, w), lambda i: (i, 0))
    return pl.pallas_call(
        body, name="wgrad_in", grid=(S // TS,),
        in_specs=[row(D_MODEL)] + [row(w) for _, w in cols],
        out_specs=_const_spec((N_DEV, D_MODEL, cb)),
        out_shape=SDS((N_DEV, D_MODEL, cb), bf16),
        scratch_shapes=[pltpu.VMEM((D_MODEL, NCOL), f32)],
        compiler_params=_cp(1, VMEM_LIMIT),
    )(h1, *[a for a, _ in segs])


def _adamw(recv, src, me, w, m, v, name):
    R, C = w.shape
    TR = 256 if R % 256 == 0 else R
    c1 = 1.0 / (1.0 - ADAM_B1 ** ADAM_STEP)
    c2 = 1.0 / (1.0 - ADAM_B2 ** ADAM_STEP)

    def body(me_ref, r_ref, own_ref, w_ref, m_ref, v_ref, g_out, d_out, m_out, v_out):
        g = None
        for s in range(N_DEV):
            t = jnp.where(me_ref[0] == s, own_ref[0], r_ref[s]).astype(f32)
            g = t if g is None else g + t
        mn = ADAM_B1 * m_ref[...] + (1.0 - ADAM_B1) * g
        vn = ADAM_B2 * v_ref[...] + (1.0 - ADAM_B2) * (g * g)
        g_out[...] = g
        m_out[...] = mn
        v_out[...] = vn
        d_out[...] = -ADAM_LR * ((mn * c1) / (jnp.sqrt(vn * c2) + ADAM_EPS) + ADAM_WD * w_ref[...])

    blk = pl.BlockSpec((TR, C), lambda i, me_ref: (i, 0))
    return pl.pallas_call(
        body, name=name,
        grid_spec=pltpu.PrefetchScalarGridSpec(
            num_scalar_prefetch=1, grid=(R // TR,),
            in_specs=[pl.BlockSpec((N_DEV, TR, C), lambda i, me_ref: (0, i, 0)),
                      pl.BlockSpec((1, TR, C), lambda i, me_ref: (me_ref[0], i, 0)), blk, blk, blk],
            out_specs=[blk, blk, blk, blk]),
        out_shape=[SDS((R, C), f32)] * 4,
        compiler_params=_cp(1, VMEM_LIMIT),
    )(me, recv, src, w, m, v)


MESH = pl.DeviceIdType.MESH
_ANY = pl.BlockSpec(memory_space=pl.ANY)


def _flip(v, d):
    return 1 - v if d else v


def _all_gather(shards):
    n = len(shards)

    def body(*refs):
        ins = refs[:n]
        outs = refs[n:2 * n]
        send_sems, recv_sems, local_sems = refs[2 * n:]
        x, y, c = lax.axis_index("x"), lax.axis_index("y"), lax.axis_index("c")
        me, sibling = (x, y, c), (x, y, 1 - c)
        chips = [(1 - x, y), (x, 1 - y), (1 - x, 1 - y)]

        def slot(px, py, pc):
            return 4 * px + 2 * py + pc

        def copy(a, k, block, to, src=None):
            dst = outs[a].at[slot(*block)]
            return pltpu.make_async_remote_copy(src_ref=dst if src is None else src, dst_ref=dst,
                                                send_sem=send_sems.at[a, k], recv_sem=recv_sems.at[a, k],
                                                device_id=to, device_id_type=MESH)

        mine, first, passed = [], [], []
        for a in range(n):
            cp = pltpu.make_async_copy(ins[a], outs[a].at[slot(*me)], local_sems.at[a])
            cp.start()
            mine.append(cp)
            fs = [copy(a, 0, me, sibling, src=ins[a])]
            fs += [copy(a, 1 + j, me, (*chip, c), src=ins[a]) for j, chip in enumerate(chips)]
            for cp in fs:
                cp.start()
            first += fs
        for j, chip in enumerate(chips):
            for a in range(n):
                copy(a, 1 + j, (*chip, c), me).wait_recv()
                cp = copy(a, 4 + j, (*chip, c), sibling)
                cp.start()
                passed.append(cp)
        for a in range(n):
            copy(a, 0, sibling, me).wait_recv()
            for j, chip in enumerate(chips):
                copy(a, 4 + j, (*chip, 1 - c), me).wait_recv()
        for cp in first + passed:
            cp.wait_send()
        for cp in mine:
            cp.wait()

    return pl.pallas_call(
        body, name="weight_all_gather",
        in_specs=[_ANY] * n, out_specs=[_ANY] * n,
        out_shape=[SDS((N_DEV,) + s.shape, s.dtype) for s in shards],
        scratch_shapes=[pltpu.SemaphoreType.DMA((n, 7)), pltpu.SemaphoreType.DMA((n, 7)), pltpu.SemaphoreType.DMA((n,))],
        compiler_params=pltpu.CompilerParams(has_side_effects=True),
    )(*shards)


def _grad_exchange(blocked, whole):
    arrs = list(blocked) + list(whole)
    n, nb = len(arrs), len(blocked)
    rel = [(dx, dy, dc) for dx in (0, 1) for dy in (0, 1) for dc in (0, 1) if dx or dy or dc]

    def body(*refs):
        ins = refs[:n]
        outs = refs[n:2 * n]
        send_sems, recv_sems, local_sems = refs[2 * n:]
        x, y, c = lax.axis_index("x"), lax.axis_index("y"), lax.axis_index("c")
        me = 4 * x + 2 * y + c
        sends, locs = [], []
        for a in range(n):
            cp = pltpu.make_async_copy(ins[a].at[me] if a < nb else ins[a], outs[a].at[me], local_sems.at[a])
            cp.start()
            locs.append(cp)
            for k, (dx, dy, dc) in enumerate(rel):
                peer = (_flip(x, dx), _flip(y, dy), _flip(c, dc))
                pidx = 4 * peer[0] + 2 * peer[1] + peer[2]
                cp = pltpu.make_async_remote_copy(src_ref=ins[a].at[pidx] if a < nb else ins[a], dst_ref=outs[a].at[me],
                                                  send_sem=send_sems.at[a, k], recv_sem=recv_sems.at[a, k],
                                                  device_id=peer, device_id_type=MESH)
                cp.start()
                sends.append(cp)
        for a in range(n):
            for k, (dx, dy, dc) in enumerate(rel):
                peer = (_flip(x, dx), _flip(y, dy), _flip(c, dc))
                pidx = 4 * peer[0] + 2 * peer[1] + peer[2]
                pltpu.make_async_remote_copy(src_ref=outs[a].at[pidx], dst_ref=outs[a].at[pidx],
                                             send_sem=send_sems.at[a, k], recv_sem=recv_sems.at[a, k],
                                             device_id=peer, device_id_type=MESH).wait_recv()
        for cp in sends:
            cp.wait_send()
        for cp in locs:
            cp.wait()

    shapes = [SDS(a.shape, a.dtype) for a in blocked] + [SDS((N_DEV,) + a.shape, a.dtype) for a in whole]
    return pl.pallas_call(
        body, name="grad_exchange",
        in_specs=[_ANY] * n, out_specs=[_ANY] * n, out_shape=shapes,
        scratch_shapes=[pltpu.SemaphoreType.DMA((n, 7)), pltpu.SemaphoreType.DMA((n, 7)), pltpu.SemaphoreType.DMA((n,))],
        compiler_params=pltpu.CompilerParams(has_side_effects=True),
    )(*arrs)


_HBM = pl.BlockSpec(memory_space=pltpu.HBM)
_SEM = pl.BlockSpec(memory_space=pltpu.SEMAPHORE)
_REL = [(dx, dy, dc) for dx in (0, 1) for dy in (0, 1) for dc in (0, 1) if dx or dy or dc]


N_PEER = len(_REL)
_EFFECT = pltpu.SideEffectType.DATAFLOW_SIDE_EFFECTING


def _peer_copies(srcs, lands, send_sems, recv_sems, blocked, as_receiver):
    x, y, c = lax.axis_index("x"), lax.axis_index("y"), lax.axis_index("c")
    me = 4 * x + 2 * y + c
    cps = []
    for a in range(len(srcs)):
        for k, (dx, dy, dc) in enumerate(_REL):
            peer = (_flip(x, dx), _flip(y, dy), _flip(c, dc))
            pidx = 4 * peer[0] + 2 * peer[1] + peer[2]
            cps.append(pltpu.make_async_remote_copy(
                src_ref=srcs[a].at[pidx] if blocked else srcs[a], dst_ref=lands[a].at[pidx if as_receiver else me],
                send_sem=send_sems[a * N_PEER + k], recv_sem=recv_sems[a * N_PEER + k],
                device_id=peer, device_id_type=MESH))
    return cps


def _exchange_start(srcs, after, blocked, name):
    n = len(srcs)
    ns = n * N_PEER
    lands = [lax.empty(s.shape if blocked else (N_DEV,) + s.shape, s.dtype) for s in srcs]

    def body(*refs):
        ins, lnd = refs[:n], refs[n:2 * n]
        outs = refs[2 * n + 1:]
        for cp in _peer_copies(ins, lnd, outs[:ns], outs[ns:2 * ns], blocked, False):
            cp.start()
        outs[-1][...] = jnp.zeros_like(outs[-1])

    res = pl.pallas_call(
        body, name=name,
        in_specs=[_HBM] * (2 * n) + [_ANY],
        out_specs=[_SEM] * (2 * ns) + [_HBM] * (2 * n) + [pl.BlockSpec(memory_space=pltpu.VMEM)],
        out_shape=[pltpu.SemaphoreType.DMA(())] * (2 * ns) + [pltpu.HBM(s.shape, s.dtype) for s in srcs]
                  + [pltpu.HBM(l.shape, l.dtype) for l in lands] + [SDS((8, LANE), f32)],
        input_output_aliases={i: 2 * ns + i for i in range(2 * n)},
        compiler_params=pltpu.CompilerParams(has_side_effects=_EFFECT),
    )(*[pltpu.with_memory_space_constraint(s, pltpu.HBM) for s in srcs],
      *[pltpu.with_memory_space_constraint(l, pltpu.HBM) for l in lands], after)
    return list(res[:2 * ns]), list(res[2 * ns:2 * ns + n]), list(res[2 * ns + n:2 * ns + 2 * n]), res[-1]


def _exchange_wait(sems, srcs, lands, after, blocked, name):
    n = len(srcs)
    ns = n * N_PEER

    def body(*refs):
        ins, lnd = refs[:n], refs[n:2 * n]
        sem_refs = refs[2 * n:2 * n + 2 * ns]
        for cp in _peer_copies(ins, lnd, sem_refs[:ns], sem_refs[ns:], blocked, True):
            cp.wait_send()
            cp.wait_recv()

    res = pl.pallas_call(
        body, name=name,
        in_specs=[_HBM] * (2 * n) + [_SEM] * (2 * ns) + [_ANY],
        out_specs=[_HBM] * (2 * n),
        out_shape=[pltpu.HBM(s.shape, s.dtype) for s in srcs] + [pltpu.HBM(l.shape, l.dtype) for l in lands],
        input_output_aliases={i: i for i in range(2 * n)},
        compiler_params=pltpu.CompilerParams(has_side_effects=_EFFECT),
    )(*srcs, *lands, *sems, after)
    return list(res[:n]), list(res[n:])


def _local_step(x, tgt, wcat, convw, late_weights, early_grads, last_grads, token, a_log, dt_bias, onorm_g, rel_bias,
                g_mix_pre, g_mix_post, g_ffn_pre, g_ffn_post):
    S = x.shape[0]
    bk_np = _bucket_tables()
    bk = jnp.asarray(bk_np)
    bt = _bias_tables(rel_bias, bk)
    proj, h1 = _proj_fwd(x, g_mix_pre, wcat, token, S)
    nu = S // CHUNK * GDN_HEADS
    qkv_u = _gdn_prep(proj, convw, S).reshape(3, nu, CHUNK, GDN_HD)
    intra, t_inv = _gdn_intra_fwd(qkv_u, proj, a_log, dt_bias, S)
    oa, states = _gdn_scan_fwd(intra, proj, onorm_g, S)
    ob, lse0, lse1 = _swa_fwd(proj, bt, S)
    wout, ffn_weights = late_weights(ob)
    mix, x1 = _mix_fwd(oa, ob, wout, x, g_mix_post, S)
    wgate, wup, wdown = ffn_weights(x1)
    dx1, h2, act, dgate_f, dup_f, df, loss, d_gfpre, d_gfpost = _ffn(x1, tgt, g_ffn_pre, g_ffn_post, wgate, wup, wdown, S)
    rows8 = lambda g: g.reshape(N_DEV, D_FF // N_DEV, D_MODEL)
    g_gate = rows8(_wgrad(dgate_f, h2, S, "wgrad_gate"))
    g_up = rows8(_wgrad(dup_f, h2, S, "wgrad_up"))
    g_down = rows8(_wgrad(act, df, S, "wgrad_down"))
    dmix, d_oab, d_gmpost, nd = _mix_bwd(dx1, mix, g_mix_post, wout, ob, S)
    g_out = jnp.concatenate([_wgrad(oa, dmix, S, "wgrad_out_a"), _wgrad(ob, dmix, S, "wgrad_out_b")], axis=0)
    token = early_grads(g_out.reshape(N_DEV, D_MODEL // N_DEV, D_MODEL), g_gate, g_up, g_down)
    dqb, dkb, dvb, dsb = _swa_bwd(proj, bt, nd, lse0, lse1, d_oab, token, S)
    *cots, dgate_a, d_og = _gdn_scan_bwd(intra, states, proj, d_oab, onorm_g, token, S)
    dqkv_u, dpg, d_alog, d_dtb = _gdn_intra_bwd(qkv_u, proj, a_log, dt_bias, t_inv, cots, S)
    dqkv_a, d_conv = _gdn_prep_bwd(proj, convw, dqkv_u.reshape(3, S // CHUNK, GDN_HEADS, CHUNK, GDN_HD), S)
    segs = [(dqkv_a, COL_A), (dgate_a, COL_A + 3 * GDN_W), (dqb, COL_B), (dkb, COL_B + SWA_W), (dvb, COL_B + 2 * SWA_W),
            (dpg, COL_G)]
    token = last_grads(_wgrad_in(h1, segs, S), d_conv)
    grad_x, d_gmpre = _proj_bwd(x, dx1, g_mix_pre, wcat, segs, token, S)
    d_rel = _rel_bias_grad(dsb, bk, bk_np)
    small = dict(a_log=d_alog[:, GDN_HEADS:2 * GDN_HEADS], dt_bias=d_dtb[:, GDN_HEADS:2 * GDN_HEADS], onorm_g=d_og, rel_bias=d_rel,
                 g_mix_pre=d_gmpre, g_mix_post=d_gmpost, g_ffn_pre=d_gfpre, g_ffn_post=d_gfpost)
    return loss, grad_x, small


SMALL = ("a_log", "dt_bias", "onorm_g", "rel_bias", "g_mix_pre", "g_mix_post", "g_ffn_pre", "g_ffn_post")
PACK_ROWS = 8


def _pack_small(d, loss=None):
    rest = jnp.concatenate([d["onorm_g"].reshape(-1), d["a_log"].reshape(-1), d["dt_bias"].reshape(-1),
                            d["rel_bias"].reshape(-1)])
    rest = jnp.concatenate([rest, jnp.zeros((D_MODEL - rest.shape[0],), f32)])
    extra = jnp.zeros((D_MODEL,), f32) if loss is None else jnp.concatenate([loss.reshape(1), jnp.zeros((D_MODEL - 1,), f32)])
    rows = [d["g_mix_pre"].reshape(-1), d["g_mix_post"].reshape(-1), d["g_ffn_pre"].reshape(-1),
            d["g_ffn_post"].reshape(-1), rest, extra]
    return jnp.concatenate([jnp.stack(rows), jnp.zeros((PACK_ROWS - len(rows), D_MODEL), f32)], axis=0)


def _unpack_small(p):
    o = GDN_HD
    return dict(g_mix_pre=p[0:1], g_mix_post=p[1:2], g_ffn_pre=p[2:3], g_ffn_post=p[3:4],
                onorm_g=p[4:5, :o], a_log=p[4:5, o:o + 4], dt_bias=p[4:5, o + 4:o + 8],
                rel_bias=p[4, o + 8:o + 8 + NUM_BUCKETS * SWA_HEADS].reshape(NUM_BUCKETS, SWA_HEADS))


def kernel(x, w_in, conv_w, a_log, dt_bias, onorm_g, rel_bias, w_out, g_mix_pre, g_mix_post, w_gate, w_up, w_down, g_ffn_pre, g_ffn_post, loss_target, m_w_in, m_conv_w, m_a_log, m_dt_bias, m_onorm_g, m_rel_bias, m_w_out, m_g_mix_pre, m_g_mix_post, m_w_gate, m_w_up, m_w_down, m_g_ffn_pre, m_g_ffn_post, v_w_in, v_conv_w, v_a_log, v_dt_bias, v_onorm_g, v_rel_bias, v_w_out, v_g_mix_pre, v_g_mix_post, v_w_gate, v_w_up, v_w_down, v_g_ffn_pre, v_g_ffn_post):
    big = ("w_in", "conv_w", "w_out", "w_gate", "w_up", "w_down")
    transposed = ("w_gate", "w_up")
    tr = lambda k, a: a.T if k in transposed else a
    w_sh = {k: tr(k, a[0]) for k, a in dict(w_in=w_in, conv_w=conv_w, w_out=w_out, w_gate=w_gate, w_up=w_up, w_down=w_down).items()}
    m_sh = {k: tr(k, a[0]) for k, a in dict(w_in=m_w_in, conv_w=m_conv_w, w_out=m_w_out, w_gate=m_w_gate, w_up=m_w_up,
                                             w_down=m_w_down).items()}
    v_sh = {k: tr(k, a[0]) for k, a in dict(w_in=v_w_in, conv_w=v_conv_w, w_out=v_w_out, w_gate=v_w_gate, w_up=v_w_up,
                                             w_down=v_w_down).items()}
    w_small = dict(a_log=a_log, dt_bias=dt_bias, onorm_g=onorm_g, rel_bias=rel_bias, g_mix_pre=g_mix_pre,
                   g_mix_post=g_mix_post, g_ffn_pre=g_ffn_pre, g_ffn_post=g_ffn_post)
    m_small = dict(a_log=m_a_log, dt_bias=m_dt_bias, onorm_g=m_onorm_g, rel_bias=m_rel_bias, g_mix_pre=m_g_mix_pre,
                   g_mix_post=m_g_mix_post, g_ffn_pre=m_g_ffn_pre, g_ffn_post=m_g_ffn_post)
    v_small = dict(a_log=v_a_log, dt_bias=v_dt_bias, onorm_g=v_onorm_g, rel_bias=v_rel_bias, g_mix_pre=v_g_mix_pre,
                   g_mix_post=v_g_mix_post, g_ffn_pre=v_g_ffn_pre, g_ffn_post=v_g_ffn_post)

    me = 4 * lax.axis_index("x") + 2 * lax.axis_index("y") + lax.axis_index("c")
    me1 = me.reshape(1).astype(jnp.int32)
    own = lambda full, part: lax.dynamic_update_index_in_dim(full, part, me, 0)
    cols = lambda g: g.reshape(g.shape[0], N_DEV, g.shape[1] // N_DEV).transpose(1, 0, 2)
    late = ("w_out", "w_gate", "w_up", "w_down")

    late_src = [w_sh[k].astype(bf16) for k in late]
    g_in, g_conv = _all_gather([w_sh["w_in"].astype(bf16), w_sh["conv_w"]])
    g_sems, g_src, g_land, g_token = _exchange_start(late_src, g_conv, False, "late_weights_start")
    wcat = _wcat_from_blocks(g_in)
    convw = g_conv.transpose(1, 0, 2).reshape(4, 3 * GDN_W)

    def late_weights(after):
        pick = lambda idx: [g_sems[half * len(late) * N_PEER + a * N_PEER + k] for half in (0, 1) for a in idx for k in range(N_PEER)]
        (s_out,), (l_out,) = _exchange_wait(pick([0]), g_src[:1], g_land[:1], after, False, "w_out_wait")

        def ffn_weights(after2):
            srcs, lands = _exchange_wait(pick([1, 2, 3]), g_src[1:], g_land[1:], after2, False, "ffn_weights_wait")
            return [own(l, s).reshape(D_FF, D_MODEL) for l, s in zip(lands, srcs)]

        return own(l_out, s_out).reshape(D_MODEL, D_MODEL), ffn_weights

    early, last = {}, {}

    def early_grads(*blocks):
        early["sems"], early["src"], early["land"], token = _exchange_start(list(blocks), me1, True, "late_grads_start")
        return token

    def last_grads(gw_in, gw_conv):
        src = [gw_in, cols(gw_conv)]
        last["sems"], last["src"], last["land"], token = _exchange_start(src, me1, True, "last_grads_start")
        return token

    loss_p, grad_x, gsmall = _local_step(
        x[0], loss_target[0], wcat, convw, late_weights, early_grads, last_grads, g_token,
        a_log, dt_bias, onorm_g, rel_bias, g_mix_pre, g_mix_post, g_ffn_pre, g_ffn_post)

    (r_small,) = _grad_exchange([], [_pack_small(gsmall, loss_p[0, 0])])
    outs = {}
    for names, ex, after, name in ((late, early, grad_x, "late_grads_wait"), (("w_in", "conv_w"), last, r_small, "last_grads_wait")):
        srcs, lands = _exchange_wait(ex["sems"], ex["src"], ex["land"], after, True, name)
        for k, l, s in zip(names, lands, srcs):
            outs[k] = _adamw(l, s, me1, w_sh[k], m_sh[k], v_sh[k], "adamw_" + k)
    sm = _adamw(r_small, r_small, me1, _pack_small(w_small), _pack_small(m_small), _pack_small(v_small), "adamw_small")
    loss = sm[0][5, 0]
    sm = [_unpack_small(t) for t in sm]
    for k in SMALL:
        outs[k] = tuple(t[k].reshape(w_small[k].shape) for t in sm)

    order = ("w_in", "conv_w", "a_log", "dt_bias", "onorm_g", "rel_bias", "w_out", "g_mix_pre", "g_mix_post", "w_gate",
             "w_up", "w_down", "g_ffn_pre", "g_ffn_post")
    lead = lambda k, t: tr(k, t)[None] if k in big else t
    res = [loss, grad_x[None]]
    for i in range(4):
        res += [lead(k, outs[k][i]) for k in order]
    return tuple(res)
```

```python
import functools
import math

import numpy as np
import jax
import jax.numpy as jnp
from jax import lax
from jax.experimental import pallas as pl
from jax.experimental.pallas import tpu as pltpu

f32 = jnp.float32
bf16 = jnp.bfloat16
SDS = jax.ShapeDtypeStruct

D_MODEL = 1024
GDN_HEADS = 4
GDN_HD = 128
GDN_W = 512
CHUNK = 64
SWA_HEADS = 8
SWA_HD = 64
SWA_W = 512
D_FF = 2816
IN_COLS = 3592
PATTERNS = ((128, 1), (512, 4), (2048, 16))
SWA_BLK = 128
NUM_BUCKETS = 32
MAX_DISTANCE = 2048
RMS_EPS = 1e-6
NEG = -1e30
N_DEV = 8

COL_A = 0
COL_B = 2048
COL_G = 3584
NCOL = 3712
LANE = 128

ADAM_LR, ADAM_B1, ADAM_B2, ADAM_EPS, ADAM_WD, ADAM_STEP = 0.001, 0.9, 0.999, 1e-08, 0.01, 10

VMEM_LIMIT = 56 * 1024 * 1024

HI = lax.Precision.HIGHEST
HIGH = lax.Precision.HIGH


def _cp(n_grid=0, vmem=None):
    kw = {}
    if n_grid:
        kw["dimension_semantics"] = ("arbitrary",) * n_grid
    if vmem:
        kw["vmem_limit_bytes"] = vmem
    return pltpu.CompilerParams(**kw)


def _dot(a, b):
    return jnp.dot(a, b, preferred_element_type=f32)


def _dot_nt(a, b):
    return lax.dot_general(a, b, (((1,), (1,)), ((), ())), preferred_element_type=f32)


def _dot_tn(a, b):
    return lax.dot_general(a, b, (((0,), (0,)), ((), ())), preferred_element_type=f32)


def _dot_hi(a, b):
    return jnp.dot(a, b, precision=HI, preferred_element_type=f32)


def _sigmoid(x):
    return 0.5 * jnp.tanh(0.5 * x) + 0.5


def _softplus(x):
    return jnp.maximum(x, 0.0) + jnp.log(1.0 + jnp.exp(-jnp.abs(x)))


def _const_spec(shape):
    nd = len(shape)
    return pl.BlockSpec(shape, lambda *_: (0,) * nd)


def _resident_spec(shape):
    nd = len(shape)
    return pl.BlockSpec(shape, lambda *_: (0,) * nd, pipeline_mode=pl.Buffered(1))


def _t5_bucket_np(dist):
    max_exact = NUM_BUCKETS // 2
    d = np.maximum(dist, 1).astype(np.float32)
    log_b = max_exact + (np.log(d / np.float32(max_exact)) / np.float32(math.log(MAX_DISTANCE / max_exact))
                         * np.float32(NUM_BUCKETS - max_exact)).astype(np.int32)
    return np.where(dist < max_exact, dist, np.minimum(log_b, NUM_BUCKETS - 1)).astype(np.int32)


def _bucket_tables():
    w = SWA_BLK
    qi = np.arange(w)[:, None]
    kj = np.arange(w)[None, :]
    rel = np.where(kj <= qi, qi - kj, qi + w - kj)
    out = np.zeros((len(PATTERNS), w, w), np.int32)
    for p, (_, dil) in enumerate(PATTERNS):
        steps = _t5_bucket_np(np.arange(w + 1) * dil)
        assert steps[w] == steps[w - 1]
        out[p] = steps[rel]
    return out


def _bias_tables(rel_bias, bk):
    def body(rb_ref, bk_ref, o_ref):
        b_idx = bk_ref[0]
        for h in range(SWA_HEADS):
            def lp(b, acc):
                return jnp.where(b_idx == b, rb_ref[b, h], acc)
            o_ref[0, h] = lax.fori_loop(0, NUM_BUCKETS, lp, jnp.zeros((SWA_BLK, SWA_BLK), f32))

    return pl.pallas_call(
        body, name="bias_tables", grid=(3,),
        in_specs=[pl.BlockSpec(memory_space=pltpu.SMEM), pl.BlockSpec((1, SWA_BLK, SWA_BLK), lambda p: (p, 0, 0))],
        out_specs=pl.BlockSpec((1, SWA_HEADS, SWA_BLK, SWA_BLK), lambda p: (p, 0, 0, 0)),
        out_shape=SDS((3, SWA_HEADS, SWA_BLK, SWA_BLK), f32),
        compiler_params=_cp(1),
    )(rel_bias, bk)


def _rel_bias_grad(dsb, bk, bk_np):
    present = [sorted(set(int(v) for v in np.unique(bk_np[p]))) for p in range(3)]

    def body(ds_ref, bk_ref, o_ref):
        row = lax.broadcasted_iota(jnp.int32, (NUM_BUCKETS, LANE), 0)
        col = lax.broadcasted_iota(jnp.int32, (NUM_BUCKETS, SWA_HEADS), 1)
        out = jnp.zeros((NUM_BUCKETS, SWA_HEADS), f32)
        for hp in range(4):
            for hh in range(2):
                acc = jnp.zeros((NUM_BUCKETS, LANE), f32)
                for p in range(3):
                    tile = ds_ref[hp, p, hh]
                    b_idx = bk_ref[p]
                    for b in present[p]:
                        part = jnp.sum(jnp.where(b_idx == b, tile, 0.0), axis=0, keepdims=True)
                        acc = acc + jnp.where(row == b, part, 0.0)
                tot = jnp.sum(acc, axis=1, keepdims=True)
                out = out + jnp.where(col == 2 * hp + hh, tot, 0.0)
        o_ref[...] = out

    return pl.pallas_call(body, name="rel_bias_grad", out_shape=SDS((NUM_BUCKETS, SWA_HEADS), f32),
                          compiler_params=_cp(0, 32 * 1024 * 1024))(dsb, bk)


def _proj_fwd(x, g_pre, wcat, after, S):
    TS = 512

    def body(x_ref, g_ref, w_ref, after_ref, o_ref, h_ref):
        xv = x_ref[...]
        r = lax.rsqrt(jnp.mean(xv * xv, axis=-1, keepdims=True) + RMS_EPS)
        h = (xv * r * g_ref[...]).astype(bf16)
        h_ref[...] = h
        o_ref[...] = _dot(h, w_ref[...])

    return pl.pallas_call(
        body, name="proj_fwd", grid=(S // TS,),
        in_specs=[pl.BlockSpec((TS, D_MODEL), lambda i: (i, 0)), _const_spec((1, D_MODEL)),
                  _resident_spec((D_MODEL, NCOL)), _ANY],
        out_specs=[pl.BlockSpec((TS, NCOL), lambda i: (i, 0)), pl.BlockSpec((TS, D_MODEL), lambda i: (i, 0))],
        out_shape=[SDS((S, NCOL), f32), SDS((S, D_MODEL), bf16)],
        compiler_params=_cp(1, VMEM_LIMIT),
    )(x, g_pre, wcat, after)


CONV_RT = 256
HALO = 8


CONV_NC = CONV_RT // CHUNK


def _gdn_prep(proj, conv_w, S):
    def body(p_ref, cw_ref, o_ref, xs_ref):
        t = pl.program_id(0)
        xs_ref[pl.ds(0, HALO), :] = jnp.zeros((HALO, LANE), f32)
        xs_ref[pl.ds(HALO, S), :] = p_ref[...]
        w = cw_ref[...]
        is_qk = t < 2
        scale = jnp.where(t == 0, GDN_HD ** -0.5, 1.0).astype(f32)

        def lp(c, carry):
            st = pl.multiple_of(c * CONV_RT, CONV_RT)
            pre = xs_ref[pl.ds(st + HALO - 3, CONV_RT), :] * w[0:1, :]
            for i in range(1, 4):
                pre = pre + xs_ref[pl.ds(st + HALO - 3 + i, CONV_RT), :] * w[i:i + 1, :]
            s = pre * _sigmoid(pre)
            nrm = s * lax.rsqrt(jnp.sum(s * s, axis=-1, keepdims=True) + 1e-6) * scale
            out = jnp.where(is_qk, nrm, s)
            for i in range(CONV_NC):
                o_ref[0, c * CONV_NC + i, 0] = out[i * CHUNK:(i + 1) * CHUNK]
            return carry

        lax.fori_loop(0, S // CONV_RT, lp, 0)

    return pl.pallas_call(
        body, name="gdn_prep", grid=(3, GDN_HEADS),
        in_specs=[pl.BlockSpec((S, LANE), lambda t, h: (0, t * GDN_HEADS + h)),
                  pl.BlockSpec((4, LANE), lambda t, h: (0, t * GDN_HEADS + h))],
        out_specs=pl.BlockSpec((1, S // CHUNK, 1, CHUNK, GDN_HD), lambda t, h: (t, 0, h, 0, 0)),
        out_shape=SDS((3, S // CHUNK, GDN_HEADS, CHUNK, GDN_HD), f32),
        scratch_shapes=[pltpu.VMEM((S + HALO, LANE), f32)],
        compiler_params=_cp(2, VMEM_LIMIT),
    )(proj, conv_w)


def _gdn_prep_bwd(proj, conv_w, dqkv, S):
    def body(p_ref, cw_ref, d_ref, dx_ref, dw_ref, xs_ref, dp_ref):
        t = pl.program_id(0)
        xs_ref[pl.ds(0, HALO), :] = jnp.zeros((HALO, LANE), f32)
        xs_ref[pl.ds(HALO, S), :] = p_ref[...]
        dp_ref[pl.ds(S, HALO), :] = jnp.zeros((HALO, LANE), f32)
        w = cw_ref[...]
        is_qk = t < 2
        scale = jnp.where(t == 0, GDN_HD ** -0.5, 1.0).astype(f32)

        def lp1(c, dw):
            st = pl.multiple_of(c * CONV_RT, CONV_RT)
            taps = [xs_ref[pl.ds(st + HALO - 3 + i, CONV_RT), :] for i in range(4)]
            pre = taps[0] * w[0:1, :]
            for i in range(1, 4):
                pre = pre + taps[i] * w[i:i + 1, :]
            sg = _sigmoid(pre)
            s = pre * sg
            d_out = jnp.concatenate([d_ref[0, c * CONV_NC + i, 0] for i in range(CONV_NC)], axis=0)
            rn = lax.rsqrt(jnp.sum(s * s, axis=-1, keepdims=True) + 1e-6)
            n = s * rn
            dn = d_out * scale
            ds_qk = rn * (dn - n * jnp.sum(dn * n, axis=-1, keepdims=True))
            ds = jnp.where(is_qk, ds_qk, d_out)
            dpre = ds * (sg * (1.0 + pre * (1.0 - sg)))
            dp_ref[pl.ds(st, CONV_RT), :] = dpre
            return tuple(dw[i] + jnp.sum(dpre * taps[i], axis=0, keepdims=True) for i in range(4))

        z = jnp.zeros((1, LANE), f32)
        dw = lax.fori_loop(0, S // CONV_RT, lp1, (z, z, z, z))
        for i in range(4):
            dw_ref[pl.ds(i, 1), :] = dw[i]

        def lp2(c, carry):
            st = pl.multiple_of(c * CONV_RT, CONV_RT)
            dx = dp_ref[pl.ds(st, CONV_RT), :] * w[3:4, :]
            for i in range(3):
                dx = dx + dp_ref[pl.ds(st + 3 - i, CONV_RT), :] * w[i:i + 1, :]
            dx_ref[pl.ds(st, CONV_RT), :] = dx.astype(bf16)
            return carry

        lax.fori_loop(0, S // CONV_RT, lp2, 0)

    col = lambda rows: pl.BlockSpec((rows, LANE), lambda t, h: (0, t * GDN_HEADS + h))
    return pl.pallas_call(
        body, name="gdn_prep_bwd", grid=(3, GDN_HEADS),
        in_specs=[col(S), col(4), pl.BlockSpec((1, S // CHUNK, 1, CHUNK, GDN_HD), lambda t, h: (t, 0, h, 0, 0))],
        out_specs=[col(S), col(4)],
        out_shape=[SDS((S, 3 * GDN_W), bf16), SDS((4, 3 * GDN_W), f32)],
        scratch_shapes=[pltpu.VMEM((S + HALO, LANE), f32), pltpu.VMEM((S + HALO, LANE), f32)],
        compiler_params=_cp(2, VMEM_LIMIT),
    )(proj, conv_w, dqkv)


def _bdot(a, b, prec=None):
    return lax.dot_general(a, b, (((2,), (1,)), ((0,), (0,))), precision=prec, preferred_element_type=f32)


def _bdot_nt(a, b, prec=None):
    return lax.dot_general(a, b, (((2,), (2,)), ((0,), (0,))), precision=prec, preferred_element_type=f32)


def _bdot_tn(a, b, prec=None):
    return lax.dot_general(a, b, (((1,), (1,)), ((0,), (0,))), precision=prec, preferred_element_type=f32)


@jax.custom_vjp
def _tri_inv_saved(a, t):
    return t


def _tri_inv_saved_fwd(a, t):
    return t, t


def _tri_inv_saved_bwd(t, dt):
    return -_bdot_tn(t, _bdot_nt(dt, t, HIGH), HIGH), jnp.zeros_like(t)


_tri_inv_saved.defvjp(_tri_inv_saved_fwd, _tri_inv_saved_bwd)


def _gdn_intra(q, k, v, beta, g, t_saved=None):
    nb = q.shape[0]
    c = CHUNK
    ii = lax.broadcasted_iota(jnp.int32, (c, c), 0)
    jj = lax.broadcasted_iota(jnp.int32, (c, c), 1)
    eye = ii == jj
    tril = ii >= jj
    strict = ii > jj
    ones = jnp.ones((nb, c, c), f32)
    eye_f = eye.astype(f32)

    g_row = _bdot(ones, jnp.where(eye, g, 0.0), HI)
    gc = jnp.sum(jnp.where(tril, g_row, 0.0), axis=2, keepdims=True)
    gc_row = _bdot(ones, jnp.where(eye, gc, 0.0), HI)
    decay = jnp.where(tril, jnp.exp(jnp.where(tril, gc - gc_row, 0.0)), 0.0)
    last = lax.broadcasted_iota(jnp.int32, (c, 1), 0) == c - 1
    gc_last = jnp.sum(jnp.where(last, gc, 0.0), axis=1, keepdims=True)
    e_gc = jnp.exp(gc)

    kb = k * beta
    k16 = k.astype(bf16)
    a = jnp.where(strict, _bdot_nt(kb.astype(bf16), k16) * decay, 0.0)
    if t_saved is None:
        xp = -a
        t_inv = eye_f + xp
        for level in range(5):
            if level < 2:
                xp = _bdot(xp, xp, HIGH)
                t_inv = t_inv + _bdot(t_inv, xp, HIGH)
            else:
                x16 = xp.astype(bf16)
                xp = _bdot(x16, x16)
                t_inv = t_inv + _bdot(t_inv.astype(bf16), xp.astype(bf16))
    else:
        t_inv = _tri_inv_saved(a, t_saved)
    t16 = t_inv.astype(bf16)
    u = _bdot(t16, (v * beta).astype(bf16))
    w = _bdot(t16, (kb * e_gc).astype(bf16))
    attn = jnp.where(tril, _bdot_nt(q.astype(bf16), k16) * decay, 0.0)
    gam = jnp.broadcast_to(jnp.exp(gc_last), (nb, 1, GDN_HD))
    return u, w, attn, q * e_gc, k * jnp.exp(gc_last - gc), gam, t_inv


GDN_TB = 256
GDN_NC = GDN_TB // CHUNK
GDN_NU = GDN_NC * GDN_HEADS


def _gdn_gates(pg_ref, al_ref, db_ref):
    lane1 = lax.broadcasted_iota(jnp.int32, (1, LANE), 1)
    a_lane = jnp.zeros((1, LANE), f32)
    b_lane = jnp.zeros((1, LANE), f32)
    for h in range(GDN_HEADS):
        a_lane = jnp.where(lane1 == GDN_HEADS + h, al_ref[0, h], a_lane)
        b_lane = jnp.where(lane1 == GDN_HEADS + h, db_ref[0, h], b_lane)
    pg = pg_ref[...]
    z = pg + b_lane
    return _sigmoid(pg), -jnp.exp(a_lane) * _softplus(z), z, a_lane


def _gdn_unit_inputs(qkv_ref, beta_all, g_all):
    units = [(cl, h) for cl in range(GDN_NC) for h in range(GDN_HEADS)]
    beta = jnp.stack([beta_all[cl * CHUNK:(cl + 1) * CHUNK, h:h + 1] for cl, h in units])
    g = jnp.stack([g_all[cl * CHUNK:(cl + 1) * CHUNK, GDN_HEADS + h:GDN_HEADS + h + 1] for cl, h in units])
    return qkv_ref[0], qkv_ref[1], qkv_ref[2], beta, g


def _unit_spec(*tail):
    nd = len(tail)
    return pl.BlockSpec((GDN_NU,) + tail, lambda i: (i,) + (0,) * nd)


def _gdn_intra_shapes(S):
    nu = S // CHUNK * GDN_HEADS
    row = SDS((nu, CHUNK, GDN_HD), f32)
    return [row, row, SDS((nu, CHUNK, CHUNK), f32), row, row, SDS((nu, 1, GDN_HD), f32)]


_GDN_INTRA_SPECS = lambda: [_unit_spec(CHUNK, GDN_HD), _unit_spec(CHUNK, GDN_HD), _unit_spec(CHUNK, CHUNK),
                            _unit_spec(CHUNK, GDN_HD), _unit_spec(CHUNK, GDN_HD), _unit_spec(1, GDN_HD)]


def _gdn_intra_fwd(qkv_u, proj, a_log, dt_bias, S):
    def body(qkv_ref, pg_ref, al_ref, db_ref, *outs):
        beta_all, g_all, _, _ = _gdn_gates(pg_ref, al_ref, db_ref)
        res = _gdn_intra(*_gdn_unit_inputs(qkv_ref, beta_all, g_all))
        for o_ref, r in zip(outs, res):
            o_ref[...] = r

    nu = S // CHUNK * GDN_HEADS
    *intra, t_inv = pl.pallas_call(
        body, name="gdn_intra_fwd", grid=(S // GDN_TB,),
        in_specs=[pl.BlockSpec((3, GDN_NU, CHUNK, GDN_HD), lambda i: (0, i, 0, 0)),
                  pl.BlockSpec((GDN_TB, LANE), lambda i: (i, COL_G // LANE)),
                  pl.BlockSpec(memory_space=pltpu.SMEM), pl.BlockSpec(memory_space=pltpu.SMEM)],
        out_specs=_GDN_INTRA_SPECS() + [_unit_spec(CHUNK, CHUNK)],
        out_shape=_gdn_intra_shapes(S) + [SDS((nu, CHUNK, CHUNK), f32)],
        compiler_params=_cp(1, VMEM_LIMIT),
    )(qkv_u, proj, a_log, dt_bias)
    return intra, t_inv


def _gdn_intra_bwd(qkv_u, proj, a_log, dt_bias, t_inv, cots, S):
    def body(qkv_ref, pg_ref, al_ref, db_ref, t_ref, du_ref, dw_ref, da_ref, dqd_ref, dkd_ref, dgm_ref,
             dqkv_ref, dpg_ref, dal_ref, ddb_ref):
        @pl.when(pl.program_id(0) == 0)
        def _():
            dal_ref[...] = jnp.zeros_like(dal_ref)
            ddb_ref[...] = jnp.zeros_like(ddb_ref)

        t_saved = t_ref[...]
        beta_all, g_all, z, a_lane = _gdn_gates(pg_ref, al_ref, db_ref)
        _, vjp = jax.vjp(lambda *a: _gdn_intra(*a, t_saved=t_saved)[:6], *_gdn_unit_inputs(qkv_ref, beta_all, g_all))
        dq, dk, dv, dbeta, dg = vjp((du_ref[...], dw_ref[...], da_ref[...], dqd_ref[...], dkd_ref[...], dgm_ref[...]))
        dqkv_ref[0] = dq
        dqkv_ref[1] = dk
        dqkv_ref[2] = dv
        lane = lax.broadcasted_iota(jnp.int32, (CHUNK, LANE), 1)
        rows = []
        for cl in range(GDN_NC):
            t = jnp.zeros((CHUNK, LANE), f32)
            for h in range(GDN_HEADS):
                b = cl * GDN_HEADS + h
                t = t + jnp.where(lane == h, dbeta[b], 0.0) + jnp.where(lane == GDN_HEADS + h, dg[b], 0.0)
            rows.append(t)
        d_all = jnp.concatenate(rows, axis=0)
        is_beta = lax.broadcasted_iota(jnp.int32, (GDN_TB, LANE), 1) < GDN_HEADS
        dz = d_all * (-jnp.exp(a_lane)) * _sigmoid(z)
        dpg_ref[...] = jnp.where(is_beta, d_all * beta_all * (1.0 - beta_all), dz).astype(bf16)
        dal_ref[...] += jnp.sum(jnp.where(is_beta, 0.0, d_all * g_all), axis=0, keepdims=True)
        ddb_ref[...] += jnp.sum(jnp.where(is_beta, 0.0, dz), axis=0, keepdims=True)

    acc = _const_spec((1, LANE))
    nu = S // CHUNK * GDN_HEADS
    return pl.pallas_call(
        body, name="gdn_intra_bwd", grid=(S // GDN_TB,),
        in_specs=[pl.BlockSpec((3, GDN_NU, CHUNK, GDN_HD), lambda i: (0, i, 0, 0)),
                  pl.BlockSpec((GDN_TB, LANE), lambda i: (i, COL_G // LANE)),
                  pl.BlockSpec(memory_space=pltpu.SMEM), pl.BlockSpec(memory_space=pltpu.SMEM),
                  _unit_spec(CHUNK, CHUNK)] + _GDN_INTRA_SPECS(),
        out_specs=[pl.BlockSpec((3, GDN_NU, CHUNK, GDN_HD), lambda i: (0, i, 0, 0)),
                   pl.BlockSpec((GDN_TB, LANE), lambda i: (i, 0)), acc, acc],
        out_shape=[SDS((3, nu, CHUNK, GDN_HD), f32), SDS((S, LANE), bf16), SDS((1, LANE), f32), SDS((1, LANE), f32)],
        compiler_params=_cp(1, VMEM_LIMIT),
    )(qkv_u, proj, a_log, dt_bias, t_inv, *cots)


def _gdn_scan_fwd(intra, proj, onorm_g, S):
    def body(u_ref, w_ref, at_ref, qd_ref, kd_ref, gm_ref, gate_ref, og_ref, out_ref, st_ref, s_scr):
        @pl.when(pl.program_id(0) == 0)
        def _():
            s_scr[...] = jnp.zeros_like(s_scr)

        og = og_ref[...]
        s = s_scr[...]
        chain = []
        for cl in range(GDN_NC):
            us = slice(cl * GDN_HEADS, (cl + 1) * GDN_HEADS)
            st_ref[us] = s
            s16 = s.astype(bf16)
            vn16 = (u_ref[us] - _bdot(w_ref[us].astype(bf16), s16)).astype(bf16)
            chain.append((us, s16, vn16))
            s = s * gm_ref[us] + _bdot_tn(kd_ref[us].astype(bf16), vn16)
        s_scr[...] = s
        for cl, (us, s16, vn16) in enumerate(chain):
            rows = slice(cl * CHUNK, (cl + 1) * CHUNK)
            o = _bdot(qd_ref[us].astype(bf16), s16) + _bdot(at_ref[us].astype(bf16), vn16)
            for h in range(GDN_HEADS):
                oh = o[h]
                gt = gate_ref[rows, h * GDN_HD:(h + 1) * GDN_HD]
                on = oh * lax.rsqrt(jnp.mean(oh * oh, axis=-1, keepdims=True) + RMS_EPS) * og
                out_ref[rows, h * GDN_HD:(h + 1) * GDN_HD] = on * (gt * _sigmoid(gt))

    nu = S // CHUNK * GDN_HEADS
    return pl.pallas_call(
        body, name="gdn_scan_fwd", grid=(S // GDN_TB,),
        in_specs=_GDN_INTRA_SPECS() + [pl.BlockSpec((GDN_TB, GDN_W), lambda i: (i, 3)), _const_spec((1, GDN_HD))],
        out_specs=[pl.BlockSpec((GDN_TB, GDN_W), lambda i: (i, 0)), _unit_spec(GDN_HD, GDN_HD)],
        out_shape=[SDS((S, GDN_W), f32), SDS((nu, GDN_HD, GDN_HD), f32)],
        scratch_shapes=[pltpu.VMEM((GDN_HEADS, GDN_HD, GDN_HD), f32)],
        compiler_params=_cp(1, VMEM_LIMIT),
    )(*intra, proj, onorm_g)


def _gdn_scan_bwd(intra, states, proj, d_oab, onorm_g, after, S):
    n_steps = S // GDN_TB

    def body(u_ref, w_ref, at_ref, qd_ref, kd_ref, gm_ref, st_ref, gate_ref, do_ref, og_ref, after_ref,
             du_ref, dw_ref, dat_ref, dqd_ref, dkd_ref, dgm_ref, dgate_ref, dog_ref, ds_scr):
        @pl.when(pl.program_id(0) == 0)
        def _():
            ds_scr[...] = jnp.zeros_like(ds_scr)
            dog_ref[...] = jnp.zeros_like(dog_ref)

        og = og_ref[...]
        ii = lax.broadcasted_iota(jnp.int32, (CHUNK, CHUNK), 0)
        jj = lax.broadcasted_iota(jnp.int32, (CHUNK, CHUNK), 1)
        tril = ii >= jj
        dog = jnp.zeros((1, GDN_HD), f32)
        pre = []
        for cl in range(GDN_NC):
            us = slice(cl * GDN_HEADS, (cl + 1) * GDN_HEADS)
            rows = slice(cl * CHUNK, (cl + 1) * CHUNK)
            s016 = st_ref[us].astype(bf16)
            w16 = w_ref[us].astype(bf16)
            qd16 = qd_ref[us].astype(bf16)
            at16 = at_ref[us].astype(bf16)
            vn16 = (u_ref[us] - _bdot(w16, s016)).astype(bf16)
            o = _bdot(qd16, s016) + _bdot(at16, vn16)
            do_h = []
            for h in range(GDN_HEADS):
                oh = o[h]
                lanes = slice(h * GDN_HD, (h + 1) * GDN_HD)
                gt = gate_ref[rows, lanes]
                d_out = do_ref[rows, lanes]
                r = lax.rsqrt(jnp.mean(oh * oh, axis=-1, keepdims=True) + RMS_EPS)
                n = oh * r
                sg = _sigmoid(gt)
                silu = gt * sg
                dog = dog + jnp.sum(d_out * n * silu, axis=0, keepdims=True)
                dgate_ref[rows, lanes] = (d_out * n * og * (sg * (1.0 + gt * (1.0 - sg)))).astype(bf16)
                dn = d_out * og * silu
                do_h.append(r * (dn - n * jnp.mean(dn * n, axis=-1, keepdims=True)))
            do16 = jnp.stack(do_h).astype(bf16)
            pre.append((us, s016, w16, vn16, do16, _bdot_tn(at16, do16), _bdot_tn(qd16, do16)))
        ds = ds_scr[...]
        chain = [None] * GDN_NC
        for cl in reversed(range(GDN_NC)):
            us, s016, w16, vn16, do16, at_do, qd_do = pre[cl]
            ds16 = ds.astype(bf16)
            dvn = at_do + _bdot(kd_ref[us].astype(bf16), ds16)
            dvn16 = dvn.astype(bf16)
            chain[cl] = (ds, ds16, dvn, dvn16)
            ds = qd_do + ds * gm_ref[us] - _bdot_tn(w16, dvn16)
        ds_scr[...] = ds
        for cl in range(GDN_NC):
            us, s016, w16, vn16, do16, _, _ = pre[cl]
            ds_in, ds16, dvn, dvn16 = chain[cl]
            du_ref[us] = dvn
            dw_ref[us] = -_bdot_nt(dvn16, s016)
            dat_ref[us] = jnp.where(tril, _bdot_nt(do16, vn16), 0.0)
            dqd_ref[us] = _bdot_nt(do16, s016)
            dkd_ref[us] = _bdot_nt(vn16, ds16)
            dgm_ref[us] = jnp.sum(st_ref[us] * ds_in, axis=1, keepdims=True)
        dog_ref[...] += dog

    def unit(*tail):
        nd = len(tail)
        return pl.BlockSpec((GDN_NU,) + tail, lambda i: (n_steps - 1 - i,) + (0,) * nd)

    intra_specs = [unit(CHUNK, GDN_HD), unit(CHUNK, GDN_HD), unit(CHUNK, CHUNK), unit(CHUNK, GDN_HD),
                   unit(CHUNK, GDN_HD), unit(1, GDN_HD)]
    tok = lambda c: pl.BlockSpec((GDN_TB, GDN_W), lambda i: (n_steps - 1 - i, c))
    return pl.pallas_call(
        body, name="gdn_scan_bwd", grid=(n_steps,),
        in_specs=intra_specs + [unit(GDN_HD, GDN_HD), tok(3), tok(0), _const_spec((1, GDN_HD)), _ANY],
        out_specs=intra_specs + [tok(0), _const_spec((1, GDN_HD))],
        out_shape=_gdn_intra_shapes(S) + [SDS((S, GDN_W), bf16), SDS((1, GDN_HD), f32)],
        scratch_shapes=[pltpu.VMEM((GDN_HEADS, GDN_HD, GDN_HD), f32)],
        compiler_params=_cp(1, VMEM_LIMIT),
    )(*intra, states, proj, d_oab, onorm_g, after)


SWA_UNROLL = 4


def _swa_tiles(q_ref, k_ref, v_ref, it, d, nb_log2, S):
    nb = 1 << nb_log2
    r = lax.shift_right_logical(it, nb_log2)
    blk = lax.bitwise_and(it, nb - 1)
    qs = blk * (SWA_BLK * d) + r
    ps = jnp.maximum(blk - 1, 0) * (SWA_BLK * d) + r
    if d > 1:
        rows_c, rows_p = pl.ds(qs, SWA_BLK, stride=d), pl.ds(ps, SWA_BLK, stride=d)
    else:
        rows_c, rows_p = pl.ds(pl.multiple_of(qs, SWA_BLK), SWA_BLK), pl.ds(pl.multiple_of(ps, SWA_BLK), SWA_BLK)
    return rows_c, rows_p, blk > 0


def _swa_prev_modes(nb):
    if nb >= SWA_UNROLL:
        return ["load"] + ["reuse"] * (SWA_UNROLL - 1)
    return ["none" if u % nb == 0 else "reuse" for u in range(SWA_UNROLL)]


def _swa_fwd(proj, bt, S):
    scale = SWA_HD ** -0.5

    def body(q_ref, k_ref, v_ref, bt_ref, o_ref, lse0_ref, lse1_ref, m0_scr, m1_scr, a0_scr, a1_scr):
        lane = lax.broadcasted_iota(jnp.int32, (SWA_BLK, LANE), 1)
        h0 = lane < SWA_HD
        qi = lax.broadcasted_iota(jnp.int32, (SWA_BLK, SWA_BLK), 0)
        kj = lax.broadcasted_iota(jnp.int32, (SWA_BLK, SWA_BLK), 1)
        lower = kj <= qi
        ones16 = jnp.ones((LANE, SWA_BLK), bf16)
        m_scrs = (m0_scr, m1_scr)
        a_scrs = (a0_scr, a1_scr)
        for p, (_, d) in reversed(list(enumerate(PATTERNS))):
            nb_log2 = int(math.log2(S // d // SWA_BLK))
            first = p == len(PATTERNS) - 1

            def lp(i, carry, p=p, d=d, nb_log2=nb_log2, first=first):
                heads = [h0, jnp.logical_not(h0)]
                modes = _swa_prev_modes(1 << nb_log2)
                tiles = []
                kc_f = None
                for u in range(SWA_UNROLL):
                    rows_c, rows_p, has_prev = _swa_tiles(q_ref, k_ref, v_ref, i * SWA_UNROLL + u, d, nb_log2, S)
                    kp_f = {"load": lambda: k_ref[rows_p, :], "reuse": lambda: kc_f, "none": lambda: None}[modes[u]]()
                    has_prev = {"load": has_prev, "reuse": True, "none": False}[modes[u]]
                    q = q_ref[rows_c, :] * scale
                    kc_f = k_ref[rows_c, :]
                    kc = kc_f.astype(bf16)
                    logits = []
                    for mh in heads:
                        q_h = jnp.where(mh, q, 0.0)
                        qh = q_h.astype(bf16)
                        if kp_f is None:
                            logits.append((_dot_nt(qh, kc), None, None))
                        else:
                            logits.append((_dot_nt(qh, kc), _dot_nt(qh, kp_f.astype(bf16)), _dot((q_h * kp_f).astype(bf16), ones16)))
                    tiles.append((rows_c, rows_p, has_prev, logits))
                probs = []
                for rows_c, rows_p, has_prev, logits in tiles:
                    per_head = []
                    for h, (s_c, s_p, far) in enumerate(logits):
                        if has_prev is False:
                            s = jnp.where(lower, s_c + bt_ref[p, h], NEG)
                            s_far = None
                        else:
                            s = jnp.where(lower, s_c, s_p) + bt_ref[p, h]
                            s_far = far + bt_ref[p, h, SWA_BLK - 1:SWA_BLK, 0:1]
                            if has_prev is not True:
                                s = jnp.where(jnp.logical_or(lower, has_prev), s, NEG)
                                s_far = jnp.where(has_prev, s_far, NEG)
                        mn = jnp.max(s, axis=1, keepdims=True)
                        if s_far is not None:
                            mn = jnp.maximum(s_far, mn)
                        alpha = None
                        if not first:
                            mo = m_scrs[h][rows_c, :]
                            mn = jnp.maximum(mo, mn)
                            alpha = jnp.exp(mo - mn)
                        mn = jnp.broadcast_to(mn, (SWA_BLK, LANE))
                        pm = jnp.exp(s - mn)
                        per_head.append((mn, alpha, None if s_far is None else jnp.exp(s_far - mn),
                                         jnp.where(lower, pm, 0.0).astype(bf16),
                                         None if s_far is None else jnp.where(lower, 0.0, pm).astype(bf16)))
                    probs.append(per_head)
                acc_old = [None if first else (a0_scr[t[0], :], a1_scr[t[0], :]) for t in tiles]
                done = []
                vc = None
                for u, ((rows_c, rows_p, _, _), per_head, old) in enumerate(zip(tiles, probs, acc_old)):
                    vp = {"load": lambda: v_ref[rows_p, :], "reuse": lambda: vc, "none": lambda: None}[modes[u]]()
                    vc = v_ref[rows_c, :]
                    acc_new = []
                    for h, (mn, alpha, p_far, pc16, pp16) in enumerate(per_head):
                        pv = _dot(pc16, jnp.where(heads[h], vc, 1.0).astype(bf16))
                        if pp16 is not None:
                            vpa = jnp.where(heads[h], vp, 1.0)
                            pv = pv + _dot(pp16, vpa.astype(bf16)) + p_far * vpa
                        acc_new.append(pv if first else alpha * old[h] + pv)
                    done.append((rows_c, per_head[0][0], per_head[1][0], acc_new[0], acc_new[1]))
                for rows_c, m0_new, m1_new, a0_new, a1_new in done:
                    m0_scr[rows_c, :] = m0_new
                    m1_scr[rows_c, :] = m1_new
                    a0_scr[rows_c, :] = a0_new
                    a1_scr[rows_c, :] = a1_new
                return carry

            lax.fori_loop(0, S // SWA_BLK // SWA_UNROLL, lp, 0)

        def fin(c, carry):
            rows = pl.ds(pl.multiple_of(c * SWA_BLK, SWA_BLK), SWA_BLK)
            a0 = a0_scr[rows, :]
            a1 = a1_scr[rows, :]
            l0 = jnp.where(h0, pltpu.roll(a0, SWA_HD, 1), a0)
            l1 = jnp.where(h0, a1, pltpu.roll(a1, SWA_HD, 1))
            o_ref[rows, :] = jnp.where(h0, a0 / l0, a1 / l1)
            lse0_ref[rows, :] = m0_scr[rows, :] + jnp.log(l0)
            lse1_ref[rows, :] = m1_scr[rows, :] + jnp.log(l1)
            return carry

        lax.fori_loop(0, S // SWA_BLK, fin, 0)

    qb = COL_B // LANE
    col = lambda c: pl.BlockSpec((S, LANE), lambda hp, c=c: (0, c + hp))
    return pl.pallas_call(
        body, name="swa_fwd", grid=(4,),
        in_specs=[col(qb), col(qb + 4), col(qb + 8), pl.BlockSpec((3, 2, SWA_BLK, SWA_BLK), lambda hp: (0, hp, 0, 0))],
        out_specs=[col(0), col(0), col(0)],
        out_shape=[SDS((S, SWA_W), f32)] * 3,
        scratch_shapes=[pltpu.VMEM((S, LANE), f32)] * 4,
        compiler_params=_cp(1, VMEM_LIMIT),
    )(proj, proj, proj, bt)


def _swa_bwd(proj, bt, nd, lse0, lse1, d_oab, after, S):
    scale = SWA_HD ** -0.5

    def body(q_ref, k_ref, v_ref, bt_ref, nd_scr, lse0_ref, lse1_ref, do_ref, after_ref, dq_ref, dk_ref, dv_ref, dsb_ref,
             dq_scr, dk_scr, dv_scr):
        lane = lax.broadcasted_iota(jnp.int32, (SWA_BLK, LANE), 1)
        h0 = lane < SWA_HD
        qi = lax.broadcasted_iota(jnp.int32, (SWA_BLK, SWA_BLK), 0)
        kj = lax.broadcasted_iota(jnp.int32, (SWA_BLK, SWA_BLK), 1)
        lower = kj <= qi
        eye = kj == qi
        rel127 = jnp.logical_or(kj == qi + 1, jnp.logical_and(qi == SWA_BLK - 1, kj == 0))
        ones16 = jnp.ones((LANE, SWA_BLK), bf16)
        lse_refs = (lse0_ref, lse1_ref)
        dk_scr[...] = jnp.zeros((S, LANE), f32)
        dv_scr[...] = jnp.zeros((S, LANE), f32)
        dsb_ref[...] = jnp.zeros_like(dsb_ref)

        for p, (_, d) in reversed(list(enumerate(PATTERNS))):
            nb_log2 = int(math.log2(S // d // SWA_BLK))
            first = p == len(PATTERNS) - 1

            def lp(i, carry, p=p, d=d, nb_log2=nb_log2, first=first):
                heads = [h0, jnp.logical_not(h0)]
                modes = _swa_prev_modes(1 << nb_log2)
                tiles = []
                kc_f = vc_f = None
                for u in range(SWA_UNROLL):
                    rows_c, rows_p, has_prev = _swa_tiles(q_ref, k_ref, v_ref, i * SWA_UNROLL + u, d, nb_log2, S)
                    kp_f = {"load": lambda: k_ref[rows_p, :], "reuse": lambda: kc_f, "none": lambda: None}[modes[u]]()
                    vp_f = {"load": lambda: v_ref[rows_p, :], "reuse": lambda: vc_f, "none": lambda: None}[modes[u]]()
                    has_prev = {"load": has_prev, "reuse": True, "none": False}[modes[u]]
                    q = q_ref[rows_c, :] * scale
                    kc_f = k_ref[rows_c, :]
                    vc_f = v_ref[rows_c, :]
                    kc = kc_f.astype(bf16)
                    kp = None if kp_f is None else kp_f.astype(bf16)
                    do = do_ref[rows_c, :]
                    nd = nd_scr[rows_c, :]
                    per_head = []
                    for mh in heads:
                        q_h = jnp.where(mh, q, 0.0)
                        do_a = jnp.where(mh, do, nd)
                        qh = q_h.astype(bf16)
                        doa = do_a.astype(bf16)
                        doh = jnp.where(mh, do, 0.0).astype(bf16)
                        dd_c = _dot_nt(doa, jnp.where(mh, vc_f, 1.0).astype(bf16))
                        if kp_f is None:
                            per_head.append((qh, doh, _dot_nt(qh, kc), None, None, dd_c, None, None))
                        else:
                            vpa = jnp.where(mh, vp_f, 1.0)
                            per_head.append((qh, doh, _dot_nt(qh, kc), _dot_nt(qh, kp), _dot((q_h * kp_f).astype(bf16), ones16),
                                             dd_c, _dot_nt(doa, vpa.astype(bf16)), _dot((do_a * vpa).astype(bf16), ones16)))
                    tiles.append((rows_c, rows_p, has_prev, kc, kp, per_head))
                grads = []
                for rows_c, rows_p, has_prev, kc, kp, per_head in tiles:
                    out = []
                    for h, (qh, doh, s_c, s_p, far, dd_c, dd_p, dd_far) in enumerate(per_head):
                        lse_h = lse_refs[h][rows_c, :]
                        if has_prev is False:
                            pm = jnp.exp(jnp.where(lower, s_c + bt_ref[p, h], NEG) - lse_h)
                            dsm = pm * dd_c
                            out.append((dsm, dsm.astype(bf16), None, pm.astype(bf16), None))
                            continue
                        s = jnp.where(lower, s_c, s_p) + bt_ref[p, h]
                        s_far = far + bt_ref[p, h, SWA_BLK - 1:SWA_BLK, 0:1]
                        if has_prev is not True:
                            s = jnp.where(jnp.logical_or(lower, has_prev), s, NEG)
                            s_far = jnp.where(has_prev, s_far, NEG)
                        pm = jnp.exp(s - lse_h)
                        p_far = jnp.exp(s_far - lse_h)
                        dsm = pm * jnp.where(lower, dd_c, dd_p)
                        ds_far = p_far * dd_far
                        out.append((dsm + jnp.where(rel127, ds_far, 0.0),
                                    jnp.where(lower, dsm, 0.0).astype(bf16),
                                    jnp.where(lower, jnp.where(eye, ds_far, 0.0), dsm).astype(bf16),
                                    jnp.where(lower, pm, 0.0).astype(bf16),
                                    jnp.where(lower, jnp.where(eye, p_far, 0.0), pm).astype(bf16)))
                    grads.append(out)
                done = []
                add = lambda acc, t: t if acc is None else acc + t
                for (rows_c, rows_p, _, kc, kp, per_head), out in zip(tiles, grads):
                    dq_t = dkc_t = dkp_t = dvc_t = dvp_t = None
                    for h, (_, dsc16, dsp16, pc16, pp16) in enumerate(out):
                        qh, doh = per_head[h][0], per_head[h][1]
                        dq_h = _dot(dsc16, kc)
                        dkc_t = add(dkc_t, _dot_tn(dsc16, qh))
                        dvc_t = add(dvc_t, _dot_tn(pc16, doh))
                        if dsp16 is not None:
                            dq_h = dq_h + _dot(dsp16, kp)
                            dkp_t = add(dkp_t, _dot_tn(dsp16, qh))
                            dvp_t = add(dvp_t, _dot_tn(pp16, doh))
                        dq_t = add(dq_t, jnp.where(heads[h], dq_h * scale, 0.0))
                    done.append([rows_c, rows_p, dq_t, dkc_t, dkp_t, dvc_t, dvp_t])
                for u in range(1, SWA_UNROLL):
                    if modes[u] == "reuse":
                        done[u - 1][3] = done[u - 1][3] + done[u][4]
                        done[u - 1][5] = done[u - 1][5] + done[u][6]
                for h in range(2):
                    tot = grads[0][h][0]
                    for g in grads[1:]:
                        tot = tot + g[h][0]
                    dsb_ref[0, p, h] += tot
                for u, (rows_c, rows_p, dq_t, dkc_t, dkp_t, dvc_t, dvp_t) in enumerate(done):
                    dq_scr[rows_c, :] = dq_t if first else dq_scr[rows_c, :] + dq_t
                    dk_scr[rows_c, :] = dk_scr[rows_c, :] + dkc_t
                    dv_scr[rows_c, :] = dv_scr[rows_c, :] + dvc_t
                    if modes[u] == "load":
                        dk_scr[rows_p, :] = dk_scr[rows_p, :] + dkp_t
                        dv_scr[rows_p, :] = dv_scr[rows_p, :] + dvp_t
                return carry

            lax.fori_loop(0, S // SWA_BLK // SWA_UNROLL, lp, 0)
        dq_ref[...] = dq_scr[...].astype(bf16)
        dk_ref[...] = dk_scr[...].astype(bf16)
        dv_ref[...] = dv_scr[...].astype(bf16)

    qb = COL_B // LANE
    col = lambda c: pl.BlockSpec((S, LANE), lambda hp, c=c: (0, c + hp))
    return pl.pallas_call(
        body, name="swa_bwd", grid=(4,),
        in_specs=[col(qb), col(qb + 4), col(qb + 8),
                  pl.BlockSpec((3, 2, SWA_BLK, SWA_BLK), lambda hp: (0, hp, 0, 0)),
                  col(0), col(0), col(0), col(4), _ANY],
        out_specs=[col(0), col(0), col(0),
                   pl.BlockSpec((1, 3, 2, SWA_BLK, SWA_BLK), lambda hp: (hp, 0, 0, 0, 0))],
        out_shape=[SDS((S, SWA_W), bf16)] * 3 + [SDS((4, 3, 2, SWA_BLK, SWA_BLK), f32)],
        scratch_shapes=[pltpu.VMEM((S, LANE), f32)] * 3,
        compiler_params=_cp(1, VMEM_LIMIT),
    )(proj, proj, proj, bt, nd, lse0, lse1, d_oab, after)


def _mix_fwd(oa, ob, w_out, x, g_post, S):
    TS = 512

    def body(oa_ref, ob_ref, w_ref, x_ref, g_ref, mix_ref, x1_ref):
        mix = _dot(oa_ref[...].astype(bf16), w_ref[0:GDN_W, :]) + _dot(ob_ref[...].astype(bf16), w_ref[GDN_W:D_MODEL, :])
        r = lax.rsqrt(jnp.mean(mix * mix, axis=-1, keepdims=True) + RMS_EPS)
        mix_ref[...] = mix
        x1_ref[...] = x_ref[...] + mix * r * g_ref[...]

    row = lambda w: pl.BlockSpec((TS, w), lambda i: (i, 0))
    return pl.pallas_call(
        body, name="mix_fwd", grid=(S // TS,),
        in_specs=[row(GDN_W), row(SWA_W), _resident_spec((D_MODEL, D_MODEL)), row(D_MODEL), _const_spec((1, D_MODEL))],
        out_specs=[row(D_MODEL), row(D_MODEL)],
        out_shape=[SDS((S, D_MODEL), f32), SDS((S, D_MODEL), f32)],
        compiler_params=_cp(1, VMEM_LIMIT),
    )(oa, ob, w_out, x, g_post)


def _mix_bwd(dx1, mix, g_post, w_out, ob, S):
    TS = 512

    def body(dx1_ref, mix_ref, g_ref, w_ref, ob_ref, dmix_ref, doab_ref, dg_ref, nd_ref):
        @pl.when(pl.program_id(0) == 0)
        def _():
            dg_ref[...] = jnp.zeros_like(dg_ref)

        mix = mix_ref[...]
        dz = dx1_ref[...]
        r = lax.rsqrt(jnp.mean(mix * mix, axis=-1, keepdims=True) + RMS_EPS)
        n = mix * r
        dg_ref[...] += jnp.sum(dz * n, axis=0, keepdims=True)
        dn = dz * g_ref[...]
        dmix = (r * (dn - n * jnp.mean(dn * n, axis=-1, keepdims=True))).astype(bf16)
        dmix_ref[...] = dmix
        doab = _dot_nt(dmix, w_ref[...])
        doab_ref[...] = doab
        hi_ = lax.shift_right_logical(lax.broadcasted_iota(jnp.int32, (SWA_W, SWA_W), 0), 6)
        hj_ = lax.shift_right_logical(lax.broadcasted_iota(jnp.int32, (SWA_W, SWA_W), 1), 6)
        swap = (hi_ == lax.bitwise_xor(hj_, 1)).astype(bf16)
        dlt = doab[:, GDN_W:] * ob_ref[...]
        hi = dlt.astype(bf16)
        nd_ref[...] = (_dot(hi, swap) + _dot((dlt - hi.astype(f32)).astype(bf16), swap)) * (-1.0 / SWA_HD)

    row = lambda w=D_MODEL: pl.BlockSpec((TS, w), lambda i: (i, 0))
    return pl.pallas_call(
        body, name="mix_bwd", grid=(S // TS,),
        in_specs=[row(), row(), _const_spec((1, D_MODEL)), _resident_spec((D_MODEL, D_MODEL)), row(SWA_W)],
        out_specs=[row(), row(), _const_spec((1, D_MODEL)), row(SWA_W)],
        out_shape=[SDS((S, D_MODEL), bf16), SDS((S, D_MODEL), f32), SDS((1, D_MODEL), f32), SDS((S, SWA_W), f32)],
        compiler_params=_cp(1, VMEM_LIMIT),
    )(dx1, mix, g_post, w_out, ob)


FFN_TS = 256
FFN_CH = 1408


def _ffn(x1, tgt, g_pre, g_post, wg, wu, wd, S):
    def body(x1_ref, t_ref, gp_ref, gq_ref, wg_ref, wu_ref, wd_ref,
             dx1_ref, h2_ref, act_ref, dgate_ref, dup_ref, df_ref, loss_ref, dgp_ref, dgq_ref, gate_scr, up_scr):
        @pl.when(pl.program_id(0) == 0)
        def _():
            loss_ref[...] = jnp.zeros_like(loss_ref)
            dgp_ref[...] = jnp.zeros_like(dgp_ref)
            dgq_ref[...] = jnp.zeros_like(dgq_ref)

        x1v = x1_ref[...]
        gp = gp_ref[...]
        gq = gq_ref[...]
        r2 = lax.rsqrt(jnp.mean(x1v * x1v, axis=-1, keepdims=True) + RMS_EPS)
        n2 = x1v * r2
        h2 = (n2 * gp).astype(bf16)
        h2_ref[...] = h2
        chunks = [slice(c * FFN_CH, (c + 1) * FFN_CH) for c in range(D_FF // FFN_CH)]
        for cs in chunks:
            gate_scr[:, cs] = _dot_nt(h2, wg_ref[cs, :])
            up_scr[:, cs] = _dot_nt(h2, wu_ref[cs, :])
        acts = []
        for cs in chunks:
            gate = gate_scr[:, cs]
            act = (gate * _sigmoid(gate) * up_scr[:, cs]).astype(bf16)
            act_ref[:, cs] = act
            acts.append(act)
        f = _dot(acts[0], wd_ref[chunks[0], :])
        for act, cs in zip(acts[1:], chunks[1:]):
            f = f + _dot(act, wd_ref[cs, :])
        r3 = lax.rsqrt(jnp.mean(f * f, axis=-1, keepdims=True) + RMS_EPS)
        n3 = f * r3
        err = x1v + n3 * gq - t_ref[...]
        loss_ref[...] += 0.5 * jnp.sum(jnp.mean(err * err, axis=-1, keepdims=True), axis=0, keepdims=True)
        dy = err * (1.0 / D_MODEL)
        dgq_ref[...] += jnp.sum(dy * n3, axis=0, keepdims=True)
        dn3 = dy * gq
        df = (r3 * (dn3 - n3 * jnp.mean(dn3 * n3, axis=-1, keepdims=True))).astype(bf16)
        df_ref[...] = df
        dacts = [_dot_nt(df, wd_ref[cs, :]) for cs in chunks]
        dgs = []
        for dact, cs in zip(dacts, chunks):
            gate = gate_scr[:, cs]
            sg = _sigmoid(gate)
            dup = (dact * gate * sg).astype(bf16)
            dgate = (dact * up_scr[:, cs] * (sg * (1.0 + gate * (1.0 - sg)))).astype(bf16)
            dup_ref[:, cs] = dup
            dgate_ref[:, cs] = dgate
            dgs.append((dgate, dup))
        dh2 = None
        for (dgate, dup), cs in zip(dgs, chunks):
            t = _dot(dgate, wg_ref[cs, :]) + _dot(dup, wu_ref[cs, :])
            dh2 = t if dh2 is None else dh2 + t
        dgp_ref[...] += jnp.sum(dh2 * n2, axis=0, keepdims=True)
        dn2 = dh2 * gp
        dx1_ref[...] = dy + r2 * (dn2 - n2 * jnp.mean(dn2 * n2, axis=-1, keepdims=True))

    row = lambda w: pl.BlockSpec((FFN_TS, w), lambda i: (i, 0))
    vec = _const_spec((1, D_MODEL))
    return pl.pallas_call(
        body, name="ffn_fwd_bwd", grid=(S // FFN_TS,),
        in_specs=[row(D_MODEL), row(D_MODEL), vec, vec, _resident_spec((D_FF, D_MODEL)), _resident_spec((D_FF, D_MODEL)),
                  _resident_spec((D_FF, D_MODEL))],
        out_specs=[row(D_MODEL), row(D_MODEL), row(D_FF), row(D_FF), row(D_FF), row(D_MODEL), _const_spec((1, LANE)), vec, vec],
        out_shape=[SDS((S, D_MODEL), f32), SDS((S, D_MODEL), bf16), SDS((S, D_FF), bf16), SDS((S, D_FF), bf16),
                   SDS((S, D_FF), bf16), SDS((S, D_MODEL), bf16), SDS((1, LANE), f32), SDS((1, D_MODEL), f32),
                   SDS((1, D_MODEL), f32)],
        scratch_shapes=[pltpu.VMEM((FFN_TS, D_FF), f32), pltpu.VMEM((FFN_TS, D_FF), f32)],
        compiler_params=_cp(1, VMEM_LIMIT),
    )(x1, tgt, g_pre, g_post, wg, wu, wd)


def _proj_bwd(x, dx1, g_pre, wcat, segs, after, S):
    TS = 512
    n = len(segs)
    cols = [(c0, a.shape[1]) for a, c0 in segs]

    def body(*refs):
        x_ref, dx1_ref, g_ref, w_ref = refs[:4]
        seg_refs = refs[4:4 + n]
        gx_ref, dg_ref = refs[5 + n:]

        @pl.when(pl.program_id(0) == 0)
        def _():
            dg_ref[...] = jnp.zeros_like(dg_ref)

        dh = jnp.zeros((TS, D_MODEL), f32)
        for s_ref, (c0, w) in zip(seg_refs, cols):
            dh = dh + _dot_nt(s_ref[...], w_ref[:, c0:c0 + w])
        xv = x_ref[...]
        g = g_ref[...]
        r = lax.rsqrt(jnp.mean(xv * xv, axis=-1, keepdims=True) + RMS_EPS)
        nx = xv * r
        dg_ref[...] += jnp.sum(dh * nx, axis=0, keepdims=True)
        dn = dh * g
        gx_ref[...] = dx1_ref[...] + r * (dn - nx * jnp.mean(dn * nx, axis=-1, keepdims=True))

    row = lambda w: pl.BlockSpec((TS, w), lambda i: (i, 0))
    return pl.pallas_call(
        body, name="proj_bwd", grid=(S // TS,),
        in_specs=[row(D_MODEL), row(D_MODEL), _const_spec((1, D_MODEL)), _resident_spec((D_MODEL, NCOL))]
                 + [row(w) for _, w in cols] + [_ANY],
        out_specs=[row(D_MODEL), _const_spec((1, D_MODEL))],
        out_shape=[SDS((S, D_MODEL), f32), SDS((1, D_MODEL), f32)],
        compiler_params=_cp(1, VMEM_LIMIT),
    )(x, dx1, g_pre, wcat, *[a for a, _ in segs], after)


def _wgrad(a, b, S, name):
    TS = 1024
    K = a.shape[1]
    N = b.shape[1]
    TN = next(t for t in (512, 1408, N) if N % t == 0)

    def body(a_ref, b_ref, o_ref, acc):
        @pl.when(pl.program_id(1) == 0)
        def _():
            acc[...] = jnp.zeros_like(acc)

        acc[...] += _dot_tn(a_ref[...].astype(bf16), b_ref[...])

        @pl.when(pl.program_id(1) == pl.num_programs(1) - 1)
        def _():
            o_ref[...] = acc[...].astype(bf16)

    return pl.pallas_call(
        body, name=name, grid=(N // TN, S // TS),
        in_specs=[pl.BlockSpec((TS, K), lambda j, s: (s, 0)), pl.BlockSpec((TS, TN), lambda j, s: (s, j))],
        out_specs=pl.BlockSpec((K, TN), lambda j, s: (0, j)), out_shape=SDS((K, N), bf16),
        scratch_shapes=[pltpu.VMEM((K, TN), f32)],
        compiler_params=_cp(2, VMEM_LIMIT),
    )(a, b)


def _w_in_pieces():
    n_a, n_g = 4 * GDN_W, 2 * GDN_HEADS
    cb = IN_COLS // N_DEV
    bounds = [(0, n_a, COL_A), (n_a, n_a + n_g, COL_G), (n_a + n_g, IN_COLS, COL_B)]
    out = []
    for j in range(N_DEV):
        lo, hi = j * cb, (j + 1) * cb
        for s0, s1, dst in bounds:
            a, b = max(lo, s0), min(hi, s1)
            if a < b:
                out.append((j, a - lo, b - a, dst + a - s0))
    return out


def _wcat_from_blocks(g_in):
    TR = 256
    cb = IN_COLS // N_DEV
    pieces = _w_in_pieces()

    def body(w_ref, o_ref):
        o_ref[:, COL_G:NCOL] = jnp.zeros((TR, NCOL - COL_G), bf16)
        for j, off, w, dst in pieces:
            o_ref[:, dst:dst + w] = w_ref[j, :, off:off + w]

    return pl.pallas_call(
        body, name="wcat_from_blocks", grid=(D_MODEL // TR,),
        in_specs=[pl.BlockSpec((N_DEV, TR, cb), lambda i: (0, i, 0))],
        out_specs=pl.BlockSpec((TR, NCOL), lambda i: (i, 0)),
        out_shape=SDS((D_MODEL, NCOL), bf16),
        compiler_params=_cp(1, VMEM_LIMIT),
    )(g_in)


def _wgrad_in(h1, segs, S):
    TS = 1024
    n = len(segs)
    cols = [(c0, a.shape[1]) for a, c0 in segs]
    cb = IN_COLS // N_DEV
    pieces = _w_in_pieces()

    def body(*refs):
        h_ref = refs[0]
        seg_refs = refs[1:1 + n]
        o_ref, acc = refs[1 + n], refs[2 + n]

        @pl.when(pl.program_id(0) == 0)
        def _():
            acc[...] = jnp.zeros_like(acc)

        h = h_ref[...]
        for s_ref, (c0, w) in zip(seg_refs, cols):
            acc[:, c0:c0 + w] += _dot_tn(h, s_ref[...])

        @pl.when(pl.program_id(0) == pl.num_programs(0) - 1)
        def _():
            for j, off, w, src in pieces:
                o_ref[j, :, off:off + w] = acc[:, src:src + w].astype(bf16)

    row = lambda w: pl.BlockSpec((TS, w), lambda i: (i, 0))
    return pl.pallas_call(
        body, name="wgrad_in", grid=(S // TS,),
        in_specs=[row(D_MODEL)] + [row(w) for _, w in cols],
        out_specs=_const_spec((N_DEV, D_MODEL, cb)),
        out_shape=SDS((N_DEV, D_MODEL, cb), bf16),
        scratch_shapes=[pltpu.VMEM((D_MODEL, NCOL), f32)],
        compiler_params=_cp(1, VMEM_LIMIT),
    )(h1, *[a for a, _ in segs])


def _adamw(recv, src, me, w, m, v, name):
    R, C = w.shape
    TR = 256 if R % 256 == 0 else R
    c1 = 1.0 / (1.0 - ADAM_B1 ** ADAM_STEP)
    c2 = 1.0 / (1.0 - ADAM_B2 ** ADAM_STEP)

    def body(me_ref, r_ref, own_ref, w_ref, m_ref, v_ref, g_out, d_out, m_out, v_out):
        g = None
        for s in range(N_DEV):
            t = jnp.where(me_ref[0] == s, own_ref[0], r_ref[s]).astype(f32)
            g = t if g is None else g + t
        mn = ADAM_B1 * m_ref[...] + (1.0 - ADAM_B1) * g
        vn = ADAM_B2 * v_ref[...] + (1.0 - ADAM_B2) * (g * g)
        g_out[...] = g
        m_out[...] = mn
        v_out[...] = vn
        d_out[...] = -ADAM_LR * ((mn * c1) / (jnp.sqrt(vn * c2) + ADAM_EPS) + ADAM_WD * w_ref[...])

    blk = pl.BlockSpec((TR, C), lambda i, me_ref: (i, 0))
    return pl.pallas_call(
        body, name=name,
        grid_spec=pltpu.PrefetchScalarGridSpec(
            num_scalar_prefetch=1, grid=(R // TR,),
            in_specs=[pl.BlockSpec((N_DEV, TR, C), lambda i, me_ref: (0, i, 0)),
                      pl.BlockSpec((1, TR, C), lambda i, me_ref: (me_ref[0], i, 0)), blk, blk, blk],
            out_specs=[blk, blk, blk, blk]),
        out_shape=[SDS((R, C), f32)] * 4,
        compiler_params=_cp(1, VMEM_LIMIT),
    )(me, recv, src, w, m, v)


MESH = pl.DeviceIdType.MESH
_ANY = pl.BlockSpec(memory_space=pl.ANY)


def _flip(v, d):
    return 1 - v if d else v


def _all_gather(shards):
    n = len(shards)

    def body(*refs):
        ins = refs[:n]
        outs = refs[n:2 * n]
        send_sems, recv_sems, local_sems = refs[2 * n:]
        x, y, c = lax.axis_index("x"), lax.axis_index("y"), lax.axis_index("c")
        me, sibling = (x, y, c), (x, y, 1 - c)
        chips = [(1 - x, y), (x, 1 - y), (1 - x, 1 - y)]

        def slot(px, py, pc):
            return 4 * px + 2 * py + pc

        def copy(a, k, block, to, src=None):
            dst = outs[a].at[slot(*block)]
            return pltpu.make_async_remote_copy(src_ref=dst if src is None else src, dst_ref=dst,
                                                send_sem=send_sems.at[a, k], recv_sem=recv_sems.at[a, k],
                                                device_id=to, device_id_type=MESH)

        mine, first, passed = [], [], []
        for a in range(n):
            cp = pltpu.make_async_copy(ins[a], outs[a].at[slot(*me)], local_sems.at[a])
            cp.start()
            mine.append(cp)
            fs = [copy(a, 0, me, sibling, src=ins[a])]
            fs += [copy(a, 1 + j, me, (*chip, c), src=ins[a]) for j, chip in enumerate(chips)]
            for cp in fs:
                cp.start()
            first += fs
        for j, chip in enumerate(chips):
            for a in range(n):
                copy(a, 1 + j, (*chip, c), me).wait_recv()
                cp = copy(a, 4 + j, (*chip, c), sibling)
                cp.start()
                passed.append(cp)
        for a in range(n):
            copy(a, 0, sibling, me).wait_recv()
            for j, chip in enumerate(chips):
                copy(a, 4 + j, (*chip, 1 - c), me).wait_recv()
        for cp in first + passed:
            cp.wait_send()
        for cp in mine:
            cp.wait()

    return pl.pallas_call(
        body, name="weight_all_gather",
        in_specs=[_ANY] * n, out_specs=[_ANY] * n,
        out_shape=[SDS((N_DEV,) + s.shape, s.dtype) for s in shards],
        scratch_shapes=[pltpu.SemaphoreType.DMA((n, 7)), pltpu.SemaphoreType.DMA((n, 7)), pltpu.SemaphoreType.DMA((n,))],
        compiler_params=pltpu.CompilerParams(has_side_effects=True),
    )(*shards)


def _grad_exchange(blocked, whole):
    arrs = list(blocked) + list(whole)
    n, nb = len(arrs), len(blocked)
    rel = [(dx, dy, dc) for dx in (0, 1) for dy in (0, 1) for dc in (0, 1) if dx or dy or dc]

    def body(*refs):
        ins = refs[:n]
        outs = refs[n:2 * n]
        send_sems, recv_sems, local_sems = refs[2 * n:]
        x, y, c = lax.axis_index("x"), lax.axis_index("y"), lax.axis_index("c")
        me = 4 * x + 2 * y + c
        sends, locs = [], []
        for a in range(n):
            cp = pltpu.make_async_copy(ins[a].at[me] if a < nb else ins[a], outs[a].at[me], local_sems.at[a])
            cp.start()
            locs.append(cp)
            for k, (dx, dy, dc) in enumerate(rel):
                peer = (_flip(x, dx), _flip(y, dy), _flip(c, dc))
                pidx = 4 * peer[0] + 2 * peer[1] + peer[2]
                cp = pltpu.make_async_remote_copy(src_ref=ins[a].at[pidx] if a < nb else ins[a], dst_ref=outs[a].at[me],
                                                  send_sem=send_sems.at[a, k], recv_sem=recv_sems.at[a, k],
                                                  device_id=peer, device_id_type=MESH)
                cp.start()
                sends.append(cp)
        for a in range(n):
            for k, (dx, dy, dc) in enumerate(rel):
                peer = (_flip(x, dx), _flip(y, dy), _flip(c, dc))
                pidx = 4 * peer[0] + 2 * peer[1] + peer[2]
                pltpu.make_async_remote_copy(src_ref=outs[a].at[pidx], dst_ref=outs[a].at[pidx],
                                             send_sem=send_sems.at[a, k], recv_sem=recv_sems.at[a, k],
                                             device_id=peer, device_id_type=MESH).wait_recv()
        for cp in sends:
            cp.wait_send()
        for cp in locs:
            cp.wait()

    shapes = [SDS(a.shape, a.dtype) for a in blocked] + [SDS((N_DEV,) + a.shape, a.dtype) for a in whole]
    return pl.pallas_call(
        body, name="grad_exchange",
        in_specs=[_ANY] * n, out_specs=[_ANY] * n, out_shape=shapes,
        scratch_shapes=[pltpu.SemaphoreType.DMA((n, 7)), pltpu.SemaphoreType.DMA((n, 7)), pltpu.SemaphoreType.DMA((n,))],
        compiler_params=pltpu.CompilerParams(has_side_effects=True),
    )(*arrs)


_HBM = pl.BlockSpec(memory_space=pltpu.HBM)
_SEM = pl.BlockSpec(memory_space=pltpu.SEMAPHORE)
_REL = [(dx, dy, dc) for dx in (0, 1) for dy in (0, 1) for dc in (0, 1) if dx or dy or dc]


N_PEER = len(_REL)
_EFFECT = pltpu.SideEffectType.DATAFLOW_SIDE_EFFECTING


def _peer_copies(srcs, lands, send_sems, recv_sems, blocked, as_receiver):
    x, y, c = lax.axis_index("x"), lax.axis_index("y"), lax.axis_index("c")
    me = 4 * x + 2 * y + c
    cps = []
    for a in range(len(srcs)):
        for k, (dx, dy, dc) in enumerate(_REL):
            peer = (_flip(x, dx), _flip(y, dy), _flip(c, dc))
            pidx = 4 * peer[0] + 2 * peer[1] + peer[2]
            cps.append(pltpu.make_async_remote_copy(
                src_ref=srcs[a].at[pidx] if blocked else srcs[a], dst_ref=lands[a].at[pidx if as_receiver else me],
                send_sem=send_sems[a * N_PEER + k], recv_sem=recv_sems[a * N_PEER + k],
                device_id=peer, device_id_type=MESH))
    return cps


def _exchange_start(srcs, after, blocked, name):
    n = len(srcs)
    ns = n * N_PEER
    lands = [lax.empty(s.shape if blocked else (N_DEV,) + s.shape, s.dtype) for s in srcs]

    def body(*refs):
        ins, lnd = refs[:n], refs[n:2 * n]
        outs = refs[2 * n + 1:]
        for cp in _peer_copies(ins, lnd, outs[:ns], outs[ns:2 * ns], blocked, False):
            cp.start()
        outs[-1][...] = jnp.zeros_like(outs[-1])

    res = pl.pallas_call(
        body, name=name,
        in_specs=[_HBM] * (2 * n) + [_ANY],
        out_specs=[_SEM] * (2 * ns) + [_HBM] * (2 * n) + [pl.BlockSpec(memory_space=pltpu.VMEM)],
        out_shape=[pltpu.SemaphoreType.DMA(())] * (2 * ns) + [pltpu.HBM(s.shape, s.dtype) for s in srcs]
                  + [pltpu.HBM(l.shape, l.dtype) for l in lands] + [SDS((8, LANE), f32)],
        input_output_aliases={i: 2 * ns + i for i in range(2 * n)},
        compiler_params=pltpu.CompilerParams(has_side_effects=_EFFECT),
    )(*[pltpu.with_memory_space_constraint(s, pltpu.HBM) for s in srcs],
      *[pltpu.with_memory_space_constraint(l, pltpu.HBM) for l in lands], after)
    return list(res[:2 * ns]), list(res[2 * ns:2 * ns + n]), list(res[2 * ns + n:2 * ns + 2 * n]), res[-1]


def _exchange_wait(sems, srcs, lands, after, blocked, name):
    n = len(srcs)
    ns = n * N_PEER

    def body(*refs):
        ins, lnd = refs[:n], refs[n:2 * n]
        sem_refs = refs[2 * n:2 * n + 2 * ns]
        for cp in _peer_copies(ins, lnd, sem_refs[:ns], sem_refs[ns:], blocked, True):
            cp.wait_send()
            cp.wait_recv()

    res = pl.pallas_call(
        body, name=name,
        in_specs=[_HBM] * (2 * n) + [_SEM] * (2 * ns) + [_ANY],
        out_specs=[_HBM] * (2 * n),
        out_shape=[pltpu.HBM(s.shape, s.dtype) for s in srcs] + [pltpu.HBM(l.shape, l.dtype) for l in lands],
        input_output_aliases={i: i for i in range(2 * n)},
        compiler_params=pltpu.CompilerParams(has_side_effects=_EFFECT),
    )(*srcs, *lands, *sems, after)
    return list(res[:n]), list(res[n:])


def _local_step(x, tgt, wcat, convw, late_weights, early_grads, last_grads, token, a_log, dt_bias, onorm_g, rel_bias,
                g_mix_pre, g_mix_post, g_ffn_pre, g_ffn_post):
    S = x.shape[0]
    bk_np = _bucket_tables()
    bk = jnp.asarray(bk_np)
    bt = _bias_tables(rel_bias, bk)
    proj, h1 = _proj_fwd(x, g_mix_pre, wcat, token, S)
    nu = S // CHUNK * GDN_HEADS
    qkv_u = _gdn_prep(proj, convw, S).reshape(3, nu, CHUNK, GDN_HD)
    intra, t_inv = _gdn_intra_fwd(qkv_u, proj, a_log, dt_bias, S)
    oa, states = _gdn_scan_fwd(intra, proj, onorm_g, S)
    ob, lse0, lse1 = _swa_fwd(proj, bt, S)
    wout, ffn_weights = late_weights(ob)
    mix, x1 = _mix_fwd(oa, ob, wout, x, g_mix_post, S)
    wgate, wup, wdown = ffn_weights(x1)
    dx1, h2, act, dgate_f, dup_f, df, loss, d_gfpre, d_gfpost = _ffn(x1, tgt, g_ffn_pre, g_ffn_post, wgate, wup, wdown, S)
    rows8 = lambda g: g.reshape(N_DEV, D_FF // N_DEV, D_MODEL)
    g_gate = rows8(_wgrad(dgate_f, h2, S, "wgrad_gate"))
    g_up = rows8(_wgrad(dup_f, h2, S, "wgrad_up"))
    g_down = rows8(_wgrad(act, df, S, "wgrad_down"))
    dmix, d_oab, d_gmpost, nd = _mix_bwd(dx1, mix, g_mix_post, wout, ob, S)
    g_out = jnp.concatenate([_wgrad(oa, dmix, S, "wgrad_out_a"), _wgrad(ob, dmix, S, "wgrad_out_b")], axis=0)
    token = early_grads(g_out.reshape(N_DEV, D_MODEL // N_DEV, D_MODEL), g_gate, g_up, g_down)
    dqb, dkb, dvb, dsb = _swa_bwd(proj, bt, nd, lse0, lse1, d_oab, token, S)
    *cots, dgate_a, d_og = _gdn_scan_bwd(intra, states, proj, d_oab, onorm_g, token, S)
    dqkv_u, dpg, d_alog, d_dtb = _gdn_intra_bwd(qkv_u, proj, a_log, dt_bias, t_inv, cots, S)
    dqkv_a, d_conv = _gdn_prep_bwd(proj, convw, dqkv_u.reshape(3, S // CHUNK, GDN_HEADS, CHUNK, GDN_HD), S)
    segs = [(dqkv_a, COL_A), (dgate_a, COL_A + 3 * GDN_W), (dqb, COL_B), (dkb, COL_B + SWA_W), (dvb, COL_B + 2 * SWA_W),
            (dpg, COL_G)]
    token = last_grads(_wgrad_in(h1, segs, S), d_conv)
    grad_x, d_gmpre = _proj_bwd(x, dx1, g_mix_pre, wcat, segs, token, S)
    d_rel = _rel_bias_grad(dsb, bk, bk_np)
    small = dict(a_log=d_alog[:, GDN_HEADS:2 * GDN_HEADS], dt_bias=d_dtb[:, GDN_HEADS:2 * GDN_HEADS], onorm_g=d_og, rel_bias=d_rel,
                 g_mix_pre=d_gmpre, g_mix_post=d_gmpost, g_ffn_pre=d_gfpre, g_ffn_post=d_gfpost)
    return loss, grad_x, small


SMALL = ("a_log", "dt_bias", "onorm_g", "rel_bias", "g_mix_pre", "g_mix_post", "g_ffn_pre", "g_ffn_post")
PACK_ROWS = 8


def _pack_small(d, loss=None):
    rest = jnp.concatenate([d["onorm_g"].reshape(-1), d["a_log"].reshape(-1), d["dt_bias"].reshape(-1),
                            d["rel_bias"].reshape(-1)])
    rest = jnp.concatenate([rest, jnp.zeros((D_MODEL - rest.shape[0],), f32)])
    extra = jnp.zeros((D_MODEL,), f32) if loss is None else jnp.concatenate([loss.reshape(1), jnp.zeros((D_MODEL - 1,), f32)])
    rows = [d["g_mix_pre"].reshape(-1), d["g_mix_post"].reshape(-1), d["g_ffn_pre"].reshape(-1),
            d["g_ffn_post"].reshape(-1), rest, extra]
    return jnp.concatenate([jnp.stack(rows), jnp.zeros((PACK_ROWS - len(rows), D_MODEL), f32)], axis=0)


def _unpack_small(p):
    o = GDN_HD
    return dict(g_mix_pre=p[0:1], g_mix_post=p[1:2], g_ffn_pre=p[2:3], g_ffn_post=p[3:4],
                onorm_g=p[4:5, :o], a_log=p[4:5, o:o + 4], dt_bias=p[4:5, o + 4:o + 8],
                rel_bias=p[4, o + 8:o + 8 + NUM_BUCKETS * SWA_HEADS].reshape(NUM_BUCKETS, SWA_HEADS))


def kernel(x, w_in, conv_w, a_log, dt_bias, onorm_g, rel_bias, w_out, g_mix_pre, g_mix_post, w_gate, w_up, w_down, g_ffn_pre, g_ffn_post, loss_target, m_w_in, m_conv_w, m_a_log, m_dt_bias, m_onorm_g, m_rel_bias, m_w_out, m_g_mix_pre, m_g_mix_post, m_w_gate, m_w_up, m_w_down, m_g_ffn_pre, m_g_ffn_post, v_w_in, v_conv_w, v_a_log, v_dt_bias, v_onorm_g, v_rel_bias, v_w_out, v_g_mix_pre, v_g_mix_post, v_w_gate, v_w_up, v_w_down, v_g_ffn_pre, v_g_ffn_post):
    big = ("w_in", "conv_w", "w_out", "w_gate", "w_up", "w_down")
    transposed = ("w_gate", "w_up")
    tr = lambda k, a: a.T if k in transposed else a
    w_sh = {k: tr(k, a[0]) for k, a in dict(w_in=w_in, conv_w=conv_w, w_out=w_out, w_gate=w_gate, w_up=w_up, w_down=w_down).items()}
    m_sh = {k: tr(k, a[0]) for k, a in dict(w_in=m_w_in, conv_w=m_conv_w, w_out=m_w_out, w_gate=m_w_gate, w_up=m_w_up,
                                             w_down=m_w_down).items()}
    v_sh = {k: tr(k, a[0]) for k, a in dict(w_in=v_w_in, conv_w=v_conv_w, w_out=v_w_out, w_gate=v_w_gate, w_up=v_w_up,
                                             w_down=v_w_down).items()}
    w_small = dict(a_log=a_log, dt_bias=dt_bias, onorm_g=onorm_g, rel_bias=rel_bias, g_mix_pre=g_mix_pre,
                   g_mix_post=g_mix_post, g_ffn_pre=g_ffn_pre, g_ffn_post=g_ffn_post)
    m_small = dict(a_log=m_a_log, dt_bias=m_dt_bias, onorm_g=m_onorm_g, rel_bias=m_rel_bias, g_mix_pre=m_g_mix_pre,
                   g_mix_post=m_g_mix_post, g_ffn_pre=m_g_ffn_pre, g_ffn_post=m_g_ffn_post)
    v_small = dict(a_log=v_a_log, dt_bias=v_dt_bias, onorm_g=v_onorm_g, rel_bias=v_rel_bias, g_mix_pre=v_g_mix_pre,
                   g_mix_post=v_g_mix_post, g_ffn_pre=v_g_ffn_pre, g_ffn_post=v_g_ffn_post)

    me = 4 * lax.axis_index("x") + 2 * lax.axis_index("y") + lax.axis_index("c")
    me1 = me.reshape(1).astype(jnp.int32)
    own = lambda full, part: lax.dynamic_update_index_in_dim(full, part, me, 0)
    cols = lambda g: g.reshape(g.shape[0], N_DEV, g.shape[1] // N_DEV).transpose(1, 0, 2)
    late = ("w_out", "w_gate", "w_up", "w_down")

    late_src = [w_sh[k].astype(bf16) for k in late]
    g_in, g_conv = _all_gather([w_sh["w_in"].astype(bf16), w_sh["conv_w"]])
    g_sems, g_src, g_land, g_token = _exchange_start(late_src, g_conv, False, "late_weights_start")
    wcat = _wcat_from_blocks(g_in)
    convw = g_conv.transpose(1, 0, 2).reshape(4, 3 * GDN_W)

    def late_weights(after):
        pick = lambda idx: [g_sems[half * len(late) * N_PEER + a * N_PEER + k] for half in (0, 1) for a in idx for k in range(N_PEER)]
        (s_out,), (l_out,) = _exchange_wait(pick([0]), g_src[:1], g_land[:1], after, False, "w_out_wait")

        def ffn_weights(after2):
            srcs, lands = _exchange_wait(pick([1, 2, 3]), g_src[1:], g_land[1:], after2, False, "ffn_weights_wait")
            return [own(l, s).reshape(D_FF, D_MODEL) for l, s in zip(lands, srcs)]

        return own(l_out, s_out).reshape(D_MODEL, D_MODEL), ffn_weights

    early, last = {}, {}

    def early_grads(*blocks):
        early["sems"], early["src"], early["land"], token = _exchange_start(list(blocks), me1, True, "late_grads_start")
        return token

    def last_grads(gw_in, gw_conv):
        src = [gw_in, cols(gw_conv)]
        last["sems"], last["src"], last["land"], token = _exchange_start(src, me1, True, "last_grads_start")
        return token

    loss_p, grad_x, gsmall = _local_step(
        x[0], loss_target[0], wcat, convw, late_weights, early_grads, last_grads, g_token,
        a_log, dt_bias, onorm_g, rel_bias, g_mix_pre, g_mix_post, g_ffn_pre, g_ffn_post)

    (r_small,) = _grad_exchange([], [_pack_small(gsmall, loss_p[0, 0])])
    outs = {}
    for names, ex, after, name in ((late, early, grad_x, "late_grads_wait"), (("w_in", "conv_w"), last, r_small, "last_grads_wait")):
        srcs, lands = _exchange_wait(ex["sems"], ex["src"], ex["land"], after, True, name)
        for k, l, s in zip(names, lands, srcs):
            outs[k] = _adamw(l, s, me1, w_sh[k], m_sh[k], v_sh[k], "adamw_" + k)
    sm = _adamw(r_small, r_small, me1, _pack_small(w_small), _pack_small(m_small), _pack_small(v_small), "adamw_small")
    loss = sm[0][5, 0]
    sm = [_unpack_small(t) for t in sm]
    for k in SMALL:
        outs[k] = tuple(t[k].reshape(w_small[k].shape) for t in sm)

    order = ("w_in", "conv_w", "a_log", "dt_bias", "onorm_g", "rel_bias", "w_out", "g_mix_pre", "g_mix_post", "w_gate",
             "w_up", "w_down", "g_ffn_pre", "g_ffn_post")
    lead = lambda k, t: tr(k, t)[None] if k in big else t
    res = [loss, grad_x[None]]
    for i in range(4):
        res += [lead(k, outs[k][i]) for k in order]
    return tuple(res)
```

```python
import functools
import math

import numpy as np
import jax
import jax.numpy as jnp
from jax import lax
from jax.experimental import pallas as pl
from jax.experimental.pallas import tpu as pltpu

f32 = jnp.float32
bf16 = jnp.bfloat16
SDS = jax.ShapeDtypeStruct

D_MODEL = 1024
GDN_HEADS = 4
GDN_HD = 128
GDN_W = 512
CHUNK = 64
SWA_HEADS = 8
SWA_HD = 64
SWA_W = 512
D_FF = 2816
IN_COLS = 3592
PATTERNS = ((128, 1), (512, 4), (2048, 16))
SWA_BLK = 128
NUM_BUCKETS = 32
MAX_DISTANCE = 2048
RMS_EPS = 1e-6
NEG = -1e30
N_DEV = 8

COL_A = 0
COL_B = 2048
COL_G = 3584
NCOL = 3712
LANE = 128

ADAM_LR, ADAM_B1, ADAM_B2, ADAM_EPS, ADAM_WD, ADAM_STEP = 0.001, 0.9, 0.999, 1e-08, 0.01, 10

VMEM_LIMIT = 56 * 1024 * 1024

HI = lax.Precision.HIGHEST
HIGH = lax.Precision.HIGH


def _cp(n_grid=0, vmem=None):
    kw = {}
    if n_grid:
        kw["dimension_semantics"] = ("arbitrary",) * n_grid
    if vmem:
        kw["vmem_limit_bytes"] = vmem
    return pltpu.CompilerParams(**kw)


def _dot(a, b):
    return jnp.dot(a, b, preferred_element_type=f32)


def _dot_nt(a, b):
    return lax.dot_general(a, b, (((1,), (1,)), ((), ())), preferred_element_type=f32)


def _dot_tn(a, b):
    return lax.dot_general(a, b, (((0,), (0,)), ((), ())), preferred_element_type=f32)


def _dot_hi(a, b):
    return jnp.dot(a, b, precision=HI, preferred_element_type=f32)


def _sigmoid(x):
    return 0.5 * jnp.tanh(0.5 * x) + 0.5


def _softplus(x):
    return jnp.maximum(x, 0.0) + jnp.log(1.0 + jnp.exp(-jnp.abs(x)))


def _const_spec(shape):
    nd = len(shape)
    return pl.BlockSpec(shape, lambda *_: (0,) * nd)


def _resident_spec(shape):
    nd = len(shape)
    return pl.BlockSpec(shape, lambda *_: (0,) * nd, pipeline_mode=pl.Buffered(1))


def _t5_bucket_np(dist):
    max_exact = NUM_BUCKETS // 2
    d = np.maximum(dist, 1).astype(np.float32)
    log_b = max_exact + (np.log(d / np.float32(max_exact)) / np.float32(math.log(MAX_DISTANCE / max_exact))
                         * np.float32(NUM_BUCKETS - max_exact)).astype(np.int32)
    return np.where(dist < max_exact, dist, np.minimum(log_b, NUM_BUCKETS - 1)).astype(np.int32)


def _bucket_tables():
    w = SWA_BLK
    qi = np.arange(w)[:, None]
    kj = np.arange(w)[None, :]
    rel = np.where(kj <= qi, qi - kj, qi + w - kj)
    out = np.zeros((len(PATTERNS), w, w), np.int32)
    for p, (_, dil) in enumerate(PATTERNS):
        steps = _t5_bucket_np(np.arange(w + 1) * dil)
        assert steps[w] == steps[w - 1]
        out[p] = steps[rel]
    return out


def _bias_tables(rel_bias, bk):
    def body(rb_ref, bk_ref, o_ref):
        b_idx = bk_ref[0]
        for h in range(SWA_HEADS):
            def lp(b, acc):
                return jnp.where(b_idx == b, rb_ref[b, h], acc)
            o_ref[0, h] = lax.fori_loop(0, NUM_BUCKETS, lp, jnp.zeros((SWA_BLK, SWA_BLK), f32))

    return pl.pallas_call(
        body, name="bias_tables", grid=(3,),
        in_specs=[pl.BlockSpec(memory_space=pltpu.SMEM), pl.BlockSpec((1, SWA_BLK, SWA_BLK), lambda p: (p, 0, 0))],
        out_specs=pl.BlockSpec((1, SWA_HEADS, SWA_BLK, SWA_BLK), lambda p: (p, 0, 0, 0)),
        out_shape=SDS((3, SWA_HEADS, SWA_BLK, SWA_BLK), f32),
        compiler_params=_cp(1),
    )(rel_bias, bk)


def _rel_bias_grad(dsb, bk, bk_np):
    present = [sorted(set(int(v) for v in np.unique(bk_np[p]))) for p in range(3)]

    def body(ds_ref, bk_ref, o_ref):
        row = lax.broadcasted_iota(jnp.int32, (NUM_BUCKETS, LANE), 0)
        col = lax.broadcasted_iota(jnp.int32, (NUM_BUCKETS, SWA_HEADS), 1)
        out = jnp.zeros((NUM_BUCKETS, SWA_HEADS), f32)
        for hp in range(4):
            for hh in range(2):
                acc = jnp.zeros((NUM_BUCKETS, LANE), f32)
                for p in range(3):
                    tile = ds_ref[hp, p, hh]
                    b_idx = bk_ref[p]
                    for b in present[p]:
                        part = jnp.sum(jnp.where(b_idx == b, tile, 0.0), axis=0, keepdims=True)
                        acc = acc + jnp.where(row == b, part, 0.0)
                tot = jnp.sum(acc, axis=1, keepdims=True)
                out = out + jnp.where(col == 2 * hp + hh, tot, 0.0)
        o_ref[...] = out

    return pl.pallas_call(body, name="rel_bias_grad", out_shape=SDS((NUM_BUCKETS, SWA_HEADS), f32),
                          compiler_params=_cp(0, 32 * 1024 * 1024))(dsb, bk)


def _proj_fwd(x, g_pre, wcat, after, S):
    TS = 512

    def body(x_ref, g_ref, w_ref, after_ref, o_ref, h_ref):
        xv = x_ref[...]
        r = lax.rsqrt(jnp.mean(xv * xv, axis=-1, keepdims=True) + RMS_EPS)
        h = (xv * r * g_ref[...]).astype(bf16)
        h_ref[...] = h
        o_ref[...] = _dot(h, w_ref[...])

    return pl.pallas_call(
        body, name="proj_fwd", grid=(S // TS,),
        in_specs=[pl.BlockSpec((TS, D_MODEL), lambda i: (i, 0)), _const_spec((1, D_MODEL)),
                  _resident_spec((D_MODEL, NCOL)), _ANY],
        out_specs=[pl.BlockSpec((TS, NCOL), lambda i: (i, 0)), pl.BlockSpec((TS, D_MODEL), lambda i: (i, 0))],
        out_shape=[SDS((S, NCOL), f32), SDS((S, D_MODEL), bf16)],
        compiler_params=_cp(1, VMEM_LIMIT),
    )(x, g_pre, wcat, after)


CONV_RT = 256
HALO = 8


CONV_NC = CONV_RT // CHUNK


def _gdn_prep(proj, conv_w, S):
    def body(p_ref, cw_ref, o_ref, xs_ref):
        t = pl.program_id(0)
        xs_ref[pl.ds(0, HALO), :] = jnp.zeros((HALO, LANE), f32)
        xs_ref[pl.ds(HALO, S), :] = p_ref[...]
        w = cw_ref[...]
        is_qk = t < 2
        scale = jnp.where(t == 0, GDN_HD ** -0.5, 1.0).astype(f32)

        def lp(c, carry):
            st = pl.multiple_of(c * CONV_RT, CONV_RT)
            pre = xs_ref[pl.ds(st + HALO - 3, CONV_RT), :] * w[0:1, :]
            for i in range(1, 4):
                pre = pre + xs_ref[pl.ds(st + HALO - 3 + i, CONV_RT), :] * w[i:i + 1, :]
            s = pre * _sigmoid(pre)
            nrm = s * lax.rsqrt(jnp.sum(s * s, axis=-1, keepdims=True) + 1e-6) * scale
            out = jnp.where(is_qk, nrm, s)
            for i in range(CONV_NC):
                o_ref[0, c * CONV_NC + i, 0] = out[i * CHUNK:(i + 1) * CHUNK]
            return carry

        lax.fori_loop(0, S // CONV_RT, lp, 0)

    return pl.pallas_call(
        body, name="gdn_prep", grid=(3, GDN_HEADS),
        in_specs=[pl.BlockSpec((S, LANE), lambda t, h: (0, t * GDN_HEADS + h)),
                  pl.BlockSpec((4, LANE), lambda t, h: (0, t * GDN_HEADS + h))],
        out_specs=pl.BlockSpec((1, S // CHUNK, 1, CHUNK, GDN_HD), lambda t, h: (t, 0, h, 0, 0)),
        out_shape=SDS((3, S // CHUNK, GDN_HEADS, CHUNK, GDN_HD), f32),
        scratch_shapes=[pltpu.VMEM((S + HALO, LANE), f32)],
        compiler_params=_cp(2, VMEM_LIMIT),
    )(proj, conv_w)


def _gdn_prep_bwd(proj, conv_w, dqkv, S):
    def body(p_ref, cw_ref, d_ref, dx_ref, dw_ref, xs_ref, dp_ref):
        t = pl.program_id(0)
        xs_ref[pl.ds(0, HALO), :] = jnp.zeros((HALO, LANE), f32)
        xs_ref[pl.ds(HALO, S), :] = p_ref[...]
        dp_ref[pl.ds(S, HALO), :] = jnp.zeros((HALO, LANE), f32)
        w = cw_ref[...]
        is_qk = t < 2
        scale = jnp.where(t == 0, GDN_HD ** -0.5, 1.0).astype(f32)

        def lp1(c, dw):
            st = pl.multiple_of(c * CONV_RT, CONV_RT)
            taps = [xs_ref[pl.ds(st + HALO - 3 + i, CONV_RT), :] for i in range(4)]
            pre = taps[0] * w[0:1, :]
            for i in range(1, 4):
                pre = pre + taps[i] * w[i:i + 1, :]
            sg = _sigmoid(pre)
            s = pre * sg
            d_out = jnp.concatenate([d_ref[0, c * CONV_NC + i, 0] for i in range(CONV_NC)], axis=0)
            rn = lax.rsqrt(jnp.sum(s * s, axis=-1, keepdims=True) + 1e-6)
            n = s * rn
            dn = d_out * scale
            ds_qk = rn * (dn - n * jnp.sum(dn * n, axis=-1, keepdims=True))
            ds = jnp.where(is_qk, ds_qk, d_out)
            dpre = ds * (sg * (1.0 + pre * (1.0 - sg)))
            dp_ref[pl.ds(st, CONV_RT), :] = dpre
            return tuple(dw[i] + jnp.sum(dpre * taps[i], axis=0, keepdims=True) for i in range(4))

        z = jnp.zeros((1, LANE), f32)
        dw = lax.fori_loop(0, S // CONV_RT, lp1, (z, z, z, z))
        for i in range(4):
            dw_ref[pl.ds(i, 1), :] = dw[i]

        def lp2(c, carry):
            st = pl.multiple_of(c * CONV_RT, CONV_RT)
            dx = dp_ref[pl.ds(st, CONV_RT), :] * w[3:4, :]
            for i in range(3):
                dx = dx + dp_ref[pl.ds(st + 3 - i, CONV_RT), :] * w[i:i + 1, :]
            dx_ref[pl.ds(st, CONV_RT), :] = dx.astype(bf16)
            return carry

        lax.fori_loop(0, S // CONV_RT, lp2, 0)

    col = lambda rows: pl.BlockSpec((rows, LANE), lambda t, h: (0, t * GDN_HEADS + h))
    return pl.pallas_call(
        body, name="gdn_prep_bwd", grid=(3, GDN_HEADS),
        in_specs=[col(S), col(4), pl.BlockSpec((1, S // CHUNK, 1, CHUNK, GDN_HD), lambda t, h: (t, 0, h, 0, 0))],
        out_specs=[col(S), col(4)],
        out_shape=[SDS((S, 3 * GDN_W), bf16), SDS((4, 3 * GDN_W), f32)],
        scratch_shapes=[pltpu.VMEM((S + HALO, LANE), f32), pltpu.VMEM((S + HALO, LANE), f32)],
        compiler_params=_cp(2, VMEM_LIMIT),
    )(proj, conv_w, dqkv)


def _bdot(a, b, prec=None):
    return lax.dot_general(a, b, (((2,), (1,)), ((0,), (0,))), precision=prec, preferred_element_type=f32)


def _bdot_nt(a, b, prec=None):
    return lax.dot_general(a, b, (((2,), (2,)), ((0,), (0,))), precision=prec, preferred_element_type=f32)


def _bdot_tn(a, b, prec=None):
    return lax.dot_general(a, b, (((1,), (1,)), ((0,), (0,))), precision=prec, preferred_element_type=f32)


@jax.custom_vjp
def _tri_inv_saved(a, t):
    return t


def _tri_inv_saved_fwd(a, t):
    return t, t


def _tri_inv_saved_bwd(t, dt):
    return -_bdot_tn(t, _bdot_nt(dt, t, HIGH), HIGH), jnp.zeros_like(t)


_tri_inv_saved.defvjp(_tri_inv_saved_fwd, _tri_inv_saved_bwd)


def _gdn_intra(q, k, v, beta, g, t_saved=None):
    nb = q.shape[0]
    c = CHUNK
    ii = lax.broadcasted_iota(jnp.int32, (c, c), 0)
    jj = lax.broadcasted_iota(jnp.int32, (c, c), 1)
    eye = ii == jj
    tril = ii >= jj
    strict = ii > jj
    ones = jnp.ones((nb, c, c), f32)
    eye_f = eye.astype(f32)

    g_row = _bdot(ones, jnp.where(eye, g, 0.0), HI)
    gc = jnp.sum(jnp.where(tril, g_row, 0.0), axis=2, keepdims=True)
    gc_row = _bdot(ones, jnp.where(eye, gc, 0.0), HI)
    decay = jnp.where(tril, jnp.exp(jnp.where(tril, gc - gc_row, 0.0)), 0.0)
    last = lax.broadcasted_iota(jnp.int32, (c, 1), 0) == c - 1
    gc_last = jnp.sum(jnp.where(last, gc, 0.0), axis=1, keepdims=True)
    e_gc = jnp.exp(gc)

    kb = k * beta
    k16 = k.astype(bf16)
    a = jnp.where(strict, _bdot_nt(kb.astype(bf16), k16) * decay, 0.0)
    if t_saved is None:
        xp = -a
        t_inv = eye_f + xp
        for level in range(5):
            if level < 2:
                xp = _bdot(xp, xp, HIGH)
                t_inv = t_inv + _bdot(t_inv, xp, HIGH)
            else:
                x16 = xp.astype(bf16)
                xp = _bdot(x16, x16)
                t_inv = t_inv + _bdot(t_inv.astype(bf16), xp.astype(bf16))
    else:
        t_inv = _tri_inv_saved(a, t_saved)
    t16 = t_inv.astype(bf16)
    u = _bdot(t16, (v * beta).astype(bf16))
    w = _bdot(t16, (kb * e_gc).astype(bf16))
    attn = jnp.where(tril, _bdot_nt(q.astype(bf16), k16) * decay, 0.0)
    gam = jnp.broadcast_to(jnp.exp(gc_last), (nb, 1, GDN_HD))
    return u, w, attn, q * e_gc, k * jnp.exp(gc_last - gc), gam, t_inv


GDN_TB = 256
GDN_NC = GDN_TB // CHUNK
GDN_NU = GDN_NC * GDN_HEADS


def _gdn_gates(pg_ref, al_ref, db_ref):
    lane1 = lax.broadcasted_iota(jnp.int32, (1, LANE), 1)
    a_lane = jnp.zeros((1, LANE), f32)
    b_lane = jnp.zeros((1, LANE), f32)
    for h in range(GDN_HEADS):
        a_lane = jnp.where(lane1 == GDN_HEADS + h, al_ref[0, h], a_lane)
        b_lane = jnp.where(lane1 == GDN_HEADS + h, db_ref[0, h], b_lane)
    pg = pg_ref[...]
    z = pg + b_lane
    return _sigmoid(pg), -jnp.exp(a_lane) * _softplus(z), z, a_lane


def _gdn_unit_inputs(qkv_ref, beta_all, g_all):
    units = [(cl, h) for cl in range(GDN_NC) for h in range(GDN_HEADS)]
    beta = jnp.stack([beta_all[cl * CHUNK:(cl + 1) * CHUNK, h:h + 1] for cl, h in units])
    g = jnp.stack([g_all[cl * CHUNK:(cl + 1) * CHUNK, GDN_HEADS + h:GDN_HEADS + h + 1] for cl, h in units])
    return qkv_ref[0], qkv_ref[1], qkv_ref[2], beta, g


def _unit_spec(*tail):
    nd = len(tail)
    return pl.BlockSpec((GDN_NU,) + tail, lambda i: (i,) + (0,) * nd)


def _gdn_intra_shapes(S):
    nu = S // CHUNK * GDN_HEADS
    row = SDS((nu, CHUNK, GDN_HD), f32)
    return [row, row, SDS((nu, CHUNK, CHUNK), f32), row, row, SDS((nu, 1, GDN_HD), f32)]


_GDN_INTRA_SPECS = lambda: [_unit_spec(CHUNK, GDN_HD), _unit_spec(CHUNK, GDN_HD), _unit_spec(CHUNK, CHUNK),
                            _unit_spec(CHUNK, GDN_HD), _unit_spec(CHUNK, GDN_HD), _unit_spec(1, GDN_HD)]


def _gdn_intra_fwd(qkv_u, proj, a_log, dt_bias, S):
    def body(qkv_ref, pg_ref, al_ref, db_ref, *outs):
        beta_all, g_all, _, _ = _gdn_gates(pg_ref, al_ref, db_ref)
        res = _gdn_intra(*_gdn_unit_inputs(qkv_ref, beta_all, g_all))
        for o_ref, r in zip(outs, res):
            o_ref[...] = r

    nu = S // CHUNK * GDN_HEADS
    *intra, t_inv = pl.pallas_call(
        body, name="gdn_intra_fwd", grid=(S // GDN_TB,),
        in_specs=[pl.BlockSpec((3, GDN_NU, CHUNK, GDN_HD), lambda i: (0, i, 0, 0)),
                  pl.BlockSpec((GDN_TB, LANE), lambda i: (i, COL_G // LANE)),
                  pl.BlockSpec(memory_space=pltpu.SMEM), pl.BlockSpec(memory_space=pltpu.SMEM)],
        out_specs=_GDN_INTRA_SPECS() + [_unit_spec(CHUNK, CHUNK)],
        out_shape=_gdn_intra_shapes(S) + [SDS((nu, CHUNK, CHUNK), f32)],
        compiler_params=_cp(1, VMEM_LIMIT),
    )(qkv_u, proj, a_log, dt_bias)
    return intra, t_inv


def _gdn_intra_bwd(qkv_u, proj, a_log, dt_bias, t_inv, cots, S):
    def body(qkv_ref, pg_ref, al_ref, db_ref, t_ref, du_ref, dw_ref, da_ref, dqd_ref, dkd_ref, dgm_ref,
             dqkv_ref, dpg_ref, dal_ref, ddb_ref):
        @pl.when(pl.program_id(0) == 0)
        def _():
            dal_ref[...] = jnp.zeros_like(dal_ref)
            ddb_ref[...] = jnp.zeros_like(ddb_ref)

        t_saved = t_ref[...]
        beta_all, g_all, z, a_lane = _gdn_gates(pg_ref, al_ref, db_ref)
        _, vjp = jax.vjp(lambda *a: _gdn_intra(*a, t_saved=t_saved)[:6], *_gdn_unit_inputs(qkv_ref, beta_all, g_all))
        dq, dk, dv, dbeta, dg = vjp((du_ref[...], dw_ref[...], da_ref[...], dqd_ref[...], dkd_ref[...], dgm_ref[...]))
        dqkv_ref[0] = dq
        dqkv_ref[1] = dk
        dqkv_ref[2] = dv
        lane = lax.broadcasted_iota(jnp.int32, (CHUNK, LANE), 1)
        rows = []
        for cl in range(GDN_NC):
            t = jnp.zeros((CHUNK, LANE), f32)
            for h in range(GDN_HEADS):
                b = cl * GDN_HEADS + h
                t = t + jnp.where(lane == h, dbeta[b], 0.0) + jnp.where(lane == GDN_HEADS + h, dg[b], 0.0)
            rows.append(t)
        d_all = jnp.concatenate(rows, axis=0)
        is_beta = lax.broadcasted_iota(jnp.int32, (GDN_TB, LANE), 1) < GDN_HEADS
        dz = d_all * (-jnp.exp(a_lane)) * _sigmoid(z)
        dpg_ref[...] = jnp.where(is_beta, d_all * beta_all * (1.0 - beta_all), dz).astype(bf16)
        dal_ref[...] += jnp.sum(jnp.where(is_beta, 0.0, d_all * g_all), axis=0, keepdims=True)
        ddb_ref[...] += jnp.sum(jnp.where(is_beta, 0.0, dz), axis=0, keepdims=True)

    acc = _const_spec((1, LANE))
    nu = S // CHUNK * GDN_HEADS
    return pl.pallas_call(
        body, name="gdn_intra_bwd", grid=(S // GDN_TB,),
        in_specs=[pl.BlockSpec((3, GDN_NU, CHUNK, GDN_HD), lambda i: (0, i, 0, 0)),
                  pl.BlockSpec((GDN_TB, LANE), lambda i: (i, COL_G // LANE)),
                  pl.BlockSpec(memory_space=pltpu.SMEM), pl.BlockSpec(memory_space=pltpu.SMEM),
                  _unit_spec(CHUNK, CHUNK)] + _GDN_INTRA_SPECS(),
        out_specs=[pl.BlockSpec((3, GDN_NU, CHUNK, GDN_HD), lambda i: (0, i, 0, 0)),
                   pl.BlockSpec((GDN_TB, LANE), lambda i: (i, 0)), acc, acc],
        out_shape=[SDS((3, nu, CHUNK, GDN_HD), f32), SDS((S, LANE), bf16), SDS((1, LANE), f32), SDS((1, LANE), f32)],
        compiler_params=_cp(1, VMEM_LIMIT),
    )(qkv_u, proj, a_log, dt_bias, t_inv, *cots)


def _gdn_scan_fwd(intra, proj, onorm_g, S):
    def body(u_ref, w_ref, at_ref, qd_ref, kd_ref, gm_ref, gate_ref, og_ref, out_ref, st_ref, s_scr):
        @pl.when(pl.program_id(0) == 0)
        def _():
            s_scr[...] = jnp.zeros_like(s_scr)

        og = og_ref[...]
        s = s_scr[...]
        chain = []
        for cl in range(GDN_NC):
            us = slice(cl * GDN_HEADS, (cl + 1) * GDN_HEADS)
            st_ref[us] = s
            s16 = s.astype(bf16)
            vn16 = (u_ref[us] - _bdot(w_ref[us].astype(bf16), s16)).astype(bf16)
            chain.append((us, s16, vn16))
            s = s * gm_ref[us] + _bdot_tn(kd_ref[us].astype(bf16), vn16)
        s_scr[...] = s
        for cl, (us, s16, vn16) in enumerate(chain):
            rows = slice(cl * CHUNK, (cl + 1) * CHUNK)
            o = _bdot(qd_ref[us].astype(bf16), s16) + _bdot(at_ref[us].astype(bf16), vn16)
            for h in range(GDN_HEADS):
                oh = o[h]
                gt = gate_ref[rows, h * GDN_HD:(h + 1) * GDN_HD]
                on = oh * lax.rsqrt(jnp.mean(oh * oh, axis=-1, keepdims=True) + RMS_EPS) * og
                out_ref[rows, h * GDN_HD:(h + 1) * GDN_HD] = on * (gt * _sigmoid(gt))

    nu = S // CHUNK * GDN_HEADS
    return pl.pallas_call(
        body, name="gdn_scan_fwd", grid=(S // GDN_TB,),
        in_specs=_GDN_INTRA_SPECS() + [pl.BlockSpec((GDN_TB, GDN_W), lambda i: (i, 3)), _const_spec((1, GDN_HD))],
        out_specs=[pl.BlockSpec((GDN_TB, GDN_W), lambda i: (i, 0)), _unit_spec(GDN_HD, GDN_HD)],
        out_shape=[SDS((S, GDN_W), f32), SDS((nu, GDN_HD, GDN_HD), f32)],
        scratch_shapes=[pltpu.VMEM((GDN_HEADS, GDN_HD, GDN_HD), f32)],
        compiler_params=_cp(1, VMEM_LIMIT),
    )(*intra, proj, onorm_g)


def _gdn_scan_bwd(intra, states, proj, d_oab, onorm_g, after, S):
    n_steps = S // GDN_TB

    def body(u_ref, w_ref, at_ref, qd_ref, kd_ref, gm_ref, st_ref, gate_ref, do_ref, og_ref, after_ref,
             du_ref, dw_ref, dat_ref, dqd_ref, dkd_ref, dgm_ref, dgate_ref, dog_ref, ds_scr):
        @pl.when(pl.program_id(0) == 0)
        def _():
            ds_scr[...] = jnp.zeros_like(ds_scr)
            dog_ref[...] = jnp.zeros_like(dog_ref)

        og = og_ref[...]
        ii = lax.broadcasted_iota(jnp.int32, (CHUNK, CHUNK), 0)
        jj = lax.broadcasted_iota(jnp.int32, (CHUNK, CHUNK), 1)
        tril = ii >= jj
        dog = jnp.zeros((1, GDN_HD), f32)
        pre = []
        for cl in range(GDN_NC):
            us = slice(cl * GDN_HEADS, (cl + 1) * GDN_HEADS)
            rows = slice(cl * CHUNK, (cl + 1) * CHUNK)
            s016 = st_ref[us].astype(bf16)
            w16 = w_ref[us].astype(bf16)
            qd16 = qd_ref[us].astype(bf16)
            at16 = at_ref[us].astype(bf16)
            vn16 = (u_ref[us] - _bdot(w16, s016)).astype(bf16)
            o = _bdot(qd16, s016) + _bdot(at16, vn16)
            do_h = []
            for h in range(GDN_HEADS):
                oh = o[h]
                lanes = slice(h * GDN_HD, (h + 1) * GDN_HD)
                gt = gate_ref[rows, lanes]
                d_out = do_ref[rows, lanes]
                r = lax.rsqrt(jnp.mean(oh * oh, axis=-1, keepdims=True) + RMS_EPS)
                n = oh * r
                sg = _sigmoid(gt)
                silu = gt * sg
                dog = dog + jnp.sum(d_out * n * silu, axis=0, keepdims=True)
                dgate_ref[rows, lanes] = (d_out * n * og * (sg * (1.0 + gt * (1.0 - sg)))).astype(bf16)
                dn = d_out * og * silu
                do_h.append(r * (dn - n * jnp.mean(dn * n, axis=-1, keepdims=True)))
            do16 = jnp.stack(do_h).astype(bf16)
            pre.append((us, s016, w16, vn16, do16, _bdot_tn(at16, do16), _bdot_tn(qd16, do16)))
        ds = ds_scr[...]
        chain = [None] * GDN_NC
        for cl in reversed(range(GDN_NC)):
            us, s016, w16, vn16, do16, at_do, qd_do = pre[cl]
            ds16 = ds.astype(bf16)
            dvn = at_do + _bdot(kd_ref[us].astype(bf16), ds16)
            dvn16 = dvn.astype(bf16)
            chain[cl] = (ds, ds16, dvn, dvn16)
            ds = qd_do + ds * gm_ref[us] - _bdot_tn(w16, dvn16)
        ds_scr[...] = ds
        for cl in range(GDN_NC):
            us, s016, w16, vn16, do16, _, _ = pre[cl]
            ds_in, ds16, dvn, dvn16 = chain[cl]
            du_ref[us] = dvn
            dw_ref[us] = -_bdot_nt(dvn16, s016)
            dat_ref[us] = jnp.where(tril, _bdot_nt(do16, vn16), 0.0)
            dqd_ref[us] = _bdot_nt(do16, s016)
            dkd_ref[us] = _bdot_nt(vn16, ds16)
            dgm_ref[us] = jnp.sum(st_ref[us] * ds_in, axis=1, keepdims=True)
        dog_ref[...] += dog

    def unit(*tail):
        nd = len(tail)
        return pl.BlockSpec((GDN_NU,) + tail, lambda i: (n_steps - 1 - i,) + (0,) * nd)

    intra_specs = [unit(CHUNK, GDN_HD), unit(CHUNK, GDN_HD), unit(CHUNK, CHUNK), unit(CHUNK, GDN_HD),
                   unit(CHUNK, GDN_HD), unit(1, GDN_HD)]
    tok = lambda c: pl.BlockSpec((GDN_TB, GDN_W), lambda i: (n_steps - 1 - i, c))
    return pl.pallas_call(
        body, name="gdn_scan_bwd", grid=(n_steps,),
        in_specs=intra_specs + [unit(GDN_HD, GDN_HD), tok(3), tok(0), _const_spec((1, GDN_HD)), _ANY],
        out_specs=intra_specs + [tok(0), _const_spec((1, GDN_HD))],
        out_shape=_gdn_intra_shapes(S) + [SDS((S, GDN_W), bf16), SDS((1, GDN_HD), f32)],
        scratch_shapes=[pltpu.VMEM((GDN_HEADS, GDN_HD, GDN_HD), f32)],
        compiler_params=_cp(1, VMEM_LIMIT),
    )(*intra, states, proj, d_oab, onorm_g, after)


SWA_UNROLL = 8


def _swa_tiles(q_ref, k_ref, v_ref, it, d, nb_log2, S):
    nb = 1 << nb_log2
    r = lax.shift_right_logical(it, nb_log2)
    blk = lax.bitwise_and(it, nb - 1)
    qs = blk * (SWA_BLK * d) + r
    ps = jnp.maximum(blk - 1, 0) * (SWA_BLK * d) + r
    if d > 1:
        rows_c, rows_p = pl.ds(qs, SWA_BLK, stride=d), pl.ds(ps, SWA_BLK, stride=d)
    else:
        rows_c, rows_p = pl.ds(pl.multiple_of(qs, SWA_BLK), SWA_BLK), pl.ds(pl.multiple_of(ps, SWA_BLK), SWA_BLK)
    return rows_c, rows_p, blk > 0


def _swa_prev_modes(nb):
    if nb >= SWA_UNROLL:
        return ["load"] + ["reuse"] * (SWA_UNROLL - 1)
    return ["none" if u % nb == 0 else "reuse" for u in range(SWA_UNROLL)]


def _swa_fwd(proj, bt, S):
    scale = SWA_HD ** -0.5

    def body(q_ref, k_ref, v_ref, bt_ref, o_ref, lse0_ref, lse1_ref, m0_scr, m1_scr, a0_scr, a1_scr):
        lane = lax.broadcasted_iota(jnp.int32, (SWA_BLK, LANE), 1)
        h0 = lane < SWA_HD
        qi = lax.broadcasted_iota(jnp.int32, (SWA_BLK, SWA_BLK), 0)
        kj = lax.broadcasted_iota(jnp.int32, (SWA_BLK, SWA_BLK), 1)
        lower = kj <= qi
        ones16 = jnp.ones((LANE, SWA_BLK), bf16)
        m_scrs = (m0_scr, m1_scr)
        a_scrs = (a0_scr, a1_scr)
        for p, (_, d) in reversed(list(enumerate(PATTERNS))):
            nb_log2 = int(math.log2(S // d // SWA_BLK))
            first = p == len(PATTERNS) - 1

            def lp(i, carry, p=p, d=d, nb_log2=nb_log2, first=first):
                heads = [h0, jnp.logical_not(h0)]
                modes = _swa_prev_modes(1 << nb_log2)
                tiles = []
                kc_f = None
                for u in range(SWA_UNROLL):
                    rows_c, rows_p, has_prev = _swa_tiles(q_ref, k_ref, v_ref, i * SWA_UNROLL + u, d, nb_log2, S)
                    kp_f = {"load": lambda: k_ref[rows_p, :], "reuse": lambda: kc_f, "none": lambda: None}[modes[u]]()
                    has_prev = {"load": has_prev, "reuse": True, "none": False}[modes[u]]
                    q = q_ref[rows_c, :]
                    kc_f = k_ref[rows_c, :]
                    kc = kc_f.astype(bf16)
                    logits = []
                    for mh in heads:
                        q_h = jnp.where(mh, q, 0.0)
                        qh = q_h.astype(bf16)
                        if kp_f is None:
                            logits.append((_dot_nt(qh, kc), None, None))
                        else:
                            logits.append((_dot_nt(qh, kc), _dot_nt(qh, kp_f.astype(bf16)), _dot((q_h * kp_f).astype(bf16), ones16)))
                    tiles.append((rows_c, rows_p, has_prev, logits))
                probs = []
                for rows_c, rows_p, has_prev, logits in tiles:
                    per_head = []
                    for h, (s_c, s_p, far) in enumerate(logits):
                        if has_prev is False:
                            s = jnp.where(lower, s_c * scale + bt_ref[p, h], NEG)
                            s_far = None
                        else:
                            s = jnp.where(lower, s_c, s_p) * scale + bt_ref[p, h]
                            s_far = far * scale + bt_ref[p, h, SWA_BLK - 1:SWA_BLK, 0:1]
                            if has_prev is not True:
                                s = jnp.where(jnp.logical_or(lower, has_prev), s, NEG)
                                s_far = jnp.where(has_prev, s_far, NEG)
                        mn = jnp.max(s, axis=1, keepdims=True)
                        if s_far is not None:
                            mn = jnp.maximum(s_far, mn)
                        alpha = None
                        if not first:
                            mo = m_scrs[h][rows_c, :]
                            mn = jnp.maximum(mo, mn)
                            alpha = jnp.exp(mo - mn)
                        mn = jnp.broadcast_to(mn, (SWA_BLK, LANE))
                        pm = jnp.exp(s - mn)
                        per_head.append((mn, alpha, None if s_far is None else jnp.exp(s_far - mn),
                                         jnp.where(lower, pm, 0.0).astype(bf16),
                                         None if s_far is None else jnp.where(lower, 0.0, pm).astype(bf16)))
                    probs.append(per_head)
                acc_old = [None if first else (a0_scr[t[0], :], a1_scr[t[0], :]) for t in tiles]
                done = []
                vc = None
                for u, ((rows_c, rows_p, _, _), per_head, old) in enumerate(zip(tiles, probs, acc_old)):
                    vp = {"load": lambda: v_ref[rows_p, :], "reuse": lambda: vc, "none": lambda: None}[modes[u]]()
                    vc = v_ref[rows_c, :]
                    acc_new = []
                    for h, (mn, alpha, p_far, pc16, pp16) in enumerate(per_head):
                        pv = _dot(pc16, jnp.where(heads[h], vc, 1.0).astype(bf16))
                        if pp16 is not None:
                            vpa = jnp.where(heads[h], vp, 1.0)
                            pv = pv + _dot(pp16, vpa.astype(bf16)) + p_far * vpa
                        acc_new.append(pv if first else alpha * old[h] + pv)
                    done.append((rows_c, per_head[0][0], per_head[1][0], acc_new[0], acc_new[1]))
                for rows_c, m0_new, m1_new, a0_new, a1_new in done:
                    m0_scr[rows_c, :] = m0_new
                    m1_scr[rows_c, :] = m1_new
                    a0_scr[rows_c, :] = a0_new
                    a1_scr[rows_c, :] = a1_new
                return carry

            lax.fori_loop(0, S // SWA_BLK // SWA_UNROLL, lp, 0)

        def fin(c, carry):
            rows = pl.ds(pl.multiple_of(c * SWA_BLK, SWA_BLK), SWA_BLK)
            a0 = a0_scr[rows, :]
            a1 = a1_scr[rows, :]
            l0 = jnp.where(h0, pltpu.roll(a0, SWA_HD, 1), a0)
            l1 = jnp.where(h0, a1, pltpu.roll(a1, SWA_HD, 1))
            o_ref[rows, :] = jnp.where(h0, a0 / l0, a1 / l1)
            lse0_ref[rows, :] = m0_scr[rows, :] + jnp.log(l0)
            lse1_ref[rows, :] = m1_scr[rows, :] + jnp.log(l1)
            return carry

        lax.fori_loop(0, S // SWA_BLK, fin, 0)

    qb = COL_B // LANE
    col = lambda c: pl.BlockSpec((S, LANE), lambda hp, c=c: (0, c + hp))
    return pl.pallas_call(
        body, name="swa_fwd", grid=(4,),
        in_specs=[col(qb), col(qb + 4), col(qb + 8), pl.BlockSpec((3, 2, SWA_BLK, SWA_BLK), lambda hp: (0, hp, 0, 0))],
        out_specs=[col(0), col(0), col(0)],
        out_shape=[SDS((S, SWA_W), f32)] * 3,
        scratch_shapes=[pltpu.VMEM((S, LANE), f32)] * 4,
        compiler_params=_cp(1, VMEM_LIMIT),
    )(proj, proj, proj, bt)


def _swa_bwd(proj, bt, nd, lse0, lse1, d_oab, after, S):
    scale = SWA_HD ** -0.5

    def body(q_ref, k_ref, v_ref, bt_ref, nd_scr, lse0_ref, lse1_ref, do_ref, after_ref, dq_ref, dk_ref, dv_ref, dsb_ref,
             dq_scr, dk_scr, dv_scr):
        lane = lax.broadcasted_iota(jnp.int32, (SWA_BLK, LANE), 1)
        h0 = lane < SWA_HD
        qi = lax.broadcasted_iota(jnp.int32, (SWA_BLK, SWA_BLK), 0)
        kj = lax.broadcasted_iota(jnp.int32, (SWA_BLK, SWA_BLK), 1)
        lower = kj <= qi
        eye = kj == qi
        rel127 = jnp.logical_or(kj == qi + 1, jnp.logical_and(qi == SWA_BLK - 1, kj == 0))
        ones16 = jnp.ones((LANE, SWA_BLK), bf16)
        lse_refs = (lse0_ref, lse1_ref)
        dk_scr[...] = jnp.zeros((S, LANE), f32)
        dv_scr[...] = jnp.zeros((S, LANE), f32)
        dsb_ref[...] = jnp.zeros_like(dsb_ref)

        for p, (_, d) in reversed(list(enumerate(PATTERNS))):
            nb_log2 = int(math.log2(S // d // SWA_BLK))
            first = p == len(PATTERNS) - 1

            def lp(i, carry, p=p, d=d, nb_log2=nb_log2, first=first):
                heads = [h0, jnp.logical_not(h0)]
                modes = _swa_prev_modes(1 << nb_log2)
                tiles = []
                kc_f = vc_f = None
                for u in range(SWA_UNROLL):
                    rows_c, rows_p, has_prev = _swa_tiles(q_ref, k_ref, v_ref, i * SWA_UNROLL + u, d, nb_log2, S)
                    kp_f = {"load": lambda: k_ref[rows_p, :], "reuse": lambda: kc_f, "none": lambda: None}[modes[u]]()
                    vp_f = {"load": lambda: v_ref[rows_p, :], "reuse": lambda: vc_f, "none": lambda: None}[modes[u]]()
                    has_prev = {"load": has_prev, "reuse": True, "none": False}[modes[u]]
                    q = q_ref[rows_c, :]
                    kc_f = k_ref[rows_c, :]
                    vc_f = v_ref[rows_c, :]
                    kc = kc_f.astype(bf16)
                    kp = None if kp_f is None else kp_f.astype(bf16)
                    do = do_ref[rows_c, :]
                    nd = nd_scr[rows_c, :]
                    per_head = []
                    for mh in heads:
                        q_h = jnp.where(mh, q, 0.0)
                        do_a = jnp.where(mh, do, nd)
                        qh = q_h.astype(bf16)
                        doa = do_a.astype(bf16)
                        doh = jnp.where(mh, do, 0.0).astype(bf16)
                        dd_c = _dot_nt(doa, jnp.where(mh, vc_f, 1.0).astype(bf16))
                        if kp_f is None:
                            per_head.append((qh, doh, _dot_nt(qh, kc), None, None, dd_c, None, None))
                        else:
                            vpa = jnp.where(mh, vp_f, 1.0)
                            per_head.append((qh, doh, _dot_nt(qh, kc), _dot_nt(qh, kp), _dot((q_h * kp_f).astype(bf16), ones16),
                                             dd_c, _dot_nt(doa, vpa.astype(bf16)), _dot((do_a * vpa).astype(bf16), ones16)))
                    tiles.append((rows_c, rows_p, has_prev, kc, kp, per_head))
                grads = []
                for rows_c, rows_p, has_prev, kc, kp, per_head in tiles:
                    out = []
                    for h, (qh, doh, s_c, s_p, far, dd_c, dd_p, dd_far) in enumerate(per_head):
                        lse_h = lse_refs[h][rows_c, :]
                        if has_prev is False:
                            pm = jnp.exp(jnp.where(lower, s_c * scale + bt_ref[p, h], NEG) - lse_h)
                            dsm = pm * dd_c
                            out.append((dsm, dsm.astype(bf16), None, pm.astype(bf16), None))
                            continue
                        s = jnp.where(lower, s_c, s_p) * scale + bt_ref[p, h]
                        s_far = far * scale + bt_ref[p, h, SWA_BLK - 1:SWA_BLK, 0:1]
                        if has_prev is not True:
                            s = jnp.where(jnp.logical_or(lower, has_prev), s, NEG)
                            s_far = jnp.where(has_prev, s_far, NEG)
                        pm = jnp.exp(s - lse_h)
                        p_far = jnp.exp(s_far - lse_h)
                        dsm = pm * jnp.where(lower, dd_c, dd_p)
                        ds_far = p_far * dd_far
                        out.append((dsm + jnp.where(rel127, ds_far, 0.0),
                                    jnp.where(lower, dsm, 0.0).astype(bf16),
                                    jnp.where(lower, jnp.where(eye, ds_far, 0.0), dsm).astype(bf16),
                                    jnp.where(lower, pm, 0.0).astype(bf16),
                                    jnp.where(lower, jnp.where(eye, p_far, 0.0), pm).astype(bf16)))
                    grads.append(out)
                done = []
                add = lambda acc, t: t if acc is None else acc + t
                for (rows_c, rows_p, _, kc, kp, per_head), out in zip(tiles, grads):
                    dq_t = dkc_t = dkp_t = dvc_t = dvp_t = None
                    for h, (_, dsc16, dsp16, pc16, pp16) in enumerate(out):
                        qh, doh = per_head[h][0], per_head[h][1]
                        dq_h = _dot(dsc16, kc)
                        dkc_t = add(dkc_t, _dot_tn(dsc16, qh) * scale)
                        dvc_t = add(dvc_t, _dot_tn(pc16, doh))
                        if dsp16 is not None:
                            dq_h = dq_h + _dot(dsp16, kp)
                            dkp_t = add(dkp_t, _dot_tn(dsp16, qh) * scale)
                            dvp_t = add(dvp_t, _dot_tn(pp16, doh))
                        dq_t = add(dq_t, jnp.where(heads[h], dq_h * scale, 0.0))
                    done.append([rows_c, rows_p, dq_t, dkc_t, dkp_t, dvc_t, dvp_t])
                for u in range(1, SWA_UNROLL):
                    if modes[u] == "reuse":
                        done[u - 1][3] = done[u - 1][3] + done[u][4]
                        done[u - 1][5] = done[u - 1][5] + done[u][6]
                for h in range(2):
                    tot = grads[0][h][0]
                    for g in grads[1:]:
                        tot = tot + g[h][0]
                    dsb_ref[0, p, h] += tot
                for u, (rows_c, rows_p, dq_t, dkc_t, dkp_t, dvc_t, dvp_t) in enumerate(done):
                    dq_scr[rows_c, :] = dq_t if first else dq_scr[rows_c, :] + dq_t
                    dk_scr[rows_c, :] = dk_scr[rows_c, :] + dkc_t
                    dv_scr[rows_c, :] = dv_scr[rows_c, :] + dvc_t
                    if modes[u] == "load":
                        dk_scr[rows_p, :] = dk_scr[rows_p, :] + dkp_t
                        dv_scr[rows_p, :] = dv_scr[rows_p, :] + dvp_t
                return carry

            lax.fori_loop(0, S // SWA_BLK // SWA_UNROLL, lp, 0)
        dq_ref[...] = dq_scr[...].astype(bf16)
        dk_ref[...] = dk_scr[...].astype(bf16)
        dv_ref[...] = dv_scr[...].astype(bf16)

    qb = COL_B // LANE
    col = lambda c: pl.BlockSpec((S, LANE), lambda hp, c=c: (0, c + hp))
    return pl.pallas_call(
        body, name="swa_bwd", grid=(4,),
        in_specs=[col(qb), col(qb + 4), col(qb + 8),
                  pl.BlockSpec((3, 2, SWA_BLK, SWA_BLK), lambda hp: (0, hp, 0, 0)),
                  col(0), col(0), col(0), col(4), _ANY],
        out_specs=[col(0), col(0), col(0),
                   pl.BlockSpec((1, 3, 2, SWA_BLK, SWA_BLK), lambda hp: (hp, 0, 0, 0, 0))],
        out_shape=[SDS((S, SWA_W), bf16)] * 3 + [SDS((4, 3, 2, SWA_BLK, SWA_BLK), f32)],
        scratch_shapes=[pltpu.VMEM((S, LANE), f32)] * 3,
        compiler_params=_cp(1, VMEM_LIMIT),
    )(proj, proj, proj, bt, nd, lse0, lse1, d_oab, after)


def _mix_fwd(oa, ob, w_out, x, g_post, S):
    TS = 512

    def body(oa_ref, ob_ref, w_ref, x_ref, g_ref, mix_ref, x1_ref):
        mix = _dot(oa_ref[...].astype(bf16), w_ref[0:GDN_W, :]) + _dot(ob_ref[...].astype(bf16), w_ref[GDN_W:D_MODEL, :])
        r = lax.rsqrt(jnp.mean(mix * mix, axis=-1, keepdims=True) + RMS_EPS)
        mix_ref[...] = mix
        x1_ref[...] = x_ref[...] + mix * r * g_ref[...]

    row = lambda w: pl.BlockSpec((TS, w), lambda i: (i, 0))
    return pl.pallas_call(
        body, name="mix_fwd", grid=(S // TS,),
        in_specs=[row(GDN_W), row(SWA_W), _resident_spec((D_MODEL, D_MODEL)), row(D_MODEL), _const_spec((1, D_MODEL))],
        out_specs=[row(D_MODEL), row(D_MODEL)],
        out_shape=[SDS((S, D_MODEL), f32), SDS((S, D_MODEL), f32)],
        compiler_params=_cp(1, VMEM_LIMIT),
    )(oa, ob, w_out, x, g_post)


def _mix_bwd(dx1, mix, g_post, w_out, ob, S):
    TS = 512

    def body(dx1_ref, mix_ref, g_ref, w_ref, ob_ref, dmix_ref, doab_ref, dg_ref, nd_ref):
        @pl.when(pl.program_id(0) == 0)
        def _():
            dg_ref[...] = jnp.zeros_like(dg_ref)

        mix = mix_ref[...]
        dz = dx1_ref[...]
        r = lax.rsqrt(jnp.mean(mix * mix, axis=-1, keepdims=True) + RMS_EPS)
        n = mix * r
        dg_ref[...] += jnp.sum(dz * n, axis=0, keepdims=True)
        dn = dz * g_ref[...]
        dmix = (r * (dn - n * jnp.mean(dn * n, axis=-1, keepdims=True))).astype(bf16)
        dmix_ref[...] = dmix
        doab = _dot_nt(dmix, w_ref[...])
        doab_ref[...] = doab
        hi_ = lax.shift_right_logical(lax.broadcasted_iota(jnp.int32, (SWA_W, SWA_W), 0), 6)
        hj_ = lax.shift_right_logical(lax.broadcasted_iota(jnp.int32, (SWA_W, SWA_W), 1), 6)
        swap = (hi_ == lax.bitwise_xor(hj_, 1)).astype(bf16)
        dlt = doab[:, GDN_W:] * ob_ref[...]
        hi = dlt.astype(bf16)
        nd_ref[...] = (_dot(hi, swap) + _dot((dlt - hi.astype(f32)).astype(bf16), swap)) * (-1.0 / SWA_HD)

    row = lambda w=D_MODEL: pl.BlockSpec((TS, w), lambda i: (i, 0))
    return pl.pallas_call(
        body, name="mix_bwd", grid=(S // TS,),
        in_specs=[row(), row(), _const_spec((1, D_MODEL)), _resident_spec((D_MODEL, D_MODEL)), row(SWA_W)],
        out_specs=[row(), row(), _const_spec((1, D_MODEL)), row(SWA_W)],
        out_shape=[SDS((S, D_MODEL), bf16), SDS((S, D_MODEL), f32), SDS((1, D_MODEL), f32), SDS((S, SWA_W), f32)],
        compiler_params=_cp(1, VMEM_LIMIT),
    )(dx1, mix, g_post, w_out, ob)


FFN_TS = 256
FFN_CH = 1408


def _ffn(x1, tgt, g_pre, g_post, wg, wu, wd, S):
    def body(x1_ref, t_ref, gp_ref, gq_ref, wg_ref, wu_ref, wd_ref,
             dx1_ref, h2_ref, act_ref, dgate_ref, dup_ref, df_ref, loss_ref, dgp_ref, dgq_ref, gate_scr, up_scr):
        @pl.when(pl.program_id(0) == 0)
        def _():
            loss_ref[...] = jnp.zeros_like(loss_ref)
            dgp_ref[...] = jnp.zeros_like(dgp_ref)
            dgq_ref[...] = jnp.zeros_like(dgq_ref)

        x1v = x1_ref[...]
        gp = gp_ref[...]
        gq = gq_ref[...]
        r2 = lax.rsqrt(jnp.mean(x1v * x1v, axis=-1, keepdims=True) + RMS_EPS)
        n2 = x1v * r2
        h2 = (n2 * gp).astype(bf16)
        h2_ref[...] = h2
        chunks = [slice(c * FFN_CH, (c + 1) * FFN_CH) for c in range(D_FF // FFN_CH)]
        for cs in chunks:
            gate_scr[:, cs] = _dot_nt(h2, wg_ref[cs, :])
            up_scr[:, cs] = _dot_nt(h2, wu_ref[cs, :])
        acts = []
        for cs in chunks:
            gate = gate_scr[:, cs]
            act = (gate * _sigmoid(gate) * up_scr[:, cs]).astype(bf16)
            act_ref[:, cs] = act
            acts.append(act)
        f = _dot(acts[0], wd_ref[chunks[0], :])
        for act, cs in zip(acts[1:], chunks[1:]):
            f = f + _dot(act, wd_ref[cs, :])
        r3 = lax.rsqrt(jnp.mean(f * f, axis=-1, keepdims=True) + RMS_EPS)
        n3 = f * r3
        err = x1v + n3 * gq - t_ref[...]
        loss_ref[...] += 0.5 * jnp.sum(jnp.mean(err * err, axis=-1, keepdims=True), axis=0, keepdims=True)
        dy = err * (1.0 / D_MODEL)
        dgq_ref[...] += jnp.sum(dy * n3, axis=0, keepdims=True)
        dn3 = dy * gq
        df = (r3 * (dn3 - n3 * jnp.mean(dn3 * n3, axis=-1, keepdims=True))).astype(bf16)
        df_ref[...] = df
        dacts = [_dot_nt(df, wd_ref[cs, :]) for cs in chunks]
        dgs = []
        for dact, cs in zip(dacts, chunks):
            gate = gate_scr[:, cs]
            sg = _sigmoid(gate)
            dup = (dact * gate * sg).astype(bf16)
            dgate = (dact * up_scr[:, cs] * (sg * (1.0 + gate * (1.0 - sg)))).astype(bf16)
            dup_ref[:, cs] = dup
            dgate_ref[:, cs] = dgate
            dgs.append((dgate, dup))
        dh2 = None
        for (dgate, dup), cs in zip(dgs, chunks):
            t = _dot(dgate, wg_ref[cs, :]) + _dot(dup, wu_ref[cs, :])
            dh2 = t if dh2 is None else dh2 + t
        dgp_ref[...] += jnp.sum(dh2 * n2, axis=0, keepdims=True)
        dn2 = dh2 * gp
        dx1_ref[...] = dy + r2 * (dn2 - n2 * jnp.mean(dn2 * n2, axis=-1, keepdims=True))

    row = lambda w: pl.BlockSpec((FFN_TS, w), lambda i: (i, 0))
    vec = _const_spec((1, D_MODEL))
    return pl.pallas_call(
        body, name="ffn_fwd_bwd", grid=(S // FFN_TS,),
        in_specs=[row(D_MODEL), row(D_MODEL), vec, vec, _resident_spec((D_FF, D_MODEL)), _resident_spec((D_FF, D_MODEL)),
                  _resident_spec((D_FF, D_MODEL))],
        out_specs=[row(D_MODEL), row(D_MODEL), row(D_FF), row(D_FF), row(D_FF), row(D_MODEL), _const_spec((1, LANE)), vec, vec],
        out_shape=[SDS((S, D_MODEL), f32), SDS((S, D_MODEL), bf16), SDS((S, D_FF), bf16), SDS((S, D_FF), bf16),
                   SDS((S, D_FF), bf16), SDS((S, D_MODEL), bf16), SDS((1, LANE), f32), SDS((1, D_MODEL), f32),
                   SDS((1, D_MODEL), f32)],
        scratch_shapes=[pltpu.VMEM((FFN_TS, D_FF), f32), pltpu.VMEM((FFN_TS, D_FF), f32)],
        compiler_params=_cp(1, VMEM_LIMIT),
    )(x1, tgt, g_pre, g_post, wg, wu, wd)


def _proj_bwd(x, dx1, g_pre, wcat, segs, after, S):
    TS = 512
    n = len(segs)
    cols = [(c0, a.shape[1]) for a, c0 in segs]

    def body(*refs):
        x_ref, dx1_ref, g_ref, w_ref = refs[:4]
        seg_refs = refs[4:4 + n]
        gx_ref, dg_ref = refs[5 + n:]

        @pl.when(pl.program_id(0) == 0)
        def _():
            dg_ref[...] = jnp.zeros_like(dg_ref)

        dh = jnp.zeros((TS, D_MODEL), f32)
        for s_ref, (c0, w) in zip(seg_refs, cols):
            dh = dh + _dot_nt(s_ref[...], w_ref[:, c0:c0 + w])
        xv = x_ref[...]
        g = g_ref[...]
        r = lax.rsqrt(jnp.mean(xv * xv, axis=-1, keepdims=True) + RMS_EPS)
        nx = xv * r
        dg_ref[...] += jnp.sum(dh * nx, axis=0, keepdims=True)
        dn = dh * g
        gx_ref[...] = dx1_ref[...] + r * (dn - nx * jnp.mean(dn * nx, axis=-1, keepdims=True))

    row = lambda w: pl.BlockSpec((TS, w), lambda i: (i, 0))
    return pl.pallas_call(
        body, name="proj_bwd", grid=(S // TS,),
        in_specs=[row(D_MODEL), row(D_MODEL), _const_spec((1, D_MODEL)), _resident_spec((D_MODEL, NCOL))]
                 + [row(w) for _, w in cols] + [_ANY],
        out_specs=[row(D_MODEL), _const_spec((1, D_MODEL))],
        out_shape=[SDS((S, D_MODEL), f32), SDS((1, D_MODEL), f32)],
        compiler_params=_cp(1, VMEM_LIMIT),
    )(x, dx1, g_pre, wcat, *[a for a, _ in segs], after)


def _wgrad(a, b, S, name):
    TS = 1024
    K = a.shape[1]
    N = b.shape[1]
    TN = next(t for t in (512, 1408, N) if N % t == 0)

    def body(a_ref, b_ref, o_ref, acc):
        @pl.when(pl.program_id(1) == 0)
        def _():
            acc[...] = jnp.zeros_like(acc)

        acc[...] += _dot_tn(a_ref[...].astype(bf16), b_ref[...])

        @pl.when(pl.program_id(1) == pl.num_programs(1) - 1)
        def _():
            o_ref[...] = acc[...].astype(bf16)

    return pl.pallas_call(
        body, name=name, grid=(N // TN, S // TS),
        in_specs=[pl.BlockSpec((TS, K), lambda j, s: (s, 0)), pl.BlockSpec((TS, TN), lambda j, s: (s, j))],
        out_specs=pl.BlockSpec((K, TN), lambda j, s: (0, j)), out_shape=SDS((K, N), bf16),
        scratch_shapes=[pltpu.VMEM((K, TN), f32)],
        compiler_params=_cp(2, VMEM_LIMIT),
    )(a, b)


def _w_in_pieces():
    n_a, n_g = 4 * GDN_W, 2 * GDN_HEADS
    cb = IN_COLS // N_DEV
    bounds = [(0, n_a, COL_A), (n_a, n_a + n_g, COL_G), (n_a + n_g, IN_COLS, COL_B)]
    out = []
    for j in range(N_DEV):
        lo, hi = j * cb, (j + 1) * cb
        for s0, s1, dst in bounds:
            a, b = max(lo, s0), min(hi, s1)
            if a < b:
                out.append((j, a - lo, b - a, dst + a - s0))
    return out


def _wcat_from_blocks(g_in):
    TR = 256
    cb = IN_COLS // N_DEV
    pieces = _w_in_pieces()

    def body(w_ref, o_ref):
        o_ref[:, COL_G:NCOL] = jnp.zeros((TR, NCOL - COL_G), bf16)
        for j, off, w, dst in pieces:
            o_ref[:, dst:dst + w] = w_ref[j, :, off:off + w]

    return pl.pallas_call(
        body, name="wcat_from_blocks", grid=(D_MODEL // TR,),
        in_specs=[pl.BlockSpec((N_DEV, TR, cb), lambda i: (0, i, 0))],
        out_specs=pl.BlockSpec((TR, NCOL), lambda i: (i, 0)),
        out_shape=SDS((D_MODEL, NCOL), bf16),
        compiler_params=_cp(1, VMEM_LIMIT),
    )(g_in)


def _wgrad_in(h1, segs, S):
    TS = 1024
    n = len(segs)
    cols = [(c0, a.shape[1]) for a, c0 in segs]
    cb = IN_COLS // N_DEV
    pieces = _w_in_pieces()

    def body(*refs):
        h_ref = refs[0]
        seg_refs = refs[1:1 + n]
        o_ref, acc = refs[1 + n], refs[2 + n]

        @pl.when(pl.program_id(0) == 0)
        def _():
            acc[...] = jnp.zeros_like(acc)

        h = h_ref[...]
        for s_ref, (c0, w) in zip(seg_refs, cols):
            acc[:, c0:c0 + w] += _dot_tn(h, s_ref[...])

        @pl.when(pl.program_id(0) == pl.num_programs(0) - 1)
        def _():
            for j, off, w, src in pieces:
                o_ref[j, :, off:off + w] = acc[:, src:src + w].astype(bf16)

    row = lambda w: pl.BlockSpec((TS, w), lambda i: (i, 0))
    return pl.pallas_call(
        body, name="wgrad_in", grid=(S // TS,),
        in_specs=[row(D_MODEL)] + [row(w) for _, w in cols],
        out_specs=_const_spec((N_DEV, D_MODEL, cb)),
        out_shape=SDS((N_DEV, D_MODEL, cb), bf16),
        scratch_shapes=[pltpu.VMEM((D_MODEL, NCOL), f32)],
        compiler_params=_cp(1, VMEM_LIMIT),
    )(h1, *[a for a, _ in segs])


def _adamw(recv, src, me, w, m, v, name):
    R, C = w.shape
    TR = 256 if R % 256 == 0 else R
    c1 = 1.0 / (1.0 - ADAM_B1 ** ADAM_STEP)
    c2 = 1.0 / (1.0 - ADAM_B2 ** ADAM_STEP)

    def body(me_ref, r_ref, own_ref, w_ref, m_ref, v_ref, g_out, d_out, m_out, v_out):
        g = None
        for s in range(N_DEV):
            t = jnp.where(me_ref[0] == s, own_ref[0], r_ref[s]).astype(f32)
            g = t if g is None else g + t
        mn = ADAM_B1 * m_ref[...] + (1.0 - ADAM_B1) * g
        vn = ADAM_B2 * v_ref[...] + (1.0 - ADAM_B2) * (g * g)
        g_out[...] = g
        m_out[...] = mn
        v_out[...] = vn
        d_out[...] = -ADAM_LR * ((mn * c1) / (jnp.sqrt(vn * c2) + ADAM_EPS) + ADAM_WD * w_ref[...])

    blk = pl.BlockSpec((TR, C), lambda i, me_ref: (i, 0))
    return pl.pallas_call(
        body, name=name,
        grid_spec=pltpu.PrefetchScalarGridSpec(
            num_scalar_prefetch=1, grid=(R // TR,),
            in_specs=[pl.BlockSpec((N_DEV, TR, C), lambda i, me_ref: (0, i, 0)),
                      pl.BlockSpec((1, TR, C), lambda i, me_ref: (me_ref[0], i, 0)), blk, blk, blk],
            out_specs=[blk, blk, blk, blk]),
        out_shape=[SDS((R, C), f32)] * 4,
        compiler_params=_cp(1, VMEM_LIMIT),
    )(me, recv, src, w, m, v)


MESH = pl.DeviceIdType.MESH
_ANY = pl.BlockSpec(memory_space=pl.ANY)


def _flip(v, d):
    return 1 - v if d else v


def _all_gather(shards):
    n = len(shards)

    def body(*refs):
        ins = refs[:n]
        outs = refs[n:2 * n]
        send_sems, recv_sems, local_sems = refs[2 * n:]
        x, y, c = lax.axis_index("x"), lax.axis_index("y"), lax.axis_index("c")
        me, sibling = (x, y, c), (x, y, 1 - c)
        chips = [(1 - x, y), (x, 1 - y), (1 - x, 1 - y)]

        def slot(px, py, pc):
            return 4 * px + 2 * py + pc

        def copy(a, k, block, to, src=None):
            dst = outs[a].at[slot(*block)]
            return pltpu.make_async_remote_copy(src_ref=dst if src is None else src, dst_ref=dst,
                                                send_sem=send_sems.at[a, k], recv_sem=recv_sems.at[a, k],
                                                device_id=to, device_id_type=MESH)

        mine, first, passed = [], [], []
        for a in range(n):
            cp = pltpu.make_async_copy(ins[a], outs[a].at[slot(*me)], local_sems.at[a])
            cp.start()
            mine.append(cp)
            fs = [copy(a, 0, me, sibling, src=ins[a])]
            fs += [copy(a, 1 + j, me, (*chip, c), src=ins[a]) for j, chip in enumerate(chips)]
            for cp in fs:
                cp.start()
            first += fs
        for j, chip in enumerate(chips):
            for a in range(n):
                copy(a, 1 + j, (*chip, c), me).wait_recv()
                cp = copy(a, 4 + j, (*chip, c), sibling)
                cp.start()
                passed.append(cp)
        for a in range(n):
            copy(a, 0, sibling, me).wait_recv()
            for j, chip in enumerate(chips):
                copy(a, 4 + j, (*chip, 1 - c), me).wait_recv()
        for cp in first + passed:
            cp.wait_send()
        for cp in mine:
            cp.wait()

    return pl.pallas_call(
        body, name="weight_all_gather",
        in_specs=[_ANY] * n, out_specs=[_ANY] * n,
        out_shape=[SDS((N_DEV,) + s.shape, s.dtype) for s in shards],
        scratch_shapes=[pltpu.SemaphoreType.DMA((n, 7)), pltpu.SemaphoreType.DMA((n, 7)), pltpu.SemaphoreType.DMA((n,))],
        compiler_params=pltpu.CompilerParams(has_side_effects=True),
    )(*shards)


def _grad_exchange(blocked, whole):
    arrs = list(blocked) + list(whole)
    n, nb = len(arrs), len(blocked)
    rel = [(dx, dy, dc) for dx in (0, 1) for dy in (0, 1) for dc in (0, 1) if dx or dy or dc]

    def body(*refs):
        ins = refs[:n]
        outs = refs[n:2 * n]
        send_sems, recv_sems, local_sems = refs[2 * n:]
        x, y, c = lax.axis_index("x"), lax.axis_index("y"), lax.axis_index("c")
        me = 4 * x + 2 * y + c
        sends, locs = [], []
        for a in range(n):
            cp = pltpu.make_async_copy(ins[a].at[me] if a < nb else ins[a], outs[a].at[me], local_sems.at[a])
            cp.start()
            locs.append(cp)
            for k, (dx, dy, dc) in enumerate(rel):
                peer = (_flip(x, dx), _flip(y, dy), _flip(c, dc))
                pidx = 4 * peer[0] + 2 * peer[1] + peer[2]
                cp = pltpu.make_async_remote_copy(src_ref=ins[a].at[pidx] if a < nb else ins[a], dst_ref=outs[a].at[me],
                                                  send_sem=send_sems.at[a, k], recv_sem=recv_sems.at[a, k],
                                                  device_id=peer, device_id_type=MESH)
                cp.start()
                sends.append(cp)
        for a in range(n):
            for k, (dx, dy, dc) in enumerate(rel):
                peer = (_flip(x, dx), _flip(y, dy), _flip(c, dc))
                pidx = 4 * peer[0] + 2 * peer[1] + peer[2]
                pltpu.make_async_remote_copy(src_ref=outs[a].at[pidx], dst_ref=outs[a].at[pidx],
                                             send_sem=send_sems.at[a, k], recv_sem=recv_sems.at[a, k],
                                             device_id=peer, device_id_type=MESH).wait_recv()
        for cp in sends:
            cp.wait_send()
        for cp in locs:
            cp.wait()

    shapes = [SDS(a.shape, a.dtype) for a in blocked] + [SDS((N_DEV,) + a.shape, a.dtype) for a in whole]
    return pl.pallas_call(
        body, name="grad_exchange",
        in_specs=[_ANY] * n, out_specs=[_ANY] * n, out_shape=shapes,
        scratch_shapes=[pltpu.SemaphoreType.DMA((n, 7)), pltpu.SemaphoreType.DMA((n, 7)), pltpu.SemaphoreType.DMA((n,))],
        compiler_params=pltpu.CompilerParams(has_side_effects=True),
    )(*arrs)


_HBM = pl.BlockSpec(memory_space=pltpu.HBM)
_SEM = pl.BlockSpec(memory_space=pltpu.SEMAPHORE)
_REL = [(dx, dy, dc) for dx in (0, 1) for dy in (0, 1) for dc in (0, 1) if dx or dy or dc]


N_PEER = len(_REL)
_EFFECT = pltpu.SideEffectType.DATAFLOW_SIDE_EFFECTING


def _peer_copies(srcs, lands, send_sems, recv_sems, blocked, as_receiver):
    x, y, c = lax.axis_index("x"), lax.axis_index("y"), lax.axis_index("c")
    me = 4 * x + 2 * y + c
    cps = []
    for a in range(len(srcs)):
        for k, (dx, dy, dc) in enumerate(_REL):
            peer = (_flip(x, dx), _flip(y, dy), _flip(c, dc))
            pidx = 4 * peer[0] + 2 * peer[1] + peer[2]
            cps.append(pltpu.make_async_remote_copy(
                src_ref=srcs[a].at[pidx] if blocked else srcs[a], dst_ref=lands[a].at[pidx if as_receiver else me],
                send_sem=send_sems[a * N_PEER + k], recv_sem=recv_sems[a * N_PEER + k],
                device_id=peer, device_id_type=MESH))
    return cps


def _exchange_start(srcs, after, blocked, name):
    n = len(srcs)
    ns = n * N_PEER
    lands = [lax.empty(s.shape if blocked else (N_DEV,) + s.shape, s.dtype) for s in srcs]

    def body(*refs):
        ins, lnd = refs[:n], refs[n:2 * n]
        outs = refs[2 * n + 1:]
        for cp in _peer_copies(ins, lnd, outs[:ns], outs[ns:2 * ns], blocked, False):
            cp.start()
        outs[-1][...] = jnp.zeros_like(outs[-1])

    res = pl.pallas_call(
        body, name=name,
        in_specs=[_HBM] * (2 * n) + [_ANY],
        out_specs=[_SEM] * (2 * ns) + [_HBM] * (2 * n) + [pl.BlockSpec(memory_space=pltpu.VMEM)],
        out_shape=[pltpu.SemaphoreType.DMA(())] * (2 * ns) + [pltpu.HBM(s.shape, s.dtype) for s in srcs]
                  + [pltpu.HBM(l.shape, l.dtype) for l in lands] + [SDS((8, LANE), f32)],
        input_output_aliases={i: 2 * ns + i for i in range(2 * n)},
        compiler_params=pltpu.CompilerParams(has_side_effects=_EFFECT),
    )(*[pltpu.with_memory_space_constraint(s, pltpu.HBM) for s in srcs],
      *[pltpu.with_memory_space_constraint(l, pltpu.HBM) for l in lands], after)
    return list(res[:2 * ns]), list(res[2 * ns:2 * ns + n]), list(res[2 * ns + n:2 * ns + 2 * n]), res[-1]


def _exchange_wait(sems, srcs, lands, after, blocked, name):
    n = len(srcs)
    ns = n * N_PEER

    def body(*refs):
        ins, lnd = refs[:n], refs[n:2 * n]
        sem_refs = refs[2 * n:2 * n + 2 * ns]
        for cp in _peer_copies(ins, lnd, sem_refs[:ns], sem_refs[ns:], blocked, True):
            cp.wait_send()
            cp.wait_recv()

    res = pl.pallas_call(
        body, name=name,
        in_specs=[_HBM] * (2 * n) + [_SEM] * (2 * ns) + [_ANY],
        out_specs=[_HBM] * (2 * n),
        out_shape=[pltpu.HBM(s.shape, s.dtype) for s in srcs] + [pltpu.HBM(l.shape, l.dtype) for l in lands],
        input_output_aliases={i: i for i in range(2 * n)},
        compiler_params=pltpu.CompilerParams(has_side_effects=_EFFECT),
    )(*srcs, *lands, *sems, after)
    return list(res[:n]), list(res[n:])


def _local_step(x, tgt, wcat, convw, late_weights, early_grads, last_grads, token, a_log, dt_bias, onorm_g, rel_bias,
                g_mix_pre, g_mix_post, g_ffn_pre, g_ffn_post):
    S = x.shape[0]
    bk_np = _bucket_tables()
    bk = jnp.asarray(bk_np)
    bt = _bias_tables(rel_bias, bk)
    proj, h1 = _proj_fwd(x, g_mix_pre, wcat, token, S)
    nu = S // CHUNK * GDN_HEADS
    qkv_u = _gdn_prep(proj, convw, S).reshape(3, nu, CHUNK, GDN_HD)
    intra, t_inv = _gdn_intra_fwd(qkv_u, proj, a_log, dt_bias, S)
    oa, states = _gdn_scan_fwd(intra, proj, onorm_g, S)
    ob, lse0, lse1 = _swa_fwd(proj, bt, S)
    wout, ffn_weights = late_weights(ob)
    mix, x1 = _mix_fwd(oa, ob, wout, x, g_mix_post, S)
    wgate, wup, wdown = ffn_weights(x1)
    dx1, h2, act, dgate_f, dup_f, df, loss, d_gfpre, d_gfpost = _ffn(x1, tgt, g_ffn_pre, g_ffn_post, wgate, wup, wdown, S)
    rows8 = lambda g: g.reshape(N_DEV, D_FF // N_DEV, D_MODEL)
    g_gate = rows8(_wgrad(dgate_f, h2, S, "wgrad_gate"))
    g_up = rows8(_wgrad(dup_f, h2, S, "wgrad_up"))
    g_down = rows8(_wgrad(act, df, S, "wgrad_down"))
    dmix, d_oab, d_gmpost, nd = _mix_bwd(dx1, mix, g_mix_post, wout, ob, S)
    g_out = jnp.concatenate([_wgrad(oa, dmix, S, "wgrad_out_a"), _wgrad(ob, dmix, S, "wgrad_out_b")], axis=0)
    token = early_grads(g_out.reshape(N_DEV, D_MODEL // N_DEV, D_MODEL), g_gate, g_up, g_down)
    dqb, dkb, dvb, dsb = _swa_bwd(proj, bt, nd, lse0, lse1, d_oab, token, S)
    *cots, dgate_a, d_og = _gdn_scan_bwd(intra, states, proj, d_oab, onorm_g, token, S)
    dqkv_u, dpg, d_alog, d_dtb = _gdn_intra_bwd(qkv_u, proj, a_log, dt_bias, t_inv, cots, S)
    dqkv_a, d_conv = _gdn_prep_bwd(proj, convw, dqkv_u.reshape(3, S // CHUNK, GDN_HEADS, CHUNK, GDN_HD), S)
    segs = [(dqkv_a, COL_A), (dgate_a, COL_A + 3 * GDN_W), (dqb, COL_B), (dkb, COL_B + SWA_W), (dvb, COL_B + 2 * SWA_W),
            (dpg, COL_G)]
    token = last_grads(_wgrad_in(h1, segs, S), d_conv)
    grad_x, d_gmpre = _proj_bwd(x, dx1, g_mix_pre, wcat, segs, token, S)
    d_rel = _rel_bias_grad(dsb, bk, bk_np)
    small = dict(a_log=d_alog[:, GDN_HEADS:2 * GDN_HEADS], dt_bias=d_dtb[:, GDN_HEADS:2 * GDN_HEADS], onorm_g=d_og, rel_bias=d_rel,
                 g_mix_pre=d_gmpre, g_mix_post=d_gmpost, g_ffn_pre=d_gfpre, g_ffn_post=d_gfpost)
    return loss, grad_x, small


SMALL = ("a_log", "dt_bias", "onorm_g", "rel_bias", "g_mix_pre", "g_mix_post", "g_ffn_pre", "g_ffn_post")
PACK_ROWS = 8


def _pack_small(d, loss=None):
    rest = jnp.concatenate([d["onorm_g"].reshape(-1), d["a_log"].reshape(-1), d["dt_bias"].reshape(-1),
                            d["rel_bias"].reshape(-1)])
    rest = jnp.concatenate([rest, jnp.zeros((D_MODEL - rest.shape[0],), f32)])
    extra = jnp.zeros((D_MODEL,), f32) if loss is None else jnp.concatenate([loss.reshape(1), jnp.zeros((D_MODEL - 1,), f32)])
    rows = [d["g_mix_pre"].reshape(-1), d["g_mix_post"].reshape(-1), d["g_ffn_pre"].reshape(-1),
            d["g_ffn_post"].reshape(-1), rest, extra]
    return jnp.concatenate([jnp.stack(rows), jnp.zeros((PACK_ROWS - len(rows), D_MODEL), f32)], axis=0)


def _unpack_small(p):
    o = GDN_HD
    return dict(g_mix_pre=p[0:1], g_mix_post=p[1:2], g_ffn_pre=p[2:3], g_ffn_post=p[3:4],
                onorm_g=p[4:5, :o], a_log=p[4:5, o:o + 4], dt_bias=p[4:5, o + 4:o + 8],
                rel_bias=p[4, o + 8:o + 8 + NUM_BUCKETS * SWA_HEADS].reshape(NUM_BUCKETS, SWA_HEADS))


def kernel(x, w_in, conv_w, a_log, dt_bias, onorm_g, rel_bias, w_out, g_mix_pre, g_mix_post, w_gate, w_up, w_down, g_ffn_pre, g_ffn_post, loss_target, m_w_in, m_conv_w, m_a_log, m_dt_bias, m_onorm_g, m_rel_bias, m_w_out, m_g_mix_pre, m_g_mix_post, m_w_gate, m_w_up, m_w_down, m_g_ffn_pre, m_g_ffn_post, v_w_in, v_conv_w, v_a_log, v_dt_bias, v_onorm_g, v_rel_bias, v_w_out, v_g_mix_pre, v_g_mix_post, v_w_gate, v_w_up, v_w_down, v_g_ffn_pre, v_g_ffn_post):
    big = ("w_in", "conv_w", "w_out", "w_gate", "w_up", "w_down")
    transposed = ("w_gate", "w_up")
    tr = lambda k, a: a.T if k in transposed else a
    w_sh = {k: tr(k, a[0]) for k, a in dict(w_in=w_in, conv_w=conv_w, w_out=w_out, w_gate=w_gate, w_up=w_up, w_down=w_down).items()}
    m_sh = {k: tr(k, a[0]) for k, a in dict(w_in=m_w_in, conv_w=m_conv_w, w_out=m_w_out, w_gate=m_w_gate, w_up=m_w_up,
                                             w_down=m_w_down).items()}
    v_sh = {k: tr(k, a[0]) for k, a in dict(w_in=v_w_in, conv_w=v_conv_w, w_out=v_w_out, w_gate=v_w_gate, w_up=v_w_up,
                                             w_down=v_w_down).items()}
    w_small = dict(a_log=a_log, dt_bias=dt_bias, onorm_g=onorm_g, rel_bias=rel_bias, g_mix_pre=g_mix_pre,
                   g_mix_post=g_mix_post, g_ffn_pre=g_ffn_pre, g_ffn_post=g_ffn_post)
    m_small = dict(a_log=m_a_log, dt_bias=m_dt_bias, onorm_g=m_onorm_g, rel_bias=m_rel_bias, g_mix_pre=m_g_mix_pre,
                   g_mix_post=m_g_mix_post, g_ffn_pre=m_g_ffn_pre, g_ffn_post=m_g_ffn_post)
    v_small = dict(a_log=v_a_log, dt_bias=v_dt_bias, onorm_g=v_onorm_g, rel_bias=v_rel_bias, g_mix_pre=v_g_mix_pre,
                   g_mix_post=v_g_mix_post, g_ffn_pre=v_g_ffn_pre, g_ffn_post=v_g_ffn_post)

    me = 4 * lax.axis_index("x") + 2 * lax.axis_index("y") + lax.axis_index("c")
    me1 = me.reshape(1).astype(jnp.int32)
    own = lambda full, part: lax.dynamic_update_index_in_dim(full, part, me, 0)
    cols = lambda g: g.reshape(g.shape[0], N_DEV, g.shape[1] // N_DEV).transpose(1, 0, 2)
    late = ("w_out", "w_gate", "w_up", "w_down")

    late_src = [w_sh[k].astype(bf16) for k in late]
    g_in, g_conv = _all_gather([w_sh["w_in"].astype(bf16), w_sh["conv_w"]])
    g_sems, g_src, g_land, g_token = _exchange_start(late_src, g_conv, False, "late_weights_start")
    wcat = _wcat_from_blocks(g_in)
    convw = g_conv.transpose(1, 0, 2).reshape(4, 3 * GDN_W)

    def late_weights(after):
        pick = lambda idx: [g_sems[half * len(late) * N_PEER + a * N_PEER + k] for half in (0, 1) for a in idx for k in range(N_PEER)]
        (s_out,), (l_out,) = _exchange_wait(pick([0]), g_src[:1], g_land[:1], after, False, "w_out_wait")

        def ffn_weights(after2):
            srcs, lands = _exchange_wait(pick([1, 2, 3]), g_src[1:], g_land[1:], after2, False, "ffn_weights_wait")
            return [own(l, s).reshape(D_FF, D_MODEL) for l, s in zip(lands, srcs)]

        return own(l_out, s_out).reshape(D_MODEL, D_MODEL), ffn_weights

    early, last = {}, {}

    def early_grads(*blocks):
        early["sems"], early["src"], early["land"], token = _exchange_start(list(blocks), me1, True, "late_grads_start")
        return token

    def last_grads(gw_in, gw_conv):
        src = [gw_in, cols(gw_conv)]
        last["sems"], last["src"], last["land"], token = _exchange_start(src, me1, True, "last_grads_start")
        return token

    loss_p, grad_x, gsmall = _local_step(
        x[0], loss_target[0], wcat, convw, late_weights, early_grads, last_grads, g_token,
        a_log, dt_bias, onorm_g, rel_bias, g_mix_pre, g_mix_post, g_ffn_pre, g_ffn_post)

    (r_small,) = _grad_exchange([], [_pack_small(gsmall, loss_p[0, 0])])
    outs = {}
    for names, ex, after, name in ((late, early, grad_x, "late_grads_wait"), (("w_in", "conv_w"), last, r_small, "last_grads_wait")):
        srcs, lands = _exchange_wait(ex["sems"], ex["src"], ex["land"], after, True, name)
        for k, l, s in zip(names, lands, srcs):
            outs[k] = _adamw(l, s, me1, w_sh[k], m_sh[k], v_sh[k], "adamw_" + k)
    sm = _adamw(r_small, r_small, me1, _pack_small(w_small), _pack_small(m_small), _pack_small(v_small), "adamw_small")
    loss = sm[0][5, 0]
    sm = [_unpack_small(t) for t in sm]
    for k in SMALL:
        outs[k] = tuple(t[k].reshape(w_small[k].shape) for t in sm)

    order = ("w_in", "conv_w", "a_log", "dt_bias", "onorm_g", "rel_bias", "w_out", "g_mix_pre", "g_mix_post", "w_gate",
             "w_up", "w_down", "g_ffn_pre", "g_ffn_post")
    lead = lambda k, t: tr(k, t)[None] if k in big else t
    res = [loss, grad_x[None]]
    for i in range(4):
        res += [lead(k, outs[k][i]) for k in order]
    return tuple(res)
```

```python
import functools
import math

import numpy as np
import jax
import jax.numpy as jnp
from jax import lax
from jax.experimental import pallas as pl
from jax.experimental.pallas import tpu as pltpu

f32 = jnp.float32
bf16 = jnp.bfloat16
SDS = jax.ShapeDtypeStruct

D_MODEL = 1024
GDN_HEADS = 4
GDN_HD = 128
GDN_W = 512
CHUNK = 64
SWA_HEADS = 8
SWA_HD = 64
SWA_W = 512
D_FF = 2816
IN_COLS = 3592
PATTERNS = ((128, 1), (512, 4), (2048, 16))
SWA_BLK = 128
NUM_BUCKETS = 32
MAX_DISTANCE = 2048
RMS_EPS = 1e-6
NEG = -1e30
N_DEV = 8

COL_A = 0
COL_B = 2048
COL_G = 3584
NCOL = 3712
LANE = 128

ADAM_LR, ADAM_B1, ADAM_B2, ADAM_EPS, ADAM_WD, ADAM_STEP = 0.001, 0.9, 0.999, 1e-08, 0.01, 10

VMEM_LIMIT = 56 * 1024 * 1024

HIGH = lax.Precision.HIGH


def _cp(n_grid=0, vmem=None):
    kw = {}
    if n_grid:
        kw["dimension_semantics"] = ("arbitrary",) * n_grid
    if vmem:
        kw["vmem_limit_bytes"] = vmem
    return pltpu.CompilerParams(**kw)


def _dot(a, b):
    return jnp.dot(a, b, preferred_element_type=f32)


def _dot_nt(a, b):
    return lax.dot_general(a, b, (((1,), (1,)), ((), ())), preferred_element_type=f32)


def _dot_tn(a, b):
    return lax.dot_general(a, b, (((0,), (0,)), ((), ())), preferred_element_type=f32)


def _sigmoid(x):
    return 0.5 * jnp.tanh(0.5 * x) + 0.5


def _softplus(x):
    return jnp.maximum(x, 0.0) + jnp.log(1.0 + jnp.exp(-jnp.abs(x)))


def _const_spec(shape):
    nd = len(shape)
    return pl.BlockSpec(shape, lambda *_: (0,) * nd)


def _resident_spec(shape):
    nd = len(shape)
    return pl.BlockSpec(shape, lambda *_: (0,) * nd, pipeline_mode=pl.Buffered(1))


def _t5_bucket_np(dist):
    max_exact = NUM_BUCKETS // 2
    d = np.maximum(dist, 1).astype(np.float32)
    log_b = max_exact + (np.log(d / np.float32(max_exact)) / np.float32(math.log(MAX_DISTANCE / max_exact))
                         * np.float32(NUM_BUCKETS - max_exact)).astype(np.int32)
    return np.where(dist < max_exact, dist, np.minimum(log_b, NUM_BUCKETS - 1)).astype(np.int32)


def _bucket_tables():
    w = SWA_BLK
    qi = np.arange(w)[:, None]
    kj = np.arange(w)[None, :]
    rel = np.where(kj <= qi, qi - kj, qi + w - kj)
    out = np.zeros((len(PATTERNS), w, w), np.int32)
    for p, (_, dil) in enumerate(PATTERNS):
        steps = _t5_bucket_np(np.arange(w + 1) * dil)
        assert steps[w] == steps[w - 1]
        out[p] = steps[rel]
    return out


def _bias_tables(rel_bias, bk):
    def body(rb_ref, bk_ref, o_ref):
        b_idx = bk_ref[0]
        for h in range(SWA_HEADS):
            def lp(b, acc):
                return jnp.where(b_idx == b, rb_ref[b, h], acc)
            o_ref[0, h] = lax.fori_loop(0, NUM_BUCKETS, lp, jnp.zeros((SWA_BLK, SWA_BLK), f32))

    return pl.pallas_call(
        body, name="bias_tables", grid=(3,),
        in_specs=[pl.BlockSpec(memory_space=pltpu.SMEM), pl.BlockSpec((1, SWA_BLK, SWA_BLK), lambda p: (p, 0, 0))],
        out_specs=pl.BlockSpec((1, SWA_HEADS, SWA_BLK, SWA_BLK), lambda p: (p, 0, 0, 0)),
        out_shape=SDS((3, SWA_HEADS, SWA_BLK, SWA_BLK), f32),
        compiler_params=_cp(1),
    )(rel_bias, bk)


def _rel_bias_grad(dsb, bk, bk_np):
    present = [sorted(set(int(v) for v in np.unique(bk_np[p]))) for p in range(3)]

    def body(ds_ref, bk_ref, o_ref):
        row = lax.broadcasted_iota(jnp.int32, (NUM_BUCKETS, LANE), 0)
        col = lax.broadcasted_iota(jnp.int32, (NUM_BUCKETS, SWA_HEADS), 1)
        out = jnp.zeros((NUM_BUCKETS, SWA_HEADS), f32)
        for hp in range(4):
            for hh in range(2):
                acc = jnp.zeros((NUM_BUCKETS, LANE), f32)
                for p in range(3):
                    tile = ds_ref[hp, p, hh]
                    b_idx = bk_ref[p]
                    for b in present[p]:
                        part = jnp.sum(jnp.where(b_idx == b, tile, 0.0), axis=0, keepdims=True)
                        acc = acc + jnp.where(row == b, part, 0.0)
                tot = jnp.sum(acc, axis=1, keepdims=True)
                out = out + jnp.where(col == 2 * hp + hh, tot, 0.0)
        o_ref[...] = out

    return pl.pallas_call(body, name="rel_bias_grad", out_shape=SDS((NUM_BUCKETS, SWA_HEADS), f32),
                          compiler_params=_cp(0, 32 * 1024 * 1024))(dsb, bk)


def _proj_fwd(x, g_pre, wcat, after, S):
    TS = 512

    def body(x_ref, g_ref, w_ref, after_ref, o_ref, h_ref):
        xv = x_ref[...]
        r = lax.rsqrt(jnp.mean(xv * xv, axis=-1, keepdims=True) + RMS_EPS)
        h = (xv * r * g_ref[...]).astype(bf16)
        h_ref[...] = h
        o_ref[...] = _dot(h, w_ref[...])

    return pl.pallas_call(
        body, name="proj_fwd", grid=(S // TS,),
        in_specs=[pl.BlockSpec((TS, D_MODEL), lambda i: (i, 0)), _const_spec((1, D_MODEL)),
                  _resident_spec((D_MODEL, NCOL)), _ANY],
        out_specs=[pl.BlockSpec((TS, NCOL), lambda i: (i, 0)), pl.BlockSpec((TS, D_MODEL), lambda i: (i, 0))],
        out_shape=[SDS((S, NCOL), f32), SDS((S, D_MODEL), bf16)],
        compiler_params=_cp(1, VMEM_LIMIT),
    )(x, g_pre, wcat, after)


CONV_RT = 256
HALO = 8


CONV_NC = CONV_RT // CHUNK


def _gdn_prep(proj, conv_w, S):
    def body(p_ref, cw_ref, o_ref, xs_ref):
        t = pl.program_id(0)
        xs_ref[pl.ds(0, HALO), :] = jnp.zeros((HALO, LANE), f32)
        xs_ref[pl.ds(HALO, S), :] = p_ref[...]
        w = cw_ref[...]
        is_qk = t < 2
        scale = jnp.where(t == 0, GDN_HD ** -0.5, 1.0).astype(f32)

        def lp(c, carry):
            st = pl.multiple_of(c * CONV_RT, CONV_RT)
            pre = xs_ref[pl.ds(st + HALO - 3, CONV_RT), :] * w[0:1, :]
            for i in range(1, 4):
                pre = pre + xs_ref[pl.ds(st + HALO - 3 + i, CONV_RT), :] * w[i:i + 1, :]
            s = pre * _sigmoid(pre)
            nrm = s * lax.rsqrt(jnp.sum(s * s, axis=-1, keepdims=True) + 1e-6) * scale
            out = jnp.where(is_qk, nrm, s)
            for i in range(CONV_NC):
                o_ref[0, c * CONV_NC + i, 0] = out[i * CHUNK:(i + 1) * CHUNK]
            return carry

        lax.fori_loop(0, S // CONV_RT, lp, 0)

    return pl.pallas_call(
        body, name="gdn_prep", grid=(3, GDN_HEADS),
        in_specs=[pl.BlockSpec((S, LANE), lambda t, h: (0, t * GDN_HEADS + h)),
                  pl.BlockSpec((4, LANE), lambda t, h: (0, t * GDN_HEADS + h))],
        out_specs=pl.BlockSpec((1, S // CHUNK, 1, CHUNK, GDN_HD), lambda t, h: (t, 0, h, 0, 0)),
        out_shape=SDS((3, S // CHUNK, GDN_HEADS, CHUNK, GDN_HD), f32),
        scratch_shapes=[pltpu.VMEM((S + HALO, LANE), f32)],
        compiler_params=_cp(2, VMEM_LIMIT),
    )(proj, conv_w)


def _gdn_prep_bwd(proj, conv_w, dqkv, S):
    def body(p_ref, cw_ref, d_ref, dx_ref, dw_ref, xs_ref, dp_ref):
        t = pl.program_id(0)
        xs_ref[pl.ds(0, HALO), :] = jnp.zeros((HALO, LANE), f32)
        xs_ref[pl.ds(HALO, S), :] = p_ref[...]
        dp_ref[pl.ds(S, HALO), :] = jnp.zeros((HALO, LANE), f32)
        w = cw_ref[...]
        is_qk = t < 2
        scale = jnp.where(t == 0, GDN_HD ** -0.5, 1.0).astype(f32)

        def lp1(c, dw):
            st = pl.multiple_of(c * CONV_RT, CONV_RT)
            taps = [xs_ref[pl.ds(st + HALO - 3 + i, CONV_RT), :] for i in range(4)]
            pre = taps[0] * w[0:1, :]
            for i in range(1, 4):
                pre = pre + taps[i] * w[i:i + 1, :]
            sg = _sigmoid(pre)
            s = pre * sg
            d_out = jnp.concatenate([d_ref[0, c * CONV_NC + i, 0] for i in range(CONV_NC)], axis=0)
            rn = lax.rsqrt(jnp.sum(s * s, axis=-1, keepdims=True) + 1e-6)
            n = s * rn
            dn = d_out * scale
            ds_qk = rn * (dn - n * jnp.sum(dn * n, axis=-1, keepdims=True))
            ds = jnp.where(is_qk, ds_qk, d_out)
            dpre = ds * (sg * (1.0 + pre * (1.0 - sg)))
            dp_ref[pl.ds(st, CONV_RT), :] = dpre
            return tuple(dw[i] + jnp.sum(dpre * taps[i], axis=0, keepdims=True) for i in range(4))

        z = jnp.zeros((1, LANE), f32)
        dw = lax.fori_loop(0, S // CONV_RT, lp1, (z, z, z, z))
        for i in range(4):
            dw_ref[pl.ds(i, 1), :] = dw[i]

        def lp2(c, carry):
            st = pl.multiple_of(c * CONV_RT, CONV_RT)
            dx = dp_ref[pl.ds(st, CONV_RT), :] * w[3:4, :]
            for i in range(3):
                dx = dx + dp_ref[pl.ds(st + 3 - i, CONV_RT), :] * w[i:i + 1, :]
            dx_ref[pl.ds(st, CONV_RT), :] = dx.astype(bf16)
            return carry

        lax.fori_loop(0, S // CONV_RT, lp2, 0)

    col = lambda rows: pl.BlockSpec((rows, LANE), lambda t, h: (0, t * GDN_HEADS + h))
    return pl.pallas_call(
        body, name="gdn_prep_bwd", grid=(3, GDN_HEADS),
        in_specs=[col(S), col(4), pl.BlockSpec((1, S // CHUNK, 1, CHUNK, GDN_HD), lambda t, h: (t, 0, h, 0, 0))],
        out_specs=[col(S), col(4)],
        out_shape=[SDS((S, 3 * GDN_W), bf16), SDS((4, 3 * GDN_W), f32)],
        scratch_shapes=[pltpu.VMEM((S + HALO, LANE), f32), pltpu.VMEM((S + HALO, LANE), f32)],
        compiler_params=_cp(2, VMEM_LIMIT),
    )(proj, conv_w, dqkv)


def _bdot(a, b, prec=None):
    return lax.dot_general(a, b, (((2,), (1,)), ((0,), (0,))), precision=prec, preferred_element_type=f32)


def _bdot_nt(a, b, prec=None):
    return lax.dot_general(a, b, (((2,), (2,)), ((0,), (0,))), precision=prec, preferred_element_type=f32)


def _bdot_tn(a, b, prec=None):
    return lax.dot_general(a, b, (((1,), (1,)), ((0,), (0,))), precision=prec, preferred_element_type=f32)


@jax.custom_vjp
def _tri_inv_saved(a, t):
    return t


def _tri_inv_saved_fwd(a, t):
    return t, t


def _tri_inv_saved_bwd(t, dt):
    return -_bdot_tn(t, _bdot_nt(dt, t, HIGH), HIGH), jnp.zeros_like(t)


_tri_inv_saved.defvjp(_tri_inv_saved_fwd, _tri_inv_saved_bwd)


def _gdn_intra(q, k, v, beta, g, t_saved=None):
    nb = q.shape[0]
    c = CHUNK
    ii = lax.broadcasted_iota(jnp.int32, (c, c), 0)
    jj = lax.broadcasted_iota(jnp.int32, (c, c), 1)
    eye = ii == jj
    tril = ii >= jj
    strict = ii > jj
    ones = jnp.ones((nb, c, c), f32)
    eye_f = eye.astype(f32)

    g_row = _bdot(ones, jnp.where(eye, g, 0.0), HIGH)
    gc = jnp.sum(jnp.where(tril, g_row, 0.0), axis=2, keepdims=True)
    gc_row = _bdot(ones, jnp.where(eye, gc, 0.0), HIGH)
    decay = jnp.where(tril, jnp.exp(jnp.where(tril, gc - gc_row, 0.0)), 0.0)
    last = lax.broadcasted_iota(jnp.int32, (c, 1), 0) == c - 1
    gc_last = jnp.sum(jnp.where(last, gc, 0.0), axis=1, keepdims=True)
    e_gc = jnp.exp(gc)

    kb = k * beta
    k16 = k.astype(bf16)
    a = jnp.where(strict, _bdot_nt(kb.astype(bf16), k16) * decay, 0.0)
    if t_saved is None:
        xp = -a
        t_inv = eye_f + xp
        for level in range(5):
            if level < 2:
                xp = _bdot(xp, xp, HIGH)
                t_inv = t_inv + _bdot(t_inv, xp, HIGH)
            else:
                x16 = xp.astype(bf16)
                xp = _bdot(x16, x16)
                t_inv = t_inv + _bdot(t_inv.astype(bf16), xp.astype(bf16))
    else:
        t_inv = _tri_inv_saved(a, t_saved)
    t16 = t_inv.astype(bf16)
    u = _bdot(t16, (v * beta).astype(bf16))
    w = _bdot(t16, (kb * e_gc).astype(bf16))
    attn = jnp.where(tril, _bdot_nt(q.astype(bf16), k16) * decay, 0.0)
    gam = jnp.broadcast_to(jnp.exp(gc_last), (nb, 1, GDN_HD))
    return u, w, attn, q * e_gc, k * jnp.exp(gc_last - gc), gam, t_inv


GDN_TB = 256
GDN_NC = GDN_TB // CHUNK
GDN_NU = GDN_NC * GDN_HEADS


def _gdn_gates(pg_ref, al_ref, db_ref):
    lane1 = lax.broadcasted_iota(jnp.int32, (1, LANE), 1)
    a_lane = jnp.zeros((1, LANE), f32)
    b_lane = jnp.zeros((1, LANE), f32)
    for h in range(GDN_HEADS):
        a_lane = jnp.where(lane1 == GDN_HEADS + h, al_ref[0, h], a_lane)
        b_lane = jnp.where(lane1 == GDN_HEADS + h, db_ref[0, h], b_lane)
    pg = pg_ref[...]
    z = pg + b_lane
    return _sigmoid(pg), -jnp.exp(a_lane) * _softplus(z), z, a_lane


def _gdn_unit_inputs(qkv_ref, beta_all, g_all):
    units = [(cl, h) for cl in range(GDN_NC) for h in range(GDN_HEADS)]
    beta = jnp.stack([beta_all[cl * CHUNK:(cl + 1) * CHUNK, h:h + 1] for cl, h in units])
    g = jnp.stack([g_all[cl * CHUNK:(cl + 1) * CHUNK, GDN_HEADS + h:GDN_HEADS + h + 1] for cl, h in units])
    return qkv_ref[0], qkv_ref[1], qkv_ref[2], beta, g


def _unit_spec(*tail):
    nd = len(tail)
    return pl.BlockSpec((GDN_NU,) + tail, lambda i: (i,) + (0,) * nd)


def _gdn_intra_shapes(S):
    nu = S // CHUNK * GDN_HEADS
    row = SDS((nu, CHUNK, GDN_HD), f32)
    return [row, row, SDS((nu, CHUNK, CHUNK), f32), row, row, SDS((nu, 1, GDN_HD), f32)]


_GDN_INTRA_SPECS = lambda: [_unit_spec(CHUNK, GDN_HD), _unit_spec(CHUNK, GDN_HD), _unit_spec(CHUNK, CHUNK),
                            _unit_spec(CHUNK, GDN_HD), _unit_spec(CHUNK, GDN_HD), _unit_spec(1, GDN_HD)]


def _gdn_intra_fwd(qkv_u, proj, a_log, dt_bias, S):
    def body(qkv_ref, pg_ref, al_ref, db_ref, *outs):
        beta_all, g_all, _, _ = _gdn_gates(pg_ref, al_ref, db_ref)
        res = _gdn_intra(*_gdn_unit_inputs(qkv_ref, beta_all, g_all))
        for o_ref, r in zip(outs, res):
            o_ref[...] = r

    nu = S // CHUNK * GDN_HEADS
    *intra, t_inv = pl.pallas_call(
        body, name="gdn_intra_fwd", grid=(S // GDN_TB,),
        in_specs=[pl.BlockSpec((3, GDN_NU, CHUNK, GDN_HD), lambda i: (0, i, 0, 0)),
                  pl.BlockSpec((GDN_TB, LANE), lambda i: (i, COL_G // LANE)),
                  pl.BlockSpec(memory_space=pltpu.SMEM), pl.BlockSpec(memory_space=pltpu.SMEM)],
        out_specs=_GDN_INTRA_SPECS() + [_unit_spec(CHUNK, CHUNK)],
        out_shape=_gdn_intra_shapes(S) + [SDS((nu, CHUNK, CHUNK), f32)],
        compiler_params=_cp(1, VMEM_LIMIT),
    )(qkv_u, proj, a_log, dt_bias)
    return intra, t_inv


def _gdn_intra_bwd(qkv_u, proj, a_log, dt_bias, t_inv, cots, S):
    def body(qkv_ref, pg_ref, al_ref, db_ref, t_ref, du_ref, dw_ref, da_ref, dqd_ref, dkd_ref, dgm_ref,
             dqkv_ref, dpg_ref, dal_ref, ddb_ref):
        @pl.when(pl.program_id(0) == 0)
        def _():
            dal_ref[...] = jnp.zeros_like(dal_ref)
            ddb_ref[...] = jnp.zeros_like(ddb_ref)

        t_saved = t_ref[...]
        beta_all, g_all, z, a_lane = _gdn_gates(pg_ref, al_ref, db_ref)
        _, vjp = jax.vjp(lambda *a: _gdn_intra(*a, t_saved=t_saved)[:6], *_gdn_unit_inputs(qkv_ref, beta_all, g_all))
        dq, dk, dv, dbeta, dg = vjp((du_ref[...], dw_ref[...], da_ref[...], dqd_ref[...], dkd_ref[...], dgm_ref[...]))
        dqkv_ref[0] = dq
        dqkv_ref[1] = dk
        dqkv_ref[2] = dv
        lane = lax.broadcasted_iota(jnp.int32, (CHUNK, LANE), 1)
        rows = []
        for cl in range(GDN_NC):
            t = jnp.zeros((CHUNK, LANE), f32)
            for h in range(GDN_HEADS):
                b = cl * GDN_HEADS + h
                t = t + jnp.where(lane == h, dbeta[b], 0.0) + jnp.where(lane == GDN_HEADS + h, dg[b], 0.0)
            rows.append(t)
        d_all = jnp.concatenate(rows, axis=0)
        is_beta = lax.broadcasted_iota(jnp.int32, (GDN_TB, LANE), 1) < GDN_HEADS
        dz = d_all * (-jnp.exp(a_lane)) * _sigmoid(z)
        dpg_ref[...] = jnp.where(is_beta, d_all * beta_all * (1.0 - beta_all), dz).astype(bf16)
        dal_ref[...] += jnp.sum(jnp.where(is_beta, 0.0, d_all * g_all), axis=0, keepdims=True)
        ddb_ref[...] += jnp.sum(jnp.where(is_beta, 0.0, dz), axis=0, keepdims=True)

    acc = _const_spec((1, LANE))
    nu = S // CHUNK * GDN_HEADS
    return pl.pallas_call(
        body, name="gdn_intra_bwd", grid=(S // GDN_TB,),
        in_specs=[pl.BlockSpec((3, GDN_NU, CHUNK, GDN_HD), lambda i: (0, i, 0, 0)),
                  pl.BlockSpec((GDN_TB, LANE), lambda i: (i, COL_G // LANE)),
                  pl.BlockSpec(memory_space=pltpu.SMEM), pl.BlockSpec(memory_space=pltpu.SMEM),
                  _unit_spec(CHUNK, CHUNK)] + _GDN_INTRA_SPECS(),
        out_specs=[pl.BlockSpec((3, GDN_NU, CHUNK, GDN_HD), lambda i: (0, i, 0, 0)),
                   pl.BlockSpec((GDN_TB, LANE), lambda i: (i, 0)), acc, acc],
        out_shape=[SDS((3, nu, CHUNK, GDN_HD), f32), SDS((S, LANE), bf16), SDS((1, LANE), f32), SDS((1, LANE), f32)],
        compiler_params=_cp(1, VMEM_LIMIT),
    )(qkv_u, proj, a_log, dt_bias, t_inv, *cots)


def _gdn_scan_fwd(intra, proj, onorm_g, S):
    def body(u_ref, w_ref, at_ref, qd_ref, kd_ref, gm_ref, gate_ref, og_ref, out_ref, st_ref, s_scr):
        @pl.when(pl.program_id(0) == 0)
        def _():
            s_scr[...] = jnp.zeros_like(s_scr)

        og = og_ref[...]
        s = s_scr[...]
        def out_mm(us, s16, vn16):
            return _bdot(qd_ref[us].astype(bf16), s16) + _bdot(at_ref[us].astype(bf16), vn16)

        outs, prev = [], None
        for cl in range(GDN_NC):
            us = slice(cl * GDN_HEADS, (cl + 1) * GDN_HEADS)
            st_ref[us] = s
            s16 = s.astype(bf16)
            ws = _bdot(w_ref[us].astype(bf16), s16)
            if prev is not None:
                outs.append(out_mm(*prev))
            vn16 = (u_ref[us] - ws).astype(bf16)
            prev = (us, s16, vn16)
            s = s * gm_ref[us] + _bdot_tn(kd_ref[us].astype(bf16), vn16)
        outs.append(out_mm(*prev))
        s_scr[...] = s
        for cl, o in enumerate(outs):
            rows = slice(cl * CHUNK, (cl + 1) * CHUNK)
            for h in range(GDN_HEADS):
                oh = o[h]
                gt = gate_ref[rows, h * GDN_HD:(h + 1) * GDN_HD]
                on = oh * lax.rsqrt(jnp.mean(oh * oh, axis=-1, keepdims=True) + RMS_EPS) * og
                out_ref[rows, h * GDN_HD:(h + 1) * GDN_HD] = on * (gt * _sigmoid(gt))

    nu = S // CHUNK * GDN_HEADS
    return pl.pallas_call(
        body, name="gdn_scan_fwd", grid=(S // GDN_TB,),
        in_specs=_GDN_INTRA_SPECS() + [pl.BlockSpec((GDN_TB, GDN_W), lambda i: (i, 3)), _const_spec((1, GDN_HD))],
        out_specs=[pl.BlockSpec((GDN_TB, GDN_W), lambda i: (i, 0)), _unit_spec(GDN_HD, GDN_HD)],
        out_shape=[SDS((S, GDN_W), f32), SDS((nu, GDN_HD, GDN_HD), f32)],
        scratch_shapes=[pltpu.VMEM((GDN_HEADS, GDN_HD, GDN_HD), f32)],
        compiler_params=_cp(1, VMEM_LIMIT),
    )(*intra, proj, onorm_g)


def _gdn_scan_bwd(intra, states, proj, d_oab, onorm_g, after, S):
    n_steps = S // GDN_TB

    def body(u_ref, w_ref, at_ref, qd_ref, kd_ref, gm_ref, st_ref, gate_ref, do_ref, og_ref, after_ref,
             du_ref, dw_ref, dat_ref, dqd_ref, dkd_ref, dgm_ref, dgate_ref, dog_ref, ds_scr):
        @pl.when(pl.program_id(0) == 0)
        def _():
            ds_scr[...] = jnp.zeros_like(ds_scr)
            dog_ref[...] = jnp.zeros_like(dog_ref)

        og = og_ref[...]
        ii = lax.broadcasted_iota(jnp.int32, (CHUNK, CHUNK), 0)
        jj = lax.broadcasted_iota(jnp.int32, (CHUNK, CHUNK), 1)
        tril = ii >= jj
        dog = jnp.zeros((1, GDN_HD), f32)
        pre = []
        for cl in range(GDN_NC):
            us = slice(cl * GDN_HEADS, (cl + 1) * GDN_HEADS)
            rows = slice(cl * CHUNK, (cl + 1) * CHUNK)
            s016 = st_ref[us].astype(bf16)
            w16 = w_ref[us].astype(bf16)
            qd16 = qd_ref[us].astype(bf16)
            at16 = at_ref[us].astype(bf16)
            vn16 = (u_ref[us] - _bdot(w16, s016)).astype(bf16)
            o = _bdot(qd16, s016) + _bdot(at16, vn16)
            do_h = []
            for h in range(GDN_HEADS):
                oh = o[h]
                lanes = slice(h * GDN_HD, (h + 1) * GDN_HD)
                gt = gate_ref[rows, lanes]
                d_out = do_ref[rows, lanes]
                r = lax.rsqrt(jnp.mean(oh * oh, axis=-1, keepdims=True) + RMS_EPS)
                n = oh * r
                sg = _sigmoid(gt)
                silu = gt * sg
                dog = dog + jnp.sum(d_out * n * silu, axis=0, keepdims=True)
                dgate_ref[rows, lanes] = (d_out * n * og * (sg * (1.0 + gt * (1.0 - sg)))).astype(bf16)
                dn = d_out * og * silu
                do_h.append(r * (dn - n * jnp.mean(dn * n, axis=-1, keepdims=True)))
            do16 = jnp.stack(do_h).astype(bf16)
            pre.append((us, s016, w16, vn16, do16, _bdot_tn(at16, do16), _bdot_tn(qd16, do16)))
        ds = ds_scr[...]
        chain = [None] * GDN_NC
        for cl in reversed(range(GDN_NC)):
            us, s016, w16, vn16, do16, at_do, qd_do = pre[cl]
            ds16 = ds.astype(bf16)
            dvn = at_do + _bdot(kd_ref[us].astype(bf16), ds16)
            dvn16 = dvn.astype(bf16)
            chain[cl] = (ds, ds16, dvn, dvn16)
            ds = qd_do + ds * gm_ref[us] - _bdot_tn(w16, dvn16)
        ds_scr[...] = ds
        for cl in range(GDN_NC):
            us, s016, w16, vn16, do16, _, _ = pre[cl]
            ds_in, ds16, dvn, dvn16 = chain[cl]
            du_ref[us] = dvn
            dw_ref[us] = -_bdot_nt(dvn16, s016)
            dat_ref[us] = jnp.where(tril, _bdot_nt(do16, vn16), 0.0)
            dqd_ref[us] = _bdot_nt(do16, s016)
            dkd_ref[us] = _bdot_nt(vn16, ds16)
            dgm_ref[us] = jnp.sum(st_ref[us] * ds_in, axis=1, keepdims=True)
        dog_ref[...] += dog

    def unit(*tail):
        nd = len(tail)
        return pl.BlockSpec((GDN_NU,) + tail, lambda i: (n_steps - 1 - i,) + (0,) * nd)

    intra_specs = [unit(CHUNK, GDN_HD), unit(CHUNK, GDN_HD), unit(CHUNK, CHUNK), unit(CHUNK, GDN_HD),
                   unit(CHUNK, GDN_HD), unit(1, GDN_HD)]
    tok = lambda c: pl.BlockSpec((GDN_TB, GDN_W), lambda i: (n_steps - 1 - i, c))
    return pl.pallas_call(
        body, name="gdn_scan_bwd", grid=(n_steps,),
        in_specs=intra_specs + [unit(GDN_HD, GDN_HD), tok(3), tok(0), _const_spec((1, GDN_HD)), _ANY],
        out_specs=intra_specs + [tok(0), _const_spec((1, GDN_HD))],
        out_shape=_gdn_intra_shapes(S) + [SDS((S, GDN_W), bf16), SDS((1, GDN_HD), f32)],
        scratch_shapes=[pltpu.VMEM((GDN_HEADS, GDN_HD, GDN_HD), f32)],
        compiler_params=_cp(1, VMEM_LIMIT),
    )(*intra, states, proj, d_oab, onorm_g, after)


SWA_UNROLL = 8


def _swa_tiles(q_ref, k_ref, v_ref, it, d, nb_log2, S):
    nb = 1 << nb_log2
    r = lax.shift_right_logical(it, nb_log2)
    blk = lax.bitwise_and(it, nb - 1)
    qs = blk * (SWA_BLK * d) + r
    ps = jnp.maximum(blk - 1, 0) * (SWA_BLK * d) + r
    if d > 1:
        rows_c, rows_p = pl.ds(qs, SWA_BLK, stride=d), pl.ds(ps, SWA_BLK, stride=d)
    else:
        rows_c, rows_p = pl.ds(pl.multiple_of(qs, SWA_BLK), SWA_BLK), pl.ds(pl.multiple_of(ps, SWA_BLK), SWA_BLK)
    return rows_c, rows_p, blk > 0


def _swa_prev_modes(nb):
    if nb >= SWA_UNROLL:
        return ["load"] + ["reuse"] * (SWA_UNROLL - 1)
    return ["none" if u % nb == 0 else "reuse" for u in range(SWA_UNROLL)]


def _swa_fwd(proj, bt, S):
    scale = SWA_HD ** -0.5

    def body(q_ref, k_ref, v_ref, bt_ref, o_ref, lse0_ref, lse1_ref, m0_scr, m1_scr, a0_scr, a1_scr):
        lane = lax.broadcasted_iota(jnp.int32, (SWA_BLK, LANE), 1)
        h0 = lane < SWA_HD
        qi = lax.broadcasted_iota(jnp.int32, (SWA_BLK, SWA_BLK), 0)
        kj = lax.broadcasted_iota(jnp.int32, (SWA_BLK, SWA_BLK), 1)
        lower = kj <= qi
        ones16 = jnp.ones((LANE, SWA_BLK), bf16)
        m_scrs = (m0_scr, m1_scr)
        a_scrs = (a0_scr, a1_scr)
        for p, (_, d) in reversed(list(enumerate(PATTERNS))):
            nb_log2 = int(math.log2(S // d // SWA_BLK))
            first = p == len(PATTERNS) - 1

            def lp(i, carry, p=p, d=d, nb_log2=nb_log2, first=first):
                heads = [h0, jnp.logical_not(h0)]
                modes = _swa_prev_modes(1 << nb_log2)
                tiles = []
                kc_f = None
                for u in range(SWA_UNROLL):
                    rows_c, rows_p, has_prev = _swa_tiles(q_ref, k_ref, v_ref, i * SWA_UNROLL + u, d, nb_log2, S)
                    kp_f = {"load": lambda: k_ref[rows_p, :], "reuse": lambda: kc_f, "none": lambda: None}[modes[u]]()
                    has_prev = {"load": has_prev, "reuse": True, "none": False}[modes[u]]
                    q = q_ref[rows_c, :]
                    kc_f = k_ref[rows_c, :]
                    kc = kc_f.astype(bf16)
                    logits = []
                    for mh in heads:
                        q_h = jnp.where(mh, q, 0.0)
                        qh = q_h.astype(bf16)
                        if kp_f is None:
                            logits.append((_dot_nt(qh, kc), None, None))
                        else:
                            logits.append((_dot_nt(qh, kc), _dot_nt(qh, kp_f.astype(bf16)), _dot((q_h * kp_f).astype(bf16), ones16)))
                    tiles.append((rows_c, rows_p, has_prev, logits))
                probs = []
                for rows_c, rows_p, has_prev, logits in tiles:
                    per_head = []
                    for h, (s_c, s_p, far) in enumerate(logits):
                        if has_prev is False:
                            s = jnp.where(lower, s_c * scale + bt_ref[p, h], NEG)
                            s_far = None
                        else:
                            s = jnp.where(lower, s_c, s_p) * scale + bt_ref[p, h]
                            s_far = far * scale + bt_ref[p, h, SWA_BLK - 1:SWA_BLK, 0:1]
                            if has_prev is not True:
                                s = jnp.where(jnp.logical_or(lower, has_prev), s, NEG)
                                s_far = jnp.where(has_prev, s_far, NEG)
                        mn = jnp.max(s, axis=1, keepdims=True)
                        if s_far is not None:
                            mn = jnp.maximum(s_far, mn)
                        alpha = None
                        if not first:
                            mo = m_scrs[h][rows_c, :]
                            mn = jnp.maximum(mo, mn)
                            alpha = jnp.exp(mo - mn)
                        mn = jnp.broadcast_to(mn, (SWA_BLK, LANE))
                        pm = jnp.exp(s - mn)
                        per_head.append((mn, alpha, None if s_far is None else jnp.exp(s_far - mn),
                                         jnp.where(lower, pm, 0.0).astype(bf16),
                                         None if s_far is None else jnp.where(lower, 0.0, pm).astype(bf16)))
                    probs.append(per_head)
                acc_old = [None if first else (a0_scr[t[0], :], a1_scr[t[0], :]) for t in tiles]
                done = []
                vc = None
                for u, ((rows_c, rows_p, _, _), per_head, old) in enumerate(zip(tiles, probs, acc_old)):
                    vp = {"load": lambda: v_ref[rows_p, :], "reuse": lambda: vc, "none": lambda: None}[modes[u]]()
                    vc = v_ref[rows_c, :]
                    acc_new = []
                    for h, (mn, alpha, p_far, pc16, pp16) in enumerate(per_head):
                        pv = _dot(pc16, jnp.where(heads[h], vc, 1.0).astype(bf16))
                        if pp16 is not None:
                            vpa = jnp.where(heads[h], vp, 1.0)
                            pv = pv + _dot(pp16, vpa.astype(bf16)) + p_far * vpa
                        acc_new.append(pv if first else alpha * old[h] + pv)
                    done.append((rows_c, per_head[0][0], per_head[1][0], acc_new[0], acc_new[1]))
                for rows_c, m0_new, m1_new, a0_new, a1_new in done:
                    m0_scr[rows_c, :] = m0_new
                    m1_scr[rows_c, :] = m1_new
                    a0_scr[rows_c, :] = a0_new
                    a1_scr[rows_c, :] = a1_new
                return carry

            lax.fori_loop(0, S // SWA_BLK // SWA_UNROLL, lp, 0)

        def fin(c, carry):
            rows = pl.ds(pl.multiple_of(c * SWA_BLK, SWA_BLK), SWA_BLK)
            a0 = a0_scr[rows, :]
            a1 = a1_scr[rows, :]
            l0 = jnp.where(h0, pltpu.roll(a0, SWA_HD, 1), a0)
            l1 = jnp.where(h0, a1, pltpu.roll(a1, SWA_HD, 1))
            o_ref[rows, :] = jnp.where(h0, a0 / l0, a1 / l1)
            lse0_ref[rows, :] = m0_scr[rows, :] + jnp.log(l0)
            lse1_ref[rows, :] = m1_scr[rows, :] + jnp.log(l1)
            return carry

        lax.fori_loop(0, S // SWA_BLK, fin, 0)

    qb = COL_B // LANE
    col = lambda c: pl.BlockSpec((S, LANE), lambda hp, c=c: (0, c + hp))
    return pl.pallas_call(
        body, name="swa_fwd", grid=(4,),
        in_specs=[col(qb), col(qb + 4), col(qb + 8), pl.BlockSpec((3, 2, SWA_BLK, SWA_BLK), lambda hp: (0, hp, 0, 0))],
        out_specs=[col(0), col(0), col(0)],
        out_shape=[SDS((S, SWA_W), f32)] * 3,
        scratch_shapes=[pltpu.VMEM((S, LANE), f32)] * 4,
        compiler_params=_cp(1, VMEM_LIMIT),
    )(proj, proj, proj, bt)


def _swa_bwd(proj, bt, nd, lse0, lse1, d_oab, after, S):
    scale = SWA_HD ** -0.5

    def body(q_ref, k_ref, v_ref, bt_ref, nd_scr, lse0_ref, lse1_ref, do_ref, after_ref, dq_ref, dk_ref, dv_ref, dsb_ref,
             dq_scr, dk_scr, dv_scr):
        lane = lax.broadcasted_iota(jnp.int32, (SWA_BLK, LANE), 1)
        h0 = lane < SWA_HD
        qi = lax.broadcasted_iota(jnp.int32, (SWA_BLK, SWA_BLK), 0)
        kj = lax.broadcasted_iota(jnp.int32, (SWA_BLK, SWA_BLK), 1)
        lower = kj <= qi
        eye = kj == qi
        rel127 = jnp.logical_or(kj == qi + 1, jnp.logical_and(qi == SWA_BLK - 1, kj == 0))
        ones16 = jnp.ones((LANE, SWA_BLK), bf16)
        lse_refs = (lse0_ref, lse1_ref)
        dk_scr[...] = jnp.zeros((S, LANE), f32)
        dv_scr[...] = jnp.zeros((S, LANE), f32)
        dsb_ref[...] = jnp.zeros_like(dsb_ref)

        for p, (_, d) in reversed(list(enumerate(PATTERNS))):
            nb_log2 = int(math.log2(S // d // SWA_BLK))
            first = p == len(PATTERNS) - 1

            def lp(i, carry, p=p, d=d, nb_log2=nb_log2, first=first):
                heads = [h0, jnp.logical_not(h0)]
                modes = _swa_prev_modes(1 << nb_log2)
                tiles = []
                kc_f = vc_f = None
                for u in range(SWA_UNROLL):
                    rows_c, rows_p, has_prev = _swa_tiles(q_ref, k_ref, v_ref, i * SWA_UNROLL + u, d, nb_log2, S)
                    kp_f = {"load": lambda: k_ref[rows_p, :], "reuse": lambda: kc_f, "none": lambda: None}[modes[u]]()
                    vp_f = {"load": lambda: v_ref[rows_p, :], "reuse": lambda: vc_f, "none": lambda: None}[modes[u]]()
                    has_prev = {"load": has_prev, "reuse": True, "none": False}[modes[u]]
                    q = q_ref[rows_c, :]
                    kc_f = k_ref[rows_c, :]
                    vc_f = v_ref[rows_c, :]
                    kc = kc_f.astype(bf16)
                    kp = None if kp_f is None else kp_f.astype(bf16)
                    do = do_ref[rows_c, :]
                    nd = nd_scr[rows_c, :]
                    per_head = []
                    for mh in heads:
                        q_h = jnp.where(mh, q, 0.0)
                        do_a = jnp.where(mh, do, nd)
                        qh = q_h.astype(bf16)
                        doa = do_a.astype(bf16)
                        doh = jnp.where(mh, do, 0.0).astype(bf16)
                        dd_c = _dot_nt(doa, jnp.where(mh, vc_f, 1.0).astype(bf16))
                        if kp_f is None:
                            per_head.append((qh, doh, _dot_nt(qh, kc), None, None, dd_c, None, None))
                        else:
                            vpa = jnp.where(mh, vp_f, 1.0)
                            per_head.append((qh, doh, _dot_nt(qh, kc), _dot_nt(qh, kp), _dot((q_h * kp_f).astype(bf16), ones16),
                                             dd_c, _dot_nt(doa, vpa.astype(bf16)), _dot((do_a * vpa).astype(bf16), ones16)))
                    tiles.append((rows_c, rows_p, has_prev, kc, kp, per_head))
                grads = []
                for rows_c, rows_p, has_prev, kc, kp, per_head in tiles:
                    out = []
                    for h, (qh, doh, s_c, s_p, far, dd_c, dd_p, dd_far) in enumerate(per_head):
                        lse_h = lse_refs[h][rows_c, :]
                        if has_prev is False:
                            pm = jnp.exp(jnp.where(lower, s_c * scale + bt_ref[p, h], NEG) - lse_h)
                            dsm = pm * dd_c
                            out.append((dsm, dsm.astype(bf16), None, pm.astype(bf16), None))
                            continue
                        s = jnp.where(lower, s_c, s_p) * scale + bt_ref[p, h]
                        s_far = far * scale + bt_ref[p, h, SWA_BLK - 1:SWA_BLK, 0:1]
                        if has_prev is not True:
                            s = jnp.where(jnp.logical_or(lower, has_prev), s, NEG)
                            s_far = jnp.where(has_prev, s_far, NEG)
                        pm = jnp.exp(s - lse_h)
                        p_far = jnp.exp(s_far - lse_h)
                        dsm = pm * jnp.where(lower, dd_c, dd_p)
                        ds_far = p_far * dd_far
                        out.append((dsm + jnp.where(rel127, ds_far, 0.0),
                                    jnp.where(lower, dsm, 0.0).astype(bf16),
                                    jnp.where(lower, jnp.where(eye, ds_far, 0.0), dsm).astype(bf16),
                                    jnp.where(lower, pm, 0.0).astype(bf16),
                                    jnp.where(lower, jnp.where(eye, p_far, 0.0), pm).astype(bf16)))
                    grads.append(out)
                done = []
                add = lambda acc, t: t if acc is None else acc + t
                for (rows_c, rows_p, _, kc, kp, per_head), out in zip(tiles, grads):
                    dq_t = dkc_t = dkp_t = dvc_t = dvp_t = None
                    for h, (_, dsc16, dsp16, pc16, pp16) in enumerate(out):
                        qh, doh = per_head[h][0], per_head[h][1]
                        dq_h = _dot(dsc16, kc)
                        dkc_t = add(dkc_t, _dot_tn(dsc16, qh) * scale)
                        dvc_t = add(dvc_t, _dot_tn(pc16, doh))
                        if dsp16 is not None:
                            dq_h = dq_h + _dot(dsp16, kp)
                            dkp_t = add(dkp_t, _dot_tn(dsp16, qh) * scale)
                            dvp_t = add(dvp_t, _dot_tn(pp16, doh))
                        dq_t = add(dq_t, jnp.where(heads[h], dq_h * scale, 0.0))
                    done.append([rows_c, rows_p, dq_t, dkc_t, dkp_t, dvc_t, dvp_t])
                for u in range(1, SWA_UNROLL):
                    if modes[u] == "reuse":
                        done[u - 1][3] = done[u - 1][3] + done[u][4]
                        done[u - 1][5] = done[u - 1][5] + done[u][6]
                for h in range(2):
                    tot = grads[0][h][0]
                    for g in grads[1:]:
                        tot = tot + g[h][0]
                    dsb_ref[0, p, h] += tot
                for u, (rows_c, rows_p, dq_t, dkc_t, dkp_t, dvc_t, dvp_t) in enumerate(done):
                    dq_scr[rows_c, :] = dq_t if first else dq_scr[rows_c, :] + dq_t
                    dk_scr[rows_c, :] = dk_scr[rows_c, :] + dkc_t
                    dv_scr[rows_c, :] = dv_scr[rows_c, :] + dvc_t
                    if modes[u] == "load":
                        dk_scr[rows_p, :] = dk_scr[rows_p, :] + dkp_t
                        dv_scr[rows_p, :] = dv_scr[rows_p, :] + dvp_t
                return carry

            lax.fori_loop(0, S // SWA_BLK // SWA_UNROLL, lp, 0)
        dq_ref[...] = dq_scr[...].astype(bf16)
        dk_ref[...] = dk_scr[...].astype(bf16)
        dv_ref[...] = dv_scr[...].astype(bf16)

    qb = COL_B // LANE
    col = lambda c: pl.BlockSpec((S, LANE), lambda hp, c=c: (0, c + hp))
    return pl.pallas_call(
        body, name="swa_bwd", grid=(4,),
        in_specs=[col(qb), col(qb + 4), col(qb + 8),
                  pl.BlockSpec((3, 2, SWA_BLK, SWA_BLK), lambda hp: (0, hp, 0, 0)),
                  col(0), col(0), col(0), col(4), _ANY],
        out_specs=[col(0), col(0), col(0),
                   pl.BlockSpec((1, 3, 2, SWA_BLK, SWA_BLK), lambda hp: (hp, 0, 0, 0, 0))],
        out_shape=[SDS((S, SWA_W), bf16)] * 3 + [SDS((4, 3, 2, SWA_BLK, SWA_BLK), f32)],
        scratch_shapes=[pltpu.VMEM((S, LANE), f32)] * 3,
        compiler_params=_cp(1, VMEM_LIMIT),
    )(proj, proj, proj, bt, nd, lse0, lse1, d_oab, after)


def _mix_fwd(oa, ob, w_out, x, g_post, S):
    TS = 512

    def body(oa_ref, ob_ref, w_ref, x_ref, g_ref, mix_ref, x1_ref):
        mix = _dot(oa_ref[...].astype(bf16), w_ref[0:GDN_W, :]) + _dot(ob_ref[...].astype(bf16), w_ref[GDN_W:D_MODEL, :])
        r = lax.rsqrt(jnp.mean(mix * mix, axis=-1, keepdims=True) + RMS_EPS)
        mix_ref[...] = mix
        x1_ref[...] = x_ref[...] + mix * r * g_ref[...]

    row = lambda w: pl.BlockSpec((TS, w), lambda i: (i, 0))
    return pl.pallas_call(
        body, name="mix_fwd", grid=(S // TS,),
        in_specs=[row(GDN_W), row(SWA_W), _resident_spec((D_MODEL, D_MODEL)), row(D_MODEL), _const_spec((1, D_MODEL))],
        out_specs=[row(D_MODEL), row(D_MODEL)],
        out_shape=[SDS((S, D_MODEL), f32), SDS((S, D_MODEL), f32)],
        compiler_params=_cp(1, VMEM_LIMIT),
    )(oa, ob, w_out, x, g_post)


def _mix_bwd(dx1, mix, g_post, w_out, ob, S):
    TS = 512

    def body(dx1_ref, mix_ref, g_ref, w_ref, ob_ref, dmix_ref, doab_ref, dg_ref, nd_ref):
        @pl.when(pl.program_id(0) == 0)
        def _():
            dg_ref[...] = jnp.zeros_like(dg_ref)

        mix = mix_ref[...]
        dz = dx1_ref[...]
        r = lax.rsqrt(jnp.mean(mix * mix, axis=-1, keepdims=True) + RMS_EPS)
        n = mix * r
        dg_ref[...] += jnp.sum(dz * n, axis=0, keepdims=True)
        dn = dz * g_ref[...]
        dmix = (r * (dn - n * jnp.mean(dn * n, axis=-1, keepdims=True))).astype(bf16)
        dmix_ref[...] = dmix
        doab = _dot_nt(dmix, w_ref[...])
        doab_ref[...] = doab
        hi_ = lax.shift_right_logical(lax.broadcasted_iota(jnp.int32, (SWA_W, SWA_W), 0), 6)
        hj_ = lax.shift_right_logical(lax.broadcasted_iota(jnp.int32, (SWA_W, SWA_W), 1), 6)
        swap = (hi_ == lax.bitwise_xor(hj_, 1)).astype(bf16)
        dlt = doab[:, GDN_W:] * ob_ref[...]
        hi = dlt.astype(bf16)
        nd_ref[...] = (_dot(hi, swap) + _dot((dlt - hi.astype(f32)).astype(bf16), swap)) * (-1.0 / SWA_HD)

    row = lambda w=D_MODEL: pl.BlockSpec((TS, w), lambda i: (i, 0))
    return pl.pallas_call(
        body, name="mix_bwd", grid=(S // TS,),
        in_specs=[row(), row(), _const_spec((1, D_MODEL)), _resident_spec((D_MODEL, D_MODEL)), row(SWA_W)],
        out_specs=[row(), row(), _const_spec((1, D_MODEL)), row(SWA_W)],
        out_shape=[SDS((S, D_MODEL), bf16), SDS((S, D_MODEL), f32), SDS((1, D_MODEL), f32), SDS((S, SWA_W), f32)],
        compiler_params=_cp(1, VMEM_LIMIT),
    )(dx1, mix, g_post, w_out, ob)


FFN_TS = 256
FFN_CH = 1408


def _ffn(x1, tgt, g_pre, g_post, wg, wu, wd, S):
    def body(x1_ref, t_ref, gp_ref, gq_ref, wg_ref, wu_ref, wd_ref,
             dx1_ref, h2_ref, act_ref, dgate_ref, dup_ref, df_ref, loss_ref, dgp_ref, dgq_ref, gate_scr, up_scr):
        @pl.when(pl.program_id(0) == 0)
        def _():
            loss_ref[...] = jnp.zeros_like(loss_ref)
            dgp_ref[...] = jnp.zeros_like(dgp_ref)
            dgq_ref[...] = jnp.zeros_like(dgq_ref)

        x1v = x1_ref[...]
        gp = gp_ref[...]
        gq = gq_ref[...]
        r2 = lax.rsqrt(jnp.mean(x1v * x1v, axis=-1, keepdims=True) + RMS_EPS)
        n2 = x1v * r2
        h2 = (n2 * gp).astype(bf16)
        h2_ref[...] = h2
        chunks = [slice(c * FFN_CH, (c + 1) * FFN_CH) for c in range(D_FF // FFN_CH)]
        for cs in chunks:
            gate_scr[:, cs] = _dot_nt(h2, wg_ref[cs, :])
            up_scr[:, cs] = _dot_nt(h2, wu_ref[cs, :])
        acts = []
        for cs in chunks:
            gate = gate_scr[:, cs]
            act = (gate * _sigmoid(gate) * up_scr[:, cs]).astype(bf16)
            act_ref[:, cs] = act
            acts.append(act)
        f = _dot(acts[0], wd_ref[chunks[0], :])
        for act, cs in zip(acts[1:], chunks[1:]):
            f = f + _dot(act, wd_ref[cs, :])
        r3 = lax.rsqrt(jnp.mean(f * f, axis=-1, keepdims=True) + RMS_EPS)
        n3 = f * r3
        err = x1v + n3 * gq - t_ref[...]
        loss_ref[...] += 0.5 * jnp.sum(jnp.mean(err * err, axis=-1, keepdims=True), axis=0, keepdims=True)
        dy = err * (1.0 / D_MODEL)
        dgq_ref[...] += jnp.sum(dy * n3, axis=0, keepdims=True)
        dn3 = dy * gq
        df = (r3 * (dn3 - n3 * jnp.mean(dn3 * n3, axis=-1, keepdims=True))).astype(bf16)
        df_ref[...] = df
        dacts = [_dot_nt(df, wd_ref[cs, :]) for cs in chunks]
        dgs = []
        for dact, cs in zip(dacts, chunks):
            gate = gate_scr[:, cs]
            sg = _sigmoid(gate)
            dup = (dact * gate * sg).astype(bf16)
            dgate = (dact * up_scr[:, cs] * (sg * (1.0 + gate * (1.0 - sg)))).astype(bf16)
            dup_ref[:, cs] = dup
            dgate_ref[:, cs] = dgate
            dgs.append((dgate, dup))
        dh2 = None
        for (dgate, dup), cs in zip(dgs, chunks):
            t = _dot(dgate, wg_ref[cs, :]) + _dot(dup, wu_ref[cs, :])
            dh2 = t if dh2 is None else dh2 + t
        dgp_ref[...] += jnp.sum(dh2 * n2, axis=0, keepdims=True)
        dn2 = dh2 * gp
        dx1_ref[...] = dy + r2 * (dn2 - n2 * jnp.mean(dn2 * n2, axis=-1, keepdims=True))

    row = lambda w: pl.BlockSpec((FFN_TS, w), lambda i: (i, 0))
    vec = _const_spec((1, D_MODEL))
    return pl.pallas_call(
        body, name="ffn_fwd_bwd", grid=(S // FFN_TS,),
        in_specs=[row(D_MODEL), row(D_MODEL), vec, vec, _resident_spec((D_FF, D_MODEL)), _resident_spec((D_FF, D_MODEL)),
                  _resident_spec((D_FF, D_MODEL))],
        out_specs=[row(D_MODEL), row(D_MODEL), row(D_FF), row(D_FF), row(D_FF), row(D_MODEL), _const_spec((1, LANE)), vec, vec],
        out_shape=[SDS((S, D_MODEL), f32), SDS((S, D_MODEL), bf16), SDS((S, D_FF), bf16), SDS((S, D_FF), bf16),
                   SDS((S, D_FF), bf16), SDS((S, D_MODEL), bf16), SDS((1, LANE), f32), SDS((1, D_MODEL), f32),
                   SDS((1, D_MODEL), f32)],
        scratch_shapes=[pltpu.VMEM((FFN_TS, D_FF), f32), pltpu.VMEM((FFN_TS, D_FF), f32)],
        compiler_params=_cp(1, VMEM_LIMIT),
    )(x1, tgt, g_pre, g_post, wg, wu, wd)


def _proj_bwd(x, dx1, g_pre, wcat, segs, after, S):
    TS = 512
    n = len(segs)
    cols = [(c0, a.shape[1]) for a, c0 in segs]

    def body(*refs):
        x_ref, dx1_ref, g_ref, w_ref = refs[:4]
        seg_refs = refs[4:4 + n]
        gx_ref, dg_ref = refs[5 + n:]

        @pl.when(pl.program_id(0) == 0)
        def _():
            dg_ref[...] = jnp.zeros_like(dg_ref)

        dh = jnp.zeros((TS, D_MODEL), f32)
        for s_ref, (c0, w) in zip(seg_refs, cols):
            dh = dh + _dot_nt(s_ref[...], w_ref[:, c0:c0 + w])
        xv = x_ref[...]
        g = g_ref[...]
        r = lax.rsqrt(jnp.mean(xv * xv, axis=-1, keepdims=True) + RMS_EPS)
        nx = xv * r
        dg_ref[...] += jnp.sum(dh * nx, axis=0, keepdims=True)
        dn = dh * g
        gx_ref[...] = dx1_ref[...] + r * (dn - nx * jnp.mean(dn * nx, axis=-1, keepdims=True))

    row = lambda w: pl.BlockSpec((TS, w), lambda i: (i, 0))
    return pl.pallas_call(
        body, name="proj_bwd", grid=(S // TS,),
        in_specs=[row(D_MODEL), row(D_MODEL), _const_spec((1, D_MODEL)), _resident_spec((D_MODEL, NCOL))]
                 + [row(w) for _, w in cols] + [_ANY],
        out_specs=[row(D_MODEL), _const_spec((1, D_MODEL))],
        out_shape=[SDS((S, D_MODEL), f32), SDS((1, D_MODEL), f32)],
        compiler_params=_cp(1, VMEM_LIMIT),
    )(x, dx1, g_pre, wcat, *[a for a, _ in segs], after)


def _wgrad(a, b, S, name):
    TS = 1024
    K = a.shape[1]
    N = b.shape[1]
    TN = next(t for t in (512, 1408, N) if N % t == 0)

    def body(a_ref, b_ref, o_ref, acc):
        @pl.when(pl.program_id(1) == 0)
        def _():
            acc[...] = jnp.zeros_like(acc)

        acc[...] += _dot_tn(a_ref[...].astype(bf16), b_ref[...])

        @pl.when(pl.program_id(1) == pl.num_programs(1) - 1)
        def _():
            o_ref[...] = acc[...].astype(bf16)

    return pl.pallas_call(
        body, name=name, grid=(N // TN, S // TS),
        in_specs=[pl.BlockSpec((TS, K), lambda j, s: (s, 0)), pl.BlockSpec((TS, TN), lambda j, s: (s, j))],
        out_specs=pl.BlockSpec((K, TN), lambda j, s: (0, j)), out_shape=SDS((K, N), bf16),
        scratch_shapes=[pltpu.VMEM((K, TN), f32)],
        compiler_params=_cp(2, VMEM_LIMIT),
    )(a, b)


def _wgrad_out(oa, ob, dmix, S):
    TS, TN = 1024, 512

    def body(a_ref, b_ref, d_ref, o_ref, acc):
        @pl.when(pl.program_id(1) == 0)
        def _():
            acc[...] = jnp.zeros_like(acc)

        d = d_ref[...]
        acc[0:GDN_W, :] += _dot_tn(a_ref[...].astype(bf16), d)
        acc[GDN_W:D_MODEL, :] += _dot_tn(b_ref[...].astype(bf16), d)

        @pl.when(pl.program_id(1) == pl.num_programs(1) - 1)
        def _():
            o_ref[...] = acc[...].astype(bf16)

    tok = lambda w: pl.BlockSpec((TS, w), lambda j, s: (s, 0))
    return pl.pallas_call(
        body, name="wgrad_out", grid=(D_MODEL // TN, S // TS),
        in_specs=[tok(GDN_W), tok(SWA_W), pl.BlockSpec((TS, TN), lambda j, s: (s, j))],
        out_specs=pl.BlockSpec((D_MODEL, TN), lambda j, s: (0, j)), out_shape=SDS((D_MODEL, D_MODEL), bf16),
        scratch_shapes=[pltpu.VMEM((D_MODEL, TN), f32)],
        compiler_params=_cp(2, VMEM_LIMIT),
    )(oa, ob, dmix)


def _w_in_pieces():
    n_a, n_g = 4 * GDN_W, 2 * GDN_HEADS
    cb = IN_COLS // N_DEV
    bounds = [(0, n_a, COL_A), (n_a, n_a + n_g, COL_G), (n_a + n_g, IN_COLS, COL_B)]
    out = []
    for j in range(N_DEV):
        lo, hi = j * cb, (j + 1) * cb
        for s0, s1, dst in bounds:
            a, b = max(lo, s0), min(hi, s1)
            if a < b:
                out.append((j, a - lo, b - a, dst + a - s0))
    return out


def _wcat_from_blocks(g_in):
    TR = 256
    cb = IN_COLS // N_DEV
    pieces = _w_in_pieces()

    def body(w_ref, o_ref):
        o_ref[:, COL_G:NCOL] = jnp.zeros((TR, NCOL - COL_G), bf16)
        for j, off, w, dst in pieces:
            o_ref[:, dst:dst + w] = w_ref[j, :, off:off + w]

    return pl.pallas_call(
        body, name="wcat_from_blocks", grid=(D_MODEL // TR,),
        in_specs=[pl.BlockSpec((N_DEV, TR, cb), lambda i: (0, i, 0))],
        out_specs=pl.BlockSpec((TR, NCOL), lambda i: (i, 0)),
        out_shape=SDS((D_MODEL, NCOL), bf16),
        compiler_params=_cp(1, VMEM_LIMIT),
    )(g_in)


def _wgrad_in(h1, segs, S):
    TS = 1024
    n = len(segs)
    cols = [(c0, a.shape[1]) for a, c0 in segs]
    cb = IN_COLS // N_DEV
    pieces = _w_in_pieces()

    def body(*refs):
        h_ref = refs[0]
        seg_refs = refs[1:1 + n]
        o_ref, acc = refs[1 + n], refs[2 + n]

        @pl.when(pl.program_id(0) == 0)
        def _():
            acc[...] = jnp.zeros_like(acc)

        h = h_ref[...]
        for s_ref, (c0, w) in zip(seg_refs, cols):
            acc[:, c0:c0 + w] += _dot_tn(h, s_ref[...])

        @pl.when(pl.program_id(0) == pl.num_programs(0) - 1)
        def _():
            for j, off, w, src in pieces:
                o_ref[j, :, off:off + w] = acc[:, src:src + w].astype(bf16)

    row = lambda w: pl.BlockSpec((TS, w), lambda i: (i, 0))
    return pl.pallas_call(
        body, name="wgrad_in", grid=(S // TS,),
        in_specs=[row(D_MODEL)] + [row(w) for _, w in cols],
        out_specs=_const_spec((N_DEV, D_MODEL, cb)),
        out_shape=SDS((N_DEV, D_MODEL, cb), bf16),
        scratch_shapes=[pltpu.VMEM((D_MODEL, NCOL), f32)],
        compiler_params=_cp(1, VMEM_LIMIT),
    )(h1, *[a for a, _ in segs])


def _adamw(recv, src, me, w, m, v, name):
    R, C = w.shape
    TR = 256 if R % 256 == 0 else R
    c1 = 1.0 / (1.0 - ADAM_B1 ** ADAM_STEP)
    c2 = 1.0 / (1.0 - ADAM_B2 ** ADAM_STEP)

    def body(me_ref, r_ref, own_ref, w_ref, m_ref, v_ref, g_out, d_out, m_out, v_out):
        g = None
        for s in range(N_DEV):
            t = jnp.where(me_ref[0] == s, own_ref[0], r_ref[s]).astype(f32)
            g = t if g is None else g + t
        mn = ADAM_B1 * m_ref[...] + (1.0 - ADAM_B1) * g
        vn = ADAM_B2 * v_ref[...] + (1.0 - ADAM_B2) * (g * g)
        g_out[...] = g
        m_out[...] = mn
        v_out[...] = vn
        d_out[...] = -ADAM_LR * ((mn * c1) / (jnp.sqrt(vn * c2) + ADAM_EPS) + ADAM_WD * w_ref[...])

    blk = pl.BlockSpec((TR, C), lambda i, me_ref: (i, 0))
    return pl.pallas_call(
        body, name=name,
        grid_spec=pltpu.PrefetchScalarGridSpec(
            num_scalar_prefetch=1, grid=(R // TR,),
            in_specs=[pl.BlockSpec((N_DEV, TR, C), lambda i, me_ref: (0, i, 0)),
                      pl.BlockSpec((1, TR, C), lambda i, me_ref: (me_ref[0], i, 0)), blk, blk, blk],
            out_specs=[blk, blk, blk, blk]),
        out_shape=[SDS((R, C), f32)] * 4,
        compiler_params=_cp(1, VMEM_LIMIT),
    )(me, recv, src, w, m, v)


MESH = pl.DeviceIdType.MESH
_ANY = pl.BlockSpec(memory_space=pl.ANY)


def _flip(v, d):
    return 1 - v if d else v


def _all_gather(shards):
    n = len(shards)

    def body(*refs):
        ins = refs[:n]
        outs = refs[n:2 * n]
        send_sems, recv_sems, local_sems = refs[2 * n:]
        x, y, c = lax.axis_index("x"), lax.axis_index("y"), lax.axis_index("c")
        me, sibling = (x, y, c), (x, y, 1 - c)
        chips = [(1 - x, y), (x, 1 - y), (1 - x, 1 - y)]

        def slot(px, py, pc):
            return 4 * px + 2 * py + pc

        def copy(a, k, block, to, src=None):
            dst = outs[a].at[slot(*block)]
            return pltpu.make_async_remote_copy(src_ref=dst if src is None else src, dst_ref=dst,
                                                send_sem=send_sems.at[a, k], recv_sem=recv_sems.at[a, k],
                                                device_id=to, device_id_type=MESH)

        mine, first, passed = [], [], []
        for a in range(n):
            cp = pltpu.make_async_copy(ins[a], outs[a].at[slot(*me)], local_sems.at[a])
            cp.start()
            mine.append(cp)
            fs = [copy(a, 0, me, sibling, src=ins[a])]
            fs += [copy(a, 1 + j, me, (*chip, c), src=ins[a]) for j, chip in enumerate(chips)]
            for cp in fs:
                cp.start()
            first += fs
        for j, chip in enumerate(chips):
            for a in range(n):
                copy(a, 1 + j, (*chip, c), me).wait_recv()
                cp = copy(a, 4 + j, (*chip, c), sibling)
                cp.start()
                passed.append(cp)
        for a in range(n):
            copy(a, 0, sibling, me).wait_recv()
            for j, chip in enumerate(chips):
                copy(a, 4 + j, (*chip, 1 - c), me).wait_recv()
        for cp in first + passed:
            cp.wait_send()
        for cp in mine:
            cp.wait()

    return pl.pallas_call(
        body, name="weight_all_gather",
        in_specs=[_ANY] * n, out_specs=[_ANY] * n,
        out_shape=[SDS((N_DEV,) + s.shape, s.dtype) for s in shards],
        scratch_shapes=[pltpu.SemaphoreType.DMA((n, 7)), pltpu.SemaphoreType.DMA((n, 7)), pltpu.SemaphoreType.DMA((n,))],
        compiler_params=pltpu.CompilerParams(has_side_effects=True),
    )(*shards)


def _grad_exchange(blocked, whole):
    arrs = list(blocked) + list(whole)
    n, nb = len(arrs), len(blocked)
    rel = [(dx, dy, dc) for dx in (0, 1) for dy in (0, 1) for dc in (0, 1) if dx or dy or dc]

    def body(*refs):
        ins = refs[:n]
        outs = refs[n:2 * n]
        send_sems, recv_sems, local_sems = refs[2 * n:]
        x, y, c = lax.axis_index("x"), lax.axis_index("y"), lax.axis_index("c")
        me = 4 * x + 2 * y + c
        sends, locs = [], []
        for a in range(n):
            cp = pltpu.make_async_copy(ins[a].at[me] if a < nb else ins[a], outs[a].at[me], local_sems.at[a])
            cp.start()
            locs.append(cp)
            for k, (dx, dy, dc) in enumerate(rel):
                peer = (_flip(x, dx), _flip(y, dy), _flip(c, dc))
                pidx = 4 * peer[0] + 2 * peer[1] + peer[2]
                cp = pltpu.make_async_remote_copy(src_ref=ins[a].at[pidx] if a < nb else ins[a], dst_ref=outs[a].at[me],
                                                  send_sem=send_sems.at[a, k], recv_sem=recv_sems.at[a, k],
                                                  device_id=peer, device_id_type=MESH)
                cp.start()
                sends.append(cp)
        for a in range(n):
            for k, (dx, dy, dc) in enumerate(rel):
                peer = (_flip(x, dx), _flip(y, dy), _flip(c, dc))
                pidx = 4 * peer[0] + 2 * peer[1] + peer[2]
                pltpu.make_async_remote_copy(src_ref=outs[a].at[pidx], dst_ref=outs[a].at[pidx],
                                             send_sem=send_sems.at[a, k], recv_sem=recv_sems.at[a, k],
                                             device_id=peer, device_id_type=MESH).wait_recv()
        for cp in sends:
            cp.wait_send()
        for cp in locs:
            cp.wait()

    shapes = [SDS(a.shape, a.dtype) for a in blocked] + [SDS((N_DEV,) + a.shape, a.dtype) for a in whole]
    return pl.pallas_call(
        body, name="grad_exchange",
        in_specs=[_ANY] * n, out_specs=[_ANY] * n, out_shape=shapes,
        scratch_shapes=[pltpu.SemaphoreType.DMA((n, 7)), pltpu.SemaphoreType.DMA((n, 7)), pltpu.SemaphoreType.DMA((n,))],
        compiler_params=pltpu.CompilerParams(has_side_effects=True),
    )(*arrs)


_HBM = pl.BlockSpec(memory_space=pltpu.HBM)
_SEM = pl.BlockSpec(memory_space=pltpu.SEMAPHORE)
_REL = [(dx, dy, dc) for dx in (0, 1) for dy in (0, 1) for dc in (0, 1) if dx or dy or dc]


N_PEER = len(_REL)
_EFFECT = pltpu.SideEffectType.DATAFLOW_SIDE_EFFECTING


def _peer_copies(srcs, lands, send_sems, recv_sems, blocked, as_receiver):
    x, y, c = lax.axis_index("x"), lax.axis_index("y"), lax.axis_index("c")
    me = 4 * x + 2 * y + c
    cps = []
    for a in range(len(srcs)):
        for k, (dx, dy, dc) in enumerate(_REL):
            peer = (_flip(x, dx), _flip(y, dy), _flip(c, dc))
            pidx = 4 * peer[0] + 2 * peer[1] + peer[2]
            cps.append(pltpu.make_async_remote_copy(
                src_ref=srcs[a].at[pidx] if blocked else srcs[a], dst_ref=lands[a].at[pidx if as_receiver else me],
                send_sem=send_sems[a * N_PEER + k], recv_sem=recv_sems[a * N_PEER + k],
                device_id=peer, device_id_type=MESH))
    return cps


def _exchange_start(srcs, after, blocked, name):
    n = len(srcs)
    ns = n * N_PEER
    lands = [lax.empty(s.shape if blocked else (N_DEV,) + s.shape, s.dtype) for s in srcs]

    def body(*refs):
        ins, lnd = refs[:n], refs[n:2 * n]
        outs = refs[2 * n + 1:]
        for cp in _peer_copies(ins, lnd, outs[:ns], outs[ns:2 * ns], blocked, False):
            cp.start()
        outs[-1][...] = jnp.zeros_like(outs[-1])

    res = pl.pallas_call(
        body, name=name,
        in_specs=[_HBM] * (2 * n) + [_ANY],
        out_specs=[_SEM] * (2 * ns) + [_HBM] * (2 * n) + [pl.BlockSpec(memory_space=pltpu.VMEM)],
        out_shape=[pltpu.SemaphoreType.DMA(())] * (2 * ns) + [pltpu.HBM(s.shape, s.dtype) for s in srcs]
                  + [pltpu.HBM(l.shape, l.dtype) for l in lands] + [SDS((8, LANE), f32)],
        input_output_aliases={i: 2 * ns + i for i in range(2 * n)},
        compiler_params=pltpu.CompilerParams(has_side_effects=_EFFECT),
    )(*[pltpu.with_memory_space_constraint(s, pltpu.HBM) for s in srcs],
      *[pltpu.with_memory_space_constraint(l, pltpu.HBM) for l in lands], after)
    return list(res[:2 * ns]), list(res[2 * ns:2 * ns + n]), list(res[2 * ns + n:2 * ns + 2 * n]), res[-1]


def _exchange_wait(sems, srcs, lands, after, blocked, name):
    n = len(srcs)
    ns = n * N_PEER

    def body(*refs):
        ins, lnd = refs[:n], refs[n:2 * n]
        sem_refs = refs[2 * n:2 * n + 2 * ns]
        for cp in _peer_copies(ins, lnd, sem_refs[:ns], sem_refs[ns:], blocked, True):
            cp.wait_send()
            cp.wait_recv()

    res = pl.pallas_call(
        body, name=name,
        in_specs=[_HBM] * (2 * n) + [_SEM] * (2 * ns) + [_ANY],
        out_specs=[_HBM] * (2 * n),
        out_shape=[pltpu.HBM(s.shape, s.dtype) for s in srcs] + [pltpu.HBM(l.shape, l.dtype) for l in lands],
        input_output_aliases={i: i for i in range(2 * n)},
        compiler_params=pltpu.CompilerParams(has_side_effects=_EFFECT),
    )(*srcs, *lands, *sems, after)
    return list(res[:n]), list(res[n:])


def _local_step(x, tgt, wcat, convw, late_weights, early_grads, last_grads, token, a_log, dt_bias, onorm_g, rel_bias,
                g_mix_pre, g_mix_post, g_ffn_pre, g_ffn_post):
    S = x.shape[0]
    bk_np = _bucket_tables()
    bk = jnp.asarray(bk_np)
    bt = _bias_tables(rel_bias, bk)
    proj, h1 = _proj_fwd(x, g_mix_pre, wcat, token, S)
    nu = S // CHUNK * GDN_HEADS
    qkv_u = _gdn_prep(proj, convw, S).reshape(3, nu, CHUNK, GDN_HD)
    intra, t_inv = _gdn_intra_fwd(qkv_u, proj, a_log, dt_bias, S)
    oa, states = _gdn_scan_fwd(intra, proj, onorm_g, S)
    ob, lse0, lse1 = _swa_fwd(proj, bt, S)
    wout, ffn_weights = late_weights(ob)
    mix, x1 = _mix_fwd(oa, ob, wout, x, g_mix_post, S)
    wgate, wup, wdown = ffn_weights(x1)
    dx1, h2, act, dgate_f, dup_f, df, loss, d_gfpre, d_gfpost = _ffn(x1, tgt, g_ffn_pre, g_ffn_post, wgate, wup, wdown, S)
    rows8 = lambda g: g.reshape(N_DEV, D_FF // N_DEV, D_MODEL)
    g_gate = rows8(_wgrad(dgate_f, h2, S, "wgrad_gate"))
    g_up = rows8(_wgrad(dup_f, h2, S, "wgrad_up"))
    g_down = rows8(_wgrad(act, df, S, "wgrad_down"))
    dmix, d_oab, d_gmpost, nd = _mix_bwd(dx1, mix, g_mix_post, wout, ob, S)
    g_out = _wgrad_out(oa, ob, dmix, S)
    token = early_grads(g_out.reshape(N_DEV, D_MODEL // N_DEV, D_MODEL), g_gate, g_up, g_down)
    dqb, dkb, dvb, dsb = _swa_bwd(proj, bt, nd, lse0, lse1, d_oab, token, S)
    *cots, dgate_a, d_og = _gdn_scan_bwd(intra, states, proj, d_oab, onorm_g, token, S)
    dqkv_u, dpg, d_alog, d_dtb = _gdn_intra_bwd(qkv_u, proj, a_log, dt_bias, t_inv, cots, S)
    dqkv_a, d_conv = _gdn_prep_bwd(proj, convw, dqkv_u.reshape(3, S // CHUNK, GDN_HEADS, CHUNK, GDN_HD), S)
    segs = [(dqkv_a, COL_A), (dgate_a, COL_A + 3 * GDN_W), (dqb, COL_B), (dkb, COL_B + SWA_W), (dvb, COL_B + 2 * SWA_W),
            (dpg, COL_G)]
    token = last_grads(_wgrad_in(h1, segs, S), d_conv)
    grad_x, d_gmpre = _proj_bwd(x, dx1, g_mix_pre, wcat, segs, token, S)
    d_rel = _rel_bias_grad(dsb, bk, bk_np)
    small = dict(a_log=d_alog[:, GDN_HEADS:2 * GDN_HEADS], dt_bias=d_dtb[:, GDN_HEADS:2 * GDN_HEADS], onorm_g=d_og, rel_bias=d_rel,
                 g_mix_pre=d_gmpre, g_mix_post=d_gmpost, g_ffn_pre=d_gfpre, g_ffn_post=d_gfpost)
    return loss, grad_x, small


SMALL = ("a_log", "dt_bias", "onorm_g", "rel_bias", "g_mix_pre", "g_mix_post", "g_ffn_pre", "g_ffn_post")
PACK_ROWS = 8


def _pack_small(d, loss=None):
    rest = jnp.concatenate([d["onorm_g"].reshape(-1), d["a_log"].reshape(-1), d["dt_bias"].reshape(-1),
                            d["rel_bias"].reshape(-1)])
    rest = jnp.concatenate([rest, jnp.zeros((D_MODEL - rest.shape[0],), f32)])
    extra = jnp.zeros((D_MODEL,), f32) if loss is None else jnp.concatenate([loss.reshape(1), jnp.zeros((D_MODEL - 1,), f32)])
    rows = [d["g_mix_pre"].reshape(-1), d["g_mix_post"].reshape(-1), d["g_ffn_pre"].reshape(-1),
            d["g_ffn_post"].reshape(-1), rest, extra]
    return jnp.concatenate([jnp.stack(rows), jnp.zeros((PACK_ROWS - len(rows), D_MODEL), f32)], axis=0)


def _unpack_small(p):
    o = GDN_HD
    return dict(g_mix_pre=p[0:1], g_mix_post=p[1:2], g_ffn_pre=p[2:3], g_ffn_post=p[3:4],
                onorm_g=p[4:5, :o], a_log=p[4:5, o:o + 4], dt_bias=p[4:5, o + 4:o + 8],
                rel_bias=p[4, o + 8:o + 8 + NUM_BUCKETS * SWA_HEADS].reshape(NUM_BUCKETS, SWA_HEADS))


def kernel(x, w_in, conv_w, a_log, dt_bias, onorm_g, rel_bias, w_out, g_mix_pre, g_mix_post, w_gate, w_up, w_down, g_ffn_pre, g_ffn_post, loss_target, m_w_in, m_conv_w, m_a_log, m_dt_bias, m_onorm_g, m_rel_bias, m_w_out, m_g_mix_pre, m_g_mix_post, m_w_gate, m_w_up, m_w_down, m_g_ffn_pre, m_g_ffn_post, v_w_in, v_conv_w, v_a_log, v_dt_bias, v_onorm_g, v_rel_bias, v_w_out, v_g_mix_pre, v_g_mix_post, v_w_gate, v_w_up, v_w_down, v_g_ffn_pre, v_g_ffn_post):
    big = ("w_in", "conv_w", "w_out", "w_gate", "w_up", "w_down")
    transposed = ("w_gate", "w_up")
    tr = lambda k, a: a.T if k in transposed else a
    w_sh = {k: tr(k, a[0]) for k, a in dict(w_in=w_in, conv_w=conv_w, w_out=w_out, w_gate=w_gate, w_up=w_up, w_down=w_down).items()}
    m_sh = {k: tr(k, a[0]) for k, a in dict(w_in=m_w_in, conv_w=m_conv_w, w_out=m_w_out, w_gate=m_w_gate, w_up=m_w_up,
                                             w_down=m_w_down).items()}
    v_sh = {k: tr(k, a[0]) for k, a in dict(w_in=v_w_in, conv_w=v_conv_w, w_out=v_w_out, w_gate=v_w_gate, w_up=v_w_up,
                                             w_down=v_w_down).items()}
    w_small = dict(a_log=a_log, dt_bias=dt_bias, onorm_g=onorm_g, rel_bias=rel_bias, g_mix_pre=g_mix_pre,
                   g_mix_post=g_mix_post, g_ffn_pre=g_ffn_pre, g_ffn_post=g_ffn_post)
    m_small = dict(a_log=m_a_log, dt_bias=m_dt_bias, onorm_g=m_onorm_g, rel_bias=m_rel_bias, g_mix_pre=m_g_mix_pre,
                   g_mix_post=m_g_mix_post, g_ffn_pre=m_g_ffn_pre, g_ffn_post=m_g_ffn_post)
    v_small = dict(a_log=v_a_log, dt_bias=v_dt_bias, onorm_g=v_onorm_g, rel_bias=v_rel_bias, g_mix_pre=v_g_mix_pre,
                   g_mix_post=v_g_mix_post, g_ffn_pre=v_g_ffn_pre, g_ffn_post=v_g_ffn_post)

    me = 4 * lax.axis_index("x") + 2 * lax.axis_index("y") + lax.axis_index("c")
    me1 = me.reshape(1).astype(jnp.int32)
    own = lambda full, part: lax.dynamic_update_index_in_dim(full, part, me, 0)
    cols = lambda g: g.reshape(g.shape[0], N_DEV, g.shape[1] // N_DEV).transpose(1, 0, 2)
    late = ("w_out", "w_gate", "w_up", "w_down")

    late_src = [w_sh[k].astype(bf16) for k in late]
    g_in, g_conv = _all_gather([w_sh["w_in"].astype(bf16), w_sh["conv_w"]])
    g_sems, g_src, g_land, g_token = _exchange_start(late_src, g_conv, False, "late_weights_start")
    wcat = _wcat_from_blocks(g_in)
    convw = g_conv.transpose(1, 0, 2).reshape(4, 3 * GDN_W)

    def late_weights(after):
        pick = lambda idx: [g_sems[half * len(late) * N_PEER + a * N_PEER + k] for half in (0, 1) for a in idx for k in range(N_PEER)]
        (s_out,), (l_out,) = _exchange_wait(pick([0]), g_src[:1], g_land[:1], after, False, "w_out_wait")

        def ffn_weights(after2):
            srcs, lands = _exchange_wait(pick([1, 2, 3]), g_src[1:], g_land[1:], after2, False, "ffn_weights_wait")
            return [own(l, s).reshape(D_FF, D_MODEL) for l, s in zip(lands, srcs)]

        return own(l_out, s_out).reshape(D_MODEL, D_MODEL), ffn_weights

    early, last = {}, {}

    def early_grads(*blocks):
        early["sems"], early["src"], early["land"], token = _exchange_start(list(blocks), me1, True, "late_grads_start")
        return token

    def last_grads(gw_in, gw_conv):
        src = [gw_in, cols(gw_conv)]
        last["sems"], last["src"], last["land"], token = _exchange_start(src, me1, True, "last_grads_start")
        return token

    loss_p, grad_x, gsmall = _local_step(
        x[0], loss_target[0], wcat, convw, late_weights, early_grads, last_grads, g_token,
        a_log, dt_bias, onorm_g, rel_bias, g_mix_pre, g_mix_post, g_ffn_pre, g_ffn_post)

    (r_small,) = _grad_exchange([], [_pack_small(gsmall, loss_p[0, 0])])
    outs = {}
    for names, ex, after, name in ((late, early, grad_x, "late_grads_wait"), (("w_in", "conv_w"), last, r_small, "last_grads_wait")):
        srcs, lands = _exchange_wait(ex["sems"], ex["src"], ex["land"], after, True, name)
        for k, l, s in zip(names, lands, srcs):
            outs[k] = _adamw(l, s, me1, w_sh[k], m_sh[k], v_sh[k], "adamw_" + k)
    sm = _adamw(r_small, r_small, me1, _pack_small(w_small), _pack_small(m_small), _pack_small(v_small), "adamw_small")
    loss = sm[0][5, 0]
    sm = [_unpack_small(t) for t in sm]
    for k in SMALL:
        outs[k] = tuple(t[k].reshape(w_small[k].shape) for t in sm)

    order = ("w_in", "conv_w", "a_log", "dt_bias", "onorm_g", "rel_bias", "w_out", "g_mix_pre", "g_mix_post", "w_gate",
             "w_up", "w_down", "g_ffn_pre", "g_ffn_post")
    lead = lambda k, t: tr(k, t)[None] if k in big else t
    res = [loss, grad_x[None]]
    for i in range(4):
        res += [lead(k, outs[k][i]) for k in order]
    return tuple(res)
```

```python
import functools
import math

import numpy as np
import jax
import jax.numpy as jnp
from jax import lax
from jax.experimental import pallas as pl
from jax.experimental.pallas import tpu as pltpu

f32 = jnp.float32
bf16 = jnp.bfloat16
SDS = jax.ShapeDtypeStruct

D_MODEL = 1024
GDN_HEADS = 4
GDN_HD = 128
GDN_W = 512
CHUNK = 64
SWA_HEADS = 8
SWA_HD = 64
SWA_W = 512
D_FF = 2816
IN_COLS = 3592
PATTERNS = ((128, 1), (512, 4), (2048, 16))
SWA_BLK = 128
NUM_BUCKETS = 32
MAX_DISTANCE = 2048
RMS_EPS = 1e-6
NEG = -1e30
N_DEV = 8

COL_A = 0
COL_B = 2048
COL_G = 3584
NCOL = 3712
LANE = 128

ADAM_LR, ADAM_B1, ADAM_B2, ADAM_EPS, ADAM_WD, ADAM_STEP = 0.001, 0.9, 0.999, 1e-08, 0.01, 10

VMEM_LIMIT = 56 * 1024 * 1024

HIGH = lax.Precision.HIGH


def _cp(n_grid=0, vmem=None):
    kw = {}
    if n_grid:
        kw["dimension_semantics"] = ("arbitrary",) * n_grid
    if vmem:
        kw["vmem_limit_bytes"] = vmem
    return pltpu.CompilerParams(**kw)


def _dot(a, b):
    return jnp.dot(a, b, preferred_element_type=f32)


def _dot_nt(a, b):
    return lax.dot_general(a, b, (((1,), (1,)), ((), ())), preferred_element_type=f32)


def _dot_tn(a, b):
    return lax.dot_general(a, b, (((0,), (0,)), ((), ())), preferred_element_type=f32)


def _sigmoid(x):
    return 0.5 * jnp.tanh(0.5 * x) + 0.5


def _softplus(x):
    return jnp.maximum(x, 0.0) + jnp.log(1.0 + jnp.exp(-jnp.abs(x)))


def _const_spec(shape):
    nd = len(shape)
    return pl.BlockSpec(shape, lambda *_: (0,) * nd)


def _resident_spec(shape):
    nd = len(shape)
    return pl.BlockSpec(shape, lambda *_: (0,) * nd, pipeline_mode=pl.Buffered(1))


def _t5_bucket_np(dist):
    max_exact = NUM_BUCKETS // 2
    d = np.maximum(dist, 1).astype(np.float32)
    log_b = max_exact + (np.log(d / np.float32(max_exact)) / np.float32(math.log(MAX_DISTANCE / max_exact))
                         * np.float32(NUM_BUCKETS - max_exact)).astype(np.int32)
    return np.where(dist < max_exact, dist, np.minimum(log_b, NUM_BUCKETS - 1)).astype(np.int32)


def _bucket_tables():
    w = SWA_BLK
    qi = np.arange(w)[:, None]
    kj = np.arange(w)[None, :]
    rel = np.where(kj <= qi, qi - kj, qi + w - kj)
    out = np.zeros((len(PATTERNS), w, w), np.int32)
    for p, (_, dil) in enumerate(PATTERNS):
        steps = _t5_bucket_np(np.arange(w + 1) * dil)
        assert steps[w] == steps[w - 1]
        out[p] = steps[rel]
    return out


def _bias_tables(rel_bias, bk):
    def body(rb_ref, bk_ref, o_ref):
        b_idx = bk_ref[0]
        for h in range(SWA_HEADS):
            def lp(b, acc):
                return jnp.where(b_idx == b, rb_ref[b, h], acc)
            o_ref[0, h] = lax.fori_loop(0, NUM_BUCKETS, lp, jnp.zeros((SWA_BLK, SWA_BLK), f32))

    return pl.pallas_call(
        body, name="bias_tables", grid=(3,),
        in_specs=[pl.BlockSpec(memory_space=pltpu.SMEM), pl.BlockSpec((1, SWA_BLK, SWA_BLK), lambda p: (p, 0, 0))],
        out_specs=pl.BlockSpec((1, SWA_HEADS, SWA_BLK, SWA_BLK), lambda p: (p, 0, 0, 0)),
        out_shape=SDS((3, SWA_HEADS, SWA_BLK, SWA_BLK), f32),
        compiler_params=_cp(1),
    )(rel_bias, bk)


def _rel_bias_grad(dsb, bk, bk_np):
    present = [sorted(set(int(v) for v in np.unique(bk_np[p]))) for p in range(3)]

    def body(ds_ref, bk_ref, o_ref):
        row = lax.broadcasted_iota(jnp.int32, (NUM_BUCKETS, LANE), 0)
        col = lax.broadcasted_iota(jnp.int32, (NUM_BUCKETS, SWA_HEADS), 1)
        out = jnp.zeros((NUM_BUCKETS, SWA_HEADS), f32)
        for hp in range(4):
            for hh in range(2):
                acc = jnp.zeros((NUM_BUCKETS, LANE), f32)
                for p in range(3):
                    tile = ds_ref[hp, p, hh]
                    b_idx = bk_ref[p]
                    for b in present[p]:
                        part = jnp.sum(jnp.where(b_idx == b, tile, 0.0), axis=0, keepdims=True)
                        acc = acc + jnp.where(row == b, part, 0.0)
                tot = jnp.sum(acc, axis=1, keepdims=True)
                out = out + jnp.where(col == 2 * hp + hh, tot, 0.0)
        o_ref[...] = out

    return pl.pallas_call(body, name="rel_bias_grad", out_shape=SDS((NUM_BUCKETS, SWA_HEADS), f32),
                          compiler_params=_cp(0, 32 * 1024 * 1024))(dsb, bk)


def _proj_fwd(x, g_pre, wcat, after, S):
    TS = 512

    def body(x_ref, g_ref, w_ref, after_ref, o_ref, h_ref):
        xv = x_ref[...]
        r = lax.rsqrt(jnp.mean(xv * xv, axis=-1, keepdims=True) + RMS_EPS)
        h = (xv * r * g_ref[...]).astype(bf16)
        h_ref[...] = h
        o_ref[...] = _dot(h, w_ref[...])

    return pl.pallas_call(
        body, name="proj_fwd", grid=(S // TS,),
        in_specs=[pl.BlockSpec((TS, D_MODEL), lambda i: (i, 0)), _const_spec((1, D_MODEL)),
                  _resident_spec((D_MODEL, NCOL)), _ANY],
        out_specs=[pl.BlockSpec((TS, NCOL), lambda i: (i, 0)), pl.BlockSpec((TS, D_MODEL), lambda i: (i, 0))],
        out_shape=[SDS((S, NCOL), f32), SDS((S, D_MODEL), bf16)],
        compiler_params=_cp(1, VMEM_LIMIT),
    )(x, g_pre, wcat, after)


CONV_RT = 256
HALO = 8


CONV_NC = CONV_RT // CHUNK


def _gdn_prep(proj, conv_w, S):
    def body(p_ref, cw_ref, o_ref, xs_ref):
        t = pl.program_id(0)
        xs_ref[pl.ds(0, HALO), :] = jnp.zeros((HALO, LANE), f32)
        xs_ref[pl.ds(HALO, S), :] = p_ref[...]
        w = cw_ref[...]
        is_qk = t < 2
        scale = jnp.where(t == 0, GDN_HD ** -0.5, 1.0).astype(f32)

        def lp(c, carry):
            st = pl.multiple_of(c * CONV_RT, CONV_RT)
            pre = xs_ref[pl.ds(st + HALO - 3, CONV_RT), :] * w[0:1, :]
            for i in range(1, 4):
                pre = pre + xs_ref[pl.ds(st + HALO - 3 + i, CONV_RT), :] * w[i:i + 1, :]
            s = pre * _sigmoid(pre)
            nrm = s * lax.rsqrt(jnp.sum(s * s, axis=-1, keepdims=True) + 1e-6) * scale
            out = jnp.where(is_qk, nrm, s)
            for i in range(CONV_NC):
                o_ref[0, c * CONV_NC + i, 0] = out[i * CHUNK:(i + 1) * CHUNK]
            return carry

        lax.fori_loop(0, S // CONV_RT, lp, 0)

    return pl.pallas_call(
        body, name="gdn_prep", grid=(3, GDN_HEADS),
        in_specs=[pl.BlockSpec((S, LANE), lambda t, h: (0, t * GDN_HEADS + h)),
                  pl.BlockSpec((4, LANE), lambda t, h: (0, t * GDN_HEADS + h))],
        out_specs=pl.BlockSpec((1, S // CHUNK, 1, CHUNK, GDN_HD), lambda t, h: (t, 0, h, 0, 0)),
        out_shape=SDS((3, S // CHUNK, GDN_HEADS, CHUNK, GDN_HD), f32),
        scratch_shapes=[pltpu.VMEM((S + HALO, LANE), f32)],
        compiler_params=_cp(2, VMEM_LIMIT),
    )(proj, conv_w)


def _gdn_prep_bwd(proj, conv_w, dqkv, S):
    def body(p_ref, cw_ref, d_ref, dx_ref, dw_ref, xs_ref, dp_ref):
        t = pl.program_id(0)
        xs_ref[pl.ds(0, HALO), :] = jnp.zeros((HALO, LANE), f32)
        xs_ref[pl.ds(HALO, S), :] = p_ref[...]
        dp_ref[pl.ds(S, HALO), :] = jnp.zeros((HALO, LANE), f32)
        w = cw_ref[...]
        is_qk = t < 2
        scale = jnp.where(t == 0, GDN_HD ** -0.5, 1.0).astype(f32)

        def lp1(c, dw):
            st = pl.multiple_of(c * CONV_RT, CONV_RT)
            taps = [xs_ref[pl.ds(st + HALO - 3 + i, CONV_RT), :] for i in range(4)]
            pre = taps[0] * w[0:1, :]
            for i in range(1, 4):
                pre = pre + taps[i] * w[i:i + 1, :]
            sg = _sigmoid(pre)
            s = pre * sg
            d_out = jnp.concatenate([d_ref[0, c * CONV_NC + i, 0] for i in range(CONV_NC)], axis=0)
            rn = lax.rsqrt(jnp.sum(s * s, axis=-1, keepdims=True) + 1e-6)
            n = s * rn
            dn = d_out * scale
            ds_qk = rn * (dn - n * jnp.sum(dn * n, axis=-1, keepdims=True))
            ds = jnp.where(is_qk, ds_qk, d_out)
            dpre = ds * (sg * (1.0 + pre * (1.0 - sg)))
            dp_ref[pl.ds(st, CONV_RT), :] = dpre
            return tuple(dw[i] + jnp.sum(dpre * taps[i], axis=0, keepdims=True) for i in range(4))

        z = jnp.zeros((1, LANE), f32)
        dw = lax.fori_loop(0, S // CONV_RT, lp1, (z, z, z, z))
        for i in range(4):
            dw_ref[pl.ds(i, 1), :] = dw[i]

        def lp2(c, carry):
            st = pl.multiple_of(c * CONV_RT, CONV_RT)
            dx = dp_ref[pl.ds(st, CONV_RT), :] * w[3:4, :]
            for i in range(3):
                dx = dx + dp_ref[pl.ds(st + 3 - i, CONV_RT), :] * w[i:i + 1, :]
            dx_ref[pl.ds(st, CONV_RT), :] = dx.astype(bf16)
            return carry

        lax.fori_loop(0, S // CONV_RT, lp2, 0)

    col = lambda rows: pl.BlockSpec((rows, LANE), lambda t, h: (0, t * GDN_HEADS + h))
    return pl.pallas_call(
        body, name="gdn_prep_bwd", grid=(3, GDN_HEADS),
        in_specs=[col(S), col(4), pl.BlockSpec((1, S // CHUNK, 1, CHUNK, GDN_HD), lambda t, h: (t, 0, h, 0, 0))],
        out_specs=[col(S), col(4)],
        out_shape=[SDS((S, 3 * GDN_W), bf16), SDS((4, 3 * GDN_W), f32)],
        scratch_shapes=[pltpu.VMEM((S + HALO, LANE), f32), pltpu.VMEM((S + HALO, LANE), f32)],
        compiler_params=_cp(2, VMEM_LIMIT),
    )(proj, conv_w, dqkv)


def _bdot(a, b, prec=None):
    return lax.dot_general(a, b, (((2,), (1,)), ((0,), (0,))), precision=prec, preferred_element_type=f32)


def _bdot_nt(a, b, prec=None):
    return lax.dot_general(a, b, (((2,), (2,)), ((0,), (0,))), precision=prec, preferred_element_type=f32)


def _bdot_tn(a, b, prec=None):
    return lax.dot_general(a, b, (((1,), (1,)), ((0,), (0,))), precision=prec, preferred_element_type=f32)


@jax.custom_vjp
def _tri_inv_saved(a, t):
    return t


def _tri_inv_saved_fwd(a, t):
    return t, t


def _tri_inv_saved_bwd(t, dt):
    return -_bdot_tn(t, _bdot_nt(dt, t, HIGH), HIGH), jnp.zeros_like(t)


_tri_inv_saved.defvjp(_tri_inv_saved_fwd, _tri_inv_saved_bwd)


def _gdn_intra(q, k, v, beta, g, t_saved=None):
    nb = q.shape[0]
    c = CHUNK
    ii = lax.broadcasted_iota(jnp.int32, (c, c), 0)
    jj = lax.broadcasted_iota(jnp.int32, (c, c), 1)
    eye = ii == jj
    tril = ii >= jj
    strict = ii > jj
    ones = jnp.ones((nb, c, c), f32)
    eye_f = eye.astype(f32)

    g_row = _bdot(ones, jnp.where(eye, g, 0.0), HIGH)
    gc = jnp.sum(jnp.where(tril, g_row, 0.0), axis=2, keepdims=True)
    gc_row = _bdot(ones, jnp.where(eye, gc, 0.0), HIGH)
    decay = jnp.where(tril, jnp.exp(jnp.where(tril, gc - gc_row, 0.0)), 0.0)
    last = lax.broadcasted_iota(jnp.int32, (c, 1), 0) == c - 1
    gc_last = jnp.sum(jnp.where(last, gc, 0.0), axis=1, keepdims=True)
    e_gc = jnp.exp(gc)

    kb = k * beta
    k16 = k.astype(bf16)
    a = jnp.where(strict, _bdot_nt(kb.astype(bf16), k16) * decay, 0.0)
    if t_saved is None:
        xp = -a
        t_inv = eye_f + xp
        for level in range(5):
            if level < 2:
                xp = _bdot(xp, xp, HIGH)
                t_inv = t_inv + _bdot(t_inv, xp, HIGH)
            else:
                x16 = xp.astype(bf16)
                xp = _bdot(x16, x16)
                t_inv = t_inv + _bdot(t_inv.astype(bf16), xp.astype(bf16))
    else:
        t_inv = _tri_inv_saved(a, t_saved)
    t16 = t_inv.astype(bf16)
    u = _bdot(t16, (v * beta).astype(bf16))
    w = _bdot(t16, (kb * e_gc).astype(bf16))
    attn = jnp.where(tril, _bdot_nt(q.astype(bf16), k16) * decay, 0.0)
    gam = jnp.broadcast_to(jnp.exp(gc_last), (nb, 1, GDN_HD))
    return u, w, attn, q * e_gc, k * jnp.exp(gc_last - gc), gam, t_inv


GDN_TB = 512
GDN_NC = GDN_TB // CHUNK
GDN_NU = GDN_NC * GDN_HEADS


def _gdn_gates(pg_ref, al_ref, db_ref):
    lane1 = lax.broadcasted_iota(jnp.int32, (1, LANE), 1)
    a_lane = jnp.zeros((1, LANE), f32)
    b_lane = jnp.zeros((1, LANE), f32)
    for h in range(GDN_HEADS):
        a_lane = jnp.where(lane1 == GDN_HEADS + h, al_ref[0, h], a_lane)
        b_lane = jnp.where(lane1 == GDN_HEADS + h, db_ref[0, h], b_lane)
    pg = pg_ref[...]
    z = pg + b_lane
    return _sigmoid(pg), -jnp.exp(a_lane) * _softplus(z), z, a_lane


def _gdn_unit_inputs(qkv_ref, beta_all, g_all):
    units = [(cl, h) for cl in range(GDN_NC) for h in range(GDN_HEADS)]
    beta = jnp.stack([beta_all[cl * CHUNK:(cl + 1) * CHUNK, h:h + 1] for cl, h in units])
    g = jnp.stack([g_all[cl * CHUNK:(cl + 1) * CHUNK, GDN_HEADS + h:GDN_HEADS + h + 1] for cl, h in units])
    return qkv_ref[0], qkv_ref[1], qkv_ref[2], beta, g


def _unit_spec(*tail):
    nd = len(tail)
    return pl.BlockSpec((GDN_NU,) + tail, lambda i: (i,) + (0,) * nd)


def _gdn_intra_shapes(S):
    nu = S // CHUNK * GDN_HEADS
    row = SDS((nu, CHUNK, GDN_HD), f32)
    return [row, row, SDS((nu, CHUNK, CHUNK), f32), row, row, SDS((nu, 1, GDN_HD), f32)]


_GDN_INTRA_SPECS = lambda: [_unit_spec(CHUNK, GDN_HD), _unit_spec(CHUNK, GDN_HD), _unit_spec(CHUNK, CHUNK),
                            _unit_spec(CHUNK, GDN_HD), _unit_spec(CHUNK, GDN_HD), _unit_spec(1, GDN_HD)]


def _gdn_intra_fwd(qkv_u, proj, a_log, dt_bias, S):
    def body(qkv_ref, pg_ref, al_ref, db_ref, *outs):
        beta_all, g_all, _, _ = _gdn_gates(pg_ref, al_ref, db_ref)
        res = _gdn_intra(*_gdn_unit_inputs(qkv_ref, beta_all, g_all))
        for o_ref, r in zip(outs, res):
            o_ref[...] = r

    nu = S // CHUNK * GDN_HEADS
    *intra, t_inv = pl.pallas_call(
        body, name="gdn_intra_fwd", grid=(S // GDN_TB,),
        in_specs=[pl.BlockSpec((3, GDN_NU, CHUNK, GDN_HD), lambda i: (0, i, 0, 0)),
                  pl.BlockSpec((GDN_TB, LANE), lambda i: (i, COL_G // LANE)),
                  pl.BlockSpec(memory_space=pltpu.SMEM), pl.BlockSpec(memory_space=pltpu.SMEM)],
        out_specs=_GDN_INTRA_SPECS() + [_unit_spec(CHUNK, CHUNK)],
        out_shape=_gdn_intra_shapes(S) + [SDS((nu, CHUNK, CHUNK), f32)],
        compiler_params=_cp(1, VMEM_LIMIT),
    )(qkv_u, proj, a_log, dt_bias)
    return intra, t_inv


def _gdn_intra_bwd(qkv_u, proj, a_log, dt_bias, t_inv, cots, S):
    def body(qkv_ref, pg_ref, al_ref, db_ref, t_ref, du_ref, dw_ref, da_ref, dqd_ref, dkd_ref, dgm_ref,
             dqkv_ref, dpg_ref, dal_ref, ddb_ref):
        @pl.when(pl.program_id(0) == 0)
        def _():
            dal_ref[...] = jnp.zeros_like(dal_ref)
            ddb_ref[...] = jnp.zeros_like(ddb_ref)

        t_saved = t_ref[...]
        beta_all, g_all, z, a_lane = _gdn_gates(pg_ref, al_ref, db_ref)
        _, vjp = jax.vjp(lambda *a: _gdn_intra(*a, t_saved=t_saved)[:6], *_gdn_unit_inputs(qkv_ref, beta_all, g_all))
        dq, dk, dv, dbeta, dg = vjp((du_ref[...], dw_ref[...], da_ref[...], dqd_ref[...], dkd_ref[...], dgm_ref[...]))
        dqkv_ref[0] = dq
        dqkv_ref[1] = dk
        dqkv_ref[2] = dv
        lane = lax.broadcasted_iota(jnp.int32, (CHUNK, LANE), 1)
        rows = []
        for cl in range(GDN_NC):
            t = jnp.zeros((CHUNK, LANE), f32)
            for h in range(GDN_HEADS):
                b = cl * GDN_HEADS + h
                t = t + jnp.where(lane == h, dbeta[b], 0.0) + jnp.where(lane == GDN_HEADS + h, dg[b], 0.0)
            rows.append(t)
        d_all = jnp.concatenate(rows, axis=0)
        is_beta = lax.broadcasted_iota(jnp.int32, (GDN_TB, LANE), 1) < GDN_HEADS
        dz = d_all * (-jnp.exp(a_lane)) * _sigmoid(z)
        dpg_ref[...] = jnp.where(is_beta, d_all * beta_all * (1.0 - beta_all), dz).astype(bf16)
        dal_ref[...] += jnp.sum(jnp.where(is_beta, 0.0, d_all * g_all), axis=0, keepdims=True)
        ddb_ref[...] += jnp.sum(jnp.where(is_beta, 0.0, dz), axis=0, keepdims=True)

    acc = _const_spec((1, LANE))
    nu = S // CHUNK * GDN_HEADS
    return pl.pallas_call(
        body, name="gdn_intra_bwd", grid=(S // GDN_TB,),
        in_specs=[pl.BlockSpec((3, GDN_NU, CHUNK, GDN_HD), lambda i: (0, i, 0, 0)),
                  pl.BlockSpec((GDN_TB, LANE), lambda i: (i, COL_G // LANE)),
                  pl.BlockSpec(memory_space=pltpu.SMEM), pl.BlockSpec(memory_space=pltpu.SMEM),
                  _unit_spec(CHUNK, CHUNK)] + _GDN_INTRA_SPECS(),
        out_specs=[pl.BlockSpec((3, GDN_NU, CHUNK, GDN_HD), lambda i: (0, i, 0, 0)),
                   pl.BlockSpec((GDN_TB, LANE), lambda i: (i, 0)), acc, acc],
        out_shape=[SDS((3, nu, CHUNK, GDN_HD), f32), SDS((S, LANE), bf16), SDS((1, LANE), f32), SDS((1, LANE), f32)],
        compiler_params=_cp(1, VMEM_LIMIT),
    )(qkv_u, proj, a_log, dt_bias, t_inv, *cots)


def _gdn_scan_fwd(intra, proj, onorm_g, S):
    def body(u_ref, w_ref, at_ref, qd_ref, kd_ref, gm_ref, gate_ref, og_ref, out_ref, st_ref, s_scr):
        @pl.when(pl.program_id(0) == 0)
        def _():
            s_scr[...] = jnp.zeros_like(s_scr)

        og = og_ref[...]
        s = s_scr[...]
        def out_mm(us, s16, vn16):
            return _bdot(qd_ref[us].astype(bf16), s16) + _bdot(at_ref[us].astype(bf16), vn16)

        outs, prev = [], None
        for cl in range(GDN_NC):
            us = slice(cl * GDN_HEADS, (cl + 1) * GDN_HEADS)
            st_ref[us] = s
            s16 = s.astype(bf16)
            ws = _bdot(w_ref[us].astype(bf16), s16)
            if prev is not None:
                outs.append(out_mm(*prev))
            vn16 = (u_ref[us] - ws).astype(bf16)
            prev = (us, s16, vn16)
            s = s * gm_ref[us] + _bdot_tn(kd_ref[us].astype(bf16), vn16)
        outs.append(out_mm(*prev))
        s_scr[...] = s
        for cl, o in enumerate(outs):
            rows = slice(cl * CHUNK, (cl + 1) * CHUNK)
            for h in range(GDN_HEADS):
                oh = o[h]
                gt = gate_ref[rows, h * GDN_HD:(h + 1) * GDN_HD]
                on = oh * lax.rsqrt(jnp.mean(oh * oh, axis=-1, keepdims=True) + RMS_EPS) * og
                out_ref[rows, h * GDN_HD:(h + 1) * GDN_HD] = on * (gt * _sigmoid(gt))

    nu = S // CHUNK * GDN_HEADS
    return pl.pallas_call(
        body, name="gdn_scan_fwd", grid=(S // GDN_TB,),
        in_specs=_GDN_INTRA_SPECS() + [pl.BlockSpec((GDN_TB, GDN_W), lambda i: (i, 3)), _const_spec((1, GDN_HD))],
        out_specs=[pl.BlockSpec((GDN_TB, GDN_W), lambda i: (i, 0)), _unit_spec(GDN_HD, GDN_HD)],
        out_shape=[SDS((S, GDN_W), f32), SDS((nu, GDN_HD, GDN_HD), f32)],
        scratch_shapes=[pltpu.VMEM((GDN_HEADS, GDN_HD, GDN_HD), f32)],
        compiler_params=_cp(1, VMEM_LIMIT),
    )(*intra, proj, onorm_g)


def _gdn_scan_bwd(intra, states, proj, d_oab, onorm_g, after, S):
    n_steps = S // GDN_TB

    def body(u_ref, w_ref, at_ref, qd_ref, kd_ref, gm_ref, st_ref, gate_ref, do_ref, og_ref, after_ref,
             du_ref, dw_ref, dat_ref, dqd_ref, dkd_ref, dgm_ref, dgate_ref, dog_ref, ds_scr):
        @pl.when(pl.program_id(0) == 0)
        def _():
            ds_scr[...] = jnp.zeros_like(ds_scr)
            dog_ref[...] = jnp.zeros_like(dog_ref)

        og = og_ref[...]
        ii = lax.broadcasted_iota(jnp.int32, (CHUNK, CHUNK), 0)
        jj = lax.broadcasted_iota(jnp.int32, (CHUNK, CHUNK), 1)
        tril = ii >= jj
        dog = jnp.zeros((1, GDN_HD), f32)
        pre = []
        for cl in range(GDN_NC):
            us = slice(cl * GDN_HEADS, (cl + 1) * GDN_HEADS)
            rows = slice(cl * CHUNK, (cl + 1) * CHUNK)
            s016 = st_ref[us].astype(bf16)
            w16 = w_ref[us].astype(bf16)
            qd16 = qd_ref[us].astype(bf16)
            at16 = at_ref[us].astype(bf16)
            vn16 = (u_ref[us] - _bdot(w16, s016)).astype(bf16)
            o = _bdot(qd16, s016) + _bdot(at16, vn16)
            do_h = []
            for h in range(GDN_HEADS):
                oh = o[h]
                lanes = slice(h * GDN_HD, (h + 1) * GDN_HD)
                gt = gate_ref[rows, lanes]
                d_out = do_ref[rows, lanes]
                r = lax.rsqrt(jnp.mean(oh * oh, axis=-1, keepdims=True) + RMS_EPS)
                n = oh * r
                sg = _sigmoid(gt)
                silu = gt * sg
                dog = dog + jnp.sum(d_out * n * silu, axis=0, keepdims=True)
                dgate_ref[rows, lanes] = (d_out * n * og * (sg * (1.0 + gt * (1.0 - sg)))).astype(bf16)
                dn = d_out * og * silu
                do_h.append(r * (dn - n * jnp.mean(dn * n, axis=-1, keepdims=True)))
            do16 = jnp.stack(do_h).astype(bf16)
            pre.append((us, s016, w16, vn16, do16, _bdot_tn(at16, do16), _bdot_tn(qd16, do16)))
        ds = ds_scr[...]
        chain = [None] * GDN_NC
        for cl in reversed(range(GDN_NC)):
            us, s016, w16, vn16, do16, at_do, qd_do = pre[cl]
            ds16 = ds.astype(bf16)
            dvn = at_do + _bdot(kd_ref[us].astype(bf16), ds16)
            dvn16 = dvn.astype(bf16)
            chain[cl] = (ds, ds16, dvn, dvn16)
            ds = qd_do + ds * gm_ref[us] - _bdot_tn(w16, dvn16)
        ds_scr[...] = ds
        for cl in range(GDN_NC):
            us, s016, w16, vn16, do16, _, _ = pre[cl]
            ds_in, ds16, dvn, dvn16 = chain[cl]
            du_ref[us] = dvn
            dw_ref[us] = -_bdot_nt(dvn16, s016)
            dat_ref[us] = jnp.where(tril, _bdot_nt(do16, vn16), 0.0)
            dqd_ref[us] = _bdot_nt(do16, s016)
            dkd_ref[us] = _bdot_nt(vn16, ds16)
            dgm_ref[us] = jnp.sum(st_ref[us] * ds_in, axis=1, keepdims=True)
        dog_ref[...] += dog

    def unit(*tail):
        nd = len(tail)
        return pl.BlockSpec((GDN_NU,) + tail, lambda i: (n_steps - 1 - i,) + (0,) * nd)

    intra_specs = [unit(CHUNK, GDN_HD), unit(CHUNK, GDN_HD), unit(CHUNK, CHUNK), unit(CHUNK, GDN_HD),
                   unit(CHUNK, GDN_HD), unit(1, GDN_HD)]
    tok = lambda c: pl.BlockSpec((GDN_TB, GDN_W), lambda i: (n_steps - 1 - i, c))
    return pl.pallas_call(
        body, name="gdn_scan_bwd", grid=(n_steps,),
        in_specs=intra_specs + [unit(GDN_HD, GDN_HD), tok(3), tok(0), _const_spec((1, GDN_HD)), _ANY],
        out_specs=intra_specs + [tok(0), _const_spec((1, GDN_HD))],
        out_shape=_gdn_intra_shapes(S) + [SDS((S, GDN_W), bf16), SDS((1, GDN_HD), f32)],
        scratch_shapes=[pltpu.VMEM((GDN_HEADS, GDN_HD, GDN_HD), f32)],
        compiler_params=_cp(1, VMEM_LIMIT),
    )(*intra, states, proj, d_oab, onorm_g, after)


SWA_UNROLL = 8


def _swa_tiles(q_ref, k_ref, v_ref, it, d, nb_log2, S):
    nb = 1 << nb_log2
    r = lax.shift_right_logical(it, nb_log2)
    blk = lax.bitwise_and(it, nb - 1)
    qs = blk * (SWA_BLK * d) + r
    ps = jnp.maximum(blk - 1, 0) * (SWA_BLK * d) + r
    if d > 1:
        rows_c, rows_p = pl.ds(qs, SWA_BLK, stride=d), pl.ds(ps, SWA_BLK, stride=d)
    else:
        rows_c, rows_p = pl.ds(pl.multiple_of(qs, SWA_BLK), SWA_BLK), pl.ds(pl.multiple_of(ps, SWA_BLK), SWA_BLK)
    return rows_c, rows_p, blk > 0


def _swa_prev_modes(nb):
    if nb >= SWA_UNROLL:
        return ["load"] + ["reuse"] * (SWA_UNROLL - 1)
    return ["none" if u % nb == 0 else "reuse" for u in range(SWA_UNROLL)]


def _swa_fwd(proj, bt, S):
    scale = SWA_HD ** -0.5

    def body(q_ref, k_ref, v_ref, bt_ref, o_ref, lse0_ref, lse1_ref, m0_scr, m1_scr, a0_scr, a1_scr):
        lane = lax.broadcasted_iota(jnp.int32, (SWA_BLK, LANE), 1)
        h0 = lane < SWA_HD
        qi = lax.broadcasted_iota(jnp.int32, (SWA_BLK, SWA_BLK), 0)
        kj = lax.broadcasted_iota(jnp.int32, (SWA_BLK, SWA_BLK), 1)
        lower = kj <= qi
        ones16 = jnp.ones((LANE, SWA_BLK), bf16)
        m_scrs = (m0_scr, m1_scr)
        a_scrs = (a0_scr, a1_scr)
        for p, (_, d) in reversed(list(enumerate(PATTERNS))):
            nb_log2 = int(math.log2(S // d // SWA_BLK))
            first = p == len(PATTERNS) - 1

            def lp(i, carry, p=p, d=d, nb_log2=nb_log2, first=first):
                heads = [h0, jnp.logical_not(h0)]
                modes = _swa_prev_modes(1 << nb_log2)
                tiles = []
                kc_f = None
                for u in range(SWA_UNROLL):
                    rows_c, rows_p, has_prev = _swa_tiles(q_ref, k_ref, v_ref, i * SWA_UNROLL + u, d, nb_log2, S)
                    kp_f = {"load": lambda: k_ref[rows_p, :], "reuse": lambda: kc_f, "none": lambda: None}[modes[u]]()
                    has_prev = {"load": has_prev, "reuse": True, "none": False}[modes[u]]
                    q = q_ref[rows_c, :]
                    kc_f = k_ref[rows_c, :]
                    kc = kc_f.astype(bf16)
                    logits = []
                    for mh in heads:
                        q_h = jnp.where(mh, q, 0.0)
                        qh = q_h.astype(bf16)
                        if kp_f is None:
                            logits.append((_dot_nt(qh, kc), None, None))
                        else:
                            logits.append((_dot_nt(qh, kc), _dot_nt(qh, kp_f.astype(bf16)), _dot((q_h * kp_f).astype(bf16), ones16)))
                    tiles.append((rows_c, rows_p, has_prev, logits))
                probs = []
                for rows_c, rows_p, has_prev, logits in tiles:
                    per_head = []
                    for h, (s_c, s_p, far) in enumerate(logits):
                        if has_prev is False:
                            s = jnp.where(lower, s_c * scale + bt_ref[p, h], NEG)
                            s_far = None
                        else:
                            s = jnp.where(lower, s_c, s_p) * scale + bt_ref[p, h]
                            s_far = far * scale + bt_ref[p, h, SWA_BLK - 1:SWA_BLK, 0:1]
                            if has_prev is not True:
                                s = jnp.where(jnp.logical_or(lower, has_prev), s, NEG)
                                s_far = jnp.where(has_prev, s_far, NEG)
                        mn = jnp.max(s, axis=1, keepdims=True)
                        if s_far is not None:
                            mn = jnp.maximum(s_far, mn)
                        alpha = None
                        if not first:
                            mo = m_scrs[h][rows_c, :]
                            mn = jnp.maximum(mo, mn)
                            alpha = jnp.exp(mo - mn)
                        mn = jnp.broadcast_to(mn, (SWA_BLK, LANE))
                        pm = jnp.exp(s - mn)
                        per_head.append((mn, alpha, None if s_far is None else jnp.exp(s_far - mn),
                                         jnp.where(lower, pm, 0.0).astype(bf16),
                                         None if s_far is None else jnp.where(lower, 0.0, pm).astype(bf16)))
                    probs.append(per_head)
                acc_old = [None if first else (a0_scr[t[0], :], a1_scr[t[0], :]) for t in tiles]
                done = []
                vc = None
                for u, ((rows_c, rows_p, _, _), per_head, old) in enumerate(zip(tiles, probs, acc_old)):
                    vp = {"load": lambda: v_ref[rows_p, :], "reuse": lambda: vc, "none": lambda: None}[modes[u]]()
                    vc = v_ref[rows_c, :]
                    acc_new = []
                    for h, (mn, alpha, p_far, pc16, pp16) in enumerate(per_head):
                        pv = _dot(pc16, jnp.where(heads[h], vc, 1.0).astype(bf16))
                        if pp16 is not None:
                            vpa = jnp.where(heads[h], vp, 1.0)
                            pv = pv + _dot(pp16, vpa.astype(bf16)) + p_far * vpa
                        acc_new.append(pv if first else alpha * old[h] + pv)
                    done.append((rows_c, per_head[0][0], per_head[1][0], acc_new[0], acc_new[1]))
                for rows_c, m0_new, m1_new, a0_new, a1_new in done:
                    m0_scr[rows_c, :] = m0_new
                    m1_scr[rows_c, :] = m1_new
                    a0_scr[rows_c, :] = a0_new
                    a1_scr[rows_c, :] = a1_new
                return carry

            lax.fori_loop(0, S // SWA_BLK // SWA_UNROLL, lp, 0)

        def fin(c, carry):
            rows = pl.ds(pl.multiple_of(c * SWA_BLK, SWA_BLK), SWA_BLK)
            a0 = a0_scr[rows, :]
            a1 = a1_scr[rows, :]
            l0 = jnp.where(h0, pltpu.roll(a0, SWA_HD, 1), a0)
            l1 = jnp.where(h0, a1, pltpu.roll(a1, SWA_HD, 1))
            o_ref[rows, :] = jnp.where(h0, a0 / l0, a1 / l1)
            lse0_ref[rows, :] = m0_scr[rows, :] + jnp.log(l0)
            lse1_ref[rows, :] = m1_scr[rows, :] + jnp.log(l1)
            return carry

        lax.fori_loop(0, S // SWA_BLK, fin, 0)

    qb = COL_B // LANE
    col = lambda c: pl.BlockSpec((S, LANE), lambda hp, c=c: (0, c + hp))
    return pl.pallas_call(
        body, name="swa_fwd", grid=(4,),
        in_specs=[col(qb), col(qb + 4), col(qb + 8), pl.BlockSpec((3, 2, SWA_BLK, SWA_BLK), lambda hp: (0, hp, 0, 0))],
        out_specs=[col(0), col(0), col(0)],
        out_shape=[SDS((S, SWA_W), f32)] * 3,
        scratch_shapes=[pltpu.VMEM((S, LANE), f32)] * 4,
        compiler_params=_cp(1, VMEM_LIMIT),
    )(proj, proj, proj, bt)


def _swa_bwd(proj, bt, nd, lse0, lse1, d_oab, after, S):
    scale = SWA_HD ** -0.5

    def body(q_ref, k_ref, v_ref, bt_ref, nd_scr, lse0_ref, lse1_ref, do_ref, after_ref, dq_ref, dk_ref, dv_ref, dsb_ref,
             dq_scr, dk_scr, dv_scr):
        lane = lax.broadcasted_iota(jnp.int32, (SWA_BLK, LANE), 1)
        h0 = lane < SWA_HD
        qi = lax.broadcasted_iota(jnp.int32, (SWA_BLK, SWA_BLK), 0)
        kj = lax.broadcasted_iota(jnp.int32, (SWA_BLK, SWA_BLK), 1)
        lower = kj <= qi
        eye = kj == qi
        rel127 = jnp.logical_or(kj == qi + 1, jnp.logical_and(qi == SWA_BLK - 1, kj == 0))
        ones16 = jnp.ones((LANE, SWA_BLK), bf16)
        lse_refs = (lse0_ref, lse1_ref)
        dk_scr[...] = jnp.zeros((S, LANE), f32)
        dv_scr[...] = jnp.zeros((S, LANE), f32)
        dsb_ref[...] = jnp.zeros_like(dsb_ref)

        for p, (_, d) in reversed(list(enumerate(PATTERNS))):
            nb_log2 = int(math.log2(S // d // SWA_BLK))
            first = p == len(PATTERNS) - 1

            def lp(i, carry, p=p, d=d, nb_log2=nb_log2, first=first):
                heads = [h0, jnp.logical_not(h0)]
                modes = _swa_prev_modes(1 << nb_log2)
                tiles = []
                kc_f = vc_f = None
                for u in range(SWA_UNROLL):
                    rows_c, rows_p, has_prev = _swa_tiles(q_ref, k_ref, v_ref, i * SWA_UNROLL + u, d, nb_log2, S)
                    kp_f = {"load": lambda: k_ref[rows_p, :], "reuse": lambda: kc_f, "none": lambda: None}[modes[u]]()
                    vp_f = {"load": lambda: v_ref[rows_p, :], "reuse": lambda: vc_f, "none": lambda: None}[modes[u]]()
                    has_prev = {"load": has_prev, "reuse": True, "none": False}[modes[u]]
                    q = q_ref[rows_c, :]
                    kc_f = k_ref[rows_c, :]
                    vc_f = v_ref[rows_c, :]
                    kc = kc_f.astype(bf16)
                    kp = None if kp_f is None else kp_f.astype(bf16)
                    do = do_ref[rows_c, :]
                    nd = nd_scr[rows_c, :]
                    per_head = []
                    for mh in heads:
                        q_h = jnp.where(mh, q, 0.0)
                        do_a = jnp.where(mh, do, nd)
                        qh = q_h.astype(bf16)
                        doa = do_a.astype(bf16)
                        doh = jnp.where(mh, do, 0.0).astype(bf16)
                        dd_c = _dot_nt(doa, jnp.where(mh, vc_f, 1.0).astype(bf16))
                        if kp_f is None:
                            per_head.append((qh, doh, _dot_nt(qh, kc), None, None, dd_c, None, None))
                        else:
                            vpa = jnp.where(mh, vp_f, 1.0)
                            per_head.append((qh, doh, _dot_nt(qh, kc), _dot_nt(qh, kp), _dot((q_h * kp_f).astype(bf16), ones16),
                                             dd_c, _dot_nt(doa, vpa.astype(bf16)), _dot((do_a * vpa).astype(bf16), ones16)))
                    tiles.append((rows_c, rows_p, has_prev, kc, kp, per_head))
                grads = []
                for rows_c, rows_p, has_prev, kc, kp, per_head in tiles:
                    out = []
                    for h, (qh, doh, s_c, s_p, far, dd_c, dd_p, dd_far) in enumerate(per_head):
                        lse_h = lse_refs[h][rows_c, :]
                        if has_prev is False:
                            pm = jnp.exp(jnp.where(lower, s_c * scale + bt_ref[p, h], NEG) - lse_h)
                            dsm = pm * dd_c
                            out.append((dsm, dsm.astype(bf16), None, pm.astype(bf16), None))
                            continue
                        s = jnp.where(lower, s_c, s_p) * scale + bt_ref[p, h]
                        s_far = far * scale + bt_ref[p, h, SWA_BLK - 1:SWA_BLK, 0:1]
                        if has_prev is not True:
                            s = jnp.where(jnp.logical_or(lower, has_prev), s, NEG)
                            s_far = jnp.where(has_prev, s_far, NEG)
                        pm = jnp.exp(s - lse_h)
                        p_far = jnp.exp(s_far - lse_h)
                        dsm = pm * jnp.where(lower, dd_c, dd_p)
                        ds_far = p_far * dd_far
                        out.append((dsm + jnp.where(rel127, ds_far, 0.0),
                                    jnp.where(lower, dsm, 0.0).astype(bf16),
                                    jnp.where(lower, jnp.where(eye, ds_far, 0.0), dsm).astype(bf16),
                                    jnp.where(lower, pm, 0.0).astype(bf16),
                                    jnp.where(lower, jnp.where(eye, p_far, 0.0), pm).astype(bf16)))
                    grads.append(out)
                done = []
                add = lambda acc, t: t if acc is None else acc + t
                for (rows_c, rows_p, _, kc, kp, per_head), out in zip(tiles, grads):
                    dq_t = dkc_t = dkp_t = dvc_t = dvp_t = None
                    for h, (_, dsc16, dsp16, pc16, pp16) in enumerate(out):
                        qh, doh = per_head[h][0], per_head[h][1]
                        dq_h = _dot(dsc16, kc)
                        dkc_t = add(dkc_t, _dot_tn(dsc16, qh) * scale)
                        dvc_t = add(dvc_t, _dot_tn(pc16, doh))
                        if dsp16 is not None:
                            dq_h = dq_h + _dot(dsp16, kp)
                            dkp_t = add(dkp_t, _dot_tn(dsp16, qh) * scale)
                            dvp_t = add(dvp_t, _dot_tn(pp16, doh))
                        dq_t = add(dq_t, jnp.where(heads[h], dq_h * scale, 0.0))
                    done.append([rows_c, rows_p, dq_t, dkc_t, dkp_t, dvc_t, dvp_t])
                for u in range(1, SWA_UNROLL):
                    if modes[u] == "reuse":
                        done[u - 1][3] = done[u - 1][3] + done[u][4]
                        done[u - 1][5] = done[u - 1][5] + done[u][6]
                for h in range(2):
                    tot = grads[0][h][0]
                    for g in grads[1:]:
                        tot = tot + g[h][0]
                    dsb_ref[0, p, h] += tot
                for u, (rows_c, rows_p, dq_t, dkc_t, dkp_t, dvc_t, dvp_t) in enumerate(done):
                    dq_scr[rows_c, :] = dq_t if first else dq_scr[rows_c, :] + dq_t
                    dk_scr[rows_c, :] = dk_scr[rows_c, :] + dkc_t
                    dv_scr[rows_c, :] = dv_scr[rows_c, :] + dvc_t
                    if modes[u] == "load":
                        dk_scr[rows_p, :] = dk_scr[rows_p, :] + dkp_t
                        dv_scr[rows_p, :] = dv_scr[rows_p, :] + dvp_t
                return carry

            lax.fori_loop(0, S // SWA_BLK // SWA_UNROLL, lp, 0)
        dq_ref[...] = dq_scr[...].astype(bf16)
        dk_ref[...] = dk_scr[...].astype(bf16)
        dv_ref[...] = dv_scr[...].astype(bf16)

    qb = COL_B // LANE
    col = lambda c: pl.BlockSpec((S, LANE), lambda hp, c=c: (0, c + hp))
    return pl.pallas_call(
        body, name="swa_bwd", grid=(4,),
        in_specs=[col(qb), col(qb + 4), col(qb + 8),
                  pl.BlockSpec((3, 2, SWA_BLK, SWA_BLK), lambda hp: (0, hp, 0, 0)),
                  col(0), col(0), col(0), col(4), _ANY],
        out_specs=[col(0), col(0), col(0),
                   pl.BlockSpec((1, 3, 2, SWA_BLK, SWA_BLK), lambda hp: (hp, 0, 0, 0, 0))],
        out_shape=[SDS((S, SWA_W), bf16)] * 3 + [SDS((4, 3, 2, SWA_BLK, SWA_BLK), f32)],
        scratch_shapes=[pltpu.VMEM((S, LANE), f32)] * 3,
        compiler_params=_cp(1, VMEM_LIMIT),
    )(proj, proj, proj, bt, nd, lse0, lse1, d_oab, after)


def _mix_fwd(oa, ob, w_out, x, g_post, S):
    TS = 512

    def body(oa_ref, ob_ref, w_ref, x_ref, g_ref, mix_ref, x1_ref):
        mix = _dot(oa_ref[...].astype(bf16), w_ref[0:GDN_W, :]) + _dot(ob_ref[...].astype(bf16), w_ref[GDN_W:D_MODEL, :])
        r = lax.rsqrt(jnp.mean(mix * mix, axis=-1, keepdims=True) + RMS_EPS)
        mix_ref[...] = mix
        x1_ref[...] = x_ref[...] + mix * r * g_ref[...]

    row = lambda w: pl.BlockSpec((TS, w), lambda i: (i, 0))
    return pl.pallas_call(
        body, name="mix_fwd", grid=(S // TS,),
        in_specs=[row(GDN_W), row(SWA_W), _resident_spec((D_MODEL, D_MODEL)), row(D_MODEL), _const_spec((1, D_MODEL))],
        out_specs=[row(D_MODEL), row(D_MODEL)],
        out_shape=[SDS((S, D_MODEL), f32), SDS((S, D_MODEL), f32)],
        compiler_params=_cp(1, VMEM_LIMIT),
    )(oa, ob, w_out, x, g_post)


def _mix_bwd(dx1, mix, g_post, w_out, ob, S):
    TS = 512

    def body(dx1_ref, mix_ref, g_ref, w_ref, ob_ref, dmix_ref, doab_ref, dg_ref, nd_ref):
        @pl.when(pl.program_id(0) == 0)
        def _():
            dg_ref[...] = jnp.zeros_like(dg_ref)

        mix = mix_ref[...]
        dz = dx1_ref[...]
        r = lax.rsqrt(jnp.mean(mix * mix, axis=-1, keepdims=True) + RMS_EPS)
        n = mix * r
        dg_ref[...] += jnp.sum(dz * n, axis=0, keepdims=True)
        dn = dz * g_ref[...]
        dmix = (r * (dn - n * jnp.mean(dn * n, axis=-1, keepdims=True))).astype(bf16)
        dmix_ref[...] = dmix
        doab = _dot_nt(dmix, w_ref[...])
        doab_ref[...] = doab
        hi_ = lax.shift_right_logical(lax.broadcasted_iota(jnp.int32, (SWA_W, SWA_W), 0), 6)
        hj_ = lax.shift_right_logical(lax.broadcasted_iota(jnp.int32, (SWA_W, SWA_W), 1), 6)
        swap = (hi_ == lax.bitwise_xor(hj_, 1)).astype(bf16)
        dlt = doab[:, GDN_W:] * ob_ref[...]
        hi = dlt.astype(bf16)
        nd_ref[...] = (_dot(hi, swap) + _dot((dlt - hi.astype(f32)).astype(bf16), swap)) * (-1.0 / SWA_HD)

    row = lambda w=D_MODEL: pl.BlockSpec((TS, w), lambda i: (i, 0))
    return pl.pallas_call(
        body, name="mix_bwd", grid=(S // TS,),
        in_specs=[row(), row(), _const_spec((1, D_MODEL)), _resident_spec((D_MODEL, D_MODEL)), row(SWA_W)],
        out_specs=[row(), row(), _const_spec((1, D_MODEL)), row(SWA_W)],
        out_shape=[SDS((S, D_MODEL), bf16), SDS((S, D_MODEL), f32), SDS((1, D_MODEL), f32), SDS((S, SWA_W), f32)],
        compiler_params=_cp(1, VMEM_LIMIT),
    )(dx1, mix, g_post, w_out, ob)


FFN_TS = 256
FFN_CH = 1408


def _ffn(x1, tgt, g_pre, g_post, wg, wu, wd, S):
    def body(x1_ref, t_ref, gp_ref, gq_ref, wg_ref, wu_ref, wd_ref,
             dx1_ref, h2_ref, act_ref, dgate_ref, dup_ref, df_ref, loss_ref, dgp_ref, dgq_ref, gate_scr, up_scr):
        @pl.when(pl.program_id(0) == 0)
        def _():
            loss_ref[...] = jnp.zeros_like(loss_ref)
            dgp_ref[...] = jnp.zeros_like(dgp_ref)
            dgq_ref[...] = jnp.zeros_like(dgq_ref)

        x1v = x1_ref[...]
        gp = gp_ref[...]
        gq = gq_ref[...]
        r2 = lax.rsqrt(jnp.mean(x1v * x1v, axis=-1, keepdims=True) + RMS_EPS)
        n2 = x1v * r2
        h2 = (n2 * gp).astype(bf16)
        h2_ref[...] = h2
        chunks = [slice(c * FFN_CH, (c + 1) * FFN_CH) for c in range(D_FF // FFN_CH)]
        for cs in chunks:
            gate_scr[:, cs] = _dot_nt(h2, wg_ref[cs, :])
            up_scr[:, cs] = _dot_nt(h2, wu_ref[cs, :])
        acts = []
        for cs in chunks:
            gate = gate_scr[:, cs]
            act = (gate * _sigmoid(gate) * up_scr[:, cs]).astype(bf16)
            act_ref[:, cs] = act
            acts.append(act)
        f = _dot(acts[0], wd_ref[chunks[0], :])
        for act, cs in zip(acts[1:], chunks[1:]):
            f = f + _dot(act, wd_ref[cs, :])
        r3 = lax.rsqrt(jnp.mean(f * f, axis=-1, keepdims=True) + RMS_EPS)
        n3 = f * r3
        err = x1v + n3 * gq - t_ref[...]
        loss_ref[...] += 0.5 * jnp.sum(jnp.mean(err * err, axis=-1, keepdims=True), axis=0, keepdims=True)
        dy = err * (1.0 / D_MODEL)
        dgq_ref[...] += jnp.sum(dy * n3, axis=0, keepdims=True)
        dn3 = dy * gq
        df = (r3 * (dn3 - n3 * jnp.mean(dn3 * n3, axis=-1, keepdims=True))).astype(bf16)
        df_ref[...] = df
        dacts = [_dot_nt(df, wd_ref[cs, :]) for cs in chunks]
        dgs = []
        for dact, cs in zip(dacts, chunks):
            gate = gate_scr[:, cs]
            sg = _sigmoid(gate)
            dup = (dact * gate * sg).astype(bf16)
            dgate = (dact * up_scr[:, cs] * (sg * (1.0 + gate * (1.0 - sg)))).astype(bf16)
            dup_ref[:, cs] = dup
            dgate_ref[:, cs] = dgate
            dgs.append((dgate, dup))
        dh2 = None
        for (dgate, dup), cs in zip(dgs, chunks):
            t = _dot(dgate, wg_ref[cs, :]) + _dot(dup, wu_ref[cs, :])
            dh2 = t if dh2 is None else dh2 + t
        dgp_ref[...] += jnp.sum(dh2 * n2, axis=0, keepdims=True)
        dn2 = dh2 * gp
        dx1_ref[...] = dy + r2 * (dn2 - n2 * jnp.mean(dn2 * n2, axis=-1, keepdims=True))

    row = lambda w: pl.BlockSpec((FFN_TS, w), lambda i: (i, 0))
    vec = _const_spec((1, D_MODEL))
    return pl.pallas_call(
        body, name="ffn_fwd_bwd", grid=(S // FFN_TS,),
        in_specs=[row(D_MODEL), row(D_MODEL), vec, vec, _resident_spec((D_FF, D_MODEL)), _resident_spec((D_FF, D_MODEL)),
                  _resident_spec((D_FF, D_MODEL))],
        out_specs=[row(D_MODEL), row(D_MODEL), row(D_FF), row(D_FF), row(D_FF), row(D_MODEL), _const_spec((1, LANE)), vec, vec],
        out_shape=[SDS((S, D_MODEL), f32), SDS((S, D_MODEL), bf16), SDS((S, D_FF), bf16), SDS((S, D_FF), bf16),
                   SDS((S, D_FF), bf16), SDS((S, D_MODEL), bf16), SDS((1, LANE), f32), SDS((1, D_MODEL), f32),
                   SDS((1, D_MODEL), f32)],
        scratch_shapes=[pltpu.VMEM((FFN_TS, D_FF), f32), pltpu.VMEM((FFN_TS, D_FF), f32)],
        compiler_params=_cp(1, VMEM_LIMIT),
    )(x1, tgt, g_pre, g_post, wg, wu, wd)


def _proj_bwd(x, dx1, g_pre, wcat, segs, after, S):
    TS = 512
    n = len(segs)
    cols = [(c0, a.shape[1]) for a, c0 in segs]

    def body(*refs):
        x_ref, dx1_ref, g_ref, w_ref = refs[:4]
        seg_refs = refs[4:4 + n]
        gx_ref, dg_ref = refs[5 + n:]

        @pl.when(pl.program_id(0) == 0)
        def _():
            dg_ref[...] = jnp.zeros_like(dg_ref)

        dh = jnp.zeros((TS, D_MODEL), f32)
        for s_ref, (c0, w) in zip(seg_refs, cols):
            dh = dh + _dot_nt(s_ref[...], w_ref[:, c0:c0 + w])
        xv = x_ref[...]
        g = g_ref[...]
        r = lax.rsqrt(jnp.mean(xv * xv, axis=-1, keepdims=True) + RMS_EPS)
        nx = xv * r
        dg_ref[...] += jnp.sum(dh * nx, axis=0, keepdims=True)
        dn = dh * g
        gx_ref[...] = dx1_ref[...] + r * (dn - nx * jnp.mean(dn * nx, axis=-1, keepdims=True))

    row = lambda w: pl.BlockSpec((TS, w), lambda i: (i, 0))
    return pl.pallas_call(
        body, name="proj_bwd", grid=(S // TS,),
        in_specs=[row(D_MODEL), row(D_MODEL), _const_spec((1, D_MODEL)), _resident_spec((D_MODEL, NCOL))]
                 + [row(w) for _, w in cols] + [_ANY],
        out_specs=[row(D_MODEL), _const_spec((1, D_MODEL))],
        out_shape=[SDS((S, D_MODEL), f32), SDS((1, D_MODEL), f32)],
        compiler_params=_cp(1, VMEM_LIMIT),
    )(x, dx1, g_pre, wcat, *[a for a, _ in segs], after)


def _wgrad(a, b, S, name):
    TS = 1024
    K = a.shape[1]
    N = b.shape[1]
    TN = next(t for t in (512, 1408, N) if N % t == 0)

    def body(a_ref, b_ref, o_ref, acc):
        @pl.when(pl.program_id(1) == 0)
        def _():
            acc[...] = jnp.zeros_like(acc)

        acc[...] += _dot_tn(a_ref[...].astype(bf16), b_ref[...])

        @pl.when(pl.program_id(1) == pl.num_programs(1) - 1)
        def _():
            o_ref[...] = acc[...].astype(bf16)

    return pl.pallas_call(
        body, name=name, grid=(N // TN, S // TS),
        in_specs=[pl.BlockSpec((TS, K), lambda j, s: (s, 0)), pl.BlockSpec((TS, TN), lambda j, s: (s, j))],
        out_specs=pl.BlockSpec((K, TN), lambda j, s: (0, j)), out_shape=SDS((K, N), bf16),
        scratch_shapes=[pltpu.VMEM((K, TN), f32)],
        compiler_params=_cp(2, VMEM_LIMIT),
    )(a, b)


def _wgrad_out(oa, ob, dmix, S):
    TS, TN = 1024, 512

    def body(a_ref, b_ref, d_ref, o_ref, acc):
        @pl.when(pl.program_id(1) == 0)
        def _():
            acc[...] = jnp.zeros_like(acc)

        d = d_ref[...]
        acc[0:GDN_W, :] += _dot_tn(a_ref[...].astype(bf16), d)
        acc[GDN_W:D_MODEL, :] += _dot_tn(b_ref[...].astype(bf16), d)

        @pl.when(pl.program_id(1) == pl.num_programs(1) - 1)
        def _():
            o_ref[...] = acc[...].astype(bf16)

    tok = lambda w: pl.BlockSpec((TS, w), lambda j, s: (s, 0))
    return pl.pallas_call(
        body, name="wgrad_out", grid=(D_MODEL // TN, S // TS),
        in_specs=[tok(GDN_W), tok(SWA_W), pl.BlockSpec((TS, TN), lambda j, s: (s, j))],
        out_specs=pl.BlockSpec((D_MODEL, TN), lambda j, s: (0, j)), out_shape=SDS((D_MODEL, D_MODEL), bf16),
        scratch_shapes=[pltpu.VMEM((D_MODEL, TN), f32)],
        compiler_params=_cp(2, VMEM_LIMIT),
    )(oa, ob, dmix)


def _w_in_pieces():
    n_a, n_g = 4 * GDN_W, 2 * GDN_HEADS
    cb = IN_COLS // N_DEV
    bounds = [(0, n_a, COL_A), (n_a, n_a + n_g, COL_G), (n_a + n_g, IN_COLS, COL_B)]
    out = []
    for j in range(N_DEV):
        lo, hi = j * cb, (j + 1) * cb
        for s0, s1, dst in bounds:
            a, b = max(lo, s0), min(hi, s1)
            if a < b:
                out.append((j, a - lo, b - a, dst + a - s0))
    return out


def _wcat_from_blocks(g_in):
    TR = 256
    cb = IN_COLS // N_DEV
    pieces = _w_in_pieces()

    def body(w_ref, o_ref):
        o_ref[:, COL_G:NCOL] = jnp.zeros((TR, NCOL - COL_G), bf16)
        for j, off, w, dst in pieces:
            o_ref[:, dst:dst + w] = w_ref[j, :, off:off + w]

    return pl.pallas_call(
        body, name="wcat_from_blocks", grid=(D_MODEL // TR,),
        in_specs=[pl.BlockSpec((N_DEV, TR, cb), lambda i: (0, i, 0))],
        out_specs=pl.BlockSpec((TR, NCOL), lambda i: (i, 0)),
        out_shape=SDS((D_MODEL, NCOL), bf16),
        compiler_params=_cp(1, VMEM_LIMIT),
    )(g_in)


def _wgrad_in(h1, segs, S):
    TS = 1024
    n = len(segs)
    cols = [(c0, a.shape[1]) for a, c0 in segs]
    cb = IN_COLS // N_DEV
    pieces = _w_in_pieces()

    def body(*refs):
        h_ref = refs[0]
        seg_refs = refs[1:1 + n]
        o_ref, acc = refs[1 + n], refs[2 + n]

        @pl.when(pl.program_id(0) == 0)
        def _():
            acc[...] = jnp.zeros_like(acc)

        h = h_ref[...]
        for s_ref, (c0, w) in zip(seg_refs, cols):
            acc[:, c0:c0 + w] += _dot_tn(h, s_ref[...])

        @pl.when(pl.program_id(0) == pl.num_programs(0) - 1)
        def _():
            for j, off, w, src in pieces:
                o_ref[j, :, off:off + w] = acc[:, src:src + w].astype(bf16)

    row = lambda w: pl.BlockSpec((TS, w), lambda i: (i, 0))
    return pl.pallas_call(
        body, name="wgrad_in", grid=(S // TS,),
        in_specs=[row(D_MODEL)] + [row(w) for _, w in cols],
        out_specs=_const_spec((N_DEV, D_MODEL, cb)),
        out_shape=SDS((N_DEV, D_MODEL, cb), bf16),
        scratch_shapes=[pltpu.VMEM((D_MODEL, NCOL), f32)],
        compiler_params=_cp(1, VMEM_LIMIT),
    )(h1, *[a for a, _ in segs])


def _adamw(recv, src, me, w, m, v, name):
    R, C = w.shape
    TR = 256 if R % 256 == 0 else R
    c1 = 1.0 / (1.0 - ADAM_B1 ** ADAM_STEP)
    c2 = 1.0 / (1.0 - ADAM_B2 ** ADAM_STEP)

    def body(me_ref, r_ref, own_ref, w_ref, m_ref, v_ref, g_out, d_out, m_out, v_out):
        g = None
        for s in range(N_DEV):
            t = jnp.where(me_ref[0] == s, own_ref[0], r_ref[s]).astype(f32)
            g = t if g is None else g + t
        mn = ADAM_B1 * m_ref[...] + (1.0 - ADAM_B1) * g
        vn = ADAM_B2 * v_ref[...] + (1.0 - ADAM_B2) * (g * g)
        g_out[...] = g
        m_out[...] = mn
        v_out[...] = vn
        d_out[...] = -ADAM_LR * ((mn * c1) / (jnp.sqrt(vn * c2) + ADAM_EPS) + ADAM_WD * w_ref[...])

    blk = pl.BlockSpec((TR, C), lambda i, me_ref: (i, 0))
    return pl.pallas_call(
        body, name=name,
        grid_spec=pltpu.PrefetchScalarGridSpec(
            num_scalar_prefetch=1, grid=(R // TR,),
            in_specs=[pl.BlockSpec((N_DEV, TR, C), lambda i, me_ref: (0, i, 0)),
                      pl.BlockSpec((1, TR, C), lambda i, me_ref: (me_ref[0], i, 0)), blk, blk, blk],
            out_specs=[blk, blk, blk, blk]),
        out_shape=[SDS((R, C), f32)] * 4,
        compiler_params=_cp(1, VMEM_LIMIT),
    )(me, recv, src, w, m, v)


MESH = pl.DeviceIdType.MESH
_ANY = pl.BlockSpec(memory_space=pl.ANY)


def _flip(v, d):
    return 1 - v if d else v


def _all_gather(shards):
    n = len(shards)

    def body(*refs):
        ins = refs[:n]
        outs = refs[n:2 * n]
        send_sems, recv_sems, local_sems = refs[2 * n:]
        x, y, c = lax.axis_index("x"), lax.axis_index("y"), lax.axis_index("c")
        me, sibling = (x, y, c), (x, y, 1 - c)
        chips = [(1 - x, y), (x, 1 - y), (1 - x, 1 - y)]

        def slot(px, py, pc):
            return 4 * px + 2 * py + pc

        def copy(a, k, block, to, src=None):
            dst = outs[a].at[slot(*block)]
            return pltpu.make_async_remote_copy(src_ref=dst if src is None else src, dst_ref=dst,
                                                send_sem=send_sems.at[a, k], recv_sem=recv_sems.at[a, k],
                                                device_id=to, device_id_type=MESH)

        mine, first, passed = [], [], []
        for a in range(n):
            cp = pltpu.make_async_copy(ins[a], outs[a].at[slot(*me)], local_sems.at[a])
            cp.start()
            mine.append(cp)
            fs = [copy(a, 0, me, sibling, src=ins[a])]
            fs += [copy(a, 1 + j, me, (*chip, c), src=ins[a]) for j, chip in enumerate(chips)]
            for cp in fs:
                cp.start()
            first += fs
        for j, chip in enumerate(chips):
            for a in range(n):
                copy(a, 1 + j, (*chip, c), me).wait_recv()
                cp = copy(a, 4 + j, (*chip, c), sibling)
                cp.start()
                passed.append(cp)
        for a in range(n):
            copy(a, 0, sibling, me).wait_recv()
            for j, chip in enumerate(chips):
                copy(a, 4 + j, (*chip, 1 - c), me).wait_recv()
        for cp in first + passed:
            cp.wait_send()
        for cp in mine:
            cp.wait()

    return pl.pallas_call(
        body, name="weight_all_gather",
        in_specs=[_ANY] * n, out_specs=[_ANY] * n,
        out_shape=[SDS((N_DEV,) + s.shape, s.dtype) for s in shards],
        scratch_shapes=[pltpu.SemaphoreType.DMA((n, 7)), pltpu.SemaphoreType.DMA((n, 7)), pltpu.SemaphoreType.DMA((n,))],
        compiler_params=pltpu.CompilerParams(has_side_effects=True),
    )(*shards)


def _grad_exchange(blocked, whole):
    arrs = list(blocked) + list(whole)
    n, nb = len(arrs), len(blocked)
    rel = [(dx, dy, dc) for dx in (0, 1) for dy in (0, 1) for dc in (0, 1) if dx or dy or dc]

    def body(*refs):
        ins = refs[:n]
        outs = refs[n:2 * n]
        send_sems, recv_sems, local_sems = refs[2 * n:]
        x, y, c = lax.axis_index("x"), lax.axis_index("y"), lax.axis_index("c")
        me = 4 * x + 2 * y + c
        sends, locs = [], []
        for a in range(n):
            cp = pltpu.make_async_copy(ins[a].at[me] if a < nb else ins[a], outs[a].at[me], local_sems.at[a])
            cp.start()
            locs.append(cp)
            for k, (dx, dy, dc) in enumerate(rel):
                peer = (_flip(x, dx), _flip(y, dy), _flip(c, dc))
                pidx = 4 * peer[0] + 2 * peer[1] + peer[2]
                cp = pltpu.make_async_remote_copy(src_ref=ins[a].at[pidx] if a < nb else ins[a], dst_ref=outs[a].at[me],
                                                  send_sem=send_sems.at[a, k], recv_sem=recv_sems.at[a, k],
                                                  device_id=peer, device_id_type=MESH)
                cp.start()
                sends.append(cp)
        for a in range(n):
            for k, (dx, dy, dc) in enumerate(rel):
                peer = (_flip(x, dx), _flip(y, dy), _flip(c, dc))
                pidx = 4 * peer[0] + 2 * peer[1] + peer[2]
                pltpu.make_async_remote_copy(src_ref=outs[a].at[pidx], dst_ref=outs[a].at[pidx],
                                             send_sem=send_sems.at[a, k], recv_sem=recv_sems.at[a, k],
                                             device_id=peer, device_id_type=MESH).wait_recv()
        for cp in sends:
            cp.wait_send()
        for cp in locs:
            cp.wait()

    shapes = [SDS(a.shape, a.dtype) for a in blocked] + [SDS((N_DEV,) + a.shape, a.dtype) for a in whole]
    return pl.pallas_call(
        body, name="grad_exchange",
        in_specs=[_ANY] * n, out_specs=[_ANY] * n, out_shape=shapes,
        scratch_shapes=[pltpu.SemaphoreType.DMA((n, 7)), pltpu.SemaphoreType.DMA((n, 7)), pltpu.SemaphoreType.DMA((n,))],
        compiler_params=pltpu.CompilerParams(has_side_effects=True),
    )(*arrs)


_HBM = pl.BlockSpec(memory_space=pltpu.HBM)
_SEM = pl.BlockSpec(memory_space=pltpu.SEMAPHORE)
_REL = [(dx, dy, dc) for dx in (0, 1) for dy in (0, 1) for dc in (0, 1) if dx or dy or dc]


N_PEER = len(_REL)
_EFFECT = pltpu.SideEffectType.DATAFLOW_SIDE_EFFECTING


def _peer_copies(srcs, lands, send_sems, recv_sems, blocked, as_receiver):
    x, y, c = lax.axis_index("x"), lax.axis_index("y"), lax.axis_index("c")
    me = 4 * x + 2 * y + c
    cps = []
    for a in range(len(srcs)):
        for k, (dx, dy, dc) in enumerate(_REL):
            peer = (_flip(x, dx), _flip(y, dy), _flip(c, dc))
            pidx = 4 * peer[0] + 2 * peer[1] + peer[2]
            cps.append(pltpu.make_async_remote_copy(
                src_ref=srcs[a].at[pidx] if blocked else srcs[a], dst_ref=lands[a].at[pidx if as_receiver else me],
                send_sem=send_sems[a * N_PEER + k], recv_sem=recv_sems[a * N_PEER + k],
                device_id=peer, device_id_type=MESH))
    return cps


def _exchange_start(srcs, after, blocked, name):
    n = len(srcs)
    ns = n * N_PEER
    lands = [lax.empty(s.shape if blocked else (N_DEV,) + s.shape, s.dtype) for s in srcs]

    def body(*refs):
        ins, lnd = refs[:n], refs[n:2 * n]
        outs = refs[2 * n + 1:]
        for cp in _peer_copies(ins, lnd, outs[:ns], outs[ns:2 * ns], blocked, False):
            cp.start()
        outs[-1][...] = jnp.zeros_like(outs[-1])

    res = pl.pallas_call(
        body, name=name,
        in_specs=[_HBM] * (2 * n) + [_ANY],
        out_specs=[_SEM] * (2 * ns) + [_HBM] * (2 * n) + [pl.BlockSpec(memory_space=pltpu.VMEM)],
        out_shape=[pltpu.SemaphoreType.DMA(())] * (2 * ns) + [pltpu.HBM(s.shape, s.dtype) for s in srcs]
                  + [pltpu.HBM(l.shape, l.dtype) for l in lands] + [SDS((8, LANE), f32)],
        input_output_aliases={i: 2 * ns + i for i in range(2 * n)},
        compiler_params=pltpu.CompilerParams(has_side_effects=_EFFECT),
    )(*[pltpu.with_memory_space_constraint(s, pltpu.HBM) for s in srcs],
      *[pltpu.with_memory_space_constraint(l, pltpu.HBM) for l in lands], after)
    return list(res[:2 * ns]), list(res[2 * ns:2 * ns + n]), list(res[2 * ns + n:2 * ns + 2 * n]), res[-1]


def _exchange_wait(sems, srcs, lands, after, blocked, name):
    n = len(srcs)
    ns = n * N_PEER

    def body(*refs):
        ins, lnd = refs[:n], refs[n:2 * n]
        sem_refs = refs[2 * n:2 * n + 2 * ns]
        for cp in _peer_copies(ins, lnd, sem_refs[:ns], sem_refs[ns:], blocked, True):
            cp.wait_send()
            cp.wait_recv()

    res = pl.pallas_call(
        body, name=name,
        in_specs=[_HBM] * (2 * n) + [_SEM] * (2 * ns) + [_ANY],
        out_specs=[_HBM] * (2 * n),
        out_shape=[pltpu.HBM(s.shape, s.dtype) for s in srcs] + [pltpu.HBM(l.shape, l.dtype) for l in lands],
        input_output_aliases={i: i for i in range(2 * n)},
        compiler_params=pltpu.CompilerParams(has_side_effects=_EFFECT),
    )(*srcs, *lands, *sems, after)
    return list(res[:n]), list(res[n:])


def _local_step(x, tgt, wcat, convw, late_weights, early_grads, last_grads, token, a_log, dt_bias, onorm_g, rel_bias,
                g_mix_pre, g_mix_post, g_ffn_pre, g_ffn_post):
    S = x.shape[0]
    bk_np = _bucket_tables()
    bk = jnp.asarray(bk_np)
    bt = _bias_tables(rel_bias, bk)
    proj, h1 = _proj_fwd(x, g_mix_pre, wcat, token, S)
    nu = S // CHUNK * GDN_HEADS
    qkv_u = _gdn_prep(proj, convw, S).reshape(3, nu, CHUNK, GDN_HD)
    intra, t_inv = _gdn_intra_fwd(qkv_u, proj, a_log, dt_bias, S)
    oa, states = _gdn_scan_fwd(intra, proj, onorm_g, S)
    ob, lse0, lse1 = _swa_fwd(proj, bt, S)
    wout, ffn_weights = late_weights(ob)
    mix, x1 = _mix_fwd(oa, ob, wout, x, g_mix_post, S)
    wgate, wup, wdown = ffn_weights(x1)
    dx1, h2, act, dgate_f, dup_f, df, loss, d_gfpre, d_gfpost = _ffn(x1, tgt, g_ffn_pre, g_ffn_post, wgate, wup, wdown, S)
    rows8 = lambda g: g.reshape(N_DEV, D_FF // N_DEV, D_MODEL)
    g_gate = rows8(_wgrad(dgate_f, h2, S, "wgrad_gate"))
    g_up = rows8(_wgrad(dup_f, h2, S, "wgrad_up"))
    g_down = rows8(_wgrad(act, df, S, "wgrad_down"))
    dmix, d_oab, d_gmpost, nd = _mix_bwd(dx1, mix, g_mix_post, wout, ob, S)
    g_out = _wgrad_out(oa, ob, dmix, S)
    token = early_grads(g_out.reshape(N_DEV, D_MODEL // N_DEV, D_MODEL), g_gate, g_up, g_down)
    dqb, dkb, dvb, dsb = _swa_bwd(proj, bt, nd, lse0, lse1, d_oab, token, S)
    *cots, dgate_a, d_og = _gdn_scan_bwd(intra, states, proj, d_oab, onorm_g, token, S)
    dqkv_u, dpg, d_alog, d_dtb = _gdn_intra_bwd(qkv_u, proj, a_log, dt_bias, t_inv, cots, S)
    dqkv_a, d_conv = _gdn_prep_bwd(proj, convw, dqkv_u.reshape(3, S // CHUNK, GDN_HEADS, CHUNK, GDN_HD), S)
    segs = [(dqkv_a, COL_A), (dgate_a, COL_A + 3 * GDN_W), (dqb, COL_B), (dkb, COL_B + SWA_W), (dvb, COL_B + 2 * SWA_W),
            (dpg, COL_G)]
    token = last_grads(_wgrad_in(h1, segs, S), d_conv)
    grad_x, d_gmpre = _proj_bwd(x, dx1, g_mix_pre, wcat, segs, token, S)
    d_rel = _rel_bias_grad(dsb, bk, bk_np)
    small = dict(a_log=d_alog[:, GDN_HEADS:2 * GDN_HEADS], dt_bias=d_dtb[:, GDN_HEADS:2 * GDN_HEADS], onorm_g=d_og, rel_bias=d_rel,
                 g_mix_pre=d_gmpre, g_mix_post=d_gmpost, g_ffn_pre=d_gfpre, g_ffn_post=d_gfpost)
    return loss, grad_x, small


SMALL = ("a_log", "dt_bias", "onorm_g", "rel_bias", "g_mix_pre", "g_mix_post", "g_ffn_pre", "g_ffn_post")
PACK_ROWS = 8


def _pack_small(d, loss=None):
    rest = jnp.concatenate([d["onorm_g"].reshape(-1), d["a_log"].reshape(-1), d["dt_bias"].reshape(-1),
                            d["rel_bias"].reshape(-1)])
    rest = jnp.concatenate([rest, jnp.zeros((D_MODEL - rest.shape[0],), f32)])
    extra = jnp.zeros((D_MODEL,), f32) if loss is None else jnp.concatenate([loss.reshape(1), jnp.zeros((D_MODEL - 1,), f32)])
    rows = [d["g_mix_pre"].reshape(-1), d["g_mix_post"].reshape(-1), d["g_ffn_pre"].reshape(-1),
            d["g_ffn_post"].reshape(-1), rest, extra]
    return jnp.concatenate([jnp.stack(rows), jnp.zeros((PACK_ROWS - len(rows), D_MODEL), f32)], axis=0)


def _unpack_small(p):
    o = GDN_HD
    return dict(g_mix_pre=p[0:1], g_mix_post=p[1:2], g_ffn_pre=p[2:3], g_ffn_post=p[3:4],
                onorm_g=p[4:5, :o], a_log=p[4:5, o:o + 4], dt_bias=p[4:5, o + 4:o + 8],
                rel_bias=p[4, o + 8:o + 8 + NUM_BUCKETS * SWA_HEADS].reshape(NUM_BUCKETS, SWA_HEADS))


def kernel(x, w_in, conv_w, a_log, dt_bias, onorm_g, rel_bias, w_out, g_mix_pre, g_mix_post, w_gate, w_up, w_down, g_ffn_pre, g_ffn_post, loss_target, m_w_in, m_conv_w, m_a_log, m_dt_bias, m_onorm_g, m_rel_bias, m_w_out, m_g_mix_pre, m_g_mix_post, m_w_gate, m_w_up, m_w_down, m_g_ffn_pre, m_g_ffn_post, v_w_in, v_conv_w, v_a_log, v_dt_bias, v_onorm_g, v_rel_bias, v_w_out, v_g_mix_pre, v_g_mix_post, v_w_gate, v_w_up, v_w_down, v_g_ffn_pre, v_g_ffn_post):
    big = ("w_in", "conv_w", "w_out", "w_gate", "w_up", "w_down")
    transposed = ("w_gate", "w_up")
    tr = lambda k, a: a.T if k in transposed else a
    w_sh = {k: tr(k, a[0]) for k, a in dict(w_in=w_in, conv_w=conv_w, w_out=w_out, w_gate=w_gate, w_up=w_up, w_down=w_down).items()}
    m_sh = {k: tr(k, a[0]) for k, a in dict(w_in=m_w_in, conv_w=m_conv_w, w_out=m_w_out, w_gate=m_w_gate, w_up=m_w_up,
                                             w_down=m_w_down).items()}
    v_sh = {k: tr(k, a[0]) for k, a in dict(w_in=v_w_in, conv_w=v_conv_w, w_out=v_w_out, w_gate=v_w_gate, w_up=v_w_up,
                                             w_down=v_w_down).items()}
    w_small = dict(a_log=a_log, dt_bias=dt_bias, onorm_g=onorm_g, rel_bias=rel_bias, g_mix_pre=g_mix_pre,
                   g_mix_post=g_mix_post, g_ffn_pre=g_ffn_pre, g_ffn_post=g_ffn_post)
    m_small = dict(a_log=m_a_log, dt_bias=m_dt_bias, onorm_g=m_onorm_g, rel_bias=m_rel_bias, g_mix_pre=m_g_mix_pre,
                   g_mix_post=m_g_mix_post, g_ffn_pre=m_g_ffn_pre, g_ffn_post=m_g_ffn_post)
    v_small = dict(a_log=v_a_log, dt_bias=v_dt_bias, onorm_g=v_onorm_g, rel_bias=v_rel_bias, g_mix_pre=v_g_mix_pre,
                   g_mix_post=v_g_mix_post, g_ffn_pre=v_g_ffn_pre, g_ffn_post=v_g_ffn_post)

    me = 4 * lax.axis_index("x") + 2 * lax.axis_index("y") + lax.axis_index("c")
    me1 = me.reshape(1).astype(jnp.int32)
    own = lambda full, part: lax.dynamic_update_index_in_dim(full, part, me, 0)
    cols = lambda g: g.reshape(g.shape[0], N_DEV, g.shape[1] // N_DEV).transpose(1, 0, 2)
    late = ("w_out", "w_gate", "w_up", "w_down")

    late_src = [w_sh[k].astype(bf16) for k in late]
    g_in, g_conv = _all_gather([w_sh["w_in"].astype(bf16), w_sh["conv_w"]])
    g_sems, g_src, g_land, g_token = _exchange_start(late_src, g_conv, False, "late_weights_start")
    wcat = _wcat_from_blocks(g_in)
    convw = g_conv.transpose(1, 0, 2).reshape(4, 3 * GDN_W)

    def late_weights(after):
        pick = lambda idx: [g_sems[half * len(late) * N_PEER + a * N_PEER + k] for half in (0, 1) for a in idx for k in range(N_PEER)]
        (s_out,), (l_out,) = _exchange_wait(pick([0]), g_src[:1], g_land[:1], after, False, "w_out_wait")

        def ffn_weights(after2):
            srcs, lands = _exchange_wait(pick([1, 2, 3]), g_src[1:], g_land[1:], after2, False, "ffn_weights_wait")
            return [own(l, s).reshape(D_FF, D_MODEL) for l, s in zip(lands, srcs)]

        return own(l_out, s_out).reshape(D_MODEL, D_MODEL), ffn_weights

    early, last = {}, {}

    def early_grads(*blocks):
        early["sems"], early["src"], early["land"], token = _exchange_start(list(blocks), me1, True, "late_grads_start")
        return token

    def last_grads(gw_in, gw_conv):
        src = [gw_in, cols(gw_conv)]
        last["sems"], last["src"], last["land"], token = _exchange_start(src, me1, True, "last_grads_start")
        return token

    loss_p, grad_x, gsmall = _local_step(
        x[0], loss_target[0], wcat, convw, late_weights, early_grads, last_grads, g_token,
        a_log, dt_bias, onorm_g, rel_bias, g_mix_pre, g_mix_post, g_ffn_pre, g_ffn_post)

    (r_small,) = _grad_exchange([], [_pack_small(gsmall, loss_p[0, 0])])
    outs = {}
    for names, ex, after, name in ((late, early, grad_x, "late_grads_wait"), (("w_in", "conv_w"), last, r_small, "last_grads_wait")):
        srcs, lands = _exchange_wait(ex["sems"], ex["src"], ex["land"], after, True, name)
        for k, l, s in zip(names, lands, srcs):
            outs[k] = _adamw(l, s, me1, w_sh[k], m_sh[k], v_sh[k], "adamw_" + k)
    sm = _adamw(r_small, r_small, me1, _pack_small(w_small), _pack_small(m_small), _pack_small(v_small), "adamw_small")
    loss = sm[0][5, 0]
    sm = [_unpack_small(t) for t in sm]
    for k in SMALL:
        outs[k] = tuple(t[k].reshape(w_small[k].shape) for t in sm)

    order = ("w_in", "conv_w", "a_log", "dt_bias", "onorm_g", "rel_bias", "w_out", "g_mix_pre", "g_mix_post", "w_gate",
             "w_up", "w_down", "g_ffn_pre", "g_ffn_post")
    lead = lambda k, t: tr(k, t)[None] if k in big else t
    res = [loss, grad_x[None]]
    for i in range(4):
        res += [lead(k, outs[k][i]) for k in order]
    return tuple(res)
```

```python
import functools
import math

import numpy as np
import jax
import jax.numpy as jnp
from jax import lax
from jax.experimental import pallas as pl
from jax.experimental.pallas import tpu as pltpu

f32 = jnp.float32
bf16 = jnp.bfloat16
SDS = jax.ShapeDtypeStruct

D_MODEL = 1024
GDN_HEADS = 4
GDN_HD = 128
GDN_W = 512
CHUNK = 64
SWA_HEADS = 8
SWA_HD = 64
SWA_W = 512
D_FF = 2816
IN_COLS = 3592
PATTERNS = ((128, 1), (512, 4), (2048, 16))
SWA_BLK = 128
NUM_BUCKETS = 32
MAX_DISTANCE = 2048
RMS_EPS = 1e-6
NEG = -1e30
N_DEV = 8

COL_A = 0
COL_B = 2048
COL_G = 3584
NCOL = 3712
LANE = 128

ADAM_LR, ADAM_B1, ADAM_B2, ADAM_EPS, ADAM_WD, ADAM_STEP = 0.001, 0.9, 0.999, 1e-08, 0.01, 10

VMEM_LIMIT = 56 * 1024 * 1024

HIGH = lax.Precision.HIGH


def _cp(n_grid=0, vmem=None):
    kw = {}
    if n_grid:
        kw["dimension_semantics"] = ("arbitrary",) * n_grid
    if vmem:
        kw["vmem_limit_bytes"] = vmem
    return pltpu.CompilerParams(**kw)


def _dot(a, b):
    return jnp.dot(a, b, preferred_element_type=f32)


def _dot_nt(a, b):
    return lax.dot_general(a, b, (((1,), (1,)), ((), ())), preferred_element_type=f32)


def _dot_tn(a, b):
    return lax.dot_general(a, b, (((0,), (0,)), ((), ())), preferred_element_type=f32)


def _sigmoid(x):
    return 0.5 * jnp.tanh(0.5 * x) + 0.5


def _softplus(x):
    return jnp.maximum(x, 0.0) + jnp.log(1.0 + jnp.exp(-jnp.abs(x)))


def _const_spec(shape):
    nd = len(shape)
    return pl.BlockSpec(shape, lambda *_: (0,) * nd)


def _resident_spec(shape):
    nd = len(shape)
    return pl.BlockSpec(shape, lambda *_: (0,) * nd, pipeline_mode=pl.Buffered(1))


def _t5_bucket_np(dist):
    max_exact = NUM_BUCKETS // 2
    d = np.maximum(dist, 1).astype(np.float32)
    log_b = max_exact + (np.log(d / np.float32(max_exact)) / np.float32(math.log(MAX_DISTANCE / max_exact))
                         * np.float32(NUM_BUCKETS - max_exact)).astype(np.int32)
    return np.where(dist < max_exact, dist, np.minimum(log_b, NUM_BUCKETS - 1)).astype(np.int32)


def _bucket_tables():
    w = SWA_BLK
    qi = np.arange(w)[:, None]
    kj = np.arange(w)[None, :]
    rel = np.where(kj <= qi, qi - kj, qi + w - kj)
    out = np.zeros((len(PATTERNS), w, w), np.int32)
    for p, (_, dil) in enumerate(PATTERNS):
        steps = _t5_bucket_np(np.arange(w + 1) * dil)
        assert steps[w] == steps[w - 1]
        out[p] = steps[rel]
    return out


def _bias_tables(rel_bias, bk):
    def body(rb_ref, bk_ref, o_ref):
        b_idx = bk_ref[0]
        for h in range(SWA_HEADS):
            def lp(b, acc):
                return jnp.where(b_idx == b, rb_ref[b, h], acc)
            o_ref[0, h] = lax.fori_loop(0, NUM_BUCKETS, lp, jnp.zeros((SWA_BLK, SWA_BLK), f32))

    return pl.pallas_call(
        body, name="bias_tables", grid=(3,),
        in_specs=[pl.BlockSpec(memory_space=pltpu.SMEM), pl.BlockSpec((1, SWA_BLK, SWA_BLK), lambda p: (p, 0, 0))],
        out_specs=pl.BlockSpec((1, SWA_HEADS, SWA_BLK, SWA_BLK), lambda p: (p, 0, 0, 0)),
        out_shape=SDS((3, SWA_HEADS, SWA_BLK, SWA_BLK), f32),
        compiler_params=_cp(1),
    )(rel_bias, bk)


def _rel_bias_grad(dsb, bk, bk_np):
    present = [sorted(set(int(v) for v in np.unique(bk_np[p]))) for p in range(3)]

    def body(ds_ref, bk_ref, o_ref):
        row = lax.broadcasted_iota(jnp.int32, (NUM_BUCKETS, LANE), 0)
        col = lax.broadcasted_iota(jnp.int32, (NUM_BUCKETS, SWA_HEADS), 1)
        out = jnp.zeros((NUM_BUCKETS, SWA_HEADS), f32)
        for hp in range(4):
            for hh in range(2):
                acc = jnp.zeros((NUM_BUCKETS, LANE), f32)
                for p in range(3):
                    tile = ds_ref[hp, p, hh]
                    b_idx = bk_ref[p]
                    for b in present[p]:
                        part = jnp.sum(jnp.where(b_idx == b, tile, 0.0), axis=0, keepdims=True)
                        acc = acc + jnp.where(row == b, part, 0.0)
                tot = jnp.sum(acc, axis=1, keepdims=True)
                out = out + jnp.where(col == 2 * hp + hh, tot, 0.0)
        o_ref[...] = out

    return pl.pallas_call(body, name="rel_bias_grad", out_shape=SDS((NUM_BUCKETS, SWA_HEADS), f32),
                          compiler_params=_cp(0, 32 * 1024 * 1024))(dsb, bk)


def _proj_fwd(x, g_pre, wcat, after, S):
    TS = 512

    def body(x_ref, g_ref, w_ref, after_ref, o_ref, h_ref):
        xv = x_ref[...]
        r = lax.rsqrt(jnp.mean(xv * xv, axis=-1, keepdims=True) + RMS_EPS)
        h = (xv * r * g_ref[...]).astype(bf16)
        h_ref[...] = h
        o_ref[...] = _dot(h, w_ref[...])

    return pl.pallas_call(
        body, name="proj_fwd", grid=(S // TS,),
        in_specs=[pl.BlockSpec((TS, D_MODEL), lambda i: (i, 0)), _const_spec((1, D_MODEL)),
                  _resident_spec((D_MODEL, NCOL)), _ANY],
        out_specs=[pl.BlockSpec((TS, NCOL), lambda i: (i, 0)), pl.BlockSpec((TS, D_MODEL), lambda i: (i, 0))],
        out_shape=[SDS((S, NCOL), f32), SDS((S, D_MODEL), bf16)],
        compiler_params=_cp(1, VMEM_LIMIT),
    )(x, g_pre, wcat, after)


CONV_RT = 256
HALO = 8


CONV_NC = CONV_RT // CHUNK


def _gdn_prep(proj, conv_w, S):
    def body(p_ref, cw_ref, o_ref, xs_ref):
        t = pl.program_id(0)
        xs_ref[pl.ds(0, HALO), :] = jnp.zeros((HALO, LANE), f32)
        xs_ref[pl.ds(HALO, S), :] = p_ref[...]
        w = cw_ref[...]
        scale = jnp.where(t == 0, GDN_HD ** -0.5, 1.0).astype(f32)

        def run(normed):
            def lp(c, carry):
                st = pl.multiple_of(c * CONV_RT, CONV_RT)
                pre = xs_ref[pl.ds(st + HALO - 3, CONV_RT), :] * w[0:1, :]
                for i in range(1, 4):
                    pre = pre + xs_ref[pl.ds(st + HALO - 3 + i, CONV_RT), :] * w[i:i + 1, :]
                out = pre * _sigmoid(pre)
                if normed:
                    out = out * lax.rsqrt(jnp.sum(out * out, axis=-1, keepdims=True) + 1e-6) * scale
                for i in range(CONV_NC):
                    o_ref[0, c * CONV_NC + i, 0] = out[i * CHUNK:(i + 1) * CHUNK]
                return carry

            lax.fori_loop(0, S // CONV_RT, lp, 0)

        pl.when(t < 2)(functools.partial(run, True))
        pl.when(t == 2)(functools.partial(run, False))

    return pl.pallas_call(
        body, name="gdn_prep", grid=(3, GDN_HEADS),
        in_specs=[pl.BlockSpec((S, LANE), lambda t, h: (0, t * GDN_HEADS + h)),
                  pl.BlockSpec((4, LANE), lambda t, h: (0, t * GDN_HEADS + h))],
        out_specs=pl.BlockSpec((1, S // CHUNK, 1, CHUNK, GDN_HD), lambda t, h: (t, 0, h, 0, 0)),
        out_shape=SDS((3, S // CHUNK, GDN_HEADS, CHUNK, GDN_HD), f32),
        scratch_shapes=[pltpu.VMEM((S + HALO, LANE), f32)],
        compiler_params=_cp(2, VMEM_LIMIT),
    )(proj, conv_w)


def _gdn_prep_bwd(proj, conv_w, dqkv, S):
    def body(p_ref, cw_ref, d_ref, dx_ref, dw_ref, xs_ref, dp_ref):
        t = pl.program_id(0)
        xs_ref[pl.ds(0, HALO), :] = jnp.zeros((HALO, LANE), f32)
        xs_ref[pl.ds(HALO, S), :] = p_ref[...]
        dp_ref[pl.ds(S, HALO), :] = jnp.zeros((HALO, LANE), f32)
        w = cw_ref[...]
        scale = jnp.where(t == 0, GDN_HD ** -0.5, 1.0).astype(f32)

        def first_pass(normed):
            def lp1(c, dw):
                st = pl.multiple_of(c * CONV_RT, CONV_RT)
                taps = [xs_ref[pl.ds(st + HALO - 3 + i, CONV_RT), :] for i in range(4)]
                pre = taps[0] * w[0:1, :]
                for i in range(1, 4):
                    pre = pre + taps[i] * w[i:i + 1, :]
                sg = _sigmoid(pre)
                ds = jnp.concatenate([d_ref[0, c * CONV_NC + i, 0] for i in range(CONV_NC)], axis=0)
                if normed:
                    s = pre * sg
                    rn = lax.rsqrt(jnp.sum(s * s, axis=-1, keepdims=True) + 1e-6)
                    n = s * rn
                    dn = ds * scale
                    ds = rn * (dn - n * jnp.sum(dn * n, axis=-1, keepdims=True))
                dpre = ds * (sg * (1.0 + pre * (1.0 - sg)))
                dp_ref[pl.ds(st, CONV_RT), :] = dpre
                return tuple(dw[i] + jnp.sum(dpre * taps[i], axis=0, keepdims=True) for i in range(4))

            z = jnp.zeros((1, LANE), f32)
            dw = lax.fori_loop(0, S // CONV_RT, lp1, (z, z, z, z))
            for i in range(4):
                dw_ref[pl.ds(i, 1), :] = dw[i]

        pl.when(t < 2)(functools.partial(first_pass, True))
        pl.when(t == 2)(functools.partial(first_pass, False))

        def lp2(c, carry):
            st = pl.multiple_of(c * CONV_RT, CONV_RT)
            dx = dp_ref[pl.ds(st, CONV_RT), :] * w[3:4, :]
            for i in range(3):
                dx = dx + dp_ref[pl.ds(st + 3 - i, CONV_RT), :] * w[i:i + 1, :]
            dx_ref[pl.ds(st, CONV_RT), :] = dx.astype(bf16)
            return carry

        lax.fori_loop(0, S // CONV_RT, lp2, 0)

    col = lambda rows: pl.BlockSpec((rows, LANE), lambda t, h: (0, t * GDN_HEADS + h))
    return pl.pallas_call(
        body, name="gdn_prep_bwd", grid=(3, GDN_HEADS),
        in_specs=[col(S), col(4), pl.BlockSpec((1, S // CHUNK, 1, CHUNK, GDN_HD), lambda t, h: (t, 0, h, 0, 0))],
        out_specs=[col(S), col(4)],
        out_shape=[SDS((S, 3 * GDN_W), bf16), SDS((4, 3 * GDN_W), f32)],
        scratch_shapes=[pltpu.VMEM((S + HALO, LANE), f32), pltpu.VMEM((S + HALO, LANE), f32)],
        compiler_params=_cp(2, VMEM_LIMIT),
    )(proj, conv_w, dqkv)


def _bdot(a, b, prec=None):
    return lax.dot_general(a, b, (((2,), (1,)), ((0,), (0,))), precision=prec, preferred_element_type=f32)


def _bdot_nt(a, b, prec=None):
    return lax.dot_general(a, b, (((2,), (2,)), ((0,), (0,))), precision=prec, preferred_element_type=f32)


def _bdot_tn(a, b, prec=None):
    return lax.dot_general(a, b, (((1,), (1,)), ((0,), (0,))), precision=prec, preferred_element_type=f32)


@jax.custom_vjp
def _tri_inv_saved(a, t):
    return t


def _tri_inv_saved_fwd(a, t):
    return t, t


def _tri_inv_saved_bwd(t, dt):
    return -_bdot_tn(t, _bdot_nt(dt, t, HIGH), HIGH), jnp.zeros_like(t)


_tri_inv_saved.defvjp(_tri_inv_saved_fwd, _tri_inv_saved_bwd)


def _gdn_intra(q, k, v, beta, g, t_saved=None):
    nb = q.shape[0]
    c = CHUNK
    ii = lax.broadcasted_iota(jnp.int32, (c, c), 0)
    jj = lax.broadcasted_iota(jnp.int32, (c, c), 1)
    eye = ii == jj
    tril = ii >= jj
    strict = ii > jj
    ones = jnp.ones((nb, c, c), f32)
    eye_f = eye.astype(f32)

    g_row = _bdot(ones, jnp.where(eye, g, 0.0), HIGH)
    gc = jnp.sum(jnp.where(tril, g_row, 0.0), axis=2, keepdims=True)
    gc_row = _bdot(ones, jnp.where(eye, gc, 0.0), HIGH)
    decay = jnp.where(tril, jnp.exp(jnp.where(tril, gc - gc_row, 0.0)), 0.0)
    last = lax.broadcasted_iota(jnp.int32, (c, 1), 0) == c - 1
    gc_last = jnp.sum(jnp.where(last, gc, 0.0), axis=1, keepdims=True)
    e_gc = jnp.exp(gc)

    kb = k * beta
    k16 = k.astype(bf16)
    a = jnp.where(strict, _bdot_nt(kb.astype(bf16), k16) * decay, 0.0)
    if t_saved is None:
        xp = -a
        t_inv = eye_f + xp
        for level in range(5):
            if level < 2:
                xp = _bdot(xp, xp, HIGH)
                t_inv = t_inv + _bdot(t_inv, xp, HIGH)
            else:
                x16 = xp.astype(bf16)
                xp = _bdot(x16, x16)
                t_inv = t_inv + _bdot(t_inv.astype(bf16), xp.astype(bf16))
    else:
        t_inv = _tri_inv_saved(a, t_saved)
    t16 = t_inv.astype(bf16)
    u = _bdot(t16, (v * beta).astype(bf16))
    w = _bdot(t16, (kb * e_gc).astype(bf16))
    attn = jnp.where(tril, _bdot_nt(q.astype(bf16), k16) * decay, 0.0)
    gam = jnp.broadcast_to(jnp.exp(gc_last), (nb, 1, GDN_HD))
    return u, w, attn, q * e_gc, k * jnp.exp(gc_last - gc), gam, t_inv


GDN_TB = 512
GDN_NC = GDN_TB // CHUNK
GDN_NU = GDN_NC * GDN_HEADS


def _gdn_gates(pg_ref, al_ref, db_ref):
    lane1 = lax.broadcasted_iota(jnp.int32, (1, LANE), 1)
    a_lane = jnp.zeros((1, LANE), f32)
    b_lane = jnp.zeros((1, LANE), f32)
    for h in range(GDN_HEADS):
        a_lane = jnp.where(lane1 == GDN_HEADS + h, al_ref[0, h], a_lane)
        b_lane = jnp.where(lane1 == GDN_HEADS + h, db_ref[0, h], b_lane)
    pg = pg_ref[...]
    z = pg + b_lane
    return _sigmoid(pg), -jnp.exp(a_lane) * _softplus(z), z, a_lane


def _gdn_unit_inputs(qkv_ref, beta_all, g_all):
    units = [(cl, h) for cl in range(GDN_NC) for h in range(GDN_HEADS)]
    beta = jnp.stack([beta_all[cl * CHUNK:(cl + 1) * CHUNK, h:h + 1] for cl, h in units])
    g = jnp.stack([g_all[cl * CHUNK:(cl + 1) * CHUNK, GDN_HEADS + h:GDN_HEADS + h + 1] for cl, h in units])
    return qkv_ref[0], qkv_ref[1], qkv_ref[2], beta, g


def _unit_spec(*tail):
    nd = len(tail)
    return pl.BlockSpec((GDN_NU,) + tail, lambda i: (i,) + (0,) * nd)


def _gdn_intra_shapes(S):
    nu = S // CHUNK * GDN_HEADS
    row = SDS((nu, CHUNK, GDN_HD), f32)
    return [row, row, SDS((nu, CHUNK, CHUNK), f32), row, row, SDS((nu, 1, GDN_HD), f32)]


_GDN_INTRA_SPECS = lambda: [_unit_spec(CHUNK, GDN_HD), _unit_spec(CHUNK, GDN_HD), _unit_spec(CHUNK, CHUNK),
                            _unit_spec(CHUNK, GDN_HD), _unit_spec(CHUNK, GDN_HD), _unit_spec(1, GDN_HD)]


def _gdn_intra_fwd(qkv_u, proj, a_log, dt_bias, S):
    def body(qkv_ref, pg_ref, al_ref, db_ref, *outs):
        beta_all, g_all, _, _ = _gdn_gates(pg_ref, al_ref, db_ref)
        res = _gdn_intra(*_gdn_unit_inputs(qkv_ref, beta_all, g_all))
        for o_ref, r in zip(outs, res):
            o_ref[...] = r

    nu = S // CHUNK * GDN_HEADS
    *intra, t_inv = pl.pallas_call(
        body, name="gdn_intra_fwd", grid=(S // GDN_TB,),
        in_specs=[pl.BlockSpec((3, GDN_NU, CHUNK, GDN_HD), lambda i: (0, i, 0, 0)),
                  pl.BlockSpec((GDN_TB, LANE), lambda i: (i, COL_G // LANE)),
                  pl.BlockSpec(memory_space=pltpu.SMEM), pl.BlockSpec(memory_space=pltpu.SMEM)],
        out_specs=_GDN_INTRA_SPECS() + [_unit_spec(CHUNK, CHUNK)],
        out_shape=_gdn_intra_shapes(S) + [SDS((nu, CHUNK, CHUNK), f32)],
        compiler_params=_cp(1, VMEM_LIMIT),
    )(qkv_u, proj, a_log, dt_bias)
    return intra, t_inv


def _gdn_intra_bwd(qkv_u, proj, a_log, dt_bias, t_inv, cots, S):
    def body(qkv_ref, pg_ref, al_ref, db_ref, t_ref, du_ref, dw_ref, da_ref, dqd_ref, dkd_ref, dgm_ref,
             dqkv_ref, dpg_ref, dal_ref, ddb_ref):
        @pl.when(pl.program_id(0) == 0)
        def _():
            dal_ref[...] = jnp.zeros_like(dal_ref)
            ddb_ref[...] = jnp.zeros_like(ddb_ref)

        t_saved = t_ref[...]
        beta_all, g_all, z, a_lane = _gdn_gates(pg_ref, al_ref, db_ref)
        _, vjp = jax.vjp(lambda *a: _gdn_intra(*a, t_saved=t_saved)[:6], *_gdn_unit_inputs(qkv_ref, beta_all, g_all))
        dq, dk, dv, dbeta, dg = vjp((du_ref[...], dw_ref[...], da_ref[...], dqd_ref[...], dkd_ref[...], dgm_ref[...]))
        dqkv_ref[0] = dq
        dqkv_ref[1] = dk
        dqkv_ref[2] = dv
        lane = lax.broadcasted_iota(jnp.int32, (CHUNK, LANE), 1)
        rows = []
        for cl in range(GDN_NC):
            t = jnp.zeros((CHUNK, LANE), f32)
            for h in range(GDN_HEADS):
                b = cl * GDN_HEADS + h
                t = t + jnp.where(lane == h, dbeta[b], 0.0) + jnp.where(lane == GDN_HEADS + h, dg[b], 0.0)
            rows.append(t)
        d_all = jnp.concatenate(rows, axis=0)
        is_beta = lax.broadcasted_iota(jnp.int32, (GDN_TB, LANE), 1) < GDN_HEADS
        dz = d_all * (-jnp.exp(a_lane)) * _sigmoid(z)
        dpg_ref[...] = jnp.where(is_beta, d_all * beta_all * (1.0 - beta_all), dz).astype(bf16)
        dal_ref[...] += jnp.sum(jnp.where(is_beta, 0.0, d_all * g_all), axis=0, keepdims=True)
        ddb_ref[...] += jnp.sum(jnp.where(is_beta, 0.0, dz), axis=0, keepdims=True)

    acc = _const_spec((1, LANE))
    nu = S // CHUNK * GDN_HEADS
    return pl.pallas_call(
        body, name="gdn_intra_bwd", grid=(S // GDN_TB,),
        in_specs=[pl.BlockSpec((3, GDN_NU, CHUNK, GDN_HD), lambda i: (0, i, 0, 0)),
                  pl.BlockSpec((GDN_TB, LANE), lambda i: (i, COL_G // LANE)),
                  pl.BlockSpec(memory_space=pltpu.SMEM), pl.BlockSpec(memory_space=pltpu.SMEM),
                  _unit_spec(CHUNK, CHUNK)] + _GDN_INTRA_SPECS(),
        out_specs=[pl.BlockSpec((3, GDN_NU, CHUNK, GDN_HD), lambda i: (0, i, 0, 0)),
                   pl.BlockSpec((GDN_TB, LANE), lambda i: (i, 0)), acc, acc],
        out_shape=[SDS((3, nu, CHUNK, GDN_HD), f32), SDS((S, LANE), bf16), SDS((1, LANE), f32), SDS((1, LANE), f32)],
        compiler_params=_cp(1, VMEM_LIMIT),
    )(qkv_u, proj, a_log, dt_bias, t_inv, *cots)


def _gdn_scan_fwd(intra, proj, onorm_g, S):
    def body(u_ref, w_ref, at_ref, qd_ref, kd_ref, gm_ref, gate_ref, og_ref, out_ref, st_ref, s_scr):
        @pl.when(pl.program_id(0) == 0)
        def _():
            s_scr[...] = jnp.zeros_like(s_scr)

        og = og_ref[...]
        s = s_scr[...]
        def out_mm(us, s16, vn16):
            return _bdot(qd_ref[us].astype(bf16), s16) + _bdot(at_ref[us].astype(bf16), vn16)

        outs, prev = [], None
        for cl in range(GDN_NC):
            us = slice(cl * GDN_HEADS, (cl + 1) * GDN_HEADS)
            st_ref[us] = s
            s16 = s.astype(bf16)
            ws = _bdot(w_ref[us].astype(bf16), s16)
            if prev is not None:
                outs.append(out_mm(*prev))
            vn16 = (u_ref[us] - ws).astype(bf16)
            prev = (us, s16, vn16)
            s = s * gm_ref[us] + _bdot_tn(kd_ref[us].astype(bf16), vn16)
        outs.append(out_mm(*prev))
        s_scr[...] = s
        for cl, o in enumerate(outs):
            rows = slice(cl * CHUNK, (cl + 1) * CHUNK)
            for h in range(GDN_HEADS):
                oh = o[h]
                gt = gate_ref[rows, h * GDN_HD:(h + 1) * GDN_HD]
                on = oh * lax.rsqrt(jnp.mean(oh * oh, axis=-1, keepdims=True) + RMS_EPS) * og
                out_ref[rows, h * GDN_HD:(h + 1) * GDN_HD] = on * (gt * _sigmoid(gt))

    nu = S // CHUNK * GDN_HEADS
    return pl.pallas_call(
        body, name="gdn_scan_fwd", grid=(S // GDN_TB,),
        in_specs=_GDN_INTRA_SPECS() + [pl.BlockSpec((GDN_TB, GDN_W), lambda i: (i, 3)), _const_spec((1, GDN_HD))],
        out_specs=[pl.BlockSpec((GDN_TB, GDN_W), lambda i: (i, 0)), _unit_spec(GDN_HD, GDN_HD)],
        out_shape=[SDS((S, GDN_W), f32), SDS((nu, GDN_HD, GDN_HD), f32)],
        scratch_shapes=[pltpu.VMEM((GDN_HEADS, GDN_HD, GDN_HD), f32)],
        compiler_params=_cp(1, VMEM_LIMIT),
    )(*intra, proj, onorm_g)


def _gdn_scan_bwd(intra, states, proj, d_oab, onorm_g, after, S):
    n_steps = S // GDN_TB

    def body(u_ref, w_ref, at_ref, qd_ref, kd_ref, gm_ref, st_ref, gate_ref, do_ref, og_ref, after_ref,
             du_ref, dw_ref, dat_ref, dqd_ref, dkd_ref, dgm_ref, dgate_ref, dog_ref, ds_scr):
        @pl.when(pl.program_id(0) == 0)
        def _():
            ds_scr[...] = jnp.zeros_like(ds_scr)
            dog_ref[...] = jnp.zeros_like(dog_ref)

        og = og_ref[...]
        ii = lax.broadcasted_iota(jnp.int32, (CHUNK, CHUNK), 0)
        jj = lax.broadcasted_iota(jnp.int32, (CHUNK, CHUNK), 1)
        tril = ii >= jj
        dog = jnp.zeros((1, GDN_HD), f32)
        pre = []
        for cl in range(GDN_NC):
            us = slice(cl * GDN_HEADS, (cl + 1) * GDN_HEADS)
            rows = slice(cl * CHUNK, (cl + 1) * CHUNK)
            s016 = st_ref[us].astype(bf16)
            w16 = w_ref[us].astype(bf16)
            qd16 = qd_ref[us].astype(bf16)
            at16 = at_ref[us].astype(bf16)
            vn16 = (u_ref[us] - _bdot(w16, s016)).astype(bf16)
            o = _bdot(qd16, s016) + _bdot(at16, vn16)
            do_h = []
            for h in range(GDN_HEADS):
                oh = o[h]
                lanes = slice(h * GDN_HD, (h + 1) * GDN_HD)
                gt = gate_ref[rows, lanes]
                d_out = do_ref[rows, lanes]
                r = lax.rsqrt(jnp.mean(oh * oh, axis=-1, keepdims=True) + RMS_EPS)
                n = oh * r
                sg = _sigmoid(gt)
                silu = gt * sg
                dog = dog + jnp.sum(d_out * n * silu, axis=0, keepdims=True)
                dgate_ref[rows, lanes] = (d_out * n * og * (sg * (1.0 + gt * (1.0 - sg)))).astype(bf16)
                dn = d_out * og * silu
                do_h.append(r * (dn - n * jnp.mean(dn * n, axis=-1, keepdims=True)))
            do16 = jnp.stack(do_h).astype(bf16)
            pre.append((us, s016, w16, vn16, do16, _bdot_tn(at16, do16), _bdot_tn(qd16, do16)))
        ds = ds_scr[...]
        chain = [None] * GDN_NC
        for cl in reversed(range(GDN_NC)):
            us, s016, w16, vn16, do16, at_do, qd_do = pre[cl]
            ds16 = ds.astype(bf16)
            dvn = at_do + _bdot(kd_ref[us].astype(bf16), ds16)
            dvn16 = dvn.astype(bf16)
            chain[cl] = (ds, ds16, dvn, dvn16)
            ds = qd_do + ds * gm_ref[us] - _bdot_tn(w16, dvn16)
        ds_scr[...] = ds
        for cl in range(GDN_NC):
            us, s016, w16, vn16, do16, _, _ = pre[cl]
            ds_in, ds16, dvn, dvn16 = chain[cl]
            du_ref[us] = dvn
            dw_ref[us] = -_bdot_nt(dvn16, s016)
            dat_ref[us] = jnp.where(tril, _bdot_nt(do16, vn16), 0.0)
            dqd_ref[us] = _bdot_nt(do16, s016)
            dkd_ref[us] = _bdot_nt(vn16, ds16)
            dgm_ref[us] = jnp.sum(st_ref[us] * ds_in, axis=1, keepdims=True)
        dog_ref[...] += dog

    def unit(*tail):
        nd = len(tail)
        return pl.BlockSpec((GDN_NU,) + tail, lambda i: (n_steps - 1 - i,) + (0,) * nd)

    intra_specs = [unit(CHUNK, GDN_HD), unit(CHUNK, GDN_HD), unit(CHUNK, CHUNK), unit(CHUNK, GDN_HD),
                   unit(CHUNK, GDN_HD), unit(1, GDN_HD)]
    tok = lambda c: pl.BlockSpec((GDN_TB, GDN_W), lambda i: (n_steps - 1 - i, c))
    return pl.pallas_call(
        body, name="gdn_scan_bwd", grid=(n_steps,),
        in_specs=intra_specs + [unit(GDN_HD, GDN_HD), tok(3), tok(0), _const_spec((1, GDN_HD)), _ANY],
        out_specs=intra_specs + [tok(0), _const_spec((1, GDN_HD))],
        out_shape=_gdn_intra_shapes(S) + [SDS((S, GDN_W), bf16), SDS((1, GDN_HD), f32)],
        scratch_shapes=[pltpu.VMEM((GDN_HEADS, GDN_HD, GDN_HD), f32)],
        compiler_params=_cp(1, VMEM_LIMIT),
    )(*intra, states, proj, d_oab, onorm_g, after)


SWA_UNROLL = 8


def _swa_tiles(it, d, nb_log2):
    nb = 1 << nb_log2
    r = lax.shift_right_logical(it, nb_log2)
    blk = lax.bitwise_and(it, nb - 1)
    qs = blk * (SWA_BLK * d) + r
    ps = jnp.maximum(blk - 1, 0) * (SWA_BLK * d) + r
    if d > 1:
        rows_c, rows_p = pl.ds(qs, SWA_BLK, stride=d), pl.ds(ps, SWA_BLK, stride=d)
    else:
        rows_c, rows_p = pl.ds(pl.multiple_of(qs, SWA_BLK), SWA_BLK), pl.ds(pl.multiple_of(ps, SWA_BLK), SWA_BLK)
    return rows_c, rows_p, blk > 0


def _swa_prev_modes(nb):
    if nb >= SWA_UNROLL:
        return ["load"] + ["reuse"] * (SWA_UNROLL - 1)
    return ["none" if u % nb == 0 else "reuse" for u in range(SWA_UNROLL)]


def _swa_fwd(proj, bt, S):
    scale = SWA_HD ** -0.5

    def body(q_ref, k_ref, v_ref, bt_ref, o_ref, lse0_ref, lse1_ref, m0_scr, m1_scr, a0_scr, a1_scr):
        lane = lax.broadcasted_iota(jnp.int32, (SWA_BLK, LANE), 1)
        h0 = lane < SWA_HD
        qi = lax.broadcasted_iota(jnp.int32, (SWA_BLK, SWA_BLK), 0)
        kj = lax.broadcasted_iota(jnp.int32, (SWA_BLK, SWA_BLK), 1)
        lower = kj <= qi
        ones16 = jnp.ones((LANE, SWA_BLK), bf16)
        m_scrs = (m0_scr, m1_scr)
        a_scrs = (a0_scr, a1_scr)
        for p, (_, d) in reversed(list(enumerate(PATTERNS))):
            nb_log2 = int(math.log2(S // d // SWA_BLK))
            first = p == len(PATTERNS) - 1

            def lp(i, carry, p=p, d=d, nb_log2=nb_log2, first=first):
                heads = [h0, jnp.logical_not(h0)]
                modes = _swa_prev_modes(1 << nb_log2)
                tiles = []
                kc_f = None
                for u in range(SWA_UNROLL):
                    rows_c, rows_p, has_prev = _swa_tiles(i * SWA_UNROLL + u, d, nb_log2)
                    kp_f = {"load": lambda: k_ref[rows_p, :], "reuse": lambda: kc_f, "none": lambda: None}[modes[u]]()
                    has_prev = {"load": has_prev, "reuse": True, "none": False}[modes[u]]
                    q = q_ref[rows_c, :]
                    kc_f = k_ref[rows_c, :]
                    kc = kc_f.astype(bf16)
                    logits = []
                    for mh in heads:
                        q_h = jnp.where(mh, q, 0.0)
                        qh = q_h.astype(bf16)
                        if kp_f is None:
                            logits.append((_dot_nt(qh, kc), None, None))
                        else:
                            logits.append((_dot_nt(qh, kc), _dot_nt(qh, kp_f.astype(bf16)), _dot((q_h * kp_f).astype(bf16), ones16)))
                    tiles.append((rows_c, rows_p, has_prev, logits))
                probs = []
                for rows_c, rows_p, has_prev, logits in tiles:
                    per_head = []
                    for h, (s_c, s_p, far) in enumerate(logits):
                        if has_prev is False:
                            s = jnp.where(lower, s_c * scale + bt_ref[p, h], NEG)
                            s_far = None
                        else:
                            s = jnp.where(lower, s_c, s_p) * scale + bt_ref[p, h]
                            s_far = far * scale + bt_ref[p, h, SWA_BLK - 1:SWA_BLK, 0:1]
                            if has_prev is not True:
                                s = jnp.where(jnp.logical_or(lower, has_prev), s, NEG)
                                s_far = jnp.where(has_prev, s_far, NEG)
                        mn = jnp.max(s, axis=1, keepdims=True)
                        if s_far is not None:
                            mn = jnp.maximum(s_far, mn)
                        alpha = None
                        if not first:
                            mo = m_scrs[h][rows_c, :]
                            mn = jnp.maximum(mo, mn)
                            alpha = jnp.exp(mo - mn)
                        mn = jnp.broadcast_to(mn, (SWA_BLK, LANE))
                        pm = jnp.exp(s - mn)
                        per_head.append((mn, alpha, None if s_far is None else jnp.exp(s_far - mn),
                                         jnp.where(lower, pm, 0.0).astype(bf16),
                                         None if s_far is None else jnp.where(lower, 0.0, pm).astype(bf16)))
                    probs.append(per_head)
                acc_old = [None if first else (a0_scr[t[0], :], a1_scr[t[0], :]) for t in tiles]
                done = []
                vc = None
                for u, ((rows_c, rows_p, _, _), per_head, old) in enumerate(zip(tiles, probs, acc_old)):
                    vp = {"load": lambda: v_ref[rows_p, :], "reuse": lambda: vc, "none": lambda: None}[modes[u]]()
                    vc = v_ref[rows_c, :]
                    acc_new = []
                    for h, (mn, alpha, p_far, pc16, pp16) in enumerate(per_head):
                        pv = _dot(pc16, jnp.where(heads[h], vc, 1.0).astype(bf16))
                        if pp16 is not None:
                            vpa = jnp.where(heads[h], vp, 1.0)
                            pv = pv + _dot(pp16, vpa.astype(bf16)) + p_far * vpa
                        acc_new.append(pv if first else alpha * old[h] + pv)
                    done.append((rows_c, per_head[0][0], per_head[1][0], acc_new[0], acc_new[1]))
                for rows_c, m0_new, m1_new, a0_new, a1_new in done:
                    m0_scr[rows_c, :] = m0_new
                    m1_scr[rows_c, :] = m1_new
                    a0_scr[rows_c, :] = a0_new
                    a1_scr[rows_c, :] = a1_new
                return carry

            lax.fori_loop(0, S // SWA_BLK // SWA_UNROLL, lp, 0)

        def fin(c, carry):
            rows = pl.ds(pl.multiple_of(c * SWA_BLK, SWA_BLK), SWA_BLK)
            a0 = a0_scr[rows, :]
            a1 = a1_scr[rows, :]
            l0 = jnp.where(h0, pltpu.roll(a0, SWA_HD, 1), a0)
            l1 = jnp.where(h0, a1, pltpu.roll(a1, SWA_HD, 1))
            o_ref[rows, :] = jnp.where(h0, a0 / l0, a1 / l1)
            lse0_ref[rows, :] = m0_scr[rows, :] + jnp.log(l0)
            lse1_ref[rows, :] = m1_scr[rows, :] + jnp.log(l1)
            return carry

        lax.fori_loop(0, S // SWA_BLK, fin, 0)

    qb = COL_B // LANE
    col = lambda c: pl.BlockSpec((S, LANE), lambda hp, c=c: (0, c + hp))
    return pl.pallas_call(
        body, name="swa_fwd", grid=(4,),
        in_specs=[col(qb), col(qb + 4), col(qb + 8), pl.BlockSpec((3, 2, SWA_BLK, SWA_BLK), lambda hp: (0, hp, 0, 0))],
        out_specs=[col(0), col(0), col(0)],
        out_shape=[SDS((S, SWA_W), f32)] * 3,
        scratch_shapes=[pltpu.VMEM((S, LANE), f32)] * 4,
        compiler_params=_cp(1, VMEM_LIMIT),
    )(proj, proj, proj, bt)


def _swa_bwd(proj, bt, nd, lse0, lse1, d_oab, after, S):
    scale = SWA_HD ** -0.5

    def body(q_ref, k_ref, v_ref, bt_ref, nd_scr, lse0_ref, lse1_ref, do_ref, after_ref, dq_ref, dk_ref, dv_ref, dsb_ref,
             dq_scr, dk_scr, dv_scr):
        lane = lax.broadcasted_iota(jnp.int32, (SWA_BLK, LANE), 1)
        h0 = lane < SWA_HD
        qi = lax.broadcasted_iota(jnp.int32, (SWA_BLK, SWA_BLK), 0)
        kj = lax.broadcasted_iota(jnp.int32, (SWA_BLK, SWA_BLK), 1)
        lower = kj <= qi
        eye = kj == qi
        rel127 = jnp.logical_or(kj == qi + 1, jnp.logical_and(qi == SWA_BLK - 1, kj == 0))
        ones16 = jnp.ones((LANE, SWA_BLK), bf16)
        lse_refs = (lse0_ref, lse1_ref)
        dk_scr[...] = jnp.zeros((S, LANE), f32)
        dv_scr[...] = jnp.zeros((S, LANE), f32)
        dsb_ref[...] = jnp.zeros_like(dsb_ref)

        for p, (_, d) in reversed(list(enumerate(PATTERNS))):
            nb_log2 = int(math.log2(S // d // SWA_BLK))
            first = p == len(PATTERNS) - 1

            def lp(i, carry, p=p, d=d, nb_log2=nb_log2, first=first):
                heads = [h0, jnp.logical_not(h0)]
                modes = _swa_prev_modes(1 << nb_log2)
                tiles = []
                kc_f = vc_f = None
                for u in range(SWA_UNROLL):
                    rows_c, rows_p, has_prev = _swa_tiles(i * SWA_UNROLL + u, d, nb_log2)
                    kp_f = {"load": lambda: k_ref[rows_p, :], "reuse": lambda: kc_f, "none": lambda: None}[modes[u]]()
                    vp_f = {"load": lambda: v_ref[rows_p, :], "reuse": lambda: vc_f, "none": lambda: None}[modes[u]]()
                    has_prev = {"load": has_prev, "reuse": True, "none": False}[modes[u]]
                    q = q_ref[rows_c, :]
                    kc_f = k_ref[rows_c, :]
                    vc_f = v_ref[rows_c, :]
                    kc = kc_f.astype(bf16)
                    kp = None if kp_f is None else kp_f.astype(bf16)
                    do = do_ref[rows_c, :]
                    nd = nd_scr[rows_c, :]
                    per_head = []
                    for mh in heads:
                        q_h = jnp.where(mh, q, 0.0)
                        do_a = jnp.where(mh, do, nd)
                        qh = q_h.astype(bf16)
                        doa = do_a.astype(bf16)
                        doh = jnp.where(mh, do, 0.0).astype(bf16)
                        dd_c = _dot_nt(doa, jnp.where(mh, vc_f, 1.0).astype(bf16))
                        if kp_f is None:
                            per_head.append((qh, doh, _dot_nt(qh, kc), None, None, dd_c, None, None))
                        else:
                            vpa = jnp.where(mh, vp_f, 1.0)
                            per_head.append((qh, doh, _dot_nt(qh, kc), _dot_nt(qh, kp), _dot((q_h * kp_f).astype(bf16), ones16),
                                             dd_c, _dot_nt(doa, vpa.astype(bf16)), _dot((do_a * vpa).astype(bf16), ones16)))
                    tiles.append((rows_c, rows_p, has_prev, kc, kp, per_head))
                grads = []
                for rows_c, rows_p, has_prev, kc, kp, per_head in tiles:
                    out = []
                    for h, (qh, doh, s_c, s_p, far, dd_c, dd_p, dd_far) in enumerate(per_head):
                        lse_h = lse_refs[h][rows_c, :]
                        if has_prev is False:
                            pm = jnp.exp(jnp.where(lower, s_c * scale + bt_ref[p, h], NEG) - lse_h)
                            dsm = pm * dd_c
                            out.append((dsm, dsm.astype(bf16), None, pm.astype(bf16), None))
                            continue
                        s = jnp.where(lower, s_c, s_p) * scale + bt_ref[p, h]
                        s_far = far * scale + bt_ref[p, h, SWA_BLK - 1:SWA_BLK, 0:1]
                        if has_prev is not True:
                            s = jnp.where(jnp.logical_or(lower, has_prev), s, NEG)
                            s_far = jnp.where(has_prev, s_far, NEG)
                        pm = jnp.exp(s - lse_h)
                        p_far = jnp.exp(s_far - lse_h)
                        dsm = pm * jnp.where(lower, dd_c, dd_p)
                        ds_far = p_far * dd_far
                        out.append((dsm + jnp.where(rel127, ds_far, 0.0),
                                    jnp.where(lower, dsm, 0.0).astype(bf16),
                                    jnp.where(lower, jnp.where(eye, ds_far, 0.0), dsm).astype(bf16),
                                    jnp.where(lower, pm, 0.0).astype(bf16),
                                    jnp.where(lower, jnp.where(eye, p_far, 0.0), pm).astype(bf16)))
                    grads.append(out)
                done = []
                add = lambda acc, t: t if acc is None else acc + t
                for (rows_c, rows_p, _, kc, kp, per_head), out in zip(tiles, grads):
                    dq_t = dkc_t = dkp_t = dvc_t = dvp_t = None
                    for h, (_, dsc16, dsp16, pc16, pp16) in enumerate(out):
                        qh, doh = per_head[h][0], per_head[h][1]
                        dq_h = _dot(dsc16, kc)
                        dkc_t = add(dkc_t, _dot_tn(dsc16, qh) * scale)
                        dvc_t = add(dvc_t, _dot_tn(pc16, doh))
                        if dsp16 is not None:
                            dq_h = dq_h + _dot(dsp16, kp)
                            dkp_t = add(dkp_t, _dot_tn(dsp16, qh) * scale)
                            dvp_t = add(dvp_t, _dot_tn(pp16, doh))
                        dq_t = add(dq_t, jnp.where(heads[h], dq_h * scale, 0.0))
                    done.append([rows_c, rows_p, dq_t, dkc_t, dkp_t, dvc_t, dvp_t])
                for u in range(1, SWA_UNROLL):
                    if modes[u] == "reuse":
                        done[u - 1][3] = done[u - 1][3] + done[u][4]
                        done[u - 1][5] = done[u - 1][5] + done[u][6]
                for h in range(2):
                    tot = grads[0][h][0]
                    for g in grads[1:]:
                        tot = tot + g[h][0]
                    dsb_ref[0, p, h] += tot
                for u, (rows_c, rows_p, dq_t, dkc_t, dkp_t, dvc_t, dvp_t) in enumerate(done):
                    dq_scr[rows_c, :] = dq_t if first else dq_scr[rows_c, :] + dq_t
                    dk_scr[rows_c, :] = dk_scr[rows_c, :] + dkc_t
                    dv_scr[rows_c, :] = dv_scr[rows_c, :] + dvc_t
                    if modes[u] == "load":
                        dk_scr[rows_p, :] = dk_scr[rows_p, :] + dkp_t
                        dv_scr[rows_p, :] = dv_scr[rows_p, :] + dvp_t
                return carry

            lax.fori_loop(0, S // SWA_BLK // SWA_UNROLL, lp, 0)
        dq_ref[...] = dq_scr[...].astype(bf16)
        dk_ref[...] = dk_scr[...].astype(bf16)
        dv_ref[...] = dv_scr[...].astype(bf16)

    qb = COL_B // LANE
    col = lambda c: pl.BlockSpec((S, LANE), lambda hp, c=c: (0, c + hp))
    return pl.pallas_call(
        body, name="swa_bwd", grid=(4,),
        in_specs=[col(qb), col(qb + 4), col(qb + 8),
                  pl.BlockSpec((3, 2, SWA_BLK, SWA_BLK), lambda hp: (0, hp, 0, 0)),
                  col(0), col(0), col(0), col(4), _ANY],
        out_specs=[col(0), col(0), col(0),
                   pl.BlockSpec((1, 3, 2, SWA_BLK, SWA_BLK), lambda hp: (hp, 0, 0, 0, 0))],
        out_shape=[SDS((S, SWA_W), bf16)] * 3 + [SDS((4, 3, 2, SWA_BLK, SWA_BLK), f32)],
        scratch_shapes=[pltpu.VMEM((S, LANE), f32)] * 3,
        compiler_params=_cp(1, VMEM_LIMIT),
    )(proj, proj, proj, bt, nd, lse0, lse1, d_oab, after)


def _mix_fwd(oa, ob, w_out, x, g_post, S):
    TS = 512

    def body(oa_ref, ob_ref, w_ref, x_ref, g_ref, mix_ref, x1_ref):
        mix = _dot(oa_ref[...].astype(bf16), w_ref[0:GDN_W, :]) + _dot(ob_ref[...].astype(bf16), w_ref[GDN_W:D_MODEL, :])
        r = lax.rsqrt(jnp.mean(mix * mix, axis=-1, keepdims=True) + RMS_EPS)
        mix_ref[...] = mix
        x1_ref[...] = x_ref[...] + mix * r * g_ref[...]

    row = lambda w: pl.BlockSpec((TS, w), lambda i: (i, 0))
    return pl.pallas_call(
        body, name="mix_fwd", grid=(S // TS,),
        in_specs=[row(GDN_W), row(SWA_W), _resident_spec((D_MODEL, D_MODEL)), row(D_MODEL), _const_spec((1, D_MODEL))],
        out_specs=[row(D_MODEL), row(D_MODEL)],
        out_shape=[SDS((S, D_MODEL), f32), SDS((S, D_MODEL), f32)],
        compiler_params=_cp(1, VMEM_LIMIT),
    )(oa, ob, w_out, x, g_post)


def _mix_bwd(dx1, mix, g_post, w_out, ob, S):
    TS = 512

    def body(dx1_ref, mix_ref, g_ref, w_ref, ob_ref, dmix_ref, doab_ref, dg_ref, nd_ref):
        @pl.when(pl.program_id(0) == 0)
        def _():
            dg_ref[...] = jnp.zeros_like(dg_ref)

        mix = mix_ref[...]
        dz = dx1_ref[...]
        r = lax.rsqrt(jnp.mean(mix * mix, axis=-1, keepdims=True) + RMS_EPS)
        n = mix * r
        dg_ref[...] += jnp.sum(dz * n, axis=0, keepdims=True)
        dn = dz * g_ref[...]
        dmix = (r * (dn - n * jnp.mean(dn * n, axis=-1, keepdims=True))).astype(bf16)
        dmix_ref[...] = dmix
        doab = _dot_nt(dmix, w_ref[...])
        doab_ref[...] = doab
        hi_ = lax.shift_right_logical(lax.broadcasted_iota(jnp.int32, (SWA_W, SWA_W), 0), 6)
        hj_ = lax.shift_right_logical(lax.broadcasted_iota(jnp.int32, (SWA_W, SWA_W), 1), 6)
        swap = (hi_ == lax.bitwise_xor(hj_, 1)).astype(bf16)
        dlt = doab[:, GDN_W:] * ob_ref[...]
        hi = dlt.astype(bf16)
        nd_ref[...] = (_dot(hi, swap) + _dot((dlt - hi.astype(f32)).astype(bf16), swap)) * (-1.0 / SWA_HD)

    row = lambda w=D_MODEL: pl.BlockSpec((TS, w), lambda i: (i, 0))
    return pl.pallas_call(
        body, name="mix_bwd", grid=(S // TS,),
        in_specs=[row(), row(), _const_spec((1, D_MODEL)), _resident_spec((D_MODEL, D_MODEL)), row(SWA_W)],
        out_specs=[row(), row(), _const_spec((1, D_MODEL)), row(SWA_W)],
        out_shape=[SDS((S, D_MODEL), bf16), SDS((S, D_MODEL), f32), SDS((1, D_MODEL), f32), SDS((S, SWA_W), f32)],
        compiler_params=_cp(1, VMEM_LIMIT),
    )(dx1, mix, g_post, w_out, ob)


FFN_TS = 256
FFN_CH = 1408


def _ffn(x1, tgt, g_pre, g_post, wg, wu, wd, S):
    def body(x1_ref, t_ref, gp_ref, gq_ref, wg_ref, wu_ref, wd_ref,
             dx1_ref, h2_ref, act_ref, dgate_ref, dup_ref, df_ref, loss_ref, dgp_ref, dgq_ref, gate_scr, up_scr):
        @pl.when(pl.program_id(0) == 0)
        def _():
            loss_ref[...] = jnp.zeros_like(loss_ref)
            dgp_ref[...] = jnp.zeros_like(dgp_ref)
            dgq_ref[...] = jnp.zeros_like(dgq_ref)

        x1v = x1_ref[...]
        gp = gp_ref[...]
        gq = gq_ref[...]
        r2 = lax.rsqrt(jnp.mean(x1v * x1v, axis=-1, keepdims=True) + RMS_EPS)
        n2 = x1v * r2
        h2 = (n2 * gp).astype(bf16)
        h2_ref[...] = h2
        chunks = [slice(c * FFN_CH, (c + 1) * FFN_CH) for c in range(D_FF // FFN_CH)]
        for cs in chunks:
            gate_scr[:, cs] = _dot_nt(h2, wg_ref[cs, :])
            up_scr[:, cs] = _dot_nt(h2, wu_ref[cs, :])
        acts = []
        for cs in chunks:
            gate = gate_scr[:, cs]
            act = (gate * _sigmoid(gate) * up_scr[:, cs]).astype(bf16)
            act_ref[:, cs] = act
            acts.append(act)
        f = _dot(acts[0], wd_ref[chunks[0], :])
        for act, cs in zip(acts[1:], chunks[1:]):
            f = f + _dot(act, wd_ref[cs, :])
        r3 = lax.rsqrt(jnp.mean(f * f, axis=-1, keepdims=True) + RMS_EPS)
        n3 = f * r3
        err = x1v + n3 * gq - t_ref[...]
        loss_ref[...] += 0.5 * jnp.sum(jnp.mean(err * err, axis=-1, keepdims=True), axis=0, keepdims=True)
        dy = err * (1.0 / D_MODEL)
        dgq_ref[...] += jnp.sum(dy * n3, axis=0, keepdims=True)
        dn3 = dy * gq
        df = (r3 * (dn3 - n3 * jnp.mean(dn3 * n3, axis=-1, keepdims=True))).astype(bf16)
        df_ref[...] = df
        dacts = [_dot_nt(df, wd_ref[cs, :]) for cs in chunks]
        dgs = []
        for dact, cs in zip(dacts, chunks):
            gate = gate_scr[:, cs]
            sg = _sigmoid(gate)
            dup = (dact * gate * sg).astype(bf16)
            dgate = (dact * up_scr[:, cs] * (sg * (1.0 + gate * (1.0 - sg)))).astype(bf16)
            dup_ref[:, cs] = dup
            dgate_ref[:, cs] = dgate
            dgs.append((dgate, dup))
        dh2 = None
        for (dgate, dup), cs in zip(dgs, chunks):
            t = _dot(dgate, wg_ref[cs, :]) + _dot(dup, wu_ref[cs, :])
            dh2 = t if dh2 is None else dh2 + t
        dgp_ref[...] += jnp.sum(dh2 * n2, axis=0, keepdims=True)
        dn2 = dh2 * gp
        dx1_ref[...] = dy + r2 * (dn2 - n2 * jnp.mean(dn2 * n2, axis=-1, keepdims=True))

    row = lambda w: pl.BlockSpec((FFN_TS, w), lambda i: (i, 0))
    vec = _const_spec((1, D_MODEL))
    return pl.pallas_call(
        body, name="ffn_fwd_bwd", grid=(S // FFN_TS,),
        in_specs=[row(D_MODEL), row(D_MODEL), vec, vec, _resident_spec((D_FF, D_MODEL)), _resident_spec((D_FF, D_MODEL)),
                  _resident_spec((D_FF, D_MODEL))],
        out_specs=[row(D_MODEL), row(D_MODEL), row(D_FF), row(D_FF), row(D_FF), row(D_MODEL), _const_spec((1, LANE)), vec, vec],
        out_shape=[SDS((S, D_MODEL), f32), SDS((S, D_MODEL), bf16), SDS((S, D_FF), bf16), SDS((S, D_FF), bf16),
                   SDS((S, D_FF), bf16), SDS((S, D_MODEL), bf16), SDS((1, LANE), f32), SDS((1, D_MODEL), f32),
                   SDS((1, D_MODEL), f32)],
        scratch_shapes=[pltpu.VMEM((FFN_TS, D_FF), f32), pltpu.VMEM((FFN_TS, D_FF), f32)],
        compiler_params=_cp(1, VMEM_LIMIT),
    )(x1, tgt, g_pre, g_post, wg, wu, wd)


def _proj_bwd(x, dx1, g_pre, wcat, segs, after, S):
    TS = 512
    n = len(segs)
    cols = [(c0, a.shape[1]) for a, c0 in segs]

    def body(*refs):
        x_ref, dx1_ref, g_ref, w_ref = refs[:4]
        seg_refs = refs[4:4 + n]
        gx_ref, dg_ref = refs[5 + n:]

        @pl.when(pl.program_id(0) == 0)
        def _():
            dg_ref[...] = jnp.zeros_like(dg_ref)

        dh = jnp.zeros((TS, D_MODEL), f32)
        for s_ref, (c0, w) in zip(seg_refs, cols):
            dh = dh + _dot_nt(s_ref[...], w_ref[:, c0:c0 + w])
        xv = x_ref[...]
        g = g_ref[...]
        r = lax.rsqrt(jnp.mean(xv * xv, axis=-1, keepdims=True) + RMS_EPS)
        nx = xv * r
        dg_ref[...] += jnp.sum(dh * nx, axis=0, keepdims=True)
        dn = dh * g
        gx_ref[...] = dx1_ref[...] + r * (dn - nx * jnp.mean(dn * nx, axis=-1, keepdims=True))

    row = lambda w: pl.BlockSpec((TS, w), lambda i: (i, 0))
    return pl.pallas_call(
        body, name="proj_bwd", grid=(S // TS,),
        in_specs=[row(D_MODEL), row(D_MODEL), _const_spec((1, D_MODEL)), _resident_spec((D_MODEL, NCOL))]
                 + [row(w) for _, w in cols] + [_ANY],
        out_specs=[row(D_MODEL), _const_spec((1, D_MODEL))],
        out_shape=[SDS((S, D_MODEL), f32), SDS((1, D_MODEL), f32)],
        compiler_params=_cp(1, VMEM_LIMIT),
    )(x, dx1, g_pre, wcat, *[a for a, _ in segs], after)


def _wgrad(a, b, S, name):
    TS = 1024
    K = a.shape[1]
    N = b.shape[1]
    TN = next(t for t in (512, 1408, N) if N % t == 0)

    def body(a_ref, b_ref, o_ref, acc):
        @pl.when(pl.program_id(1) == 0)
        def _():
            acc[...] = jnp.zeros_like(acc)

        acc[...] += _dot_tn(a_ref[...].astype(bf16), b_ref[...])

        @pl.when(pl.program_id(1) == pl.num_programs(1) - 1)
        def _():
            o_ref[...] = acc[...].astype(bf16)

    return pl.pallas_call(
        body, name=name, grid=(N // TN, S // TS),
        in_specs=[pl.BlockSpec((TS, K), lambda j, s: (s, 0)), pl.BlockSpec((TS, TN), lambda j, s: (s, j))],
        out_specs=pl.BlockSpec((K, TN), lambda j, s: (0, j)), out_shape=SDS((K, N), bf16),
        scratch_shapes=[pltpu.VMEM((K, TN), f32)],
        compiler_params=_cp(2, VMEM_LIMIT),
    )(a, b)


def _wgrad_out(oa, ob, dmix, S):
    TS, TN = 1024, 512

    def body(a_ref, b_ref, d_ref, o_ref, acc):
        @pl.when(pl.program_id(1) == 0)
        def _():
            acc[...] = jnp.zeros_like(acc)

        d = d_ref[...]
        acc[0:GDN_W, :] += _dot_tn(a_ref[...].astype(bf16), d)
        acc[GDN_W:D_MODEL, :] += _dot_tn(b_ref[...].astype(bf16), d)

        @pl.when(pl.program_id(1) == pl.num_programs(1) - 1)
        def _():
            o_ref[...] = acc[...].astype(bf16)

    tok = lambda w: pl.BlockSpec((TS, w), lambda j, s: (s, 0))
    return pl.pallas_call(
        body, name="wgrad_out", grid=(D_MODEL // TN, S // TS),
        in_specs=[tok(GDN_W), tok(SWA_W), pl.BlockSpec((TS, TN), lambda j, s: (s, j))],
        out_specs=pl.BlockSpec((D_MODEL, TN), lambda j, s: (0, j)), out_shape=SDS((D_MODEL, D_MODEL), bf16),
        scratch_shapes=[pltpu.VMEM((D_MODEL, TN), f32)],
        compiler_params=_cp(2, VMEM_LIMIT),
    )(oa, ob, dmix)


def _w_in_pieces():
    n_a, n_g = 4 * GDN_W, 2 * GDN_HEADS
    cb = IN_COLS // N_DEV
    bounds = [(0, n_a, COL_A), (n_a, n_a + n_g, COL_G), (n_a + n_g, IN_COLS, COL_B)]
    out = []
    for j in range(N_DEV):
        lo, hi = j * cb, (j + 1) * cb
        for s0, s1, dst in bounds:
            a, b = max(lo, s0), min(hi, s1)
            if a < b:
                out.append((j, a - lo, b - a, dst + a - s0))
    return out


def _wcat_from_blocks(g_in):
    TR = 256
    cb = IN_COLS // N_DEV
    pieces = _w_in_pieces()

    def body(w_ref, o_ref):
        o_ref[:, COL_G:NCOL] = jnp.zeros((TR, NCOL - COL_G), bf16)
        for j, off, w, dst in pieces:
            o_ref[:, dst:dst + w] = w_ref[j, :, off:off + w]

    return pl.pallas_call(
        body, name="wcat_from_blocks", grid=(D_MODEL // TR,),
        in_specs=[pl.BlockSpec((N_DEV, TR, cb), lambda i: (0, i, 0))],
        out_specs=pl.BlockSpec((TR, NCOL), lambda i: (i, 0)),
        out_shape=SDS((D_MODEL, NCOL), bf16),
        compiler_params=_cp(1, VMEM_LIMIT),
    )(g_in)


def _wgrad_in(h1, segs, S):
    TS = 1024
    n = len(segs)
    cols = [(c0, a.shape[1]) for a, c0 in segs]
    cb = IN_COLS // N_DEV
    pieces = _w_in_pieces()

    def body(*refs):
        h_ref = refs[0]
        seg_refs = refs[1:1 + n]
        o_ref, acc = refs[1 + n], refs[2 + n]

        @pl.when(pl.program_id(0) == 0)
        def _():
            acc[...] = jnp.zeros_like(acc)

        h = h_ref[...]
        for s_ref, (c0, w) in zip(seg_refs, cols):
            acc[:, c0:c0 + w] += _dot_tn(h, s_ref[...])

        @pl.when(pl.program_id(0) == pl.num_programs(0) - 1)
        def _():
            for j, off, w, src in pieces:
                o_ref[j, :, off:off + w] = acc[:, src:src + w].astype(bf16)

    row = lambda w: pl.BlockSpec((TS, w), lambda i: (i, 0))
    return pl.pallas_call(
        body, name="wgrad_in", grid=(S // TS,),
        in_specs=[row(D_MODEL)] + [row(w) for _, w in cols],
        out_specs=_const_spec((N_DEV, D_MODEL, cb)),
        out_shape=SDS((N_DEV, D_MODEL, cb), bf16),
        scratch_shapes=[pltpu.VMEM((D_MODEL, NCOL), f32)],
        compiler_params=_cp(1, VMEM_LIMIT),
    )(h1, *[a for a, _ in segs])


def _adamw(recv, src, me, w, m, v, name):
    R, C = w.shape
    TR = 256 if R % 256 == 0 else R
    c1 = 1.0 / (1.0 - ADAM_B1 ** ADAM_STEP)
    c2 = 1.0 / (1.0 - ADAM_B2 ** ADAM_STEP)

    def body(me_ref, r_ref, own_ref, w_ref, m_ref, v_ref, g_out, d_out, m_out, v_out):
        g = None
        for s in range(N_DEV):
            t = jnp.where(me_ref[0] == s, own_ref[0], r_ref[s]).astype(f32)
            g = t if g is None else g + t
        mn = ADAM_B1 * m_ref[...] + (1.0 - ADAM_B1) * g
        vn = ADAM_B2 * v_ref[...] + (1.0 - ADAM_B2) * (g * g)
        g_out[...] = g
        m_out[...] = mn
        v_out[...] = vn
        d_out[...] = -ADAM_LR * ((mn * c1) / (jnp.sqrt(vn * c2) + ADAM_EPS) + ADAM_WD * w_ref[...])

    blk = pl.BlockSpec((TR, C), lambda i, me_ref: (i, 0))
    return pl.pallas_call(
        body, name=name,
        grid_spec=pltpu.PrefetchScalarGridSpec(
            num_scalar_prefetch=1, grid=(R // TR,),
            in_specs=[pl.BlockSpec((N_DEV, TR, C), lambda i, me_ref: (0, i, 0)),
                      pl.BlockSpec((1, TR, C), lambda i, me_ref: (me_ref[0], i, 0)), blk, blk, blk],
            out_specs=[blk, blk, blk, blk]),
        out_shape=[SDS((R, C), f32)] * 4,
        compiler_params=_cp(1, VMEM_LIMIT),
    )(me, recv, src, w, m, v)


MESH = pl.DeviceIdType.MESH
_ANY = pl.BlockSpec(memory_space=pl.ANY)


def _flip(v, d):
    return 1 - v if d else v


def _all_gather(shards):
    n = len(shards)

    def body(*refs):
        ins = refs[:n]
        outs = refs[n:2 * n]
        send_sems, recv_sems, local_sems = refs[2 * n:]
        x, y, c = lax.axis_index("x"), lax.axis_index("y"), lax.axis_index("c")
        me, sibling = (x, y, c), (x, y, 1 - c)
        chips = [(1 - x, y), (x, 1 - y), (1 - x, 1 - y)]

        def slot(px, py, pc):
            return 4 * px + 2 * py + pc

        def copy(a, k, block, to, src=None):
            dst = outs[a].at[slot(*block)]
            return pltpu.make_async_remote_copy(src_ref=dst if src is None else src, dst_ref=dst,
                                                send_sem=send_sems.at[a, k], recv_sem=recv_sems.at[a, k],
                                                device_id=to, device_id_type=MESH)

        mine, first, passed = [], [], []
        for a in range(n):
            cp = pltpu.make_async_copy(ins[a], outs[a].at[slot(*me)], local_sems.at[a])
            cp.start()
            mine.append(cp)
            fs = [copy(a, 0, me, sibling, src=ins[a])]
            fs += [copy(a, 1 + j, me, (*chip, c), src=ins[a]) for j, chip in enumerate(chips)]
            for cp in fs:
                cp.start()
            first += fs
        for j, chip in enumerate(chips):
            for a in range(n):
                copy(a, 1 + j, (*chip, c), me).wait_recv()
                cp = copy(a, 4 + j, (*chip, c), sibling)
                cp.start()
                passed.append(cp)
        for a in range(n):
            copy(a, 0, sibling, me).wait_recv()
            for j, chip in enumerate(chips):
                copy(a, 4 + j, (*chip, 1 - c), me).wait_recv()
        for cp in first + passed:
            cp.wait_send()
        for cp in mine:
            cp.wait()

    return pl.pallas_call(
        body, name="weight_all_gather",
        in_specs=[_ANY] * n, out_specs=[_ANY] * n,
        out_shape=[SDS((N_DEV,) + s.shape, s.dtype) for s in shards],
        scratch_shapes=[pltpu.SemaphoreType.DMA((n, 7)), pltpu.SemaphoreType.DMA((n, 7)), pltpu.SemaphoreType.DMA((n,))],
        compiler_params=pltpu.CompilerParams(has_side_effects=True),
    )(*shards)


def _grad_exchange(blocked, whole):
    arrs = list(blocked) + list(whole)
    n, nb = len(arrs), len(blocked)
    rel = [(dx, dy, dc) for dx in (0, 1) for dy in (0, 1) for dc in (0, 1) if dx or dy or dc]

    def body(*refs):
        ins = refs[:n]
        outs = refs[n:2 * n]
        send_sems, recv_sems, local_sems = refs[2 * n:]
        x, y, c = lax.axis_index("x"), lax.axis_index("y"), lax.axis_index("c")
        me = 4 * x + 2 * y + c
        sends, locs = [], []
        for a in range(n):
            cp = pltpu.make_async_copy(ins[a].at[me] if a < nb else ins[a], outs[a].at[me], local_sems.at[a])
            cp.start()
            locs.append(cp)
            for k, (dx, dy, dc) in enumerate(rel):
                peer = (_flip(x, dx), _flip(y, dy), _flip(c, dc))
                pidx = 4 * peer[0] + 2 * peer[1] + peer[2]
                cp = pltpu.make_async_remote_copy(src_ref=ins[a].at[pidx] if a < nb else ins[a], dst_ref=outs[a].at[me],
                                                  send_sem=send_sems.at[a, k], recv_sem=recv_sems.at[a, k],
                                                  device_id=peer, device_id_type=MESH)
                cp.start()
                sends.append(cp)
        for a in range(n):
            for k, (dx, dy, dc) in enumerate(rel):
                peer = (_flip(x, dx), _flip(y, dy), _flip(c, dc))
                pidx = 4 * peer[0] + 2 * peer[1] + peer[2]
                pltpu.make_async_remote_copy(src_ref=outs[a].at[pidx], dst_ref=outs[a].at[pidx],
                                             send_sem=send_sems.at[a, k], recv_sem=recv_sems.at[a, k],
                                             device_id=peer, device_id_type=MESH).wait_recv()
        for cp in sends:
            cp.wait_send()
        for cp in locs:
            cp.wait()

    shapes = [SDS(a.shape, a.dtype) for a in blocked] + [SDS((N_DEV,) + a.shape, a.dtype) for a in whole]
    return pl.pallas_call(
        body, name="grad_exchange",
        in_specs=[_ANY] * n, out_specs=[_ANY] * n, out_shape=shapes,
        scratch_shapes=[pltpu.SemaphoreType.DMA((n, 7)), pltpu.SemaphoreType.DMA((n, 7)), pltpu.SemaphoreType.DMA((n,))],
        compiler_params=pltpu.CompilerParams(has_side_effects=True),
    )(*arrs)


_HBM = pl.BlockSpec(memory_space=pltpu.HBM)
_SEM = pl.BlockSpec(memory_space=pltpu.SEMAPHORE)
_REL = [(dx, dy, dc) for dx in (0, 1) for dy in (0, 1) for dc in (0, 1) if dx or dy or dc]


N_PEER = len(_REL)
_EFFECT = pltpu.SideEffectType.DATAFLOW_SIDE_EFFECTING


def _peer_copies(srcs, lands, send_sems, recv_sems, blocked, as_receiver):
    x, y, c = lax.axis_index("x"), lax.axis_index("y"), lax.axis_index("c")
    me = 4 * x + 2 * y + c
    cps = []
    for a in range(len(srcs)):
        for k, (dx, dy, dc) in enumerate(_REL):
            peer = (_flip(x, dx), _flip(y, dy), _flip(c, dc))
            pidx = 4 * peer[0] + 2 * peer[1] + peer[2]
            cps.append(pltpu.make_async_remote_copy(
                src_ref=srcs[a].at[pidx] if blocked else srcs[a], dst_ref=lands[a].at[pidx if as_receiver else me],
                send_sem=send_sems[a * N_PEER + k], recv_sem=recv_sems[a * N_PEER + k],
                device_id=peer, device_id_type=MESH))
    return cps


def _exchange_start(srcs, after, blocked, name):
    n = len(srcs)
    ns = n * N_PEER
    lands = [lax.empty(s.shape if blocked else (N_DEV,) + s.shape, s.dtype) for s in srcs]

    def body(*refs):
        ins, lnd = refs[:n], refs[n:2 * n]
        outs = refs[2 * n + 1:]
        for cp in _peer_copies(ins, lnd, outs[:ns], outs[ns:2 * ns], blocked, False):
            cp.start()
        outs[-1][...] = jnp.zeros_like(outs[-1])

    res = pl.pallas_call(
        body, name=name,
        in_specs=[_HBM] * (2 * n) + [_ANY],
        out_specs=[_SEM] * (2 * ns) + [_HBM] * (2 * n) + [pl.BlockSpec(memory_space=pltpu.VMEM)],
        out_shape=[pltpu.SemaphoreType.DMA(())] * (2 * ns) + [pltpu.HBM(s.shape, s.dtype) for s in srcs]
                  + [pltpu.HBM(l.shape, l.dtype) for l in lands] + [SDS((8, LANE), f32)],
        input_output_aliases={i: 2 * ns + i for i in range(2 * n)},
        compiler_params=pltpu.CompilerParams(has_side_effects=_EFFECT),
    )(*[pltpu.with_memory_space_constraint(s, pltpu.HBM) for s in srcs],
      *[pltpu.with_memory_space_constraint(l, pltpu.HBM) for l in lands], after)
    return list(res[:2 * ns]), list(res[2 * ns:2 * ns + n]), list(res[2 * ns + n:2 * ns + 2 * n]), res[-1]


def _exchange_wait(sems, srcs, lands, after, blocked, name):
    n = len(srcs)
    ns = n * N_PEER

    def body(*refs):
        ins, lnd = refs[:n], refs[n:2 * n]
        sem_refs = refs[2 * n:2 * n + 2 * ns]
        for cp in _peer_copies(ins, lnd, sem_refs[:ns], sem_refs[ns:], blocked, True):
            cp.wait_send()
            cp.wait_recv()

    res = pl.pallas_call(
        body, name=name,
        in_specs=[_HBM] * (2 * n) + [_SEM] * (2 * ns) + [_ANY],
        out_specs=[_HBM] * (2 * n),
        out_shape=[pltpu.HBM(s.shape, s.dtype) for s in srcs] + [pltpu.HBM(l.shape, l.dtype) for l in lands],
        input_output_aliases={i: i for i in range(2 * n)},
        compiler_params=pltpu.CompilerParams(has_side_effects=_EFFECT),
    )(*srcs, *lands, *sems, after)
    return list(res[:n]), list(res[n:])


def _local_step(x, tgt, wcat, convw, late_weights, early_grads, last_grads, token, a_log, dt_bias, onorm_g, rel_bias,
                g_mix_pre, g_mix_post, g_ffn_pre, g_ffn_post):
    S = x.shape[0]
    bk_np = _bucket_tables()
    bk = jnp.asarray(bk_np)
    bt = _bias_tables(rel_bias, bk)
    proj, h1 = _proj_fwd(x, g_mix_pre, wcat, token, S)
    nu = S // CHUNK * GDN_HEADS
    qkv_u = _gdn_prep(proj, convw, S).reshape(3, nu, CHUNK, GDN_HD)
    intra, t_inv = _gdn_intra_fwd(qkv_u, proj, a_log, dt_bias, S)
    oa, states = _gdn_scan_fwd(intra, proj, onorm_g, S)
    ob, lse0, lse1 = _swa_fwd(proj, bt, S)
    wout, ffn_weights = late_weights(ob)
    mix, x1 = _mix_fwd(oa, ob, wout, x, g_mix_post, S)
    wgate, wup, wdown = ffn_weights(x1)
    dx1, h2, act, dgate_f, dup_f, df, loss, d_gfpre, d_gfpost = _ffn(x1, tgt, g_ffn_pre, g_ffn_post, wgate, wup, wdown, S)
    rows8 = lambda g: g.reshape(N_DEV, D_FF // N_DEV, D_MODEL)
    g_gate = rows8(_wgrad(dgate_f, h2, S, "wgrad_gate"))
    g_up = rows8(_wgrad(dup_f, h2, S, "wgrad_up"))
    g_down = rows8(_wgrad(act, df, S, "wgrad_down"))
    dmix, d_oab, d_gmpost, nd = _mix_bwd(dx1, mix, g_mix_post, wout, ob, S)
    g_out = _wgrad_out(oa, ob, dmix, S)
    token = early_grads(g_out.reshape(N_DEV, D_MODEL // N_DEV, D_MODEL), g_gate, g_up, g_down)
    dqb, dkb, dvb, dsb = _swa_bwd(proj, bt, nd, lse0, lse1, d_oab, token, S)
    *cots, dgate_a, d_og = _gdn_scan_bwd(intra, states, proj, d_oab, onorm_g, token, S)
    dqkv_u, dpg, d_alog, d_dtb = _gdn_intra_bwd(qkv_u, proj, a_log, dt_bias, t_inv, cots, S)
    dqkv_a, d_conv = _gdn_prep_bwd(proj, convw, dqkv_u.reshape(3, S // CHUNK, GDN_HEADS, CHUNK, GDN_HD), S)
    segs = [(dqkv_a, COL_A), (dgate_a, COL_A + 3 * GDN_W), (dqb, COL_B), (dkb, COL_B + SWA_W), (dvb, COL_B + 2 * SWA_W),
            (dpg, COL_G)]
    token = last_grads(_wgrad_in(h1, segs, S), d_conv)
    grad_x, d_gmpre = _proj_bwd(x, dx1, g_mix_pre, wcat, segs, token, S)
    d_rel = _rel_bias_grad(dsb, bk, bk_np)
    small = dict(a_log=d_alog[:, GDN_HEADS:2 * GDN_HEADS], dt_bias=d_dtb[:, GDN_HEADS:2 * GDN_HEADS], onorm_g=d_og, rel_bias=d_rel,
                 g_mix_pre=d_gmpre, g_mix_post=d_gmpost, g_ffn_pre=d_gfpre, g_ffn_post=d_gfpost)
    return loss, grad_x, small


SMALL = ("a_log", "dt_bias", "onorm_g", "rel_bias", "g_mix_pre", "g_mix_post", "g_ffn_pre", "g_ffn_post")
PACK_ROWS = 8


def _pack_small(d, loss=None):
    rest = jnp.concatenate([d["onorm_g"].reshape(-1), d["a_log"].reshape(-1), d["dt_bias"].reshape(-1),
                            d["rel_bias"].reshape(-1)])
    rest = jnp.concatenate([rest, jnp.zeros((D_MODEL - rest.shape[0],), f32)])
    extra = jnp.zeros((D_MODEL,), f32) if loss is None else jnp.concatenate([loss.reshape(1), jnp.zeros((D_MODEL - 1,), f32)])
    rows = [d["g_mix_pre"].reshape(-1), d["g_mix_post"].reshape(-1), d["g_ffn_pre"].reshape(-1),
            d["g_ffn_post"].reshape(-1), rest, extra]
    return jnp.concatenate([jnp.stack(rows), jnp.zeros((PACK_ROWS - len(rows), D_MODEL), f32)], axis=0)


def _unpack_small(p):
    o = GDN_HD
    return dict(g_mix_pre=p[0:1], g_mix_post=p[1:2], g_ffn_pre=p[2:3], g_ffn_post=p[3:4],
                onorm_g=p[4:5, :o], a_log=p[4:5, o:o + 4], dt_bias=p[4:5, o + 4:o + 8],
                rel_bias=p[4, o + 8:o + 8 + NUM_BUCKETS * SWA_HEADS].reshape(NUM_BUCKETS, SWA_HEADS))


def kernel(x, w_in, conv_w, a_log, dt_bias, onorm_g, rel_bias, w_out, g_mix_pre, g_mix_post, w_gate, w_up, w_down, g_ffn_pre, g_ffn_post, loss_target, m_w_in, m_conv_w, m_a_log, m_dt_bias, m_onorm_g, m_rel_bias, m_w_out, m_g_mix_pre, m_g_mix_post, m_w_gate, m_w_up, m_w_down, m_g_ffn_pre, m_g_ffn_post, v_w_in, v_conv_w, v_a_log, v_dt_bias, v_onorm_g, v_rel_bias, v_w_out, v_g_mix_pre, v_g_mix_post, v_w_gate, v_w_up, v_w_down, v_g_ffn_pre, v_g_ffn_post):
    big = ("w_in", "conv_w", "w_out", "w_gate", "w_up", "w_down")
    transposed = ("w_gate", "w_up")
    tr = lambda k, a: a.T if k in transposed else a
    w_sh = {k: tr(k, a[0]) for k, a in dict(w_in=w_in, conv_w=conv_w, w_out=w_out, w_gate=w_gate, w_up=w_up, w_down=w_down).items()}
    m_sh = {k: tr(k, a[0]) for k, a in dict(w_in=m_w_in, conv_w=m_conv_w, w_out=m_w_out, w_gate=m_w_gate, w_up=m_w_up,
                                             w_down=m_w_down).items()}
    v_sh = {k: tr(k, a[0]) for k, a in dict(w_in=v_w_in, conv_w=v_conv_w, w_out=v_w_out, w_gate=v_w_gate, w_up=v_w_up,
                                             w_down=v_w_down).items()}
    w_small = dict(a_log=a_log, dt_bias=dt_bias, onorm_g=onorm_g, rel_bias=rel_bias, g_mix_pre=g_mix_pre,
                   g_mix_post=g_mix_post, g_ffn_pre=g_ffn_pre, g_ffn_post=g_ffn_post)
    m_small = dict(a_log=m_a_log, dt_bias=m_dt_bias, onorm_g=m_onorm_g, rel_bias=m_rel_bias, g_mix_pre=m_g_mix_pre,
                   g_mix_post=m_g_mix_post, g_ffn_pre=m_g_ffn_pre, g_ffn_post=m_g_ffn_post)
    v_small = dict(a_log=v_a_log, dt_bias=v_dt_bias, onorm_g=v_onorm_g, rel_bias=v_rel_bias, g_mix_pre=v_g_mix_pre,
                   g_mix_post=v_g_mix_post, g_ffn_pre=v_g_ffn_pre, g_ffn_post=v_g_ffn_post)

    me = 4 * lax.axis_index("x") + 2 * lax.axis_index("y") + lax.axis_index("c")
    me1 = me.reshape(1).astype(jnp.int32)
    own = lambda full, part: lax.dynamic_update_index_in_dim(full, part, me, 0)
    cols = lambda g: g.reshape(g.shape[0], N_DEV, g.shape[1] // N_DEV).transpose(1, 0, 2)
    late = ("w_out", "w_gate", "w_up", "w_down")

    late_src = [w_sh[k].astype(bf16) for k in late]
    g_in, g_conv = _all_gather([w_sh["w_in"].astype(bf16), w_sh["conv_w"]])
    g_sems, g_src, g_land, g_token = _exchange_start(late_src, g_conv, False, "late_weights_start")
    wcat = _wcat_from_blocks(g_in)
    convw = g_conv.transpose(1, 0, 2).reshape(4, 3 * GDN_W)

    def late_weights(after):
        pick = lambda idx: [g_sems[half * len(late) * N_PEER + a * N_PEER + k] for half in (0, 1) for a in idx for k in range(N_PEER)]
        (s_out,), (l_out,) = _exchange_wait(pick([0]), g_src[:1], g_land[:1], after, False, "w_out_wait")

        def ffn_weights(after2):
            srcs, lands = _exchange_wait(pick([1, 2, 3]), g_src[1:], g_land[1:], after2, False, "ffn_weights_wait")
            return [own(l, s).reshape(D_FF, D_MODEL) for l, s in zip(lands, srcs)]

        return own(l_out, s_out).reshape(D_MODEL, D_MODEL), ffn_weights

    early, last = {}, {}

    def early_grads(*blocks):
        early["sems"], early["src"], early["land"], token = _exchange_start(list(blocks), me1, True, "late_grads_start")
        return token

    def last_grads(gw_in, gw_conv):
        src = [gw_in, cols(gw_conv)]
        last["sems"], last["src"], last["land"], token = _exchange_start(src, me1, True, "last_grads_start")
        return token

    loss_p, grad_x, gsmall = _local_step(
        x[0], loss_target[0], wcat, convw, late_weights, early_grads, last_grads, g_token,
        a_log, dt_bias, onorm_g, rel_bias, g_mix_pre, g_mix_post, g_ffn_pre, g_ffn_post)

    (r_small,) = _grad_exchange([], [_pack_small(gsmall, loss_p[0, 0])])
    outs = {}
    for names, ex, after, name in ((late, early, grad_x, "late_grads_wait"), (("w_in", "conv_w"), last, r_small, "last_grads_wait")):
        srcs, lands = _exchange_wait(ex["sems"], ex["src"], ex["land"], after, True, name)
        for k, l, s in zip(names, lands, srcs):
            outs[k] = _adamw(l, s, me1, w_sh[k], m_sh[k], v_sh[k], "adamw_" + k)
    sm = _adamw(r_small, r_small, me1, _pack_small(w_small), _pack_small(m_small), _pack_small(v_small), "adamw_small")
    loss = sm[0][5, 0]
    sm = [_unpack_small(t) for t in sm]
    for k in SMALL:
        outs[k] = tuple(t[k].reshape(w_small[k].shape) for t in sm)

    order = ("w_in", "conv_w", "a_log", "dt_bias", "onorm_g", "rel_bias", "w_out", "g_mix_pre", "g_mix_post", "w_gate",
             "w_up", "w_down", "g_ffn_pre", "g_ffn_post")
    lead = lambda k, t: tr(k, t)[None] if k in big else t
    res = [loss, grad_x[None]]
    for i in range(4):
        res += [lead(k, outs[k][i]) for k in order]
    return tuple(res)
```

```python
import functools
import math

import numpy as np
import jax
import jax.numpy as jnp
from jax import lax
from jax.experimental import pallas as pl
from jax.experimental.pallas import tpu as pltpu

f32 = jnp.float32
bf16 = jnp.bfloat16
SDS = jax.ShapeDtypeStruct

D_MODEL = 1024
GDN_HEADS = 4
GDN_HD = 128
GDN_W = 512
CHUNK = 64
SWA_HEADS = 8
SWA_HD = 64
SWA_W = 512
D_FF = 2816
IN_COLS = 3592
PATTERNS = ((128, 1), (512, 4), (2048, 16))
SWA_BLK = 128
NUM_BUCKETS = 32
MAX_DISTANCE = 2048
RMS_EPS = 1e-6
NEG = -1e30
N_DEV = 8

COL_A = 0
COL_B = 2048
COL_G = 3584
NCOL = 3712
LANE = 128

ADAM_LR, ADAM_B1, ADAM_B2, ADAM_EPS, ADAM_WD, ADAM_STEP = 0.001, 0.9, 0.999, 1e-08, 0.01, 10

VMEM_LIMIT = 56 * 1024 * 1024

HIGH = lax.Precision.HIGH


def _cp(n_grid=0, vmem=None):
    kw = {}
    if n_grid:
        kw["dimension_semantics"] = ("arbitrary",) * n_grid
    if vmem:
        kw["vmem_limit_bytes"] = vmem
    return pltpu.CompilerParams(**kw)


def _dot(a, b):
    return jnp.dot(a, b, preferred_element_type=f32)


def _dot_nt(a, b):
    return lax.dot_general(a, b, (((1,), (1,)), ((), ())), preferred_element_type=f32)


def _dot_tn(a, b):
    return lax.dot_general(a, b, (((0,), (0,)), ((), ())), preferred_element_type=f32)


def _sigmoid(x):
    return 0.5 * jnp.tanh(0.5 * x) + 0.5


def _softplus(x):
    return jnp.maximum(x, 0.0) + jnp.log(1.0 + jnp.exp(-jnp.abs(x)))


def _const_spec(shape):
    nd = len(shape)
    return pl.BlockSpec(shape, lambda *_: (0,) * nd)


def _resident_spec(shape):
    nd = len(shape)
    return pl.BlockSpec(shape, lambda *_: (0,) * nd, pipeline_mode=pl.Buffered(1))


def _t5_bucket_np(dist):
    max_exact = NUM_BUCKETS // 2
    d = np.maximum(dist, 1).astype(np.float32)
    log_b = max_exact + (np.log(d / np.float32(max_exact)) / np.float32(math.log(MAX_DISTANCE / max_exact))
                         * np.float32(NUM_BUCKETS - max_exact)).astype(np.int32)
    return np.where(dist < max_exact, dist, np.minimum(log_b, NUM_BUCKETS - 1)).astype(np.int32)


def _bucket_tables():
    w = SWA_BLK
    qi = np.arange(w)[:, None]
    kj = np.arange(w)[None, :]
    rel = np.where(kj <= qi, qi - kj, qi + w - kj)
    out = np.zeros((len(PATTERNS), w, w), np.int32)
    for p, (_, dil) in enumerate(PATTERNS):
        steps = _t5_bucket_np(np.arange(w + 1) * dil)
        assert steps[w] == steps[w - 1]
        out[p] = steps[rel]
    return out


def _fill_bias_tables(rb_ref, bk_ref, o_ref):
    for p in range(len(PATTERNS)):
        b_idx = bk_ref[p]
        for h in range(SWA_HEADS):
            def lp(b, acc):
                return jnp.where(b_idx == b, rb_ref[b, h], acc)
            o_ref[p, h] = lax.fori_loop(0, NUM_BUCKETS, lp, jnp.zeros((SWA_BLK, SWA_BLK), f32))


def _rel_bias_grad(dsb, bk, bk_np):
    present = [sorted(set(int(v) for v in np.unique(bk_np[p]))) for p in range(3)]

    def body(ds_ref, bk_ref, o_ref):
        row = lax.broadcasted_iota(jnp.int32, (NUM_BUCKETS, LANE), 0)
        col = lax.broadcasted_iota(jnp.int32, (NUM_BUCKETS, SWA_HEADS), 1)
        out = jnp.zeros((NUM_BUCKETS, SWA_HEADS), f32)
        for hp in range(4):
            for hh in range(2):
                acc = jnp.zeros((NUM_BUCKETS, LANE), f32)
                for p in range(3):
                    tile = ds_ref[hp, p, hh]
                    b_idx = bk_ref[p]
                    for b in present[p]:
                        part = jnp.sum(jnp.where(b_idx == b, tile, 0.0), axis=0, keepdims=True)
                        acc = acc + jnp.where(row == b, part, 0.0)
                tot = jnp.sum(acc, axis=1, keepdims=True)
                out = out + jnp.where(col == 2 * hp + hh, tot, 0.0)
        o_ref[...] = out

    return pl.pallas_call(body, name="rel_bias_grad", out_shape=SDS((NUM_BUCKETS, SWA_HEADS), f32),
                          compiler_params=_cp(0, 32 * 1024 * 1024))(dsb, bk)


def _proj_fwd(x, g_pre, wcat, after, S):
    TS = 512

    def body(x_ref, g_ref, w_ref, after_ref, o_ref, h_ref):
        xv = x_ref[...]
        r = lax.rsqrt(jnp.mean(xv * xv, axis=-1, keepdims=True) + RMS_EPS)
        h = (xv * r * g_ref[...]).astype(bf16)
        h_ref[...] = h
        o_ref[...] = _dot(h, w_ref[...])

    return pl.pallas_call(
        body, name="proj_fwd", grid=(S // TS,),
        in_specs=[pl.BlockSpec((TS, D_MODEL), lambda i: (i, 0)), _const_spec((1, D_MODEL)),
                  _resident_spec((D_MODEL, NCOL)), _ANY],
        out_specs=[pl.BlockSpec((TS, NCOL), lambda i: (i, 0)), pl.BlockSpec((TS, D_MODEL), lambda i: (i, 0))],
        out_shape=[SDS((S, NCOL), f32), SDS((S, D_MODEL), bf16)],
        compiler_params=_cp(1, VMEM_LIMIT),
    )(x, g_pre, wcat, after)


CONV_RT = 256
HALO = 8


CONV_NC = CONV_RT // CHUNK


def _gdn_prep(proj, conv_w, S):
    def body(p_ref, cw_ref, o_ref, xs_ref):
        t = pl.program_id(0)
        xs_ref[pl.ds(0, HALO), :] = jnp.zeros((HALO, LANE), f32)
        xs_ref[pl.ds(HALO, S), :] = p_ref[...]
        w = cw_ref[...]
        scale = jnp.where(t == 0, GDN_HD ** -0.5, 1.0).astype(f32)

        def run(normed):
            def lp(c, carry):
                st = pl.multiple_of(c * CONV_RT, CONV_RT)
                pre = xs_ref[pl.ds(st + HALO - 3, CONV_RT), :] * w[0:1, :]
                for i in range(1, 4):
                    pre = pre + xs_ref[pl.ds(st + HALO - 3 + i, CONV_RT), :] * w[i:i + 1, :]
                out = pre * _sigmoid(pre)
                if normed:
                    out = out * lax.rsqrt(jnp.sum(out * out, axis=-1, keepdims=True) + 1e-6) * scale
                for i in range(CONV_NC):
                    o_ref[0, c * CONV_NC + i, 0] = out[i * CHUNK:(i + 1) * CHUNK]
                return carry

            lax.fori_loop(0, S // CONV_RT, lp, 0)

        pl.when(t < 2)(functools.partial(run, True))
        pl.when(t == 2)(functools.partial(run, False))

    return pl.pallas_call(
        body, name="gdn_prep", grid=(3, GDN_HEADS),
        in_specs=[pl.BlockSpec((S, LANE), lambda t, h: (0, t * GDN_HEADS + h)),
                  pl.BlockSpec((4, LANE), lambda t, h: (0, t * GDN_HEADS + h))],
        out_specs=pl.BlockSpec((1, S // CHUNK, 1, CHUNK, GDN_HD), lambda t, h: (t, 0, h, 0, 0)),
        out_shape=SDS((3, S // CHUNK, GDN_HEADS, CHUNK, GDN_HD), f32),
        scratch_shapes=[pltpu.VMEM((S + HALO, LANE), f32)],
        compiler_params=_cp(2, VMEM_LIMIT),
    )(proj, conv_w)


def _gdn_prep_bwd(proj, conv_w, dqkv, S):
    def body(p_ref, cw_ref, d_ref, dx_ref, dw_ref, xs_ref, dp_ref):
        t = pl.program_id(0)
        xs_ref[pl.ds(0, HALO), :] = jnp.zeros((HALO, LANE), f32)
        xs_ref[pl.ds(HALO, S), :] = p_ref[...]
        dp_ref[pl.ds(S, HALO), :] = jnp.zeros((HALO, LANE), f32)
        w = cw_ref[...]
        scale = jnp.where(t == 0, GDN_HD ** -0.5, 1.0).astype(f32)

        def first_pass(normed):
            def lp1(c, dw):
                st = pl.multiple_of(c * CONV_RT, CONV_RT)
                taps = [xs_ref[pl.ds(st + HALO - 3 + i, CONV_RT), :] for i in range(4)]
                pre = taps[0] * w[0:1, :]
                for i in range(1, 4):
                    pre = pre + taps[i] * w[i:i + 1, :]
                sg = _sigmoid(pre)
                ds = jnp.concatenate([d_ref[0, c * CONV_NC + i, 0] for i in range(CONV_NC)], axis=0)
                if normed:
                    s = pre * sg
                    rn = lax.rsqrt(jnp.sum(s * s, axis=-1, keepdims=True) + 1e-6)
                    n = s * rn
                    dn = ds * scale
                    ds = rn * (dn - n * jnp.sum(dn * n, axis=-1, keepdims=True))
                dpre = ds * (sg * (1.0 + pre * (1.0 - sg)))
                dp_ref[pl.ds(st, CONV_RT), :] = dpre
                return tuple(dw[i] + jnp.sum(dpre * taps[i], axis=0, keepdims=True) for i in range(4))

            z = jnp.zeros((1, LANE), f32)
            dw = lax.fori_loop(0, S // CONV_RT, lp1, (z, z, z, z))
            for i in range(4):
                dw_ref[pl.ds(i, 1), :] = dw[i]

        pl.when(t < 2)(functools.partial(first_pass, True))
        pl.when(t == 2)(functools.partial(first_pass, False))

        def lp2(c, carry):
            st = pl.multiple_of(c * CONV_RT, CONV_RT)
            dx = dp_ref[pl.ds(st, CONV_RT), :] * w[3:4, :]
            for i in range(3):
                dx = dx + dp_ref[pl.ds(st + 3 - i, CONV_RT), :] * w[i:i + 1, :]
            dx_ref[pl.ds(st, CONV_RT), :] = dx.astype(bf16)
            return carry

        lax.fori_loop(0, S // CONV_RT, lp2, 0)

    col = lambda rows: pl.BlockSpec((rows, LANE), lambda t, h: (0, t * GDN_HEADS + h))
    return pl.pallas_call(
        body, name="gdn_prep_bwd", grid=(3, GDN_HEADS),
        in_specs=[col(S), col(4), pl.BlockSpec((1, S // CHUNK, 1, CHUNK, GDN_HD), lambda t, h: (t, 0, h, 0, 0))],
        out_specs=[col(S), col(4)],
        out_shape=[SDS((S, 3 * GDN_W), bf16), SDS((4, 3 * GDN_W), f32)],
        scratch_shapes=[pltpu.VMEM((S + HALO, LANE), f32), pltpu.VMEM((S + HALO, LANE), f32)],
        compiler_params=_cp(2, VMEM_LIMIT),
    )(proj, conv_w, dqkv)


def _bdot(a, b, prec=None):
    return lax.dot_general(a, b, (((2,), (1,)), ((0,), (0,))), precision=prec, preferred_element_type=f32)


def _bdot_nt(a, b, prec=None):
    return lax.dot_general(a, b, (((2,), (2,)), ((0,), (0,))), precision=prec, preferred_element_type=f32)


def _bdot_tn(a, b, prec=None):
    return lax.dot_general(a, b, (((1,), (1,)), ((0,), (0,))), precision=prec, preferred_element_type=f32)


@jax.custom_vjp
def _tri_inv_saved(a, t):
    return t


def _tri_inv_saved_fwd(a, t):
    return t, t


def _tri_inv_saved_bwd(t, dt):
    return -_bdot_tn(t, _bdot_nt(dt, t, HIGH), HIGH), jnp.zeros_like(t)


_tri_inv_saved.defvjp(_tri_inv_saved_fwd, _tri_inv_saved_bwd)


def _gdn_intra(q, k, v, beta, g, t_saved=None):
    nb = q.shape[0]
    c = CHUNK
    ii = lax.broadcasted_iota(jnp.int32, (c, c), 0)
    jj = lax.broadcasted_iota(jnp.int32, (c, c), 1)
    eye = ii == jj
    tril = ii >= jj
    strict = ii > jj
    ones = jnp.ones((nb, c, c), f32)
    eye_f = eye.astype(f32)

    g_row = _bdot(ones, jnp.where(eye, g, 0.0), HIGH)
    gc = jnp.sum(jnp.where(tril, g_row, 0.0), axis=2, keepdims=True)
    gc_row = _bdot(ones, jnp.where(eye, gc, 0.0), HIGH)
    decay = jnp.where(tril, jnp.exp(jnp.where(tril, gc - gc_row, 0.0)), 0.0)
    last = lax.broadcasted_iota(jnp.int32, (c, 1), 0) == c - 1
    gc_last = jnp.sum(jnp.where(last, gc, 0.0), axis=1, keepdims=True)
    e_gc = jnp.exp(gc)

    kb = k * beta
    k16 = k.astype(bf16)
    a = jnp.where(strict, _bdot_nt(kb.astype(bf16), k16) * decay, 0.0)
    if t_saved is None:
        xp = -a
        t_inv = eye_f + xp
        for level in range(5):
            if level < 2:
                xp = _bdot(xp, xp, HIGH)
                t_inv = t_inv + _bdot(t_inv, xp, HIGH)
            else:
                x16 = xp.astype(bf16)
                xp = _bdot(x16, x16)
                t_inv = t_inv + _bdot(t_inv.astype(bf16), xp.astype(bf16))
    else:
        t_inv = _tri_inv_saved(a, t_saved)
    t16 = t_inv.astype(bf16)
    u = _bdot(t16, (v * beta).astype(bf16))
    w = _bdot(t16, (kb * e_gc).astype(bf16))
    attn = jnp.where(tril, _bdot_nt(q.astype(bf16), k16) * decay, 0.0)
    gam = jnp.broadcast_to(jnp.exp(gc_last), (nb, 1, GDN_HD))
    return u, w, attn, q * e_gc, k * jnp.exp(gc_last - gc), gam, t_inv


GDN_TB = 512
GDN_NC = GDN_TB // CHUNK
GDN_NU = GDN_NC * GDN_HEADS


def _gdn_gates(pg_ref, al_ref, db_ref):
    lane1 = lax.broadcasted_iota(jnp.int32, (1, LANE), 1)
    a_lane = jnp.zeros((1, LANE), f32)
    b_lane = jnp.zeros((1, LANE), f32)
    for h in range(GDN_HEADS):
        a_lane = jnp.where(lane1 == GDN_HEADS + h, al_ref[0, h], a_lane)
        b_lane = jnp.where(lane1 == GDN_HEADS + h, db_ref[0, h], b_lane)
    pg = pg_ref[...]
    z = pg + b_lane
    return _sigmoid(pg), -jnp.exp(a_lane) * _softplus(z), z, a_lane


def _gdn_unit_inputs(qkv_ref, beta_all, g_all):
    units = [(cl, h) for cl in range(GDN_NC) for h in range(GDN_HEADS)]
    beta = jnp.stack([beta_all[cl * CHUNK:(cl + 1) * CHUNK, h:h + 1] for cl, h in units])
    g = jnp.stack([g_all[cl * CHUNK:(cl + 1) * CHUNK, GDN_HEADS + h:GDN_HEADS + h + 1] for cl, h in units])
    return qkv_ref[0], qkv_ref[1], qkv_ref[2], beta, g


def _unit_spec(*tail):
    nd = len(tail)
    return pl.BlockSpec((GDN_NU,) + tail, lambda i: (i,) + (0,) * nd)


def _gdn_intra_shapes(S):
    nu = S // CHUNK * GDN_HEADS
    row = SDS((nu, CHUNK, GDN_HD), f32)
    return [row, row, SDS((nu, CHUNK, CHUNK), f32), row, row, SDS((nu, 1, GDN_HD), f32)]


_GDN_INTRA_SPECS = lambda: [_unit_spec(CHUNK, GDN_HD), _unit_spec(CHUNK, GDN_HD), _unit_spec(CHUNK, CHUNK),
                            _unit_spec(CHUNK, GDN_HD), _unit_spec(CHUNK, GDN_HD), _unit_spec(1, GDN_HD)]


def _gdn_intra_fwd(qkv_u, proj, a_log, dt_bias, S):
    def body(qkv_ref, pg_ref, al_ref, db_ref, *outs):
        beta_all, g_all, _, _ = _gdn_gates(pg_ref, al_ref, db_ref)
        res = _gdn_intra(*_gdn_unit_inputs(qkv_ref, beta_all, g_all))
        for o_ref, r in zip(outs, res):
            o_ref[...] = r

    nu = S // CHUNK * GDN_HEADS
    *intra, t_inv = pl.pallas_call(
        body, name="gdn_intra_fwd", grid=(S // GDN_TB,),
        in_specs=[pl.BlockSpec((3, GDN_NU, CHUNK, GDN_HD), lambda i: (0, i, 0, 0)),
                  pl.BlockSpec((GDN_TB, LANE), lambda i: (i, COL_G // LANE)),
                  pl.BlockSpec(memory_space=pltpu.SMEM), pl.BlockSpec(memory_space=pltpu.SMEM)],
        out_specs=_GDN_INTRA_SPECS() + [_unit_spec(CHUNK, CHUNK)],
        out_shape=_gdn_intra_shapes(S) + [SDS((nu, CHUNK, CHUNK), f32)],
        compiler_params=_cp(1, VMEM_LIMIT),
    )(qkv_u, proj, a_log, dt_bias)
    return intra, t_inv


def _gdn_intra_bwd(qkv_u, proj, a_log, dt_bias, t_inv, cots, S):
    def body(qkv_ref, pg_ref, al_ref, db_ref, t_ref, du_ref, dw_ref, da_ref, dqd_ref, dkd_ref, dgm_ref,
             dqkv_ref, dpg_ref, dal_ref, ddb_ref):
        @pl.when(pl.program_id(0) == 0)
        def _():
            dal_ref[...] = jnp.zeros_like(dal_ref)
            ddb_ref[...] = jnp.zeros_like(ddb_ref)

        t_saved = t_ref[...]
        beta_all, g_all, z, a_lane = _gdn_gates(pg_ref, al_ref, db_ref)
        _, vjp = jax.vjp(lambda *a: _gdn_intra(*a, t_saved=t_saved)[:6], *_gdn_unit_inputs(qkv_ref, beta_all, g_all))
        dq, dk, dv, dbeta, dg = vjp((du_ref[...], dw_ref[...], da_ref[...], dqd_ref[...], dkd_ref[...], dgm_ref[...]))
        dqkv_ref[0] = dq
        dqkv_ref[1] = dk
        dqkv_ref[2] = dv
        lane = lax.broadcasted_iota(jnp.int32, (CHUNK, LANE), 1)
        rows = []
        for cl in range(GDN_NC):
            t = jnp.zeros((CHUNK, LANE), f32)
            for h in range(GDN_HEADS):
                b = cl * GDN_HEADS + h
                t = t + jnp.where(lane == h, dbeta[b], 0.0) + jnp.where(lane == GDN_HEADS + h, dg[b], 0.0)
            rows.append(t)
        d_all = jnp.concatenate(rows, axis=0)
        is_beta = lax.broadcasted_iota(jnp.int32, (GDN_TB, LANE), 1) < GDN_HEADS
        dz = d_all * (-jnp.exp(a_lane)) * _sigmoid(z)
        dpg_ref[...] = jnp.where(is_beta, d_all * beta_all * (1.0 - beta_all), dz).astype(bf16)
        dal_ref[...] += jnp.sum(jnp.where(is_beta, 0.0, d_all * g_all), axis=0, keepdims=True)
        ddb_ref[...] += jnp.sum(jnp.where(is_beta, 0.0, dz), axis=0, keepdims=True)

    acc = _const_spec((1, LANE))
    nu = S // CHUNK * GDN_HEADS
    return pl.pallas_call(
        body, name="gdn_intra_bwd", grid=(S // GDN_TB,),
        in_specs=[pl.BlockSpec((3, GDN_NU, CHUNK, GDN_HD), lambda i: (0, i, 0, 0)),
                  pl.BlockSpec((GDN_TB, LANE), lambda i: (i, COL_G // LANE)),
                  pl.BlockSpec(memory_space=pltpu.SMEM), pl.BlockSpec(memory_space=pltpu.SMEM),
                  _unit_spec(CHUNK, CHUNK)] + _GDN_INTRA_SPECS(),
        out_specs=[pl.BlockSpec((3, GDN_NU, CHUNK, GDN_HD), lambda i: (0, i, 0, 0)),
                   pl.BlockSpec((GDN_TB, LANE), lambda i: (i, 0)), acc, acc],
        out_shape=[SDS((3, nu, CHUNK, GDN_HD), f32), SDS((S, LANE), bf16), SDS((1, LANE), f32), SDS((1, LANE), f32)],
        compiler_params=_cp(1, VMEM_LIMIT),
    )(qkv_u, proj, a_log, dt_bias, t_inv, *cots)


def _gdn_scan_fwd(intra, proj, onorm_g, S):
    def body(u_ref, w_ref, at_ref, qd_ref, kd_ref, gm_ref, gate_ref, og_ref, out_ref, st_ref, s_scr):
        @pl.when(pl.program_id(0) == 0)
        def _():
            s_scr[...] = jnp.zeros_like(s_scr)

        og = og_ref[...]
        s = s_scr[...]
        def out_mm(us, s16, vn16):
            return _bdot(qd_ref[us].astype(bf16), s16) + _bdot(at_ref[us].astype(bf16), vn16)

        outs, prev = [], None
        for cl in range(GDN_NC):
            us = slice(cl * GDN_HEADS, (cl + 1) * GDN_HEADS)
            st_ref[us] = s
            s16 = s.astype(bf16)
            ws = _bdot(w_ref[us].astype(bf16), s16)
            if prev is not None:
                outs.append(out_mm(*prev))
            vn16 = (u_ref[us] - ws).astype(bf16)
            prev = (us, s16, vn16)
            s = s * gm_ref[us] + _bdot_tn(kd_ref[us].astype(bf16), vn16)
        outs.append(out_mm(*prev))
        s_scr[...] = s
        for cl, o in enumerate(outs):
            rows = slice(cl * CHUNK, (cl + 1) * CHUNK)
            for h in range(GDN_HEADS):
                oh = o[h]
                gt = gate_ref[rows, h * GDN_HD:(h + 1) * GDN_HD]
                on = oh * lax.rsqrt(jnp.mean(oh * oh, axis=-1, keepdims=True) + RMS_EPS) * og
                out_ref[rows, h * GDN_HD:(h + 1) * GDN_HD] = on * (gt * _sigmoid(gt))

    nu = S // CHUNK * GDN_HEADS
    return pl.pallas_call(
        body, name="gdn_scan_fwd", grid=(S // GDN_TB,),
        in_specs=_GDN_INTRA_SPECS() + [pl.BlockSpec((GDN_TB, GDN_W), lambda i: (i, 3)), _const_spec((1, GDN_HD))],
        out_specs=[pl.BlockSpec((GDN_TB, GDN_W), lambda i: (i, 0)), _unit_spec(GDN_HD, GDN_HD)],
        out_shape=[SDS((S, GDN_W), f32), SDS((nu, GDN_HD, GDN_HD), f32)],
        scratch_shapes=[pltpu.VMEM((GDN_HEADS, GDN_HD, GDN_HD), f32)],
        compiler_params=_cp(1, VMEM_LIMIT),
    )(*intra, proj, onorm_g)


def _gdn_scan_bwd(intra, states, proj, d_oab, onorm_g, after, S):
    n_steps = S // GDN_TB

    def body(u_ref, w_ref, at_ref, qd_ref, kd_ref, gm_ref, st_ref, gate_ref, do_ref, og_ref, after_ref,
             du_ref, dw_ref, dat_ref, dqd_ref, dkd_ref, dgm_ref, dgate_ref, dog_ref, ds_scr):
        @pl.when(pl.program_id(0) == 0)
        def _():
            ds_scr[...] = jnp.zeros_like(ds_scr)
            dog_ref[...] = jnp.zeros_like(dog_ref)

        og = og_ref[...]
        ii = lax.broadcasted_iota(jnp.int32, (CHUNK, CHUNK), 0)
        jj = lax.broadcasted_iota(jnp.int32, (CHUNK, CHUNK), 1)
        tril = ii >= jj
        dog = jnp.zeros((1, GDN_HD), f32)
        pre = []
        for cl in range(GDN_NC):
            us = slice(cl * GDN_HEADS, (cl + 1) * GDN_HEADS)
            rows = slice(cl * CHUNK, (cl + 1) * CHUNK)
            s016 = st_ref[us].astype(bf16)
            w16 = w_ref[us].astype(bf16)
            qd16 = qd_ref[us].astype(bf16)
            at16 = at_ref[us].astype(bf16)
            vn16 = (u_ref[us] - _bdot(w16, s016)).astype(bf16)
            o = _bdot(qd16, s016) + _bdot(at16, vn16)
            do_h = []
            for h in range(GDN_HEADS):
                oh = o[h]
                lanes = slice(h * GDN_HD, (h + 1) * GDN_HD)
                gt = gate_ref[rows, lanes]
                d_out = do_ref[rows, lanes]
                r = lax.rsqrt(jnp.mean(oh * oh, axis=-1, keepdims=True) + RMS_EPS)
                n = oh * r
                sg = _sigmoid(gt)
                silu = gt * sg
                dog = dog + jnp.sum(d_out * n * silu, axis=0, keepdims=True)
                dgate_ref[rows, lanes] = (d_out * n * og * (sg * (1.0 + gt * (1.0 - sg)))).astype(bf16)
                dn = d_out * og * silu
                do_h.append(r * (dn - n * jnp.mean(dn * n, axis=-1, keepdims=True)))
            do16 = jnp.stack(do_h).astype(bf16)
            pre.append((us, s016, w16, vn16, do16, _bdot_tn(at16, do16), _bdot_tn(qd16, do16)))
        ds = ds_scr[...]
        chain = [None] * GDN_NC
        for cl in reversed(range(GDN_NC)):
            us, s016, w16, vn16, do16, at_do, qd_do = pre[cl]
            ds16 = ds.astype(bf16)
            dvn = at_do + _bdot(kd_ref[us].astype(bf16), ds16)
            dvn16 = dvn.astype(bf16)
            chain[cl] = (ds, ds16, dvn, dvn16)
            ds = qd_do + ds * gm_ref[us] - _bdot_tn(w16, dvn16)
        ds_scr[...] = ds
        for cl in range(GDN_NC):
            us, s016, w16, vn16, do16, _, _ = pre[cl]
            ds_in, ds16, dvn, dvn16 = chain[cl]
            du_ref[us] = dvn
            dw_ref[us] = -_bdot_nt(dvn16, s016)
            dat_ref[us] = jnp.where(tril, _bdot_nt(do16, vn16), 0.0)
            dqd_ref[us] = _bdot_nt(do16, s016)
            dkd_ref[us] = _bdot_nt(vn16, ds16)
            dgm_ref[us] = jnp.sum(st_ref[us] * ds_in, axis=1, keepdims=True)
        dog_ref[...] += dog

    def unit(*tail):
        nd = len(tail)
        return pl.BlockSpec((GDN_NU,) + tail, lambda i: (n_steps - 1 - i,) + (0,) * nd)

    intra_specs = [unit(CHUNK, GDN_HD), unit(CHUNK, GDN_HD), unit(CHUNK, CHUNK), unit(CHUNK, GDN_HD),
                   unit(CHUNK, GDN_HD), unit(1, GDN_HD)]
    tok = lambda c: pl.BlockSpec((GDN_TB, GDN_W), lambda i: (n_steps - 1 - i, c))
    return pl.pallas_call(
        body, name="gdn_scan_bwd", grid=(n_steps,),
        in_specs=intra_specs + [unit(GDN_HD, GDN_HD), tok(3), tok(0), _const_spec((1, GDN_HD)), _ANY],
        out_specs=intra_specs + [tok(0), _const_spec((1, GDN_HD))],
        out_shape=_gdn_intra_shapes(S) + [SDS((S, GDN_W), bf16), SDS((1, GDN_HD), f32)],
        scratch_shapes=[pltpu.VMEM((GDN_HEADS, GDN_HD, GDN_HD), f32)],
        compiler_params=_cp(1, VMEM_LIMIT),
    )(*intra, states, proj, d_oab, onorm_g, after)


SWA_UNROLL = 8


def _swa_tiles(it, d, nb_log2):
    nb = 1 << nb_log2
    r = lax.shift_right_logical(it, nb_log2)
    blk = lax.bitwise_and(it, nb - 1)
    qs = blk * (SWA_BLK * d) + r
    ps = jnp.maximum(blk - 1, 0) * (SWA_BLK * d) + r
    if d > 1:
        rows_c, rows_p = pl.ds(qs, SWA_BLK, stride=d), pl.ds(ps, SWA_BLK, stride=d)
    else:
        rows_c, rows_p = pl.ds(pl.multiple_of(qs, SWA_BLK), SWA_BLK), pl.ds(pl.multiple_of(ps, SWA_BLK), SWA_BLK)
    return rows_c, rows_p, blk > 0


def _swa_prev_modes(nb):
    if nb >= SWA_UNROLL:
        return ["load"] + ["reuse"] * (SWA_UNROLL - 1)
    return ["none" if u % nb == 0 else "reuse" for u in range(SWA_UNROLL)]


def _swa_fwd(proj, bt, S):
    scale = SWA_HD ** -0.5

    def body(q_ref, k_ref, v_ref, bt_ref, o_ref, lse0_ref, lse1_ref, m0_scr, m1_scr, a0_scr, a1_scr):
        lane = lax.broadcasted_iota(jnp.int32, (SWA_BLK, LANE), 1)
        h0 = lane < SWA_HD
        qi = lax.broadcasted_iota(jnp.int32, (SWA_BLK, SWA_BLK), 0)
        kj = lax.broadcasted_iota(jnp.int32, (SWA_BLK, SWA_BLK), 1)
        lower = kj <= qi
        ones16 = jnp.ones((LANE, SWA_BLK), bf16)
        m_scrs = (m0_scr, m1_scr)
        a_scrs = (a0_scr, a1_scr)
        for p, (_, d) in reversed(list(enumerate(PATTERNS))):
            nb_log2 = int(math.log2(S // d // SWA_BLK))
            first = p == len(PATTERNS) - 1

            def lp(i, carry, p=p, d=d, nb_log2=nb_log2, first=first):
                heads = [h0, jnp.logical_not(h0)]
                modes = _swa_prev_modes(1 << nb_log2)
                tiles = []
                kc_f = None
                for u in range(SWA_UNROLL):
                    rows_c, rows_p, has_prev = _swa_tiles(i * SWA_UNROLL + u, d, nb_log2)
                    kp_f = {"load": lambda: k_ref[rows_p, :], "reuse": lambda: kc_f, "none": lambda: None}[modes[u]]()
                    has_prev = {"load": has_prev, "reuse": True, "none": False}[modes[u]]
                    q = q_ref[rows_c, :]
                    kc_f = k_ref[rows_c, :]
                    kc = kc_f.astype(bf16)
                    logits = []
                    for mh in heads:
                        q_h = jnp.where(mh, q, 0.0)
                        qh = q_h.astype(bf16)
                        if kp_f is None:
                            logits.append((_dot_nt(qh, kc), None, None))
                        else:
                            logits.append((_dot_nt(qh, kc), _dot_nt(qh, kp_f.astype(bf16)), _dot((q_h * kp_f).astype(bf16), ones16)))
                    tiles.append((rows_c, rows_p, has_prev, logits))
                probs = []
                for rows_c, rows_p, has_prev, logits in tiles:
                    per_head = []
                    for h, (s_c, s_p, far) in enumerate(logits):
                        if has_prev is False:
                            s = jnp.where(lower, s_c * scale + bt_ref[p, h], NEG)
                            s_far = None
                        else:
                            s = jnp.where(lower, s_c, s_p) * scale + bt_ref[p, h]
                            s_far = far * scale + bt_ref[p, h, SWA_BLK - 1:SWA_BLK, 0:1]
                            if has_prev is not True:
                                s = jnp.where(jnp.logical_or(lower, has_prev), s, NEG)
                                s_far = jnp.where(has_prev, s_far, NEG)
                        mn = jnp.max(s, axis=1, keepdims=True)
                        if s_far is not None:
                            mn = jnp.maximum(s_far, mn)
                        alpha = None
                        if not first:
                            mo = m_scrs[h][rows_c, :]
                            mn = jnp.maximum(mo, mn)
                            alpha = jnp.exp(mo - mn)
                        mn = jnp.broadcast_to(mn, (SWA_BLK, LANE))
                        pm = jnp.exp(s - mn)
                        per_head.append((mn, alpha, None if s_far is None else jnp.exp(s_far - mn),
                                         jnp.where(lower, pm, 0.0).astype(bf16),
                                         None if s_far is None else jnp.where(lower, 0.0, pm).astype(bf16)))
                    probs.append(per_head)
                acc_old = [None if first else (a0_scr[t[0], :], a1_scr[t[0], :]) for t in tiles]
                done = []
                vc = None
                for u, ((rows_c, rows_p, _, _), per_head, old) in enumerate(zip(tiles, probs, acc_old)):
                    vp = {"load": lambda: v_ref[rows_p, :], "reuse": lambda: vc, "none": lambda: None}[modes[u]]()
                    vc = v_ref[rows_c, :]
                    acc_new = []
                    for h, (mn, alpha, p_far, pc16, pp16) in enumerate(per_head):
                        pv = _dot(pc16, jnp.where(heads[h], vc, 1.0).astype(bf16))
                        if pp16 is not None:
                            vpa = jnp.where(heads[h], vp, 1.0)
                            pv = pv + _dot(pp16, vpa.astype(bf16)) + p_far * vpa
                        acc_new.append(pv if first else alpha * old[h] + pv)
                    done.append((rows_c, per_head[0][0], per_head[1][0], acc_new[0], acc_new[1]))
                for rows_c, m0_new, m1_new, a0_new, a1_new in done:
                    m0_scr[rows_c, :] = m0_new
                    m1_scr[rows_c, :] = m1_new
                    a0_scr[rows_c, :] = a0_new
                    a1_scr[rows_c, :] = a1_new
                return carry

            lax.fori_loop(0, S // SWA_BLK // SWA_UNROLL, lp, 0)

        def fin(c, carry):
            rows = pl.ds(pl.multiple_of(c * SWA_BLK, SWA_BLK), SWA_BLK)
            a0 = a0_scr[rows, :]
            a1 = a1_scr[rows, :]
            l0 = jnp.where(h0, pltpu.roll(a0, SWA_HD, 1), a0)
            l1 = jnp.where(h0, a1, pltpu.roll(a1, SWA_HD, 1))
            o_ref[rows, :] = jnp.where(h0, a0 / l0, a1 / l1)
            lse0_ref[rows, :] = m0_scr[rows, :] + jnp.log(l0)
            lse1_ref[rows, :] = m1_scr[rows, :] + jnp.log(l1)
            return carry

        lax.fori_loop(0, S // SWA_BLK, fin, 0)

    qb = COL_B // LANE
    col = lambda c: pl.BlockSpec((S, LANE), lambda hp, c=c: (0, c + hp))
    return pl.pallas_call(
        body, name="swa_fwd", grid=(4,),
        in_specs=[col(qb), col(qb + 4), col(qb + 8), pl.BlockSpec((3, 2, SWA_BLK, SWA_BLK), lambda hp: (0, hp, 0, 0))],
        out_specs=[col(0), col(0), col(0)],
        out_shape=[SDS((S, SWA_W), f32)] * 3,
        scratch_shapes=[pltpu.VMEM((S, LANE), f32)] * 4,
        compiler_params=_cp(1, VMEM_LIMIT),
    )(proj, proj, proj, bt)


def _swa_bwd(proj, bt, nd, lse0, lse1, d_oab, after, S):
    scale = SWA_HD ** -0.5

    def body(q_ref, k_ref, v_ref, bt_ref, nd_scr, lse0_ref, lse1_ref, do_ref, after_ref, dq_ref, dk_ref, dv_ref, dsb_ref,
             dq_scr, dk_scr, dv_scr):
        lane = lax.broadcasted_iota(jnp.int32, (SWA_BLK, LANE), 1)
        h0 = lane < SWA_HD
        qi = lax.broadcasted_iota(jnp.int32, (SWA_BLK, SWA_BLK), 0)
        kj = lax.broadcasted_iota(jnp.int32, (SWA_BLK, SWA_BLK), 1)
        lower = kj <= qi
        eye = kj == qi
        rel127 = jnp.logical_or(kj == qi + 1, jnp.logical_and(qi == SWA_BLK - 1, kj == 0))
        ones16 = jnp.ones((LANE, SWA_BLK), bf16)
        lse_refs = (lse0_ref, lse1_ref)
        dk_scr[...] = jnp.zeros((S, LANE), f32)
        dv_scr[...] = jnp.zeros((S, LANE), f32)
        dsb_ref[...] = jnp.zeros_like(dsb_ref)

        for p, (_, d) in reversed(list(enumerate(PATTERNS))):
            nb_log2 = int(math.log2(S // d // SWA_BLK))
            first = p == len(PATTERNS) - 1

            def lp(i, carry, p=p, d=d, nb_log2=nb_log2, first=first):
                heads = [h0, jnp.logical_not(h0)]
                modes = _swa_prev_modes(1 << nb_log2)
                tiles = []
                kc_f = vc_f = None
                for u in range(SWA_UNROLL):
                    rows_c, rows_p, has_prev = _swa_tiles(i * SWA_UNROLL + u, d, nb_log2)
                    kp_f = {"load": lambda: k_ref[rows_p, :], "reuse": lambda: kc_f, "none": lambda: None}[modes[u]]()
                    vp_f = {"load": lambda: v_ref[rows_p, :], "reuse": lambda: vc_f, "none": lambda: None}[modes[u]]()
                    has_prev = {"load": has_prev, "reuse": True, "none": False}[modes[u]]
                    q = q_ref[rows_c, :]
                    kc_f = k_ref[rows_c, :]
                    vc_f = v_ref[rows_c, :]
                    kc = kc_f.astype(bf16)
                    kp = None if kp_f is None else kp_f.astype(bf16)
                    do = do_ref[rows_c, :]
                    nd = nd_scr[rows_c, :]
                    per_head = []
                    for mh in heads:
                        q_h = jnp.where(mh, q, 0.0)
                        do_a = jnp.where(mh, do, nd)
                        qh = q_h.astype(bf16)
                        doa = do_a.astype(bf16)
                        doh = jnp.where(mh, do, 0.0).astype(bf16)
                        dd_c = _dot_nt(doa, jnp.where(mh, vc_f, 1.0).astype(bf16))
                        if kp_f is None:
                            per_head.append((qh, doh, _dot_nt(qh, kc), None, None, dd_c, None, None))
                        else:
                            vpa = jnp.where(mh, vp_f, 1.0)
                            per_head.append((qh, doh, _dot_nt(qh, kc), _dot_nt(qh, kp), _dot((q_h * kp_f).astype(bf16), ones16),
                                             dd_c, _dot_nt(doa, vpa.astype(bf16)), _dot((do_a * vpa).astype(bf16), ones16)))
                    tiles.append((rows_c, rows_p, has_prev, kc, kp, per_head))
                grads = []
                for rows_c, rows_p, has_prev, kc, kp, per_head in tiles:
                    out = []
                    for h, (qh, doh, s_c, s_p, far, dd_c, dd_p, dd_far) in enumerate(per_head):
                        lse_h = lse_refs[h][rows_c, :]
                        if has_prev is False:
                            pm = jnp.exp(jnp.where(lower, s_c * scale + bt_ref[p, h], NEG) - lse_h)
                            dsm = pm * dd_c
                            out.append((dsm, dsm.astype(bf16), None, pm.astype(bf16), None))
                            continue
                        s = jnp.where(lower, s_c, s_p) * scale + bt_ref[p, h]
                        s_far = far * scale + bt_ref[p, h, SWA_BLK - 1:SWA_BLK, 0:1]
                        if has_prev is not True:
                            s = jnp.where(jnp.logical_or(lower, has_prev), s, NEG)
                            s_far = jnp.where(has_prev, s_far, NEG)
                        pm = jnp.exp(s - lse_h)
                        p_far = jnp.exp(s_far - lse_h)
                        dsm = pm * jnp.where(lower, dd_c, dd_p)
                        ds_far = p_far * dd_far
                        out.append((dsm + jnp.where(rel127, ds_far, 0.0),
                                    jnp.where(lower, dsm, 0.0).astype(bf16),
                                    jnp.where(lower, jnp.where(eye, ds_far, 0.0), dsm).astype(bf16),
                                    jnp.where(lower, pm, 0.0).astype(bf16),
                                    jnp.where(lower, jnp.where(eye, p_far, 0.0), pm).astype(bf16)))
                    grads.append(out)
                done = []
                add = lambda acc, t: t if acc is None else acc + t
                for (rows_c, rows_p, _, kc, kp, per_head), out in zip(tiles, grads):
                    dq_t = dkc_t = dkp_t = dvc_t = dvp_t = None
                    for h, (_, dsc16, dsp16, pc16, pp16) in enumerate(out):
                        qh, doh = per_head[h][0], per_head[h][1]
                        dq_h = _dot(dsc16, kc)
                        dkc_t = add(dkc_t, _dot_tn(dsc16, qh) * scale)
                        dvc_t = add(dvc_t, _dot_tn(pc16, doh))
                        if dsp16 is not None:
                            dq_h = dq_h + _dot(dsp16, kp)
                            dkp_t = add(dkp_t, _dot_tn(dsp16, qh) * scale)
                            dvp_t = add(dvp_t, _dot_tn(pp16, doh))
                        dq_t = add(dq_t, jnp.where(heads[h], dq_h * scale, 0.0))
                    done.append([rows_c, rows_p, dq_t, dkc_t, dkp_t, dvc_t, dvp_t])
                for u in range(1, SWA_UNROLL):
                    if modes[u] == "reuse":
                        done[u - 1][3] = done[u - 1][3] + done[u][4]
                        done[u - 1][5] = done[u - 1][5] + done[u][6]
                for h in range(2):
                    tot = grads[0][h][0]
                    for g in grads[1:]:
                        tot = tot + g[h][0]
                    dsb_ref[0, p, h] += tot
                for u, (rows_c, rows_p, dq_t, dkc_t, dkp_t, dvc_t, dvp_t) in enumerate(done):
                    dq_scr[rows_c, :] = dq_t if first else dq_scr[rows_c, :] + dq_t
                    dk_scr[rows_c, :] = dk_scr[rows_c, :] + dkc_t
                    dv_scr[rows_c, :] = dv_scr[rows_c, :] + dvc_t
                    if modes[u] == "load":
                        dk_scr[rows_p, :] = dk_scr[rows_p, :] + dkp_t
                        dv_scr[rows_p, :] = dv_scr[rows_p, :] + dvp_t
                return carry

            lax.fori_loop(0, S // SWA_BLK // SWA_UNROLL, lp, 0)
        dq_ref[...] = dq_scr[...].astype(bf16)
        dk_ref[...] = dk_scr[...].astype(bf16)
        dv_ref[...] = dv_scr[...].astype(bf16)

    qb = COL_B // LANE
    col = lambda c: pl.BlockSpec((S, LANE), lambda hp, c=c: (0, c + hp))
    return pl.pallas_call(
        body, name="swa_bwd", grid=(4,),
        in_specs=[col(qb), col(qb + 4), col(qb + 8),
                  pl.BlockSpec((3, 2, SWA_BLK, SWA_BLK), lambda hp: (0, hp, 0, 0)),
                  col(0), col(0), col(0), col(4), _ANY],
        out_specs=[col(0), col(0), col(0),
                   pl.BlockSpec((1, 3, 2, SWA_BLK, SWA_BLK), lambda hp: (hp, 0, 0, 0, 0))],
        out_shape=[SDS((S, SWA_W), bf16)] * 3 + [SDS((4, 3, 2, SWA_BLK, SWA_BLK), f32)],
        scratch_shapes=[pltpu.VMEM((S, LANE), f32)] * 3,
        compiler_params=_cp(1, VMEM_LIMIT),
    )(proj, proj, proj, bt, nd, lse0, lse1, d_oab, after)


def _mix_fwd(oa, ob, w_out, x, g_post, S):
    TS = 512

    def body(oa_ref, ob_ref, w_ref, x_ref, g_ref, mix_ref, x1_ref):
        mix = _dot(oa_ref[...].astype(bf16), w_ref[0:GDN_W, :]) + _dot(ob_ref[...].astype(bf16), w_ref[GDN_W:D_MODEL, :])
        r = lax.rsqrt(jnp.mean(mix * mix, axis=-1, keepdims=True) + RMS_EPS)
        mix_ref[...] = mix
        x1_ref[...] = x_ref[...] + mix * r * g_ref[...]

    row = lambda w: pl.BlockSpec((TS, w), lambda i: (i, 0))
    return pl.pallas_call(
        body, name="mix_fwd", grid=(S // TS,),
        in_specs=[row(GDN_W), row(SWA_W), _resident_spec((D_MODEL, D_MODEL)), row(D_MODEL), _const_spec((1, D_MODEL))],
        out_specs=[row(D_MODEL), row(D_MODEL)],
        out_shape=[SDS((S, D_MODEL), f32), SDS((S, D_MODEL), f32)],
        compiler_params=_cp(1, VMEM_LIMIT),
    )(oa, ob, w_out, x, g_post)


def _mix_bwd(dx1, mix, g_post, w_out, ob, S):
    TS = 512

    def body(dx1_ref, mix_ref, g_ref, w_ref, ob_ref, dmix_ref, doab_ref, dg_ref, nd_ref):
        @pl.when(pl.program_id(0) == 0)
        def _():
            dg_ref[...] = jnp.zeros_like(dg_ref)

        mix = mix_ref[...]
        dz = dx1_ref[...]
        r = lax.rsqrt(jnp.mean(mix * mix, axis=-1, keepdims=True) + RMS_EPS)
        n = mix * r
        dg_ref[...] += jnp.sum(dz * n, axis=0, keepdims=True)
        dn = dz * g_ref[...]
        dmix = (r * (dn - n * jnp.mean(dn * n, axis=-1, keepdims=True))).astype(bf16)
        dmix_ref[...] = dmix
        doab = _dot_nt(dmix, w_ref[...])
        doab_ref[...] = doab
        hi_ = lax.shift_right_logical(lax.broadcasted_iota(jnp.int32, (SWA_W, SWA_W), 0), 6)
        hj_ = lax.shift_right_logical(lax.broadcasted_iota(jnp.int32, (SWA_W, SWA_W), 1), 6)
        swap = (hi_ == lax.bitwise_xor(hj_, 1)).astype(bf16)
        dlt = doab[:, GDN_W:] * ob_ref[...]
        hi = dlt.astype(bf16)
        nd_ref[...] = (_dot(hi, swap) + _dot((dlt - hi.astype(f32)).astype(bf16), swap)) * (-1.0 / SWA_HD)

    row = lambda w=D_MODEL: pl.BlockSpec((TS, w), lambda i: (i, 0))
    return pl.pallas_call(
        body, name="mix_bwd", grid=(S // TS,),
        in_specs=[row(), row(), _const_spec((1, D_MODEL)), _resident_spec((D_MODEL, D_MODEL)), row(SWA_W)],
        out_specs=[row(), row(), _const_spec((1, D_MODEL)), row(SWA_W)],
        out_shape=[SDS((S, D_MODEL), bf16), SDS((S, D_MODEL), f32), SDS((1, D_MODEL), f32), SDS((S, SWA_W), f32)],
        compiler_params=_cp(1, VMEM_LIMIT),
    )(dx1, mix, g_post, w_out, ob)


FFN_TS = 256
FFN_CH = 1408


def _ffn(x1, tgt, g_pre, g_post, wg, wu, wd, S):
    def body(x1_ref, t_ref, gp_ref, gq_ref, wg_ref, wu_ref, wd_ref,
             dx1_ref, h2_ref, act_ref, dgate_ref, dup_ref, df_ref, loss_ref, dgp_ref, dgq_ref, gate_scr, up_scr):
        @pl.when(pl.program_id(0) == 0)
        def _():
            loss_ref[...] = jnp.zeros_like(loss_ref)
            dgp_ref[...] = jnp.zeros_like(dgp_ref)
            dgq_ref[...] = jnp.zeros_like(dgq_ref)

        x1v = x1_ref[...]
        gp = gp_ref[...]
        gq = gq_ref[...]
        r2 = lax.rsqrt(jnp.mean(x1v * x1v, axis=-1, keepdims=True) + RMS_EPS)
        n2 = x1v * r2
        h2 = (n2 * gp).astype(bf16)
        h2_ref[...] = h2
        chunks = [slice(c * FFN_CH, (c + 1) * FFN_CH) for c in range(D_FF // FFN_CH)]
        for cs in chunks:
            gate_scr[:, cs] = _dot_nt(h2, wg_ref[cs, :])
            up_scr[:, cs] = _dot_nt(h2, wu_ref[cs, :])
        acts = []
        for cs in chunks:
            gate = gate_scr[:, cs]
            act = (gate * _sigmoid(gate) * up_scr[:, cs]).astype(bf16)
            act_ref[:, cs] = act
            acts.append(act)
        f = _dot(acts[0], wd_ref[chunks[0], :])
        for act, cs in zip(acts[1:], chunks[1:]):
            f = f + _dot(act, wd_ref[cs, :])
        r3 = lax.rsqrt(jnp.mean(f * f, axis=-1, keepdims=True) + RMS_EPS)
        n3 = f * r3
        err = x1v + n3 * gq - t_ref[...]
        loss_ref[...] += 0.5 * jnp.sum(jnp.mean(err * err, axis=-1, keepdims=True), axis=0, keepdims=True)
        dy = err * (1.0 / D_MODEL)
        dgq_ref[...] += jnp.sum(dy * n3, axis=0, keepdims=True)
        dn3 = dy * gq
        df = (r3 * (dn3 - n3 * jnp.mean(dn3 * n3, axis=-1, keepdims=True))).astype(bf16)
        df_ref[...] = df
        dacts = [_dot_nt(df, wd_ref[cs, :]) for cs in chunks]
        dgs = []
        for dact, cs in zip(dacts, chunks):
            gate = gate_scr[:, cs]
            sg = _sigmoid(gate)
            dup = (dact * gate * sg).astype(bf16)
            dgate = (dact * up_scr[:, cs] * (sg * (1.0 + gate * (1.0 - sg)))).astype(bf16)
            dup_ref[:, cs] = dup
            dgate_ref[:, cs] = dgate
            dgs.append((dgate, dup))
        dh2 = None
        for (dgate, dup), cs in zip(dgs, chunks):
            t = _dot(dgate, wg_ref[cs, :]) + _dot(dup, wu_ref[cs, :])
            dh2 = t if dh2 is None else dh2 + t
        dgp_ref[...] += jnp.sum(dh2 * n2, axis=0, keepdims=True)
        dn2 = dh2 * gp
        dx1_ref[...] = dy + r2 * (dn2 - n2 * jnp.mean(dn2 * n2, axis=-1, keepdims=True))

    row = lambda w: pl.BlockSpec((FFN_TS, w), lambda i: (i, 0))
    vec = _const_spec((1, D_MODEL))
    return pl.pallas_call(
        body, name="ffn_fwd_bwd", grid=(S // FFN_TS,),
        in_specs=[row(D_MODEL), row(D_MODEL), vec, vec, _resident_spec((D_FF, D_MODEL)), _resident_spec((D_FF, D_MODEL)),
                  _resident_spec((D_FF, D_MODEL))],
        out_specs=[row(D_MODEL), row(D_MODEL), row(D_FF), row(D_FF), row(D_FF), row(D_MODEL), _const_spec((1, LANE)), vec, vec],
        out_shape=[SDS((S, D_MODEL), f32), SDS((S, D_MODEL), bf16), SDS((S, D_FF), bf16), SDS((S, D_FF), bf16),
                   SDS((S, D_FF), bf16), SDS((S, D_MODEL), bf16), SDS((1, LANE), f32), SDS((1, D_MODEL), f32),
                   SDS((1, D_MODEL), f32)],
        scratch_shapes=[pltpu.VMEM((FFN_TS, D_FF), f32), pltpu.VMEM((FFN_TS, D_FF), f32)],
        compiler_params=_cp(1, VMEM_LIMIT),
    )(x1, tgt, g_pre, g_post, wg, wu, wd)


def _proj_bwd(x, dx1, g_pre, wcat, segs, after, S):
    TS = 512
    n = len(segs)
    cols = [(c0, a.shape[1]) for a, c0 in segs]

    def body(*refs):
        x_ref, dx1_ref, g_ref, w_ref = refs[:4]
        seg_refs = refs[4:4 + n]
        gx_ref, dg_ref = refs[5 + n:]

        @pl.when(pl.program_id(0) == 0)
        def _():
            dg_ref[...] = jnp.zeros_like(dg_ref)

        dh = jnp.zeros((TS, D_MODEL), f32)
        for s_ref, (c0, w) in zip(seg_refs, cols):
            dh = dh + _dot_nt(s_ref[...], w_ref[:, c0:c0 + w])
        xv = x_ref[...]
        g = g_ref[...]
        r = lax.rsqrt(jnp.mean(xv * xv, axis=-1, keepdims=True) + RMS_EPS)
        nx = xv * r
        dg_ref[...] += jnp.sum(dh * nx, axis=0, keepdims=True)
        dn = dh * g
        gx_ref[...] = dx1_ref[...] + r * (dn - nx * jnp.mean(dn * nx, axis=-1, keepdims=True))

    row = lambda w: pl.BlockSpec((TS, w), lambda i: (i, 0))
    return pl.pallas_call(
        body, name="proj_bwd", grid=(S // TS,),
        in_specs=[row(D_MODEL), row(D_MODEL), _const_spec((1, D_MODEL)), _resident_spec((D_MODEL, NCOL))]
                 + [row(w) for _, w in cols] + [_ANY],
        out_specs=[row(D_MODEL), _const_spec((1, D_MODEL))],
        out_shape=[SDS((S, D_MODEL), f32), SDS((1, D_MODEL), f32)],
        compiler_params=_cp(1, VMEM_LIMIT),
    )(x, dx1, g_pre, wcat, *[a for a, _ in segs], after)


def _wgrad(a, b, S, name):
    TS = 1024
    K = a.shape[1]
    N = b.shape[1]
    TN = next(t for t in (512, 1408, N) if N % t == 0)

    def body(a_ref, b_ref, o_ref, acc):
        @pl.when(pl.program_id(1) == 0)
        def _():
            acc[...] = jnp.zeros_like(acc)

        acc[...] += _dot_tn(a_ref[...].astype(bf16), b_ref[...])

        @pl.when(pl.program_id(1) == pl.num_programs(1) - 1)
        def _():
            o_ref[...] = acc[...].astype(bf16)

    return pl.pallas_call(
        body, name=name, grid=(N // TN, S // TS),
        in_specs=[pl.BlockSpec((TS, K), lambda j, s: (s, 0)), pl.BlockSpec((TS, TN), lambda j, s: (s, j))],
        out_specs=pl.BlockSpec((K, TN), lambda j, s: (0, j)), out_shape=SDS((K, N), bf16),
        scratch_shapes=[pltpu.VMEM((K, TN), f32)],
        compiler_params=_cp(2, VMEM_LIMIT),
    )(a, b)


def _wgrad_out(oa, ob, dmix, S):
    TS, TN = 1024, 512

    def body(a_ref, b_ref, d_ref, o_ref, acc):
        @pl.when(pl.program_id(1) == 0)
        def _():
            acc[...] = jnp.zeros_like(acc)

        d = d_ref[...]
        acc[0:GDN_W, :] += _dot_tn(a_ref[...].astype(bf16), d)
        acc[GDN_W:D_MODEL, :] += _dot_tn(b_ref[...].astype(bf16), d)

        @pl.when(pl.program_id(1) == pl.num_programs(1) - 1)
        def _():
            o_ref[...] = acc[...].astype(bf16)

    tok = lambda w: pl.BlockSpec((TS, w), lambda j, s: (s, 0))
    return pl.pallas_call(
        body, name="wgrad_out", grid=(D_MODEL // TN, S // TS),
        in_specs=[tok(GDN_W), tok(SWA_W), pl.BlockSpec((TS, TN), lambda j, s: (s, j))],
        out_specs=pl.BlockSpec((D_MODEL, TN), lambda j, s: (0, j)), out_shape=SDS((D_MODEL, D_MODEL), bf16),
        scratch_shapes=[pltpu.VMEM((D_MODEL, TN), f32)],
        compiler_params=_cp(2, VMEM_LIMIT),
    )(oa, ob, dmix)


def _w_in_pieces():
    n_a, n_g = 4 * GDN_W, 2 * GDN_HEADS
    cb = IN_COLS // N_DEV
    bounds = [(0, n_a, COL_A), (n_a, n_a + n_g, COL_G), (n_a + n_g, IN_COLS, COL_B)]
    out = []
    for j in range(N_DEV):
        lo, hi = j * cb, (j + 1) * cb
        for s0, s1, dst in bounds:
            a, b = max(lo, s0), min(hi, s1)
            if a < b:
                out.append((j, a - lo, b - a, dst + a - s0))
    return out


def _wcat_from_blocks(g_in):
    TR = 256
    cb = IN_COLS // N_DEV
    pieces = _w_in_pieces()

    def body(w_ref, o_ref):
        o_ref[:, COL_G:NCOL] = jnp.zeros((TR, NCOL - COL_G), bf16)
        for j, off, w, dst in pieces:
            o_ref[:, dst:dst + w] = w_ref[j, :, off:off + w]

    return pl.pallas_call(
        body, name="wcat_from_blocks", grid=(D_MODEL // TR,),
        in_specs=[pl.BlockSpec((N_DEV, TR, cb), lambda i: (0, i, 0))],
        out_specs=pl.BlockSpec((TR, NCOL), lambda i: (i, 0)),
        out_shape=SDS((D_MODEL, NCOL), bf16),
        compiler_params=_cp(1, VMEM_LIMIT),
    )(g_in)


def _wgrad_in(h1, segs, S):
    TS = 1024
    n = len(segs)
    cols = [(c0, a.shape[1]) for a, c0 in segs]
    cb = IN_COLS // N_DEV
    pieces = _w_in_pieces()

    def body(*refs):
        h_ref = refs[0]
        seg_refs = refs[1:1 + n]
        o_ref, acc = refs[1 + n], refs[2 + n]

        @pl.when(pl.program_id(0) == 0)
        def _():
            acc[...] = jnp.zeros_like(acc)

        h = h_ref[...]
        for s_ref, (c0, w) in zip(seg_refs, cols):
            acc[:, c0:c0 + w] += _dot_tn(h, s_ref[...])

        @pl.when(pl.program_id(0) == pl.num_programs(0) - 1)
        def _():
            for j, off, w, src in pieces:
                o_ref[j, :, off:off + w] = acc[:, src:src + w].astype(bf16)

    row = lambda w: pl.BlockSpec((TS, w), lambda i: (i, 0))
    return pl.pallas_call(
        body, name="wgrad_in", grid=(S // TS,),
        in_specs=[row(D_MODEL)] + [row(w) for _, w in cols],
        out_specs=_const_spec((N_DEV, D_MODEL, cb)),
        out_shape=SDS((N_DEV, D_MODEL, cb), bf16),
        scratch_shapes=[pltpu.VMEM((D_MODEL, NCOL), f32)],
        compiler_params=_cp(1, VMEM_LIMIT),
    )(h1, *[a for a, _ in segs])


def _adamw(recv, src, me, w, m, v, name):
    R, C = w.shape
    TR = 256 if R % 256 == 0 else R
    c1 = 1.0 / (1.0 - ADAM_B1 ** ADAM_STEP)
    c2 = 1.0 / (1.0 - ADAM_B2 ** ADAM_STEP)

    def body(me_ref, r_ref, own_ref, w_ref, m_ref, v_ref, g_out, d_out, m_out, v_out):
        g = None
        for s in range(N_DEV):
            t = jnp.where(me_ref[0] == s, own_ref[0], r_ref[s]).astype(f32)
            g = t if g is None else g + t
        mn = ADAM_B1 * m_ref[...] + (1.0 - ADAM_B1) * g
        vn = ADAM_B2 * v_ref[...] + (1.0 - ADAM_B2) * (g * g)
        g_out[...] = g
        m_out[...] = mn
        v_out[...] = vn
        d_out[...] = -ADAM_LR * ((mn * c1) / (jnp.sqrt(vn * c2) + ADAM_EPS) + ADAM_WD * w_ref[...])

    blk = pl.BlockSpec((TR, C), lambda i, me_ref: (i, 0))
    return pl.pallas_call(
        body, name=name,
        grid_spec=pltpu.PrefetchScalarGridSpec(
            num_scalar_prefetch=1, grid=(R // TR,),
            in_specs=[pl.BlockSpec((N_DEV, TR, C), lambda i, me_ref: (0, i, 0)),
                      pl.BlockSpec((1, TR, C), lambda i, me_ref: (me_ref[0], i, 0)), blk, blk, blk],
            out_specs=[blk, blk, blk, blk]),
        out_shape=[SDS((R, C), f32)] * 4,
        compiler_params=_cp(1, VMEM_LIMIT),
    )(me, recv, src, w, m, v)


MESH = pl.DeviceIdType.MESH
_ANY = pl.BlockSpec(memory_space=pl.ANY)


def _flip(v, d):
    return 1 - v if d else v


def _all_gather(shards, rel_bias, bk):
    n = len(shards)

    def body(*refs):
        ins = refs[:n]
        rb_ref, bk_ref = refs[n], refs[n + 1]
        outs = refs[n + 2:2 * n + 2]
        bt_ref = refs[2 * n + 2]
        send_sems, recv_sems, local_sems = refs[2 * n + 3:]
        x, y, c = lax.axis_index("x"), lax.axis_index("y"), lax.axis_index("c")
        me, sibling = (x, y, c), (x, y, 1 - c)
        chips = [(1 - x, y), (x, 1 - y), (1 - x, 1 - y)]

        def slot(px, py, pc):
            return 4 * px + 2 * py + pc

        def copy(a, k, block, to, src=None):
            dst = outs[a].at[slot(*block)]
            return pltpu.make_async_remote_copy(src_ref=dst if src is None else src, dst_ref=dst,
                                                send_sem=send_sems.at[a, k], recv_sem=recv_sems.at[a, k],
                                                device_id=to, device_id_type=MESH)

        mine, first, passed = [], [], []
        for a in range(n):
            cp = pltpu.make_async_copy(ins[a], outs[a].at[slot(*me)], local_sems.at[a])
            cp.start()
            mine.append(cp)
            fs = [copy(a, 0, me, sibling, src=ins[a])]
            fs += [copy(a, 1 + j, me, (*chip, c), src=ins[a]) for j, chip in enumerate(chips)]
            for cp in fs:
                cp.start()
            first += fs
        _fill_bias_tables(rb_ref, bk_ref, bt_ref)
        for j, chip in enumerate(chips):
            for a in range(n):
                copy(a, 1 + j, (*chip, c), me).wait_recv()
                cp = copy(a, 4 + j, (*chip, c), sibling)
                cp.start()
                passed.append(cp)
        for a in range(n):
            copy(a, 0, sibling, me).wait_recv()
            for j, chip in enumerate(chips):
                copy(a, 4 + j, (*chip, 1 - c), me).wait_recv()
        for cp in first + passed:
            cp.wait_send()
        for cp in mine:
            cp.wait()

    vmem = pl.BlockSpec(memory_space=pltpu.VMEM)
    return pl.pallas_call(
        body, name="weight_all_gather",
        in_specs=[_ANY] * n + [pl.BlockSpec(memory_space=pltpu.SMEM), vmem], out_specs=[_ANY] * n + [vmem],
        out_shape=[SDS((N_DEV,) + s.shape, s.dtype) for s in shards] + [SDS((len(PATTERNS), SWA_HEADS, SWA_BLK, SWA_BLK), f32)],
        scratch_shapes=[pltpu.SemaphoreType.DMA((n, 7)), pltpu.SemaphoreType.DMA((n, 7)), pltpu.SemaphoreType.DMA((n,))],
        compiler_params=pltpu.CompilerParams(has_side_effects=True),
    )(*shards, rel_bias, bk)


def _grad_exchange(blocked, whole):
    arrs = list(blocked) + list(whole)
    n, nb = len(arrs), len(blocked)
    rel = [(dx, dy, dc) for dx in (0, 1) for dy in (0, 1) for dc in (0, 1) if dx or dy or dc]

    def body(*refs):
        ins = refs[:n]
        outs = refs[n:2 * n]
        send_sems, recv_sems, local_sems = refs[2 * n:]
        x, y, c = lax.axis_index("x"), lax.axis_index("y"), lax.axis_index("c")
        me = 4 * x + 2 * y + c
        sends, locs = [], []
        for a in range(n):
            cp = pltpu.make_async_copy(ins[a].at[me] if a < nb else ins[a], outs[a].at[me], local_sems.at[a])
            cp.start()
            locs.append(cp)
            for k, (dx, dy, dc) in enumerate(rel):
                peer = (_flip(x, dx), _flip(y, dy), _flip(c, dc))
                pidx = 4 * peer[0] + 2 * peer[1] + peer[2]
                cp = pltpu.make_async_remote_copy(src_ref=ins[a].at[pidx] if a < nb else ins[a], dst_ref=outs[a].at[me],
                                                  send_sem=send_sems.at[a, k], recv_sem=recv_sems.at[a, k],
                                                  device_id=peer, device_id_type=MESH)
                cp.start()
                sends.append(cp)
        for a in range(n):
            for k, (dx, dy, dc) in enumerate(rel):
                peer = (_flip(x, dx), _flip(y, dy), _flip(c, dc))
                pidx = 4 * peer[0] + 2 * peer[1] + peer[2]
                pltpu.make_async_remote_copy(src_ref=outs[a].at[pidx], dst_ref=outs[a].at[pidx],
                                             send_sem=send_sems.at[a, k], recv_sem=recv_sems.at[a, k],
                                             device_id=peer, device_id_type=MESH).wait_recv()
        for cp in sends:
            cp.wait_send()
        for cp in locs:
            cp.wait()

    shapes = [SDS(a.shape, a.dtype) for a in blocked] + [SDS((N_DEV,) + a.shape, a.dtype) for a in whole]
    return pl.pallas_call(
        body, name="grad_exchange",
        in_specs=[_ANY] * n, out_specs=[_ANY] * n, out_shape=shapes,
        scratch_shapes=[pltpu.SemaphoreType.DMA((n, 7)), pltpu.SemaphoreType.DMA((n, 7)), pltpu.SemaphoreType.DMA((n,))],
        compiler_params=pltpu.CompilerParams(has_side_effects=True),
    )(*arrs)


_HBM = pl.BlockSpec(memory_space=pltpu.HBM)
_SEM = pl.BlockSpec(memory_space=pltpu.SEMAPHORE)
_REL = [(dx, dy, dc) for dx in (0, 1) for dy in (0, 1) for dc in (0, 1) if dx or dy or dc]


N_PEER = len(_REL)
_EFFECT = pltpu.SideEffectType.DATAFLOW_SIDE_EFFECTING


def _peer_copies(srcs, lands, send_sems, recv_sems, blocked, as_receiver):
    x, y, c = lax.axis_index("x"), lax.axis_index("y"), lax.axis_index("c")
    me = 4 * x + 2 * y + c
    cps = []
    for a in range(len(srcs)):
        for k, (dx, dy, dc) in enumerate(_REL):
            peer = (_flip(x, dx), _flip(y, dy), _flip(c, dc))
            pidx = 4 * peer[0] + 2 * peer[1] + peer[2]
            cps.append(pltpu.make_async_remote_copy(
                src_ref=srcs[a].at[pidx] if blocked else srcs[a], dst_ref=lands[a].at[pidx if as_receiver else me],
                send_sem=send_sems[a * N_PEER + k], recv_sem=recv_sems[a * N_PEER + k],
                device_id=peer, device_id_type=MESH))
    return cps


def _exchange_start(srcs, after, blocked, name):
    n = len(srcs)
    ns = n * N_PEER
    lands = [lax.empty(s.shape if blocked else (N_DEV,) + s.shape, s.dtype) for s in srcs]

    def body(*refs):
        ins, lnd = refs[:n], refs[n:2 * n]
        outs = refs[2 * n + 1:]
        for cp in _peer_copies(ins, lnd, outs[:ns], outs[ns:2 * ns], blocked, False):
            cp.start()
        outs[-1][...] = jnp.zeros_like(outs[-1])

    res = pl.pallas_call(
        body, name=name,
        in_specs=[_HBM] * (2 * n) + [_ANY],
        out_specs=[_SEM] * (2 * ns) + [_HBM] * (2 * n) + [pl.BlockSpec(memory_space=pltpu.VMEM)],
        out_shape=[pltpu.SemaphoreType.DMA(())] * (2 * ns) + [pltpu.HBM(s.shape, s.dtype) for s in srcs]
                  + [pltpu.HBM(l.shape, l.dtype) for l in lands] + [SDS((8, LANE), f32)],
        input_output_aliases={i: 2 * ns + i for i in range(2 * n)},
        compiler_params=pltpu.CompilerParams(has_side_effects=_EFFECT),
    )(*[pltpu.with_memory_space_constraint(s, pltpu.HBM) for s in srcs],
      *[pltpu.with_memory_space_constraint(l, pltpu.HBM) for l in lands], after)
    return list(res[:2 * ns]), list(res[2 * ns:2 * ns + n]), list(res[2 * ns + n:2 * ns + 2 * n]), res[-1]


def _exchange_wait(sems, srcs, lands, after, blocked, name):
    n = len(srcs)
    ns = n * N_PEER

    def body(*refs):
        ins, lnd = refs[:n], refs[n:2 * n]
        sem_refs = refs[2 * n:2 * n + 2 * ns]
        for cp in _peer_copies(ins, lnd, sem_refs[:ns], sem_refs[ns:], blocked, True):
            cp.wait_send()
            cp.wait_recv()

    res = pl.pallas_call(
        body, name=name,
        in_specs=[_HBM] * (2 * n) + [_SEM] * (2 * ns) + [_ANY],
        out_specs=[_HBM] * (2 * n),
        out_shape=[pltpu.HBM(s.shape, s.dtype) for s in srcs] + [pltpu.HBM(l.shape, l.dtype) for l in lands],
        input_output_aliases={i: i for i in range(2 * n)},
        compiler_params=pltpu.CompilerParams(has_side_effects=_EFFECT),
    )(*srcs, *lands, *sems, after)
    return list(res[:n]), list(res[n:])


def _local_step(x, tgt, wcat, convw, bt, late_weights, early_grads, last_grads, token, a_log, dt_bias, onorm_g,
                g_mix_pre, g_mix_post, g_ffn_pre, g_ffn_post):
    S = x.shape[0]
    bk_np = _bucket_tables()
    bk = jnp.asarray(bk_np)
    proj, h1 = _proj_fwd(x, g_mix_pre, wcat, token, S)
    nu = S // CHUNK * GDN_HEADS
    qkv_u = _gdn_prep(proj, convw, S).reshape(3, nu, CHUNK, GDN_HD)
    intra, t_inv = _gdn_intra_fwd(qkv_u, proj, a_log, dt_bias, S)
    oa, states = _gdn_scan_fwd(intra, proj, onorm_g, S)
    ob, lse0, lse1 = _swa_fwd(proj, bt, S)
    wout, ffn_weights = late_weights(ob)
    mix, x1 = _mix_fwd(oa, ob, wout, x, g_mix_post, S)
    wgate, wup, wdown = ffn_weights(x1)
    dx1, h2, act, dgate_f, dup_f, df, loss, d_gfpre, d_gfpost = _ffn(x1, tgt, g_ffn_pre, g_ffn_post, wgate, wup, wdown, S)
    rows8 = lambda g: g.reshape(N_DEV, D_FF // N_DEV, D_MODEL)
    g_gate = rows8(_wgrad(dgate_f, h2, S, "wgrad_gate"))
    g_up = rows8(_wgrad(dup_f, h2, S, "wgrad_up"))
    g_down = rows8(_wgrad(act, df, S, "wgrad_down"))
    dmix, d_oab, d_gmpost, nd = _mix_bwd(dx1, mix, g_mix_post, wout, ob, S)
    g_out = _wgrad_out(oa, ob, dmix, S)
    token = early_grads(g_out.reshape(N_DEV, D_MODEL // N_DEV, D_MODEL), g_gate, g_up, g_down)
    dqb, dkb, dvb, dsb = _swa_bwd(proj, bt, nd, lse0, lse1, d_oab, token, S)
    *cots, dgate_a, d_og = _gdn_scan_bwd(intra, states, proj, d_oab, onorm_g, token, S)
    dqkv_u, dpg, d_alog, d_dtb = _gdn_intra_bwd(qkv_u, proj, a_log, dt_bias, t_inv, cots, S)
    dqkv_a, d_conv = _gdn_prep_bwd(proj, convw, dqkv_u.reshape(3, S // CHUNK, GDN_HEADS, CHUNK, GDN_HD), S)
    segs = [(dqkv_a, COL_A), (dgate_a, COL_A + 3 * GDN_W), (dqb, COL_B), (dkb, COL_B + SWA_W), (dvb, COL_B + 2 * SWA_W),
            (dpg, COL_G)]
    token = last_grads(_wgrad_in(h1, segs, S), d_conv)
    grad_x, d_gmpre = _proj_bwd(x, dx1, g_mix_pre, wcat, segs, token, S)
    d_rel = _rel_bias_grad(dsb, bk, bk_np)
    small = dict(a_log=d_alog[:, GDN_HEADS:2 * GDN_HEADS], dt_bias=d_dtb[:, GDN_HEADS:2 * GDN_HEADS], onorm_g=d_og, rel_bias=d_rel,
                 g_mix_pre=d_gmpre, g_mix_post=d_gmpost, g_ffn_pre=d_gfpre, g_ffn_post=d_gfpost)
    return loss, grad_x, small


SMALL = ("a_log", "dt_bias", "onorm_g", "rel_bias", "g_mix_pre", "g_mix_post", "g_ffn_pre", "g_ffn_post")
PACK_ROWS = 8


def _pack_small(d, loss=None):
    rest = jnp.concatenate([d["onorm_g"].reshape(-1), d["a_log"].reshape(-1), d["dt_bias"].reshape(-1),
                            d["rel_bias"].reshape(-1)])
    rest = jnp.concatenate([rest, jnp.zeros((D_MODEL - rest.shape[0],), f32)])
    extra = jnp.zeros((D_MODEL,), f32) if loss is None else jnp.concatenate([loss.reshape(1), jnp.zeros((D_MODEL - 1,), f32)])
    rows = [d["g_mix_pre"].reshape(-1), d["g_mix_post"].reshape(-1), d["g_ffn_pre"].reshape(-1),
            d["g_ffn_post"].reshape(-1), rest, extra]
    return jnp.concatenate([jnp.stack(rows), jnp.zeros((PACK_ROWS - len(rows), D_MODEL), f32)], axis=0)


def _unpack_small(p):
    o = GDN_HD
    return dict(g_mix_pre=p[0:1], g_mix_post=p[1:2], g_ffn_pre=p[2:3], g_ffn_post=p[3:4],
                onorm_g=p[4:5, :o], a_log=p[4:5, o:o + 4], dt_bias=p[4:5, o + 4:o + 8],
                rel_bias=p[4, o + 8:o + 8 + NUM_BUCKETS * SWA_HEADS].reshape(NUM_BUCKETS, SWA_HEADS))


def kernel(x, w_in, conv_w, a_log, dt_bias, onorm_g, rel_bias, w_out, g_mix_pre, g_mix_post, w_gate, w_up, w_down, g_ffn_pre, g_ffn_post, loss_target, m_w_in, m_conv_w, m_a_log, m_dt_bias, m_onorm_g, m_rel_bias, m_w_out, m_g_mix_pre, m_g_mix_post, m_w_gate, m_w_up, m_w_down, m_g_ffn_pre, m_g_ffn_post, v_w_in, v_conv_w, v_a_log, v_dt_bias, v_onorm_g, v_rel_bias, v_w_out, v_g_mix_pre, v_g_mix_post, v_w_gate, v_w_up, v_w_down, v_g_ffn_pre, v_g_ffn_post):
    big = ("w_in", "conv_w", "w_out", "w_gate", "w_up", "w_down")
    transposed = ("w_gate", "w_up")
    tr = lambda k, a: a.T if k in transposed else a
    w_sh = {k: tr(k, a[0]) for k, a in dict(w_in=w_in, conv_w=conv_w, w_out=w_out, w_gate=w_gate, w_up=w_up, w_down=w_down).items()}
    m_sh = {k: tr(k, a[0]) for k, a in dict(w_in=m_w_in, conv_w=m_conv_w, w_out=m_w_out, w_gate=m_w_gate, w_up=m_w_up,
                                             w_down=m_w_down).items()}
    v_sh = {k: tr(k, a[0]) for k, a in dict(w_in=v_w_in, conv_w=v_conv_w, w_out=v_w_out, w_gate=v_w_gate, w_up=v_w_up,
                                             w_down=v_w_down).items()}
    w_small = dict(a_log=a_log, dt_bias=dt_bias, onorm_g=onorm_g, rel_bias=rel_bias, g_mix_pre=g_mix_pre,
                   g_mix_post=g_mix_post, g_ffn_pre=g_ffn_pre, g_ffn_post=g_ffn_post)
    m_small = dict(a_log=m_a_log, dt_bias=m_dt_bias, onorm_g=m_onorm_g, rel_bias=m_rel_bias, g_mix_pre=m_g_mix_pre,
                   g_mix_post=m_g_mix_post, g_ffn_pre=m_g_ffn_pre, g_ffn_post=m_g_ffn_post)
    v_small = dict(a_log=v_a_log, dt_bias=v_dt_bias, onorm_g=v_onorm_g, rel_bias=v_rel_bias, g_mix_pre=v_g_mix_pre,
                   g_mix_post=v_g_mix_post, g_ffn_pre=v_g_ffn_pre, g_ffn_post=v_g_ffn_post)

    me = 4 * lax.axis_index("x") + 2 * lax.axis_index("y") + lax.axis_index("c")
    me1 = me.reshape(1).astype(jnp.int32)
    own = lambda full, part: lax.dynamic_update_index_in_dim(full, part, me, 0)
    cols = lambda g: g.reshape(g.shape[0], N_DEV, g.shape[1] // N_DEV).transpose(1, 0, 2)
    late = ("w_out", "w_gate", "w_up", "w_down")

    late_src = [w_sh[k].astype(bf16) for k in late]
    g_in, g_conv, bt = _all_gather([w_sh["w_in"].astype(bf16), w_sh["conv_w"]], rel_bias, jnp.asarray(_bucket_tables()))
    g_sems, g_src, g_land, g_token = _exchange_start(late_src, g_conv, False, "late_weights_start")
    wcat = _wcat_from_blocks(g_in)
    convw = g_conv.transpose(1, 0, 2).reshape(4, 3 * GDN_W)

    def late_weights(after):
        pick = lambda idx: [g_sems[half * len(late) * N_PEER + a * N_PEER + k] for half in (0, 1) for a in idx for k in range(N_PEER)]
        (s_out,), (l_out,) = _exchange_wait(pick([0]), g_src[:1], g_land[:1], after, False, "w_out_wait")

        def ffn_weights(after2):
            srcs, lands = _exchange_wait(pick([1, 2, 3]), g_src[1:], g_land[1:], after2, False, "ffn_weights_wait")
            return [own(l, s).reshape(D_FF, D_MODEL) for l, s in zip(lands, srcs)]

        return own(l_out, s_out).reshape(D_MODEL, D_MODEL), ffn_weights

    early, last = {}, {}

    def early_grads(*blocks):
        early["sems"], early["src"], early["land"], token = _exchange_start(list(blocks), me1, True, "late_grads_start")
        return token

    def last_grads(gw_in, gw_conv):
        src = [gw_in, cols(gw_conv)]
        last["sems"], last["src"], last["land"], token = _exchange_start(src, me1, True, "last_grads_start")
        return token

    loss_p, grad_x, gsmall = _local_step(
        x[0], loss_target[0], wcat, convw, bt, late_weights, early_grads, last_grads, g_token,
        a_log, dt_bias, onorm_g, g_mix_pre, g_mix_post, g_ffn_pre, g_ffn_post)

    (r_small,) = _grad_exchange([], [_pack_small(gsmall, loss_p[0, 0])])
    outs = {}
    for names, ex, after, name in ((late, early, grad_x, "late_grads_wait"), (("w_in", "conv_w"), last, r_small, "last_grads_wait")):
        srcs, lands = _exchange_wait(ex["sems"], ex["src"], ex["land"], after, True, name)
        for k, l, s in zip(names, lands, srcs):
            outs[k] = _adamw(l, s, me1, w_sh[k], m_sh[k], v_sh[k], "adamw_" + k)
    sm = _adamw(r_small, r_small, me1, _pack_small(w_small), _pack_small(m_small), _pack_small(v_small), "adamw_small")
    loss = sm[0][5, 0]
    sm = [_unpack_small(t) for t in sm]
    for k in SMALL:
        outs[k] = tuple(t[k].reshape(w_small[k].shape) for t in sm)

    order = ("w_in", "conv_w", "a_log", "dt_bias", "onorm_g", "rel_bias", "w_out", "g_mix_pre", "g_mix_post", "w_gate",
             "w_up", "w_down", "g_ffn_pre", "g_ffn_post")
    lead = lambda k, t: tr(k, t)[None] if k in big else t
    res = [loss, grad_x[None]]
    for i in range(4):
        res += [lead(k, outs[k][i]) for k in order]
    return tuple(res)
```

```python
import functools
import math

import numpy as np
import jax
import jax.numpy as jnp
from jax import lax
from jax.experimental import pallas as pl
from jax.experimental.pallas import tpu as pltpu

f32 = jnp.float32
bf16 = jnp.bfloat16
SDS = jax.ShapeDtypeStruct

D_MODEL = 1024
GDN_HEADS = 4
GDN_HD = 128
GDN_W = 512
CHUNK = 64
SWA_HEADS = 8
SWA_HD = 64
SWA_W = 512
D_FF = 2816
IN_COLS = 3592
PATTERNS = ((128, 1), (512, 4), (2048, 16))
SWA_BLK = 128
NUM_BUCKETS = 32
MAX_DISTANCE = 2048
RMS_EPS = 1e-6
NEG = -1e30
N_DEV = 8

COL_A = 0
COL_B = 2048
COL_G = 3584
NCOL = 3712
LANE = 128

ADAM_LR, ADAM_B1, ADAM_B2, ADAM_EPS, ADAM_WD, ADAM_STEP = 0.001, 0.9, 0.999, 1e-08, 0.01, 10

VMEM_LIMIT = 56 * 1024 * 1024

HIGH = lax.Precision.HIGH


def _cp(n_grid=0, vmem=None):
    kw = {}
    if n_grid:
        kw["dimension_semantics"] = ("arbitrary",) * n_grid
    if vmem:
        kw["vmem_limit_bytes"] = vmem
    return pltpu.CompilerParams(**kw)


def _dot(a, b):
    return jnp.dot(a, b, preferred_element_type=f32)


def _dot_nt(a, b):
    return lax.dot_general(a, b, (((1,), (1,)), ((), ())), preferred_element_type=f32)


def _dot_tn(a, b):
    return lax.dot_general(a, b, (((0,), (0,)), ((), ())), preferred_element_type=f32)


def _sigmoid(x):
    return 0.5 * jnp.tanh(0.5 * x) + 0.5


def _softplus(x):
    return jnp.maximum(x, 0.0) + jnp.log(1.0 + jnp.exp(-jnp.abs(x)))


def _const_spec(shape):
    nd = len(shape)
    return pl.BlockSpec(shape, lambda *_: (0,) * nd)


def _resident_spec(shape):
    nd = len(shape)
    return pl.BlockSpec(shape, lambda *_: (0,) * nd, pipeline_mode=pl.Buffered(1))


def _t5_bucket_np(dist):
    max_exact = NUM_BUCKETS // 2
    d = np.maximum(dist, 1).astype(np.float32)
    log_b = max_exact + (np.log(d / np.float32(max_exact)) / np.float32(math.log(MAX_DISTANCE / max_exact))
                         * np.float32(NUM_BUCKETS - max_exact)).astype(np.int32)
    return np.where(dist < max_exact, dist, np.minimum(log_b, NUM_BUCKETS - 1)).astype(np.int32)


def _bucket_tables():
    w = SWA_BLK
    qi = np.arange(w)[:, None]
    kj = np.arange(w)[None, :]
    rel = np.where(kj <= qi, qi - kj, qi + w - kj)
    out = np.zeros((len(PATTERNS), w, w), np.int32)
    for p, (_, dil) in enumerate(PATTERNS):
        steps = _t5_bucket_np(np.arange(w + 1) * dil)
        assert steps[w] == steps[w - 1]
        out[p] = steps[rel]
    return out


def _fill_bias_tables(rb_ref, bk_ref, o_ref):
    for p in range(len(PATTERNS)):
        b_idx = bk_ref[p]
        for h in range(SWA_HEADS):
            def lp(b, acc):
                return jnp.where(b_idx == b, rb_ref[b, h], acc)
            o_ref[p, h] = lax.fori_loop(0, NUM_BUCKETS, lp, jnp.zeros((SWA_BLK, SWA_BLK), f32))


def _rel_bias_grad(dsb, bk, bk_np):
    present = [sorted(set(int(v) for v in np.unique(bk_np[p]))) for p in range(3)]

    def body(ds_ref, bk_ref, o_ref):
        row = lax.broadcasted_iota(jnp.int32, (NUM_BUCKETS, LANE), 0)
        col = lax.broadcasted_iota(jnp.int32, (NUM_BUCKETS, SWA_HEADS), 1)
        out = jnp.zeros((NUM_BUCKETS, SWA_HEADS), f32)
        for hp in range(4):
            for hh in range(2):
                acc = jnp.zeros((NUM_BUCKETS, LANE), f32)
                for p in range(3):
                    tile = ds_ref[hp, p, hh]
                    b_idx = bk_ref[p]
                    for b in present[p]:
                        part = jnp.sum(jnp.where(b_idx == b, tile, 0.0), axis=0, keepdims=True)
                        acc = acc + jnp.where(row == b, part, 0.0)
                tot = jnp.sum(acc, axis=1, keepdims=True)
                out = out + jnp.where(col == 2 * hp + hh, tot, 0.0)
        o_ref[...] = out

    return pl.pallas_call(body, name="rel_bias_grad", out_shape=SDS((NUM_BUCKETS, SWA_HEADS), f32),
                          compiler_params=_cp(0, 32 * 1024 * 1024))(dsb, bk)


def _proj_fwd(x, g_pre, wcat, after, S):
    TS = 512

    def body(x_ref, g_ref, w_ref, after_ref, o_ref, h_ref):
        xv = x_ref[...]
        r = lax.rsqrt(jnp.mean(xv * xv, axis=-1, keepdims=True) + RMS_EPS)
        h = (xv * r * g_ref[...]).astype(bf16)
        h_ref[...] = h
        o_ref[...] = _dot(h, w_ref[...])

    return pl.pallas_call(
        body, name="proj_fwd", grid=(S // TS,),
        in_specs=[pl.BlockSpec((TS, D_MODEL), lambda i: (i, 0)), _const_spec((1, D_MODEL)),
                  _resident_spec((D_MODEL, NCOL)), _ANY],
        out_specs=[pl.BlockSpec((TS, NCOL), lambda i: (i, 0)), pl.BlockSpec((TS, D_MODEL), lambda i: (i, 0))],
        out_shape=[SDS((S, NCOL), f32), SDS((S, D_MODEL), bf16)],
        compiler_params=_cp(1, VMEM_LIMIT),
    )(x, g_pre, wcat, after)


CONV_RT = 256
HALO = 8


CONV_NC = CONV_RT // CHUNK


def _gdn_prep(proj, conv_w, S):
    def body(p_ref, cw_ref, o_ref, xs_ref):
        t = pl.program_id(0)
        xs_ref[pl.ds(0, HALO), :] = jnp.zeros((HALO, LANE), f32)
        xs_ref[pl.ds(HALO, S), :] = p_ref[...]
        w = cw_ref[...]
        scale = jnp.where(t == 0, GDN_HD ** -0.5, 1.0).astype(f32)

        def run(normed):
            def lp(c, carry):
                st = pl.multiple_of(c * CONV_RT, CONV_RT)
                pre = xs_ref[pl.ds(st + HALO - 3, CONV_RT), :] * w[0:1, :]
                for i in range(1, 4):
                    pre = pre + xs_ref[pl.ds(st + HALO - 3 + i, CONV_RT), :] * w[i:i + 1, :]
                out = pre * _sigmoid(pre)
                if normed:
                    out = out * lax.rsqrt(jnp.sum(out * out, axis=-1, keepdims=True) + 1e-6) * scale
                for i in range(CONV_NC):
                    o_ref[0, c * CONV_NC + i, 0] = out[i * CHUNK:(i + 1) * CHUNK]
                return carry

            lax.fori_loop(0, S // CONV_RT, lp, 0)

        pl.when(t < 2)(functools.partial(run, True))
        pl.when(t == 2)(functools.partial(run, False))

    return pl.pallas_call(
        body, name="gdn_prep", grid=(3, GDN_HEADS),
        in_specs=[pl.BlockSpec((S, LANE), lambda t, h: (0, t * GDN_HEADS + h)),
                  pl.BlockSpec((4, LANE), lambda t, h: (0, t * GDN_HEADS + h))],
        out_specs=pl.BlockSpec((1, S // CHUNK, 1, CHUNK, GDN_HD), lambda t, h: (t, 0, h, 0, 0)),
        out_shape=SDS((3, S // CHUNK, GDN_HEADS, CHUNK, GDN_HD), f32),
        scratch_shapes=[pltpu.VMEM((S + HALO, LANE), f32)],
        compiler_params=_cp(2, VMEM_LIMIT),
    )(proj, conv_w)


def _gdn_prep_bwd(proj, conv_w, dqkv, S):
    def body(p_ref, cw_ref, d_ref, dx_ref, dw_ref, xs_ref, dp_ref):
        t = pl.program_id(0)
        xs_ref[pl.ds(0, HALO), :] = jnp.zeros((HALO, LANE), f32)
        xs_ref[pl.ds(HALO, S), :] = p_ref[...]
        dp_ref[pl.ds(S, HALO), :] = jnp.zeros((HALO, LANE), f32)
        w = cw_ref[...]
        scale = jnp.where(t == 0, GDN_HD ** -0.5, 1.0).astype(f32)

        def first_pass(normed):
            def lp1(c, dw):
                st = pl.multiple_of(c * CONV_RT, CONV_RT)
                taps = [xs_ref[pl.ds(st + HALO - 3 + i, CONV_RT), :] for i in range(4)]
                pre = taps[0] * w[0:1, :]
                for i in range(1, 4):
                    pre = pre + taps[i] * w[i:i + 1, :]
                sg = _sigmoid(pre)
                ds = jnp.concatenate([d_ref[0, c * CONV_NC + i, 0] for i in range(CONV_NC)], axis=0)
                if normed:
                    s = pre * sg
                    rn = lax.rsqrt(jnp.sum(s * s, axis=-1, keepdims=True) + 1e-6)
                    n = s * rn
                    dn = ds * scale
                    ds = rn * (dn - n * jnp.sum(dn * n, axis=-1, keepdims=True))
                dpre = ds * (sg * (1.0 + pre * (1.0 - sg)))
                dp_ref[pl.ds(st, CONV_RT), :] = dpre
                return tuple(dw[i] + jnp.sum(dpre * taps[i], axis=0, keepdims=True) for i in range(4))

            z = jnp.zeros((1, LANE), f32)
            dw = lax.fori_loop(0, S // CONV_RT, lp1, (z, z, z, z))
            for i in range(4):
                dw_ref[pl.ds(i, 1), :] = dw[i]

        pl.when(t < 2)(functools.partial(first_pass, True))
        pl.when(t == 2)(functools.partial(first_pass, False))

        def lp2(c, carry):
            st = pl.multiple_of(c * CONV_RT, CONV_RT)
            dx = dp_ref[pl.ds(st, CONV_RT), :] * w[3:4, :]
            for i in range(3):
                dx = dx + dp_ref[pl.ds(st + 3 - i, CONV_RT), :] * w[i:i + 1, :]
            dx_ref[pl.ds(st, CONV_RT), :] = dx.astype(bf16)
            return carry

        lax.fori_loop(0, S // CONV_RT, lp2, 0)

    col = lambda rows: pl.BlockSpec((rows, LANE), lambda t, h: (0, t * GDN_HEADS + h))
    return pl.pallas_call(
        body, name="gdn_prep_bwd", grid=(3, GDN_HEADS),
        in_specs=[col(S), col(4), pl.BlockSpec((1, S // CHUNK, 1, CHUNK, GDN_HD), lambda t, h: (t, 0, h, 0, 0))],
        out_specs=[col(S), col(4)],
        out_shape=[SDS((S, 3 * GDN_W), bf16), SDS((4, 3 * GDN_W), f32)],
        scratch_shapes=[pltpu.VMEM((S + HALO, LANE), f32), pltpu.VMEM((S + HALO, LANE), f32)],
        compiler_params=_cp(2, VMEM_LIMIT),
    )(proj, conv_w, dqkv)


def _bdot(a, b, prec=None):
    return lax.dot_general(a, b, (((2,), (1,)), ((0,), (0,))), precision=prec, preferred_element_type=f32)


def _bdot_nt(a, b, prec=None):
    return lax.dot_general(a, b, (((2,), (2,)), ((0,), (0,))), precision=prec, preferred_element_type=f32)


def _bdot_tn(a, b, prec=None):
    return lax.dot_general(a, b, (((1,), (1,)), ((0,), (0,))), precision=prec, preferred_element_type=f32)


@jax.custom_vjp
def _tri_inv_saved(a, t):
    return t


def _tri_inv_saved_fwd(a, t):
    return t, t


def _tri_inv_saved_bwd(t, dt):
    return -_bdot_tn(t, _bdot_nt(dt, t, HIGH), HIGH), jnp.zeros_like(t)


_tri_inv_saved.defvjp(_tri_inv_saved_fwd, _tri_inv_saved_bwd)


def _gdn_intra(q, k, v, beta, g, t_saved=None):
    nb = q.shape[0]
    c = CHUNK
    ii = lax.broadcasted_iota(jnp.int32, (c, c), 0)
    jj = lax.broadcasted_iota(jnp.int32, (c, c), 1)
    eye = ii == jj
    tril = ii >= jj
    strict = ii > jj
    ones = jnp.ones((nb, c, c), f32)
    eye_f = eye.astype(f32)

    g_row = _bdot(ones, jnp.where(eye, g, 0.0), HIGH)
    gc = jnp.sum(jnp.where(tril, g_row, 0.0), axis=2, keepdims=True)
    gc_row = _bdot(ones, jnp.where(eye, gc, 0.0), HIGH)
    decay = jnp.where(tril, jnp.exp(jnp.where(tril, gc - gc_row, 0.0)), 0.0)
    last = lax.broadcasted_iota(jnp.int32, (c, 1), 0) == c - 1
    gc_last = jnp.sum(jnp.where(last, gc, 0.0), axis=1, keepdims=True)
    e_gc = jnp.exp(gc)

    kb = k * beta
    k16 = k.astype(bf16)
    a = jnp.where(strict, _bdot_nt(kb.astype(bf16), k16) * decay, 0.0)
    if t_saved is None:
        xp = -a
        t_inv = eye_f + xp
        for level in range(5):
            if level < 2:
                xp = _bdot(xp, xp, HIGH)
                t_inv = t_inv + _bdot(t_inv, xp, HIGH)
            else:
                x16 = xp.astype(bf16)
                xp = _bdot(x16, x16)
                t_inv = t_inv + _bdot(t_inv.astype(bf16), xp.astype(bf16))
    else:
        t_inv = _tri_inv_saved(a, t_saved)
    t16 = t_inv.astype(bf16)
    u = _bdot(t16, (v * beta).astype(bf16))
    w = _bdot(t16, (kb * e_gc).astype(bf16))
    attn = jnp.where(tril, _bdot_nt(q.astype(bf16), k16) * decay, 0.0)
    gam = jnp.broadcast_to(jnp.exp(gc_last), (nb, 1, GDN_HD))
    return u, w, attn, q * e_gc, k * jnp.exp(gc_last - gc), gam, t_inv


GDN_TB = 512
GDN_NC = GDN_TB // CHUNK
GDN_NU = GDN_NC * GDN_HEADS


def _gdn_gates(pg_ref, al_ref, db_ref):
    lane1 = lax.broadcasted_iota(jnp.int32, (1, LANE), 1)
    a_lane = jnp.zeros((1, LANE), f32)
    b_lane = jnp.zeros((1, LANE), f32)
    for h in range(GDN_HEADS):
        a_lane = jnp.where(lane1 == GDN_HEADS + h, al_ref[0, h], a_lane)
        b_lane = jnp.where(lane1 == GDN_HEADS + h, db_ref[0, h], b_lane)
    pg = pg_ref[...]
    z = pg + b_lane
    return _sigmoid(pg), -jnp.exp(a_lane) * _softplus(z), z, a_lane


def _gdn_unit_inputs(qkv_ref, beta_all, g_all):
    units = [(cl, h) for cl in range(GDN_NC) for h in range(GDN_HEADS)]
    beta = jnp.stack([beta_all[cl * CHUNK:(cl + 1) * CHUNK, h:h + 1] for cl, h in units])
    g = jnp.stack([g_all[cl * CHUNK:(cl + 1) * CHUNK, GDN_HEADS + h:GDN_HEADS + h + 1] for cl, h in units])
    return qkv_ref[0], qkv_ref[1], qkv_ref[2], beta, g


def _unit_spec(*tail):
    nd = len(tail)
    return pl.BlockSpec((GDN_NU,) + tail, lambda i: (i,) + (0,) * nd)


def _gdn_intra_shapes(S):
    nu = S // CHUNK * GDN_HEADS
    row = SDS((nu, CHUNK, GDN_HD), f32)
    return [row, row, SDS((nu, CHUNK, CHUNK), f32), row, row, SDS((nu, 1, GDN_HD), f32)]


_GDN_INTRA_SPECS = lambda: [_unit_spec(CHUNK, GDN_HD), _unit_spec(CHUNK, GDN_HD), _unit_spec(CHUNK, CHUNK),
                            _unit_spec(CHUNK, GDN_HD), _unit_spec(CHUNK, GDN_HD), _unit_spec(1, GDN_HD)]


def _gdn_intra_fwd(qkv_u, proj, a_log, dt_bias, S):
    def body(qkv_ref, pg_ref, al_ref, db_ref, *outs):
        beta_all, g_all, _, _ = _gdn_gates(pg_ref, al_ref, db_ref)
        res = _gdn_intra(*_gdn_unit_inputs(qkv_ref, beta_all, g_all))
        for o_ref, r in zip(outs, res):
            o_ref[...] = r

    nu = S // CHUNK * GDN_HEADS
    *intra, t_inv = pl.pallas_call(
        body, name="gdn_intra_fwd", grid=(S // GDN_TB,),
        in_specs=[pl.BlockSpec((3, GDN_NU, CHUNK, GDN_HD), lambda i: (0, i, 0, 0)),
                  pl.BlockSpec((GDN_TB, LANE), lambda i: (i, COL_G // LANE)),
                  pl.BlockSpec(memory_space=pltpu.SMEM), pl.BlockSpec(memory_space=pltpu.SMEM)],
        out_specs=_GDN_INTRA_SPECS() + [_unit_spec(CHUNK, CHUNK)],
        out_shape=_gdn_intra_shapes(S) + [SDS((nu, CHUNK, CHUNK), f32)],
        compiler_params=_cp(1, VMEM_LIMIT),
    )(qkv_u, proj, a_log, dt_bias)
    return intra, t_inv


def _gdn_intra_bwd(qkv_u, proj, a_log, dt_bias, t_inv, cots, S):
    def body(qkv_ref, pg_ref, al_ref, db_ref, t_ref, du_ref, dw_ref, da_ref, dqd_ref, dkd_ref, dgm_ref,
             dqkv_ref, dpg_ref, dal_ref, ddb_ref):
        @pl.when(pl.program_id(0) == 0)
        def _():
            dal_ref[...] = jnp.zeros_like(dal_ref)
            ddb_ref[...] = jnp.zeros_like(ddb_ref)

        t_saved = t_ref[...]
        beta_all, g_all, z, a_lane = _gdn_gates(pg_ref, al_ref, db_ref)
        _, vjp = jax.vjp(lambda *a: _gdn_intra(*a, t_saved=t_saved)[:6], *_gdn_unit_inputs(qkv_ref, beta_all, g_all))
        dq, dk, dv, dbeta, dg = vjp((du_ref[...], dw_ref[...], da_ref[...], dqd_ref[...], dkd_ref[...], dgm_ref[...]))
        dqkv_ref[0] = dq
        dqkv_ref[1] = dk
        dqkv_ref[2] = dv
        lane = lax.broadcasted_iota(jnp.int32, (CHUNK, LANE), 1)
        rows = []
        for cl in range(GDN_NC):
            t = jnp.zeros((CHUNK, LANE), f32)
            for h in range(GDN_HEADS):
                b = cl * GDN_HEADS + h
                t = t + jnp.where(lane == h, dbeta[b], 0.0) + jnp.where(lane == GDN_HEADS + h, dg[b], 0.0)
            rows.append(t)
        d_all = jnp.concatenate(rows, axis=0)
        is_beta = lax.broadcasted_iota(jnp.int32, (GDN_TB, LANE), 1) < GDN_HEADS
        dz = d_all * (-jnp.exp(a_lane)) * _sigmoid(z)
        dpg_ref[...] = jnp.where(is_beta, d_all * beta_all * (1.0 - beta_all), dz).astype(bf16)
        dal_ref[...] += jnp.sum(jnp.where(is_beta, 0.0, d_all * g_all), axis=0, keepdims=True)
        ddb_ref[...] += jnp.sum(jnp.where(is_beta, 0.0, dz), axis=0, keepdims=True)

    acc = _const_spec((1, LANE))
    nu = S // CHUNK * GDN_HEADS
    return pl.pallas_call(
        body, name="gdn_intra_bwd", grid=(S // GDN_TB,),
        in_specs=[pl.BlockSpec((3, GDN_NU, CHUNK, GDN_HD), lambda i: (0, i, 0, 0)),
                  pl.BlockSpec((GDN_TB, LANE), lambda i: (i, COL_G // LANE)),
                  pl.BlockSpec(memory_space=pltpu.SMEM), pl.BlockSpec(memory_space=pltpu.SMEM),
                  _unit_spec(CHUNK, CHUNK)] + _GDN_INTRA_SPECS(),
        out_specs=[pl.BlockSpec((3, GDN_NU, CHUNK, GDN_HD), lambda i: (0, i, 0, 0)),
                   pl.BlockSpec((GDN_TB, LANE), lambda i: (i, 0)), acc, acc],
        out_shape=[SDS((3, nu, CHUNK, GDN_HD), f32), SDS((S, LANE), bf16), SDS((1, LANE), f32), SDS((1, LANE), f32)],
        compiler_params=_cp(1, VMEM_LIMIT),
    )(qkv_u, proj, a_log, dt_bias, t_inv, *cots)


def _gdn_scan_fwd(intra, proj, onorm_g, S):
    def body(u_ref, w_ref, at_ref, qd_ref, kd_ref, gm_ref, gate_ref, og_ref, out_ref, st_ref, s_scr):
        @pl.when(pl.program_id(0) == 0)
        def _():
            s_scr[...] = jnp.zeros_like(s_scr)

        og = og_ref[...]
        s = s_scr[...]
        def out_mm(us, s16, vn16):
            return _bdot(qd_ref[us].astype(bf16), s16) + _bdot(at_ref[us].astype(bf16), vn16)

        outs, prev = [], None
        for cl in range(GDN_NC):
            us = slice(cl * GDN_HEADS, (cl + 1) * GDN_HEADS)
            st_ref[us] = s
            s16 = s.astype(bf16)
            ws = _bdot(w_ref[us].astype(bf16), s16)
            if prev is not None:
                outs.append(out_mm(*prev))
            vn16 = (u_ref[us] - ws).astype(bf16)
            prev = (us, s16, vn16)
            s = s * gm_ref[us] + _bdot_tn(kd_ref[us].astype(bf16), vn16)
        outs.append(out_mm(*prev))
        s_scr[...] = s
        for cl, o in enumerate(outs):
            rows = slice(cl * CHUNK, (cl + 1) * CHUNK)
            for h in range(GDN_HEADS):
                oh = o[h]
                gt = gate_ref[rows, h * GDN_HD:(h + 1) * GDN_HD]
                on = oh * lax.rsqrt(jnp.mean(oh * oh, axis=-1, keepdims=True) + RMS_EPS) * og
                out_ref[rows, h * GDN_HD:(h + 1) * GDN_HD] = on * (gt * _sigmoid(gt))

    nu = S // CHUNK * GDN_HEADS
    return pl.pallas_call(
        body, name="gdn_scan_fwd", grid=(S // GDN_TB,),
        in_specs=_GDN_INTRA_SPECS() + [pl.BlockSpec((GDN_TB, GDN_W), lambda i: (i, 3)), _const_spec((1, GDN_HD))],
        out_specs=[pl.BlockSpec((GDN_TB, GDN_W), lambda i: (i, 0)), _unit_spec(GDN_HD, GDN_HD)],
        out_shape=[SDS((S, GDN_W), f32), SDS((nu, GDN_HD, GDN_HD), f32)],
        scratch_shapes=[pltpu.VMEM((GDN_HEADS, GDN_HD, GDN_HD), f32)],
        compiler_params=_cp(1, VMEM_LIMIT),
    )(*intra, proj, onorm_g)


def _gdn_scan_bwd(intra, states, proj, d_oab, onorm_g, after, S):
    n_steps = S // GDN_TB

    def body(u_ref, w_ref, at_ref, qd_ref, kd_ref, gm_ref, st_ref, gate_ref, do_ref, og_ref, after_ref,
             du_ref, dw_ref, dat_ref, dqd_ref, dkd_ref, dgm_ref, dgate_ref, dog_ref, ds_scr):
        @pl.when(pl.program_id(0) == 0)
        def _():
            ds_scr[...] = jnp.zeros_like(ds_scr)
            dog_ref[...] = jnp.zeros_like(dog_ref)

        og = og_ref[...]
        ii = lax.broadcasted_iota(jnp.int32, (CHUNK, CHUNK), 0)
        jj = lax.broadcasted_iota(jnp.int32, (CHUNK, CHUNK), 1)
        tril = ii >= jj
        dog = jnp.zeros((1, GDN_HD), f32)
        pre = []
        for cl in range(GDN_NC):
            us = slice(cl * GDN_HEADS, (cl + 1) * GDN_HEADS)
            rows = slice(cl * CHUNK, (cl + 1) * CHUNK)
            s016 = st_ref[us].astype(bf16)
            w16 = w_ref[us].astype(bf16)
            qd16 = qd_ref[us].astype(bf16)
            at16 = at_ref[us].astype(bf16)
            vn16 = (u_ref[us] - _bdot(w16, s016)).astype(bf16)
            o = _bdot(qd16, s016) + _bdot(at16, vn16)
            do_h = []
            for h in range(GDN_HEADS):
                oh = o[h]
                lanes = slice(h * GDN_HD, (h + 1) * GDN_HD)
                gt = gate_ref[rows, lanes]
                d_out = do_ref[rows, lanes]
                r = lax.rsqrt(jnp.mean(oh * oh, axis=-1, keepdims=True) + RMS_EPS)
                n = oh * r
                sg = _sigmoid(gt)
                silu = gt * sg
                dog = dog + jnp.sum(d_out * n * silu, axis=0, keepdims=True)
                dgate_ref[rows, lanes] = (d_out * n * og * (sg * (1.0 + gt * (1.0 - sg)))).astype(bf16)
                dn = d_out * og * silu
                do_h.append(r * (dn - n * jnp.mean(dn * n, axis=-1, keepdims=True)))
            do16 = jnp.stack(do_h).astype(bf16)
            pre.append((us, s016, w16, vn16, do16, _bdot_tn(at16, do16), _bdot_tn(qd16, do16)))
        ds = ds_scr[...]
        chain = [None] * GDN_NC
        for cl in reversed(range(GDN_NC)):
            us, s016, w16, vn16, do16, at_do, qd_do = pre[cl]
            ds16 = ds.astype(bf16)
            dvn = at_do + _bdot(kd_ref[us].astype(bf16), ds16)
            dvn16 = dvn.astype(bf16)
            chain[cl] = (ds, ds16, dvn, dvn16)
            ds = qd_do + ds * gm_ref[us] - _bdot_tn(w16, dvn16)
        ds_scr[...] = ds
        for cl in range(GDN_NC):
            us, s016, w16, vn16, do16, _, _ = pre[cl]
            ds_in, ds16, dvn, dvn16 = chain[cl]
            du_ref[us] = dvn
            dw_ref[us] = -_bdot_nt(dvn16, s016)
            dat_ref[us] = jnp.where(tril, _bdot_nt(do16, vn16), 0.0)
            dqd_ref[us] = _bdot_nt(do16, s016)
            dkd_ref[us] = _bdot_nt(vn16, ds16)
            dgm_ref[us] = jnp.sum(st_ref[us] * ds_in, axis=1, keepdims=True)
        dog_ref[...] += dog

    def unit(*tail):
        nd = len(tail)
        return pl.BlockSpec((GDN_NU,) + tail, lambda i: (n_steps - 1 - i,) + (0,) * nd)

    intra_specs = [unit(CHUNK, GDN_HD), unit(CHUNK, GDN_HD), unit(CHUNK, CHUNK), unit(CHUNK, GDN_HD),
                   unit(CHUNK, GDN_HD), unit(1, GDN_HD)]
    tok = lambda c: pl.BlockSpec((GDN_TB, GDN_W), lambda i: (n_steps - 1 - i, c))
    return pl.pallas_call(
        body, name="gdn_scan_bwd", grid=(n_steps,),
        in_specs=intra_specs + [unit(GDN_HD, GDN_HD), tok(3), tok(0), _const_spec((1, GDN_HD)), _ANY],
        out_specs=intra_specs + [tok(0), _const_spec((1, GDN_HD))],
        out_shape=_gdn_intra_shapes(S) + [SDS((S, GDN_W), bf16), SDS((1, GDN_HD), f32)],
        scratch_shapes=[pltpu.VMEM((GDN_HEADS, GDN_HD, GDN_HD), f32)],
        compiler_params=_cp(1, VMEM_LIMIT),
    )(*intra, states, proj, d_oab, onorm_g, after)


SWA_UNROLL = 8


def _swa_tiles(it, d, nb_log2):
    nb = 1 << nb_log2
    r = lax.shift_right_logical(it, nb_log2)
    blk = lax.bitwise_and(it, nb - 1)
    qs = blk * (SWA_BLK * d) + r
    ps = jnp.maximum(blk - 1, 0) * (SWA_BLK * d) + r
    if d > 1:
        rows_c, rows_p = pl.ds(qs, SWA_BLK, stride=d), pl.ds(ps, SWA_BLK, stride=d)
    else:
        rows_c, rows_p = pl.ds(pl.multiple_of(qs, SWA_BLK), SWA_BLK), pl.ds(pl.multiple_of(ps, SWA_BLK), SWA_BLK)
    return rows_c, rows_p, blk > 0


def _swa_prev_modes(nb):
    if nb >= SWA_UNROLL:
        return ["load"] + ["reuse"] * (SWA_UNROLL - 1)
    return ["none" if u % nb == 0 else "reuse" for u in range(SWA_UNROLL)]


def _swa_fwd(proj, bt, S):
    scale = SWA_HD ** -0.5

    def body(q_ref, k_ref, v_ref, bt_ref, o_ref, lse0_ref, lse1_ref, m0_scr, m1_scr, a0_scr, a1_scr):
        lane = lax.broadcasted_iota(jnp.int32, (SWA_BLK, LANE), 1)
        h0 = lane < SWA_HD
        qi = lax.broadcasted_iota(jnp.int32, (SWA_BLK, SWA_BLK), 0)
        kj = lax.broadcasted_iota(jnp.int32, (SWA_BLK, SWA_BLK), 1)
        lower = kj <= qi
        ones16 = jnp.ones((LANE, SWA_BLK), bf16)
        m_scrs = (m0_scr, m1_scr)
        a_scrs = (a0_scr, a1_scr)
        for p, (_, d) in reversed(list(enumerate(PATTERNS))):
            nb_log2 = int(math.log2(S // d // SWA_BLK))
            first = p == len(PATTERNS) - 1

            def lp(i, carry, p=p, d=d, nb_log2=nb_log2, first=first):
                heads = [h0, jnp.logical_not(h0)]
                modes = _swa_prev_modes(1 << nb_log2)
                tiles = []
                kc_f = None
                for u in range(SWA_UNROLL):
                    rows_c, rows_p, has_prev = _swa_tiles(i * SWA_UNROLL + u, d, nb_log2)
                    kp_f = {"load": lambda: k_ref[rows_p, :], "reuse": lambda: kc_f, "none": lambda: None}[modes[u]]()
                    has_prev = {"load": has_prev, "reuse": True, "none": False}[modes[u]]
                    q = q_ref[rows_c, :]
                    kc_f = k_ref[rows_c, :]
                    kc = kc_f.astype(bf16)
                    logits = []
                    for mh in heads:
                        q_h = jnp.where(mh, q, 0.0)
                        qh = q_h.astype(bf16)
                        if kp_f is None:
                            logits.append((_dot_nt(qh, kc), None, None))
                        else:
                            logits.append((_dot_nt(qh, kc), _dot_nt(qh, kp_f.astype(bf16)), _dot((q_h * kp_f).astype(bf16), ones16)))
                    tiles.append((rows_c, rows_p, has_prev, logits))
                probs = []
                for rows_c, rows_p, has_prev, logits in tiles:
                    per_head = []
                    for h, (s_c, s_p, far) in enumerate(logits):
                        if has_prev is False:
                            s = jnp.where(lower, s_c * scale + bt_ref[p, h], NEG)
                            s_far = None
                        else:
                            s = jnp.where(lower, s_c, s_p) * scale + bt_ref[p, h]
                            s_far = far * scale + bt_ref[p, h, SWA_BLK - 1:SWA_BLK, 0:1]
                            if has_prev is not True:
                                s = jnp.where(jnp.logical_or(lower, has_prev), s, NEG)
                                s_far = jnp.where(has_prev, s_far, NEG)
                        mn = jnp.max(s, axis=1, keepdims=True)
                        if s_far is not None:
                            mn = jnp.maximum(s_far, mn)
                        alpha = None
                        if not first:
                            mo = m_scrs[h][rows_c, :]
                            mn = jnp.maximum(mo, mn)
                            alpha = jnp.exp(mo - mn)
                        mn = jnp.broadcast_to(mn, (SWA_BLK, LANE))
                        pm = jnp.exp(s - mn)
                        per_head.append((mn, alpha, None if s_far is None else jnp.exp(s_far - mn),
                                         jnp.where(lower, pm, 0.0).astype(bf16),
                                         None if s_far is None else jnp.where(lower, 0.0, pm).astype(bf16)))
                    probs.append(per_head)
                acc_old = [None if first else (a0_scr[t[0], :], a1_scr[t[0], :]) for t in tiles]
                done = []
                vc = None
                for u, ((rows_c, rows_p, _, _), per_head, old) in enumerate(zip(tiles, probs, acc_old)):
                    vp = {"load": lambda: v_ref[rows_p, :], "reuse": lambda: vc, "none": lambda: None}[modes[u]]()
                    vc = v_ref[rows_c, :]
                    acc_new = []
                    for h, (mn, alpha, p_far, pc16, pp16) in enumerate(per_head):
                        pv = _dot(pc16, jnp.where(heads[h], vc, 1.0).astype(bf16))
                        if pp16 is not None:
                            vpa = jnp.where(heads[h], vp, 1.0)
                            pv = pv + _dot(pp16, vpa.astype(bf16)) + p_far * vpa
                        acc_new.append(pv if first else alpha * old[h] + pv)
                    done.append((rows_c, per_head[0][0], per_head[1][0], acc_new[0], acc_new[1]))
                for rows_c, m0_new, m1_new, a0_new, a1_new in done:
                    m0_scr[rows_c, :] = m0_new
                    m1_scr[rows_c, :] = m1_new
                    a0_scr[rows_c, :] = a0_new
                    a1_scr[rows_c, :] = a1_new
                return carry

            lax.fori_loop(0, S // SWA_BLK // SWA_UNROLL, lp, 0)

        def fin(c, carry):
            rows = pl.ds(pl.multiple_of(c * SWA_BLK, SWA_BLK), SWA_BLK)
            a0 = a0_scr[rows, :]
            a1 = a1_scr[rows, :]
            l0 = jnp.where(h0, pltpu.roll(a0, SWA_HD, 1), a0)
            l1 = jnp.where(h0, a1, pltpu.roll(a1, SWA_HD, 1))
            o_ref[rows, :] = jnp.where(h0, a0 / l0, a1 / l1)
            lse0_ref[rows, :] = m0_scr[rows, :] + jnp.log(l0)
            lse1_ref[rows, :] = m1_scr[rows, :] + jnp.log(l1)
            return carry

        lax.fori_loop(0, S // SWA_BLK, fin, 0)

    qb = COL_B // LANE
    col = lambda c: pl.BlockSpec((S, LANE), lambda hp, c=c: (0, c + hp))
    return pl.pallas_call(
        body, name="swa_fwd", grid=(4,),
        in_specs=[col(qb), col(qb + 4), col(qb + 8), pl.BlockSpec((3, 2, SWA_BLK, SWA_BLK), lambda hp: (0, hp, 0, 0))],
        out_specs=[col(0), col(0), col(0)],
        out_shape=[SDS((S, SWA_W), f32)] * 3,
        scratch_shapes=[pltpu.VMEM((S, LANE), f32)] * 4,
        compiler_params=_cp(1, VMEM_LIMIT),
    )(proj, proj, proj, bt)


def _swa_bwd(proj, bt, nd, lse0, lse1, d_oab, after, S):
    scale = SWA_HD ** -0.5

    def body(q_ref, k_ref, v_ref, bt_ref, nd_scr, lse0_ref, lse1_ref, do_ref, after_ref, dq_ref, dk_ref, dv_ref, dsb_ref,
             dq_scr, dk_scr, dv_scr):
        lane = lax.broadcasted_iota(jnp.int32, (SWA_BLK, LANE), 1)
        h0 = lane < SWA_HD
        qi = lax.broadcasted_iota(jnp.int32, (SWA_BLK, SWA_BLK), 0)
        kj = lax.broadcasted_iota(jnp.int32, (SWA_BLK, SWA_BLK), 1)
        lower = kj <= qi
        eye = kj == qi
        rel127 = jnp.logical_or(kj == qi + 1, jnp.logical_and(qi == SWA_BLK - 1, kj == 0))
        ones16 = jnp.ones((LANE, SWA_BLK), bf16)
        lse_refs = (lse0_ref, lse1_ref)
        dk_scr[...] = jnp.zeros((S, LANE), f32)
        dv_scr[...] = jnp.zeros((S, LANE), f32)
        dsb_ref[...] = jnp.zeros_like(dsb_ref)

        for p, (_, d) in reversed(list(enumerate(PATTERNS))):
            nb_log2 = int(math.log2(S // d // SWA_BLK))
            first = p == len(PATTERNS) - 1

            def lp(i, carry, p=p, d=d, nb_log2=nb_log2, first=first):
                heads = [h0, jnp.logical_not(h0)]
                modes = _swa_prev_modes(1 << nb_log2)
                tiles = []
                kc_f = vc_f = None
                for u in range(SWA_UNROLL):
                    rows_c, rows_p, has_prev = _swa_tiles(i * SWA_UNROLL + u, d, nb_log2)
                    kp_f = {"load": lambda: k_ref[rows_p, :], "reuse": lambda: kc_f, "none": lambda: None}[modes[u]]()
                    vp_f = {"load": lambda: v_ref[rows_p, :], "reuse": lambda: vc_f, "none": lambda: None}[modes[u]]()
                    has_prev = {"load": has_prev, "reuse": True, "none": False}[modes[u]]
                    q = q_ref[rows_c, :]
                    kc_f = k_ref[rows_c, :]
                    vc_f = v_ref[rows_c, :]
                    kc = kc_f.astype(bf16)
                    kp = None if kp_f is None else kp_f.astype(bf16)
                    do = do_ref[rows_c, :]
                    nd = nd_scr[rows_c, :]
                    per_head = []
                    for mh in heads:
                        q_h = jnp.where(mh, q, 0.0)
                        do_a = jnp.where(mh, do, nd)
                        qh = q_h.astype(bf16)
                        doa = do_a.astype(bf16)
                        doh = jnp.where(mh, do, 0.0).astype(bf16)
                        dd_c = _dot_nt(doa, jnp.where(mh, vc_f, 1.0).astype(bf16))
                        if kp_f is None:
                            per_head.append((qh, doh, _dot_nt(qh, kc), None, None, dd_c, None, None))
                        else:
                            vpa = jnp.where(mh, vp_f, 1.0)
                            per_head.append((qh, doh, _dot_nt(qh, kc), _dot_nt(qh, kp), _dot((q_h * kp_f).astype(bf16), ones16),
                                             dd_c, _dot_nt(doa, vpa.astype(bf16)), _dot((do_a * vpa).astype(bf16), ones16)))
                    tiles.append((rows_c, rows_p, has_prev, kc, kp, per_head))
                grads = []
                for rows_c, rows_p, has_prev, kc, kp, per_head in tiles:
                    out = []
                    for h, (qh, doh, s_c, s_p, far, dd_c, dd_p, dd_far) in enumerate(per_head):
                        lse_h = lse_refs[h][rows_c, :]
                        if has_prev is False:
                            pm = jnp.exp(jnp.where(lower, s_c * scale + bt_ref[p, h], NEG) - lse_h)
                            dsm = pm * dd_c
                            out.append((dsm, dsm.astype(bf16), None, pm.astype(bf16), None))
                            continue
                        s = jnp.where(lower, s_c, s_p) * scale + bt_ref[p, h]
                        s_far = far * scale + bt_ref[p, h, SWA_BLK - 1:SWA_BLK, 0:1]
                        if has_prev is not True:
                            s = jnp.where(jnp.logical_or(lower, has_prev), s, NEG)
                            s_far = jnp.where(has_prev, s_far, NEG)
                        pm = jnp.exp(s - lse_h)
                        p_far = jnp.exp(s_far - lse_h)
                        dsm = pm * jnp.where(lower, dd_c, dd_p)
                        ds_far = p_far * dd_far
                        out.append((dsm + jnp.where(rel127, ds_far, 0.0),
                                    jnp.where(lower, dsm, 0.0).astype(bf16),
                                    jnp.where(lower, jnp.where(eye, ds_far, 0.0), dsm).astype(bf16),
                                    jnp.where(lower, pm, 0.0).astype(bf16),
                                    jnp.where(lower, jnp.where(eye, p_far, 0.0), pm).astype(bf16)))
                    grads.append(out)
                done = []
                add = lambda acc, t: t if acc is None else acc + t
                for (rows_c, rows_p, _, kc, kp, per_head), out in zip(tiles, grads):
                    dq_t = dkc_t = dkp_t = dvc_t = dvp_t = None
                    for h, (_, dsc16, dsp16, pc16, pp16) in enumerate(out):
                        qh, doh = per_head[h][0], per_head[h][1]
                        dq_h = _dot(dsc16, kc)
                        dkc_t = add(dkc_t, _dot_tn(dsc16, qh) * scale)
                        dvc_t = add(dvc_t, _dot_tn(pc16, doh))
                        if dsp16 is not None:
                            dq_h = dq_h + _dot(dsp16, kp)
                            dkp_t = add(dkp_t, _dot_tn(dsp16, qh) * scale)
                            dvp_t = add(dvp_t, _dot_tn(pp16, doh))
                        dq_t = add(dq_t, jnp.where(heads[h], dq_h * scale, 0.0))
                    done.append([rows_c, rows_p, dq_t, dkc_t, dkp_t, dvc_t, dvp_t])
                for u in range(1, SWA_UNROLL):
                    if modes[u] == "reuse":
                        done[u - 1][3] = done[u - 1][3] + done[u][4]
                        done[u - 1][5] = done[u - 1][5] + done[u][6]
                for h in range(2):
                    tot = grads[0][h][0]
                    for g in grads[1:]:
                        tot = tot + g[h][0]
                    dsb_ref[0, p, h] += tot
                for u, (rows_c, rows_p, dq_t, dkc_t, dkp_t, dvc_t, dvp_t) in enumerate(done):
                    dq_scr[rows_c, :] = dq_t if first else dq_scr[rows_c, :] + dq_t
                    dk_scr[rows_c, :] = dk_scr[rows_c, :] + dkc_t
                    dv_scr[rows_c, :] = dv_scr[rows_c, :] + dvc_t
                    if modes[u] == "load":
                        dk_scr[rows_p, :] = dk_scr[rows_p, :] + dkp_t
                        dv_scr[rows_p, :] = dv_scr[rows_p, :] + dvp_t
                return carry

            lax.fori_loop(0, S // SWA_BLK // SWA_UNROLL, lp, 0)
        dq_ref[...] = dq_scr[...].astype(bf16)
        dk_ref[...] = dk_scr[...].astype(bf16)
        dv_ref[...] = dv_scr[...].astype(bf16)

    qb = COL_B // LANE
    col = lambda c: pl.BlockSpec((S, LANE), lambda hp, c=c: (0, c + hp))
    return pl.pallas_call(
        body, name="swa_bwd", grid=(4,),
        in_specs=[col(qb), col(qb + 4), col(qb + 8),
                  pl.BlockSpec((3, 2, SWA_BLK, SWA_BLK), lambda hp: (0, hp, 0, 0)),
                  col(0), col(0), col(0), col(4), _ANY],
        out_specs=[col(0), col(0), col(0),
                   pl.BlockSpec((1, 3, 2, SWA_BLK, SWA_BLK), lambda hp: (hp, 0, 0, 0, 0))],
        out_shape=[SDS((S, SWA_W), bf16)] * 3 + [SDS((4, 3, 2, SWA_BLK, SWA_BLK), f32)],
        scratch_shapes=[pltpu.VMEM((S, LANE), f32)] * 3,
        compiler_params=_cp(1, VMEM_LIMIT),
    )(proj, proj, proj, bt, nd, lse0, lse1, d_oab, after)


def _mix_fwd(oa, ob, w_out, x, g_post, S):
    TS = 512

    def body(oa_ref, ob_ref, w_ref, x_ref, g_ref, mix_ref, x1_ref):
        mix = _dot(oa_ref[...].astype(bf16), w_ref[0:GDN_W, :]) + _dot(ob_ref[...].astype(bf16), w_ref[GDN_W:D_MODEL, :])
        r = lax.rsqrt(jnp.mean(mix * mix, axis=-1, keepdims=True) + RMS_EPS)
        mix_ref[...] = mix
        x1_ref[...] = x_ref[...] + mix * r * g_ref[...]

    row = lambda w: pl.BlockSpec((TS, w), lambda i: (i, 0))
    return pl.pallas_call(
        body, name="mix_fwd", grid=(S // TS,),
        in_specs=[row(GDN_W), row(SWA_W), _resident_spec((D_MODEL, D_MODEL)), row(D_MODEL), _const_spec((1, D_MODEL))],
        out_specs=[row(D_MODEL), row(D_MODEL)],
        out_shape=[SDS((S, D_MODEL), f32), SDS((S, D_MODEL), f32)],
        compiler_params=_cp(1, VMEM_LIMIT),
    )(oa, ob, w_out, x, g_post)


def _mix_bwd(dx1, mix, g_post, w_out, ob, S):
    TS = 512

    def body(dx1_ref, mix_ref, g_ref, w_ref, ob_ref, dmix_ref, doab_ref, dg_ref, nd_ref):
        @pl.when(pl.program_id(0) == 0)
        def _():
            dg_ref[...] = jnp.zeros_like(dg_ref)

        mix = mix_ref[...]
        dz = dx1_ref[...]
        r = lax.rsqrt(jnp.mean(mix * mix, axis=-1, keepdims=True) + RMS_EPS)
        n = mix * r
        dg_ref[...] += jnp.sum(dz * n, axis=0, keepdims=True)
        dn = dz * g_ref[...]
        dmix = (r * (dn - n * jnp.mean(dn * n, axis=-1, keepdims=True))).astype(bf16)
        dmix_ref[...] = dmix
        doab = _dot_nt(dmix, w_ref[...])
        doab_ref[...] = doab
        hi_ = lax.shift_right_logical(lax.broadcasted_iota(jnp.int32, (SWA_W, SWA_W), 0), 6)
        hj_ = lax.shift_right_logical(lax.broadcasted_iota(jnp.int32, (SWA_W, SWA_W), 1), 6)
        swap = (hi_ == lax.bitwise_xor(hj_, 1)).astype(bf16)
        dlt = doab[:, GDN_W:] * ob_ref[...]
        hi = dlt.astype(bf16)
        nd_ref[...] = (_dot(hi, swap) + _dot((dlt - hi.astype(f32)).astype(bf16), swap)) * (-1.0 / SWA_HD)

    row = lambda w=D_MODEL: pl.BlockSpec((TS, w), lambda i: (i, 0))
    return pl.pallas_call(
        body, name="mix_bwd", grid=(S // TS,),
        in_specs=[row(), row(), _const_spec((1, D_MODEL)), _resident_spec((D_MODEL, D_MODEL)), row(SWA_W)],
        out_specs=[row(), row(), _const_spec((1, D_MODEL)), row(SWA_W)],
        out_shape=[SDS((S, D_MODEL), bf16), SDS((S, D_MODEL), f32), SDS((1, D_MODEL), f32), SDS((S, SWA_W), f32)],
        compiler_params=_cp(1, VMEM_LIMIT),
    )(dx1, mix, g_post, w_out, ob)


FFN_TS = 256
FFN_CH = 1408


def _ffn(x1, tgt, g_pre, g_post, wg, wu, wd, S):
    def body(x1_ref, t_ref, gp_ref, gq_ref, wg_ref, wu_ref, wd_ref,
             dx1_ref, h2_ref, act_ref, dgate_ref, dup_ref, df_ref, loss_ref, dgp_ref, dgq_ref, gate_scr, up_scr):
        @pl.when(pl.program_id(0) == 0)
        def _():
            loss_ref[...] = jnp.zeros_like(loss_ref)
            dgp_ref[...] = jnp.zeros_like(dgp_ref)
            dgq_ref[...] = jnp.zeros_like(dgq_ref)

        x1v = x1_ref[...]
        gp = gp_ref[...]
        gq = gq_ref[...]
        r2 = lax.rsqrt(jnp.mean(x1v * x1v, axis=-1, keepdims=True) + RMS_EPS)
        n2 = x1v * r2
        h2 = (n2 * gp).astype(bf16)
        h2_ref[...] = h2
        chunks = [slice(c * FFN_CH, (c + 1) * FFN_CH) for c in range(D_FF // FFN_CH)]
        for cs in chunks:
            gate_scr[:, cs] = _dot_nt(h2, wg_ref[cs, :])
            up_scr[:, cs] = _dot_nt(h2, wu_ref[cs, :])
        acts = []
        for cs in chunks:
            gate = gate_scr[:, cs]
            act = (gate * _sigmoid(gate) * up_scr[:, cs]).astype(bf16)
            act_ref[:, cs] = act
            acts.append(act)
        f = _dot(acts[0], wd_ref[chunks[0], :])
        for act, cs in zip(acts[1:], chunks[1:]):
            f = f + _dot(act, wd_ref[cs, :])
        r3 = lax.rsqrt(jnp.mean(f * f, axis=-1, keepdims=True) + RMS_EPS)
        n3 = f * r3
        err = x1v + n3 * gq - t_ref[...]
        loss_ref[...] += 0.5 * jnp.sum(jnp.mean(err * err, axis=-1, keepdims=True), axis=0, keepdims=True)
        dy = err * (1.0 / D_MODEL)
        dgq_ref[...] += jnp.sum(dy * n3, axis=0, keepdims=True)
        dn3 = dy * gq
        df = (r3 * (dn3 - n3 * jnp.mean(dn3 * n3, axis=-1, keepdims=True))).astype(bf16)
        df_ref[...] = df
        dacts = [_dot_nt(df, wd_ref[cs, :]) for cs in chunks]
        dgs = []
        for dact, cs in zip(dacts, chunks):
            gate = gate_scr[:, cs]
            sg = _sigmoid(gate)
            dup = (dact * gate * sg).astype(bf16)
            dgate = (dact * up_scr[:, cs] * (sg * (1.0 + gate * (1.0 - sg)))).astype(bf16)
            dup_ref[:, cs] = dup
            dgate_ref[:, cs] = dgate
            dgs.append((dgate, dup))
        dh2 = None
        for (dgate, dup), cs in zip(dgs, chunks):
            t = _dot(dgate, wg_ref[cs, :]) + _dot(dup, wu_ref[cs, :])
            dh2 = t if dh2 is None else dh2 + t
        dgp_ref[...] += jnp.sum(dh2 * n2, axis=0, keepdims=True)
        dn2 = dh2 * gp
        dx1_ref[...] = dy + r2 * (dn2 - n2 * jnp.mean(dn2 * n2, axis=-1, keepdims=True))

    row = lambda w: pl.BlockSpec((FFN_TS, w), lambda i: (i, 0))
    vec = _const_spec((1, D_MODEL))
    return pl.pallas_call(
        body, name="ffn_fwd_bwd", grid=(S // FFN_TS,),
        in_specs=[row(D_MODEL), row(D_MODEL), vec, vec, _resident_spec((D_FF, D_MODEL)), _resident_spec((D_FF, D_MODEL)),
                  _resident_spec((D_FF, D_MODEL))],
        out_specs=[row(D_MODEL), row(D_MODEL), row(D_FF), row(D_FF), row(D_FF), row(D_MODEL), _const_spec((1, LANE)), vec, vec],
        out_shape=[SDS((S, D_MODEL), f32), SDS((S, D_MODEL), bf16), SDS((S, D_FF), bf16), SDS((S, D_FF), bf16),
                   SDS((S, D_FF), bf16), SDS((S, D_MODEL), bf16), SDS((1, LANE), f32), SDS((1, D_MODEL), f32),
                   SDS((1, D_MODEL), f32)],
        scratch_shapes=[pltpu.VMEM((FFN_TS, D_FF), f32), pltpu.VMEM((FFN_TS, D_FF), f32)],
        compiler_params=_cp(1, VMEM_LIMIT),
    )(x1, tgt, g_pre, g_post, wg, wu, wd)


def _proj_bwd(x, dx1, g_pre, wcat, segs, after, S):
    TS = 512
    n = len(segs)
    cols = [(c0, a.shape[1]) for a, c0 in segs]

    def body(*refs):
        x_ref, dx1_ref, g_ref, w_ref = refs[:4]
        seg_refs = refs[4:4 + n]
        gx_ref, dg_ref = refs[5 + n:]

        @pl.when(pl.program_id(0) == 0)
        def _():
            dg_ref[...] = jnp.zeros_like(dg_ref)

        dh = jnp.zeros((TS, D_MODEL), f32)
        for s_ref, (c0, w) in zip(seg_refs, cols):
            dh = dh + _dot_nt(s_ref[...], w_ref[:, c0:c0 + w])
        xv = x_ref[...]
        g = g_ref[...]
        r = lax.rsqrt(jnp.mean(xv * xv, axis=-1, keepdims=True) + RMS_EPS)
        nx = xv * r
        dg_ref[...] += jnp.sum(dh * nx, axis=0, keepdims=True)
        dn = dh * g
        gx_ref[...] = dx1_ref[...] + r * (dn - nx * jnp.mean(dn * nx, axis=-1, keepdims=True))

    row = lambda w: pl.BlockSpec((TS, w), lambda i: (i, 0))
    return pl.pallas_call(
        body, name="proj_bwd", grid=(S // TS,),
        in_specs=[row(D_MODEL), row(D_MODEL), _const_spec((1, D_MODEL)), _resident_spec((D_MODEL, NCOL))]
                 + [row(w) for _, w in cols] + [_ANY],
        out_specs=[row(D_MODEL), _const_spec((1, D_MODEL))],
        out_shape=[SDS((S, D_MODEL), f32), SDS((1, D_MODEL), f32)],
        compiler_params=_cp(1, VMEM_LIMIT),
    )(x, dx1, g_pre, wcat, *[a for a, _ in segs], after)


def _wgrad(a, b, S, name):
    TS = 2048
    K = a.shape[1]
    N = b.shape[1]
    TN = next(t for t in (512, 1408, N) if N % t == 0)

    def body(a_ref, b_ref, o_ref, acc):
        @pl.when(pl.program_id(1) == 0)
        def _():
            acc[...] = jnp.zeros_like(acc)

        acc[...] += _dot_tn(a_ref[...].astype(bf16), b_ref[...])

        @pl.when(pl.program_id(1) == pl.num_programs(1) - 1)
        def _():
            o_ref[...] = acc[...].astype(bf16)

    return pl.pallas_call(
        body, name=name, grid=(N // TN, S // TS),
        in_specs=[pl.BlockSpec((TS, K), lambda j, s: (s, 0)), pl.BlockSpec((TS, TN), lambda j, s: (s, j))],
        out_specs=pl.BlockSpec((K, TN), lambda j, s: (0, j)), out_shape=SDS((K, N), bf16),
        scratch_shapes=[pltpu.VMEM((K, TN), f32)],
        compiler_params=_cp(2, VMEM_LIMIT),
    )(a, b)


def _wgrad_out(oa, ob, dmix, S):
    TS, TN = 1024, 512

    def body(a_ref, b_ref, d_ref, o_ref, acc):
        @pl.when(pl.program_id(1) == 0)
        def _():
            acc[...] = jnp.zeros_like(acc)

        d = d_ref[...]
        acc[0:GDN_W, :] += _dot_tn(a_ref[...].astype(bf16), d)
        acc[GDN_W:D_MODEL, :] += _dot_tn(b_ref[...].astype(bf16), d)

        @pl.when(pl.program_id(1) == pl.num_programs(1) - 1)
        def _():
            o_ref[...] = acc[...].astype(bf16)

    tok = lambda w: pl.BlockSpec((TS, w), lambda j, s: (s, 0))
    return pl.pallas_call(
        body, name="wgrad_out", grid=(D_MODEL // TN, S // TS),
        in_specs=[tok(GDN_W), tok(SWA_W), pl.BlockSpec((TS, TN), lambda j, s: (s, j))],
        out_specs=pl.BlockSpec((D_MODEL, TN), lambda j, s: (0, j)), out_shape=SDS((D_MODEL, D_MODEL), bf16),
        scratch_shapes=[pltpu.VMEM((D_MODEL, TN), f32)],
        compiler_params=_cp(2, VMEM_LIMIT),
    )(oa, ob, dmix)


def _w_in_pieces():
    n_a, n_g = 4 * GDN_W, 2 * GDN_HEADS
    cb = IN_COLS // N_DEV
    bounds = [(0, n_a, COL_A), (n_a, n_a + n_g, COL_G), (n_a + n_g, IN_COLS, COL_B)]
    out = []
    for j in range(N_DEV):
        lo, hi = j * cb, (j + 1) * cb
        for s0, s1, dst in bounds:
            a, b = max(lo, s0), min(hi, s1)
            if a < b:
                out.append((j, a - lo, b - a, dst + a - s0))
    return out


def _wcat_from_blocks(g_in):
    TR = 256
    cb = IN_COLS // N_DEV
    pieces = _w_in_pieces()

    def body(w_ref, o_ref):
        o_ref[:, COL_G:NCOL] = jnp.zeros((TR, NCOL - COL_G), bf16)
        for j, off, w, dst in pieces:
            o_ref[:, dst:dst + w] = w_ref[j, :, off:off + w]

    return pl.pallas_call(
        body, name="wcat_from_blocks", grid=(D_MODEL // TR,),
        in_specs=[pl.BlockSpec((N_DEV, TR, cb), lambda i: (0, i, 0))],
        out_specs=pl.BlockSpec((TR, NCOL), lambda i: (i, 0)),
        out_shape=SDS((D_MODEL, NCOL), bf16),
        compiler_params=_cp(1, VMEM_LIMIT),
    )(g_in)


def _wgrad_in(h1, segs, S):
    TS = 1024
    n = len(segs)
    cols = [(c0, a.shape[1]) for a, c0 in segs]
    cb = IN_COLS // N_DEV
    pieces = _w_in_pieces()

    def body(*refs):
        h_ref = refs[0]
        seg_refs = refs[1:1 + n]
        o_ref, acc = refs[1 + n], refs[2 + n]

        @pl.when(pl.program_id(0) == 0)
        def _():
            acc[...] = jnp.zeros_like(acc)

        h = h_ref[...]
        for s_ref, (c0, w) in zip(seg_refs, cols):
            acc[:, c0:c0 + w] += _dot_tn(h, s_ref[...])

        @pl.when(pl.program_id(0) == pl.num_programs(0) - 1)
        def _():
            for j, off, w, src in pieces:
                o_ref[j, :, off:off + w] = acc[:, src:src + w].astype(bf16)

    row = lambda w: pl.BlockSpec((TS, w), lambda i: (i, 0))
    return pl.pallas_call(
        body, name="wgrad_in", grid=(S // TS,),
        in_specs=[row(D_MODEL)] + [row(w) for _, w in cols],
        out_specs=_const_spec((N_DEV, D_MODEL, cb)),
        out_shape=SDS((N_DEV, D_MODEL, cb), bf16),
        scratch_shapes=[pltpu.VMEM((D_MODEL, NCOL), f32)],
        compiler_params=_cp(1, VMEM_LIMIT),
    )(h1, *[a for a, _ in segs])


def _adamw(recv, src, me, w, m, v, name):
    R, C = w.shape
    TR = 256 if R % 256 == 0 else R
    c1 = 1.0 / (1.0 - ADAM_B1 ** ADAM_STEP)
    c2 = 1.0 / (1.0 - ADAM_B2 ** ADAM_STEP)

    def body(me_ref, r_ref, own_ref, w_ref, m_ref, v_ref, g_out, d_out, m_out, v_out):
        g = None
        for s in range(N_DEV):
            t = jnp.where(me_ref[0] == s, own_ref[0], r_ref[s]).astype(f32)
            g = t if g is None else g + t
        mn = ADAM_B1 * m_ref[...] + (1.0 - ADAM_B1) * g
        vn = ADAM_B2 * v_ref[...] + (1.0 - ADAM_B2) * (g * g)
        g_out[...] = g
        m_out[...] = mn
        v_out[...] = vn
        d_out[...] = -ADAM_LR * ((mn * c1) / (jnp.sqrt(vn * c2) + ADAM_EPS) + ADAM_WD * w_ref[...])

    blk = pl.BlockSpec((TR, C), lambda i, me_ref: (i, 0))
    return pl.pallas_call(
        body, name=name,
        grid_spec=pltpu.PrefetchScalarGridSpec(
            num_scalar_prefetch=1, grid=(R // TR,),
            in_specs=[pl.BlockSpec((N_DEV, TR, C), lambda i, me_ref: (0, i, 0)),
                      pl.BlockSpec((1, TR, C), lambda i, me_ref: (me_ref[0], i, 0)), blk, blk, blk],
            out_specs=[blk, blk, blk, blk]),
        out_shape=[SDS((R, C), f32)] * 4,
        compiler_params=_cp(1, VMEM_LIMIT),
    )(me, recv, src, w, m, v)


MESH = pl.DeviceIdType.MESH
_ANY = pl.BlockSpec(memory_space=pl.ANY)


def _flip(v, d):
    return 1 - v if d else v


def _all_gather(shards, rel_bias, bk):
    n = len(shards)

    def body(*refs):
        ins = refs[:n]
        rb_ref, bk_ref = refs[n], refs[n + 1]
        outs = refs[n + 2:2 * n + 2]
        bt_ref = refs[2 * n + 2]
        send_sems, recv_sems, local_sems = refs[2 * n + 3:]
        x, y, c = lax.axis_index("x"), lax.axis_index("y"), lax.axis_index("c")
        me, sibling = (x, y, c), (x, y, 1 - c)
        chips = [(1 - x, y), (x, 1 - y), (1 - x, 1 - y)]

        def slot(px, py, pc):
            return 4 * px + 2 * py + pc

        def copy(a, k, block, to, src=None):
            dst = outs[a].at[slot(*block)]
            return pltpu.make_async_remote_copy(src_ref=dst if src is None else src, dst_ref=dst,
                                                send_sem=send_sems.at[a, k], recv_sem=recv_sems.at[a, k],
                                                device_id=to, device_id_type=MESH)

        mine, first, passed = [], [], []
        for a in range(n):
            cp = pltpu.make_async_copy(ins[a], outs[a].at[slot(*me)], local_sems.at[a])
            cp.start()
            mine.append(cp)
            fs = [copy(a, 0, me, sibling, src=ins[a])]
            fs += [copy(a, 1 + j, me, (*chip, c), src=ins[a]) for j, chip in enumerate(chips)]
            for cp in fs:
                cp.start()
            first += fs
        _fill_bias_tables(rb_ref, bk_ref, bt_ref)
        for j, chip in enumerate(chips):
            for a in range(n):
                copy(a, 1 + j, (*chip, c), me).wait_recv()
                cp = copy(a, 4 + j, (*chip, c), sibling)
                cp.start()
                passed.append(cp)
        for a in range(n):
            copy(a, 0, sibling, me).wait_recv()
            for j, chip in enumerate(chips):
                copy(a, 4 + j, (*chip, 1 - c), me).wait_recv()
        for cp in first + passed:
            cp.wait_send()
        for cp in mine:
            cp.wait()

    vmem = pl.BlockSpec(memory_space=pltpu.VMEM)
    return pl.pallas_call(
        body, name="weight_all_gather",
        in_specs=[_ANY] * n + [pl.BlockSpec(memory_space=pltpu.SMEM), vmem], out_specs=[_ANY] * n + [vmem],
        out_shape=[SDS((N_DEV,) + s.shape, s.dtype) for s in shards] + [SDS((len(PATTERNS), SWA_HEADS, SWA_BLK, SWA_BLK), f32)],
        scratch_shapes=[pltpu.SemaphoreType.DMA((n, 7)), pltpu.SemaphoreType.DMA((n, 7)), pltpu.SemaphoreType.DMA((n,))],
        compiler_params=pltpu.CompilerParams(has_side_effects=True),
    )(*shards, rel_bias, bk)


def _grad_exchange(blocked, whole):
    arrs = list(blocked) + list(whole)
    n, nb = len(arrs), len(blocked)
    rel = [(dx, dy, dc) for dx in (0, 1) for dy in (0, 1) for dc in (0, 1) if dx or dy or dc]

    def body(*refs):
        ins = refs[:n]
        outs = refs[n:2 * n]
        send_sems, recv_sems, local_sems = refs[2 * n:]
        x, y, c = lax.axis_index("x"), lax.axis_index("y"), lax.axis_index("c")
        me = 4 * x + 2 * y + c
        sends, locs = [], []
        for a in range(n):
            cp = pltpu.make_async_copy(ins[a].at[me] if a < nb else ins[a], outs[a].at[me], local_sems.at[a])
            cp.start()
            locs.append(cp)
            for k, (dx, dy, dc) in enumerate(rel):
                peer = (_flip(x, dx), _flip(y, dy), _flip(c, dc))
                pidx = 4 * peer[0] + 2 * peer[1] + peer[2]
                cp = pltpu.make_async_remote_copy(src_ref=ins[a].at[pidx] if a < nb else ins[a], dst_ref=outs[a].at[me],
                                                  send_sem=send_sems.at[a, k], recv_sem=recv_sems.at[a, k],
                                                  device_id=peer, device_id_type=MESH)
                cp.start()
                sends.append(cp)
        for a in range(n):
            for k, (dx, dy, dc) in enumerate(rel):
                peer = (_flip(x, dx), _flip(y, dy), _flip(c, dc))
                pidx = 4 * peer[0] + 2 * peer[1] + peer[2]
                pltpu.make_async_remote_copy(src_ref=outs[a].at[pidx], dst_ref=outs[a].at[pidx],
                                             send_sem=send_sems.at[a, k], recv_sem=recv_sems.at[a, k],
                                             device_id=peer, device_id_type=MESH).wait_recv()
        for cp in sends:
            cp.wait_send()
        for cp in locs:
            cp.wait()

    shapes = [SDS(a.shape, a.dtype) for a in blocked] + [SDS((N_DEV,) + a.shape, a.dtype) for a in whole]
    return pl.pallas_call(
        body, name="grad_exchange",
        in_specs=[_ANY] * n, out_specs=[_ANY] * n, out_shape=shapes,
        scratch_shapes=[pltpu.SemaphoreType.DMA((n, 7)), pltpu.SemaphoreType.DMA((n, 7)), pltpu.SemaphoreType.DMA((n,))],
        compiler_params=pltpu.CompilerParams(has_side_effects=True),
    )(*arrs)


_HBM = pl.BlockSpec(memory_space=pltpu.HBM)
_SEM = pl.BlockSpec(memory_space=pltpu.SEMAPHORE)
_REL = [(dx, dy, dc) for dx in (0, 1) for dy in (0, 1) for dc in (0, 1) if dx or dy or dc]


N_PEER = len(_REL)
_EFFECT = pltpu.SideEffectType.DATAFLOW_SIDE_EFFECTING


def _peer_copies(srcs, lands, send_sems, recv_sems, blocked, as_receiver):
    x, y, c = lax.axis_index("x"), lax.axis_index("y"), lax.axis_index("c")
    me = 4 * x + 2 * y + c
    cps = []
    for a in range(len(srcs)):
        for k, (dx, dy, dc) in enumerate(_REL):
            peer = (_flip(x, dx), _flip(y, dy), _flip(c, dc))
            pidx = 4 * peer[0] + 2 * peer[1] + peer[2]
            cps.append(pltpu.make_async_remote_copy(
                src_ref=srcs[a].at[pidx] if blocked else srcs[a], dst_ref=lands[a].at[pidx if as_receiver else me],
                send_sem=send_sems[a * N_PEER + k], recv_sem=recv_sems[a * N_PEER + k],
                device_id=peer, device_id_type=MESH))
    return cps


def _exchange_start(srcs, after, blocked, name):
    n = len(srcs)
    ns = n * N_PEER
    lands = [lax.empty(s.shape if blocked else (N_DEV,) + s.shape, s.dtype) for s in srcs]

    def body(*refs):
        ins, lnd = refs[:n], refs[n:2 * n]
        outs = refs[2 * n + 1:]
        for cp in _peer_copies(ins, lnd, outs[:ns], outs[ns:2 * ns], blocked, False):
            cp.start()
        outs[-1][...] = jnp.zeros_like(outs[-1])

    res = pl.pallas_call(
        body, name=name,
        in_specs=[_HBM] * (2 * n) + [_ANY],
        out_specs=[_SEM] * (2 * ns) + [_HBM] * (2 * n) + [pl.BlockSpec(memory_space=pltpu.VMEM)],
        out_shape=[pltpu.SemaphoreType.DMA(())] * (2 * ns) + [pltpu.HBM(s.shape, s.dtype) for s in srcs]
                  + [pltpu.HBM(l.shape, l.dtype) for l in lands] + [SDS((8, LANE), f32)],
        input_output_aliases={i: 2 * ns + i for i in range(2 * n)},
        compiler_params=pltpu.CompilerParams(has_side_effects=_EFFECT),
    )(*[pltpu.with_memory_space_constraint(s, pltpu.HBM) for s in srcs],
      *[pltpu.with_memory_space_constraint(l, pltpu.HBM) for l in lands], after)
    return list(res[:2 * ns]), list(res[2 * ns:2 * ns + n]), list(res[2 * ns + n:2 * ns + 2 * n]), res[-1]


def _exchange_wait(sems, srcs, lands, after, blocked, name):
    n = len(srcs)
    ns = n * N_PEER

    def body(*refs):
        ins, lnd = refs[:n], refs[n:2 * n]
        sem_refs = refs[2 * n:2 * n + 2 * ns]
        for cp in _peer_copies(ins, lnd, sem_refs[:ns], sem_refs[ns:], blocked, True):
            cp.wait_send()
            cp.wait_recv()

    res = pl.pallas_call(
        body, name=name,
        in_specs=[_HBM] * (2 * n) + [_SEM] * (2 * ns) + [_ANY],
        out_specs=[_HBM] * (2 * n),
        out_shape=[pltpu.HBM(s.shape, s.dtype) for s in srcs] + [pltpu.HBM(l.shape, l.dtype) for l in lands],
        input_output_aliases={i: i for i in range(2 * n)},
        compiler_params=pltpu.CompilerParams(has_side_effects=_EFFECT),
    )(*srcs, *lands, *sems, after)
    return list(res[:n]), list(res[n:])


def _local_step(x, tgt, wcat, convw, bt, late_weights, early_grads, last_grads, token, a_log, dt_bias, onorm_g,
                g_mix_pre, g_mix_post, g_ffn_pre, g_ffn_post):
    S = x.shape[0]
    bk_np = _bucket_tables()
    bk = jnp.asarray(bk_np)
    proj, h1 = _proj_fwd(x, g_mix_pre, wcat, token, S)
    nu = S // CHUNK * GDN_HEADS
    qkv_u = _gdn_prep(proj, convw, S).reshape(3, nu, CHUNK, GDN_HD)
    intra, t_inv = _gdn_intra_fwd(qkv_u, proj, a_log, dt_bias, S)
    oa, states = _gdn_scan_fwd(intra, proj, onorm_g, S)
    ob, lse0, lse1 = _swa_fwd(proj, bt, S)
    wout, ffn_weights = late_weights(ob)
    mix, x1 = _mix_fwd(oa, ob, wout, x, g_mix_post, S)
    wgate, wup, wdown = ffn_weights(x1)
    dx1, h2, act, dgate_f, dup_f, df, loss, d_gfpre, d_gfpost = _ffn(x1, tgt, g_ffn_pre, g_ffn_post, wgate, wup, wdown, S)
    rows8 = lambda g: g.reshape(N_DEV, D_FF // N_DEV, D_MODEL)
    g_gate = rows8(_wgrad(dgate_f, h2, S, "wgrad_gate"))
    g_up = rows8(_wgrad(dup_f, h2, S, "wgrad_up"))
    g_down = rows8(_wgrad(act, df, S, "wgrad_down"))
    dmix, d_oab, d_gmpost, nd = _mix_bwd(dx1, mix, g_mix_post, wout, ob, S)
    g_out = _wgrad_out(oa, ob, dmix, S)
    token = early_grads(g_out.reshape(N_DEV, D_MODEL // N_DEV, D_MODEL), g_gate, g_up, g_down)
    dqb, dkb, dvb, dsb = _swa_bwd(proj, bt, nd, lse0, lse1, d_oab, token, S)
    *cots, dgate_a, d_og = _gdn_scan_bwd(intra, states, proj, d_oab, onorm_g, token, S)
    dqkv_u, dpg, d_alog, d_dtb = _gdn_intra_bwd(qkv_u, proj, a_log, dt_bias, t_inv, cots, S)
    dqkv_a, d_conv = _gdn_prep_bwd(proj, convw, dqkv_u.reshape(3, S // CHUNK, GDN_HEADS, CHUNK, GDN_HD), S)
    segs = [(dqkv_a, COL_A), (dgate_a, COL_A + 3 * GDN_W), (dqb, COL_B), (dkb, COL_B + SWA_W), (dvb, COL_B + 2 * SWA_W),
            (dpg, COL_G)]
    token = last_grads(_wgrad_in(h1, segs, S), d_conv)
    grad_x, d_gmpre = _proj_bwd(x, dx1, g_mix_pre, wcat, segs, token, S)
    d_rel = _rel_bias_grad(dsb, bk, bk_np)
    small = dict(a_log=d_alog[:, GDN_HEADS:2 * GDN_HEADS], dt_bias=d_dtb[:, GDN_HEADS:2 * GDN_HEADS], onorm_g=d_og, rel_bias=d_rel,
                 g_mix_pre=d_gmpre, g_mix_post=d_gmpost, g_ffn_pre=d_gfpre, g_ffn_post=d_gfpost)
    return loss, grad_x, small


SMALL = ("a_log", "dt_bias", "onorm_g", "rel_bias", "g_mix_pre", "g_mix_post", "g_ffn_pre", "g_ffn_post")
PACK_ROWS = 8


def _pack_small(d, loss=None):
    rest = jnp.concatenate([d["onorm_g"].reshape(-1), d["a_log"].reshape(-1), d["dt_bias"].reshape(-1),
                            d["rel_bias"].reshape(-1)])
    rest = jnp.concatenate([rest, jnp.zeros((D_MODEL - rest.shape[0],), f32)])
    extra = jnp.zeros((D_MODEL,), f32) if loss is None else jnp.concatenate([loss.reshape(1), jnp.zeros((D_MODEL - 1,), f32)])
    rows = [d["g_mix_pre"].reshape(-1), d["g_mix_post"].reshape(-1), d["g_ffn_pre"].reshape(-1),
            d["g_ffn_post"].reshape(-1), rest, extra]
    return jnp.concatenate([jnp.stack(rows), jnp.zeros((PACK_ROWS - len(rows), D_MODEL), f32)], axis=0)


def _unpack_small(p):
    o = GDN_HD
    return dict(g_mix_pre=p[0:1], g_mix_post=p[1:2], g_ffn_pre=p[2:3], g_ffn_post=p[3:4],
                onorm_g=p[4:5, :o], a_log=p[4:5, o:o + 4], dt_bias=p[4:5, o + 4:o + 8],
                rel_bias=p[4, o + 8:o + 8 + NUM_BUCKETS * SWA_HEADS].reshape(NUM_BUCKETS, SWA_HEADS))


def kernel(x, w_in, conv_w, a_log, dt_bias, onorm_g, rel_bias, w_out, g_mix_pre, g_mix_post, w_gate, w_up, w_down, g_ffn_pre, g_ffn_post, loss_target, m_w_in, m_conv_w, m_a_log, m_dt_bias, m_onorm_g, m_rel_bias, m_w_out, m_g_mix_pre, m_g_mix_post, m_w_gate, m_w_up, m_w_down, m_g_ffn_pre, m_g_ffn_post, v_w_in, v_conv_w, v_a_log, v_dt_bias, v_onorm_g, v_rel_bias, v_w_out, v_g_mix_pre, v_g_mix_post, v_w_gate, v_w_up, v_w_down, v_g_ffn_pre, v_g_ffn_post):
    big = ("w_in", "conv_w", "w_out", "w_gate", "w_up", "w_down")
    transposed = ("w_gate", "w_up")
    tr = lambda k, a: a.T if k in transposed else a
    w_sh = {k: tr(k, a[0]) for k, a in dict(w_in=w_in, conv_w=conv_w, w_out=w_out, w_gate=w_gate, w_up=w_up, w_down=w_down).items()}
    m_sh = {k: tr(k, a[0]) for k, a in dict(w_in=m_w_in, conv_w=m_conv_w, w_out=m_w_out, w_gate=m_w_gate, w_up=m_w_up,
                                             w_down=m_w_down).items()}
    v_sh = {k: tr(k, a[0]) for k, a in dict(w_in=v_w_in, conv_w=v_conv_w, w_out=v_w_out, w_gate=v_w_gate, w_up=v_w_up,
                                             w_down=v_w_down).items()}
    w_small = dict(a_log=a_log, dt_bias=dt_bias, onorm_g=onorm_g, rel_bias=rel_bias, g_mix_pre=g_mix_pre,
                   g_mix_post=g_mix_post, g_ffn_pre=g_ffn_pre, g_ffn_post=g_ffn_post)
    m_small = dict(a_log=m_a_log, dt_bias=m_dt_bias, onorm_g=m_onorm_g, rel_bias=m_rel_bias, g_mix_pre=m_g_mix_pre,
                   g_mix_post=m_g_mix_post, g_ffn_pre=m_g_ffn_pre, g_ffn_post=m_g_ffn_post)
    v_small = dict(a_log=v_a_log, dt_bias=v_dt_bias, onorm_g=v_onorm_g, rel_bias=v_rel_bias, g_mix_pre=v_g_mix_pre,
                   g_mix_post=v_g_mix_post, g_ffn_pre=v_g_ffn_pre, g_ffn_post=v_g_ffn_post)

    me = 4 * lax.axis_index("x") + 2 * lax.axis_index("y") + lax.axis_index("c")
    me1 = me.reshape(1).astype(jnp.int32)
    own = lambda full, part: lax.dynamic_update_index_in_dim(full, part, me, 0)
    cols = lambda g: g.reshape(g.shape[0], N_DEV, g.shape[1] // N_DEV).transpose(1, 0, 2)
    late = ("w_out", "w_gate", "w_up", "w_down")

    late_src = [w_sh[k].astype(bf16) for k in late]
    g_in, g_conv, bt = _all_gather([w_sh["w_in"].astype(bf16), w_sh["conv_w"]], rel_bias, jnp.asarray(_bucket_tables()))
    g_sems, g_src, g_land, g_token = _exchange_start(late_src, g_conv, False, "late_weights_start")
    wcat = _wcat_from_blocks(g_in)
    convw = g_conv.transpose(1, 0, 2).reshape(4, 3 * GDN_W)

    def late_weights(after):
        pick = lambda idx: [g_sems[half * len(late) * N_PEER + a * N_PEER + k] for half in (0, 1) for a in idx for k in range(N_PEER)]
        (s_out,), (l_out,) = _exchange_wait(pick([0]), g_src[:1], g_land[:1], after, False, "w_out_wait")

        def ffn_weights(after2):
            srcs, lands = _exchange_wait(pick([1, 2, 3]), g_src[1:], g_land[1:], after2, False, "ffn_weights_wait")
            return [own(l, s).reshape(D_FF, D_MODEL) for l, s in zip(lands, srcs)]

        return own(l_out, s_out).reshape(D_MODEL, D_MODEL), ffn_weights

    early, last = {}, {}

    def early_grads(*blocks):
        early["sems"], early["src"], early["land"], token = _exchange_start(list(blocks), me1, True, "late_grads_start")
        return token

    def last_grads(gw_in, gw_conv):
        src = [gw_in, cols(gw_conv)]
        last["sems"], last["src"], last["land"], token = _exchange_start(src, me1, True, "last_grads_start")
        return token

    loss_p, grad_x, gsmall = _local_step(
        x[0], loss_target[0], wcat, convw, bt, late_weights, early_grads, last_grads, g_token,
        a_log, dt_bias, onorm_g, g_mix_pre, g_mix_post, g_ffn_pre, g_ffn_post)

    (r_small,) = _grad_exchange([], [_pack_small(gsmall, loss_p[0, 0])])
    outs = {}
    for names, ex, after, name in ((late, early, grad_x, "late_grads_wait"), (("w_in", "conv_w"), last, r_small, "last_grads_wait")):
        srcs, lands = _exchange_wait(ex["sems"], ex["src"], ex["land"], after, True, name)
        for k, l, s in zip(names, lands, srcs):
            outs[k] = _adamw(l, s, me1, w_sh[k], m_sh[k], v_sh[k], "adamw_" + k)
    sm = _adamw(r_small, r_small, me1, _pack_small(w_small), _pack_small(m_small), _pack_small(v_small), "adamw_small")
    loss = sm[0][5, 0]
    sm = [_unpack_small(t) for t in sm]
    for k in SMALL:
        outs[k] = tuple(t[k].reshape(w_small[k].shape) for t in sm)

    order = ("w_in", "conv_w", "a_log", "dt_bias", "onorm_g", "rel_bias", "w_out", "g_mix_pre", "g_mix_post", "w_gate",
             "w_up", "w_down", "g_ffn_pre", "g_ffn_post")
    lead = lambda k, t: tr(k, t)[None] if k in big else t
    res = [loss, grad_x[None]]
    for i in range(4):
        res += [lead(k, outs[k][i]) for k in order]
    return tuple(res)
```

```python
import functools
import math

import numpy as np
import jax
import jax.numpy as jnp
from jax import lax
from jax.experimental import pallas as pl
from jax.experimental.pallas import tpu as pltpu

f32 = jnp.float32
bf16 = jnp.bfloat16
SDS = jax.ShapeDtypeStruct

D_MODEL = 1024
GDN_HEADS = 4
GDN_HD = 128
GDN_W = 512
CHUNK = 64
SWA_HEADS = 8
SWA_HD = 64
SWA_W = 512
D_FF = 2816
IN_COLS = 3592
PATTERNS = ((128, 1), (512, 4), (2048, 16))
SWA_BLK = 128
NUM_BUCKETS = 32
MAX_DISTANCE = 2048
RMS_EPS = 1e-6
NEG = -1e30
N_DEV = 8

COL_A = 0
COL_B = 2048
COL_G = 3584
NCOL = 3712
LANE = 128

ADAM_LR, ADAM_B1, ADAM_B2, ADAM_EPS, ADAM_WD, ADAM_STEP = 0.001, 0.9, 0.999, 1e-08, 0.01, 10

VMEM_LIMIT = 56 * 1024 * 1024

HIGH = lax.Precision.HIGH


def _cp(n_grid=0, vmem=None):
    kw = {}
    if n_grid:
        kw["dimension_semantics"] = ("arbitrary",) * n_grid
    if vmem:
        kw["vmem_limit_bytes"] = vmem
    return pltpu.CompilerParams(**kw)


def _dot(a, b):
    return jnp.dot(a, b, preferred_element_type=f32)


def _dot_nt(a, b):
    return lax.dot_general(a, b, (((1,), (1,)), ((), ())), preferred_element_type=f32)


def _dot_tn(a, b):
    return lax.dot_general(a, b, (((0,), (0,)), ((), ())), preferred_element_type=f32)


def _sigmoid(x):
    return 0.5 * jnp.tanh(0.5 * x) + 0.5


def _softplus(x):
    return jnp.maximum(x, 0.0) + jnp.log(1.0 + jnp.exp(-jnp.abs(x)))


def _const_spec(shape):
    nd = len(shape)
    return pl.BlockSpec(shape, lambda *_: (0,) * nd)


def _resident_spec(shape):
    nd = len(shape)
    return pl.BlockSpec(shape, lambda *_: (0,) * nd, pipeline_mode=pl.Buffered(1))


def _t5_bucket_np(dist):
    max_exact = NUM_BUCKETS // 2
    d = np.maximum(dist, 1).astype(np.float32)
    log_b = max_exact + (np.log(d / np.float32(max_exact)) / np.float32(math.log(MAX_DISTANCE / max_exact))
                         * np.float32(NUM_BUCKETS - max_exact)).astype(np.int32)
    return np.where(dist < max_exact, dist, np.minimum(log_b, NUM_BUCKETS - 1)).astype(np.int32)


def _bucket_tables():
    w = SWA_BLK
    qi = np.arange(w)[:, None]
    kj = np.arange(w)[None, :]
    rel = np.where(kj <= qi, qi - kj, qi + w - kj)
    out = np.zeros((len(PATTERNS), w, w), np.int32)
    for p, (_, dil) in enumerate(PATTERNS):
        steps = _t5_bucket_np(np.arange(w + 1) * dil)
        assert steps[w] == steps[w - 1]
        out[p] = steps[rel]
    return out


def _fill_bias_tables(rb_ref, bk_ref, o_ref):
    for p in range(len(PATTERNS)):
        b_idx = bk_ref[p]
        for h in range(SWA_HEADS):
            def lp(b, acc):
                return jnp.where(b_idx == b, rb_ref[b, h], acc)
            o_ref[p, h] = lax.fori_loop(0, NUM_BUCKETS, lp, jnp.zeros((SWA_BLK, SWA_BLK), f32))


def _rel_bias_grad(dsb, bk, bk_np):
    present = [sorted(set(int(v) for v in np.unique(bk_np[p]))) for p in range(3)]

    def body(ds_ref, bk_ref, o_ref):
        row = lax.broadcasted_iota(jnp.int32, (NUM_BUCKETS, LANE), 0)
        col = lax.broadcasted_iota(jnp.int32, (NUM_BUCKETS, SWA_HEADS), 1)
        out = jnp.zeros((NUM_BUCKETS, SWA_HEADS), f32)
        for hp in range(4):
            for hh in range(2):
                acc = jnp.zeros((NUM_BUCKETS, LANE), f32)
                for p in range(3):
                    tile = ds_ref[hp, p, hh]
                    b_idx = bk_ref[p]
                    for b in present[p]:
                        part = jnp.sum(jnp.where(b_idx == b, tile, 0.0), axis=0, keepdims=True)
                        acc = acc + jnp.where(row == b, part, 0.0)
                tot = jnp.sum(acc, axis=1, keepdims=True)
                out = out + jnp.where(col == 2 * hp + hh, tot, 0.0)
        o_ref[...] = out

    return pl.pallas_call(body, name="rel_bias_grad", out_shape=SDS((NUM_BUCKETS, SWA_HEADS), f32),
                          compiler_params=_cp(0, 32 * 1024 * 1024))(dsb, bk)


def _proj_fwd(x, g_pre, wcat, after, S):
    TS = 512

    def body(x_ref, g_ref, w_ref, after_ref, o_ref, h_ref):
        xv = x_ref[...]
        r = lax.rsqrt(jnp.mean(xv * xv, axis=-1, keepdims=True) + RMS_EPS)
        h = (xv * r * g_ref[...]).astype(bf16)
        h_ref[...] = h
        o_ref[...] = _dot(h, w_ref[...])

    return pl.pallas_call(
        body, name="proj_fwd", grid=(S // TS,),
        in_specs=[pl.BlockSpec((TS, D_MODEL), lambda i: (i, 0)), _const_spec((1, D_MODEL)),
                  _resident_spec((D_MODEL, NCOL)), _ANY],
        out_specs=[pl.BlockSpec((TS, NCOL), lambda i: (i, 0)), pl.BlockSpec((TS, D_MODEL), lambda i: (i, 0))],
        out_shape=[SDS((S, NCOL), f32), SDS((S, D_MODEL), bf16)],
        compiler_params=_cp(1, VMEM_LIMIT),
    )(x, g_pre, wcat, after)


CONV_RT = 256
HALO = 8


CONV_NC = CONV_RT // CHUNK


def _gdn_prep(proj, conv_w, S):
    def body(p_ref, cw_ref, o_ref, xs_ref):
        t = pl.program_id(0)
        xs_ref[pl.ds(0, HALO), :] = jnp.zeros((HALO, LANE), f32)
        xs_ref[pl.ds(HALO, S), :] = p_ref[...]
        w = cw_ref[...]
        scale = jnp.where(t == 0, GDN_HD ** -0.5, 1.0).astype(f32)

        def run(normed):
            def lp(c, carry):
                st = pl.multiple_of(c * CONV_RT, CONV_RT)
                pre = xs_ref[pl.ds(st + HALO - 3, CONV_RT), :] * w[0:1, :]
                for i in range(1, 4):
                    pre = pre + xs_ref[pl.ds(st + HALO - 3 + i, CONV_RT), :] * w[i:i + 1, :]
                out = pre * _sigmoid(pre)
                if normed:
                    out = out * lax.rsqrt(jnp.sum(out * out, axis=-1, keepdims=True) + 1e-6) * scale
                for i in range(CONV_NC):
                    o_ref[0, c * CONV_NC + i, 0] = out[i * CHUNK:(i + 1) * CHUNK]
                return carry

            lax.fori_loop(0, S // CONV_RT, lp, 0)

        pl.when(t < 2)(functools.partial(run, True))
        pl.when(t == 2)(functools.partial(run, False))

    return pl.pallas_call(
        body, name="gdn_prep", grid=(3, GDN_HEADS),
        in_specs=[pl.BlockSpec((S, LANE), lambda t, h: (0, t * GDN_HEADS + h)),
                  pl.BlockSpec((4, LANE), lambda t, h: (0, t * GDN_HEADS + h))],
        out_specs=pl.BlockSpec((1, S // CHUNK, 1, CHUNK, GDN_HD), lambda t, h: (t, 0, h, 0, 0)),
        out_shape=SDS((3, S // CHUNK, GDN_HEADS, CHUNK, GDN_HD), f32),
        scratch_shapes=[pltpu.VMEM((S + HALO, LANE), f32)],
        compiler_params=_cp(2, VMEM_LIMIT),
    )(proj, conv_w)


def _gdn_prep_bwd(proj, conv_w, dqkv, S):
    def body(p_ref, cw_ref, d_ref, dx_ref, dw_ref, xs_ref, dp_ref):
        t = pl.program_id(0)
        xs_ref[pl.ds(0, HALO), :] = jnp.zeros((HALO, LANE), f32)
        xs_ref[pl.ds(HALO, S), :] = p_ref[...]
        dp_ref[pl.ds(S, HALO), :] = jnp.zeros((HALO, LANE), f32)
        w = cw_ref[...]
        scale = jnp.where(t == 0, GDN_HD ** -0.5, 1.0).astype(f32)

        def first_pass(normed):
            def lp1(c, dw):
                st = pl.multiple_of(c * CONV_RT, CONV_RT)
                taps = [xs_ref[pl.ds(st + HALO - 3 + i, CONV_RT), :] for i in range(4)]
                pre = taps[0] * w[0:1, :]
                for i in range(1, 4):
                    pre = pre + taps[i] * w[i:i + 1, :]
                sg = _sigmoid(pre)
                ds = jnp.concatenate([d_ref[0, c * CONV_NC + i, 0] for i in range(CONV_NC)], axis=0)
                if normed:
                    s = pre * sg
                    rn = lax.rsqrt(jnp.sum(s * s, axis=-1, keepdims=True) + 1e-6)
                    n = s * rn
                    dn = ds * scale
                    ds = rn * (dn - n * jnp.sum(dn * n, axis=-1, keepdims=True))
                dpre = ds * (sg * (1.0 + pre * (1.0 - sg)))
                dp_ref[pl.ds(st, CONV_RT), :] = dpre
                return tuple(dw[i] + jnp.sum(dpre * taps[i], axis=0, keepdims=True) for i in range(4))

            z = jnp.zeros((1, LANE), f32)
            dw = lax.fori_loop(0, S // CONV_RT, lp1, (z, z, z, z))
            for i in range(4):
                dw_ref[pl.ds(i, 1), :] = dw[i]

        pl.when(t < 2)(functools.partial(first_pass, True))
        pl.when(t == 2)(functools.partial(first_pass, False))

        def lp2(c, carry):
            st = pl.multiple_of(c * CONV_RT, CONV_RT)
            dx = dp_ref[pl.ds(st, CONV_RT), :] * w[3:4, :]
            for i in range(3):
                dx = dx + dp_ref[pl.ds(st + 3 - i, CONV_RT), :] * w[i:i + 1, :]
            dx_ref[pl.ds(st, CONV_RT), :] = dx.astype(bf16)
            return carry

        lax.fori_loop(0, S // CONV_RT, lp2, 0)

    col = lambda rows: pl.BlockSpec((rows, LANE), lambda t, h: (0, t * GDN_HEADS + h))
    return pl.pallas_call(
        body, name="gdn_prep_bwd", grid=(3, GDN_HEADS),
        in_specs=[col(S), col(4), pl.BlockSpec((1, S // CHUNK, 1, CHUNK, GDN_HD), lambda t, h: (t, 0, h, 0, 0))],
        out_specs=[col(S), col(4)],
        out_shape=[SDS((S, 3 * GDN_W), bf16), SDS((4, 3 * GDN_W), f32)],
        scratch_shapes=[pltpu.VMEM((S + HALO, LANE), f32), pltpu.VMEM((S + HALO, LANE), f32)],
        compiler_params=_cp(2, VMEM_LIMIT),
    )(proj, conv_w, dqkv)


def _bdot(a, b, prec=None):
    return lax.dot_general(a, b, (((2,), (1,)), ((0,), (0,))), precision=prec, preferred_element_type=f32)


def _bdot_nt(a, b, prec=None):
    return lax.dot_general(a, b, (((2,), (2,)), ((0,), (0,))), precision=prec, preferred_element_type=f32)


def _bdot_tn(a, b, prec=None):
    return lax.dot_general(a, b, (((1,), (1,)), ((0,), (0,))), precision=prec, preferred_element_type=f32)


@jax.custom_vjp
def _tri_inv_saved(a, t):
    return t


def _tri_inv_saved_fwd(a, t):
    return t, t


def _tri_inv_saved_bwd(t, dt):
    return -_bdot_tn(t, _bdot_nt(dt, t, HIGH), HIGH), jnp.zeros_like(t)


_tri_inv_saved.defvjp(_tri_inv_saved_fwd, _tri_inv_saved_bwd)


def _gdn_intra(q, k, v, beta, g, t_saved=None):
    nb = q.shape[0]
    c = CHUNK
    ii = lax.broadcasted_iota(jnp.int32, (c, c), 0)
    jj = lax.broadcasted_iota(jnp.int32, (c, c), 1)
    eye = ii == jj
    tril = ii >= jj
    strict = ii > jj
    ones = jnp.ones((nb, c, c), f32)
    eye_f = eye.astype(f32)

    g_row = _bdot(ones, jnp.where(eye, g, 0.0), HIGH)
    gc = jnp.sum(jnp.where(tril, g_row, 0.0), axis=2, keepdims=True)
    gc_row = _bdot(ones, jnp.where(eye, gc, 0.0), HIGH)
    decay = jnp.where(tril, jnp.exp(jnp.where(tril, gc - gc_row, 0.0)), 0.0)
    last = lax.broadcasted_iota(jnp.int32, (c, 1), 0) == c - 1
    gc_last = jnp.sum(jnp.where(last, gc, 0.0), axis=1, keepdims=True)
    e_gc = jnp.exp(gc)

    kb = k * beta
    k16 = k.astype(bf16)
    a = jnp.where(strict, _bdot_nt(kb.astype(bf16), k16) * decay, 0.0)
    if t_saved is None:
        xp = -a
        t_inv = eye_f + xp
        for level in range(5):
            if level < 1:
                xp = _bdot(xp, xp, HIGH)
                t_inv = t_inv + _bdot(t_inv, xp, HIGH)
            else:
                x16 = xp.astype(bf16)
                xp = _bdot(x16, x16)
                t_inv = t_inv + _bdot(t_inv.astype(bf16), xp.astype(bf16))
    else:
        t_inv = _tri_inv_saved(a, t_saved)
    t16 = t_inv.astype(bf16)
    u = _bdot(t16, (v * beta).astype(bf16))
    w = _bdot(t16, (kb * e_gc).astype(bf16))
    attn = jnp.where(tril, _bdot_nt(q.astype(bf16), k16) * decay, 0.0)
    gam = jnp.broadcast_to(jnp.exp(gc_last), (nb, 1, GDN_HD))
    return u, w, attn, q * e_gc, k * jnp.exp(gc_last - gc), gam, t_inv


GDN_TB = 512
GDN_NC = GDN_TB // CHUNK
GDN_NU = GDN_NC * GDN_HEADS


def _gdn_gates(pg_ref, al_ref, db_ref):
    lane1 = lax.broadcasted_iota(jnp.int32, (1, LANE), 1)
    a_lane = jnp.zeros((1, LANE), f32)
    b_lane = jnp.zeros((1, LANE), f32)
    for h in range(GDN_HEADS):
        a_lane = jnp.where(lane1 == GDN_HEADS + h, al_ref[0, h], a_lane)
        b_lane = jnp.where(lane1 == GDN_HEADS + h, db_ref[0, h], b_lane)
    pg = pg_ref[...]
    z = pg + b_lane
    return _sigmoid(pg), -jnp.exp(a_lane) * _softplus(z), z, a_lane


def _gdn_unit_inputs(qkv_ref, beta_all, g_all):
    units = [(cl, h) for cl in range(GDN_NC) for h in range(GDN_HEADS)]
    beta = jnp.stack([beta_all[cl * CHUNK:(cl + 1) * CHUNK, h:h + 1] for cl, h in units])
    g = jnp.stack([g_all[cl * CHUNK:(cl + 1) * CHUNK, GDN_HEADS + h:GDN_HEADS + h + 1] for cl, h in units])
    return qkv_ref[0], qkv_ref[1], qkv_ref[2], beta, g


def _unit_spec(*tail):
    nd = len(tail)
    return pl.BlockSpec((GDN_NU,) + tail, lambda i: (i,) + (0,) * nd)


def _gdn_intra_shapes(S):
    nu = S // CHUNK * GDN_HEADS
    row = SDS((nu, CHUNK, GDN_HD), f32)
    return [row, row, SDS((nu, CHUNK, CHUNK), f32), row, row, SDS((nu, 1, GDN_HD), f32)]


_GDN_INTRA_SPECS = lambda: [_unit_spec(CHUNK, GDN_HD), _unit_spec(CHUNK, GDN_HD), _unit_spec(CHUNK, CHUNK),
                            _unit_spec(CHUNK, GDN_HD), _unit_spec(CHUNK, GDN_HD), _unit_spec(1, GDN_HD)]


def _gdn_intra_fwd(qkv_u, proj, a_log, dt_bias, S):
    def body(qkv_ref, pg_ref, al_ref, db_ref, *outs):
        beta_all, g_all, _, _ = _gdn_gates(pg_ref, al_ref, db_ref)
        res = _gdn_intra(*_gdn_unit_inputs(qkv_ref, beta_all, g_all))
        for o_ref, r in zip(outs, res):
            o_ref[...] = r

    nu = S // CHUNK * GDN_HEADS
    *intra, t_inv = pl.pallas_call(
        body, name="gdn_intra_fwd", grid=(S // GDN_TB,),
        in_specs=[pl.BlockSpec((3, GDN_NU, CHUNK, GDN_HD), lambda i: (0, i, 0, 0)),
                  pl.BlockSpec((GDN_TB, LANE), lambda i: (i, COL_G // LANE)),
                  pl.BlockSpec(memory_space=pltpu.SMEM), pl.BlockSpec(memory_space=pltpu.SMEM)],
        out_specs=_GDN_INTRA_SPECS() + [_unit_spec(CHUNK, CHUNK)],
        out_shape=_gdn_intra_shapes(S) + [SDS((nu, CHUNK, CHUNK), f32)],
        compiler_params=_cp(1, VMEM_LIMIT),
    )(qkv_u, proj, a_log, dt_bias)
    return intra, t_inv


def _gdn_intra_bwd(qkv_u, proj, a_log, dt_bias, t_inv, cots, S):
    def body(qkv_ref, pg_ref, al_ref, db_ref, t_ref, du_ref, dw_ref, da_ref, dqd_ref, dkd_ref, dgm_ref,
             dqkv_ref, dpg_ref, dal_ref, ddb_ref):
        @pl.when(pl.program_id(0) == 0)
        def _():
            dal_ref[...] = jnp.zeros_like(dal_ref)
            ddb_ref[...] = jnp.zeros_like(ddb_ref)

        t_saved = t_ref[...]
        beta_all, g_all, z, a_lane = _gdn_gates(pg_ref, al_ref, db_ref)
        _, vjp = jax.vjp(lambda *a: _gdn_intra(*a, t_saved=t_saved)[:6], *_gdn_unit_inputs(qkv_ref, beta_all, g_all))
        dq, dk, dv, dbeta, dg = vjp((du_ref[...], dw_ref[...], da_ref[...], dqd_ref[...], dkd_ref[...], dgm_ref[...]))
        dqkv_ref[0] = dq
        dqkv_ref[1] = dk
        dqkv_ref[2] = dv
        lane = lax.broadcasted_iota(jnp.int32, (CHUNK, LANE), 1)
        rows = []
        for cl in range(GDN_NC):
            t = jnp.zeros((CHUNK, LANE), f32)
            for h in range(GDN_HEADS):
                b = cl * GDN_HEADS + h
                t = t + jnp.where(lane == h, dbeta[b], 0.0) + jnp.where(lane == GDN_HEADS + h, dg[b], 0.0)
            rows.append(t)
        d_all = jnp.concatenate(rows, axis=0)
        is_beta = lax.broadcasted_iota(jnp.int32, (GDN_TB, LANE), 1) < GDN_HEADS
        dz = d_all * (-jnp.exp(a_lane)) * _sigmoid(z)
        dpg_ref[...] = jnp.where(is_beta, d_all * beta_all * (1.0 - beta_all), dz).astype(bf16)
        dal_ref[...] += jnp.sum(jnp.where(is_beta, 0.0, d_all * g_all), axis=0, keepdims=True)
        ddb_ref[...] += jnp.sum(jnp.where(is_beta, 0.0, dz), axis=0, keepdims=True)

    acc = _const_spec((1, LANE))
    nu = S // CHUNK * GDN_HEADS
    return pl.pallas_call(
        body, name="gdn_intra_bwd", grid=(S // GDN_TB,),
        in_specs=[pl.BlockSpec((3, GDN_NU, CHUNK, GDN_HD), lambda i: (0, i, 0, 0)),
                  pl.BlockSpec((GDN_TB, LANE), lambda i: (i, COL_G // LANE)),
                  pl.BlockSpec(memory_space=pltpu.SMEM), pl.BlockSpec(memory_space=pltpu.SMEM),
                  _unit_spec(CHUNK, CHUNK)] + _GDN_INTRA_SPECS(),
        out_specs=[pl.BlockSpec((3, GDN_NU, CHUNK, GDN_HD), lambda i: (0, i, 0, 0)),
                   pl.BlockSpec((GDN_TB, LANE), lambda i: (i, 0)), acc, acc],
        out_shape=[SDS((3, nu, CHUNK, GDN_HD), f32), SDS((S, LANE), bf16), SDS((1, LANE), f32), SDS((1, LANE), f32)],
        compiler_params=_cp(1, VMEM_LIMIT),
    )(qkv_u, proj, a_log, dt_bias, t_inv, *cots)


def _gdn_scan_fwd(intra, proj, onorm_g, S):
    def body(u_ref, w_ref, at_ref, qd_ref, kd_ref, gm_ref, gate_ref, og_ref, out_ref, st_ref, s_scr):
        @pl.when(pl.program_id(0) == 0)
        def _():
            s_scr[...] = jnp.zeros_like(s_scr)

        og = og_ref[...]
        s = s_scr[...]
        def out_mm(us, s16, vn16):
            return _bdot(qd_ref[us].astype(bf16), s16) + _bdot(at_ref[us].astype(bf16), vn16)

        outs, prev = [], None
        for cl in range(GDN_NC):
            us = slice(cl * GDN_HEADS, (cl + 1) * GDN_HEADS)
            st_ref[us] = s
            s16 = s.astype(bf16)
            ws = _bdot(w_ref[us].astype(bf16), s16)
            if prev is not None:
                outs.append(out_mm(*prev))
            vn16 = (u_ref[us] - ws).astype(bf16)
            prev = (us, s16, vn16)
            s = s * gm_ref[us] + _bdot_tn(kd_ref[us].astype(bf16), vn16)
        outs.append(out_mm(*prev))
        s_scr[...] = s
        for cl, o in enumerate(outs):
            rows = slice(cl * CHUNK, (cl + 1) * CHUNK)
            for h in range(GDN_HEADS):
                oh = o[h]
                gt = gate_ref[rows, h * GDN_HD:(h + 1) * GDN_HD]
                on = oh * lax.rsqrt(jnp.mean(oh * oh, axis=-1, keepdims=True) + RMS_EPS) * og
                out_ref[rows, h * GDN_HD:(h + 1) * GDN_HD] = on * (gt * _sigmoid(gt))

    nu = S // CHUNK * GDN_HEADS
    return pl.pallas_call(
        body, name="gdn_scan_fwd", grid=(S // GDN_TB,),
        in_specs=_GDN_INTRA_SPECS() + [pl.BlockSpec((GDN_TB, GDN_W), lambda i: (i, 3)), _const_spec((1, GDN_HD))],
        out_specs=[pl.BlockSpec((GDN_TB, GDN_W), lambda i: (i, 0)), _unit_spec(GDN_HD, GDN_HD)],
        out_shape=[SDS((S, GDN_W), f32), SDS((nu, GDN_HD, GDN_HD), f32)],
        scratch_shapes=[pltpu.VMEM((GDN_HEADS, GDN_HD, GDN_HD), f32)],
        compiler_params=_cp(1, VMEM_LIMIT),
    )(*intra, proj, onorm_g)


def _gdn_scan_bwd(intra, states, proj, d_oab, onorm_g, after, S):
    n_steps = S // GDN_TB

    def body(u_ref, w_ref, at_ref, qd_ref, kd_ref, gm_ref, st_ref, gate_ref, do_ref, og_ref, after_ref,
             du_ref, dw_ref, dat_ref, dqd_ref, dkd_ref, dgm_ref, dgate_ref, dog_ref, ds_scr):
        @pl.when(pl.program_id(0) == 0)
        def _():
            ds_scr[...] = jnp.zeros_like(ds_scr)
            dog_ref[...] = jnp.zeros_like(dog_ref)

        og = og_ref[...]
        ii = lax.broadcasted_iota(jnp.int32, (CHUNK, CHUNK), 0)
        jj = lax.broadcasted_iota(jnp.int32, (CHUNK, CHUNK), 1)
        tril = ii >= jj
        dog = jnp.zeros((1, GDN_HD), f32)
        pre = []
        for cl in range(GDN_NC):
            us = slice(cl * GDN_HEADS, (cl + 1) * GDN_HEADS)
            rows = slice(cl * CHUNK, (cl + 1) * CHUNK)
            s016 = st_ref[us].astype(bf16)
            w16 = w_ref[us].astype(bf16)
            qd16 = qd_ref[us].astype(bf16)
            at16 = at_ref[us].astype(bf16)
            vn16 = (u_ref[us] - _bdot(w16, s016)).astype(bf16)
            o = _bdot(qd16, s016) + _bdot(at16, vn16)
            do_h = []
            for h in range(GDN_HEADS):
                oh = o[h]
                lanes = slice(h * GDN_HD, (h + 1) * GDN_HD)
                gt = gate_ref[rows, lanes]
                d_out = do_ref[rows, lanes]
                r = lax.rsqrt(jnp.mean(oh * oh, axis=-1, keepdims=True) + RMS_EPS)
                n = oh * r
                sg = _sigmoid(gt)
                silu = gt * sg
                dog = dog + jnp.sum(d_out * n * silu, axis=0, keepdims=True)
                dgate_ref[rows, lanes] = (d_out * n * og * (sg * (1.0 + gt * (1.0 - sg)))).astype(bf16)
                dn = d_out * og * silu
                do_h.append(r * (dn - n * jnp.mean(dn * n, axis=-1, keepdims=True)))
            do16 = jnp.stack(do_h).astype(bf16)
            pre.append((us, s016, w16, vn16, do16, _bdot_tn(at16, do16), _bdot_tn(qd16, do16)))
        ds = ds_scr[...]
        chain = [None] * GDN_NC
        for cl in reversed(range(GDN_NC)):
            us, s016, w16, vn16, do16, at_do, qd_do = pre[cl]
            ds16 = ds.astype(bf16)
            dvn = at_do + _bdot(kd_ref[us].astype(bf16), ds16)
            dvn16 = dvn.astype(bf16)
            chain[cl] = (ds, ds16, dvn, dvn16)
            ds = qd_do + ds * gm_ref[us] - _bdot_tn(w16, dvn16)
        ds_scr[...] = ds
        for cl in range(GDN_NC):
            us, s016, w16, vn16, do16, _, _ = pre[cl]
            ds_in, ds16, dvn, dvn16 = chain[cl]
            du_ref[us] = dvn
            dw_ref[us] = -_bdot_nt(dvn16, s016)
            dat_ref[us] = jnp.where(tril, _bdot_nt(do16, vn16), 0.0)
            dqd_ref[us] = _bdot_nt(do16, s016)
            dkd_ref[us] = _bdot_nt(vn16, ds16)
            dgm_ref[us] = jnp.sum(st_ref[us] * ds_in, axis=1, keepdims=True)
        dog_ref[...] += dog

    def unit(*tail):
        nd = len(tail)
        return pl.BlockSpec((GDN_NU,) + tail, lambda i: (n_steps - 1 - i,) + (0,) * nd)

    intra_specs = [unit(CHUNK, GDN_HD), unit(CHUNK, GDN_HD), unit(CHUNK, CHUNK), unit(CHUNK, GDN_HD),
                   unit(CHUNK, GDN_HD), unit(1, GDN_HD)]
    tok = lambda c: pl.BlockSpec((GDN_TB, GDN_W), lambda i: (n_steps - 1 - i, c))
    return pl.pallas_call(
        body, name="gdn_scan_bwd", grid=(n_steps,),
        in_specs=intra_specs + [unit(GDN_HD, GDN_HD), tok(3), tok(0), _const_spec((1, GDN_HD)), _ANY],
        out_specs=intra_specs + [tok(0), _const_spec((1, GDN_HD))],
        out_shape=_gdn_intra_shapes(S) + [SDS((S, GDN_W), bf16), SDS((1, GDN_HD), f32)],
        scratch_shapes=[pltpu.VMEM((GDN_HEADS, GDN_HD, GDN_HD), f32)],
        compiler_params=_cp(1, VMEM_LIMIT),
    )(*intra, states, proj, d_oab, onorm_g, after)


SWA_UNROLL = 8


def _swa_tiles(it, d, nb_log2):
    nb = 1 << nb_log2
    r = lax.shift_right_logical(it, nb_log2)
    blk = lax.bitwise_and(it, nb - 1)
    qs = blk * (SWA_BLK * d) + r
    ps = jnp.maximum(blk - 1, 0) * (SWA_BLK * d) + r
    if d > 1:
        rows_c, rows_p = pl.ds(qs, SWA_BLK, stride=d), pl.ds(ps, SWA_BLK, stride=d)
    else:
        rows_c, rows_p = pl.ds(pl.multiple_of(qs, SWA_BLK), SWA_BLK), pl.ds(pl.multiple_of(ps, SWA_BLK), SWA_BLK)
    return rows_c, rows_p, blk > 0


def _swa_prev_modes(nb):
    if nb >= SWA_UNROLL:
        return ["load"] + ["reuse"] * (SWA_UNROLL - 1)
    return ["none" if u % nb == 0 else "reuse" for u in range(SWA_UNROLL)]


def _swa_fwd(proj, bt, S):
    scale = SWA_HD ** -0.5

    def body(q_ref, k_ref, v_ref, bt_ref, o_ref, lse0_ref, lse1_ref, m0_scr, m1_scr, a0_scr, a1_scr):
        lane = lax.broadcasted_iota(jnp.int32, (SWA_BLK, LANE), 1)
        h0 = lane < SWA_HD
        qi = lax.broadcasted_iota(jnp.int32, (SWA_BLK, SWA_BLK), 0)
        kj = lax.broadcasted_iota(jnp.int32, (SWA_BLK, SWA_BLK), 1)
        lower = kj <= qi
        ones16 = jnp.ones((LANE, SWA_BLK), bf16)
        m_scrs = (m0_scr, m1_scr)
        a_scrs = (a0_scr, a1_scr)
        for p, (_, d) in reversed(list(enumerate(PATTERNS))):
            nb_log2 = int(math.log2(S // d // SWA_BLK))
            first = p == len(PATTERNS) - 1

            def lp(i, carry, p=p, d=d, nb_log2=nb_log2, first=first):
                heads = [h0, jnp.logical_not(h0)]
                modes = _swa_prev_modes(1 << nb_log2)
                tiles = []
                kc_f = None
                for u in range(SWA_UNROLL):
                    rows_c, rows_p, has_prev = _swa_tiles(i * SWA_UNROLL + u, d, nb_log2)
                    kp_f = {"load": lambda: k_ref[rows_p, :], "reuse": lambda: kc_f, "none": lambda: None}[modes[u]]()
                    has_prev = {"load": has_prev, "reuse": True, "none": False}[modes[u]]
                    q = q_ref[rows_c, :]
                    kc_f = k_ref[rows_c, :]
                    kc = kc_f.astype(bf16)
                    logits = []
                    for mh in heads:
                        q_h = jnp.where(mh, q, 0.0)
                        qh = q_h.astype(bf16)
                        if kp_f is None:
                            logits.append((_dot_nt(qh, kc), None, None))
                        else:
                            logits.append((_dot_nt(qh, kc), _dot_nt(qh, kp_f.astype(bf16)), _dot((q_h * kp_f).astype(bf16), ones16)))
                    tiles.append((rows_c, rows_p, has_prev, logits))
                probs = []
                for rows_c, rows_p, has_prev, logits in tiles:
                    per_head = []
                    for h, (s_c, s_p, far) in enumerate(logits):
                        if has_prev is False:
                            s = jnp.where(lower, s_c * scale + bt_ref[p, h], NEG)
                            s_far = None
                        else:
                            s = jnp.where(lower, s_c, s_p) * scale + bt_ref[p, h]
                            s_far = far * scale + bt_ref[p, h, SWA_BLK - 1:SWA_BLK, 0:1]
                            if has_prev is not True:
                                s = jnp.where(jnp.logical_or(lower, has_prev), s, NEG)
                                s_far = jnp.where(has_prev, s_far, NEG)
                        mn = jnp.max(s, axis=1, keepdims=True)
                        if s_far is not None:
                            mn = jnp.maximum(s_far, mn)
                        alpha = None
                        if not first:
                            mo = m_scrs[h][rows_c, :]
                            mn = jnp.maximum(mo, mn)
                            alpha = jnp.exp(mo - mn)
                        mn = jnp.broadcast_to(mn, (SWA_BLK, LANE))
                        pm = jnp.exp(s - mn)
                        per_head.append((mn, alpha, None if s_far is None else jnp.exp(s_far - mn),
                                         jnp.where(lower, pm, 0.0).astype(bf16),
                                         None if s_far is None else jnp.where(lower, 0.0, pm).astype(bf16)))
                    probs.append(per_head)
                acc_old = [None if first else (a0_scr[t[0], :], a1_scr[t[0], :]) for t in tiles]
                done = []
                vc = None
                for u, ((rows_c, rows_p, _, _), per_head, old) in enumerate(zip(tiles, probs, acc_old)):
                    vp = {"load": lambda: v_ref[rows_p, :], "reuse": lambda: vc, "none": lambda: None}[modes[u]]()
                    vc = v_ref[rows_c, :]
                    acc_new = []
                    for h, (mn, alpha, p_far, pc16, pp16) in enumerate(per_head):
                        pv = _dot(pc16, jnp.where(heads[h], vc, 1.0).astype(bf16))
                        if pp16 is not None:
                            vpa = jnp.where(heads[h], vp, 1.0)
                            pv = pv + _dot(pp16, vpa.astype(bf16)) + p_far * vpa
                        acc_new.append(pv if first else alpha * old[h] + pv)
                    done.append((rows_c, per_head[0][0], per_head[1][0], acc_new[0], acc_new[1]))
                for rows_c, m0_new, m1_new, a0_new, a1_new in done:
                    m0_scr[rows_c, :] = m0_new
                    m1_scr[rows_c, :] = m1_new
                    a0_scr[rows_c, :] = a0_new
                    a1_scr[rows_c, :] = a1_new
                return carry

            lax.fori_loop(0, S // SWA_BLK // SWA_UNROLL, lp, 0)

        def fin(c, carry):
            rows = pl.ds(pl.multiple_of(c * SWA_BLK, SWA_BLK), SWA_BLK)
            a0 = a0_scr[rows, :]
            a1 = a1_scr[rows, :]
            l0 = jnp.where(h0, pltpu.roll(a0, SWA_HD, 1), a0)
            l1 = jnp.where(h0, a1, pltpu.roll(a1, SWA_HD, 1))
            o_ref[rows, :] = jnp.where(h0, a0 / l0, a1 / l1)
            lse0_ref[rows, :] = m0_scr[rows, :] + jnp.log(l0)
            lse1_ref[rows, :] = m1_scr[rows, :] + jnp.log(l1)
            return carry

        lax.fori_loop(0, S // SWA_BLK, fin, 0)

    qb = COL_B // LANE
    col = lambda c: pl.BlockSpec((S, LANE), lambda hp, c=c: (0, c + hp))
    return pl.pallas_call(
        body, name="swa_fwd", grid=(4,),
        in_specs=[col(qb), col(qb + 4), col(qb + 8), pl.BlockSpec((3, 2, SWA_BLK, SWA_BLK), lambda hp: (0, hp, 0, 0))],
        out_specs=[col(0), col(0), col(0)],
        out_shape=[SDS((S, SWA_W), f32)] * 3,
        scratch_shapes=[pltpu.VMEM((S, LANE), f32)] * 4,
        compiler_params=_cp(1, VMEM_LIMIT),
    )(proj, proj, proj, bt)


def _swa_bwd(proj, bt, nd, lse0, lse1, d_oab, after, S):
    scale = SWA_HD ** -0.5

    def body(q_ref, k_ref, v_ref, bt_ref, nd_scr, lse0_ref, lse1_ref, do_ref, after_ref, dq_ref, dk_ref, dv_ref, dsb_ref,
             dq_scr, dk_scr, dv_scr):
        lane = lax.broadcasted_iota(jnp.int32, (SWA_BLK, LANE), 1)
        h0 = lane < SWA_HD
        qi = lax.broadcasted_iota(jnp.int32, (SWA_BLK, SWA_BLK), 0)
        kj = lax.broadcasted_iota(jnp.int32, (SWA_BLK, SWA_BLK), 1)
        lower = kj <= qi
        eye = kj == qi
        rel127 = jnp.logical_or(kj == qi + 1, jnp.logical_and(qi == SWA_BLK - 1, kj == 0))
        ones16 = jnp.ones((LANE, SWA_BLK), bf16)
        lse_refs = (lse0_ref, lse1_ref)
        dk_scr[...] = jnp.zeros((S, LANE), f32)
        dv_scr[...] = jnp.zeros((S, LANE), f32)
        dsb_ref[...] = jnp.zeros_like(dsb_ref)

        for p, (_, d) in reversed(list(enumerate(PATTERNS))):
            nb_log2 = int(math.log2(S // d // SWA_BLK))
            first = p == len(PATTERNS) - 1

            def lp(i, carry, p=p, d=d, nb_log2=nb_log2, first=first):
                heads = [h0, jnp.logical_not(h0)]
                modes = _swa_prev_modes(1 << nb_log2)
                tiles = []
                kc_f = vc_f = None
                for u in range(SWA_UNROLL):
                    rows_c, rows_p, has_prev = _swa_tiles(i * SWA_UNROLL + u, d, nb_log2)
                    kp_f = {"load": lambda: k_ref[rows_p, :], "reuse": lambda: kc_f, "none": lambda: None}[modes[u]]()
                    vp_f = {"load": lambda: v_ref[rows_p, :], "reuse": lambda: vc_f, "none": lambda: None}[modes[u]]()
                    has_prev = {"load": has_prev, "reuse": True, "none": False}[modes[u]]
                    q = q_ref[rows_c, :]
                    kc_f = k_ref[rows_c, :]
                    vc_f = v_ref[rows_c, :]
                    kc = kc_f.astype(bf16)
                    kp = None if kp_f is None else kp_f.astype(bf16)
                    do = do_ref[rows_c, :]
                    nd = nd_scr[rows_c, :]
                    per_head = []
                    for mh in heads:
                        q_h = jnp.where(mh, q, 0.0)
                        do_a = jnp.where(mh, do, nd)
                        qh = q_h.astype(bf16)
                        doa = do_a.astype(bf16)
                        doh = jnp.where(mh, do, 0.0).astype(bf16)
                        dd_c = _dot_nt(doa, jnp.where(mh, vc_f, 1.0).astype(bf16))
                        if kp_f is None:
                            per_head.append((qh, doh, _dot_nt(qh, kc), None, None, dd_c, None, None))
                        else:
                            vpa = jnp.where(mh, vp_f, 1.0)
                            per_head.append((qh, doh, _dot_nt(qh, kc), _dot_nt(qh, kp), _dot((q_h * kp_f).astype(bf16), ones16),
                                             dd_c, _dot_nt(doa, vpa.astype(bf16)), _dot((do_a * vpa).astype(bf16), ones16)))
                    tiles.append((rows_c, rows_p, has_prev, kc, kp, per_head))
                grads = []
                for rows_c, rows_p, has_prev, kc, kp, per_head in tiles:
                    out = []
                    for h, (qh, doh, s_c, s_p, far, dd_c, dd_p, dd_far) in enumerate(per_head):
                        lse_h = lse_refs[h][rows_c, :]
                        if has_prev is False:
                            pm = jnp.exp(jnp.where(lower, s_c * scale + bt_ref[p, h], NEG) - lse_h)
                            dsm = pm * dd_c
                            out.append((dsm, dsm.astype(bf16), None, pm.astype(bf16), None))
                            continue
                        s = jnp.where(lower, s_c, s_p) * scale + bt_ref[p, h]
                        s_far = far * scale + bt_ref[p, h, SWA_BLK - 1:SWA_BLK, 0:1]
                        if has_prev is not True:
                            s = jnp.where(jnp.logical_or(lower, has_prev), s, NEG)
                            s_far = jnp.where(has_prev, s_far, NEG)
                        pm = jnp.exp(s - lse_h)
                        p_far = jnp.exp(s_far - lse_h)
                        dsm = pm * jnp.where(lower, dd_c, dd_p)
                        ds_far = p_far * dd_far
                        out.append((dsm + jnp.where(rel127, ds_far, 0.0),
                                    jnp.where(lower, dsm, 0.0).astype(bf16),
                                    jnp.where(lower, jnp.where(eye, ds_far, 0.0), dsm).astype(bf16),
                                    jnp.where(lower, pm, 0.0).astype(bf16),
                                    jnp.where(lower, jnp.where(eye, p_far, 0.0), pm).astype(bf16)))
                    grads.append(out)
                done = []
                add = lambda acc, t: t if acc is None else acc + t
                for (rows_c, rows_p, _, kc, kp, per_head), out in zip(tiles, grads):
                    dq_t = dkc_t = dkp_t = dvc_t = dvp_t = None
                    for h, (_, dsc16, dsp16, pc16, pp16) in enumerate(out):
                        qh, doh = per_head[h][0], per_head[h][1]
                        dq_h = _dot(dsc16, kc)
                        dkc_t = add(dkc_t, _dot_tn(dsc16, qh) * scale)
                        dvc_t = add(dvc_t, _dot_tn(pc16, doh))
                        if dsp16 is not None:
                            dq_h = dq_h + _dot(dsp16, kp)
                            dkp_t = add(dkp_t, _dot_tn(dsp16, qh) * scale)
                            dvp_t = add(dvp_t, _dot_tn(pp16, doh))
                        dq_t = add(dq_t, jnp.where(heads[h], dq_h * scale, 0.0))
                    done.append([rows_c, rows_p, dq_t, dkc_t, dkp_t, dvc_t, dvp_t])
                for u in range(1, SWA_UNROLL):
                    if modes[u] == "reuse":
                        done[u - 1][3] = done[u - 1][3] + done[u][4]
                        done[u - 1][5] = done[u - 1][5] + done[u][6]
                for h in range(2):
                    tot = grads[0][h][0]
                    for g in grads[1:]:
                        tot = tot + g[h][0]
                    dsb_ref[0, p, h] += tot
                for u, (rows_c, rows_p, dq_t, dkc_t, dkp_t, dvc_t, dvp_t) in enumerate(done):
                    dq_scr[rows_c, :] = dq_t if first else dq_scr[rows_c, :] + dq_t
                    dk_scr[rows_c, :] = dk_scr[rows_c, :] + dkc_t
                    dv_scr[rows_c, :] = dv_scr[rows_c, :] + dvc_t
                    if modes[u] == "load":
                        dk_scr[rows_p, :] = dk_scr[rows_p, :] + dkp_t
                        dv_scr[rows_p, :] = dv_scr[rows_p, :] + dvp_t
                return carry

            lax.fori_loop(0, S // SWA_BLK // SWA_UNROLL, lp, 0)
        dq_ref[...] = dq_scr[...].astype(bf16)
        dk_ref[...] = dk_scr[...].astype(bf16)
        dv_ref[...] = dv_scr[...].astype(bf16)

    qb = COL_B // LANE
    col = lambda c: pl.BlockSpec((S, LANE), lambda hp, c=c: (0, c + hp))
    return pl.pallas_call(
        body, name="swa_bwd", grid=(4,),
        in_specs=[col(qb), col(qb + 4), col(qb + 8),
                  pl.BlockSpec((3, 2, SWA_BLK, SWA_BLK), lambda hp: (0, hp, 0, 0)),
                  col(0), col(0), col(0), col(4), _ANY],
        out_specs=[col(0), col(0), col(0),
                   pl.BlockSpec((1, 3, 2, SWA_BLK, SWA_BLK), lambda hp: (hp, 0, 0, 0, 0))],
        out_shape=[SDS((S, SWA_W), bf16)] * 3 + [SDS((4, 3, 2, SWA_BLK, SWA_BLK), f32)],
        scratch_shapes=[pltpu.VMEM((S, LANE), f32)] * 3,
        compiler_params=_cp(1, VMEM_LIMIT),
    )(proj, proj, proj, bt, nd, lse0, lse1, d_oab, after)


def _mix_fwd(oa, ob, w_out, x, g_post, S):
    TS = 1024

    def body(oa_ref, ob_ref, w_ref, x_ref, g_ref, mix_ref, x1_ref):
        mix = _dot(oa_ref[...].astype(bf16), w_ref[0:GDN_W, :]) + _dot(ob_ref[...].astype(bf16), w_ref[GDN_W:D_MODEL, :])
        r = lax.rsqrt(jnp.mean(mix * mix, axis=-1, keepdims=True) + RMS_EPS)
        mix_ref[...] = mix
        x1_ref[...] = x_ref[...] + mix * r * g_ref[...]

    row = lambda w: pl.BlockSpec((TS, w), lambda i: (i, 0))
    return pl.pallas_call(
        body, name="mix_fwd", grid=(S // TS,),
        in_specs=[row(GDN_W), row(SWA_W), _resident_spec((D_MODEL, D_MODEL)), row(D_MODEL), _const_spec((1, D_MODEL))],
        out_specs=[row(D_MODEL), row(D_MODEL)],
        out_shape=[SDS((S, D_MODEL), f32), SDS((S, D_MODEL), f32)],
        compiler_params=_cp(1, VMEM_LIMIT),
    )(oa, ob, w_out, x, g_post)


def _mix_bwd(dx1, mix, g_post, w_out, ob, S):
    TS = 512

    def body(dx1_ref, mix_ref, g_ref, w_ref, ob_ref, dmix_ref, doab_ref, dg_ref, nd_ref):
        @pl.when(pl.program_id(0) == 0)
        def _():
            dg_ref[...] = jnp.zeros_like(dg_ref)

        mix = mix_ref[...]
        dz = dx1_ref[...]
        r = lax.rsqrt(jnp.mean(mix * mix, axis=-1, keepdims=True) + RMS_EPS)
        n = mix * r
        dg_ref[...] += jnp.sum(dz * n, axis=0, keepdims=True)
        dn = dz * g_ref[...]
        dmix = (r * (dn - n * jnp.mean(dn * n, axis=-1, keepdims=True))).astype(bf16)
        dmix_ref[...] = dmix
        doab = _dot_nt(dmix, w_ref[...])
        doab_ref[...] = doab
        hi_ = lax.shift_right_logical(lax.broadcasted_iota(jnp.int32, (SWA_W, SWA_W), 0), 6)
        hj_ = lax.shift_right_logical(lax.broadcasted_iota(jnp.int32, (SWA_W, SWA_W), 1), 6)
        swap = (hi_ == lax.bitwise_xor(hj_, 1)).astype(bf16)
        dlt = doab[:, GDN_W:] * ob_ref[...]
        hi = dlt.astype(bf16)
        nd_ref[...] = (_dot(hi, swap) + _dot((dlt - hi.astype(f32)).astype(bf16), swap)) * (-1.0 / SWA_HD)

    row = lambda w=D_MODEL: pl.BlockSpec((TS, w), lambda i: (i, 0))
    return pl.pallas_call(
        body, name="mix_bwd", grid=(S // TS,),
        in_specs=[row(), row(), _const_spec((1, D_MODEL)), _resident_spec((D_MODEL, D_MODEL)), row(SWA_W)],
        out_specs=[row(), row(), _const_spec((1, D_MODEL)), row(SWA_W)],
        out_shape=[SDS((S, D_MODEL), bf16), SDS((S, D_MODEL), f32), SDS((1, D_MODEL), f32), SDS((S, SWA_W), f32)],
        compiler_params=_cp(1, VMEM_LIMIT),
    )(dx1, mix, g_post, w_out, ob)


FFN_TS = 256
FFN_CH = 1408


def _ffn(x1, tgt, g_pre, g_post, wg, wu, wd, S):
    def body(x1_ref, t_ref, gp_ref, gq_ref, wg_ref, wu_ref, wd_ref,
             dx1_ref, h2_ref, act_ref, dgate_ref, dup_ref, df_ref, loss_ref, dgp_ref, dgq_ref, gate_scr, up_scr):
        @pl.when(pl.program_id(0) == 0)
        def _():
            loss_ref[...] = jnp.zeros_like(loss_ref)
            dgp_ref[...] = jnp.zeros_like(dgp_ref)
            dgq_ref[...] = jnp.zeros_like(dgq_ref)

        x1v = x1_ref[...]
        gp = gp_ref[...]
        gq = gq_ref[...]
        r2 = lax.rsqrt(jnp.mean(x1v * x1v, axis=-1, keepdims=True) + RMS_EPS)
        n2 = x1v * r2
        h2 = (n2 * gp).astype(bf16)
        h2_ref[...] = h2
        chunks = [slice(c * FFN_CH, (c + 1) * FFN_CH) for c in range(D_FF // FFN_CH)]
        for cs in chunks:
            gate_scr[:, cs] = _dot_nt(h2, wg_ref[cs, :])
            up_scr[:, cs] = _dot_nt(h2, wu_ref[cs, :])
        acts = []
        for cs in chunks:
            gate = gate_scr[:, cs]
            act = (gate * _sigmoid(gate) * up_scr[:, cs]).astype(bf16)
            act_ref[:, cs] = act
            acts.append(act)
        f = _dot(acts[0], wd_ref[chunks[0], :])
        for act, cs in zip(acts[1:], chunks[1:]):
            f = f + _dot(act, wd_ref[cs, :])
        r3 = lax.rsqrt(jnp.mean(f * f, axis=-1, keepdims=True) + RMS_EPS)
        n3 = f * r3
        err = x1v + n3 * gq - t_ref[...]
        loss_ref[...] += 0.5 * jnp.sum(jnp.mean(err * err, axis=-1, keepdims=True), axis=0, keepdims=True)
        dy = err * (1.0 / D_MODEL)
        dgq_ref[...] += jnp.sum(dy * n3, axis=0, keepdims=True)
        dn3 = dy * gq
        df = (r3 * (dn3 - n3 * jnp.mean(dn3 * n3, axis=-1, keepdims=True))).astype(bf16)
        df_ref[...] = df
        dacts = [_dot_nt(df, wd_ref[cs, :]) for cs in chunks]
        dgs = []
        for dact, cs in zip(dacts, chunks):
            gate = gate_scr[:, cs]
            sg = _sigmoid(gate)
            dup = (dact * gate * sg).astype(bf16)
            dgate = (dact * up_scr[:, cs] * (sg * (1.0 + gate * (1.0 - sg)))).astype(bf16)
            dup_ref[:, cs] = dup
            dgate_ref[:, cs] = dgate
            dgs.append((dgate, dup))
        dh2 = None
        for (dgate, dup), cs in zip(dgs, chunks):
            t = _dot(dgate, wg_ref[cs, :]) + _dot(dup, wu_ref[cs, :])
            dh2 = t if dh2 is None else dh2 + t
        dgp_ref[...] += jnp.sum(dh2 * n2, axis=0, keepdims=True)
        dn2 = dh2 * gp
        dx1_ref[...] = dy + r2 * (dn2 - n2 * jnp.mean(dn2 * n2, axis=-1, keepdims=True))

    row = lambda w: pl.BlockSpec((FFN_TS, w), lambda i: (i, 0))
    vec = _const_spec((1, D_MODEL))
    return pl.pallas_call(
        body, name="ffn_fwd_bwd", grid=(S // FFN_TS,),
        in_specs=[row(D_MODEL), row(D_MODEL), vec, vec, _resident_spec((D_FF, D_MODEL)), _resident_spec((D_FF, D_MODEL)),
                  _resident_spec((D_FF, D_MODEL))],
        out_specs=[row(D_MODEL), row(D_MODEL), row(D_FF), row(D_FF), row(D_FF), row(D_MODEL), _const_spec((1, LANE)), vec, vec],
        out_shape=[SDS((S, D_MODEL), f32), SDS((S, D_MODEL), bf16), SDS((S, D_FF), bf16), SDS((S, D_FF), bf16),
                   SDS((S, D_FF), bf16), SDS((S, D_MODEL), bf16), SDS((1, LANE), f32), SDS((1, D_MODEL), f32),
                   SDS((1, D_MODEL), f32)],
        scratch_shapes=[pltpu.VMEM((FFN_TS, D_FF), f32), pltpu.VMEM((FFN_TS, D_FF), f32)],
        compiler_params=_cp(1, VMEM_LIMIT),
    )(x1, tgt, g_pre, g_post, wg, wu, wd)


def _proj_bwd(x, dx1, g_pre, wcat, segs, after, S):
    TS = 512
    n = len(segs)
    cols = [(c0, a.shape[1]) for a, c0 in segs]

    def body(*refs):
        x_ref, dx1_ref, g_ref, w_ref = refs[:4]
        seg_refs = refs[4:4 + n]
        gx_ref, dg_ref = refs[5 + n:]

        @pl.when(pl.program_id(0) == 0)
        def _():
            dg_ref[...] = jnp.zeros_like(dg_ref)

        dh = jnp.zeros((TS, D_MODEL), f32)
        for s_ref, (c0, w) in zip(seg_refs, cols):
            dh = dh + _dot_nt(s_ref[...], w_ref[:, c0:c0 + w])
        xv = x_ref[...]
        g = g_ref[...]
        r = lax.rsqrt(jnp.mean(xv * xv, axis=-1, keepdims=True) + RMS_EPS)
        nx = xv * r
        dg_ref[...] += jnp.sum(dh * nx, axis=0, keepdims=True)
        dn = dh * g
        gx_ref[...] = dx1_ref[...] + r * (dn - nx * jnp.mean(dn * nx, axis=-1, keepdims=True))

    row = lambda w: pl.BlockSpec((TS, w), lambda i: (i, 0))
    return pl.pallas_call(
        body, name="proj_bwd", grid=(S // TS,),
        in_specs=[row(D_MODEL), row(D_MODEL), _const_spec((1, D_MODEL)), _resident_spec((D_MODEL, NCOL))]
                 + [row(w) for _, w in cols] + [_ANY],
        out_specs=[row(D_MODEL), _const_spec((1, D_MODEL))],
        out_shape=[SDS((S, D_MODEL), f32), SDS((1, D_MODEL), f32)],
        compiler_params=_cp(1, VMEM_LIMIT),
    )(x, dx1, g_pre, wcat, *[a for a, _ in segs], after)


def _wgrad(a, b, S, name):
    TS = 2048
    K = a.shape[1]
    N = b.shape[1]
    TN = next(t for t in (512, 1408, N) if N % t == 0)

    def body(a_ref, b_ref, o_ref, acc):
        @pl.when(pl.program_id(1) == 0)
        def _():
            acc[...] = jnp.zeros_like(acc)

        acc[...] += _dot_tn(a_ref[...].astype(bf16), b_ref[...])

        @pl.when(pl.program_id(1) == pl.num_programs(1) - 1)
        def _():
            o_ref[...] = acc[...].astype(bf16)

    return pl.pallas_call(
        body, name=name, grid=(N // TN, S // TS),
        in_specs=[pl.BlockSpec((TS, K), lambda j, s: (s, 0)), pl.BlockSpec((TS, TN), lambda j, s: (s, j))],
        out_specs=pl.BlockSpec((K, TN), lambda j, s: (0, j)), out_shape=SDS((K, N), bf16),
        scratch_shapes=[pltpu.VMEM((K, TN), f32)],
        compiler_params=_cp(2, VMEM_LIMIT),
    )(a, b)


def _wgrad_out(oa, ob, dmix, S):
    TS, TN = 1024, 512

    def body(a_ref, b_ref, d_ref, o_ref, acc):
        @pl.when(pl.program_id(1) == 0)
        def _():
            acc[...] = jnp.zeros_like(acc)

        d = d_ref[...]
        acc[0:GDN_W, :] += _dot_tn(a_ref[...].astype(bf16), d)
        acc[GDN_W:D_MODEL, :] += _dot_tn(b_ref[...].astype(bf16), d)

        @pl.when(pl.program_id(1) == pl.num_programs(1) - 1)
        def _():
            o_ref[...] = acc[...].astype(bf16)

    tok = lambda w: pl.BlockSpec((TS, w), lambda j, s: (s, 0))
    return pl.pallas_call(
        body, name="wgrad_out", grid=(D_MODEL // TN, S // TS),
        in_specs=[tok(GDN_W), tok(SWA_W), pl.BlockSpec((TS, TN), lambda j, s: (s, j))],
        out_specs=pl.BlockSpec((D_MODEL, TN), lambda j, s: (0, j)), out_shape=SDS((D_MODEL, D_MODEL), bf16),
        scratch_shapes=[pltpu.VMEM((D_MODEL, TN), f32)],
        compiler_params=_cp(2, VMEM_LIMIT),
    )(oa, ob, dmix)


def _w_in_pieces():
    n_a, n_g = 4 * GDN_W, 2 * GDN_HEADS
    cb = IN_COLS // N_DEV
    bounds = [(0, n_a, COL_A), (n_a, n_a + n_g, COL_G), (n_a + n_g, IN_COLS, COL_B)]
    out = []
    for j in range(N_DEV):
        lo, hi = j * cb, (j + 1) * cb
        for s0, s1, dst in bounds:
            a, b = max(lo, s0), min(hi, s1)
            if a < b:
                out.append((j, a - lo, b - a, dst + a - s0))
    return out


def _wcat_from_blocks(g_in):
    TR = 256
    cb = IN_COLS // N_DEV
    pieces = _w_in_pieces()

    def body(w_ref, o_ref):
        o_ref[:, COL_G:NCOL] = jnp.zeros((TR, NCOL - COL_G), bf16)
        for j, off, w, dst in pieces:
            o_ref[:, dst:dst + w] = w_ref[j, :, off:off + w]

    return pl.pallas_call(
        body, name="wcat_from_blocks", grid=(D_MODEL // TR,),
        in_specs=[pl.BlockSpec((N_DEV, TR, cb), lambda i: (0, i, 0))],
        out_specs=pl.BlockSpec((TR, NCOL), lambda i: (i, 0)),
        out_shape=SDS((D_MODEL, NCOL), bf16),
        compiler_params=_cp(1, VMEM_LIMIT),
    )(g_in)


def _wgrad_in(h1, segs, S):
    TS = 1024
    n = len(segs)
    cols = [(c0, a.shape[1]) for a, c0 in segs]
    cb = IN_COLS // N_DEV
    pieces = _w_in_pieces()

    def body(*refs):
        h_ref = refs[0]
        seg_refs = refs[1:1 + n]
        o_ref, acc = refs[1 + n], refs[2 + n]

        @pl.when(pl.program_id(0) == 0)
        def _():
            acc[...] = jnp.zeros_like(acc)

        h = h_ref[...]
        for s_ref, (c0, w) in zip(seg_refs, cols):
            acc[:, c0:c0 + w] += _dot_tn(h, s_ref[...])

        @pl.when(pl.program_id(0) == pl.num_programs(0) - 1)
        def _():
            for j, off, w, src in pieces:
                o_ref[j, :, off:off + w] = acc[:, src:src + w].astype(bf16)

    row = lambda w: pl.BlockSpec((TS, w), lambda i: (i, 0))
    return pl.pallas_call(
        body, name="wgrad_in", grid=(S // TS,),
        in_specs=[row(D_MODEL)] + [row(w) for _, w in cols],
        out_specs=_const_spec((N_DEV, D_MODEL, cb)),
        out_shape=SDS((N_DEV, D_MODEL, cb), bf16),
        scratch_shapes=[pltpu.VMEM((D_MODEL, NCOL), f32)],
        compiler_params=_cp(1, VMEM_LIMIT),
    )(h1, *[a for a, _ in segs])


def _adamw(recv, src, me, w, m, v, name):
    R, C = w.shape
    TR = 256 if R % 256 == 0 else R
    c1 = 1.0 / (1.0 - ADAM_B1 ** ADAM_STEP)
    c2 = 1.0 / (1.0 - ADAM_B2 ** ADAM_STEP)

    def body(me_ref, r_ref, own_ref, w_ref, m_ref, v_ref, g_out, d_out, m_out, v_out):
        g = None
        for s in range(N_DEV):
            t = jnp.where(me_ref[0] == s, own_ref[0], r_ref[s]).astype(f32)
            g = t if g is None else g + t
        mn = ADAM_B1 * m_ref[...] + (1.0 - ADAM_B1) * g
        vn = ADAM_B2 * v_ref[...] + (1.0 - ADAM_B2) * (g * g)
        g_out[...] = g
        m_out[...] = mn
        v_out[...] = vn
        d_out[...] = -ADAM_LR * ((mn * c1) / (jnp.sqrt(vn * c2) + ADAM_EPS) + ADAM_WD * w_ref[...])

    blk = pl.BlockSpec((TR, C), lambda i, me_ref: (i, 0))
    return pl.pallas_call(
        body, name=name,
        grid_spec=pltpu.PrefetchScalarGridSpec(
            num_scalar_prefetch=1, grid=(R // TR,),
            in_specs=[pl.BlockSpec((N_DEV, TR, C), lambda i, me_ref: (0, i, 0)),
                      pl.BlockSpec((1, TR, C), lambda i, me_ref: (me_ref[0], i, 0)), blk, blk, blk],
            out_specs=[blk, blk, blk, blk]),
        out_shape=[SDS((R, C), f32)] * 4,
        compiler_params=_cp(1, VMEM_LIMIT),
    )(me, recv, src, w, m, v)


MESH = pl.DeviceIdType.MESH
_ANY = pl.BlockSpec(memory_space=pl.ANY)


def _flip(v, d):
    return 1 - v if d else v


def _all_gather(shards, rel_bias, bk):
    n = len(shards)

    def body(*refs):
        ins = refs[:n]
        rb_ref, bk_ref = refs[n], refs[n + 1]
        outs = refs[n + 2:2 * n + 2]
        bt_ref = refs[2 * n + 2]
        send_sems, recv_sems, local_sems = refs[2 * n + 3:]
        x, y, c = lax.axis_index("x"), lax.axis_index("y"), lax.axis_index("c")
        me, sibling = (x, y, c), (x, y, 1 - c)
        chips = [(1 - x, y), (x, 1 - y), (1 - x, 1 - y)]

        def slot(px, py, pc):
            return 4 * px + 2 * py + pc

        def copy(a, k, block, to, src=None):
            dst = outs[a].at[slot(*block)]
            return pltpu.make_async_remote_copy(src_ref=dst if src is None else src, dst_ref=dst,
                                                send_sem=send_sems.at[a, k], recv_sem=recv_sems.at[a, k],
                                                device_id=to, device_id_type=MESH)

        mine, first, passed = [], [], []
        for a in range(n):
            cp = pltpu.make_async_copy(ins[a], outs[a].at[slot(*me)], local_sems.at[a])
            cp.start()
            mine.append(cp)
            fs = [copy(a, 0, me, sibling, src=ins[a])]
            fs += [copy(a, 1 + j, me, (*chip, c), src=ins[a]) for j, chip in enumerate(chips)]
            for cp in fs:
                cp.start()
            first += fs
        _fill_bias_tables(rb_ref, bk_ref, bt_ref)
        for j, chip in enumerate(chips):
            for a in range(n):
                copy(a, 1 + j, (*chip, c), me).wait_recv()
                cp = copy(a, 4 + j, (*chip, c), sibling)
                cp.start()
                passed.append(cp)
        for a in range(n):
            copy(a, 0, sibling, me).wait_recv()
            for j, chip in enumerate(chips):
                copy(a, 4 + j, (*chip, 1 - c), me).wait_recv()
        for cp in first + passed:
            cp.wait_send()
        for cp in mine:
            cp.wait()

    vmem = pl.BlockSpec(memory_space=pltpu.VMEM)
    return pl.pallas_call(
        body, name="weight_all_gather",
        in_specs=[_ANY] * n + [pl.BlockSpec(memory_space=pltpu.SMEM), vmem], out_specs=[_ANY] * n + [vmem],
        out_shape=[SDS((N_DEV,) + s.shape, s.dtype) for s in shards] + [SDS((len(PATTERNS), SWA_HEADS, SWA_BLK, SWA_BLK), f32)],
        scratch_shapes=[pltpu.SemaphoreType.DMA((n, 7)), pltpu.SemaphoreType.DMA((n, 7)), pltpu.SemaphoreType.DMA((n,))],
        compiler_params=pltpu.CompilerParams(has_side_effects=True),
    )(*shards, rel_bias, bk)


def _grad_exchange(blocked, whole):
    arrs = list(blocked) + list(whole)
    n, nb = len(arrs), len(blocked)
    rel = [(dx, dy, dc) for dx in (0, 1) for dy in (0, 1) for dc in (0, 1) if dx or dy or dc]

    def body(*refs):
        ins = refs[:n]
        outs = refs[n:2 * n]
        send_sems, recv_sems, local_sems = refs[2 * n:]
        x, y, c = lax.axis_index("x"), lax.axis_index("y"), lax.axis_index("c")
        me = 4 * x + 2 * y + c
        sends, locs = [], []
        for a in range(n):
            cp = pltpu.make_async_copy(ins[a].at[me] if a < nb else ins[a], outs[a].at[me], local_sems.at[a])
            cp.start()
            locs.append(cp)
            for k, (dx, dy, dc) in enumerate(rel):
                peer = (_flip(x, dx), _flip(y, dy), _flip(c, dc))
                pidx = 4 * peer[0] + 2 * peer[1] + peer[2]
                cp = pltpu.make_async_remote_copy(src_ref=ins[a].at[pidx] if a < nb else ins[a], dst_ref=outs[a].at[me],
                                                  send_sem=send_sems.at[a, k], recv_sem=recv_sems.at[a, k],
                                                  device_id=peer, device_id_type=MESH)
                cp.start()
                sends.append(cp)
        for a in range(n):
            for k, (dx, dy, dc) in enumerate(rel):
                peer = (_flip(x, dx), _flip(y, dy), _flip(c, dc))
                pidx = 4 * peer[0] + 2 * peer[1] + peer[2]
                pltpu.make_async_remote_copy(src_ref=outs[a].at[pidx], dst_ref=outs[a].at[pidx],
                                             send_sem=send_sems.at[a, k], recv_sem=recv_sems.at[a, k],
                                             device_id=peer, device_id_type=MESH).wait_recv()
        for cp in sends:
            cp.wait_send()
        for cp in locs:
            cp.wait()

    shapes = [SDS(a.shape, a.dtype) for a in blocked] + [SDS((N_DEV,) + a.shape, a.dtype) for a in whole]
    return pl.pallas_call(
        body, name="grad_exchange",
        in_specs=[_ANY] * n, out_specs=[_ANY] * n, out_shape=shapes,
        scratch_shapes=[pltpu.SemaphoreType.DMA((n, 7)), pltpu.SemaphoreType.DMA((n, 7)), pltpu.SemaphoreType.DMA((n,))],
        compiler_params=pltpu.CompilerParams(has_side_effects=True),
    )(*arrs)


_HBM = pl.BlockSpec(memory_space=pltpu.HBM)
_SEM = pl.BlockSpec(memory_space=pltpu.SEMAPHORE)
_REL = [(dx, dy, dc) for dx in (0, 1) for dy in (0, 1) for dc in (0, 1) if dx or dy or dc]


N_PEER = len(_REL)
_EFFECT = pltpu.SideEffectType.DATAFLOW_SIDE_EFFECTING


def _peer_copies(srcs, lands, send_sems, recv_sems, blocked, as_receiver):
    x, y, c = lax.axis_index("x"), lax.axis_index("y"), lax.axis_index("c")
    me = 4 * x + 2 * y + c
    cps = []
    for a in range(len(srcs)):
        for k, (dx, dy, dc) in enumerate(_REL):
            peer = (_flip(x, dx), _flip(y, dy), _flip(c, dc))
            pidx = 4 * peer[0] + 2 * peer[1] + peer[2]
            cps.append(pltpu.make_async_remote_copy(
                src_ref=srcs[a].at[pidx] if blocked else srcs[a], dst_ref=lands[a].at[pidx if as_receiver else me],
                send_sem=send_sems[a * N_PEER + k], recv_sem=recv_sems[a * N_PEER + k],
                device_id=peer, device_id_type=MESH))
    return cps


def _exchange_start(srcs, after, blocked, name):
    n = len(srcs)
    ns = n * N_PEER
    lands = [lax.empty(s.shape if blocked else (N_DEV,) + s.shape, s.dtype) for s in srcs]

    def body(*refs):
        ins, lnd = refs[:n], refs[n:2 * n]
        outs = refs[2 * n + 1:]
        for cp in _peer_copies(ins, lnd, outs[:ns], outs[ns:2 * ns], blocked, False):
            cp.start()
        outs[-1][...] = jnp.zeros_like(outs[-1])

    res = pl.pallas_call(
        body, name=name,
        in_specs=[_HBM] * (2 * n) + [_ANY],
        out_specs=[_SEM] * (2 * ns) + [_HBM] * (2 * n) + [pl.BlockSpec(memory_space=pltpu.VMEM)],
        out_shape=[pltpu.SemaphoreType.DMA(())] * (2 * ns) + [pltpu.HBM(s.shape, s.dtype) for s in srcs]
                  + [pltpu.HBM(l.shape, l.dtype) for l in lands] + [SDS((8, LANE), f32)],
        input_output_aliases={i: 2 * ns + i for i in range(2 * n)},
        compiler_params=pltpu.CompilerParams(has_side_effects=_EFFECT),
    )(*[pltpu.with_memory_space_constraint(s, pltpu.HBM) for s in srcs],
      *[pltpu.with_memory_space_constraint(l, pltpu.HBM) for l in lands], after)
    return list(res[:2 * ns]), list(res[2 * ns:2 * ns + n]), list(res[2 * ns + n:2 * ns + 2 * n]), res[-1]


def _exchange_wait(sems, srcs, lands, after, blocked, name):
    n = len(srcs)
    ns = n * N_PEER

    def body(*refs):
        ins, lnd = refs[:n], refs[n:2 * n]
        sem_refs = refs[2 * n:2 * n + 2 * ns]
        for cp in _peer_copies(ins, lnd, sem_refs[:ns], sem_refs[ns:], blocked, True):
            cp.wait_send()
            cp.wait_recv()

    res = pl.pallas_call(
        body, name=name,
        in_specs=[_HBM] * (2 * n) + [_SEM] * (2 * ns) + [_ANY],
        out_specs=[_HBM] * (2 * n),
        out_shape=[pltpu.HBM(s.shape, s.dtype) for s in srcs] + [pltpu.HBM(l.shape, l.dtype) for l in lands],
        input_output_aliases={i: i for i in range(2 * n)},
        compiler_params=pltpu.CompilerParams(has_side_effects=_EFFECT),
    )(*srcs, *lands, *sems, after)
    return list(res[:n]), list(res[n:])


def _local_step(x, tgt, wcat, convw, bt, late_weights, early_grads, last_grads, token, a_log, dt_bias, onorm_g,
                g_mix_pre, g_mix_post, g_ffn_pre, g_ffn_post):
    S = x.shape[0]
    bk_np = _bucket_tables()
    bk = jnp.asarray(bk_np)
    proj, h1 = _proj_fwd(x, g_mix_pre, wcat, token, S)
    nu = S // CHUNK * GDN_HEADS
    qkv_u = _gdn_prep(proj, convw, S).reshape(3, nu, CHUNK, GDN_HD)
    intra, t_inv = _gdn_intra_fwd(qkv_u, proj, a_log, dt_bias, S)
    oa, states = _gdn_scan_fwd(intra, proj, onorm_g, S)
    ob, lse0, lse1 = _swa_fwd(proj, bt, S)
    wout, ffn_weights = late_weights(ob)
    mix, x1 = _mix_fwd(oa, ob, wout, x, g_mix_post, S)
    wgate, wup, wdown = ffn_weights(x1)
    dx1, h2, act, dgate_f, dup_f, df, loss, d_gfpre, d_gfpost = _ffn(x1, tgt, g_ffn_pre, g_ffn_post, wgate, wup, wdown, S)
    rows8 = lambda g: g.reshape(N_DEV, D_FF // N_DEV, D_MODEL)
    g_gate = rows8(_wgrad(dgate_f, h2, S, "wgrad_gate"))
    g_up = rows8(_wgrad(dup_f, h2, S, "wgrad_up"))
    g_down = rows8(_wgrad(act, df, S, "wgrad_down"))
    dmix, d_oab, d_gmpost, nd = _mix_bwd(dx1, mix, g_mix_post, wout, ob, S)
    g_out = _wgrad_out(oa, ob, dmix, S)
    token = early_grads(g_out.reshape(N_DEV, D_MODEL // N_DEV, D_MODEL), g_gate, g_up, g_down)
    dqb, dkb, dvb, dsb = _swa_bwd(proj, bt, nd, lse0, lse1, d_oab, token, S)
    *cots, dgate_a, d_og = _gdn_scan_bwd(intra, states, proj, d_oab, onorm_g, token, S)
    dqkv_u, dpg, d_alog, d_dtb = _gdn_intra_bwd(qkv_u, proj, a_log, dt_bias, t_inv, cots, S)
    dqkv_a, d_conv = _gdn_prep_bwd(proj, convw, dqkv_u.reshape(3, S // CHUNK, GDN_HEADS, CHUNK, GDN_HD), S)
    segs = [(dqkv_a, COL_A), (dgate_a, COL_A + 3 * GDN_W), (dqb, COL_B), (dkb, COL_B + SWA_W), (dvb, COL_B + 2 * SWA_W),
            (dpg, COL_G)]
    token = last_grads(_wgrad_in(h1, segs, S), d_conv)
    grad_x, d_gmpre = _proj_bwd(x, dx1, g_mix_pre, wcat, segs, token, S)
    d_rel = _rel_bias_grad(dsb, bk, bk_np)
    small = dict(a_log=d_alog[:, GDN_HEADS:2 * GDN_HEADS], dt_bias=d_dtb[:, GDN_HEADS:2 * GDN_HEADS], onorm_g=d_og, rel_bias=d_rel,
                 g_mix_pre=d_gmpre, g_mix_post=d_gmpost, g_ffn_pre=d_gfpre, g_ffn_post=d_gfpost)
    return loss, grad_x, small


SMALL = ("a_log", "dt_bias", "onorm_g", "rel_bias", "g_mix_pre", "g_mix_post", "g_ffn_pre", "g_ffn_post")
PACK_ROWS = 8


def _pack_small(d, loss=None):
    rest = jnp.concatenate([d["onorm_g"].reshape(-1), d["a_log"].reshape(-1), d["dt_bias"].reshape(-1),
                            d["rel_bias"].reshape(-1)])
    rest = jnp.concatenate([rest, jnp.zeros((D_MODEL - rest.shape[0],), f32)])
    extra = jnp.zeros((D_MODEL,), f32) if loss is None else jnp.concatenate([loss.reshape(1), jnp.zeros((D_MODEL - 1,), f32)])
    rows = [d["g_mix_pre"].reshape(-1), d["g_mix_post"].reshape(-1), d["g_ffn_pre"].reshape(-1),
            d["g_ffn_post"].reshape(-1), rest, extra]
    return jnp.concatenate([jnp.stack(rows), jnp.zeros((PACK_ROWS - len(rows), D_MODEL), f32)], axis=0)


def _unpack_small(p):
    o = GDN_HD
    return dict(g_mix_pre=p[0:1], g_mix_post=p[1:2], g_ffn_pre=p[2:3], g_ffn_post=p[3:4],
                onorm_g=p[4:5, :o], a_log=p[4:5, o:o + 4], dt_bias=p[4:5, o + 4:o + 8],
                rel_bias=p[4, o + 8:o + 8 + NUM_BUCKETS * SWA_HEADS].reshape(NUM_BUCKETS, SWA_HEADS))


def kernel(x, w_in, conv_w, a_log, dt_bias, onorm_g, rel_bias, w_out, g_mix_pre, g_mix_post, w_gate, w_up, w_down, g_ffn_pre, g_ffn_post, loss_target, m_w_in, m_conv_w, m_a_log, m_dt_bias, m_onorm_g, m_rel_bias, m_w_out, m_g_mix_pre, m_g_mix_post, m_w_gate, m_w_up, m_w_down, m_g_ffn_pre, m_g_ffn_post, v_w_in, v_conv_w, v_a_log, v_dt_bias, v_onorm_g, v_rel_bias, v_w_out, v_g_mix_pre, v_g_mix_post, v_w_gate, v_w_up, v_w_down, v_g_ffn_pre, v_g_ffn_post):
    big = ("w_in", "conv_w", "w_out", "w_gate", "w_up", "w_down")
    transposed = ("w_gate", "w_up")
    tr = lambda k, a: a.T if k in transposed else a
    w_sh = {k: tr(k, a[0]) for k, a in dict(w_in=w_in, conv_w=conv_w, w_out=w_out, w_gate=w_gate, w_up=w_up, w_down=w_down).items()}
    m_sh = {k: tr(k, a[0]) for k, a in dict(w_in=m_w_in, conv_w=m_conv_w, w_out=m_w_out, w_gate=m_w_gate, w_up=m_w_up,
                                             w_down=m_w_down).items()}
    v_sh = {k: tr(k, a[0]) for k, a in dict(w_in=v_w_in, conv_w=v_conv_w, w_out=v_w_out, w_gate=v_w_gate, w_up=v_w_up,
                                             w_down=v_w_down).items()}
    w_small = dict(a_log=a_log, dt_bias=dt_bias, onorm_g=onorm_g, rel_bias=rel_bias, g_mix_pre=g_mix_pre,
                   g_mix_post=g_mix_post, g_ffn_pre=g_ffn_pre, g_ffn_post=g_ffn_post)
    m_small = dict(a_log=m_a_log, dt_bias=m_dt_bias, onorm_g=m_onorm_g, rel_bias=m_rel_bias, g_mix_pre=m_g_mix_pre,
                   g_mix_post=m_g_mix_post, g_ffn_pre=m_g_ffn_pre, g_ffn_post=m_g_ffn_post)
    v_small = dict(a_log=v_a_log, dt_bias=v_dt_bias, onorm_g=v_onorm_g, rel_bias=v_rel_bias, g_mix_pre=v_g_mix_pre,
                   g_mix_post=v_g_mix_post, g_ffn_pre=v_g_ffn_pre, g_ffn_post=v_g_ffn_post)

    me = 4 * lax.axis_index("x") + 2 * lax.axis_index("y") + lax.axis_index("c")
    me1 = me.reshape(1).astype(jnp.int32)
    own = lambda full, part: lax.dynamic_update_index_in_dim(full, part, me, 0)
    cols = lambda g: g.reshape(g.shape[0], N_DEV, g.shape[1] // N_DEV).transpose(1, 0, 2)
    late = ("w_out", "w_gate", "w_up", "w_down")

    late_src = [w_sh[k].astype(bf16) for k in late]
    g_in, g_conv, bt = _all_gather([w_sh["w_in"].astype(bf16), w_sh["conv_w"]], rel_bias, jnp.asarray(_bucket_tables()))
    g_sems, g_src, g_land, g_token = _exchange_start(late_src, g_conv, False, "late_weights_start")
    wcat = _wcat_from_blocks(g_in)
    convw = g_conv.transpose(1, 0, 2).reshape(4, 3 * GDN_W)

    def late_weights(after):
        pick = lambda idx: [g_sems[half * len(late) * N_PEER + a * N_PEER + k] for half in (0, 1) for a in idx for k in range(N_PEER)]
        (s_out,), (l_out,) = _exchange_wait(pick([0]), g_src[:1], g_land[:1], after, False, "w_out_wait")

        def ffn_weights(after2):
            srcs, lands = _exchange_wait(pick([1, 2, 3]), g_src[1:], g_land[1:], after2, False, "ffn_weights_wait")
            return [own(l, s).reshape(D_FF, D_MODEL) for l, s in zip(lands, srcs)]

        return own(l_out, s_out).reshape(D_MODEL, D_MODEL), ffn_weights

    early, last = {}, {}

    def early_grads(*blocks):
        early["sems"], early["src"], early["land"], token = _exchange_start(list(blocks), me1, True, "late_grads_start")
        return token

    def last_grads(gw_in, gw_conv):
        src = [gw_in, cols(gw_conv)]
        last["sems"], last["src"], last["land"], token = _exchange_start(src, me1, True, "last_grads_start")
        return token

    loss_p, grad_x, gsmall = _local_step(
        x[0], loss_target[0], wcat, convw, bt, late_weights, early_grads, last_grads, g_token,
        a_log, dt_bias, onorm_g, g_mix_pre, g_mix_post, g_ffn_pre, g_ffn_post)

    (r_small,) = _grad_exchange([], [_pack_small(gsmall, loss_p[0, 0])])
    outs = {}
    for names, ex, after, name in ((late, early, grad_x, "late_grads_wait"), (("w_in", "conv_w"), last, r_small, "last_grads_wait")):
        srcs, lands = _exchange_wait(ex["sems"], ex["src"], ex["land"], after, True, name)
        for k, l, s in zip(names, lands, srcs):
            outs[k] = _adamw(l, s, me1, w_sh[k], m_sh[k], v_sh[k], "adamw_" + k)
    sm = _adamw(r_small, r_small, me1, _pack_small(w_small), _pack_small(m_small), _pack_small(v_small), "adamw_small")
    loss = sm[0][5, 0]
    sm = [_unpack_small(t) for t in sm]
    for k in SMALL:
        outs[k] = tuple(t[k].reshape(w_small[k].shape) for t in sm)

    order = ("w_in", "conv_w", "a_log", "dt_bias", "onorm_g", "rel_bias", "w_out", "g_mix_pre", "g_mix_post", "w_gate",
             "w_up", "w_down", "g_ffn_pre", "g_ffn_post")
    lead = lambda k, t: tr(k, t)[None] if k in big else t
    res = [loss, grad_x[None]]
    for i in range(4):
        res += [lead(k, outs[k][i]) for k in order]
    return tuple(res)
```

```python
import functools
import math

import numpy as np
import jax
import jax.numpy as jnp
from jax import lax
from jax.experimental import pallas as pl
from jax.experimental.pallas import tpu as pltpu

f32 = jnp.float32
bf16 = jnp.bfloat16
SDS = jax.ShapeDtypeStruct

D_MODEL = 1024
GDN_HEADS = 4
GDN_HD = 128
GDN_W = 512
CHUNK = 64
SWA_HEADS = 8
SWA_HD = 64
SWA_W = 512
D_FF = 2816
IN_COLS = 3592
PATTERNS = ((128, 1), (512, 4), (2048, 16))
SWA_BLK = 128
NUM_BUCKETS = 32
MAX_DISTANCE = 2048
RMS_EPS = 1e-6
NEG = -1e30
N_DEV = 8

COL_A = 0
COL_B = 2048
COL_G = 3584
NCOL = 3712
LANE = 128

ADAM_LR, ADAM_B1, ADAM_B2, ADAM_EPS, ADAM_WD, ADAM_STEP = 0.001, 0.9, 0.999, 1e-08, 0.01, 10

VMEM_LIMIT = 56 * 1024 * 1024

HIGH = lax.Precision.HIGH


def _cp(n_grid=0, vmem=None):
    kw = {}
    if n_grid:
        kw["dimension_semantics"] = ("arbitrary",) * n_grid
    if vmem:
        kw["vmem_limit_bytes"] = vmem
    return pltpu.CompilerParams(**kw)


def _dot(a, b):
    return jnp.dot(a, b, preferred_element_type=f32)


def _dot_nt(a, b):
    return lax.dot_general(a, b, (((1,), (1,)), ((), ())), preferred_element_type=f32)


def _dot_tn(a, b):
    return lax.dot_general(a, b, (((0,), (0,)), ((), ())), preferred_element_type=f32)


def _sigmoid(x):
    return 0.5 * jnp.tanh(0.5 * x) + 0.5


def _softplus(x):
    return jnp.maximum(x, 0.0) + jnp.log(1.0 + jnp.exp(-jnp.abs(x)))


def _const_spec(shape):
    nd = len(shape)
    return pl.BlockSpec(shape, lambda *_: (0,) * nd)


def _resident_spec(shape):
    nd = len(shape)
    return pl.BlockSpec(shape, lambda *_: (0,) * nd, pipeline_mode=pl.Buffered(1))


def _t5_bucket_np(dist):
    max_exact = NUM_BUCKETS // 2
    d = np.maximum(dist, 1).astype(np.float32)
    log_b = max_exact + (np.log(d / np.float32(max_exact)) / np.float32(math.log(MAX_DISTANCE / max_exact))
                         * np.float32(NUM_BUCKETS - max_exact)).astype(np.int32)
    return np.where(dist < max_exact, dist, np.minimum(log_b, NUM_BUCKETS - 1)).astype(np.int32)


def _bucket_tables():
    w = SWA_BLK
    qi = np.arange(w)[:, None]
    kj = np.arange(w)[None, :]
    rel = np.where(kj <= qi, qi - kj, qi + w - kj)
    out = np.zeros((len(PATTERNS), w, w), np.int32)
    for p, (_, dil) in enumerate(PATTERNS):
        steps = _t5_bucket_np(np.arange(w + 1) * dil)
        assert steps[w] == steps[w - 1]
        out[p] = steps[rel]
    return out


def _fill_bias_tables(rb_ref, bk_ref, o_ref):
    for p in range(len(PATTERNS)):
        b_idx = bk_ref[p]
        for h in range(SWA_HEADS):
            def lp(b, acc):
                return jnp.where(b_idx == b, rb_ref[b, h], acc)
            o_ref[p, h] = lax.fori_loop(0, NUM_BUCKETS, lp, jnp.zeros((SWA_BLK, SWA_BLK), f32))


def _rel_bias_grad(dsb, bk, bk_np):
    present = [sorted(set(int(v) for v in np.unique(bk_np[p]))) for p in range(3)]

    def body(ds_ref, bk_ref, o_ref):
        row = lax.broadcasted_iota(jnp.int32, (NUM_BUCKETS, LANE), 0)
        col = lax.broadcasted_iota(jnp.int32, (NUM_BUCKETS, SWA_HEADS), 1)
        out = jnp.zeros((NUM_BUCKETS, SWA_HEADS), f32)
        for hp in range(4):
            for hh in range(2):
                acc = jnp.zeros((NUM_BUCKETS, LANE), f32)
                for p in range(3):
                    tile = ds_ref[hp, p, hh]
                    b_idx = bk_ref[p]
                    for b in present[p]:
                        part = jnp.sum(jnp.where(b_idx == b, tile, 0.0), axis=0, keepdims=True)
                        acc = acc + jnp.where(row == b, part, 0.0)
                tot = jnp.sum(acc, axis=1, keepdims=True)
                out = out + jnp.where(col == 2 * hp + hh, tot, 0.0)
        o_ref[...] = out

    return pl.pallas_call(body, name="rel_bias_grad", out_shape=SDS((NUM_BUCKETS, SWA_HEADS), f32),
                          compiler_params=_cp(0, 32 * 1024 * 1024))(dsb, bk)


def _proj_fwd(x, g_pre, wcat, after, S):
    TS = 512

    def body(x_ref, g_ref, w_ref, after_ref, o_ref, h_ref):
        xv = x_ref[...]
        r = lax.rsqrt(jnp.mean(xv * xv, axis=-1, keepdims=True) + RMS_EPS)
        h = (xv * r * g_ref[...]).astype(bf16)
        h_ref[...] = h
        o_ref[...] = _dot(h, w_ref[...])

    return pl.pallas_call(
        body, name="proj_fwd", grid=(S // TS,),
        in_specs=[pl.BlockSpec((TS, D_MODEL), lambda i: (i, 0)), _const_spec((1, D_MODEL)),
                  _resident_spec((D_MODEL, NCOL)), _ANY],
        out_specs=[pl.BlockSpec((TS, NCOL), lambda i: (i, 0)), pl.BlockSpec((TS, D_MODEL), lambda i: (i, 0))],
        out_shape=[SDS((S, NCOL), f32), SDS((S, D_MODEL), bf16)],
        compiler_params=_cp(1, VMEM_LIMIT),
    )(x, g_pre, wcat, after)


CONV_RT = 256
HALO = 8


CONV_NC = CONV_RT // CHUNK


def _gdn_prep(proj, conv_w, S):
    def body(p_ref, cw_ref, o_ref, xs_ref):
        t = pl.program_id(0)
        xs_ref[pl.ds(0, HALO), :] = jnp.zeros((HALO, LANE), f32)
        xs_ref[pl.ds(HALO, S), :] = p_ref[...]
        w = cw_ref[...]
        scale = jnp.where(t == 0, GDN_HD ** -0.5, 1.0).astype(f32)

        def run(normed):
            def lp(c, carry):
                st = pl.multiple_of(c * CONV_RT, CONV_RT)
                pre = xs_ref[pl.ds(st + HALO - 3, CONV_RT), :] * w[0:1, :]
                for i in range(1, 4):
                    pre = pre + xs_ref[pl.ds(st + HALO - 3 + i, CONV_RT), :] * w[i:i + 1, :]
                out = pre * _sigmoid(pre)
                if normed:
                    out = out * lax.rsqrt(jnp.sum(out * out, axis=-1, keepdims=True) + 1e-6) * scale
                for i in range(CONV_NC):
                    o_ref[0, c * CONV_NC + i, 0] = out[i * CHUNK:(i + 1) * CHUNK]
                return carry

            lax.fori_loop(0, S // CONV_RT, lp, 0)

        pl.when(t < 2)(functools.partial(run, True))
        pl.when(t == 2)(functools.partial(run, False))

    return pl.pallas_call(
        body, name="gdn_prep", grid=(3, GDN_HEADS),
        in_specs=[pl.BlockSpec((S, LANE), lambda t, h: (0, t * GDN_HEADS + h)),
                  pl.BlockSpec((4, LANE), lambda t, h: (0, t * GDN_HEADS + h))],
        out_specs=pl.BlockSpec((1, S // CHUNK, 1, CHUNK, GDN_HD), lambda t, h: (t, 0, h, 0, 0)),
        out_shape=SDS((3, S // CHUNK, GDN_HEADS, CHUNK, GDN_HD), f32),
        scratch_shapes=[pltpu.VMEM((S + HALO, LANE), f32)],
        compiler_params=_cp(2, VMEM_LIMIT),
    )(proj, conv_w)


def _gdn_prep_bwd(proj, conv_w, dqkv, S):
    def body(p_ref, cw_ref, d_ref, dx_ref, dw_ref, xs_ref, dp_ref):
        t = pl.program_id(0)
        xs_ref[pl.ds(0, HALO), :] = jnp.zeros((HALO, LANE), f32)
        xs_ref[pl.ds(HALO, S), :] = p_ref[...]
        dp_ref[pl.ds(S, HALO), :] = jnp.zeros((HALO, LANE), f32)
        w = cw_ref[...]
        scale = jnp.where(t == 0, GDN_HD ** -0.5, 1.0).astype(f32)

        def first_pass(normed):
            def lp1(c, dw):
                st = pl.multiple_of(c * CONV_RT, CONV_RT)
                taps = [xs_ref[pl.ds(st + HALO - 3 + i, CONV_RT), :] for i in range(4)]
                pre = taps[0] * w[0:1, :]
                for i in range(1, 4):
                    pre = pre + taps[i] * w[i:i + 1, :]
                sg = _sigmoid(pre)
                ds = jnp.concatenate([d_ref[0, c * CONV_NC + i, 0] for i in range(CONV_NC)], axis=0)
                if normed:
                    s = pre * sg
                    rn = lax.rsqrt(jnp.sum(s * s, axis=-1, keepdims=True) + 1e-6)
                    n = s * rn
                    dn = ds * scale
                    ds = rn * (dn - n * jnp.sum(dn * n, axis=-1, keepdims=True))
                dpre = ds * (sg * (1.0 + pre * (1.0 - sg)))
                dp_ref[pl.ds(st, CONV_RT), :] = dpre
                return tuple(dw[i] + jnp.sum(dpre * taps[i], axis=0, keepdims=True) for i in range(4))

            z = jnp.zeros((1, LANE), f32)
            dw = lax.fori_loop(0, S // CONV_RT, lp1, (z, z, z, z))
            for i in range(4):
                dw_ref[pl.ds(i, 1), :] = dw[i]

        pl.when(t < 2)(functools.partial(first_pass, True))
        pl.when(t == 2)(functools.partial(first_pass, False))

        def lp2(c, carry):
            st = pl.multiple_of(c * CONV_RT, CONV_RT)
            dx = dp_ref[pl.ds(st, CONV_RT), :] * w[3:4, :]
            for i in range(3):
                dx = dx + dp_ref[pl.ds(st + 3 - i, CONV_RT), :] * w[i:i + 1, :]
            dx_ref[pl.ds(st, CONV_RT), :] = dx.astype(bf16)
            return carry

        lax.fori_loop(0, S // CONV_RT, lp2, 0)

    col = lambda rows: pl.BlockSpec((rows, LANE), lambda t, h: (0, t * GDN_HEADS + h))
    return pl.pallas_call(
        body, name="gdn_prep_bwd", grid=(3, GDN_HEADS),
        in_specs=[col(S), col(4), pl.BlockSpec((1, S // CHUNK, 1, CHUNK, GDN_HD), lambda t, h: (t, 0, h, 0, 0))],
        out_specs=[col(S), col(4)],
        out_shape=[SDS((S, 3 * GDN_W), bf16), SDS((4, 3 * GDN_W), f32)],
        scratch_shapes=[pltpu.VMEM((S + HALO, LANE), f32), pltpu.VMEM((S + HALO, LANE), f32)],
        compiler_params=_cp(2, VMEM_LIMIT),
    )(proj, conv_w, dqkv)


def _bdot(a, b, prec=None):
    return lax.dot_general(a, b, (((2,), (1,)), ((0,), (0,))), precision=prec, preferred_element_type=f32)


def _bdot_nt(a, b, prec=None):
    return lax.dot_general(a, b, (((2,), (2,)), ((0,), (0,))), precision=prec, preferred_element_type=f32)


def _bdot_tn(a, b, prec=None):
    return lax.dot_general(a, b, (((1,), (1,)), ((0,), (0,))), precision=prec, preferred_element_type=f32)


@jax.custom_vjp
def _tri_inv_saved(a, t):
    return t


def _tri_inv_saved_fwd(a, t):
    return t, t


def _tri_inv_saved_bwd(t, dt):
    return -_bdot_tn(t, _bdot_nt(dt, t, HIGH), HIGH), jnp.zeros_like(t)


_tri_inv_saved.defvjp(_tri_inv_saved_fwd, _tri_inv_saved_bwd)


def _gdn_intra(q, k, v, beta, g, t_saved=None):
    nb = q.shape[0]
    c = CHUNK
    ii = lax.broadcasted_iota(jnp.int32, (c, c), 0)
    jj = lax.broadcasted_iota(jnp.int32, (c, c), 1)
    eye = ii == jj
    tril = ii >= jj
    strict = ii > jj
    ones = jnp.ones((nb, c, c), f32)
    eye_f = eye.astype(f32)

    g_row = _bdot(ones, jnp.where(eye, g, 0.0), HIGH)
    gc = jnp.sum(jnp.where(tril, g_row, 0.0), axis=2, keepdims=True)
    gc_row = _bdot(ones, jnp.where(eye, gc, 0.0), HIGH)
    decay = jnp.where(tril, jnp.exp(jnp.where(tril, gc - gc_row, 0.0)), 0.0)
    last = lax.broadcasted_iota(jnp.int32, (c, 1), 0) == c - 1
    gc_last = jnp.sum(jnp.where(last, gc, 0.0), axis=1, keepdims=True)
    e_gc = jnp.exp(gc)

    kb = k * beta
    k16 = k.astype(bf16)
    a = jnp.where(strict, _bdot_nt(kb.astype(bf16), k16) * decay, 0.0)
    if t_saved is None:
        xp = -a
        t_inv = eye_f + xp
        for level in range(5):
            if level < 2:
                xp = _bdot(xp, xp, HIGH)
                t_inv = t_inv + _bdot(t_inv, xp, HIGH)
            else:
                x16 = xp.astype(bf16)
                xp = _bdot(x16, x16)
                t_inv = t_inv + _bdot(t_inv.astype(bf16), xp.astype(bf16))
    else:
        t_inv = _tri_inv_saved(a, t_saved)
    t16 = t_inv.astype(bf16)
    u = _bdot(t16, (v * beta).astype(bf16))
    w = _bdot(t16, (kb * e_gc).astype(bf16))
    attn = jnp.where(tril, _bdot_nt(q.astype(bf16), k16) * decay, 0.0)
    gam = jnp.broadcast_to(jnp.exp(gc_last), (nb, 1, GDN_HD))
    return u, w, attn, q * e_gc, k * jnp.exp(gc_last - gc), gam, t_inv


GDN_TB = 512
GDN_NC = GDN_TB // CHUNK
GDN_NU = GDN_NC * GDN_HEADS


def _gdn_gates(pg_ref, al_ref, db_ref):
    lane1 = lax.broadcasted_iota(jnp.int32, (1, LANE), 1)
    a_lane = jnp.zeros((1, LANE), f32)
    b_lane = jnp.zeros((1, LANE), f32)
    for h in range(GDN_HEADS):
        a_lane = jnp.where(lane1 == GDN_HEADS + h, al_ref[0, h], a_lane)
        b_lane = jnp.where(lane1 == GDN_HEADS + h, db_ref[0, h], b_lane)
    pg = pg_ref[...]
    z = pg + b_lane
    return _sigmoid(pg), -jnp.exp(a_lane) * _softplus(z), z, a_lane


def _gdn_unit_inputs(qkv_ref, beta_all, g_all):
    units = [(cl, h) for cl in range(GDN_NC) for h in range(GDN_HEADS)]
    beta = jnp.stack([beta_all[cl * CHUNK:(cl + 1) * CHUNK, h:h + 1] for cl, h in units])
    g = jnp.stack([g_all[cl * CHUNK:(cl + 1) * CHUNK, GDN_HEADS + h:GDN_HEADS + h + 1] for cl, h in units])
    return qkv_ref[0], qkv_ref[1], qkv_ref[2], beta, g


def _unit_spec(*tail):
    nd = len(tail)
    return pl.BlockSpec((GDN_NU,) + tail, lambda i: (i,) + (0,) * nd)


def _gdn_intra_shapes(S):
    nu = S // CHUNK * GDN_HEADS
    row = SDS((nu, CHUNK, GDN_HD), f32)
    return [row, row, SDS((nu, CHUNK, CHUNK), f32), row, row, SDS((nu, 1, GDN_HD), f32)]


_GDN_INTRA_SPECS = lambda: [_unit_spec(CHUNK, GDN_HD), _unit_spec(CHUNK, GDN_HD), _unit_spec(CHUNK, CHUNK),
                            _unit_spec(CHUNK, GDN_HD), _unit_spec(CHUNK, GDN_HD), _unit_spec(1, GDN_HD)]


def _gdn_intra_fwd(qkv_u, proj, a_log, dt_bias, S):
    def body(qkv_ref, pg_ref, al_ref, db_ref, *outs):
        beta_all, g_all, _, _ = _gdn_gates(pg_ref, al_ref, db_ref)
        res = _gdn_intra(*_gdn_unit_inputs(qkv_ref, beta_all, g_all))
        for o_ref, r in zip(outs, res):
            o_ref[...] = r

    nu = S // CHUNK * GDN_HEADS
    *intra, t_inv = pl.pallas_call(
        body, name="gdn_intra_fwd", grid=(S // GDN_TB,),
        in_specs=[pl.BlockSpec((3, GDN_NU, CHUNK, GDN_HD), lambda i: (0, i, 0, 0)),
                  pl.BlockSpec((GDN_TB, LANE), lambda i: (i, COL_G // LANE)),
                  pl.BlockSpec(memory_space=pltpu.SMEM), pl.BlockSpec(memory_space=pltpu.SMEM)],
        out_specs=_GDN_INTRA_SPECS() + [_unit_spec(CHUNK, CHUNK)],
        out_shape=_gdn_intra_shapes(S) + [SDS((nu, CHUNK, CHUNK), f32)],
        compiler_params=_cp(1, VMEM_LIMIT),
    )(qkv_u, proj, a_log, dt_bias)
    return intra, t_inv


def _gdn_intra_bwd(qkv_u, proj, a_log, dt_bias, t_inv, cots, S):
    def body(qkv_ref, pg_ref, al_ref, db_ref, t_ref, du_ref, dw_ref, da_ref, dqd_ref, dkd_ref, dgm_ref,
             dqkv_ref, dpg_ref, dal_ref, ddb_ref):
        @pl.when(pl.program_id(0) == 0)
        def _():
            dal_ref[...] = jnp.zeros_like(dal_ref)
            ddb_ref[...] = jnp.zeros_like(ddb_ref)

        beta_all, g_all, z, a_lane = _gdn_gates(pg_ref, al_ref, db_ref)
        lane = lax.broadcasted_iota(jnp.int32, (CHUNK, LANE), 1)
        rows = []
        hc = GDN_NC // 2
        for half in range(2):
            us = slice(half * hc * GDN_HEADS, (half + 1) * hc * GDN_HEADS)
            units = [(half * hc + cl, h) for cl in range(hc) for h in range(GDN_HEADS)]
            beta = jnp.stack([beta_all[cl * CHUNK:(cl + 1) * CHUNK, h:h + 1] for cl, h in units])
            g = jnp.stack([g_all[cl * CHUNK:(cl + 1) * CHUNK, GDN_HEADS + h:GDN_HEADS + h + 1] for cl, h in units])
            t_saved = t_ref[us]
            _, vjp = jax.vjp(lambda *a, t_saved=t_saved: _gdn_intra(*a, t_saved=t_saved)[:6],
                             qkv_ref[0, us], qkv_ref[1, us], qkv_ref[2, us], beta, g)
            dq, dk, dv, dbeta, dg = vjp((du_ref[us], dw_ref[us], da_ref[us], dqd_ref[us], dkd_ref[us], dgm_ref[us]))
            dqkv_ref[0, us] = dq
            dqkv_ref[1, us] = dk
            dqkv_ref[2, us] = dv
            for cl in range(hc):
                t = jnp.zeros((CHUNK, LANE), f32)
                for h in range(GDN_HEADS):
                    b = cl * GDN_HEADS + h
                    t = t + jnp.where(lane == h, dbeta[b], 0.0) + jnp.where(lane == GDN_HEADS + h, dg[b], 0.0)
                rows.append(t)
        d_all = jnp.concatenate(rows, axis=0)
        is_beta = lax.broadcasted_iota(jnp.int32, (GDN_TB, LANE), 1) < GDN_HEADS
        dz = d_all * (-jnp.exp(a_lane)) * _sigmoid(z)
        dpg_ref[...] = jnp.where(is_beta, d_all * beta_all * (1.0 - beta_all), dz).astype(bf16)
        dal_ref[...] += jnp.sum(jnp.where(is_beta, 0.0, d_all * g_all), axis=0, keepdims=True)
        ddb_ref[...] += jnp.sum(jnp.where(is_beta, 0.0, dz), axis=0, keepdims=True)

    acc = _const_spec((1, LANE))
    nu = S // CHUNK * GDN_HEADS
    return pl.pallas_call(
        body, name="gdn_intra_bwd", grid=(S // GDN_TB,),
        in_specs=[pl.BlockSpec((3, GDN_NU, CHUNK, GDN_HD), lambda i: (0, i, 0, 0)),
                  pl.BlockSpec((GDN_TB, LANE), lambda i: (i, COL_G // LANE)),
                  pl.BlockSpec(memory_space=pltpu.SMEM), pl.BlockSpec(memory_space=pltpu.SMEM),
                  _unit_spec(CHUNK, CHUNK)] + _GDN_INTRA_SPECS(),
        out_specs=[pl.BlockSpec((3, GDN_NU, CHUNK, GDN_HD), lambda i: (0, i, 0, 0)),
                   pl.BlockSpec((GDN_TB, LANE), lambda i: (i, 0)), acc, acc],
        out_shape=[SDS((3, nu, CHUNK, GDN_HD), f32), SDS((S, LANE), bf16), SDS((1, LANE), f32), SDS((1, LANE), f32)],
        compiler_params=_cp(1, VMEM_LIMIT),
    )(qkv_u, proj, a_log, dt_bias, t_inv, *cots)


def _gdn_scan_fwd(intra, proj, onorm_g, S):
    def body(u_ref, w_ref, at_ref, qd_ref, kd_ref, gm_ref, gate_ref, og_ref, out_ref, st_ref, s_scr):
        @pl.when(pl.program_id(0) == 0)
        def _():
            s_scr[...] = jnp.zeros_like(s_scr)

        og = og_ref[...]
        s = s_scr[...]
        def out_mm(us, s16, vn16):
            return _bdot(qd_ref[us].astype(bf16), s16) + _bdot(at_ref[us].astype(bf16), vn16)

        outs, prev = [], None
        for cl in range(GDN_NC):
            us = slice(cl * GDN_HEADS, (cl + 1) * GDN_HEADS)
            st_ref[us] = s
            s16 = s.astype(bf16)
            ws = _bdot(w_ref[us].astype(bf16), s16)
            if prev is not None:
                outs.append(out_mm(*prev))
            vn16 = (u_ref[us] - ws).astype(bf16)
            prev = (us, s16, vn16)
            s = s * gm_ref[us] + _bdot_tn(kd_ref[us].astype(bf16), vn16)
        outs.append(out_mm(*prev))
        s_scr[...] = s
        for cl, o in enumerate(outs):
            rows = slice(cl * CHUNK, (cl + 1) * CHUNK)
            for h in range(GDN_HEADS):
                oh = o[h]
                gt = gate_ref[rows, h * GDN_HD:(h + 1) * GDN_HD]
                on = oh * lax.rsqrt(jnp.mean(oh * oh, axis=-1, keepdims=True) + RMS_EPS) * og
                out_ref[rows, h * GDN_HD:(h + 1) * GDN_HD] = on * (gt * _sigmoid(gt))

    nu = S // CHUNK * GDN_HEADS
    return pl.pallas_call(
        body, name="gdn_scan_fwd", grid=(S // GDN_TB,),
        in_specs=_GDN_INTRA_SPECS() + [pl.BlockSpec((GDN_TB, GDN_W), lambda i: (i, 3)), _const_spec((1, GDN_HD))],
        out_specs=[pl.BlockSpec((GDN_TB, GDN_W), lambda i: (i, 0)), _unit_spec(GDN_HD, GDN_HD)],
        out_shape=[SDS((S, GDN_W), f32), SDS((nu, GDN_HD, GDN_HD), f32)],
        scratch_shapes=[pltpu.VMEM((GDN_HEADS, GDN_HD, GDN_HD), f32)],
        compiler_params=_cp(1, VMEM_LIMIT),
    )(*intra, proj, onorm_g)


def _gdn_scan_bwd(intra, states, proj, d_oab, onorm_g, after, S):
    n_steps = S // GDN_TB

    def body(u_ref, w_ref, at_ref, qd_ref, kd_ref, gm_ref, st_ref, gate_ref, do_ref, og_ref, after_ref,
             du_ref, dw_ref, dat_ref, dqd_ref, dkd_ref, dgm_ref, dgate_ref, dog_ref, ds_scr):
        @pl.when(pl.program_id(0) == 0)
        def _():
            ds_scr[...] = jnp.zeros_like(ds_scr)
            dog_ref[...] = jnp.zeros_like(dog_ref)

        og = og_ref[...]
        ii = lax.broadcasted_iota(jnp.int32, (CHUNK, CHUNK), 0)
        jj = lax.broadcasted_iota(jnp.int32, (CHUNK, CHUNK), 1)
        tril = ii >= jj
        dog = jnp.zeros((1, GDN_HD), f32)
        pre = []
        for cl in range(GDN_NC):
            us = slice(cl * GDN_HEADS, (cl + 1) * GDN_HEADS)
            rows = slice(cl * CHUNK, (cl + 1) * CHUNK)
            s016 = st_ref[us].astype(bf16)
            w16 = w_ref[us].astype(bf16)
            qd16 = qd_ref[us].astype(bf16)
            at16 = at_ref[us].astype(bf16)
            vn16 = (u_ref[us] - _bdot(w16, s016)).astype(bf16)
            o = _bdot(qd16, s016) + _bdot(at16, vn16)
            do_h = []
            for h in range(GDN_HEADS):
                oh = o[h]
                lanes = slice(h * GDN_HD, (h + 1) * GDN_HD)
                gt = gate_ref[rows, lanes]
                d_out = do_ref[rows, lanes]
                r = lax.rsqrt(jnp.mean(oh * oh, axis=-1, keepdims=True) + RMS_EPS)
                n = oh * r
                sg = _sigmoid(gt)
                silu = gt * sg
                dog = dog + jnp.sum(d_out * n * silu, axis=0, keepdims=True)
                dgate_ref[rows, lanes] = (d_out * n * og * (sg * (1.0 + gt * (1.0 - sg)))).astype(bf16)
                dn = d_out * og * silu
                do_h.append(r * (dn - n * jnp.mean(dn * n, axis=-1, keepdims=True)))
            do16 = jnp.stack(do_h).astype(bf16)
            pre.append((us, s016, w16, vn16, do16, _bdot_tn(at16, do16), _bdot_tn(qd16, do16)))
        ds = ds_scr[...]
        chain = [None] * GDN_NC
        for cl in reversed(range(GDN_NC)):
            us, s016, w16, vn16, do16, at_do, qd_do = pre[cl]
            ds16 = ds.astype(bf16)
            dvn = at_do + _bdot(kd_ref[us].astype(bf16), ds16)
            dvn16 = dvn.astype(bf16)
            chain[cl] = (ds, ds16, dvn, dvn16)
            ds = qd_do + ds * gm_ref[us] - _bdot_tn(w16, dvn16)
        ds_scr[...] = ds
        for cl in range(GDN_NC):
            us, s016, w16, vn16, do16, _, _ = pre[cl]
            ds_in, ds16, dvn, dvn16 = chain[cl]
            du_ref[us] = dvn
            dw_ref[us] = -_bdot_nt(dvn16, s016)
            dat_ref[us] = jnp.where(tril, _bdot_nt(do16, vn16), 0.0)
            dqd_ref[us] = _bdot_nt(do16, s016)
            dkd_ref[us] = _bdot_nt(vn16, ds16)
            dgm_ref[us] = jnp.sum(st_ref[us] * ds_in, axis=1, keepdims=True)
        dog_ref[...] += dog

    def unit(*tail):
        nd = len(tail)
        return pl.BlockSpec((GDN_NU,) + tail, lambda i: (n_steps - 1 - i,) + (0,) * nd)

    intra_specs = [unit(CHUNK, GDN_HD), unit(CHUNK, GDN_HD), unit(CHUNK, CHUNK), unit(CHUNK, GDN_HD),
                   unit(CHUNK, GDN_HD), unit(1, GDN_HD)]
    tok = lambda c: pl.BlockSpec((GDN_TB, GDN_W), lambda i: (n_steps - 1 - i, c))
    return pl.pallas_call(
        body, name="gdn_scan_bwd", grid=(n_steps,),
        in_specs=intra_specs + [unit(GDN_HD, GDN_HD), tok(3), tok(0), _const_spec((1, GDN_HD)), _ANY],
        out_specs=intra_specs + [tok(0), _const_spec((1, GDN_HD))],
        out_shape=_gdn_intra_shapes(S) + [SDS((S, GDN_W), bf16), SDS((1, GDN_HD), f32)],
        scratch_shapes=[pltpu.VMEM((GDN_HEADS, GDN_HD, GDN_HD), f32)],
        compiler_params=_cp(1, VMEM_LIMIT),
    )(*intra, states, proj, d_oab, onorm_g, after)


SWA_UNROLL = 8


def _swa_tiles(it, d, nb_log2):
    nb = 1 << nb_log2
    r = lax.shift_right_logical(it, nb_log2)
    blk = lax.bitwise_and(it, nb - 1)
    qs = blk * (SWA_BLK * d) + r
    ps = jnp.maximum(blk - 1, 0) * (SWA_BLK * d) + r
    if d > 1:
        rows_c, rows_p = pl.ds(qs, SWA_BLK, stride=d), pl.ds(ps, SWA_BLK, stride=d)
    else:
        rows_c, rows_p = pl.ds(pl.multiple_of(qs, SWA_BLK), SWA_BLK), pl.ds(pl.multiple_of(ps, SWA_BLK), SWA_BLK)
    return rows_c, rows_p, blk > 0


def _swa_prev_modes(nb):
    if nb >= SWA_UNROLL:
        return ["load"] + ["reuse"] * (SWA_UNROLL - 1)
    return ["none" if u % nb == 0 else "reuse" for u in range(SWA_UNROLL)]


def _swa_fwd(proj, bt, S):
    scale = SWA_HD ** -0.5

    def body(q_ref, k_ref, v_ref, bt_ref, o_ref, lse0_ref, lse1_ref, m0_scr, m1_scr, a0_scr, a1_scr):
        lane = lax.broadcasted_iota(jnp.int32, (SWA_BLK, LANE), 1)
        h0 = lane < SWA_HD
        qi = lax.broadcasted_iota(jnp.int32, (SWA_BLK, SWA_BLK), 0)
        kj = lax.broadcasted_iota(jnp.int32, (SWA_BLK, SWA_BLK), 1)
        lower = kj <= qi
        ones16 = jnp.ones((LANE, SWA_BLK), bf16)
        m_scrs = (m0_scr, m1_scr)
        a_scrs = (a0_scr, a1_scr)
        for p, (_, d) in reversed(list(enumerate(PATTERNS))):
            nb_log2 = int(math.log2(S // d // SWA_BLK))
            first = p == len(PATTERNS) - 1

            def lp(i, carry, p=p, d=d, nb_log2=nb_log2, first=first):
                heads = [h0, jnp.logical_not(h0)]
                modes = _swa_prev_modes(1 << nb_log2)
                tiles = []
                kc_f = None
                for u in range(SWA_UNROLL):
                    rows_c, rows_p, has_prev = _swa_tiles(i * SWA_UNROLL + u, d, nb_log2)
                    kp_f = {"load": lambda: k_ref[rows_p, :], "reuse": lambda: kc_f, "none": lambda: None}[modes[u]]()
                    has_prev = {"load": has_prev, "reuse": True, "none": False}[modes[u]]
                    q = q_ref[rows_c, :]
                    kc_f = k_ref[rows_c, :]
                    kc = kc_f.astype(bf16)
                    logits = []
                    for mh in heads:
                        q_h = jnp.where(mh, q, 0.0)
                        qh = q_h.astype(bf16)
                        if kp_f is None:
                            logits.append((_dot_nt(qh, kc), None, None))
                        else:
                            logits.append((_dot_nt(qh, kc), _dot_nt(qh, kp_f.astype(bf16)), _dot((q_h * kp_f).astype(bf16), ones16)))
                    tiles.append((rows_c, rows_p, has_prev, logits))
                probs = []
                for rows_c, rows_p, has_prev, logits in tiles:
                    per_head = []
                    for h, (s_c, s_p, far) in enumerate(logits):
                        if has_prev is False:
                            s = jnp.where(lower, s_c * scale + bt_ref[p, h], NEG)
                            s_far = None
                        else:
                            s = jnp.where(lower, s_c, s_p) * scale + bt_ref[p, h]
                            s_far = far * scale + bt_ref[p, h, SWA_BLK - 1:SWA_BLK, 0:1]
                            if has_prev is not True:
                                s = jnp.where(jnp.logical_or(lower, has_prev), s, NEG)
                                s_far = jnp.where(has_prev, s_far, NEG)
                        mn = jnp.max(s, axis=1, keepdims=True)
                        if s_far is not None:
                            mn = jnp.maximum(s_far, mn)
                        alpha = None
                        if not first:
                            mo = m_scrs[h][rows_c, :]
                            mn = jnp.maximum(mo, mn)
                            alpha = jnp.exp(mo - mn)
                        mn = jnp.broadcast_to(mn, (SWA_BLK, LANE))
                        pm = jnp.exp(s - mn)
                        per_head.append((mn, alpha, None if s_far is None else jnp.exp(s_far - mn),
                                         jnp.where(lower, pm, 0.0).astype(bf16),
                                         None if s_far is None else jnp.where(lower, 0.0, pm).astype(bf16)))
                    probs.append(per_head)
                acc_old = [None if first else (a0_scr[t[0], :], a1_scr[t[0], :]) for t in tiles]
                done = []
                vc = None
                for u, ((rows_c, rows_p, _, _), per_head, old) in enumerate(zip(tiles, probs, acc_old)):
                    vp = {"load": lambda: v_ref[rows_p, :], "reuse": lambda: vc, "none": lambda: None}[modes[u]]()
                    vc = v_ref[rows_c, :]
                    acc_new = []
                    for h, (mn, alpha, p_far, pc16, pp16) in enumerate(per_head):
                        pv = _dot(pc16, jnp.where(heads[h], vc, 1.0).astype(bf16))
                        if pp16 is not None:
                            vpa = jnp.where(heads[h], vp, 1.0)
                            pv = pv + _dot(pp16, vpa.astype(bf16)) + p_far * vpa
                        acc_new.append(pv if first else alpha * old[h] + pv)
                    done.append((rows_c, per_head[0][0], per_head[1][0], acc_new[0], acc_new[1]))
                for rows_c, m0_new, m1_new, a0_new, a1_new in done:
                    m0_scr[rows_c, :] = m0_new
                    m1_scr[rows_c, :] = m1_new
                    a0_scr[rows_c, :] = a0_new
                    a1_scr[rows_c, :] = a1_new
                return carry

            lax.fori_loop(0, S // SWA_BLK // SWA_UNROLL, lp, 0)

        def fin(c, carry):
            rows = pl.ds(pl.multiple_of(c * SWA_BLK, SWA_BLK), SWA_BLK)
            a0 = a0_scr[rows, :]
            a1 = a1_scr[rows, :]
            l0 = jnp.where(h0, pltpu.roll(a0, SWA_HD, 1), a0)
            l1 = jnp.where(h0, a1, pltpu.roll(a1, SWA_HD, 1))
            o_ref[rows, :] = jnp.where(h0, a0 / l0, a1 / l1)
            lse0_ref[rows, :] = m0_scr[rows, :] + jnp.log(l0)
            lse1_ref[rows, :] = m1_scr[rows, :] + jnp.log(l1)
            return carry

        lax.fori_loop(0, S // SWA_BLK, fin, 0)

    qb = COL_B // LANE
    col = lambda c: pl.BlockSpec((S, LANE), lambda hp, c=c: (0, c + hp))
    return pl.pallas_call(
        body, name="swa_fwd", grid=(4,),
        in_specs=[col(qb), col(qb + 4), col(qb + 8), pl.BlockSpec((3, 2, SWA_BLK, SWA_BLK), lambda hp: (0, hp, 0, 0))],
        out_specs=[col(0), col(0), col(0)],
        out_shape=[SDS((S, SWA_W), f32)] * 3,
        scratch_shapes=[pltpu.VMEM((S, LANE), f32)] * 4,
        compiler_params=_cp(1, VMEM_LIMIT),
    )(proj, proj, proj, bt)


def _swa_bwd(proj, bt, nd, lse0, lse1, d_oab, after, S):
    scale = SWA_HD ** -0.5

    def body(q_ref, k_ref, v_ref, bt_ref, nd_scr, lse0_ref, lse1_ref, do_ref, after_ref, dq_ref, dk_ref, dv_ref, dsb_ref,
             dq_scr, dk_scr, dv_scr):
        lane = lax.broadcasted_iota(jnp.int32, (SWA_BLK, LANE), 1)
        h0 = lane < SWA_HD
        qi = lax.broadcasted_iota(jnp.int32, (SWA_BLK, SWA_BLK), 0)
        kj = lax.broadcasted_iota(jnp.int32, (SWA_BLK, SWA_BLK), 1)
        lower = kj <= qi
        eye = kj == qi
        rel127 = jnp.logical_or(kj == qi + 1, jnp.logical_and(qi == SWA_BLK - 1, kj == 0))
        ones16 = jnp.ones((LANE, SWA_BLK), bf16)
        lse_refs = (lse0_ref, lse1_ref)
        dk_scr[...] = jnp.zeros((S, LANE), f32)
        dv_scr[...] = jnp.zeros((S, LANE), f32)
        dsb_ref[...] = jnp.zeros_like(dsb_ref)

        for p, (_, d) in reversed(list(enumerate(PATTERNS))):
            nb_log2 = int(math.log2(S // d // SWA_BLK))
            first = p == len(PATTERNS) - 1

            def lp(i, carry, p=p, d=d, nb_log2=nb_log2, first=first):
                heads = [h0, jnp.logical_not(h0)]
                modes = _swa_prev_modes(1 << nb_log2)
                tiles = []
                kc_f = vc_f = None
                for u in range(SWA_UNROLL):
                    rows_c, rows_p, has_prev = _swa_tiles(i * SWA_UNROLL + u, d, nb_log2)
                    kp_f = {"load": lambda: k_ref[rows_p, :], "reuse": lambda: kc_f, "none": lambda: None}[modes[u]]()
                    vp_f = {"load": lambda: v_ref[rows_p, :], "reuse": lambda: vc_f, "none": lambda: None}[modes[u]]()
                    has_prev = {"load": has_prev, "reuse": True, "none": False}[modes[u]]
                    q = q_ref[rows_c, :]
                    kc_f = k_ref[rows_c, :]
                    vc_f = v_ref[rows_c, :]
                    kc = kc_f.astype(bf16)
                    kp = None if kp_f is None else kp_f.astype(bf16)
                    do = do_ref[rows_c, :]
                    nd = nd_scr[rows_c, :]
                    per_head = []
                    for mh in heads:
                        q_h = jnp.where(mh, q, 0.0)
                        do_a = jnp.where(mh, do, nd)
                        qh = q_h.astype(bf16)
                        doa = do_a.astype(bf16)
                        doh = jnp.where(mh, do, 0.0).astype(bf16)
                        dd_c = _dot_nt(doa, jnp.where(mh, vc_f, 1.0).astype(bf16))
                        if kp_f is None:
                            per_head.append((qh, doh, _dot_nt(qh, kc), None, None, dd_c, None, None))
                        else:
                            vpa = jnp.where(mh, vp_f, 1.0)
                            per_head.append((qh, doh, _dot_nt(qh, kc), _dot_nt(qh, kp), _dot((q_h * kp_f).astype(bf16), ones16),
                                             dd_c, _dot_nt(doa, vpa.astype(bf16)), _dot((do_a * vpa).astype(bf16), ones16)))
                    tiles.append((rows_c, rows_p, has_prev, kc, kp, per_head))
                grads = []
                for rows_c, rows_p, has_prev, kc, kp, per_head in tiles:
                    out = []
                    for h, (qh, doh, s_c, s_p, far, dd_c, dd_p, dd_far) in enumerate(per_head):
                        lse_h = lse_refs[h][rows_c, :]
                        if has_prev is False:
                            pm = jnp.exp(jnp.where(lower, s_c * scale + bt_ref[p, h], NEG) - lse_h)
                            dsm = pm * dd_c
                            out.append((dsm, dsm.astype(bf16), None, pm.astype(bf16), None))
                            continue
                        s = jnp.where(lower, s_c, s_p) * scale + bt_ref[p, h]
                        s_far = far * scale + bt_ref[p, h, SWA_BLK - 1:SWA_BLK, 0:1]
                        if has_prev is not True:
                            s = jnp.where(jnp.logical_or(lower, has_prev), s, NEG)
                            s_far = jnp.where(has_prev, s_far, NEG)
                        pm = jnp.exp(s - lse_h)
                        p_far = jnp.exp(s_far - lse_h)
                        dsm = pm * jnp.where(lower, dd_c, dd_p)
                        ds_far = p_far * dd_far
                        out.append((dsm + jnp.where(rel127, ds_far, 0.0),
                                    jnp.where(lower, dsm, 0.0).astype(bf16),
                                    jnp.where(lower, jnp.where(eye, ds_far, 0.0), dsm).astype(bf16),
                                    jnp.where(lower, pm, 0.0).astype(bf16),
                                    jnp.where(lower, jnp.where(eye, p_far, 0.0), pm).astype(bf16)))
                    grads.append(out)
                done = []
                add = lambda acc, t: t if acc is None else acc + t
                for (rows_c, rows_p, _, kc, kp, per_head), out in zip(tiles, grads):
                    dq_t = dkc_t = dkp_t = dvc_t = dvp_t = None
                    for h, (_, dsc16, dsp16, pc16, pp16) in enumerate(out):
                        qh, doh = per_head[h][0], per_head[h][1]
                        dq_h = _dot(dsc16, kc)
                        dkc_t = add(dkc_t, _dot_tn(dsc16, qh) * scale)
                        dvc_t = add(dvc_t, _dot_tn(pc16, doh))
                        if dsp16 is not None:
                            dq_h = dq_h + _dot(dsp16, kp)
                            dkp_t = add(dkp_t, _dot_tn(dsp16, qh) * scale)
                            dvp_t = add(dvp_t, _dot_tn(pp16, doh))
                        dq_t = add(dq_t, jnp.where(heads[h], dq_h * scale, 0.0))
                    done.append([rows_c, rows_p, dq_t, dkc_t, dkp_t, dvc_t, dvp_t])
                for u in range(1, SWA_UNROLL):
                    if modes[u] == "reuse":
                        done[u - 1][3] = done[u - 1][3] + done[u][4]
                        done[u - 1][5] = done[u - 1][5] + done[u][6]
                for h in range(2):
                    tot = grads[0][h][0]
                    for g in grads[1:]:
                        tot = tot + g[h][0]
                    dsb_ref[0, p, h] += tot
                for u, (rows_c, rows_p, dq_t, dkc_t, dkp_t, dvc_t, dvp_t) in enumerate(done):
                    dq_scr[rows_c, :] = dq_t if first else dq_scr[rows_c, :] + dq_t
                    dk_scr[rows_c, :] = dk_scr[rows_c, :] + dkc_t
                    dv_scr[rows_c, :] = dv_scr[rows_c, :] + dvc_t
                    if modes[u] == "load":
                        dk_scr[rows_p, :] = dk_scr[rows_p, :] + dkp_t
                        dv_scr[rows_p, :] = dv_scr[rows_p, :] + dvp_t
                return carry

            lax.fori_loop(0, S // SWA_BLK // SWA_UNROLL, lp, 0)
        dq_ref[...] = dq_scr[...].astype(bf16)
        dk_ref[...] = dk_scr[...].astype(bf16)
        dv_ref[...] = dv_scr[...].astype(bf16)

    qb = COL_B // LANE
    col = lambda c: pl.BlockSpec((S, LANE), lambda hp, c=c: (0, c + hp))
    return pl.pallas_call(
        body, name="swa_bwd", grid=(4,),
        in_specs=[col(qb), col(qb + 4), col(qb + 8),
                  pl.BlockSpec((3, 2, SWA_BLK, SWA_BLK), lambda hp: (0, hp, 0, 0)),
                  col(0), col(0), col(0), col(4), _ANY],
        out_specs=[col(0), col(0), col(0),
                   pl.BlockSpec((1, 3, 2, SWA_BLK, SWA_BLK), lambda hp: (hp, 0, 0, 0, 0))],
        out_shape=[SDS((S, SWA_W), bf16)] * 3 + [SDS((4, 3, 2, SWA_BLK, SWA_BLK), f32)],
        scratch_shapes=[pltpu.VMEM((S, LANE), f32)] * 3,
        compiler_params=_cp(1, VMEM_LIMIT),
    )(proj, proj, proj, bt, nd, lse0, lse1, d_oab, after)


def _mix_fwd(oa, ob, w_out, x, g_post, S):
    TS = 512

    def body(oa_ref, ob_ref, w_ref, x_ref, g_ref, mix_ref, x1_ref):
        mix = _dot(oa_ref[...].astype(bf16), w_ref[0:GDN_W, :]) + _dot(ob_ref[...].astype(bf16), w_ref[GDN_W:D_MODEL, :])
        r = lax.rsqrt(jnp.mean(mix * mix, axis=-1, keepdims=True) + RMS_EPS)
        mix_ref[...] = mix
        x1_ref[...] = x_ref[...] + mix * r * g_ref[...]

    row = lambda w: pl.BlockSpec((TS, w), lambda i: (i, 0))
    return pl.pallas_call(
        body, name="mix_fwd", grid=(S // TS,),
        in_specs=[row(GDN_W), row(SWA_W), _resident_spec((D_MODEL, D_MODEL)), row(D_MODEL), _const_spec((1, D_MODEL))],
        out_specs=[row(D_MODEL), row(D_MODEL)],
        out_shape=[SDS((S, D_MODEL), f32), SDS((S, D_MODEL), f32)],
        compiler_params=_cp(1, VMEM_LIMIT),
    )(oa, ob, w_out, x, g_post)


def _mix_bwd(dx1, mix, g_post, w_out, ob, S):
    TS = 512

    def body(dx1_ref, mix_ref, g_ref, w_ref, ob_ref, dmix_ref, doab_ref, dg_ref, nd_ref):
        @pl.when(pl.program_id(0) == 0)
        def _():
            dg_ref[...] = jnp.zeros_like(dg_ref)

        mix = mix_ref[...]
        dz = dx1_ref[...]
        r = lax.rsqrt(jnp.mean(mix * mix, axis=-1, keepdims=True) + RMS_EPS)
        n = mix * r
        dg_ref[...] += jnp.sum(dz * n, axis=0, keepdims=True)
        dn = dz * g_ref[...]
        dmix = (r * (dn - n * jnp.mean(dn * n, axis=-1, keepdims=True))).astype(bf16)
        dmix_ref[...] = dmix
        doab = _dot_nt(dmix, w_ref[...])
        doab_ref[...] = doab
        hi_ = lax.shift_right_logical(lax.broadcasted_iota(jnp.int32, (SWA_W, SWA_W), 0), 6)
        hj_ = lax.shift_right_logical(lax.broadcasted_iota(jnp.int32, (SWA_W, SWA_W), 1), 6)
        swap = (hi_ == lax.bitwise_xor(hj_, 1)).astype(bf16)
        dlt = doab[:, GDN_W:] * ob_ref[...]
        hi = dlt.astype(bf16)
        nd_ref[...] = (_dot(hi, swap) + _dot((dlt - hi.astype(f32)).astype(bf16), swap)) * (-1.0 / SWA_HD)

    row = lambda w=D_MODEL: pl.BlockSpec((TS, w), lambda i: (i, 0))
    return pl.pallas_call(
        body, name="mix_bwd", grid=(S // TS,),
        in_specs=[row(), row(), _const_spec((1, D_MODEL)), _resident_spec((D_MODEL, D_MODEL)), row(SWA_W)],
        out_specs=[row(), row(), _const_spec((1, D_MODEL)), row(SWA_W)],
        out_shape=[SDS((S, D_MODEL), bf16), SDS((S, D_MODEL), f32), SDS((1, D_MODEL), f32), SDS((S, SWA_W), f32)],
        compiler_params=_cp(1, VMEM_LIMIT),
    )(dx1, mix, g_post, w_out, ob)


FFN_TS = 256
FFN_CH = 1408


def _ffn(x1, tgt, g_pre, g_post, wg, wu, wd, S):
    def body(x1_ref, t_ref, gp_ref, gq_ref, wg_ref, wu_ref, wd_ref,
             dx1_ref, h2_ref, act_ref, dgate_ref, dup_ref, df_ref, loss_ref, dgp_ref, dgq_ref, gate_scr, up_scr):
        @pl.when(pl.program_id(0) == 0)
        def _():
            loss_ref[...] = jnp.zeros_like(loss_ref)
            dgp_ref[...] = jnp.zeros_like(dgp_ref)
            dgq_ref[...] = jnp.zeros_like(dgq_ref)

        x1v = x1_ref[...]
        gp = gp_ref[...]
        gq = gq_ref[...]
        r2 = lax.rsqrt(jnp.mean(x1v * x1v, axis=-1, keepdims=True) + RMS_EPS)
        n2 = x1v * r2
        h2 = (n2 * gp).astype(bf16)
        h2_ref[...] = h2
        chunks = [slice(c * FFN_CH, (c + 1) * FFN_CH) for c in range(D_FF // FFN_CH)]
        for cs in chunks:
            gate_scr[:, cs] = _dot_nt(h2, wg_ref[cs, :])
            up_scr[:, cs] = _dot_nt(h2, wu_ref[cs, :])
        acts = []
        for cs in chunks:
            gate = gate_scr[:, cs]
            act = (gate * _sigmoid(gate) * up_scr[:, cs]).astype(bf16)
            act_ref[:, cs] = act
            acts.append(act)
        f = _dot(acts[0], wd_ref[chunks[0], :])
        for act, cs in zip(acts[1:], chunks[1:]):
            f = f + _dot(act, wd_ref[cs, :])
        r3 = lax.rsqrt(jnp.mean(f * f, axis=-1, keepdims=True) + RMS_EPS)
        n3 = f * r3
        err = x1v + n3 * gq - t_ref[...]
        loss_ref[...] += 0.5 * jnp.sum(jnp.mean(err * err, axis=-1, keepdims=True), axis=0, keepdims=True)
        dy = err * (1.0 / D_MODEL)
        dgq_ref[...] += jnp.sum(dy * n3, axis=0, keepdims=True)
        dn3 = dy * gq
        df = (r3 * (dn3 - n3 * jnp.mean(dn3 * n3, axis=-1, keepdims=True))).astype(bf16)
        df_ref[...] = df
        dacts = [_dot_nt(df, wd_ref[cs, :]) for cs in chunks]
        dgs = []
        for dact, cs in zip(dacts, chunks):
            gate = gate_scr[:, cs]
            sg = _sigmoid(gate)
            dup = (dact * gate * sg).astype(bf16)
            dgate = (dact * up_scr[:, cs] * (sg * (1.0 + gate * (1.0 - sg)))).astype(bf16)
            dup_ref[:, cs] = dup
            dgate_ref[:, cs] = dgate
            dgs.append((dgate, dup))
        dh2 = None
        for (dgate, dup), cs in zip(dgs, chunks):
            t = _dot(dgate, wg_ref[cs, :]) + _dot(dup, wu_ref[cs, :])
            dh2 = t if dh2 is None else dh2 + t
        dgp_ref[...] += jnp.sum(dh2 * n2, axis=0, keepdims=True)
        dn2 = dh2 * gp
        dx1_ref[...] = dy + r2 * (dn2 - n2 * jnp.mean(dn2 * n2, axis=-1, keepdims=True))

    row = lambda w: pl.BlockSpec((FFN_TS, w), lambda i: (i, 0))
    vec = _const_spec((1, D_MODEL))
    return pl.pallas_call(
        body, name="ffn_fwd_bwd", grid=(S // FFN_TS,),
        in_specs=[row(D_MODEL), row(D_MODEL), vec, vec, _resident_spec((D_FF, D_MODEL)), _resident_spec((D_FF, D_MODEL)),
                  _resident_spec((D_FF, D_MODEL))],
        out_specs=[row(D_MODEL), row(D_MODEL), row(D_FF), row(D_FF), row(D_FF), row(D_MODEL), _const_spec((1, LANE)), vec, vec],
        out_shape=[SDS((S, D_MODEL), f32), SDS((S, D_MODEL), bf16), SDS((S, D_FF), bf16), SDS((S, D_FF), bf16),
                   SDS((S, D_FF), bf16), SDS((S, D_MODEL), bf16), SDS((1, LANE), f32), SDS((1, D_MODEL), f32),
                   SDS((1, D_MODEL), f32)],
        scratch_shapes=[pltpu.VMEM((FFN_TS, D_FF), f32), pltpu.VMEM((FFN_TS, D_FF), f32)],
        compiler_params=_cp(1, VMEM_LIMIT),
    )(x1, tgt, g_pre, g_post, wg, wu, wd)


def _proj_bwd(x, dx1, g_pre, wcat, segs, after, S):
    TS = 512
    n = len(segs)
    cols = [(c0, a.shape[1]) for a, c0 in segs]

    def body(*refs):
        x_ref, dx1_ref, g_ref, w_ref = refs[:4]
        seg_refs = refs[4:4 + n]
        gx_ref, dg_ref = refs[5 + n:]

        @pl.when(pl.program_id(0) == 0)
        def _():
            dg_ref[...] = jnp.zeros_like(dg_ref)

        dh = jnp.zeros((TS, D_MODEL), f32)
        for s_ref, (c0, w) in zip(seg_refs, cols):
            dh = dh + _dot_nt(s_ref[...], w_ref[:, c0:c0 + w])
        xv = x_ref[...]
        g = g_ref[...]
        r = lax.rsqrt(jnp.mean(xv * xv, axis=-1, keepdims=True) + RMS_EPS)
        nx = xv * r
        dg_ref[...] += jnp.sum(dh * nx, axis=0, keepdims=True)
        dn = dh * g
        gx_ref[...] = dx1_ref[...] + r * (dn - nx * jnp.mean(dn * nx, axis=-1, keepdims=True))

    row = lambda w: pl.BlockSpec((TS, w), lambda i: (i, 0))
    return pl.pallas_call(
        body, name="proj_bwd", grid=(S // TS,),
        in_specs=[row(D_MODEL), row(D_MODEL), _const_spec((1, D_MODEL)), _resident_spec((D_MODEL, NCOL))]
                 + [row(w) for _, w in cols] + [_ANY],
        out_specs=[row(D_MODEL), _const_spec((1, D_MODEL))],
        out_shape=[SDS((S, D_MODEL), f32), SDS((1, D_MODEL), f32)],
        compiler_params=_cp(1, VMEM_LIMIT),
    )(x, dx1, g_pre, wcat, *[a for a, _ in segs], after)


def _wgrad(a, b, S, name):
    TS = 2048
    K = a.shape[1]
    N = b.shape[1]
    TN = next(t for t in (512, 1408, N) if N % t == 0)

    def body(a_ref, b_ref, o_ref, acc):
        @pl.when(pl.program_id(1) == 0)
        def _():
            acc[...] = jnp.zeros_like(acc)

        acc[...] += _dot_tn(a_ref[...].astype(bf16), b_ref[...])

        @pl.when(pl.program_id(1) == pl.num_programs(1) - 1)
        def _():
            o_ref[...] = acc[...].astype(bf16)

    return pl.pallas_call(
        body, name=name, grid=(N // TN, S // TS),
        in_specs=[pl.BlockSpec((TS, K), lambda j, s: (s, 0)), pl.BlockSpec((TS, TN), lambda j, s: (s, j))],
        out_specs=pl.BlockSpec((K, TN), lambda j, s: (0, j)), out_shape=SDS((K, N), bf16),
        scratch_shapes=[pltpu.VMEM((K, TN), f32)],
        compiler_params=_cp(2, VMEM_LIMIT),
    )(a, b)


def _wgrad_out(oa, ob, dmix, S):
    TS, TN = 1024, 512

    def body(a_ref, b_ref, d_ref, o_ref, acc):
        @pl.when(pl.program_id(1) == 0)
        def _():
            acc[...] = jnp.zeros_like(acc)

        d = d_ref[...]
        acc[0:GDN_W, :] += _dot_tn(a_ref[...].astype(bf16), d)
        acc[GDN_W:D_MODEL, :] += _dot_tn(b_ref[...].astype(bf16), d)

        @pl.when(pl.program_id(1) == pl.num_programs(1) - 1)
        def _():
            o_ref[...] = acc[...].astype(bf16)

    tok = lambda w: pl.BlockSpec((TS, w), lambda j, s: (s, 0))
    return pl.pallas_call(
        body, name="wgrad_out", grid=(D_MODEL // TN, S // TS),
        in_specs=[tok(GDN_W), tok(SWA_W), pl.BlockSpec((TS, TN), lambda j, s: (s, j))],
        out_specs=pl.BlockSpec((D_MODEL, TN), lambda j, s: (0, j)), out_shape=SDS((D_MODEL, D_MODEL), bf16),
        scratch_shapes=[pltpu.VMEM((D_MODEL, TN), f32)],
        compiler_params=_cp(2, VMEM_LIMIT),
    )(oa, ob, dmix)


def _w_in_pieces():
    n_a, n_g = 4 * GDN_W, 2 * GDN_HEADS
    cb = IN_COLS // N_DEV
    bounds = [(0, n_a, COL_A), (n_a, n_a + n_g, COL_G), (n_a + n_g, IN_COLS, COL_B)]
    out = []
    for j in range(N_DEV):
        lo, hi = j * cb, (j + 1) * cb
        for s0, s1, dst in bounds:
            a, b = max(lo, s0), min(hi, s1)
            if a < b:
                out.append((j, a - lo, b - a, dst + a - s0))
    return out


def _wcat_from_blocks(g_in):
    TR = 256
    cb = IN_COLS // N_DEV
    pieces = _w_in_pieces()

    def body(w_ref, o_ref):
        o_ref[:, COL_G:NCOL] = jnp.zeros((TR, NCOL - COL_G), bf16)
        for j, off, w, dst in pieces:
            o_ref[:, dst:dst + w] = w_ref[j, :, off:off + w]

    return pl.pallas_call(
        body, name="wcat_from_blocks", grid=(D_MODEL // TR,),
        in_specs=[pl.BlockSpec((N_DEV, TR, cb), lambda i: (0, i, 0))],
        out_specs=pl.BlockSpec((TR, NCOL), lambda i: (i, 0)),
        out_shape=SDS((D_MODEL, NCOL), bf16),
        compiler_params=_cp(1, VMEM_LIMIT),
    )(g_in)


def _wgrad_in(h1, segs, S):
    TS = 1024
    n = len(segs)
    cols = [(c0, a.shape[1]) for a, c0 in segs]
    cb = IN_COLS // N_DEV
    pieces = _w_in_pieces()

    def body(*refs):
        h_ref = refs[0]
        seg_refs = refs[1:1 + n]
        o_ref, acc = refs[1 + n], refs[2 + n]

        @pl.when(pl.program_id(0) == 0)
        def _():
            acc[...] = jnp.zeros_like(acc)

        h = h_ref[...]
        for s_ref, (c0, w) in zip(seg_refs, cols):
            acc[:, c0:c0 + w] += _dot_tn(h, s_ref[...])

        @pl.when(pl.program_id(0) == pl.num_programs(0) - 1)
        def _():
            for j, off, w, src in pieces:
                o_ref[j, :, off:off + w] = acc[:, src:src + w].astype(bf16)

    row = lambda w: pl.BlockSpec((TS, w), lambda i: (i, 0))
    return pl.pallas_call(
        body, name="wgrad_in", grid=(S // TS,),
        in_specs=[row(D_MODEL)] + [row(w) for _, w in cols],
        out_specs=_const_spec((N_DEV, D_MODEL, cb)),
        out_shape=SDS((N_DEV, D_MODEL, cb), bf16),
        scratch_shapes=[pltpu.VMEM((D_MODEL, NCOL), f32)],
        compiler_params=_cp(1, VMEM_LIMIT),
    )(h1, *[a for a, _ in segs])


def _adamw(recv, src, me, w, m, v, name):
    R, C = w.shape
    TR = 256 if R % 256 == 0 else R
    c1 = 1.0 / (1.0 - ADAM_B1 ** ADAM_STEP)
    c2 = 1.0 / (1.0 - ADAM_B2 ** ADAM_STEP)

    def body(me_ref, r_ref, own_ref, w_ref, m_ref, v_ref, g_out, d_out, m_out, v_out):
        g = None
        for s in range(N_DEV):
            t = jnp.where(me_ref[0] == s, own_ref[0], r_ref[s]).astype(f32)
            g = t if g is None else g + t
        mn = ADAM_B1 * m_ref[...] + (1.0 - ADAM_B1) * g
        vn = ADAM_B2 * v_ref[...] + (1.0 - ADAM_B2) * (g * g)
        g_out[...] = g
        m_out[...] = mn
        v_out[...] = vn
        d_out[...] = -ADAM_LR * ((mn * c1) / (jnp.sqrt(vn * c2) + ADAM_EPS) + ADAM_WD * w_ref[...])

    blk = pl.BlockSpec((TR, C), lambda i, me_ref: (i, 0))
    return pl.pallas_call(
        body, name=name,
        grid_spec=pltpu.PrefetchScalarGridSpec(
            num_scalar_prefetch=1, grid=(R // TR,),
            in_specs=[pl.BlockSpec((N_DEV, TR, C), lambda i, me_ref: (0, i, 0)),
                      pl.BlockSpec((1, TR, C), lambda i, me_ref: (me_ref[0], i, 0)), blk, blk, blk],
            out_specs=[blk, blk, blk, blk]),
        out_shape=[SDS((R, C), f32)] * 4,
        compiler_params=_cp(1, VMEM_LIMIT),
    )(me, recv, src, w, m, v)


MESH = pl.DeviceIdType.MESH
_ANY = pl.BlockSpec(memory_space=pl.ANY)


def _flip(v, d):
    return 1 - v if d else v


def _all_gather(shards, rel_bias, bk):
    n = len(shards)

    def body(*refs):
        ins = refs[:n]
        rb_ref, bk_ref = refs[n], refs[n + 1]
        outs = refs[n + 2:2 * n + 2]
        bt_ref = refs[2 * n + 2]
        send_sems, recv_sems, local_sems = refs[2 * n + 3:]
        x, y, c = lax.axis_index("x"), lax.axis_index("y"), lax.axis_index("c")
        me, sibling = (x, y, c), (x, y, 1 - c)
        chips = [(1 - x, y), (x, 1 - y), (1 - x, 1 - y)]

        def slot(px, py, pc):
            return 4 * px + 2 * py + pc

        def copy(a, k, block, to, src=None):
            dst = outs[a].at[slot(*block)]
            return pltpu.make_async_remote_copy(src_ref=dst if src is None else src, dst_ref=dst,
                                                send_sem=send_sems.at[a, k], recv_sem=recv_sems.at[a, k],
                                                device_id=to, device_id_type=MESH)

        mine, first, passed = [], [], []
        for a in range(n):
            cp = pltpu.make_async_copy(ins[a], outs[a].at[slot(*me)], local_sems.at[a])
            cp.start()
            mine.append(cp)
            fs = [copy(a, 0, me, sibling, src=ins[a])]
            fs += [copy(a, 1 + j, me, (*chip, c), src=ins[a]) for j, chip in enumerate(chips)]
            for cp in fs:
                cp.start()
            first += fs
        _fill_bias_tables(rb_ref, bk_ref, bt_ref)
        for j, chip in enumerate(chips):
            for a in range(n):
                copy(a, 1 + j, (*chip, c), me).wait_recv()
                cp = copy(a, 4 + j, (*chip, c), sibling)
                cp.start()
                passed.append(cp)
        for a in range(n):
            copy(a, 0, sibling, me).wait_recv()
            for j, chip in enumerate(chips):
                copy(a, 4 + j, (*chip, 1 - c), me).wait_recv()
        for cp in first + passed:
            cp.wait_send()
        for cp in mine:
            cp.wait()

    vmem = pl.BlockSpec(memory_space=pltpu.VMEM)
    return pl.pallas_call(
        body, name="weight_all_gather",
        in_specs=[_ANY] * n + [pl.BlockSpec(memory_space=pltpu.SMEM), vmem], out_specs=[_ANY] * n + [vmem],
        out_shape=[SDS((N_DEV,) + s.shape, s.dtype) for s in shards] + [SDS((len(PATTERNS), SWA_HEADS, SWA_BLK, SWA_BLK), f32)],
        scratch_shapes=[pltpu.SemaphoreType.DMA((n, 7)), pltpu.SemaphoreType.DMA((n, 7)), pltpu.SemaphoreType.DMA((n,))],
        compiler_params=pltpu.CompilerParams(has_side_effects=True),
    )(*shards, rel_bias, bk)


def _grad_exchange(blocked, whole):
    arrs = list(blocked) + list(whole)
    n, nb = len(arrs), len(blocked)
    rel = [(dx, dy, dc) for dx in (0, 1) for dy in (0, 1) for dc in (0, 1) if dx or dy or dc]

    def body(*refs):
        ins = refs[:n]
        outs = refs[n:2 * n]
        send_sems, recv_sems, local_sems = refs[2 * n:]
        x, y, c = lax.axis_index("x"), lax.axis_index("y"), lax.axis_index("c")
        me = 4 * x + 2 * y + c
        sends, locs = [], []
        for a in range(n):
            cp = pltpu.make_async_copy(ins[a].at[me] if a < nb else ins[a], outs[a].at[me], local_sems.at[a])
            cp.start()
            locs.append(cp)
            for k, (dx, dy, dc) in enumerate(rel):
                peer = (_flip(x, dx), _flip(y, dy), _flip(c, dc))
                pidx = 4 * peer[0] + 2 * peer[1] + peer[2]
                cp = pltpu.make_async_remote_copy(src_ref=ins[a].at[pidx] if a < nb else ins[a], dst_ref=outs[a].at[me],
                                                  send_sem=send_sems.at[a, k], recv_sem=recv_sems.at[a, k],
                                                  device_id=peer, device_id_type=MESH)
                cp.start()
                sends.append(cp)
        for a in range(n):
            for k, (dx, dy, dc) in enumerate(rel):
                peer = (_flip(x, dx), _flip(y, dy), _flip(c, dc))
                pidx = 4 * peer[0] + 2 * peer[1] + peer[2]
                pltpu.make_async_remote_copy(src_ref=outs[a].at[pidx], dst_ref=outs[a].at[pidx],
                                             send_sem=send_sems.at[a, k], recv_sem=recv_sems.at[a, k],
                                             device_id=peer, device_id_type=MESH).wait_recv()
        for cp in sends:
            cp.wait_send()
        for cp in locs:
            cp.wait()

    shapes = [SDS(a.shape, a.dtype) for a in blocked] + [SDS((N_DEV,) + a.shape, a.dtype) for a in whole]
    return pl.pallas_call(
        body, name="grad_exchange",
        in_specs=[_ANY] * n, out_specs=[_ANY] * n, out_shape=shapes,
        scratch_shapes=[pltpu.SemaphoreType.DMA((n, 7)), pltpu.SemaphoreType.DMA((n, 7)), pltpu.SemaphoreType.DMA((n,))],
        compiler_params=pltpu.CompilerParams(has_side_effects=True),
    )(*arrs)


_HBM = pl.BlockSpec(memory_space=pltpu.HBM)
_SEM = pl.BlockSpec(memory_space=pltpu.SEMAPHORE)
_REL = [(dx, dy, dc) for dx in (0, 1) for dy in (0, 1) for dc in (0, 1) if dx or dy or dc]


N_PEER = len(_REL)
_EFFECT = pltpu.SideEffectType.DATAFLOW_SIDE_EFFECTING


def _peer_copies(srcs, lands, send_sems, recv_sems, blocked, as_receiver):
    x, y, c = lax.axis_index("x"), lax.axis_index("y"), lax.axis_index("c")
    me = 4 * x + 2 * y + c
    cps = []
    for a in range(len(srcs)):
        for k, (dx, dy, dc) in enumerate(_REL):
            peer = (_flip(x, dx), _flip(y, dy), _flip(c, dc))
            pidx = 4 * peer[0] + 2 * peer[1] + peer[2]
            cps.append(pltpu.make_async_remote_copy(
                src_ref=srcs[a].at[pidx] if blocked else srcs[a], dst_ref=lands[a].at[pidx if as_receiver else me],
                send_sem=send_sems[a * N_PEER + k], recv_sem=recv_sems[a * N_PEER + k],
                device_id=peer, device_id_type=MESH))
    return cps


def _exchange_start(srcs, after, blocked, name):
    n = len(srcs)
    ns = n * N_PEER
    lands = [lax.empty(s.shape if blocked else (N_DEV,) + s.shape, s.dtype) for s in srcs]

    def body(*refs):
        ins, lnd = refs[:n], refs[n:2 * n]
        outs = refs[2 * n + 1:]
        for cp in _peer_copies(ins, lnd, outs[:ns], outs[ns:2 * ns], blocked, False):
            cp.start()
        outs[-1][...] = jnp.zeros_like(outs[-1])

    res = pl.pallas_call(
        body, name=name,
        in_specs=[_HBM] * (2 * n) + [_ANY],
        out_specs=[_SEM] * (2 * ns) + [_HBM] * (2 * n) + [pl.BlockSpec(memory_space=pltpu.VMEM)],
        out_shape=[pltpu.SemaphoreType.DMA(())] * (2 * ns) + [pltpu.HBM(s.shape, s.dtype) for s in srcs]
                  + [pltpu.HBM(l.shape, l.dtype) for l in lands] + [SDS((8, LANE), f32)],
        input_output_aliases={i: 2 * ns + i for i in range(2 * n)},
        compiler_params=pltpu.CompilerParams(has_side_effects=_EFFECT),
    )(*[pltpu.with_memory_space_constraint(s, pltpu.HBM) for s in srcs],
      *[pltpu.with_memory_space_constraint(l, pltpu.HBM) for l in lands], after)
    return list(res[:2 * ns]), list(res[2 * ns:2 * ns + n]), list(res[2 * ns + n:2 * ns + 2 * n]), res[-1]


def _exchange_wait(sems, srcs, lands, after, blocked, name):
    n = len(srcs)
    ns = n * N_PEER

    def body(*refs):
        ins, lnd = refs[:n], refs[n:2 * n]
        sem_refs = refs[2 * n:2 * n + 2 * ns]
        for cp in _peer_copies(ins, lnd, sem_refs[:ns], sem_refs[ns:], blocked, True):
            cp.wait_send()
            cp.wait_recv()

    res = pl.pallas_call(
        body, name=name,
        in_specs=[_HBM] * (2 * n) + [_SEM] * (2 * ns) + [_ANY],
        out_specs=[_HBM] * (2 * n),
        out_shape=[pltpu.HBM(s.shape, s.dtype) for s in srcs] + [pltpu.HBM(l.shape, l.dtype) for l in lands],
        input_output_aliases={i: i for i in range(2 * n)},
        compiler_params=pltpu.CompilerParams(has_side_effects=_EFFECT),
    )(*srcs, *lands, *sems, after)
    return list(res[:n]), list(res[n:])


def _local_step(x, tgt, wcat, convw, bt, late_weights, early_grads, last_grads, token, a_log, dt_bias, onorm_g,
                g_mix_pre, g_mix_post, g_ffn_pre, g_ffn_post):
    S = x.shape[0]
    bk_np = _bucket_tables()
    bk = jnp.asarray(bk_np)
    proj, h1 = _proj_fwd(x, g_mix_pre, wcat, token, S)
    nu = S // CHUNK * GDN_HEADS
    qkv_u = _gdn_prep(proj, convw, S).reshape(3, nu, CHUNK, GDN_HD)
    intra, t_inv = _gdn_intra_fwd(qkv_u, proj, a_log, dt_bias, S)
    oa, states = _gdn_scan_fwd(intra, proj, onorm_g, S)
    ob, lse0, lse1 = _swa_fwd(proj, bt, S)
    wout, ffn_weights = late_weights(ob)
    mix, x1 = _mix_fwd(oa, ob, wout, x, g_mix_post, S)
    wgate, wup, wdown = ffn_weights(x1)
    dx1, h2, act, dgate_f, dup_f, df, loss, d_gfpre, d_gfpost = _ffn(x1, tgt, g_ffn_pre, g_ffn_post, wgate, wup, wdown, S)
    rows8 = lambda g: g.reshape(N_DEV, D_FF // N_DEV, D_MODEL)
    g_gate = rows8(_wgrad(dgate_f, h2, S, "wgrad_gate"))
    g_up = rows8(_wgrad(dup_f, h2, S, "wgrad_up"))
    g_down = rows8(_wgrad(act, df, S, "wgrad_down"))
    dmix, d_oab, d_gmpost, nd = _mix_bwd(dx1, mix, g_mix_post, wout, ob, S)
    g_out = _wgrad_out(oa, ob, dmix, S)
    token = early_grads(g_out.reshape(N_DEV, D_MODEL // N_DEV, D_MODEL), g_gate, g_up, g_down)
    dqb, dkb, dvb, dsb = _swa_bwd(proj, bt, nd, lse0, lse1, d_oab, token, S)
    *cots, dgate_a, d_og = _gdn_scan_bwd(intra, states, proj, d_oab, onorm_g, token, S)
    dqkv_u, dpg, d_alog, d_dtb = _gdn_intra_bwd(qkv_u, proj, a_log, dt_bias, t_inv, cots, S)
    dqkv_a, d_conv = _gdn_prep_bwd(proj, convw, dqkv_u.reshape(3, S // CHUNK, GDN_HEADS, CHUNK, GDN_HD), S)
    segs = [(dqkv_a, COL_A), (dgate_a, COL_A + 3 * GDN_W), (dqb, COL_B), (dkb, COL_B + SWA_W), (dvb, COL_B + 2 * SWA_W),
            (dpg, COL_G)]
    token = last_grads(_wgrad_in(h1, segs, S), d_conv)
    grad_x, d_gmpre = _proj_bwd(x, dx1, g_mix_pre, wcat, segs, token, S)
    d_rel = _rel_bias_grad(dsb, bk, bk_np)
    small = dict(a_log=d_alog[:, GDN_HEADS:2 * GDN_HEADS], dt_bias=d_dtb[:, GDN_HEADS:2 * GDN_HEADS], onorm_g=d_og, rel_bias=d_rel,
                 g_mix_pre=d_gmpre, g_mix_post=d_gmpost, g_ffn_pre=d_gfpre, g_ffn_post=d_gfpost)
    return loss, grad_x, small


SMALL = ("a_log", "dt_bias", "onorm_g", "rel_bias", "g_mix_pre", "g_mix_post", "g_ffn_pre", "g_ffn_post")
PACK_ROWS = 8


def _pack_small(d, loss=None):
    rest = jnp.concatenate([d["onorm_g"].reshape(-1), d["a_log"].reshape(-1), d["dt_bias"].reshape(-1),
                            d["rel_bias"].reshape(-1)])
    rest = jnp.concatenate([rest, jnp.zeros((D_MODEL - rest.shape[0],), f32)])
    extra = jnp.zeros((D_MODEL,), f32) if loss is None else jnp.concatenate([loss.reshape(1), jnp.zeros((D_MODEL - 1,), f32)])
    rows = [d["g_mix_pre"].reshape(-1), d["g_mix_post"].reshape(-1), d["g_ffn_pre"].reshape(-1),
            d["g_ffn_post"].reshape(-1), rest, extra]
    return jnp.concatenate([jnp.stack(rows), jnp.zeros((PACK_ROWS - len(rows), D_MODEL), f32)], axis=0)


def _unpack_small(p):
    o = GDN_HD
    return dict(g_mix_pre=p[0:1], g_mix_post=p[1:2], g_ffn_pre=p[2:3], g_ffn_post=p[3:4],
                onorm_g=p[4:5, :o], a_log=p[4:5, o:o + 4], dt_bias=p[4:5, o + 4:o + 8],
                rel_bias=p[4, o + 8:o + 8 + NUM_BUCKETS * SWA_HEADS].reshape(NUM_BUCKETS, SWA_HEADS))


def kernel(x, w_in, conv_w, a_log, dt_bias, onorm_g, rel_bias, w_out, g_mix_pre, g_mix_post, w_gate, w_up, w_down, g_ffn_pre, g_ffn_post, loss_target, m_w_in, m_conv_w, m_a_log, m_dt_bias, m_onorm_g, m_rel_bias, m_w_out, m_g_mix_pre, m_g_mix_post, m_w_gate, m_w_up, m_w_down, m_g_ffn_pre, m_g_ffn_post, v_w_in, v_conv_w, v_a_log, v_dt_bias, v_onorm_g, v_rel_bias, v_w_out, v_g_mix_pre, v_g_mix_post, v_w_gate, v_w_up, v_w_down, v_g_ffn_pre, v_g_ffn_post):
    big = ("w_in", "conv_w", "w_out", "w_gate", "w_up", "w_down")
    transposed = ("w_gate", "w_up")
    tr = lambda k, a: a.T if k in transposed else a
    w_sh = {k: tr(k, a[0]) for k, a in dict(w_in=w_in, conv_w=conv_w, w_out=w_out, w_gate=w_gate, w_up=w_up, w_down=w_down).items()}
    m_sh = {k: tr(k, a[0]) for k, a in dict(w_in=m_w_in, conv_w=m_conv_w, w_out=m_w_out, w_gate=m_w_gate, w_up=m_w_up,
                                             w_down=m_w_down).items()}
    v_sh = {k: tr(k, a[0]) for k, a in dict(w_in=v_w_in, conv_w=v_conv_w, w_out=v_w_out, w_gate=v_w_gate, w_up=v_w_up,
                                             w_down=v_w_down).items()}
    w_small = dict(a_log=a_log, dt_bias=dt_bias, onorm_g=onorm_g, rel_bias=rel_bias, g_mix_pre=g_mix_pre,
                   g_mix_post=g_mix_post, g_ffn_pre=g_ffn_pre, g_ffn_post=g_ffn_post)
    m_small = dict(a_log=m_a_log, dt_bias=m_dt_bias, onorm_g=m_onorm_g, rel_bias=m_rel_bias, g_mix_pre=m_g_mix_pre,
                   g_mix_post=m_g_mix_post, g_ffn_pre=m_g_ffn_pre, g_ffn_post=m_g_ffn_post)
    v_small = dict(a_log=v_a_log, dt_bias=v_dt_bias, onorm_g=v_onorm_g, rel_bias=v_rel_bias, g_mix_pre=v_g_mix_pre,
                   g_mix_post=v_g_mix_post, g_ffn_pre=v_g_ffn_pre, g_ffn_post=v_g_ffn_post)

    me = 4 * lax.axis_index("x") + 2 * lax.axis_index("y") + lax.axis_index("c")
    me1 = me.reshape(1).astype(jnp.int32)
    own = lambda full, part: lax.dynamic_update_index_in_dim(full, part, me, 0)
    cols = lambda g: g.reshape(g.shape[0], N_DEV, g.shape[1] // N_DEV).transpose(1, 0, 2)
    late = ("w_out", "w_gate", "w_up", "w_down")

    late_src = [w_sh[k].astype(bf16) for k in late]
    g_in, g_conv, bt = _all_gather([w_sh["w_in"].astype(bf16), w_sh["conv_w"]], rel_bias, jnp.asarray(_bucket_tables()))
    g_sems, g_src, g_land, g_token = _exchange_start(late_src, g_conv, False, "late_weights_start")
    wcat = _wcat_from_blocks(g_in)
    convw = g_conv.transpose(1, 0, 2).reshape(4, 3 * GDN_W)

    def late_weights(after):
        pick = lambda idx: [g_sems[half * len(late) * N_PEER + a * N_PEER + k] for half in (0, 1) for a in idx for k in range(N_PEER)]
        (s_out,), (l_out,) = _exchange_wait(pick([0]), g_src[:1], g_land[:1], after, False, "w_out_wait")

        def ffn_weights(after2):
            srcs, lands = _exchange_wait(pick([1, 2, 3]), g_src[1:], g_land[1:], after2, False, "ffn_weights_wait")
            return [own(l, s).reshape(D_FF, D_MODEL) for l, s in zip(lands, srcs)]

        return own(l_out, s_out).reshape(D_MODEL, D_MODEL), ffn_weights

    early, last = {}, {}

    def early_grads(*blocks):
        early["sems"], early["src"], early["land"], token = _exchange_start(list(blocks), me1, True, "late_grads_start")
        return token

    def last_grads(gw_in, gw_conv):
        src = [gw_in, cols(gw_conv)]
        last["sems"], last["src"], last["land"], token = _exchange_start(src, me1, True, "last_grads_start")
        return token

    loss_p, grad_x, gsmall = _local_step(
        x[0], loss_target[0], wcat, convw, bt, late_weights, early_grads, last_grads, g_token,
        a_log, dt_bias, onorm_g, g_mix_pre, g_mix_post, g_ffn_pre, g_ffn_post)

    (r_small,) = _grad_exchange([], [_pack_small(gsmall, loss_p[0, 0])])
    outs = {}
    for names, ex, after, name in ((late, early, grad_x, "late_grads_wait"), (("w_in", "conv_w"), last, r_small, "last_grads_wait")):
        srcs, lands = _exchange_wait(ex["sems"], ex["src"], ex["land"], after, True, name)
        for k, l, s in zip(names, lands, srcs):
            outs[k] = _adamw(l, s, me1, w_sh[k], m_sh[k], v_sh[k], "adamw_" + k)
    sm = _adamw(r_small, r_small, me1, _pack_small(w_small), _pack_small(m_small), _pack_small(v_small), "adamw_small")
    loss = sm[0][5, 0]
    sm = [_unpack_small(t) for t in sm]
    for k in SMALL:
        outs[k] = tuple(t[k].reshape(w_small[k].shape) for t in sm)

    order = ("w_in", "conv_w", "a_log", "dt_bias", "onorm_g", "rel_bias", "w_out", "g_mix_pre", "g_mix_post", "w_gate",
             "w_up", "w_down", "g_ffn_pre", "g_ffn_post")
    lead = lambda k, t: tr(k, t)[None] if k in big else t
    res = [loss, grad_x[None]]
    for i in range(4):
        res += [lead(k, outs[k][i]) for k in order]
    return tuple(res)
```
